```python
import math
import jax, jax.numpy as jnp
from jax import lax
import numpy as np

D_MODEL = 1024
BATCH = 8
SEQ = 8192
DEPTH = 1

HEAD_DIM = 64
N_ATTN_HEADS = 8
N_KV_HEADS = 2
N_GMLP_GROUPS = 8
GMLP_GROUP_DIM = 64
ATTN_WIDTH = N_ATTN_HEADS * HEAD_DIM
KV_WIDTH = N_KV_HEADS * HEAD_DIM
GMLP_WIDTH = N_GMLP_GROUPS * GMLP_GROUP_DIM
MIX_WIDTH = ATTN_WIDTH + GMLP_WIDTH
IN_WIDTH = ATTN_WIDTH + 2 * KV_WIDTH + 2 * GMLP_WIDTH
WINDOW = 128
BLOCK = 128
CHUNK = 128
N_BUCKETS = 32
MAX_DISTANCE = 128
D_FF = -(-8 * D_MODEL // (3 * 256)) * 256
ALPHA = (2 * DEPTH) ** 0.25
BETA = (8 * DEPTH) ** -0.25
LN_EPS = 1e-5
NEG_INF = -1e30

kernel_name = "hymba_gmlp_swa_sink_deepnorm_adaln"


def layer_norm(x, g, b):
    xf = x.astype(jnp.float32)
    mu = jnp.mean(xf, axis=-1, keepdims=True)
    var = jnp.mean(jnp.square(xf - mu), axis=-1, keepdims=True)
    return ((xf - mu) * lax.rsqrt(var + LN_EPS) * g.astype(jnp.float32) + b.astype(jnp.float32)).astype(x.dtype)


def rms_norm(x, g):
    xf = x.astype(jnp.float32)
    ms = jnp.mean(jnp.square(xf), axis=-1, keepdims=True)
    return (xf * lax.rsqrt(ms + LN_EPS) * g.astype(jnp.float32)).astype(x.dtype)


def t5_bucket(dist):
    max_exact = N_BUCKETS // 2
    n = jnp.maximum(dist, 0)
    nf = jnp.maximum(n, max_exact).astype(jnp.float32)
    large = max_exact + (jnp.log(nf / max_exact) / math.log(MAX_DISTANCE / max_exact)
                         * (N_BUCKETS - max_exact)).astype(jnp.int32)
    large = jnp.minimum(large, N_BUCKETS - 1)
    return jnp.where(n < max_exact, n, large)


def sliding_window_attention(q, k, v, sinks, rel_bias):
    B, S, H, Dh = q.shape
    nb = S // BLOCK
    G = H // N_KV_HEADS
    qb = q.reshape(B, nb, BLOCK, N_KV_HEADS, G, Dh)
    kb = k.reshape(B, nb, BLOCK, N_KV_HEADS, Dh)
    vb = v.reshape(B, nb, BLOCK, N_KV_HEADS, Dh)
    kpad = jnp.zeros_like(kb[:, :1])
    vpad = jnp.zeros_like(vb[:, :1])
    kk = jnp.concatenate([jnp.concatenate([kpad, kb[:, :-1]], axis=1), kb], axis=2)
    vv = jnp.concatenate([jnp.concatenate([vpad, vb[:, :-1]], axis=1), vb], axis=2)
    logits = jnp.einsum('bnqkgd,bnskd->bnkgqs', qb, kk,
                        preferred_element_type=jnp.float32) * (Dh ** -0.5)
    qi = jnp.arange(BLOCK)[:, None]
    si = jnp.arange(2 * BLOCK)[None, :]
    dist = qi + BLOCK - si
    in_window = (dist >= 0) & (dist < WINDOW)
    bias = rel_bias.astype(jnp.float32)[t5_bucket(dist)]
    bias = bias.transpose(2, 0, 1).reshape(N_KV_HEADS, G, BLOCK, 2 * BLOCK)
    valid = in_window[None] & ((jnp.arange(nb)[:, None, None] > 0) | (si[None] >= BLOCK))
    logits = jnp.where(valid[None, :, None, None], logits + bias[None, None], NEG_INF)
    sink = sinks.astype(jnp.float32).reshape(N_KV_HEADS, G)[None, None, :, :, None, None]
    m = jnp.maximum(jnp.max(logits, axis=-1, keepdims=True), sink)
    p = jnp.exp(logits - m)
    p = p / (jnp.sum(p, axis=-1, keepdims=True) + jnp.exp(sink - m))
    out = jnp.einsum('bnkgqs,bnskd->bnqkgd', p.astype(v.dtype), vv)
    return out.reshape(B, S, H * Dh)


def chunked_spatial_gating(u, v, ln_g, ln_b, w_s, b_s):
    B, S, _ = u.shape
    nc = S // CHUNK
    G, Dg = N_GMLP_GROUPS, GMLP_GROUP_DIM
    u = jax.nn.gelu(u).reshape(B, nc, CHUNK, G, Dg)
    v = layer_norm(jax.nn.gelu(v).reshape(B, S, G, Dg), ln_g.reshape(G, Dg), ln_b.reshape(G, Dg))
    v = v.reshape(B, nc, CHUNK, G, Dg)
    causal = jnp.tril(jnp.ones((CHUNK, CHUNK), dtype=bool))
    w = jnp.where(causal[None], w_s, jnp.zeros_like(w_s))
    mixed = jnp.einsum('gts,bnsgc->bntgc', w, v) + b_s.T[None, None, :, :, None]
    return (u * mixed).reshape(B, S, GMLP_WIDTH)


def _normal(key, shape, scale):
    return jax.random.normal(key, shape, dtype=jnp.float32) * scale


def _fwd_setup_inputs(seed: int = 0) -> dict:
    key = jax.random.key(seed)
    ks = jax.random.split(key, 24)
    L = DEPTH
    w_in = _normal(ks[5], (L, D_MODEL, IN_WIDTH), D_MODEL ** -0.5)
    v_lo, v_hi = ATTN_WIDTH + KV_WIDTH, ATTN_WIDTH + 2 * KV_WIDTH
    w_in = w_in.at[:, :, v_lo:v_hi].multiply(BETA)
    return {
        "x": _normal(ks[0], (BATCH, SEQ, D_MODEL), 1.0),
        "c": _normal(ks[1], (BATCH, D_MODEL), 1.0),
        "rel_bias": _normal(ks[2], (N_BUCKETS, N_ATTN_HEADS), 0.5),
        "w_ada": _normal(ks[3], (L, D_MODEL, 6 * D_MODEL), 0.5 * D_MODEL ** -0.5),
        "b_ada": _normal(ks[4], (L, 6 * D_MODEL), 0.01),
        "w_in": w_in,
        "b_in": _normal(ks[6], (L, IN_WIDTH), 0.01),
        "attn_sinks": _normal(ks[7], (L, N_ATTN_HEADS), 0.5),
        "gmlp_ln_g": 1.0 + _normal(ks[8], (L, GMLP_WIDTH), 0.01),
        "gmlp_ln_b": _normal(ks[9], (L, GMLP_WIDTH), 0.01),
        "gmlp_w_s": _normal(ks[10], (L, N_GMLP_GROUPS, CHUNK, CHUNK), CHUNK ** -0.5),
        "gmlp_b_s": 1.0 + _normal(ks[11], (L, N_GMLP_GROUPS, CHUNK), 0.01),
        "attn_out_g": 1.0 + _normal(ks[12], (L, ATTN_WIDTH), 0.01),
        "gmlp_out_g": 1.0 + _normal(ks[13], (L, GMLP_WIDTH), 0.01),
        "w_out": _normal(ks[14], (L, MIX_WIDTH, D_MODEL), BETA * MIX_WIDTH ** -0.5),
        "ln1_g": 1.0 + _normal(ks[15], (L, D_MODEL), 0.01),
        "ln1_b": _normal(ks[16], (L, D_MODEL), 0.01),
        "w_gate_up": _normal(ks[17], (L, D_MODEL, 2 * D_FF), D_MODEL ** -0.5),
        "w_down": _normal(ks[18], (L, D_FF, D_MODEL), BETA * D_FF ** -0.5),
        "ln2_g": 1.0 + _normal(ks[19], (L, D_MODEL), 0.01),
        "ln2_b": _normal(ks[20], (L, D_MODEL), 0.01),
    }


def _fwd_reference(x, c, rel_bias, w_ada, b_ada, w_in, b_in, attn_sinks, gmlp_ln_g, gmlp_ln_b,
              gmlp_w_s, gmlp_b_s, attn_out_g, gmlp_out_g, w_out, ln1_g, ln1_b,
              w_gate_up, w_down, ln2_g, ln2_b):
    B, S, _ = x.shape
    splits = [ATTN_WIDTH, ATTN_WIDTH + KV_WIDTH, ATTN_WIDTH + 2 * KV_WIDTH,
              ATTN_WIDTH + 2 * KV_WIDTH + GMLP_WIDTH]
    for layer in range(DEPTH):
        mod = jax.nn.silu(c) @ w_ada[layer] + b_ada[layer]
        sh1, sc1, g1, sh2, sc2, g2 = jnp.split(mod[:, None, :], 6, axis=-1)

        h = x * (1.0 + sc1) + sh1
        proj = h @ w_in[layer] + b_in[layer]
        q, k, v, gu, gv = jnp.split(proj, splits, axis=-1)
        attn = sliding_window_attention(
            q.reshape(B, S, N_ATTN_HEADS, HEAD_DIM),
            k.reshape(B, S, N_KV_HEADS, HEAD_DIM),
            v.reshape(B, S, N_KV_HEADS, HEAD_DIM),
            attn_sinks[layer], rel_bias)
        gm = chunked_spatial_gating(gu, gv, gmlp_ln_g[layer], gmlp_ln_b[layer],
                                    gmlp_w_s[layer], gmlp_b_s[layer])
        mixed = jnp.concatenate([rms_norm(attn, attn_out_g[layer]),
                                 rms_norm(gm, gmlp_out_g[layer])], axis=-1)
        y = mixed @ w_out[layer]
        x = layer_norm(ALPHA * x + g1 * y, ln1_g[layer], ln1_b[layer])

        h = x * (1.0 + sc2) + sh2
        gate, up = jnp.split(h @ w_gate_up[layer], 2, axis=-1)
        y = (jax.nn.silu(gate) * up) @ w_down[layer]
        x = layer_norm(ALPHA * x + g2 * y, ln2_g[layer], ln2_b[layer])
    return x


import jax as _jax
import jax.numpy as _jnp

TWIN_FORMAT = 'train_step'
FWD_PARAMS = ['x', 'c', 'rel_bias', 'w_ada', 'b_ada', 'w_in', 'b_in', 'attn_sinks', 'gmlp_ln_g', 'gmlp_ln_b', 'gmlp_w_s', 'gmlp_b_s', 'attn_out_g', 'gmlp_out_g', 'w_out', 'ln1_g', 'ln1_b', 'w_gate_up', 'w_down', 'ln2_g', 'ln2_b']
TWIN_WEIGHTS = ['rel_bias', 'w_ada', 'b_ada', 'w_in', 'b_in', 'attn_sinks', 'gmlp_ln_g', 'gmlp_ln_b', 'gmlp_w_s', 'gmlp_b_s', 'attn_out_g', 'gmlp_out_g', 'w_out', 'ln1_g', 'ln1_b', 'w_gate_up', 'w_down', 'ln2_g', 'ln2_b']
TWIN_DIFF_INPUT = 'x'
TWIN_INPUTS = ['x', 'c', 'rel_bias', 'w_ada', 'b_ada', 'w_in', 'b_in', 'attn_sinks', 'gmlp_ln_g', 'gmlp_ln_b', 'gmlp_w_s', 'gmlp_b_s', 'attn_out_g', 'gmlp_out_g', 'w_out', 'ln1_g', 'ln1_b', 'w_gate_up', 'w_down', 'ln2_g', 'ln2_b', 'loss_target', 'm_rel_bias', 'm_w_ada', 'm_b_ada', 'm_w_in', 'm_b_in', 'm_attn_sinks', 'm_gmlp_ln_g', 'm_gmlp_ln_b', 'm_gmlp_w_s', 'm_gmlp_b_s', 'm_attn_out_g', 'm_gmlp_out_g', 'm_w_out', 'm_ln1_g', 'm_ln1_b', 'm_w_gate_up', 'm_w_down', 'm_ln2_g', 'm_ln2_b', 'v_rel_bias', 'v_w_ada', 'v_b_ada', 'v_w_in', 'v_b_in', 'v_attn_sinks', 'v_gmlp_ln_g', 'v_gmlp_ln_b', 'v_gmlp_w_s', 'v_gmlp_b_s', 'v_attn_out_g', 'v_gmlp_out_g', 'v_w_out', 'v_ln1_g', 'v_ln1_b', 'v_w_gate_up', 'v_w_down', 'v_ln2_g', 'v_ln2_b']
TWIN_OUTPUTS = ['loss', 'grad_x', 'grad_rel_bias', 'grad_w_ada', 'grad_b_ada', 'grad_w_in', 'grad_b_in', 'grad_attn_sinks', 'grad_gmlp_ln_g', 'grad_gmlp_ln_b', 'grad_gmlp_w_s', 'grad_gmlp_b_s', 'grad_attn_out_g', 'grad_gmlp_out_g', 'grad_w_out', 'grad_ln1_g', 'grad_ln1_b', 'grad_w_gate_up', 'grad_w_down', 'grad_ln2_g', 'grad_ln2_b', 'delta_rel_bias', 'delta_w_ada', 'delta_b_ada', 'delta_w_in', 'delta_b_in', 'delta_attn_sinks', 'delta_gmlp_ln_g', 'delta_gmlp_ln_b', 'delta_gmlp_w_s', 'delta_gmlp_b_s', 'delta_attn_out_g', 'delta_gmlp_out_g', 'delta_w_out', 'delta_ln1_g', 'delta_ln1_b', 'delta_w_gate_up', 'delta_w_down', 'delta_ln2_g', 'delta_ln2_b', 'new_m_rel_bias', 'new_m_w_ada', 'new_m_b_ada', 'new_m_w_in', 'new_m_b_in', 'new_m_attn_sinks', 'new_m_gmlp_ln_g', 'new_m_gmlp_ln_b', 'new_m_gmlp_w_s', 'new_m_gmlp_b_s', 'new_m_attn_out_g', 'new_m_gmlp_out_g', 'new_m_w_out', 'new_m_ln1_g', 'new_m_ln1_b', 'new_m_w_gate_up', 'new_m_w_down', 'new_m_ln2_g', 'new_m_ln2_b', 'new_v_rel_bias', 'new_v_w_ada', 'new_v_b_ada', 'new_v_w_in', 'new_v_b_in', 'new_v_attn_sinks', 'new_v_gmlp_ln_g', 'new_v_gmlp_ln_b', 'new_v_gmlp_w_s', 'new_v_gmlp_b_s', 'new_v_attn_out_g', 'new_v_gmlp_out_g', 'new_v_w_out', 'new_v_ln1_g', 'new_v_ln1_b', 'new_v_w_gate_up', 'new_v_w_down', 'new_v_ln2_g', 'new_v_ln2_b']
TWIN_LEAF_KINDS = {'loss': 'loss', 'grad_x': 'grad_x', 'grad_rel_bias': 'grad_w', 'grad_w_ada': 'grad_w', 'grad_b_ada': 'grad_w', 'grad_w_in': 'grad_w', 'grad_b_in': 'grad_w', 'grad_attn_sinks': 'grad_w', 'grad_gmlp_ln_g': 'grad_w', 'grad_gmlp_ln_b': 'grad_w', 'grad_gmlp_w_s': 'grad_w', 'grad_gmlp_b_s': 'grad_w', 'grad_attn_out_g': 'grad_w', 'grad_gmlp_out_g': 'grad_w', 'grad_w_out': 'grad_w', 'grad_ln1_g': 'grad_w', 'grad_ln1_b': 'grad_w', 'grad_w_gate_up': 'grad_w', 'grad_w_down': 'grad_w', 'grad_ln2_g': 'grad_w', 'grad_ln2_b': 'grad_w', 'delta_rel_bias': 'delta_w', 'delta_w_ada': 'delta_w', 'delta_b_ada': 'delta_w', 'delta_w_in': 'delta_w', 'delta_b_in': 'delta_w', 'delta_attn_sinks': 'delta_w', 'delta_gmlp_ln_g': 'delta_w', 'delta_gmlp_ln_b': 'delta_w', 'delta_gmlp_w_s': 'delta_w', 'delta_gmlp_b_s': 'delta_w', 'delta_attn_out_g': 'delta_w', 'delta_gmlp_out_g': 'delta_w', 'delta_w_out': 'delta_w', 'delta_ln1_g': 'delta_w', 'delta_ln1_b': 'delta_w', 'delta_w_gate_up': 'delta_w', 'delta_w_down': 'delta_w', 'delta_ln2_g': 'delta_w', 'delta_ln2_b': 'delta_w', 'new_m_rel_bias': 'new_m', 'new_m_w_ada': 'new_m', 'new_m_b_ada': 'new_m', 'new_m_w_in': 'new_m', 'new_m_b_in': 'new_m', 'new_m_attn_sinks': 'new_m', 'new_m_gmlp_ln_g': 'new_m', 'new_m_gmlp_ln_b': 'new_m', 'new_m_gmlp_w_s': 'new_m', 'new_m_gmlp_b_s': 'new_m', 'new_m_attn_out_g': 'new_m', 'new_m_gmlp_out_g': 'new_m', 'new_m_w_out': 'new_m', 'new_m_ln1_g': 'new_m', 'new_m_ln1_b': 'new_m', 'new_m_w_gate_up': 'new_m', 'new_m_w_down': 'new_m', 'new_m_ln2_g': 'new_m', 'new_m_ln2_b': 'new_m', 'new_v_rel_bias': 'new_v', 'new_v_w_ada': 'new_v', 'new_v_b_ada': 'new_v', 'new_v_w_in': 'new_v', 'new_v_b_in': 'new_v', 'new_v_attn_sinks': 'new_v', 'new_v_gmlp_ln_g': 'new_v', 'new_v_gmlp_ln_b': 'new_v', 'new_v_gmlp_w_s': 'new_v', 'new_v_gmlp_b_s': 'new_v', 'new_v_attn_out_g': 'new_v', 'new_v_gmlp_out_g': 'new_v', 'new_v_w_out': 'new_v', 'new_v_ln1_g': 'new_v', 'new_v_ln1_b': 'new_v', 'new_v_w_gate_up': 'new_v', 'new_v_w_down': 'new_v', 'new_v_ln2_g': 'new_v', 'new_v_ln2_b': 'new_v'}


def _forward(args):
    return _fwd_reference(*[args[k] for k in FWD_PARAMS])


def _output_shape():
    out = _jax.eval_shape(lambda: _forward(_fwd_setup_inputs(0)))
    return out.shape, out.dtype

N_MICROBATCH = 1
ADAM_LR = 0.001
ADAM_B1 = 0.9
ADAM_B2 = 0.999
ADAM_EPS = 1e-08
ADAM_WD = 0.01
ADAM_STEP = 10
PER_EXAMPLE_BATCH_AXIS = {'x': 0, 'c': 0, 'loss_target': 0}
SHARED_INPUTS = []
_WEIGHT_DTYPES = {'rel_bias': _jnp.float32, 'w_ada': _jnp.float32, 'b_ada': _jnp.float32, 'w_in': _jnp.float32, 'b_in': _jnp.float32, 'attn_sinks': _jnp.float32, 'gmlp_ln_g': _jnp.float32, 'gmlp_ln_b': _jnp.float32, 'gmlp_w_s': _jnp.float32, 'gmlp_b_s': _jnp.float32, 'attn_out_g': _jnp.float32, 'gmlp_out_g': _jnp.float32, 'w_out': _jnp.float32, 'ln1_g': _jnp.float32, 'ln1_b': _jnp.float32, 'w_gate_up': _jnp.float32, 'w_down': _jnp.float32, 'ln2_g': _jnp.float32, 'ln2_b': _jnp.float32}
MOMENT_SCALE = {'rel_bias': 2.612583e-02, 'w_ada': 4.860663e-02, 'b_ada': 7.767962e-02, 'w_in': 4.486758e-02, 'b_in': 9.760925e-02, 'attn_sinks': 8.401406e-03, 'gmlp_ln_g': 2.093924e-02, 'gmlp_ln_b': 2.193684e-02, 'gmlp_w_s': 1.523721e-02, 'gmlp_b_s': 2.206867e-02, 'attn_out_g': 3.991840e-02, 'gmlp_out_g': 4.087948e-02, 'w_out': 6.642496e-02, 'ln1_g': 7.496955e-01, 'ln1_b': 3.769671e-01, 'w_gate_up': 1.719654e-02, 'w_down': 4.755417e-02, 'ln2_g': 6.389433e+01, 'ln2_b': 2.676422e+00}


def _to_microbatches(a, axis):
    t = _jnp.moveaxis(a, axis, 0)
    t = t.reshape((N_MICROBATCH, t.shape[0] // N_MICROBATCH) + t.shape[1:])
    return _jnp.moveaxis(t, 1, axis + 1)


def setup_inputs(seed: int = 0) -> dict:
    inp = _fwd_setup_inputs(seed)
    key = _jax.random.fold_in(_jax.random.key(seed), 7919)
    shape, _ = _output_shape()
    out = dict(inp)
    out["loss_target"] = _jax.random.normal(_jax.random.fold_in(key, 0), shape, _jnp.float32)
    for i, name in enumerate(TWIN_WEIGHTS):
        w = inp[name].astype(_jnp.float32)
        if MOMENT_SCALE is None:
            s = _jnp.sqrt(_jnp.mean(_jnp.square(w)) + 1e-30)
        else:
            s = MOMENT_SCALE[name]
        km, kv = _jax.random.split(_jax.random.fold_in(key, i + 1))
        out[name] = w
        out["m_" + name] = s * _jax.random.normal(km, w.shape, _jnp.float32)
        out["v_" + name] = (s * s) * _jax.random.uniform(kv, w.shape, _jnp.float32, 0.5, 1.5)
    if N_MICROBATCH > 1:
        for name, axis in PER_EXAMPLE_BATCH_AXIS.items():
            out[name] = _to_microbatches(out[name], axis)
    return {'x': out['x'], 'c': out['c'], 'rel_bias': out['rel_bias'], 'w_ada': out['w_ada'], 'b_ada': out['b_ada'], 'w_in': out['w_in'], 'b_in': out['b_in'], 'attn_sinks': out['attn_sinks'], 'gmlp_ln_g': out['gmlp_ln_g'], 'gmlp_ln_b': out['gmlp_ln_b'], 'gmlp_w_s': out['gmlp_w_s'], 'gmlp_b_s': out['gmlp_b_s'], 'attn_out_g': out['attn_out_g'], 'gmlp_out_g': out['gmlp_out_g'], 'w_out': out['w_out'], 'ln1_g': out['ln1_g'], 'ln1_b': out['ln1_b'], 'w_gate_up': out['w_gate_up'], 'w_down': out['w_down'], 'ln2_g': out['ln2_g'], 'ln2_b': out['ln2_b'], 'loss_target': out['loss_target'], 'm_rel_bias': out['m_rel_bias'], 'm_w_ada': out['m_w_ada'], 'm_b_ada': out['m_b_ada'], 'm_w_in': out['m_w_in'], 'm_b_in': out['m_b_in'], 'm_attn_sinks': out['m_attn_sinks'], 'm_gmlp_ln_g': out['m_gmlp_ln_g'], 'm_gmlp_ln_b': out['m_gmlp_ln_b'], 'm_gmlp_w_s': out['m_gmlp_w_s'], 'm_gmlp_b_s': out['m_gmlp_b_s'], 'm_attn_out_g': out['m_attn_out_g'], 'm_gmlp_out_g': out['m_gmlp_out_g'], 'm_w_out': out['m_w_out'], 'm_ln1_g': out['m_ln1_g'], 'm_ln1_b': out['m_ln1_b'], 'm_w_gate_up': out['m_w_gate_up'], 'm_w_down': out['m_w_down'], 'm_ln2_g': out['m_ln2_g'], 'm_ln2_b': out['m_ln2_b'], 'v_rel_bias': out['v_rel_bias'], 'v_w_ada': out['v_w_ada'], 'v_b_ada': out['v_b_ada'], 'v_w_in': out['v_w_in'], 'v_b_in': out['v_b_in'], 'v_attn_sinks': out['v_attn_sinks'], 'v_gmlp_ln_g': out['v_gmlp_ln_g'], 'v_gmlp_ln_b': out['v_gmlp_ln_b'], 'v_gmlp_w_s': out['v_gmlp_w_s'], 'v_gmlp_b_s': out['v_gmlp_b_s'], 'v_attn_out_g': out['v_attn_out_g'], 'v_gmlp_out_g': out['v_gmlp_out_g'], 'v_w_out': out['v_w_out'], 'v_ln1_g': out['v_ln1_g'], 'v_ln1_b': out['v_ln1_b'], 'v_w_gate_up': out['v_w_gate_up'], 'v_w_down': out['v_w_down'], 'v_ln2_g': out['v_ln2_g'], 'v_ln2_b': out['v_ln2_b']}


def _loss(weights, diff, rest, loss_target):
    with _jax.named_scope("forward"):
        args = {**rest, TWIN_DIFF_INPUT: diff, **{k: w.astype(_WEIGHT_DTYPES[k]) for k, w in weights.items()}}
        y = _forward(args)
    with _jax.named_scope("loss_head"):
        err = _jnp.square(y.astype(_jnp.float32) - loss_target)
        return 0.5 * _jnp.sum(_jnp.mean(err, axis=-1)) if err.ndim else 0.5 * err


def _adamw(w, g, m, v):
    m = ADAM_B1 * m + (1.0 - ADAM_B1) * g
    v = ADAM_B2 * v + (1.0 - ADAM_B2) * _jnp.square(g)
    m_hat = m / (1.0 - ADAM_B1 ** ADAM_STEP)
    v_hat = v / (1.0 - ADAM_B2 ** ADAM_STEP)
    delta = -ADAM_LR * (m_hat / (_jnp.sqrt(v_hat) + ADAM_EPS) + ADAM_WD * w)
    return delta, m, v


def reference(x, c, rel_bias, w_ada, b_ada, w_in, b_in, attn_sinks, gmlp_ln_g, gmlp_ln_b, gmlp_w_s, gmlp_b_s, attn_out_g, gmlp_out_g, w_out, ln1_g, ln1_b, w_gate_up, w_down, ln2_g, ln2_b, loss_target, m_rel_bias, m_w_ada, m_b_ada, m_w_in, m_b_in, m_attn_sinks, m_gmlp_ln_g, m_gmlp_ln_b, m_gmlp_w_s, m_gmlp_b_s, m_attn_out_g, m_gmlp_out_g, m_w_out, m_ln1_g, m_ln1_b, m_w_gate_up, m_w_down, m_ln2_g, m_ln2_b, v_rel_bias, v_w_ada, v_b_ada, v_w_in, v_b_in, v_attn_sinks, v_gmlp_ln_g, v_gmlp_ln_b, v_gmlp_w_s, v_gmlp_b_s, v_attn_out_g, v_gmlp_out_g, v_w_out, v_ln1_g, v_ln1_b, v_w_gate_up, v_w_down, v_ln2_g, v_ln2_b):
    given = dict(x=x, c=c, rel_bias=rel_bias, w_ada=w_ada, b_ada=b_ada, w_in=w_in, b_in=b_in, attn_sinks=attn_sinks, gmlp_ln_g=gmlp_ln_g, gmlp_ln_b=gmlp_ln_b, gmlp_w_s=gmlp_w_s, gmlp_b_s=gmlp_b_s, attn_out_g=attn_out_g, gmlp_out_g=gmlp_out_g, w_out=w_out, ln1_g=ln1_g, ln1_b=ln1_b, w_gate_up=w_gate_up, w_down=w_down, ln2_g=ln2_g, ln2_b=ln2_b, loss_target=loss_target, m_rel_bias=m_rel_bias, m_w_ada=m_w_ada, m_b_ada=m_b_ada, m_w_in=m_w_in, m_b_in=m_b_in, m_attn_sinks=m_attn_sinks, m_gmlp_ln_g=m_gmlp_ln_g, m_gmlp_ln_b=m_gmlp_ln_b, m_gmlp_w_s=m_gmlp_w_s, m_gmlp_b_s=m_gmlp_b_s, m_attn_out_g=m_attn_out_g, m_gmlp_out_g=m_gmlp_out_g, m_w_out=m_w_out, m_ln1_g=m_ln1_g, m_ln1_b=m_ln1_b, m_w_gate_up=m_w_gate_up, m_w_down=m_w_down, m_ln2_g=m_ln2_g, m_ln2_b=m_ln2_b, v_rel_bias=v_rel_bias, v_w_ada=v_w_ada, v_b_ada=v_b_ada, v_w_in=v_w_in, v_b_in=v_b_in, v_attn_sinks=v_attn_sinks, v_gmlp_ln_g=v_gmlp_ln_g, v_gmlp_ln_b=v_gmlp_ln_b, v_gmlp_w_s=v_gmlp_w_s, v_gmlp_b_s=v_gmlp_b_s, v_attn_out_g=v_attn_out_g, v_gmlp_out_g=v_gmlp_out_g, v_w_out=v_w_out, v_ln1_g=v_ln1_g, v_ln1_b=v_ln1_b, v_w_gate_up=v_w_gate_up, v_w_down=v_w_down, v_ln2_g=v_ln2_g, v_ln2_b=v_ln2_b)
    weights = {n: given[n] for n in TWIN_WEIGHTS}
    shared = {n: given[n] for n in SHARED_INPUTS}
    per_example = {n: given[n] for n in ['x', 'c']}
    grad_fn = _jax.value_and_grad(_loss, argnums=(0, 1))

    def one_microbatch(ex, loss_target):
        ex = dict(ex)
        diff = ex.pop(TWIN_DIFF_INPUT)
        return grad_fn(weights, diff, {**shared, **ex}, loss_target)

    if N_MICROBATCH == 1:
        loss, (grad_w, grad_x) = one_microbatch(per_example, given["loss_target"])
    else:
        def body(carry, xs):
            loss_sum, grad_sum = carry
            l_k, (gw_k, gx_k) = one_microbatch(xs[0], xs[1])
            with _jax.named_scope("update"):
                return (loss_sum + l_k, _jax.tree.map(_jnp.add, grad_sum, gw_k)), gx_k

        init = (_jnp.zeros((), _jnp.float32), _jax.tree.map(_jnp.zeros_like, weights))
        (loss, grad_w), grad_x = _jax.lax.scan(body, init, (per_example, given["loss_target"]))
    with _jax.named_scope("update"):
        delta_w, new_m, new_v = {}, {}, {}
        for n in TWIN_WEIGHTS:
            delta_w[n], new_m[n], new_v[n] = _adamw(weights[n], grad_w[n], given["m_" + n], given["v_" + n])
    return (loss, grad_x, *[grad_w[n] for n in TWIN_WEIGHTS], *[delta_w[n] for n in TWIN_WEIGHTS],
            *[new_m[n] for n in TWIN_WEIGHTS], *[new_v[n] for n in TWIN_WEIGHTS])
```

```python
import math

import jax
import jax.numpy as jnp
from jax import lax
from jax.experimental import pallas as pl
from jax.experimental.pallas import tpu as pltpu

F32 = jnp.float32
BF16 = jnp.bfloat16
MESH = pl.DeviceIdType.MESH

N_DEV = 8
D_MODEL = 1024
HEAD_DIM = 64
N_HEADS = 8
N_GROUPS = 8
ATTN_W = 512
KV_W = 128
GMLP_W = 512
IN_W = 1792
BLK = 128
N_BUCKETS = 32
MAX_DISTANCE = 128
D_FF = 2816
ALPHA = 2.0 ** 0.25
LN_EPS = 1e-5
NEG_INF = -1e30
ADAM_LR = 0.001
ADAM_B1 = 0.9
ADAM_B2 = 0.999
ADAM_EPS = 1e-08
ADAM_WD = 0.01
ADAM_STEP = 10
GELU_C0 = math.sqrt(2.0 / math.pi)
GELU_C1 = 0.044715

VMEM_LIMIT = 56 * 1024 * 1024


def _params(sem):
    return pltpu.CompilerParams(dimension_semantics=sem, vmem_limit_bytes=VMEM_LIMIT)


def _dot(a, b):
    return lax.dot_general(a, b, (((1,), (0,)), ((), ())), preferred_element_type=F32)


def _dot_nt(a, b):
    return lax.dot_general(a, b, (((1,), (1,)), ((), ())), preferred_element_type=F32)


def _dot_tn(a, b):
    return lax.dot_general(a, b, (((0,), (0,)), ((), ())), preferred_element_type=F32)


def _full(shape):
    nd = len(shape)
    return pl.BlockSpec(shape, lambda *_: (0,) * nd)


def _rowsum8(v):
    r, c = v.shape
    return jnp.sum(v.reshape(r // 8, 8, c), axis=0)


def _sigmoid(v):
    return 1.0 / (1.0 + jnp.exp(-v))


def _gelu_parts(v):
    v2 = v * v
    t = jnp.tanh(GELU_C0 * (v + GELU_C1 * v * v2))
    g = 0.5 * v * (1.0 + t)
    dg = 0.5 * (1.0 + t) + 0.5 * v * (1.0 - t * t) * (GELU_C0 * (1.0 + 3.0 * GELU_C1 * v2))
    return g, dg


def _ln_stats(z):
    mu = jnp.mean(z, axis=1, keepdims=True)
    zc = z - mu
    var = jnp.mean(zc * zc, axis=1, keepdims=True)
    rstd = lax.rsqrt(var + LN_EPS)
    return zc * rstd, rstd


def _ln_bwd(dxhat, xhat, rstd):
    m1 = jnp.mean(dxhat, axis=1, keepdims=True)
    m2 = jnp.mean(dxhat * xhat, axis=1, keepdims=True)
    return rstd * (dxhat - m1 - xhat * m2)


def _seg_mean64(v):
    r = v.shape[0]
    lo = lax.broadcasted_iota(jnp.int32, (r, 128), 1) < 64
    outs = []
    for j in range(v.shape[1] // 128):
        ch = v[:, 128 * j:128 * (j + 1)]
        s_lo = jnp.sum(jnp.where(lo, ch, 0.0), axis=1, keepdims=True)
        s_hi = jnp.sum(jnp.where(lo, 0.0, ch), axis=1, keepdims=True)
        outs.append(jnp.where(lo, s_lo, s_hi) * (1.0 / 64.0))
    return jnp.concatenate(outs, axis=1)


def _rms(a, g):
    r = lax.rsqrt(jnp.mean(a * a, axis=1, keepdims=True) + LN_EPS)
    return a * r * g, r


def _rms_bwd(dout, a, r, g):
    t = dout * g
    return r * t - a * (r * r * r) * jnp.mean(t * a, axis=1, keepdims=True)


def _exchange(name, arrays, scatter):
    n = len(arrays)

    def body(*refs):
        ins, outs = refs[:n], refs[n:2 * n]
        send_sems, recv_sems, loc_sems = refs[2 * n:]
        x, y, c = lax.axis_index("x"), lax.axis_index("y"), lax.axis_index("c")
        me = 4 * x + 2 * y + c

        def peer(j):
            px = 1 - x if j & 4 else x
            py = 1 - y if j & 2 else y
            pc = 1 - c if j & 1 else c
            return (px, py, pc), 4 * px + 2 * py + pc

        def remote(k, j):
            dev, idx = peer(j)
            return pltpu.make_async_remote_copy(
                src_ref=ins[k].at[idx] if scatter else ins[k],
                dst_ref=outs[k].at[me],
                send_sem=send_sems.at[k, j - 1], recv_sem=recv_sems.at[k, j - 1],
                device_id=dev, device_id_type=MESH)

        def arrival(k, j):
            dev, idx = peer(j)
            return pltpu.make_async_remote_copy(
                src_ref=ins[k].at[idx] if scatter else ins[k],
                dst_ref=outs[k].at[idx],
                send_sem=send_sems.at[k, j - 1], recv_sem=recv_sems.at[k, j - 1],
                device_id=dev, device_id_type=MESH)

        local = [pltpu.make_async_copy(ins[k].at[me] if scatter else ins[k], outs[k].at[me], loc_sems.at[k])
                 for k in range(n)]
        for cp in local:
            cp.start()
        sends = [remote(k, j) for j in (1, 2, 4, 3, 5, 6, 7) for k in range(n)]
        for cp in sends:
            cp.start()
        for j in (1, 2, 4, 3, 5, 6, 7):
            for k in range(n):
                arrival(k, j).wait_recv()
        for cp in sends:
            cp.wait_send()
        for cp in local:
            cp.wait()

    out_shape = [jax.ShapeDtypeStruct((N_DEV,) + (a.shape[1:] if scatter else a.shape), a.dtype) for a in arrays]
    any_spec = pl.BlockSpec(memory_space=pl.ANY)
    return pl.pallas_call(
        body, name=name, out_shape=out_shape,
        in_specs=[any_spec] * n, out_specs=[any_spec] * n,
        scratch_shapes=[pltpu.SemaphoreType.DMA((n, N_DEV - 1)), pltpu.SemaphoreType.DMA((n, N_DEV - 1)),
                        pltpu.SemaphoreType.DMA((n,))],
    )(*arrays)


def _mod_partial(c_all, w_ada, b_ada_cols):
    def body(c_ref, w_ref, b_ref, o_ref):
        cv = c_ref[...]
        s = (cv * _sigmoid(cv)).astype(BF16)
        o_ref[...] = _dot(s, w_ref[...].astype(BF16)) + b_ref[...]

    ncol = w_ada.shape[1]
    return pl.pallas_call(
        body, name="mod_partial", out_shape=jax.ShapeDtypeStruct((N_DEV, ncol), F32),
        in_specs=[_full(c_all.shape), _full(w_ada.shape), _full(b_ada_cols.shape)],
        out_specs=_full((N_DEV, ncol)), grid=(1,), compiler_params=_params(("arbitrary",)),
    )(c_all, w_ada, b_ada_cols)


def _bias_table(rel_bias, bucket):
    def body(rb_ref, bk_ref, o_ref):
        h = pl.program_id(0)
        bk = bk_ref[...]
        acc = jnp.zeros((BLK, 2 * BLK), F32)
        for b in range(N_BUCKETS):
            acc = jnp.where(bk == b, rb_ref[b, h], acc)
        dist = (lax.broadcasted_iota(jnp.int32, (BLK, 2 * BLK), 0) + BLK
                - lax.broadcasted_iota(jnp.int32, (BLK, 2 * BLK), 1))
        o_ref[0] = jnp.where((dist >= 0) & (dist < BLK), acc, NEG_INF)

    return pl.pallas_call(
        body, name="bias_table", out_shape=jax.ShapeDtypeStruct((N_HEADS, BLK, 2 * BLK), F32),
        in_specs=[pl.BlockSpec(memory_space=pltpu.SMEM), _full((BLK, 2 * BLK))],
        out_specs=pl.BlockSpec((1, BLK, 2 * BLK), lambda h: (h, 0, 0)), grid=(N_HEADS,),
        compiler_params=_params(("arbitrary",)),
    )(rel_bias, bucket)


def _bias_grad(dl_acc, bucket):
    def body(dl_ref, bk_ref, o_ref):
        bk = bk_ref[...]
        dl = dl_ref[0]
        lane = lax.broadcasted_iota(jnp.int32, (1, 128), 1)
        row = jnp.zeros((1, 128), F32)
        for b in range(N_BUCKETS):
            s = jnp.sum(jnp.sum(jnp.where(bk == b, dl, 0.0), axis=1, keepdims=True), axis=0, keepdims=True)
            row = jnp.where(lane == b, s, row)
        o_ref[0] = row

    return pl.pallas_call(
        body, name="bias_grad", out_shape=jax.ShapeDtypeStruct((N_HEADS, 1, 128), F32),
        in_specs=[pl.BlockSpec((1, BLK, 2 * BLK), lambda h: (h, 0, 0)), _full((BLK, 2 * BLK))],
        out_specs=pl.BlockSpec((1, 1, 128), lambda h: (h, 0, 0)), grid=(N_HEADS,),
        compiler_params=_params(("arbitrary",)),
    )(dl_acc, bucket)


def _inproj(x, sc1, sh1, w_in_t, b_in, tm):
    t, d = x.shape
    n = w_in_t.shape[0]

    def body(x_ref, sc_ref, sh_ref, w_ref, b_ref, proj_ref, h_ref):
        h = (x_ref[...] * (1.0 + sc_ref[...]) + sh_ref[...]).astype(BF16)
        h_ref[...] = h
        proj_ref[...] = _dot_nt(h, w_ref[...]) + b_ref[...]

    row = lambda w: pl.BlockSpec((tm, w), lambda i: (i, 0))
    return pl.pallas_call(
        body, name="inproj", grid=(t // tm,),
        out_shape=[jax.ShapeDtypeStruct((t, n), F32), jax.ShapeDtypeStruct((t, d), BF16)],
        in_specs=[row(d), _full((1, d)), _full((1, d)), _full((n, d)), _full((1, n))],
        out_specs=[row(n), row(d)], compiler_params=_params(("parallel",)),
    )(x, sc1, sh1, w_in_t, b_in)


def _half_masks():
    lo_q = lax.broadcasted_iota(jnp.int32, (BLK, 128), 1) < 64
    lo_k = lax.broadcasted_iota(jnp.int32, (2 * BLK, 128), 1) < 64
    return lo_q, lo_k


def _head_place(h):
    return h // 2, h % 2, h // 4


def _attn_heads(q, kk, vv, bias_ref, sinks_ref, n):
    lo_q, lo_k = _half_masks()
    kkb, kksb = kk.astype(BF16), pltpu.roll(kk, 64, 1).astype(BF16)
    vvb, vvsb = vv.astype(BF16), pltpu.roll(vv, 64, 1).astype(BF16)
    n0mask = (n == 0) & (lax.broadcasted_iota(jnp.int32, (BLK, 2 * BLK), 1) < BLK)
    chunks, probs = [], []
    for j in range(4):
        qc = q[:, 128 * j:128 * (j + 1)]
        acc = jnp.zeros((BLK, 128), F32)
        for pos in range(2):
            h = 2 * j + pos
            direct = (h // 4) == pos
            mq = lo_q if pos == 0 else jnp.logical_not(lo_q)
            mk = lo_k if pos == 0 else jnp.logical_not(lo_k)
            qm = jnp.where(mq, qc, 0.0).astype(BF16)
            logit = _dot_nt(qm, kkb if direct else kksb) * (HEAD_DIM ** -0.5) + bias_ref[h]
            logit = jnp.where(n0mask, NEG_INF, logit)
            sk = sinks_ref[h]
            m = jnp.maximum(jnp.max(logit, axis=1, keepdims=True), sk)
            e = jnp.exp(logit - m)
            es = jnp.exp(sk - m)
            den = jnp.sum(e, axis=1, keepdims=True) + es
            p = e / den
            vm = jnp.where(mk, vvb if direct else vvsb, jnp.zeros_like(vvb))
            acc = acc + _dot(p.astype(BF16), vm)
            probs.append((p, es / den))
        chunks.append(acc)
    return jnp.concatenate(chunks, axis=1), probs


def _gmlp_block(gu, gv, lng, lnb, ws_ref, bfull):
    lo_q, _ = _half_masks()
    u, du = _gelu_parts(gu)
    a, da = _gelu_parts(gv)
    mu = _seg_mean64(a)
    ac = a - mu
    rstd = lax.rsqrt(_seg_mean64(ac * ac) + LN_EPS)
    vhat = ac * rstd
    vn = vhat * lng + lnb
    chunks = []
    for j in range(4):
        vc = vn[:, 128 * j:128 * (j + 1)]
        acc = jnp.zeros((BLK, 128), F32)
        for pos in range(2):
            mq = lo_q if pos == 0 else jnp.logical_not(lo_q)
            acc = acc + _dot(ws_ref[2 * j + pos], jnp.where(mq, vc, 0.0).astype(BF16))
        chunks.append(acc)
    ms = jnp.concatenate(chunks, axis=1) + bfull
    return u * ms, (u, du, da, vhat, rstd, vn, ms)


def _mix_in_specs(nb):
    return [pl.BlockSpec((BLK, IN_W), lambda n: (n, 0)),
            pl.BlockSpec((BLK, 2 * KV_W), lambda n: (jnp.maximum(n - 1, 0), ATTN_W // (2 * KV_W))),
            _full((N_HEADS, BLK, 2 * BLK)),
            pl.BlockSpec(memory_space=pltpu.SMEM),
            _full((1, GMLP_W)), _full((1, GMLP_W)),
            _full((N_GROUPS, BLK, BLK)), _full((BLK, GMLP_W)),
            _full((1, ATTN_W)), _full((1, GMLP_W))]


def _split_proj(proj_ref, kvp_ref):
    q = proj_ref[:, 0:ATTN_W]
    k = proj_ref[:, ATTN_W:ATTN_W + KV_W]
    v = proj_ref[:, ATTN_W + KV_W:ATTN_W + 2 * KV_W]
    gu = proj_ref[:, ATTN_W + 2 * KV_W:ATTN_W + 2 * KV_W + GMLP_W]
    gv = proj_ref[:, ATTN_W + 2 * KV_W + GMLP_W:IN_W]
    kk = jnp.concatenate([kvp_ref[:, 0:KV_W], k], axis=0)
    vv = jnp.concatenate([kvp_ref[:, KV_W:2 * KV_W], v], axis=0)
    return q, kk, vv, gu, gv


def _mix_fwd(proj, bias, sinks, lng, lnb, ws, bfull, aog, gog):
    t = proj.shape[0]
    nb = t // BLK

    def body(proj_ref, kvp_ref, bias_ref, sinks_ref, lng_ref, lnb_ref, ws_ref, bfull_ref, aog_ref, gog_ref, out_ref):
        n = pl.program_id(0)
        q, kk, vv, gu, gv = _split_proj(proj_ref, kvp_ref)
        attn, _ = _attn_heads(q, kk, vv, bias_ref, sinks_ref, n)
        gm, _ = _gmlp_block(gu, gv, lng_ref[...], lnb_ref[...], ws_ref, bfull_ref[...])
        out_ref[:, 0:ATTN_W] = _rms(attn, aog_ref[...])[0].astype(BF16)
        out_ref[:, ATTN_W:ATTN_W + GMLP_W] = _rms(gm, gog_ref[...])[0].astype(BF16)

    return pl.pallas_call(
        body, name="mix_fwd", grid=(nb,), out_shape=jax.ShapeDtypeStruct((t, D_MODEL), BF16),
        in_specs=_mix_in_specs(nb), out_specs=pl.BlockSpec((BLK, D_MODEL), lambda n: (n, 0)),
        compiler_params=_params(("parallel",)),
    )(proj, proj, bias, sinks, lng, lnb, ws, bfull, aog, gog)


def _mix_bwd(proj, bias, sinks, lng, lnb, ws, ws_t, bfull, aog, gog, dy, w_out):
    t = proj.shape[0]
    nb = t // BLK

    def body(proj_ref, kvp_ref, bias_ref, sinks_ref, lng_ref, lnb_ref, ws_ref, bfull_ref, aog_ref, gog_ref,
             wst_ref, dy_ref, wout_ref,
             dproj_ref, dkvn_ref, dl_ref, dsink_ref, dlng_ref, dlnb_ref, dws_ref, dbs_ref, daog_ref, dgog_ref):
        n = pl.program_id(0)

        @pl.when(n == 0)
        def _():
            for r in (dl_ref, dsink_ref, dlng_ref, dlnb_ref, dws_ref, dbs_ref, daog_ref, dgog_ref):
                r[...] = jnp.zeros_like(r)

        lo_q, lo_k = _half_masks()
        q, kk, vv, gu, gv = _split_proj(proj_ref, kvp_ref)
        dmix = _dot_nt(dy_ref[...], wout_ref[...])
        dma, dmg = dmix[:, 0:ATTN_W], dmix[:, ATTN_W:ATTN_W + GMLP_W]

        attn, probs = _attn_heads(q, kk, vv, bias_ref, sinks_ref, n)
        aog = aog_ref[...]
        _, r_a = _rms(attn, aog)
        daog_ref[...] += _rowsum8(dma * attn * r_a)
        dattn = _rms_bwd(dma, attn, r_a, aog)

        kkb, kksb = kk.astype(BF16), pltpu.roll(kk, 64, 1).astype(BF16)
        vvb, vvsb = vv.astype(BF16), pltpu.roll(vv, 64, 1).astype(BF16)
        lane = lax.broadcasted_iota(jnp.int32, (BLK, 128), 1)
        dk_d = jnp.zeros((2 * BLK, 128), F32)
        dk_s = jnp.zeros((2 * BLK, 128), F32)
        dv_d = jnp.zeros((2 * BLK, 128), F32)
        dv_s = jnp.zeros((2 * BLK, 128), F32)
        dsink = jnp.zeros((BLK, 128), F32)
        dq_chunks = []
        for j in range(4):
            qc = q[:, 128 * j:128 * (j + 1)]
            doc = dattn[:, 128 * j:128 * (j + 1)]
            dq = jnp.zeros((BLK, 128), F32)
            for pos in range(2):
                h = 2 * j + pos
                direct = (h // 4) == pos
                mq = lo_q if pos == 0 else jnp.logical_not(lo_q)
                mk = lo_k if pos == 0 else jnp.logical_not(lo_k)
                p, psink = probs[h]
                qm = jnp.where(mq, qc, 0.0).astype(BF16)
                dom = jnp.where(mq, doc, 0.0).astype(BF16)
                dp = _dot_nt(dom, vvb if direct else vvsb)
                rs = jnp.sum(p * dp, axis=1, keepdims=True)
                dl = p * (dp - rs)
                dl_ref[h] += dl
                dsink = dsink + jnp.where(lane == h, -psink * rs, 0.0)
                dls = (dl * (HEAD_DIM ** -0.5)).astype(BF16)
                km = jnp.where(mk, kkb if direct else kksb, jnp.zeros_like(kkb))
                dq = dq + _dot(dls, km)
                dk_h = _dot_tn(dls, qm)
                dv_h = _dot_tn(p.astype(BF16), dom)
                if direct:
                    dk_d, dv_d = dk_d + dk_h, dv_d + dv_h
                else:
                    dk_s, dv_s = dk_s + dk_h, dv_s + dv_h
            dq_chunks.append(dq)
        dsink_ref[...] += dsink
        dk = dk_d + pltpu.roll(dk_s, 64, 1)
        dv = dv_d + pltpu.roll(dv_s, 64, 1)
        for j in range(4):
            dproj_ref[:, 128 * j:128 * (j + 1)] = dq_chunks[j]
        dproj_ref[:, ATTN_W:ATTN_W + KV_W] = dk[BLK:2 * BLK]
        dproj_ref[:, ATTN_W + KV_W:ATTN_W + 2 * KV_W] = dv[BLK:2 * BLK]
        dkvn_ref[:, 0:KV_W] = dk[0:BLK]
        dkvn_ref[:, KV_W:2 * KV_W] = dv[0:BLK]

        lng = lng_ref[...]
        gog = gog_ref[...]
        gm, (u, du, da, vhat, rstd, vn, ms) = _gmlp_block(gu, gv, lng, lnb_ref[...], ws_ref, bfull_ref[...])
        _, r_g = _rms(gm, gog)
        dgog_ref[...] += _rowsum8(dmg * gm * r_g)
        dgm = _rms_bwd(dmg, gm, r_g, gog)
        dproj_ref[:, ATTN_W + 2 * KV_W:ATTN_W + 2 * KV_W + GMLP_W] = dgm * ms * du
        dms = dgm * u
        dbs_ref[...] += dms
        dvn_chunks = []
        for j in range(4):
            dmc = dms[:, 128 * j:128 * (j + 1)]
            vcb = vn[:, 128 * j:128 * (j + 1)].astype(BF16)
            acc = jnp.zeros((BLK, 128), F32)
            for pos in range(2):
                g = 2 * j + pos
                mq = lo_q if pos == 0 else jnp.logical_not(lo_q)
                dm = jnp.where(mq, dmc, 0.0).astype(BF16)
                dws_ref[g] += _dot_nt(dm, vcb)
                acc = acc + _dot(wst_ref[g], dm)
            dvn_chunks.append(acc)
        dvn = jnp.concatenate(dvn_chunks, axis=1)
        dlng_ref[...] += _rowsum8(dvn * vhat)
        dlnb_ref[...] += _rowsum8(dvn)
        dvh = dvn * lng
        dact = rstd * (dvh - _seg_mean64(dvh) - vhat * _seg_mean64(dvh * vhat))
        dproj_ref[:, ATTN_W + 2 * KV_W + GMLP_W:IN_W] = dact * da

    acc8 = lambda w: jax.ShapeDtypeStruct((8, w), F32)
    out_shape = [jax.ShapeDtypeStruct((t, IN_W), F32), jax.ShapeDtypeStruct((t, 2 * KV_W), F32),
                 jax.ShapeDtypeStruct((N_HEADS, BLK, 2 * BLK), F32), jax.ShapeDtypeStruct((BLK, 128), F32),
                 acc8(GMLP_W), acc8(GMLP_W), jax.ShapeDtypeStruct((N_GROUPS, BLK, BLK), F32),
                 jax.ShapeDtypeStruct((BLK, GMLP_W), F32), acc8(ATTN_W), acc8(GMLP_W)]
    out_specs = [pl.BlockSpec((BLK, IN_W), lambda n: (n, 0)),
                 pl.BlockSpec((BLK, 2 * KV_W), lambda n: ((n + nb - 1) % nb, 0)),
                 _full((N_HEADS, BLK, 2 * BLK)), _full((BLK, 128)), _full((8, GMLP_W)), _full((8, GMLP_W)),
                 _full((N_GROUPS, BLK, BLK)), _full((BLK, GMLP_W)), _full((8, ATTN_W)), _full((8, GMLP_W))]
    in_specs = _mix_in_specs(nb) + [_full((N_GROUPS, BLK, BLK)),
                                    pl.BlockSpec((BLK, D_MODEL), lambda n: (n, 0)),
                                    _full((D_MODEL, D_MODEL))]
    return pl.pallas_call(
        body, name="mix_bwd", grid=(nb,), out_shape=out_shape, in_specs=in_specs, out_specs=out_specs,
        compiler_params=_params(("arbitrary",)),
    )(proj, proj, bias, sinks, lng, lnb, ws, bfull, aog, gog, ws_t, dy, w_out)


def _outproj(mixed, w_out, x, g1, ln1g, ln1b, sc2, sh2, tm):
    t, d = x.shape

    def body(mx_ref, w_ref, x_ref, g1_ref, lg_ref, lb_ref, sc_ref, sh_ref, y_ref, x1_ref, h2_ref):
        y = _dot(mx_ref[...], w_ref[...])
        xhat, _ = _ln_stats(ALPHA * x_ref[...] + g1_ref[...] * y)
        x1 = xhat * lg_ref[...] + lb_ref[...]
        y_ref[...] = y
        x1_ref[...] = x1
        h2_ref[...] = (x1 * (1.0 + sc_ref[...]) + sh_ref[...]).astype(BF16)

    row = pl.BlockSpec((tm, d), lambda i: (i, 0))
    vec = _full((1, d))
    return pl.pallas_call(
        body, name="outproj", grid=(t // tm,),
        out_shape=[jax.ShapeDtypeStruct((t, d), F32), jax.ShapeDtypeStruct((t, d), F32),
                   jax.ShapeDtypeStruct((t, d), BF16)],
        in_specs=[row, _full((d, d)), row, vec, vec, vec, vec, vec], out_specs=[row, row, row],
        compiler_params=_params(("parallel",)),
    )(mixed, w_out, x, g1, ln1g, ln1b, sc2, sh2)


def _ffn_up(h2, w_gu_t, tm, tn):
    t, d = h2.shape
    nff = D_FF // tn

    def body(h_ref, wg_ref, wu_ref, gate_ref, up_ref, act_ref):
        h = h_ref[...]
        g = _dot_nt(h, wg_ref[...])
        u = _dot_nt(h, wu_ref[...])
        gate_ref[...] = g.astype(BF16)
        up_ref[...] = u.astype(BF16)
        act_ref[...] = (g * _sigmoid(g) * u).astype(BF16)

    out = pl.BlockSpec((tm, tn), lambda j, i: (i, j))
    shp = jax.ShapeDtypeStruct((t, D_FF), BF16)
    return pl.pallas_call(
        body, name="ffn_up", grid=(nff, t // tm), out_shape=[shp, shp, shp],
        in_specs=[pl.BlockSpec((tm, d), lambda j, i: (i, 0)),
                  pl.BlockSpec((tn, d), lambda j, i: (j, 0)),
                  pl.BlockSpec((tn, d), lambda j, i: (j + nff, 0))],
        out_specs=[out, out, out], compiler_params=_params(("parallel", "parallel")),
    )(h2, w_gu_t, w_gu_t)


def _ffn_down(act, w_down, x1, target, g2, ln2g, ln2b, tm):
    t, d = x1.shape

    def body(act_ref, w_ref, x1_ref, tg_ref, g2_ref, lg_ref, lb_ref,
             dz_ref, dy_ref, loss_ref, dlg_ref, dlb_ref, dg2_ref):
        @pl.when(pl.program_id(0) == 0)
        def _():
            for r in (loss_ref, dlg_ref, dlb_ref, dg2_ref):
                r[...] = jnp.zeros_like(r)

        y2 = _dot(act_ref[...], w_ref[...])
        g2 = g2_ref[...]
        lg = lg_ref[...]
        xhat, rstd = _ln_stats(ALPHA * x1_ref[...] + g2 * y2)
        err = xhat * lg + lb_ref[...] - tg_ref[...]
        loss_ref[...] += _rowsum8(err * err)
        dx2 = err * (1.0 / d)
        dlg_ref[...] += _rowsum8(dx2 * xhat)
        dlb_ref[...] += _rowsum8(dx2)
        dz = _ln_bwd(dx2 * lg, xhat, rstd)
        dg2_ref[...] += _rowsum8(dz * y2)
        dz_ref[...] = dz
        dy_ref[...] = (g2 * dz).astype(BF16)

    row = pl.BlockSpec((tm, d), lambda i: (i, 0))
    vec = _full((1, d))
    acc = _full((8, d))
    acc_shape = jax.ShapeDtypeStruct((8, d), F32)
    return pl.pallas_call(
        body, name="ffn_down", grid=(t // tm,),
        out_shape=[jax.ShapeDtypeStruct((t, d), F32), jax.ShapeDtypeStruct((t, d), BF16)] + [acc_shape] * 4,
        in_specs=[pl.BlockSpec((tm, D_FF), lambda i: (i, 0)), _full((D_FF, d)), row, row, vec, vec, vec],
        out_specs=[row, row, acc, acc, acc, acc], compiler_params=_params(("arbitrary",)),
    )(act, w_down, x1, target, g2, ln2g, ln2b)


def _ffn_dact(dy2, w_down, gate, up, tm, tn):
    t, d = dy2.shape

    def body(dy_ref, w_ref, g_ref, u_ref, dg_ref, du_ref):
        dact = _dot_nt(dy_ref[...], w_ref[...])
        g = g_ref[...].astype(F32)
        u = u_ref[...].astype(F32)
        s = _sigmoid(g)
        dg_ref[...] = (dact * u * (s * (1.0 + g * (1.0 - s)))).astype(BF16)
        du_ref[...] = (dact * (g * s)).astype(BF16)

    tile = pl.BlockSpec((tm, tn), lambda j, i: (i, j))
    shp = jax.ShapeDtypeStruct((t, D_FF), BF16)
    return pl.pallas_call(
        body, name="ffn_dact", grid=(D_FF // tn, t // tm), out_shape=[shp, shp],
        in_specs=[pl.BlockSpec((tm, d), lambda j, i: (i, 0)), pl.BlockSpec((tn, d), lambda j, i: (j, 0)), tile, tile],
        out_specs=[tile, tile], compiler_params=_params(("parallel", "parallel")),
    )(dy2, w_down, gate, up)


def _ffn_dh2(dgate, dup, w_gu_t, x1, x, y, dz2, sc2, g1, ln1g, tm, tk):
    t, d = x1.shape
    nk = D_FF // tk

    def body(dg_ref, du_ref, wg_ref, wu_ref, x1_ref, x_ref, y_ref, dz2_ref, sc_ref, g1_ref, lg_ref,
             dz1_ref, dy_ref, dsc_ref, dsh_ref, dlg_ref, dlb_ref, dg1_ref, acc_ref):
        i, k = pl.program_id(0), pl.program_id(1)

        @pl.when((i == 0) & (k == 0))
        def _():
            for r in (dsc_ref, dsh_ref, dlg_ref, dlb_ref, dg1_ref):
                r[...] = jnp.zeros_like(r)

        part = _dot(dg_ref[...], wg_ref[...]) + _dot(du_ref[...], wu_ref[...])

        @pl.when(k == 0)
        def _():
            acc_ref[...] = part

        @pl.when(k > 0)
        def _():
            acc_ref[...] += part

        @pl.when(k == nk - 1)
        def _():
            dh2 = acc_ref[...]
            x1 = x1_ref[...]
            y = y_ref[...]
            g1 = g1_ref[...]
            lg = lg_ref[...]
            dsc_ref[...] += _rowsum8(dh2 * x1)
            dsh_ref[...] += _rowsum8(dh2)
            dx1 = dh2 * (1.0 + sc_ref[...]) + ALPHA * dz2_ref[...]
            xhat, rstd = _ln_stats(ALPHA * x_ref[...] + g1 * y)
            dlg_ref[...] += _rowsum8(dx1 * xhat)
            dlb_ref[...] += _rowsum8(dx1)
            dz1 = _ln_bwd(dx1 * lg, xhat, rstd)
            dg1_ref[...] += _rowsum8(dz1 * y)
            dz1_ref[...] = dz1
            dy_ref[...] = (g1 * dz1).astype(BF16)

    row = pl.BlockSpec((tm, d), lambda i, k: (i, 0))
    vec = _full((1, d))
    acc = _full((8, d))
    acc_shape = jax.ShapeDtypeStruct((8, d), F32)
    a_tile = pl.BlockSpec((tm, tk), lambda i, k: (i, k))
    return pl.pallas_call(
        body, name="ffn_dh2", grid=(t // tm, nk),
        out_shape=[jax.ShapeDtypeStruct((t, d), F32), jax.ShapeDtypeStruct((t, d), BF16)] + [acc_shape] * 5,
        in_specs=[a_tile, a_tile, pl.BlockSpec((tk, d), lambda i, k: (k, 0)),
                  pl.BlockSpec((tk, d), lambda i, k: (k + nk, 0)), row, row, row, row, vec, vec, vec],
        out_specs=[row, row, acc, acc, acc, acc, acc],
        scratch_shapes=[pltpu.VMEM((tm, d), F32)], compiler_params=_params(("arbitrary", "arbitrary")),
    )(dgate, dup, w_gu_t, w_gu_t, x1, x, y, dz2, sc2, g1, ln1g)


def _din(dproj, dkvn, w_in_t, x, dz1, sc1, tm):
    t, d = x.shape

    def body(dp_ref, dkv_ref, w_ref, x_ref, dz1_ref, sc_ref, dx_ref, dpb_ref, dbin_ref, dsc_ref, dsh_ref):
        @pl.when(pl.program_id(0) == 0)
        def _():
            for r in (dbin_ref, dsc_ref, dsh_ref):
                r[...] = jnp.zeros_like(r)

        dp = jnp.concatenate([dp_ref[:, 0:ATTN_W], dp_ref[:, ATTN_W:ATTN_W + 2 * KV_W] + dkv_ref[...],
                              dp_ref[:, ATTN_W + 2 * KV_W:IN_W]], axis=1)
        dbin_ref[...] += _rowsum8(dp)
        dpb = dp.astype(BF16)
        dpb_ref[...] = dpb
        dh = _dot(dpb, w_ref[...])
        dsc_ref[...] += _rowsum8(dh * x_ref[...])
        dsh_ref[...] += _rowsum8(dh)
        dx_ref[...] = dh * (1.0 + sc_ref[...]) + ALPHA * dz1_ref[...]

    row = lambda w: pl.BlockSpec((tm, w), lambda i: (i, 0))
    return pl.pallas_call(
        body, name="din", grid=(t // tm,),
        out_shape=[jax.ShapeDtypeStruct((t, d), F32), jax.ShapeDtypeStruct((t, IN_W), BF16),
                   jax.ShapeDtypeStruct((8, IN_W), F32), jax.ShapeDtypeStruct((8, d), F32),
                   jax.ShapeDtypeStruct((8, d), F32)],
        in_specs=[row(IN_W), row(2 * KV_W), _full((IN_W, d)), row(d), row(d), _full((1, d))],
        out_specs=[row(d), row(IN_W), _full((8, IN_W)), _full((8, d)), _full((8, d))],
        compiler_params=_params(("arbitrary",)),
    )(dproj, dkvn, w_in_t, x, dz1, sc1)


def _wgrad(name, a, b, tmm, tk):
    t, m = a.shape
    n = b.shape[1]
    nk = t // tk

    def body(a_ref, b_ref, o_ref, acc_ref):
        k = pl.program_id(1)
        part = _dot_tn(a_ref[...], b_ref[...])

        @pl.when(k == 0)
        def _():
            acc_ref[...] = part

        @pl.when(k > 0)
        def _():
            acc_ref[...] += part

        @pl.when(k == nk - 1)
        def _():
            o_ref[...] = acc_ref[...].astype(BF16)

    return pl.pallas_call(
        body, name=name, grid=(m // tmm, nk), out_shape=jax.ShapeDtypeStruct((m, n), BF16),
        in_specs=[pl.BlockSpec((tk, tmm), lambda i, k: (k, i)), pl.BlockSpec((tk, n), lambda i, k: (k, 0))],
        out_specs=pl.BlockSpec((tmm, n), lambda i, k: (i, 0)),
        scratch_shapes=[pltpu.VMEM((tmm, n), F32)], compiler_params=_params(("parallel", "arbitrary")),
    )(a, b)


def _adamw(w, g, m, v):
    m = ADAM_B1 * m + (1.0 - ADAM_B1) * g
    v = ADAM_B2 * v + (1.0 - ADAM_B2) * (g * g)
    m_hat = m / (1.0 - ADAM_B1 ** ADAM_STEP)
    v_hat = v / (1.0 - ADAM_B2 ** ADAM_STEP)
    delta = -ADAM_LR * (m_hat / (jnp.sqrt(v_hat) + ADAM_EPS) + ADAM_WD * w)
    return delta, m, v


def _adam_reduce(name, parts, w, m, v, tr):
    r, cdim = w.shape

    def body(p_ref, w_ref, m_ref, v_ref, g_ref, d_ref, mo_ref, vo_ref):
        g = p_ref[0].astype(F32)
        for s in range(1, N_DEV):
            g = g + p_ref[s].astype(F32)
        d_ref[...], mo_ref[...], vo_ref[...] = _adamw(w_ref[...], g, m_ref[...], v_ref[...])
        g_ref[...] = g

    tile = pl.BlockSpec((tr, cdim), lambda i: (i, 0))
    shp = jax.ShapeDtypeStruct((r, cdim), F32)
    return pl.pallas_call(
        body, name=name, grid=(r // tr,), out_shape=[shp] * 4,
        in_specs=[pl.BlockSpec((N_DEV, tr, cdim), lambda i: (0, i, 0)), tile, tile, tile],
        out_specs=[tile] * 4, compiler_params=_params(("parallel",)),
    )(parts, w, m, v)


def _adam_w_ada(c_all_t, dmod_cols, w, m, v):
    def body(ct_ref, dm_ref, w_ref, m_ref, v_ref, g_ref, d_ref, mo_ref, vo_ref):
        ct = ct_ref[...]
        s = (ct * _sigmoid(ct)).astype(BF16)
        g = _dot(s, dm_ref[...].astype(BF16))
        d_ref[...], mo_ref[...], vo_ref[...] = _adamw(w_ref[...], g, m_ref[...], v_ref[...])
        g_ref[...] = g

    shp = jax.ShapeDtypeStruct(w.shape, F32)
    return pl.pallas_call(
        body, name="adam_w_ada", grid=(1,), out_shape=[shp] * 4,
        in_specs=[_full(c_all_t.shape), _full(dmod_cols.shape)] + [_full(w.shape)] * 3,
        out_specs=[_full(w.shape)] * 4, compiler_params=_params(("arbitrary",)),
    )(c_all_t, dmod_cols, w, m, v)


SMALL = ["b_ada", "rel_bias", "b_in", "attn_sinks", "gmlp_ln_g", "gmlp_ln_b", "gmlp_w_s", "gmlp_b_s",
         "attn_out_g", "gmlp_out_g", "ln1_g", "ln1_b", "ln2_g", "ln2_b"]
WEIGHTS = ["rel_bias", "w_ada", "b_ada", "w_in", "b_in", "attn_sinks", "gmlp_ln_g", "gmlp_ln_b", "gmlp_w_s",
           "gmlp_b_s", "attn_out_g", "gmlp_out_g", "w_out", "ln1_g", "ln1_b", "w_gate_up", "w_down", "ln2_g", "ln2_b"]


def _seg_rows(nelem):
    return -(-nelem // 1024) * 8


def _pack(named):
    parts = []
    for name in SMALL:
        flat = named[name].reshape(-1).astype(F32)
        rows = _seg_rows(flat.shape[0])
        parts.append(jnp.pad(flat, (0, rows * 128 - flat.shape[0])).reshape(rows, 128))
    return jnp.concatenate(parts, axis=0)


def _unpack(packed, shapes):
    out, r0 = {}, 0
    for name in SMALL:
        nelem = math.prod(shapes[name])
        rows = _seg_rows(nelem)
        out[name] = packed[r0:r0 + rows].reshape(-1)[:nelem].reshape(shapes[name])
        r0 += rows
    return out


def _t5_bucket_map():
    qi = jnp.arange(BLK)[:, None]
    si = jnp.arange(2 * BLK)[None, :]
    n = jnp.maximum(qi + BLK - si, 0)
    max_exact = N_BUCKETS // 2
    nf = jnp.maximum(n, max_exact).astype(F32)
    large = max_exact + (jnp.log(nf / max_exact) / math.log(MAX_DISTANCE / max_exact)
                         * (N_BUCKETS - max_exact)).astype(jnp.int32)
    large = jnp.minimum(large, N_BUCKETS - 1)
    return jnp.where(n < max_exact, n, large).astype(jnp.int32)


def kernel(x, c, rel_bias, w_ada, b_ada, w_in, b_in, attn_sinks, gmlp_ln_g, gmlp_ln_b, gmlp_w_s, gmlp_b_s, attn_out_g, gmlp_out_g, w_out, ln1_g, ln1_b, w_gate_up, w_down, ln2_g, ln2_b, loss_target, m_rel_bias, m_w_ada, m_b_ada, m_w_in, m_b_in, m_attn_sinks, m_gmlp_ln_g, m_gmlp_ln_b, m_gmlp_w_s, m_gmlp_b_s, m_attn_out_g, m_gmlp_out_g, m_w_out, m_ln1_g, m_ln1_b, m_w_gate_up, m_w_down, m_ln2_g, m_ln2_b, v_rel_bias, v_w_ada, v_b_ada, v_w_in, v_b_in, v_attn_sinks, v_gmlp_ln_g, v_gmlp_ln_b, v_gmlp_w_s, v_gmlp_b_s, v_attn_out_g, v_gmlp_out_g, v_w_out, v_ln1_g, v_ln1_b, v_w_gate_up, v_w_down, v_ln2_g, v_ln2_b):
    wts = dict(rel_bias=rel_bias, w_ada=w_ada, b_ada=b_ada, w_in=w_in, b_in=b_in, attn_sinks=attn_sinks,
               gmlp_ln_g=gmlp_ln_g, gmlp_ln_b=gmlp_ln_b, gmlp_w_s=gmlp_w_s, gmlp_b_s=gmlp_b_s,
               attn_out_g=attn_out_g, gmlp_out_g=gmlp_out_g, w_out=w_out, ln1_g=ln1_g, ln1_b=ln1_b,
               w_gate_up=w_gate_up, w_down=w_down, ln2_g=ln2_g, ln2_b=ln2_b)
    mom_m = dict(rel_bias=m_rel_bias, w_ada=m_w_ada, b_ada=m_b_ada, w_in=m_w_in, b_in=m_b_in,
                 attn_sinks=m_attn_sinks, gmlp_ln_g=m_gmlp_ln_g, gmlp_ln_b=m_gmlp_ln_b, gmlp_w_s=m_gmlp_w_s,
                 gmlp_b_s=m_gmlp_b_s, attn_out_g=m_attn_out_g, gmlp_out_g=m_gmlp_out_g, w_out=m_w_out,
                 ln1_g=m_ln1_g, ln1_b=m_ln1_b, w_gate_up=m_w_gate_up, w_down=m_w_down, ln2_g=m_ln2_g,
                 ln2_b=m_ln2_b)
    mom_v = dict(rel_bias=v_rel_bias, w_ada=v_w_ada, b_ada=v_b_ada, w_in=v_w_in, b_in=v_b_in,
                 attn_sinks=v_attn_sinks, gmlp_ln_g=v_gmlp_ln_g, gmlp_ln_b=v_gmlp_ln_b, gmlp_w_s=v_gmlp_w_s,
                 gmlp_b_s=v_gmlp_b_s, attn_out_g=v_attn_out_g, gmlp_out_g=v_gmlp_out_g, w_out=v_w_out,
                 ln1_g=v_ln1_g, ln1_b=v_ln1_b, w_gate_up=v_w_gate_up, w_down=v_w_down, ln2_g=v_ln2_g,
                 ln2_b=v_ln2_b)

    t = x.shape[1]
    tm = min(512, t)
    tn_ff = D_FF // 2
    tk_tok = min(1024, t)
    me = 4 * lax.axis_index("x") + 2 * lax.axis_index("y") + lax.axis_index("c")
    xs = x[0]
    target = loss_target[0]

    shards = [jnp.broadcast_to(c, (8, D_MODEL)),
              w_in[0].T.astype(BF16), w_out[0].astype(BF16), w_gate_up[0].T.astype(BF16), w_down[0].astype(BF16)]
    c_g, w_in_g, w_out_g, w_gu_g, w_down_g = _exchange("gather_weights", shards, scatter=False)
    c_all = c_g[:, 0, :]
    w_in_t = w_in_g.reshape(IN_W, D_MODEL)
    w_out_f = w_out_g.reshape(D_MODEL, D_MODEL)
    w_gu_t = w_gu_g.reshape(2 * D_FF, D_MODEL)
    w_down_f = w_down_g.reshape(D_FF, D_MODEL)

    ncol = w_ada.shape[2]
    b_cols = lax.dynamic_slice(b_ada, (0, me * ncol), (1, ncol))
    mod_part = _mod_partial(c_all, w_ada[0], b_cols)
    (mod_g,) = _exchange("gather_mod", [mod_part], scatter=False)
    mod = lax.dynamic_slice(mod_g, (0, me, 0), (N_DEV, 1, ncol)).reshape(1, N_DEV * ncol)
    sh1, sc1, g1, sh2, sc2, g2 = [mod[:, i * D_MODEL:(i + 1) * D_MODEL] for i in range(6)]

    bucket = _t5_bucket_map()
    bias = _bias_table(rel_bias, bucket)
    causal = jnp.tril(jnp.ones((BLK, BLK), dtype=bool))
    ws = jnp.where(causal[None], gmlp_w_s[0], 0.0).astype(BF16)
    ws_t = jnp.swapaxes(ws, 1, 2)
    bfull = jnp.repeat(gmlp_b_s[0].T, GMLP_W // N_GROUPS, axis=1)
    sinks = attn_sinks[0]

    proj, h1 = _inproj(xs, sc1, sh1, w_in_t, b_in, tm)
    mixed = _mix_fwd(proj, bias, sinks, gmlp_ln_g, gmlp_ln_b, ws, bfull, attn_out_g, gmlp_out_g)
    y1, x1, h2 = _outproj(mixed, w_out_f, xs, g1, ln1_g, ln1_b, sc2, sh2, tm)
    gate, up, act = _ffn_up(h2, w_gu_t, tm, tn_ff)
    dz2, dy2, loss_p, d_ln2g, d_ln2b, d_g2 = _ffn_down(act, w_down_f, x1, target, g2, ln2_g, ln2_b, tm)
    loss = lax.psum(0.5 / D_MODEL * jnp.sum(loss_p), ("x", "y", "c"))

    dw_down = _wgrad("wgrad_down", act, dy2, tn_ff, tk_tok)
    dgate, dup = _ffn_dact(dy2, w_down_f, gate, up, tm, tn_ff)
    dz1, dy1, d_sc2, d_sh2, d_ln1g, d_ln1b, d_g1 = _ffn_dh2(dgate, dup, w_gu_t, x1, xs, y1, dz2, sc2, g1, ln1_g,
                                                           tm, tn_ff)
    dw_gu_t = jnp.concatenate([_wgrad("wgrad_gate", dgate, h2, tn_ff, tk_tok),
                               _wgrad("wgrad_up", dup, h2, tn_ff, tk_tok)], axis=0)
    dw_out = _wgrad("wgrad_out", mixed, dy1, D_MODEL, tk_tok)
    (dproj, dkvn, dl_acc, dsink_acc, d_lng, d_lnb, d_ws, d_bs, d_aog, d_gog) = _mix_bwd(
        proj, bias, sinks, gmlp_ln_g, gmlp_ln_b, ws, ws_t, bfull, attn_out_g, gmlp_out_g, dy1, w_out_f)
    grad_x, dproj_b, d_bin, d_sc1, d_sh1 = _din(dproj, dkvn, w_in_t, xs, dz1, sc1, tm)
    dw_in_t = _wgrad("wgrad_in", dproj_b, h1, IN_W, tk_tok)
    d_relb = _bias_grad(dl_acc, bucket)

    rsum = lambda a: jnp.sum(a, axis=0)
    dmod = jnp.concatenate([rsum(d_sh1), rsum(d_sc1), rsum(d_g1), rsum(d_sh2), rsum(d_sc2), rsum(d_g2)])
    small_g = dict(
        b_ada=dmod, rel_bias=d_relb[:, 0, :N_BUCKETS].T, b_in=rsum(d_bin), attn_sinks=rsum(dsink_acc)[:N_HEADS],
        gmlp_ln_g=rsum(d_lng), gmlp_ln_b=rsum(d_lnb), gmlp_w_s=jnp.where(causal[None], d_ws, 0.0),
        gmlp_b_s=jnp.sum(d_bs.reshape(BLK, N_GROUPS, GMLP_W // N_GROUPS), axis=2).T,
        attn_out_g=rsum(d_aog), gmlp_out_g=rsum(d_gog), ln1_g=rsum(d_ln1g), ln1_b=rsum(d_ln1b),
        ln2_g=rsum(d_ln2g), ln2_b=rsum(d_ln2b))
    (small_all,) = _exchange("gather_small_grads", [_pack(small_g)], scatter=False)
    rows = small_all.shape[1]
    small_out = _adam_reduce("adam_small", small_all, _pack(wts), _pack(mom_m), _pack(mom_v), rows)
    shapes = {k: wts[k].shape for k in SMALL}
    sg, sd, sm, sv = [_unpack(o, shapes) for o in small_out]

    dmod_all = small_all[:, :_seg_rows(6 * D_MODEL), :].reshape(N_DEV, 6 * D_MODEL)
    dmod_cols = lax.dynamic_slice(dmod_all, (0, me * ncol), (N_DEV, ncol))
    kpad = 128 - N_DEV
    ada = _adam_w_ada(jnp.pad(c_all.T, ((0, 0), (0, kpad))), jnp.pad(dmod_cols, ((0, kpad), (0, 0))),
                      w_ada[0], m_w_ada[0], v_w_ada[0])

    parts = [dw_in_t.reshape(N_DEV, -1, D_MODEL), dw_out.reshape(N_DEV, -1, D_MODEL),
             dw_gu_t.reshape(N_DEV, -1, D_MODEL), dw_down.reshape(N_DEV, -1, D_MODEL)]
    r_in, r_out, r_gu, r_down = _exchange("scatter_grads", parts, scatter=True)
    tr = lambda a: jnp.swapaxes(a, -1, -2)
    big = {}
    big["w_in"] = [tr(o)[None] for o in _adam_reduce("adam_w_in", r_in, w_in[0].T, m_w_in[0].T, v_w_in[0].T, 112)]
    big["w_out"] = [o[None] for o in _adam_reduce("adam_w_out", r_out, w_out[0], m_w_out[0], v_w_out[0], 128)]
    big["w_gate_up"] = [tr(o)[None] for o in _adam_reduce("adam_w_gu", r_gu, w_gate_up[0].T, m_w_gate_up[0].T,
                                                           v_w_gate_up[0].T, 352)]
    big["w_down"] = [o[None] for o in _adam_reduce("adam_w_down", r_down, w_down[0], m_w_down[0], v_w_down[0], 176)]
    big["w_ada"] = [o[None] for o in ada]

    outs = [[], [], [], []]
    for name in WEIGHTS:
        src = big[name] if name in big else [sg[name], sd[name], sm[name], sv[name]]
        for i in range(4):
            outs[i].append(src[i])
    return (loss, grad_x[None], *outs[0], *outs[1], *outs[2], *outs[3])
```

```python
import math

import jax
import jax.numpy as jnp
from jax import lax
from jax.experimental import pallas as pl
from jax.experimental.pallas import tpu as pltpu

F32 = jnp.float32
BF16 = jnp.bfloat16
MESH = pl.DeviceIdType.MESH

N_DEV = 8
D_MODEL = 1024
HEAD_DIM = 64
N_HEADS = 8
N_GROUPS = 8
ATTN_W = 512
KV_W = 128
GMLP_W = 512
IN_W = 1792
BLK = 128
N_BUCKETS = 32
MAX_DISTANCE = 128
D_FF = 2816
ALPHA = 2.0 ** 0.25
LN_EPS = 1e-5
NEG_INF = -1e30
ADAM_LR = 0.001
ADAM_B1 = 0.9
ADAM_B2 = 0.999
ADAM_EPS = 1e-08
ADAM_WD = 0.01
ADAM_STEP = 10
GELU_C0 = math.sqrt(2.0 / math.pi)
GELU_C1 = 0.044715

VMEM_LIMIT = 56 * 1024 * 1024


def _params(sem):
    return pltpu.CompilerParams(dimension_semantics=sem, vmem_limit_bytes=VMEM_LIMIT)


def _dot(a, b):
    return lax.dot_general(a, b, (((1,), (0,)), ((), ())), preferred_element_type=F32)


def _dot_nt(a, b):
    return lax.dot_general(a, b, (((1,), (1,)), ((), ())), preferred_element_type=F32)


def _dot_tn(a, b):
    return lax.dot_general(a, b, (((0,), (0,)), ((), ())), preferred_element_type=F32)


def _full(shape):
    nd = len(shape)
    return pl.BlockSpec(shape, lambda *_: (0,) * nd)


def _rowsum8(v):
    r, c = v.shape
    return jnp.sum(v.reshape(r // 8, 8, c), axis=0)


def _sigmoid(v):
    return 1.0 / (1.0 + jnp.exp(-v))


def _gelu_parts(v):
    v2 = v * v
    t = jnp.tanh(GELU_C0 * (v + GELU_C1 * v * v2))
    g = 0.5 * v * (1.0 + t)
    dg = 0.5 * (1.0 + t) + 0.5 * v * (1.0 - t * t) * (GELU_C0 * (1.0 + 3.0 * GELU_C1 * v2))
    return g, dg


def _ln_stats(z):
    mu = jnp.mean(z, axis=1, keepdims=True)
    zc = z - mu
    var = jnp.mean(zc * zc, axis=1, keepdims=True)
    rstd = lax.rsqrt(var + LN_EPS)
    return zc * rstd, rstd


def _ln_bwd(dxhat, xhat, rstd):
    m1 = jnp.mean(dxhat, axis=1, keepdims=True)
    m2 = jnp.mean(dxhat * xhat, axis=1, keepdims=True)
    return rstd * (dxhat - m1 - xhat * m2)


def _seg_mean64(v):
    r = v.shape[0]
    lo = lax.broadcasted_iota(jnp.int32, (r, 128), 1) < 64
    outs = []
    for j in range(v.shape[1] // 128):
        ch = v[:, 128 * j:128 * (j + 1)]
        s_lo = jnp.sum(jnp.where(lo, ch, 0.0), axis=1, keepdims=True)
        s_hi = jnp.sum(jnp.where(lo, 0.0, ch), axis=1, keepdims=True)
        outs.append(jnp.where(lo, s_lo, s_hi) * (1.0 / 64.0))
    return jnp.concatenate(outs, axis=1)


def _rms(a, g):
    r = lax.rsqrt(jnp.mean(a * a, axis=1, keepdims=True) + LN_EPS)
    return a * r * g, r


def _rms_bwd(dout, a, r, g):
    t = dout * g
    return r * t - a * (r * r * r) * jnp.mean(t * a, axis=1, keepdims=True)


PEER_ORDER = (1, 2, 4, 3, 5, 6, 7)


def _peer(j):
    x, y, c = lax.axis_index("x"), lax.axis_index("y"), lax.axis_index("c")
    px = 1 - x if j & 4 else x
    py = 1 - y if j & 2 else y
    pc = 1 - c if j & 1 else c
    return (px, py, pc), 4 * px + 2 * py + pc


def _exchange_copies(ins, outs, scatter, send_sems, recv_sems, loc_sems):
    me = 4 * lax.axis_index("x") + 2 * lax.axis_index("y") + lax.axis_index("c")
    n = len(ins)

    def src(k, idx):
        return ins[k].at[idx] if scatter[k] else ins[k]

    def remote(k, j, mine):
        dev, idx = _peer(j)
        return pltpu.make_async_remote_copy(
            src_ref=src(k, idx), dst_ref=outs[k].at[me if mine else idx],
            send_sem=send_sems.at[k, j - 1], recv_sem=recv_sems.at[k, j - 1],
            device_id=dev, device_id_type=MESH)

    local = [pltpu.make_async_copy(src(k, me), outs[k].at[me], loc_sems.at[k]) for k in range(n)]
    sends = [remote(k, j, True) for j in PEER_ORDER for k in range(n)]
    arrivals = [remote(k, j, False) for j in PEER_ORDER for k in range(n)]
    return local, sends, arrivals


def _exchange_start(*a):
    local, sends, _ = _exchange_copies(*a)
    for cp in local + sends:
        cp.start()


def _exchange_wait(*a):
    local, sends, arrivals = _exchange_copies(*a)
    for cp in arrivals:
        cp.wait_recv()
    for cp in sends:
        cp.wait_send()
    for cp in local:
        cp.wait()


def _exchange_shapes(arrays, scatter):
    return [jax.ShapeDtypeStruct((N_DEV,) + (a.shape[1:] if s else a.shape), a.dtype) for a, s in zip(arrays, scatter)]


def _exchange_sems(n):
    return [pltpu.SemaphoreType.DMA((n, N_DEV - 1)), pltpu.SemaphoreType.DMA((n, N_DEV - 1)),
            pltpu.SemaphoreType.DMA((n,))]


def _exchange(name, arrays, scatter):
    n = len(arrays)

    def body(*refs):
        a = (refs[:n], refs[n:2 * n], scatter) + tuple(refs[2 * n:])
        _exchange_start(*a)
        _exchange_wait(*a)

    any_spec = pl.BlockSpec(memory_space=pl.ANY)
    return pl.pallas_call(
        body, name=name, out_shape=_exchange_shapes(arrays, scatter),
        in_specs=[any_spec] * n, out_specs=[any_spec] * n, scratch_shapes=_exchange_sems(n),
    )(*arrays)


def _call(body, *, name, grid, in_specs, out_specs, out_shape, args, sem, scratch_shapes=(), comm=None):
    if comm is None:
        outs = pl.pallas_call(body, name=name, grid=grid, in_specs=list(in_specs), out_specs=list(out_specs),
                              out_shape=list(out_shape), scratch_shapes=list(scratch_shapes),
                              compiler_params=_params(sem))(*args)
        return list(outs), []
    arrays, scatter = comm
    n_in, n_out, nc, ns = len(in_specs), len(out_specs), len(arrays), len(scratch_shapes)

    def hosted(*refs):
        ins, cins = refs[:n_in], refs[n_in:n_in + nc]
        outs, couts = refs[n_in + nc:n_in + nc + n_out], refs[n_in + nc + n_out:n_in + 2 * nc + n_out]
        scratch = refs[n_in + 2 * nc + n_out:]
        ex = (cins, couts, scatter) + tuple(scratch[ns:])
        first = pl.program_id(0) == 0
        last = pl.program_id(0) == grid[0] - 1
        for ax in range(1, len(grid)):
            first = first & (pl.program_id(ax) == 0)
            last = last & (pl.program_id(ax) == grid[ax] - 1)

        @pl.when(first)
        def _():
            _exchange_start(*ex)

        body(*ins, *outs, *scratch[:ns])

        @pl.when(last)
        def _():
            _exchange_wait(*ex)

    any_spec = pl.BlockSpec(memory_space=pl.ANY)
    res = pl.pallas_call(
        hosted, name=name, grid=grid, in_specs=list(in_specs) + [any_spec] * nc,
        out_specs=list(out_specs) + [any_spec] * nc, out_shape=list(out_shape) + _exchange_shapes(arrays, scatter),
        scratch_shapes=list(scratch_shapes) + _exchange_sems(nc),
        compiler_params=_params(tuple("arbitrary" for _ in grid)))(*args, *arrays)
    return list(res[:n_out]), list(res[n_out:])


def _mod_partial(c_all, w_ada, b_ada_cols):
    def body(c_ref, w_ref, b_ref, o_ref):
        cv = c_ref[...]
        s = (cv * _sigmoid(cv)).astype(BF16)
        o_ref[...] = _dot(s, w_ref[...].astype(BF16)) + b_ref[...]

    ncol = w_ada.shape[1]
    return pl.pallas_call(
        body, name="mod_partial", out_shape=jax.ShapeDtypeStruct((N_DEV, ncol), F32),
        in_specs=[_full(c_all.shape), _full(w_ada.shape), _full(b_ada_cols.shape)],
        out_specs=_full((N_DEV, ncol)), grid=(1,), compiler_params=_params(("arbitrary",)),
    )(c_all, w_ada, b_ada_cols)


def _bias_table(rel_bias, bucket):
    def body(rb_ref, bk_ref, o_ref):
        h = pl.program_id(0)
        bk = bk_ref[...]
        acc = jnp.zeros((BLK, 2 * BLK), F32)
        for b in range(N_BUCKETS):
            acc = jnp.where(bk == b, rb_ref[b, h], acc)
        dist = (lax.broadcasted_iota(jnp.int32, (BLK, 2 * BLK), 0) + BLK
                - lax.broadcasted_iota(jnp.int32, (BLK, 2 * BLK), 1))
        o_ref[0] = jnp.where((dist >= 0) & (dist < BLK), acc, NEG_INF)

    return pl.pallas_call(
        body, name="bias_table", out_shape=jax.ShapeDtypeStruct((N_HEADS, BLK, 2 * BLK), F32),
        in_specs=[pl.BlockSpec(memory_space=pltpu.SMEM), _full((BLK, 2 * BLK))],
        out_specs=pl.BlockSpec((1, BLK, 2 * BLK), lambda h: (h, 0, 0)), grid=(N_HEADS,),
        compiler_params=_params(("arbitrary",)),
    )(rel_bias, bucket)


def _bias_grad(dl_acc, bucket):
    def body(dl_ref, bk_ref, o_ref):
        bk = bk_ref[...]
        dl = dl_ref[0]
        lane = lax.broadcasted_iota(jnp.int32, (1, 128), 1)
        row = jnp.zeros((1, 128), F32)
        for b in range(N_BUCKETS):
            s = jnp.sum(jnp.sum(jnp.where(bk == b, dl, 0.0), axis=1, keepdims=True), axis=0, keepdims=True)
            row = jnp.where(lane == b, s, row)
        o_ref[0] = row

    return pl.pallas_call(
        body, name="bias_grad", out_shape=jax.ShapeDtypeStruct((N_HEADS, 1, 128), F32),
        in_specs=[pl.BlockSpec((1, BLK, 2 * BLK), lambda h: (h, 0, 0)), _full((BLK, 2 * BLK))],
        out_specs=pl.BlockSpec((1, 1, 128), lambda h: (h, 0, 0)), grid=(N_HEADS,),
        compiler_params=_params(("arbitrary",)),
    )(dl_acc, bucket)


def _inproj(x, sc1, sh1, w_in_t, b_in, tm):
    t, d = x.shape
    n = w_in_t.shape[0]

    def body(x_ref, sc_ref, sh_ref, w_ref, b_ref, proj_ref, h_ref):
        h = (x_ref[...] * (1.0 + sc_ref[...]) + sh_ref[...]).astype(BF16)
        h_ref[...] = h
        proj_ref[...] = _dot_nt(h, w_ref[...]) + b_ref[...]

    row = lambda w: pl.BlockSpec((tm, w), lambda i: (i, 0))
    return pl.pallas_call(
        body, name="inproj", grid=(t // tm,),
        out_shape=[jax.ShapeDtypeStruct((t, n), F32), jax.ShapeDtypeStruct((t, d), BF16)],
        in_specs=[row(d), _full((1, d)), _full((1, d)), _full((n, d)), _full((1, n))],
        out_specs=[row(n), row(d)], compiler_params=_params(("parallel",)),
    )(x, sc1, sh1, w_in_t, b_in)


def _half_masks():
    lo_q = lax.broadcasted_iota(jnp.int32, (BLK, 128), 1) < 64
    lo_k = lax.broadcasted_iota(jnp.int32, (2 * BLK, 128), 1) < 64
    return lo_q, lo_k


def _head_place(h):
    return h // 2, h % 2, h // 4


def _attn_heads(q, kk, vv, bias_ref, sinks_ref, n):
    lo_q, lo_k = _half_masks()
    kkb, kksb = kk.astype(BF16), pltpu.roll(kk, 64, 1).astype(BF16)
    vvb, vvsb = vv.astype(BF16), pltpu.roll(vv, 64, 1).astype(BF16)
    n0mask = (n == 0) & (lax.broadcasted_iota(jnp.int32, (BLK, 2 * BLK), 1) < BLK)
    chunks, probs = [], []
    for j in range(4):
        qc = q[:, 128 * j:128 * (j + 1)]
        acc = jnp.zeros((BLK, 128), F32)
        for pos in range(2):
            h = 2 * j + pos
            direct = (h // 4) == pos
            mq = lo_q if pos == 0 else jnp.logical_not(lo_q)
            mk = lo_k if pos == 0 else jnp.logical_not(lo_k)
            qm = jnp.where(mq, qc, 0.0).astype(BF16)
            logit = _dot_nt(qm, kkb if direct else kksb) * (HEAD_DIM ** -0.5) + bias_ref[h]
            logit = jnp.where(n0mask, NEG_INF, logit)
            sk = sinks_ref[h]
            m = jnp.maximum(jnp.max(logit, axis=1, keepdims=True), sk)
            e = jnp.exp(logit - m)
            es = jnp.exp(sk - m)
            den = jnp.sum(e, axis=1, keepdims=True) + es
            p = e / den
            vm = jnp.where(mk, vvb if direct else vvsb, jnp.zeros_like(vvb))
            acc = acc + _dot(p.astype(BF16), vm)
            probs.append((p, es / den))
        chunks.append(acc)
    return jnp.concatenate(chunks, axis=1), probs


def _gmlp_block(gu, gv, lng, lnb, ws_ref, bfull):
    lo_q, _ = _half_masks()
    u, du = _gelu_parts(gu)
    a, da = _gelu_parts(gv)
    mu = _seg_mean64(a)
    ac = a - mu
    rstd = lax.rsqrt(_seg_mean64(ac * ac) + LN_EPS)
    vhat = ac * rstd
    vn = vhat * lng + lnb
    chunks = []
    for j in range(4):
        vc = vn[:, 128 * j:128 * (j + 1)]
        acc = jnp.zeros((BLK, 128), F32)
        for pos in range(2):
            mq = lo_q if pos == 0 else jnp.logical_not(lo_q)
            acc = acc + _dot(ws_ref[2 * j + pos], jnp.where(mq, vc, 0.0).astype(BF16))
        chunks.append(acc)
    ms = jnp.concatenate(chunks, axis=1) + bfull
    return u * ms, (u, du, da, vhat, rstd, vn, ms)


def _mix_in_specs(nb):
    return [pl.BlockSpec((BLK, IN_W), lambda n: (n, 0)),
            pl.BlockSpec((BLK, 2 * KV_W), lambda n: (jnp.maximum(n - 1, 0), ATTN_W // (2 * KV_W))),
            _full((N_HEADS, BLK, 2 * BLK)),
            pl.BlockSpec(memory_space=pltpu.SMEM),
            _full((1, GMLP_W)), _full((1, GMLP_W)),
            _full((N_GROUPS, BLK, BLK)), _full((BLK, GMLP_W)),
            _full((1, ATTN_W)), _full((1, GMLP_W))]


def _split_proj(proj_ref, kvp_ref):
    q = proj_ref[:, 0:ATTN_W]
    k = proj_ref[:, ATTN_W:ATTN_W + KV_W]
    v = proj_ref[:, ATTN_W + KV_W:ATTN_W + 2 * KV_W]
    gu = proj_ref[:, ATTN_W + 2 * KV_W:ATTN_W + 2 * KV_W + GMLP_W]
    gv = proj_ref[:, ATTN_W + 2 * KV_W + GMLP_W:IN_W]
    kk = jnp.concatenate([kvp_ref[:, 0:KV_W], k], axis=0)
    vv = jnp.concatenate([kvp_ref[:, KV_W:2 * KV_W], v], axis=0)
    return q, kk, vv, gu, gv


def _mix_fwd(proj, bias, sinks, lng, lnb, ws, bfull, aog, gog, comm):
    t = proj.shape[0]
    nb = t // BLK

    def body(proj_ref, kvp_ref, bias_ref, sinks_ref, lng_ref, lnb_ref, ws_ref, bfull_ref, aog_ref, gog_ref, out_ref):
        n = pl.program_id(0)
        q, kk, vv, gu, gv = _split_proj(proj_ref, kvp_ref)
        attn, _ = _attn_heads(q, kk, vv, bias_ref, sinks_ref, n)
        gm, _ = _gmlp_block(gu, gv, lng_ref[...], lnb_ref[...], ws_ref, bfull_ref[...])
        out_ref[:, 0:ATTN_W] = _rms(attn, aog_ref[...])[0].astype(BF16)
        out_ref[:, ATTN_W:ATTN_W + GMLP_W] = _rms(gm, gog_ref[...])[0].astype(BF16)

    return _call(
        body, name="mix_fwd", grid=(nb,), out_shape=[jax.ShapeDtypeStruct((t, D_MODEL), BF16)],
        in_specs=_mix_in_specs(nb), out_specs=[pl.BlockSpec((BLK, D_MODEL), lambda n: (n, 0))],
        sem=("parallel",), comm=comm, args=(proj, proj, bias, sinks, lng, lnb, ws, bfull, aog, gog))


def _mix_bwd(proj, bias, sinks, lng, lnb, ws, ws_t, bfull, aog, gog, dy, w_out, comm):
    t = proj.shape[0]
    nb = t // BLK

    def body(proj_ref, kvp_ref, bias_ref, sinks_ref, lng_ref, lnb_ref, ws_ref, bfull_ref, aog_ref, gog_ref,
             wst_ref, dy_ref, wout_ref,
             dproj_ref, dkvn_ref, dl_ref, dsink_ref, dlng_ref, dlnb_ref, dws_ref, dbs_ref, daog_ref, dgog_ref):
        n = pl.program_id(0)

        @pl.when(n == 0)
        def _():
            for r in (dl_ref, dsink_ref, dlng_ref, dlnb_ref, dws_ref, dbs_ref, daog_ref, dgog_ref):
                r[...] = jnp.zeros_like(r)

        lo_q, lo_k = _half_masks()
        q, kk, vv, gu, gv = _split_proj(proj_ref, kvp_ref)
        dmix = _dot_nt(dy_ref[...], wout_ref[...])
        dma, dmg = dmix[:, 0:ATTN_W], dmix[:, ATTN_W:ATTN_W + GMLP_W]

        attn, probs = _attn_heads(q, kk, vv, bias_ref, sinks_ref, n)
        aog = aog_ref[...]
        _, r_a = _rms(attn, aog)
        daog_ref[...] += _rowsum8(dma * attn * r_a)
        dattn = _rms_bwd(dma, attn, r_a, aog)

        kkb, kksb = kk.astype(BF16), pltpu.roll(kk, 64, 1).astype(BF16)
        vvb, vvsb = vv.astype(BF16), pltpu.roll(vv, 64, 1).astype(BF16)
        lane = lax.broadcasted_iota(jnp.int32, (BLK, 128), 1)
        dk_d = jnp.zeros((2 * BLK, 128), F32)
        dk_s = jnp.zeros((2 * BLK, 128), F32)
        dv_d = jnp.zeros((2 * BLK, 128), F32)
        dv_s = jnp.zeros((2 * BLK, 128), F32)
        dsink = jnp.zeros((BLK, 128), F32)
        dq_chunks = []
        for j in range(4):
            qc = q[:, 128 * j:128 * (j + 1)]
            doc = dattn[:, 128 * j:128 * (j + 1)]
            dq = jnp.zeros((BLK, 128), F32)
            for pos in range(2):
                h = 2 * j + pos
                direct = (h // 4) == pos
                mq = lo_q if pos == 0 else jnp.logical_not(lo_q)
                mk = lo_k if pos == 0 else jnp.logical_not(lo_k)
                p, psink = probs[h]
                qm = jnp.where(mq, qc, 0.0).astype(BF16)
                dom = jnp.where(mq, doc, 0.0).astype(BF16)
                dp = _dot_nt(dom, vvb if direct else vvsb)
                rs = jnp.sum(p * dp, axis=1, keepdims=True)
                dl = p * (dp - rs)
                dl_ref[h] += dl
                dsink = dsink + jnp.where(lane == h, -psink * rs, 0.0)
                dls = (dl * (HEAD_DIM ** -0.5)).astype(BF16)
                km = jnp.where(mk, kkb if direct else kksb, jnp.zeros_like(kkb))
                dq = dq + _dot(dls, km)
                dk_h = _dot_tn(dls, qm)
                dv_h = _dot_tn(p.astype(BF16), dom)
                if direct:
                    dk_d, dv_d = dk_d + dk_h, dv_d + dv_h
                else:
                    dk_s, dv_s = dk_s + dk_h, dv_s + dv_h
            dq_chunks.append(dq)
        dsink_ref[...] += dsink
        dk = dk_d + pltpu.roll(dk_s, 64, 1)
        dv = dv_d + pltpu.roll(dv_s, 64, 1)
        for j in range(4):
            dproj_ref[:, 128 * j:128 * (j + 1)] = dq_chunks[j]
        dproj_ref[:, ATTN_W:ATTN_W + KV_W] = dk[BLK:2 * BLK]
        dproj_ref[:, ATTN_W + KV_W:ATTN_W + 2 * KV_W] = dv[BLK:2 * BLK]
        dkvn_ref[:, 0:KV_W] = dk[0:BLK]
        dkvn_ref[:, KV_W:2 * KV_W] = dv[0:BLK]

        lng = lng_ref[...]
        gog = gog_ref[...]
        gm, (u, du, da, vhat, rstd, vn, ms) = _gmlp_block(gu, gv, lng, lnb_ref[...], ws_ref, bfull_ref[...])
        _, r_g = _rms(gm, gog)
        dgog_ref[...] += _rowsum8(dmg * gm * r_g)
        dgm = _rms_bwd(dmg, gm, r_g, gog)
        dproj_ref[:, ATTN_W + 2 * KV_W:ATTN_W + 2 * KV_W + GMLP_W] = dgm * ms * du
        dms = dgm * u
        dbs_ref[...] += dms
        dvn_chunks = []
        for j in range(4):
            dmc = dms[:, 128 * j:128 * (j + 1)]
            vcb = vn[:, 128 * j:128 * (j + 1)].astype(BF16)
            acc = jnp.zeros((BLK, 128), F32)
            for pos in range(2):
                g = 2 * j + pos
                mq = lo_q if pos == 0 else jnp.logical_not(lo_q)
                dm = jnp.where(mq, dmc, 0.0).astype(BF16)
                dws_ref[g] += _dot_nt(dm, vcb)
                acc = acc + _dot(wst_ref[g], dm)
            dvn_chunks.append(acc)
        dvn = jnp.concatenate(dvn_chunks, axis=1)
        dlng_ref[...] += _rowsum8(dvn * vhat)
        dlnb_ref[...] += _rowsum8(dvn)
        dvh = dvn * lng
        dact = rstd * (dvh - _seg_mean64(dvh) - vhat * _seg_mean64(dvh * vhat))
        dproj_ref[:, ATTN_W + 2 * KV_W + GMLP_W:IN_W] = dact * da

    acc8 = lambda w: jax.ShapeDtypeStruct((8, w), F32)
    out_shape = [jax.ShapeDtypeStruct((t, IN_W), F32), jax.ShapeDtypeStruct((t, 2 * KV_W), F32),
                 jax.ShapeDtypeStruct((N_HEADS, BLK, 2 * BLK), F32), jax.ShapeDtypeStruct((BLK, 128), F32),
                 acc8(GMLP_W), acc8(GMLP_W), jax.ShapeDtypeStruct((N_GROUPS, BLK, BLK), F32),
                 jax.ShapeDtypeStruct((BLK, GMLP_W), F32), acc8(ATTN_W), acc8(GMLP_W)]
    out_specs = [pl.BlockSpec((BLK, IN_W), lambda n: (n, 0)),
                 pl.BlockSpec((BLK, 2 * KV_W), lambda n: ((n + nb - 1) % nb, 0)),
                 _full((N_HEADS, BLK, 2 * BLK)), _full((BLK, 128)), _full((8, GMLP_W)), _full((8, GMLP_W)),
                 _full((N_GROUPS, BLK, BLK)), _full((BLK, GMLP_W)), _full((8, ATTN_W)), _full((8, GMLP_W))]
    in_specs = _mix_in_specs(nb) + [_full((N_GROUPS, BLK, BLK)),
                                    pl.BlockSpec((BLK, D_MODEL), lambda n: (n, 0)),
                                    _full((D_MODEL, D_MODEL))]
    return _call(
        body, name="mix_bwd", grid=(nb,), out_shape=out_shape, in_specs=in_specs, out_specs=out_specs,
        sem=("arbitrary",), comm=comm, args=(proj, proj, bias, sinks, lng, lnb, ws, bfull, aog, gog, ws_t, dy, w_out))


def _outproj(mixed, w_out, x, g1, ln1g, ln1b, sc2, sh2, tm):
    t, d = x.shape

    def body(mx_ref, w_ref, x_ref, g1_ref, lg_ref, lb_ref, sc_ref, sh_ref, y_ref, x1_ref, h2_ref):
        y = _dot(mx_ref[...], w_ref[...])
        xhat, _ = _ln_stats(ALPHA * x_ref[...] + g1_ref[...] * y)
        x1 = xhat * lg_ref[...] + lb_ref[...]
        y_ref[...] = y
        x1_ref[...] = x1
        h2_ref[...] = (x1 * (1.0 + sc_ref[...]) + sh_ref[...]).astype(BF16)

    row = pl.BlockSpec((tm, d), lambda i: (i, 0))
    vec = _full((1, d))
    return pl.pallas_call(
        body, name="outproj", grid=(t // tm,),
        out_shape=[jax.ShapeDtypeStruct((t, d), F32), jax.ShapeDtypeStruct((t, d), F32),
                   jax.ShapeDtypeStruct((t, d), BF16)],
        in_specs=[row, _full((d, d)), row, vec, vec, vec, vec, vec], out_specs=[row, row, row],
        compiler_params=_params(("parallel",)),
    )(mixed, w_out, x, g1, ln1g, ln1b, sc2, sh2)


def _ffn_up(h2, w_gu_t, tm, tn, comm):
    t, d = h2.shape
    nff = D_FF // tn

    def body(h_ref, wg_ref, wu_ref, gate_ref, up_ref, act_ref):
        h = h_ref[...]
        g = _dot_nt(h, wg_ref[...])
        u = _dot_nt(h, wu_ref[...])
        gate_ref[...] = g.astype(BF16)
        up_ref[...] = u.astype(BF16)
        act_ref[...] = (g * _sigmoid(g) * u).astype(BF16)

    out = pl.BlockSpec((tm, tn), lambda j, i: (i, j))
    shp = jax.ShapeDtypeStruct((t, D_FF), BF16)
    return _call(
        body, name="ffn_up", grid=(nff, t // tm), out_shape=[shp, shp, shp],
        in_specs=[pl.BlockSpec((tm, d), lambda j, i: (i, 0)),
                  pl.BlockSpec((tn, d), lambda j, i: (j, 0)),
                  pl.BlockSpec((tn, d), lambda j, i: (j + nff, 0))],
        out_specs=[out, out, out], sem=("parallel", "parallel"), comm=comm, args=(h2, w_gu_t, w_gu_t))


def _ffn_down(act, w_down, x1, target, g2, ln2g, ln2b, tm):
    t, d = x1.shape

    def body(act_ref, w_ref, x1_ref, tg_ref, g2_ref, lg_ref, lb_ref,
             dz_ref, dy_ref, loss_ref, dlg_ref, dlb_ref, dg2_ref):
        @pl.when(pl.program_id(0) == 0)
        def _():
            for r in (loss_ref, dlg_ref, dlb_ref, dg2_ref):
                r[...] = jnp.zeros_like(r)

        y2 = _dot(act_ref[...], w_ref[...])
        g2 = g2_ref[...]
        lg = lg_ref[...]
        xhat, rstd = _ln_stats(ALPHA * x1_ref[...] + g2 * y2)
        err = xhat * lg + lb_ref[...] - tg_ref[...]
        loss_ref[...] += _rowsum8(err * err)
        dx2 = err * (1.0 / d)
        dlg_ref[...] += _rowsum8(dx2 * xhat)
        dlb_ref[...] += _rowsum8(dx2)
        dz = _ln_bwd(dx2 * lg, xhat, rstd)
        dg2_ref[...] += _rowsum8(dz * y2)
        dz_ref[...] = dz
        dy_ref[...] = (g2 * dz).astype(BF16)

    row = pl.BlockSpec((tm, d), lambda i: (i, 0))
    vec = _full((1, d))
    acc = _full((8, d))
    acc_shape = jax.ShapeDtypeStruct((8, d), F32)
    return pl.pallas_call(
        body, name="ffn_down", grid=(t // tm,),
        out_shape=[jax.ShapeDtypeStruct((t, d), F32), jax.ShapeDtypeStruct((t, d), BF16)] + [acc_shape] * 4,
        in_specs=[pl.BlockSpec((tm, D_FF), lambda i: (i, 0)), _full((D_FF, d)), row, row, vec, vec, vec],
        out_specs=[row, row, acc, acc, acc, acc], compiler_params=_params(("arbitrary",)),
    )(act, w_down, x1, target, g2, ln2g, ln2b)


def _ffn_dact(dy2, w_down, gate, up, tm, tn, comm):
    t, d = dy2.shape

    def body(dy_ref, w_ref, g_ref, u_ref, dg_ref, du_ref):
        dact = _dot_nt(dy_ref[...], w_ref[...])
        g = g_ref[...].astype(F32)
        u = u_ref[...].astype(F32)
        s = _sigmoid(g)
        dg_ref[...] = (dact * u * (s * (1.0 + g * (1.0 - s)))).astype(BF16)
        du_ref[...] = (dact * (g * s)).astype(BF16)

    tile = pl.BlockSpec((tm, tn), lambda j, i: (i, j))
    shp = jax.ShapeDtypeStruct((t, D_FF), BF16)
    return _call(
        body, name="ffn_dact", grid=(D_FF // tn, t // tm), out_shape=[shp, shp],
        in_specs=[pl.BlockSpec((tm, d), lambda j, i: (i, 0)), pl.BlockSpec((tn, d), lambda j, i: (j, 0)), tile, tile],
        out_specs=[tile, tile], sem=("parallel", "parallel"), comm=comm, args=(dy2, w_down, gate, up))


def _ffn_dh2(dgate, dup, w_gu_t, x1, x, y, dz2, sc2, g1, ln1g, tm, tk):
    t, d = x1.shape
    nk = D_FF // tk

    def body(dg_ref, du_ref, wg_ref, wu_ref, x1_ref, x_ref, y_ref, dz2_ref, sc_ref, g1_ref, lg_ref,
             dz1_ref, dy_ref, dsc_ref, dsh_ref, dlg_ref, dlb_ref, dg1_ref, acc_ref):
        i, k = pl.program_id(0), pl.program_id(1)

        @pl.when((i == 0) & (k == 0))
        def _():
            for r in (dsc_ref, dsh_ref, dlg_ref, dlb_ref, dg1_ref):
                r[...] = jnp.zeros_like(r)

        part = _dot(dg_ref[...], wg_ref[...]) + _dot(du_ref[...], wu_ref[...])

        @pl.when(k == 0)
        def _():
            acc_ref[...] = part

        @pl.when(k > 0)
        def _():
            acc_ref[...] += part

        @pl.when(k == nk - 1)
        def _():
            dh2 = acc_ref[...]
            x1 = x1_ref[...]
            y = y_ref[...]
            g1 = g1_ref[...]
            lg = lg_ref[...]
            dsc_ref[...] += _rowsum8(dh2 * x1)
            dsh_ref[...] += _rowsum8(dh2)
            dx1 = dh2 * (1.0 + sc_ref[...]) + ALPHA * dz2_ref[...]
            xhat, rstd = _ln_stats(ALPHA * x_ref[...] + g1 * y)
            dlg_ref[...] += _rowsum8(dx1 * xhat)
            dlb_ref[...] += _rowsum8(dx1)
            dz1 = _ln_bwd(dx1 * lg, xhat, rstd)
            dg1_ref[...] += _rowsum8(dz1 * y)
            dz1_ref[...] = dz1
            dy_ref[...] = (g1 * dz1).astype(BF16)

    row = pl.BlockSpec((tm, d), lambda i, k: (i, 0))
    vec = _full((1, d))
    acc = _full((8, d))
    acc_shape = jax.ShapeDtypeStruct((8, d), F32)
    a_tile = pl.BlockSpec((tm, tk), lambda i, k: (i, k))
    return pl.pallas_call(
        body, name="ffn_dh2", grid=(t // tm, nk),
        out_shape=[jax.ShapeDtypeStruct((t, d), F32), jax.ShapeDtypeStruct((t, d), BF16)] + [acc_shape] * 5,
        in_specs=[a_tile, a_tile, pl.BlockSpec((tk, d), lambda i, k: (k, 0)),
                  pl.BlockSpec((tk, d), lambda i, k: (k + nk, 0)), row, row, row, row, vec, vec, vec],
        out_specs=[row, row, acc, acc, acc, acc, acc],
        scratch_shapes=[pltpu.VMEM((tm, d), F32)], compiler_params=_params(("arbitrary", "arbitrary")),
    )(dgate, dup, w_gu_t, w_gu_t, x1, x, y, dz2, sc2, g1, ln1g)


def _din(dproj, dkvn, w_in_t, x, dz1, sc1, tm, comm):
    t, d = x.shape

    def body(dp_ref, dkv_ref, w_ref, x_ref, dz1_ref, sc_ref, dx_ref, dpb_ref, dbin_ref, dsc_ref, dsh_ref):
        @pl.when(pl.program_id(0) == 0)
        def _():
            for r in (dbin_ref, dsc_ref, dsh_ref):
                r[...] = jnp.zeros_like(r)

        dp = jnp.concatenate([dp_ref[:, 0:ATTN_W], dp_ref[:, ATTN_W:ATTN_W + 2 * KV_W] + dkv_ref[...],
                              dp_ref[:, ATTN_W + 2 * KV_W:IN_W]], axis=1)
        dbin_ref[...] += _rowsum8(dp)
        dpb = dp.astype(BF16)
        dpb_ref[...] = dpb
        dh = _dot(dpb, w_ref[...])
        dsc_ref[...] += _rowsum8(dh * x_ref[...])
        dsh_ref[...] += _rowsum8(dh)
        dx_ref[...] = dh * (1.0 + sc_ref[...]) + ALPHA * dz1_ref[...]

    row = lambda w: pl.BlockSpec((tm, w), lambda i: (i, 0))
    return _call(
        body, name="din", grid=(t // tm,),
        out_shape=[jax.ShapeDtypeStruct((t, d), F32), jax.ShapeDtypeStruct((t, IN_W), BF16),
                   jax.ShapeDtypeStruct((8, IN_W), F32), jax.ShapeDtypeStruct((8, d), F32),
                   jax.ShapeDtypeStruct((8, d), F32)],
        in_specs=[row(IN_W), row(2 * KV_W), _full((IN_W, d)), row(d), row(d), _full((1, d))],
        out_specs=[row(d), row(IN_W), _full((8, IN_W)), _full((8, d)), _full((8, d))],
        sem=("arbitrary",), comm=comm, args=(dproj, dkvn, w_in_t, x, dz1, sc1))


def _wgrad(name, a, b, tmm, tk):
    t, m = a.shape
    n = b.shape[1]
    nk = t // tk

    def body(a_ref, b_ref, o_ref, acc_ref):
        k = pl.program_id(1)
        part = _dot_tn(a_ref[...], b_ref[...])

        @pl.when(k == 0)
        def _():
            acc_ref[...] = part

        @pl.when(k > 0)
        def _():
            acc_ref[...] += part

        @pl.when(k == nk - 1)
        def _():
            o_ref[...] = acc_ref[...].astype(BF16)

    return pl.pallas_call(
        body, name=name, grid=(m // tmm, nk), out_shape=jax.ShapeDtypeStruct((m, n), BF16),
        in_specs=[pl.BlockSpec((tk, tmm), lambda i, k: (k, i)), pl.BlockSpec((tk, n), lambda i, k: (k, 0))],
        out_specs=pl.BlockSpec((tmm, n), lambda i, k: (i, 0)),
        scratch_shapes=[pltpu.VMEM((tmm, n), F32)], compiler_params=_params(("parallel", "arbitrary")),
    )(a, b)


def _adamw(w, g, m, v):
    m = ADAM_B1 * m + (1.0 - ADAM_B1) * g
    v = ADAM_B2 * v + (1.0 - ADAM_B2) * (g * g)
    m_hat = m / (1.0 - ADAM_B1 ** ADAM_STEP)
    v_hat = v / (1.0 - ADAM_B2 ** ADAM_STEP)
    delta = -ADAM_LR * (m_hat / (jnp.sqrt(v_hat) + ADAM_EPS) + ADAM_WD * w)
    return delta, m, v


def _adam_reduce(name, parts, w, m, v, tr):
    r, cdim = w.shape

    def body(p_ref, w_ref, m_ref, v_ref, g_ref, d_ref, mo_ref, vo_ref):
        g = p_ref[0].astype(F32)
        for s in range(1, N_DEV):
            g = g + p_ref[s].astype(F32)
        d_ref[...], mo_ref[...], vo_ref[...] = _adamw(w_ref[...], g, m_ref[...], v_ref[...])
        g_ref[...] = g

    tile = pl.BlockSpec((tr, cdim), lambda i: (i, 0))
    shp = jax.ShapeDtypeStruct((r, cdim), F32)
    return pl.pallas_call(
        body, name=name, grid=(r // tr,), out_shape=[shp] * 4,
        in_specs=[pl.BlockSpec((N_DEV, tr, cdim), lambda i: (0, i, 0)), tile, tile, tile],
        out_specs=[tile] * 4, compiler_params=_params(("parallel",)),
    )(parts, w, m, v)


def _adam_w_ada(c_all_t, dmod_cols, w, m, v):
    def body(ct_ref, dm_ref, w_ref, m_ref, v_ref, g_ref, d_ref, mo_ref, vo_ref):
        ct = ct_ref[...]
        s = (ct * _sigmoid(ct)).astype(BF16)
        g = _dot(s, dm_ref[...].astype(BF16))
        d_ref[...], mo_ref[...], vo_ref[...] = _adamw(w_ref[...], g, m_ref[...], v_ref[...])
        g_ref[...] = g

    shp = jax.ShapeDtypeStruct(w.shape, F32)
    return pl.pallas_call(
        body, name="adam_w_ada", grid=(1,), out_shape=[shp] * 4,
        in_specs=[_full(c_all_t.shape), _full(dmod_cols.shape)] + [_full(w.shape)] * 3,
        out_specs=[_full(w.shape)] * 4, compiler_params=_params(("arbitrary",)),
    )(c_all_t, dmod_cols, w, m, v)


SMALL_EARLY = ["rel_bias", "attn_sinks", "gmlp_ln_g", "gmlp_ln_b", "gmlp_w_s", "gmlp_b_s",
               "attn_out_g", "gmlp_out_g", "ln1_g", "ln1_b", "ln2_g", "ln2_b"]
SMALL_LATE = ["b_ada", "b_in"]
WEIGHTS = ["rel_bias", "w_ada", "b_ada", "w_in", "b_in", "attn_sinks", "gmlp_ln_g", "gmlp_ln_b", "gmlp_w_s",
           "gmlp_b_s", "attn_out_g", "gmlp_out_g", "w_out", "ln1_g", "ln1_b", "w_gate_up", "w_down", "ln2_g", "ln2_b"]


def _seg_rows(nelem):
    return -(-nelem // 1024) * 8


def _pack(named, names):
    parts = []
    for name in names:
        flat = named[name].reshape(-1).astype(F32)
        rows = _seg_rows(flat.shape[0])
        parts.append(jnp.pad(flat, (0, rows * 128 - flat.shape[0])).reshape(rows, 128))
    return jnp.concatenate(parts, axis=0)


def _unpack(packed, shapes, names):
    out, r0 = {}, 0
    for name in names:
        nelem = math.prod(shapes[name])
        rows = _seg_rows(nelem)
        out[name] = packed[r0:r0 + rows].reshape(-1)[:nelem].reshape(shapes[name])
        r0 += rows
    return out


def _t5_bucket_map():
    qi = jnp.arange(BLK)[:, None]
    si = jnp.arange(2 * BLK)[None, :]
    n = jnp.maximum(qi + BLK - si, 0)
    max_exact = N_BUCKETS // 2
    nf = jnp.maximum(n, max_exact).astype(F32)
    large = max_exact + (jnp.log(nf / max_exact) / math.log(MAX_DISTANCE / max_exact)
                         * (N_BUCKETS - max_exact)).astype(jnp.int32)
    large = jnp.minimum(large, N_BUCKETS - 1)
    return jnp.where(n < max_exact, n, large).astype(jnp.int32)


def kernel(x, c, rel_bias, w_ada, b_ada, w_in, b_in, attn_sinks, gmlp_ln_g, gmlp_ln_b, gmlp_w_s, gmlp_b_s, attn_out_g, gmlp_out_g, w_out, ln1_g, ln1_b, w_gate_up, w_down, ln2_g, ln2_b, loss_target, m_rel_bias, m_w_ada, m_b_ada, m_w_in, m_b_in, m_attn_sinks, m_gmlp_ln_g, m_gmlp_ln_b, m_gmlp_w_s, m_gmlp_b_s, m_attn_out_g, m_gmlp_out_g, m_w_out, m_ln1_g, m_ln1_b, m_w_gate_up, m_w_down, m_ln2_g, m_ln2_b, v_rel_bias, v_w_ada, v_b_ada, v_w_in, v_b_in, v_attn_sinks, v_gmlp_ln_g, v_gmlp_ln_b, v_gmlp_w_s, v_gmlp_b_s, v_attn_out_g, v_gmlp_out_g, v_w_out, v_ln1_g, v_ln1_b, v_w_gate_up, v_w_down, v_ln2_g, v_ln2_b):
    wts = dict(rel_bias=rel_bias, w_ada=w_ada, b_ada=b_ada, w_in=w_in, b_in=b_in, attn_sinks=attn_sinks,
               gmlp_ln_g=gmlp_ln_g, gmlp_ln_b=gmlp_ln_b, gmlp_w_s=gmlp_w_s, gmlp_b_s=gmlp_b_s,
               attn_out_g=attn_out_g, gmlp_out_g=gmlp_out_g, w_out=w_out, ln1_g=ln1_g, ln1_b=ln1_b,
               w_gate_up=w_gate_up, w_down=w_down, ln2_g=ln2_g, ln2_b=ln2_b)
    mom_m = dict(rel_bias=m_rel_bias, w_ada=m_w_ada, b_ada=m_b_ada, w_in=m_w_in, b_in=m_b_in,
                 attn_sinks=m_attn_sinks, gmlp_ln_g=m_gmlp_ln_g, gmlp_ln_b=m_gmlp_ln_b, gmlp_w_s=m_gmlp_w_s,
                 gmlp_b_s=m_gmlp_b_s, attn_out_g=m_attn_out_g, gmlp_out_g=m_gmlp_out_g, w_out=m_w_out,
                 ln1_g=m_ln1_g, ln1_b=m_ln1_b, w_gate_up=m_w_gate_up, w_down=m_w_down, ln2_g=m_ln2_g,
                 ln2_b=m_ln2_b)
    mom_v = dict(rel_bias=v_rel_bias, w_ada=v_w_ada, b_ada=v_b_ada, w_in=v_w_in, b_in=v_b_in,
                 attn_sinks=v_attn_sinks, gmlp_ln_g=v_gmlp_ln_g, gmlp_ln_b=v_gmlp_ln_b, gmlp_w_s=v_gmlp_w_s,
                 gmlp_b_s=v_gmlp_b_s, attn_out_g=v_attn_out_g, gmlp_out_g=v_gmlp_out_g, w_out=v_w_out,
                 ln1_g=v_ln1_g, ln1_b=v_ln1_b, w_gate_up=v_w_gate_up, w_down=v_w_down, ln2_g=v_ln2_g,
                 ln2_b=v_ln2_b)

    t = x.shape[1]
    tm = min(512, t)
    tn_ff = D_FF // 2
    tk_tok = min(1024, t)
    me = 4 * lax.axis_index("x") + 2 * lax.axis_index("y") + lax.axis_index("c")
    xs = x[0]
    target = loss_target[0]

    c_g, w_in_g = _exchange("gather_in", [jnp.broadcast_to(c, (8, D_MODEL)), w_in[0].T.astype(BF16)], (False, False))
    c_all = c_g[:, 0, :]
    w_in_t = w_in_g.reshape(IN_W, D_MODEL)

    ncol = w_ada.shape[2]
    b_cols = lax.dynamic_slice(b_ada, (0, me * ncol), (1, ncol))
    mod_part = _mod_partial(c_all, w_ada[0], b_cols)
    (mod_g,) = _exchange("gather_mod", [mod_part], (False,))
    mod = lax.dynamic_slice(mod_g, (0, me, 0), (N_DEV, 1, ncol)).reshape(1, N_DEV * ncol)
    sh1, sc1, g1, sh2, sc2, g2 = [mod[:, i * D_MODEL:(i + 1) * D_MODEL] for i in range(6)]

    bucket = _t5_bucket_map()
    bias = _bias_table(rel_bias, bucket)
    causal = jnp.tril(jnp.ones((BLK, BLK), dtype=bool))
    ws = jnp.where(causal[None], gmlp_w_s[0], 0.0).astype(BF16)
    ws_t = jnp.swapaxes(ws, 1, 2)
    bfull = jnp.repeat(gmlp_b_s[0].T, GMLP_W // N_GROUPS, axis=1)
    sinks = attn_sinks[0]

    proj, h1 = _inproj(xs, sc1, sh1, w_in_t, b_in, tm)
    (mixed,), (w_out_g, w_gu_g) = _mix_fwd(
        proj, bias, sinks, gmlp_ln_g, gmlp_ln_b, ws, bfull, attn_out_g, gmlp_out_g,
        comm=([w_out[0].astype(BF16), w_gate_up[0].T.astype(BF16)], (False, False)))
    w_out_f = w_out_g.reshape(D_MODEL, D_MODEL)
    w_gu_t = w_gu_g.reshape(2 * D_FF, D_MODEL)
    y1, x1, h2 = _outproj(mixed, w_out_f, xs, g1, ln1_g, ln1_b, sc2, sh2, tm)
    (gate, up, act), (w_down_g,) = _ffn_up(h2, w_gu_t, tm, tn_ff, comm=([w_down[0].astype(BF16)], (False,)))
    w_down_f = w_down_g.reshape(D_FF, D_MODEL)
    dz2, dy2, loss_p, d_ln2g, d_ln2b, d_g2 = _ffn_down(act, w_down_f, x1, target, g2, ln2_g, ln2_b, tm)
    loss = lax.psum(0.5 / D_MODEL * jnp.sum(loss_p), ("x", "y", "c"))

    slots = lambda a: a.reshape(N_DEV, -1, D_MODEL)
    dw_down = _wgrad("wgrad_down", act, dy2, tn_ff, tk_tok)
    (dgate, dup), (r_down,) = _ffn_dact(dy2, w_down_f, gate, up, tm, tn_ff, comm=([slots(dw_down)], (True,)))
    dz1, dy1, d_sc2, d_sh2, d_ln1g, d_ln1b, d_g1 = _ffn_dh2(dgate, dup, w_gu_t, x1, xs, y1, dz2, sc2, g1, ln1_g,
                                                           tm, tn_ff)
    dw_gu_t = jnp.concatenate([_wgrad("wgrad_gate", dgate, h2, tn_ff, tk_tok),
                               _wgrad("wgrad_up", dup, h2, tn_ff, tk_tok)], axis=0)
    dw_out = _wgrad("wgrad_out", mixed, dy1, D_MODEL, tk_tok)
    ((dproj, dkvn, dl_acc, dsink_acc, d_lng, d_lnb, d_ws, d_bs, d_aog, d_gog), (r_gu, r_out)) = _mix_bwd(
        proj, bias, sinks, gmlp_ln_g, gmlp_ln_b, ws, ws_t, bfull, attn_out_g, gmlp_out_g, dy1, w_out_f,
        comm=([slots(dw_gu_t), slots(dw_out)], (True, True)))
    d_relb = _bias_grad(dl_acc, bucket)

    rsum = lambda a: jnp.sum(a, axis=0)
    early_g = dict(
        rel_bias=d_relb[:, 0, :N_BUCKETS].T, attn_sinks=rsum(dsink_acc)[:N_HEADS],
        gmlp_ln_g=rsum(d_lng), gmlp_ln_b=rsum(d_lnb), gmlp_w_s=jnp.where(causal[None], d_ws, 0.0),
        gmlp_b_s=jnp.sum(d_bs.reshape(BLK, N_GROUPS, GMLP_W // N_GROUPS), axis=2).T,
        attn_out_g=rsum(d_aog), gmlp_out_g=rsum(d_gog), ln1_g=rsum(d_ln1g), ln1_b=rsum(d_ln1b),
        ln2_g=rsum(d_ln2g), ln2_b=rsum(d_ln2b))
    (grad_x, dproj_b, d_bin, d_sc1, d_sh1), (early_all,) = _din(
        dproj, dkvn, w_in_t, xs, dz1, sc1, tm, comm=([_pack(early_g, SMALL_EARLY)], (False,)))
    dw_in_t = _wgrad("wgrad_in", dproj_b, h1, IN_W, tk_tok)
    dmod = jnp.concatenate([rsum(d_sh1), rsum(d_sc1), rsum(d_g1), rsum(d_sh2), rsum(d_sc2), rsum(d_g2)])
    late_all, r_in = _exchange("scatter_in", [_pack(dict(b_ada=dmod, b_in=rsum(d_bin)), SMALL_LATE), slots(dw_in_t)],
                               (False, True))

    small = [{}, {}, {}, {}]
    for label, names, parts in (("adam_small_early", SMALL_EARLY, early_all), ("adam_small_late", SMALL_LATE, late_all)):
        res = _adam_reduce(label, parts, _pack(wts, names), _pack(mom_m, names), _pack(mom_v, names), parts.shape[1])
        shapes = {k: wts[k].shape for k in names}
        for i in range(4):
            small[i].update(_unpack(res[i], shapes, names))

    dmod_all = late_all[:, :_seg_rows(6 * D_MODEL), :].reshape(N_DEV, 6 * D_MODEL)
    dmod_cols = lax.dynamic_slice(dmod_all, (0, me * ncol), (N_DEV, ncol))
    kpad = 128 - N_DEV
    ada = _adam_w_ada(jnp.pad(c_all.T, ((0, 0), (0, kpad))), jnp.pad(dmod_cols, ((0, kpad), (0, 0))),
                      w_ada[0], m_w_ada[0], v_w_ada[0])

    tr = lambda a: jnp.swapaxes(a, -1, -2)
    big = {}
    big["w_in"] = [tr(o)[None] for o in _adam_reduce("adam_w_in", r_in, w_in[0].T, m_w_in[0].T, v_w_in[0].T, 112)]
    big["w_out"] = [o[None] for o in _adam_reduce("adam_w_out", r_out, w_out[0], m_w_out[0], v_w_out[0], 128)]
    big["w_gate_up"] = [tr(o)[None] for o in _adam_reduce("adam_w_gu", r_gu, w_gate_up[0].T, m_w_gate_up[0].T,
                                                           v_w_gate_up[0].T, 352)]
    big["w_down"] = [o[None] for o in _adam_reduce("adam_w_down", r_down, w_down[0], m_w_down[0], v_w_down[0], 176)]
    big["w_ada"] = [o[None] for o in ada]

    outs = [[], [], [], []]
    for name in WEIGHTS:
        for i in range(4):
            outs[i].append(big[name][i] if name in big else small[i][name])
    return (loss, grad_x[None], *outs[0], *outs[1], *outs[2], *outs[3])
```

```python
import math

import jax
import jax.numpy as jnp
from jax import lax
from jax.experimental import pallas as pl
from jax.experimental.pallas import tpu as pltpu

F32 = jnp.float32
BF16 = jnp.bfloat16
MESH = pl.DeviceIdType.MESH

N_DEV = 8
D_MODEL = 1024
HEAD_DIM = 64
N_HEADS = 8
N_GROUPS = 8
ATTN_W = 512
KV_W = 128
GMLP_W = 512
IN_W = 1792
BLK = 128
N_BUCKETS = 32
MAX_DISTANCE = 128
D_FF = 2816
ALPHA = 2.0 ** 0.25
LN_EPS = 1e-5
NEG_INF = -1e30
ADAM_LR = 0.001
ADAM_B1 = 0.9
ADAM_B2 = 0.999
ADAM_EPS = 1e-08
ADAM_WD = 0.01
ADAM_STEP = 10
GELU_C0 = math.sqrt(2.0 / math.pi)
GELU_C1 = 0.044715

VMEM_LIMIT = 56 * 1024 * 1024


def _params(sem):
    return pltpu.CompilerParams(dimension_semantics=sem, vmem_limit_bytes=VMEM_LIMIT)


def _dot(a, b):
    return lax.dot_general(a, b, (((1,), (0,)), ((), ())), preferred_element_type=F32)


def _dot_nt(a, b):
    return lax.dot_general(a, b, (((1,), (1,)), ((), ())), preferred_element_type=F32)


def _dot_tn(a, b):
    return lax.dot_general(a, b, (((0,), (0,)), ((), ())), preferred_element_type=F32)


def _full(shape):
    nd = len(shape)
    return pl.BlockSpec(shape, lambda *_: (0,) * nd)


def _rowsum8(v):
    r, c = v.shape
    return jnp.sum(v.reshape(r // 8, 8, c), axis=0)


def _sigmoid(v):
    return 1.0 / (1.0 + jnp.exp(-v))


def _gelu_parts(v):
    v2 = v * v
    t = jnp.tanh(GELU_C0 * (v + GELU_C1 * v * v2))
    g = 0.5 * v * (1.0 + t)
    dg = 0.5 * (1.0 + t) + 0.5 * v * (1.0 - t * t) * (GELU_C0 * (1.0 + 3.0 * GELU_C1 * v2))
    return g, dg


def _ln_stats(z):
    mu = jnp.mean(z, axis=1, keepdims=True)
    zc = z - mu
    var = jnp.mean(zc * zc, axis=1, keepdims=True)
    rstd = lax.rsqrt(var + LN_EPS)
    return zc * rstd, rstd


def _ln_bwd(dxhat, xhat, rstd):
    m1 = jnp.mean(dxhat, axis=1, keepdims=True)
    m2 = jnp.mean(dxhat * xhat, axis=1, keepdims=True)
    return rstd * (dxhat - m1 - xhat * m2)


def _seg_mean64(v):
    r = v.shape[0]
    lo = lax.broadcasted_iota(jnp.int32, (r, 128), 1) < 64
    outs = []
    for j in range(v.shape[1] // 128):
        ch = v[:, 128 * j:128 * (j + 1)]
        s_lo = jnp.sum(jnp.where(lo, ch, 0.0), axis=1, keepdims=True)
        s_hi = jnp.sum(jnp.where(lo, 0.0, ch), axis=1, keepdims=True)
        outs.append(jnp.where(lo, s_lo, s_hi) * (1.0 / 64.0))
    return jnp.concatenate(outs, axis=1)


def _rms(a, g):
    r = lax.rsqrt(jnp.mean(a * a, axis=1, keepdims=True) + LN_EPS)
    return a * r * g, r


def _rms_bwd(dout, a, r, g):
    t = dout * g
    return r * t - a * (r * r * r) * jnp.mean(t * a, axis=1, keepdims=True)


PEER_ORDER = (1, 2, 4, 3, 5, 6, 7)


def _peer(j):
    x, y, c = lax.axis_index("x"), lax.axis_index("y"), lax.axis_index("c")
    px = 1 - x if j & 4 else x
    py = 1 - y if j & 2 else y
    pc = 1 - c if j & 1 else c
    return (px, py, pc), 4 * px + 2 * py + pc


def _exchange_copies(ins, outs, scatter, send_sems, recv_sems, loc_sems):
    me = 4 * lax.axis_index("x") + 2 * lax.axis_index("y") + lax.axis_index("c")
    n = len(ins)

    def src(k, idx):
        return ins[k].at[idx] if scatter[k] else ins[k]

    def remote(k, j, mine):
        dev, idx = _peer(j)
        return pltpu.make_async_remote_copy(
            src_ref=src(k, idx), dst_ref=outs[k].at[me if mine else idx],
            send_sem=send_sems.at[k, j - 1], recv_sem=recv_sems.at[k, j - 1],
            device_id=dev, device_id_type=MESH)

    local = [pltpu.make_async_copy(src(k, me), outs[k].at[me], loc_sems.at[k]) for k in range(n)]
    sends = [remote(k, j, True) for j in PEER_ORDER for k in range(n)]
    arrivals = [remote(k, j, False) for j in PEER_ORDER for k in range(n)]
    return local, sends, arrivals


def _exchange_start(*a):
    local, sends, _ = _exchange_copies(*a)
    for cp in local + sends:
        cp.start()


def _exchange_wait(*a):
    local, sends, arrivals = _exchange_copies(*a)
    for cp in arrivals:
        cp.wait_recv()
    for cp in sends:
        cp.wait_send()
    for cp in local:
        cp.wait()


def _exchange_shapes(arrays, scatter):
    return [jax.ShapeDtypeStruct((N_DEV,) + (a.shape[1:] if s else a.shape), a.dtype) for a, s in zip(arrays, scatter)]


def _exchange_sems(n):
    return [pltpu.SemaphoreType.DMA((n, N_DEV - 1)), pltpu.SemaphoreType.DMA((n, N_DEV - 1)),
            pltpu.SemaphoreType.DMA((n,))]


def _exchange(name, arrays, scatter):
    n = len(arrays)

    def body(*refs):
        a = (refs[:n], refs[n:2 * n], scatter) + tuple(refs[2 * n:])
        _exchange_start(*a)
        _exchange_wait(*a)

    any_spec = pl.BlockSpec(memory_space=pl.ANY)
    return pl.pallas_call(
        body, name=name, out_shape=_exchange_shapes(arrays, scatter),
        in_specs=[any_spec] * n, out_specs=[any_spec] * n, scratch_shapes=_exchange_sems(n),
    )(*arrays)


def _call(body, *, name, grid, in_specs, out_specs, out_shape, args, sem, scratch_shapes=(), comm=None):
    if comm is None:
        outs = pl.pallas_call(body, name=name, grid=grid, in_specs=list(in_specs), out_specs=list(out_specs),
                              out_shape=list(out_shape), scratch_shapes=list(scratch_shapes),
                              compiler_params=_params(sem))(*args)
        return list(outs), []
    arrays, scatter = comm
    n_in, n_out, nc, ns = len(in_specs), len(out_specs), len(arrays), len(scratch_shapes)

    def hosted(*refs):
        ins, cins = refs[:n_in], refs[n_in:n_in + nc]
        outs, couts = refs[n_in + nc:n_in + nc + n_out], refs[n_in + nc + n_out:n_in + 2 * nc + n_out]
        scratch = refs[n_in + 2 * nc + n_out:]
        ex = (cins, couts, scatter) + tuple(scratch[ns:])
        first = pl.program_id(0) == 0
        last = pl.program_id(0) == grid[0] - 1
        for ax in range(1, len(grid)):
            first = first & (pl.program_id(ax) == 0)
            last = last & (pl.program_id(ax) == grid[ax] - 1)

        @pl.when(first)
        def _():
            _exchange_start(*ex)

        body(*ins, *outs, *scratch[:ns])

        @pl.when(last)
        def _():
            _exchange_wait(*ex)

    any_spec = pl.BlockSpec(memory_space=pl.ANY)
    res = pl.pallas_call(
        hosted, name=name, grid=grid, in_specs=list(in_specs) + [any_spec] * nc,
        out_specs=list(out_specs) + [any_spec] * nc, out_shape=list(out_shape) + _exchange_shapes(arrays, scatter),
        scratch_shapes=list(scratch_shapes) + _exchange_sems(nc),
        compiler_params=_params(tuple("arbitrary" for _ in grid)))(*args, *arrays)
    return list(res[:n_out]), list(res[n_out:])


def _mod_partial(c_all, w_ada, b_ada_cols):
    def body(c_ref, w_ref, b_ref, o_ref):
        cv = c_ref[...]
        s = (cv * _sigmoid(cv)).astype(BF16)
        o_ref[...] = _dot(s, w_ref[...].astype(BF16)) + b_ref[...]

    ncol = w_ada.shape[1]
    return pl.pallas_call(
        body, name="mod_partial", out_shape=jax.ShapeDtypeStruct((N_DEV, ncol), F32),
        in_specs=[_full(c_all.shape), _full(w_ada.shape), _full(b_ada_cols.shape)],
        out_specs=_full((N_DEV, ncol)), grid=(1,), compiler_params=_params(("arbitrary",)),
    )(c_all, w_ada, b_ada_cols)


def _bias_table(rel_bias, bucket):
    def body(rb_ref, bk_ref, o_ref):
        h = pl.program_id(0)
        bk = bk_ref[...]
        acc = jnp.zeros((BLK, 2 * BLK), F32)
        for b in range(N_BUCKETS):
            acc = jnp.where(bk == b, rb_ref[b, h], acc)
        dist = (lax.broadcasted_iota(jnp.int32, (BLK, 2 * BLK), 0) + BLK
                - lax.broadcasted_iota(jnp.int32, (BLK, 2 * BLK), 1))
        o_ref[0] = jnp.where((dist >= 0) & (dist < BLK), acc, NEG_INF)

    return pl.pallas_call(
        body, name="bias_table", out_shape=jax.ShapeDtypeStruct((N_HEADS, BLK, 2 * BLK), F32),
        in_specs=[pl.BlockSpec(memory_space=pltpu.SMEM), _full((BLK, 2 * BLK))],
        out_specs=pl.BlockSpec((1, BLK, 2 * BLK), lambda h: (h, 0, 0)), grid=(N_HEADS,),
        compiler_params=_params(("arbitrary",)),
    )(rel_bias, bucket)


def _bias_grad(dl_acc, bucket):
    def body(dl_ref, bk_ref, o_ref):
        bk = bk_ref[...]
        dl = dl_ref[0]
        lane = lax.broadcasted_iota(jnp.int32, (1, 128), 1)
        row = jnp.zeros((1, 128), F32)
        for b in range(N_BUCKETS):
            s = jnp.sum(jnp.sum(jnp.where(bk == b, dl, 0.0), axis=1, keepdims=True), axis=0, keepdims=True)
            row = jnp.where(lane == b, s, row)
        o_ref[0] = row

    return pl.pallas_call(
        body, name="bias_grad", out_shape=jax.ShapeDtypeStruct((N_HEADS, 1, 128), F32),
        in_specs=[pl.BlockSpec((1, BLK, 2 * BLK), lambda h: (h, 0, 0)), _full((BLK, 2 * BLK))],
        out_specs=pl.BlockSpec((1, 1, 128), lambda h: (h, 0, 0)), grid=(N_HEADS,),
        compiler_params=_params(("arbitrary",)),
    )(dl_acc, bucket)


def _inproj(x, sc1, sh1, w_in_t, b_in, tm):
    t, d = x.shape
    n = w_in_t.shape[0]

    def body(x_ref, sc_ref, sh_ref, w_ref, b_ref, proj_ref, h_ref):
        h = (x_ref[...] * (1.0 + sc_ref[...]) + sh_ref[...]).astype(BF16)
        h_ref[...] = h
        proj_ref[...] = _dot_nt(h, w_ref[...]) + b_ref[...]

    row = lambda w: pl.BlockSpec((tm, w), lambda i: (i, 0))
    return pl.pallas_call(
        body, name="inproj", grid=(t // tm,),
        out_shape=[jax.ShapeDtypeStruct((t, n), F32), jax.ShapeDtypeStruct((t, d), BF16)],
        in_specs=[row(d), _full((1, d)), _full((1, d)), _full((n, d)), _full((1, n))],
        out_specs=[row(n), row(d)], compiler_params=_params(("parallel",)),
    )(x, sc1, sh1, w_in_t, b_in)


def _half_masks():
    lo_q = lax.broadcasted_iota(jnp.int32, (BLK, 128), 1) < 64
    lo_k = lax.broadcasted_iota(jnp.int32, (2 * BLK, 128), 1) < 64
    return lo_q, lo_k


def _head_place(h):
    return h // 2, h % 2, h // 4


def _attn_heads(q, kk, vv, bias_ref, sinks_ref, n):
    lo_q, lo_k = _half_masks()
    kkb, kksb = kk.astype(BF16), pltpu.roll(kk, 64, 1).astype(BF16)
    vvb, vvsb = vv.astype(BF16), pltpu.roll(vv, 64, 1).astype(BF16)
    n0mask = (n == 0) & (lax.broadcasted_iota(jnp.int32, (BLK, 2 * BLK), 1) < BLK)
    chunks, probs = [], []
    for j in range(4):
        qc = q[:, 128 * j:128 * (j + 1)]
        acc = jnp.zeros((BLK, 128), F32)
        for pos in range(2):
            h = 2 * j + pos
            direct = (h // 4) == pos
            mq = lo_q if pos == 0 else jnp.logical_not(lo_q)
            mk = lo_k if pos == 0 else jnp.logical_not(lo_k)
            qm = jnp.where(mq, qc, 0.0).astype(BF16)
            logit = _dot_nt(qm, kkb if direct else kksb) * (HEAD_DIM ** -0.5) + bias_ref[h]
            logit = jnp.where(n0mask, NEG_INF, logit)
            sk = sinks_ref[h]
            m = jnp.maximum(jnp.max(logit, axis=1, keepdims=True), sk)
            e = jnp.exp(logit - m)
            es = jnp.exp(sk - m)
            den = jnp.sum(e, axis=1, keepdims=True) + es
            p = e / den
            vm = jnp.where(mk, vvb if direct else vvsb, jnp.zeros_like(vvb))
            acc = acc + _dot(p.astype(BF16), vm)
            probs.append((p, es / den))
        chunks.append(acc)
    return jnp.concatenate(chunks, axis=1), probs


def _gmlp_block(gu, gv, lng, lnb, ws_ref, bfull):
    lo_q, _ = _half_masks()
    u, du = _gelu_parts(gu)
    a, da = _gelu_parts(gv)
    mu = _seg_mean64(a)
    ac = a - mu
    rstd = lax.rsqrt(_seg_mean64(ac * ac) + LN_EPS)
    vhat = ac * rstd
    vn = vhat * lng + lnb
    chunks = []
    for j in range(4):
        vc = vn[:, 128 * j:128 * (j + 1)]
        acc = jnp.zeros((BLK, 128), F32)
        for pos in range(2):
            mq = lo_q if pos == 0 else jnp.logical_not(lo_q)
            acc = acc + _dot(ws_ref[2 * j + pos], jnp.where(mq, vc, 0.0).astype(BF16))
        chunks.append(acc)
    ms = jnp.concatenate(chunks, axis=1) + bfull
    return u * ms, (u, du, da, vhat, rstd, vn, ms)


def _mix_in_specs(nb):
    return [pl.BlockSpec((BLK, IN_W), lambda n: (n, 0)),
            pl.BlockSpec((BLK, 2 * KV_W), lambda n: (jnp.maximum(n - 1, 0), ATTN_W // (2 * KV_W))),
            _full((N_HEADS, BLK, 2 * BLK)),
            pl.BlockSpec(memory_space=pltpu.SMEM),
            _full((1, GMLP_W)), _full((1, GMLP_W)),
            _full((N_GROUPS, BLK, BLK)), _full((BLK, GMLP_W)),
            _full((1, ATTN_W)), _full((1, GMLP_W))]


def _split_proj(proj_ref, kvp_ref):
    q = proj_ref[:, 0:ATTN_W]
    k = proj_ref[:, ATTN_W:ATTN_W + KV_W]
    v = proj_ref[:, ATTN_W + KV_W:ATTN_W + 2 * KV_W]
    gu = proj_ref[:, ATTN_W + 2 * KV_W:ATTN_W + 2 * KV_W + GMLP_W]
    gv = proj_ref[:, ATTN_W + 2 * KV_W + GMLP_W:IN_W]
    kk = jnp.concatenate([kvp_ref[:, 0:KV_W], k], axis=0)
    vv = jnp.concatenate([kvp_ref[:, KV_W:2 * KV_W], v], axis=0)
    return q, kk, vv, gu, gv


def _mix_fwd(proj, bias, sinks, lng, lnb, ws, bfull, aog, gog, comm):
    t = proj.shape[0]
    nb = t // BLK

    def body(proj_ref, kvp_ref, bias_ref, sinks_ref, lng_ref, lnb_ref, ws_ref, bfull_ref, aog_ref, gog_ref, out_ref):
        n = pl.program_id(0)
        q, kk, vv, gu, gv = _split_proj(proj_ref, kvp_ref)
        attn, _ = _attn_heads(q, kk, vv, bias_ref, sinks_ref, n)
        gm, _ = _gmlp_block(gu, gv, lng_ref[...], lnb_ref[...], ws_ref, bfull_ref[...])
        out_ref[:, 0:ATTN_W] = _rms(attn, aog_ref[...])[0].astype(BF16)
        out_ref[:, ATTN_W:ATTN_W + GMLP_W] = _rms(gm, gog_ref[...])[0].astype(BF16)

    return _call(
        body, name="mix_fwd", grid=(nb,), out_shape=[jax.ShapeDtypeStruct((t, D_MODEL), BF16)],
        in_specs=_mix_in_specs(nb), out_specs=[pl.BlockSpec((BLK, D_MODEL), lambda n: (n, 0))],
        sem=("parallel",), comm=comm, args=(proj, proj, bias, sinks, lng, lnb, ws, bfull, aog, gog))


def _mix_bwd(proj, bias, sinks, lng, lnb, ws, ws_t, bfull, aog, gog, dy, w_out, comm):
    t = proj.shape[0]
    nb = t // BLK

    def body(proj_ref, kvp_ref, bias_ref, sinks_ref, lng_ref, lnb_ref, ws_ref, bfull_ref, aog_ref, gog_ref,
             wst_ref, dy_ref, wout_ref,
             dproj_ref, dkvn_ref, dl_ref, dsink_ref, dlng_ref, dlnb_ref, dws_ref, dbs_ref, daog_ref, dgog_ref):
        n = pl.program_id(0)

        @pl.when(n == 0)
        def _():
            for r in (dl_ref, dsink_ref, dlng_ref, dlnb_ref, dws_ref, dbs_ref, daog_ref, dgog_ref):
                r[...] = jnp.zeros_like(r)

        lo_q, lo_k = _half_masks()
        q, kk, vv, gu, gv = _split_proj(proj_ref, kvp_ref)
        dmix = _dot_nt(dy_ref[...], wout_ref[...])
        dma, dmg = dmix[:, 0:ATTN_W], dmix[:, ATTN_W:ATTN_W + GMLP_W]

        attn, probs = _attn_heads(q, kk, vv, bias_ref, sinks_ref, n)
        aog = aog_ref[...]
        _, r_a = _rms(attn, aog)
        daog_ref[...] += _rowsum8(dma * attn * r_a)
        dattn = _rms_bwd(dma, attn, r_a, aog)

        kkb, kksb = kk.astype(BF16), pltpu.roll(kk, 64, 1).astype(BF16)
        vvb, vvsb = vv.astype(BF16), pltpu.roll(vv, 64, 1).astype(BF16)
        lane = lax.broadcasted_iota(jnp.int32, (BLK, 128), 1)
        dk_d = jnp.zeros((2 * BLK, 128), F32)
        dk_s = jnp.zeros((2 * BLK, 128), F32)
        dv_d = jnp.zeros((2 * BLK, 128), F32)
        dv_s = jnp.zeros((2 * BLK, 128), F32)
        dsink = jnp.zeros((BLK, 128), F32)
        dq_chunks = []
        for j in range(4):
            qc = q[:, 128 * j:128 * (j + 1)]
            doc = dattn[:, 128 * j:128 * (j + 1)]
            dq = jnp.zeros((BLK, 128), F32)
            for pos in range(2):
                h = 2 * j + pos
                direct = (h // 4) == pos
                mq = lo_q if pos == 0 else jnp.logical_not(lo_q)
                mk = lo_k if pos == 0 else jnp.logical_not(lo_k)
                p, psink = probs[h]
                qm = jnp.where(mq, qc, 0.0).astype(BF16)
                dom = jnp.where(mq, doc, 0.0).astype(BF16)
                dp = _dot_nt(dom, vvb if direct else vvsb)
                rs = jnp.sum(p * dp, axis=1, keepdims=True)
                dl = p * (dp - rs)
                dl_ref[h] += dl
                dsink = dsink + jnp.where(lane == h, -psink * rs, 0.0)
                dls = (dl * (HEAD_DIM ** -0.5)).astype(BF16)
                km = jnp.where(mk, kkb if direct else kksb, jnp.zeros_like(kkb))
                dq = dq + _dot(dls, km)
                dk_h = _dot_tn(dls, qm)
                dv_h = _dot_tn(p.astype(BF16), dom)
                if direct:
                    dk_d, dv_d = dk_d + dk_h, dv_d + dv_h
                else:
                    dk_s, dv_s = dk_s + dk_h, dv_s + dv_h
            dq_chunks.append(dq)
        dsink_ref[...] += dsink
        dk = dk_d + pltpu.roll(dk_s, 64, 1)
        dv = dv_d + pltpu.roll(dv_s, 64, 1)
        for j in range(4):
            dproj_ref[:, 128 * j:128 * (j + 1)] = dq_chunks[j]
        dproj_ref[:, ATTN_W:ATTN_W + KV_W] = dk[BLK:2 * BLK]
        dproj_ref[:, ATTN_W + KV_W:ATTN_W + 2 * KV_W] = dv[BLK:2 * BLK]
        dkvn_ref[:, 0:KV_W] = dk[0:BLK]
        dkvn_ref[:, KV_W:2 * KV_W] = dv[0:BLK]

        lng = lng_ref[...]
        gog = gog_ref[...]
        gm, (u, du, da, vhat, rstd, vn, ms) = _gmlp_block(gu, gv, lng, lnb_ref[...], ws_ref, bfull_ref[...])
        _, r_g = _rms(gm, gog)
        dgog_ref[...] += _rowsum8(dmg * gm * r_g)
        dgm = _rms_bwd(dmg, gm, r_g, gog)
        dproj_ref[:, ATTN_W + 2 * KV_W:ATTN_W + 2 * KV_W + GMLP_W] = dgm * ms * du
        dms = dgm * u
        dbs_ref[...] += dms
        dvn_chunks = []
        for j in range(4):
            dmc = dms[:, 128 * j:128 * (j + 1)]
            vcb = vn[:, 128 * j:128 * (j + 1)].astype(BF16)
            acc = jnp.zeros((BLK, 128), F32)
            for pos in range(2):
                g = 2 * j + pos
                mq = lo_q if pos == 0 else jnp.logical_not(lo_q)
                dm = jnp.where(mq, dmc, 0.0).astype(BF16)
                dws_ref[g] += _dot_nt(dm, vcb)
                acc = acc + _dot(wst_ref[g], dm)
            dvn_chunks.append(acc)
        dvn = jnp.concatenate(dvn_chunks, axis=1)
        dlng_ref[...] += _rowsum8(dvn * vhat)
        dlnb_ref[...] += _rowsum8(dvn)
        dvh = dvn * lng
        dact = rstd * (dvh - _seg_mean64(dvh) - vhat * _seg_mean64(dvh * vhat))
        dproj_ref[:, ATTN_W + 2 * KV_W + GMLP_W:IN_W] = dact * da

    acc8 = lambda w: jax.ShapeDtypeStruct((8, w), F32)
    out_shape = [jax.ShapeDtypeStruct((t, IN_W), F32), jax.ShapeDtypeStruct((t, 2 * KV_W), F32),
                 jax.ShapeDtypeStruct((N_HEADS, BLK, 2 * BLK), F32), jax.ShapeDtypeStruct((BLK, 128), F32),
                 acc8(GMLP_W), acc8(GMLP_W), jax.ShapeDtypeStruct((N_GROUPS, BLK, BLK), F32),
                 jax.ShapeDtypeStruct((BLK, GMLP_W), F32), acc8(ATTN_W), acc8(GMLP_W)]
    out_specs = [pl.BlockSpec((BLK, IN_W), lambda n: (n, 0)),
                 pl.BlockSpec((BLK, 2 * KV_W), lambda n: ((n + nb - 1) % nb, 0)),
                 _full((N_HEADS, BLK, 2 * BLK)), _full((BLK, 128)), _full((8, GMLP_W)), _full((8, GMLP_W)),
                 _full((N_GROUPS, BLK, BLK)), _full((BLK, GMLP_W)), _full((8, ATTN_W)), _full((8, GMLP_W))]
    in_specs = _mix_in_specs(nb) + [_full((N_GROUPS, BLK, BLK)),
                                    pl.BlockSpec((BLK, D_MODEL), lambda n: (n, 0)),
                                    _full((D_MODEL, D_MODEL))]
    return _call(
        body, name="mix_bwd", grid=(nb,), out_shape=out_shape, in_specs=in_specs, out_specs=out_specs,
        sem=("arbitrary",), comm=comm, args=(proj, proj, bias, sinks, lng, lnb, ws, bfull, aog, gog, ws_t, dy, w_out))


def _outproj(mixed, w_out, x, g1, ln1g, ln1b, sc2, sh2, tm):
    t, d = x.shape

    def body(mx_ref, w_ref, x_ref, g1_ref, lg_ref, lb_ref, sc_ref, sh_ref, y_ref, x1_ref, h2_ref):
        y = _dot(mx_ref[...], w_ref[...])
        xhat, _ = _ln_stats(ALPHA * x_ref[...] + g1_ref[...] * y)
        x1 = xhat * lg_ref[...] + lb_ref[...]
        y_ref[...] = y
        x1_ref[...] = x1
        h2_ref[...] = (x1 * (1.0 + sc_ref[...]) + sh_ref[...]).astype(BF16)

    row = pl.BlockSpec((tm, d), lambda i: (i, 0))
    vec = _full((1, d))
    return pl.pallas_call(
        body, name="outproj", grid=(t // tm,),
        out_shape=[jax.ShapeDtypeStruct((t, d), F32), jax.ShapeDtypeStruct((t, d), F32),
                   jax.ShapeDtypeStruct((t, d), BF16)],
        in_specs=[row, _full((d, d)), row, vec, vec, vec, vec, vec], out_specs=[row, row, row],
        compiler_params=_params(("parallel",)),
    )(mixed, w_out, x, g1, ln1g, ln1b, sc2, sh2)


def _ffn_up(h2, w_gu_t, tm, tn, comm):
    t, d = h2.shape
    nff = D_FF // tn

    def body(h_ref, wg_ref, wu_ref, dsu_ref, sg_ref, act_ref):
        h = h_ref[...]
        g = _dot_nt(h, wg_ref[...])
        u = _dot_nt(h, wu_ref[...])
        s = _sigmoid(g)
        sg = g * s
        dsu_ref[...] = (u * (s * (1.0 + g * (1.0 - s)))).astype(BF16)
        sg_ref[...] = sg.astype(BF16)
        act_ref[...] = (sg * u).astype(BF16)

    out = pl.BlockSpec((tm, tn), lambda j, i: (i, j))
    shp = jax.ShapeDtypeStruct((t, D_FF), BF16)
    return _call(
        body, name="ffn_up", grid=(nff, t // tm), out_shape=[shp, shp, shp],
        in_specs=[pl.BlockSpec((tm, d), lambda j, i: (i, 0)),
                  pl.BlockSpec((tn, d), lambda j, i: (j, 0)),
                  pl.BlockSpec((tn, d), lambda j, i: (j + nff, 0))],
        out_specs=[out, out, out], sem=("parallel", "parallel"), comm=comm, args=(h2, w_gu_t, w_gu_t))


def _ffn_down(act, w_down, x1, target, g2, ln2g, ln2b, tm):
    t, d = x1.shape

    def body(act_ref, w_ref, x1_ref, tg_ref, g2_ref, lg_ref, lb_ref,
             dz_ref, dy_ref, loss_ref, dlg_ref, dlb_ref, dg2_ref):
        @pl.when(pl.program_id(0) == 0)
        def _():
            for r in (loss_ref, dlg_ref, dlb_ref, dg2_ref):
                r[...] = jnp.zeros_like(r)

        y2 = _dot(act_ref[...], w_ref[...])
        g2 = g2_ref[...]
        lg = lg_ref[...]
        xhat, rstd = _ln_stats(ALPHA * x1_ref[...] + g2 * y2)
        err = xhat * lg + lb_ref[...] - tg_ref[...]
        loss_ref[...] += _rowsum8(err * err)
        dx2 = err * (1.0 / d)
        dlg_ref[...] += _rowsum8(dx2 * xhat)
        dlb_ref[...] += _rowsum8(dx2)
        dz = _ln_bwd(dx2 * lg, xhat, rstd)
        dg2_ref[...] += _rowsum8(dz * y2)
        dz_ref[...] = dz
        dy_ref[...] = (g2 * dz).astype(BF16)

    row = pl.BlockSpec((tm, d), lambda i: (i, 0))
    vec = _full((1, d))
    acc = _full((8, d))
    acc_shape = jax.ShapeDtypeStruct((8, d), F32)
    return pl.pallas_call(
        body, name="ffn_down", grid=(t // tm,),
        out_shape=[jax.ShapeDtypeStruct((t, d), F32), jax.ShapeDtypeStruct((t, d), BF16)] + [acc_shape] * 4,
        in_specs=[pl.BlockSpec((tm, D_FF), lambda i: (i, 0)), _full((D_FF, d)), row, row, vec, vec, vec],
        out_specs=[row, row, acc, acc, acc, acc], compiler_params=_params(("arbitrary",)),
    )(act, w_down, x1, target, g2, ln2g, ln2b)


def _ffn_dact(dy2, w_down, dsu, sg, tm, tn, comm):
    t, d = dy2.shape

    def body(dy_ref, w_ref, dsu_ref, sg_ref, dg_ref, du_ref):
        dact = _dot_nt(dy_ref[...], w_ref[...])
        dg_ref[...] = (dact * dsu_ref[...].astype(F32)).astype(BF16)
        du_ref[...] = (dact * sg_ref[...].astype(F32)).astype(BF16)

    tile = pl.BlockSpec((tm, tn), lambda j, i: (i, j))
    shp = jax.ShapeDtypeStruct((t, D_FF), BF16)
    return _call(
        body, name="ffn_dact", grid=(D_FF // tn, t // tm), out_shape=[shp, shp],
        in_specs=[pl.BlockSpec((tm, d), lambda j, i: (i, 0)), pl.BlockSpec((tn, d), lambda j, i: (j, 0)), tile, tile],
        out_specs=[tile, tile], sem=("parallel", "parallel"), comm=comm, args=(dy2, w_down, dsu, sg))


def _ffn_dh2(dgate, dup, w_gu_t, x1, x, y, dz2, sc2, g1, ln1g, tm):
    t, d = x1.shape
    nt = t // tm

    def body(dg_ref, du_ref, w_ref, x1_ref, x_ref, y_ref, dz2_ref, sc_ref, g1_ref, lg_ref,
             dz1_ref, dy_ref, dsc_ref, dsh_ref, dlg_ref, dlb_ref, dg1_ref, acc_ref):
        i = pl.program_id(0)

        @pl.when(i == 0)
        def _():
            for r in (dsc_ref, dsh_ref, dlg_ref, dlb_ref, dg1_ref, acc_ref):
                r[...] = jnp.zeros_like(r)

        slot = i % 2
        dh2 = acc_ref[1 - slot]
        acc_ref[slot] = _dot(dg_ref[...], w_ref[0:D_FF]) + _dot(du_ref[...], w_ref[D_FF:2 * D_FF])

        valid = i > 0

        def add(r, v):
            r[...] += jnp.where(valid, _rowsum8(v), 0.0)

        x1 = x1_ref[...]
        y = y_ref[...]
        g1 = g1_ref[...]
        add(dsc_ref, dh2 * x1)
        add(dsh_ref, dh2)
        dx1 = dh2 * (1.0 + sc_ref[...]) + ALPHA * dz2_ref[...]
        xhat, rstd = _ln_stats(ALPHA * x_ref[...] + g1 * y)
        add(dlg_ref, dx1 * xhat)
        add(dlb_ref, dx1)
        dz1 = _ln_bwd(dx1 * lg_ref[...], xhat, rstd)
        add(dg1_ref, dz1 * y)
        dz1_ref[...] = dz1
        dy_ref[...] = (g1 * dz1).astype(BF16)

    ahead = pl.BlockSpec((tm, D_FF), lambda i: (jnp.minimum(i, nt - 1), 0))
    row = pl.BlockSpec((tm, d), lambda i: (jnp.maximum(i - 1, 0), 0))
    vec = _full((1, d))
    acc = _full((8, d))
    acc_shape = jax.ShapeDtypeStruct((8, d), F32)
    return pl.pallas_call(
        body, name="ffn_dh2", grid=(nt + 1,),
        out_shape=[jax.ShapeDtypeStruct((t, d), F32), jax.ShapeDtypeStruct((t, d), BF16)] + [acc_shape] * 5,
        in_specs=[ahead, ahead, _full((2 * D_FF, d)), row, row, row, row, vec, vec, vec],
        out_specs=[row, row, acc, acc, acc, acc, acc],
        scratch_shapes=[pltpu.VMEM((2, tm, d), F32)], compiler_params=_params(("arbitrary",)),
    )(dgate, dup, w_gu_t, x1, x, y, dz2, sc2, g1, ln1g)


def _din(dproj, dkvn, w_in_t, x, dz1, sc1, tm, comm):
    t, d = x.shape

    def body(dp_ref, dkv_ref, w_ref, x_ref, dz1_ref, sc_ref, dx_ref, dpb_ref, dbin_ref, dsc_ref, dsh_ref):
        @pl.when(pl.program_id(0) == 0)
        def _():
            for r in (dbin_ref, dsc_ref, dsh_ref):
                r[...] = jnp.zeros_like(r)

        dp = jnp.concatenate([dp_ref[:, 0:ATTN_W], dp_ref[:, ATTN_W:ATTN_W + 2 * KV_W] + dkv_ref[...],
                              dp_ref[:, ATTN_W + 2 * KV_W:IN_W]], axis=1)
        dbin_ref[...] += _rowsum8(dp)
        dpb = dp.astype(BF16)
        dpb_ref[...] = dpb
        dh = _dot(dpb, w_ref[...])
        dsc_ref[...] += _rowsum8(dh * x_ref[...])
        dsh_ref[...] += _rowsum8(dh)
        dx_ref[...] = dh * (1.0 + sc_ref[...]) + ALPHA * dz1_ref[...]

    row = lambda w: pl.BlockSpec((tm, w), lambda i: (i, 0))
    return _call(
        body, name="din", grid=(t // tm,),
        out_shape=[jax.ShapeDtypeStruct((t, d), F32), jax.ShapeDtypeStruct((t, IN_W), BF16),
                   jax.ShapeDtypeStruct((8, IN_W), F32), jax.ShapeDtypeStruct((8, d), F32),
                   jax.ShapeDtypeStruct((8, d), F32)],
        in_specs=[row(IN_W), row(2 * KV_W), _full((IN_W, d)), row(d), row(d), _full((1, d))],
        out_specs=[row(d), row(IN_W), _full((8, IN_W)), _full((8, d)), _full((8, d))],
        sem=("arbitrary",), comm=comm, args=(dproj, dkvn, w_in_t, x, dz1, sc1))


def _wgrad(name, a, b, tmm, tk, comm=None):
    t, m = a.shape
    n = b.shape[1]
    nk = t // tk

    def body(a_ref, b_ref, o_ref, acc_ref):
        k = pl.program_id(1)
        part = _dot_tn(a_ref[...], b_ref[...])

        @pl.when(k == 0)
        def _():
            acc_ref[...] = part

        @pl.when(k > 0)
        def _():
            acc_ref[...] += part

        @pl.when(k == nk - 1)
        def _():
            o_ref[...] = acc_ref[...].astype(BF16)

    (out,), got = _call(
        body, name=name, grid=(m // tmm, nk), out_shape=[jax.ShapeDtypeStruct((m, n), BF16)],
        in_specs=[pl.BlockSpec((tk, tmm), lambda i, k: (k, i)), pl.BlockSpec((tk, n), lambda i, k: (k, 0))],
        out_specs=[pl.BlockSpec((tmm, n), lambda i, k: (i, 0))],
        scratch_shapes=[pltpu.VMEM((tmm, n), F32)], sem=("parallel", "arbitrary"), comm=comm, args=(a, b))
    return out if comm is None else (out, got)


def _adamw(w, g, m, v):
    m = ADAM_B1 * m + (1.0 - ADAM_B1) * g
    v = ADAM_B2 * v + (1.0 - ADAM_B2) * (g * g)
    m_hat = m / (1.0 - ADAM_B1 ** ADAM_STEP)
    v_hat = v / (1.0 - ADAM_B2 ** ADAM_STEP)
    delta = -ADAM_LR * (m_hat / (jnp.sqrt(v_hat) + ADAM_EPS) + ADAM_WD * w)
    return delta, m, v


def _adam_reduce(name, parts, w, m, v, tr):
    r, cdim = w.shape

    def body(p_ref, w_ref, m_ref, v_ref, g_ref, d_ref, mo_ref, vo_ref):
        g = p_ref[0].astype(F32)
        for s in range(1, N_DEV):
            g = g + p_ref[s].astype(F32)
        d_ref[...], mo_ref[...], vo_ref[...] = _adamw(w_ref[...], g, m_ref[...], v_ref[...])
        g_ref[...] = g

    tile = pl.BlockSpec((tr, cdim), lambda i: (i, 0))
    shp = jax.ShapeDtypeStruct((r, cdim), F32)
    return pl.pallas_call(
        body, name=name, grid=(r // tr,), out_shape=[shp] * 4,
        in_specs=[pl.BlockSpec((N_DEV, tr, cdim), lambda i: (0, i, 0)), tile, tile, tile],
        out_specs=[tile] * 4, compiler_params=_params(("parallel",)),
    )(parts, w, m, v)


def _adam_w_ada(c_all_t, dmod_cols, w, m, v):
    def body(ct_ref, dm_ref, w_ref, m_ref, v_ref, g_ref, d_ref, mo_ref, vo_ref):
        ct = ct_ref[...]
        s = (ct * _sigmoid(ct)).astype(BF16)
        g = _dot(s, dm_ref[...].astype(BF16))
        d_ref[...], mo_ref[...], vo_ref[...] = _adamw(w_ref[...], g, m_ref[...], v_ref[...])
        g_ref[...] = g

    shp = jax.ShapeDtypeStruct(w.shape, F32)
    return pl.pallas_call(
        body, name="adam_w_ada", grid=(1,), out_shape=[shp] * 4,
        in_specs=[_full(c_all_t.shape), _full(dmod_cols.shape)] + [_full(w.shape)] * 3,
        out_specs=[_full(w.shape)] * 4, compiler_params=_params(("arbitrary",)),
    )(c_all_t, dmod_cols, w, m, v)


SMALL_EARLY = ["rel_bias", "attn_sinks", "gmlp_ln_g", "gmlp_ln_b", "gmlp_w_s", "gmlp_b_s",
               "attn_out_g", "gmlp_out_g", "ln1_g", "ln1_b", "ln2_g", "ln2_b"]
SMALL_LATE = ["b_ada", "b_in"]
WEIGHTS = ["rel_bias", "w_ada", "b_ada", "w_in", "b_in", "attn_sinks", "gmlp_ln_g", "gmlp_ln_b", "gmlp_w_s",
           "gmlp_b_s", "attn_out_g", "gmlp_out_g", "w_out", "ln1_g", "ln1_b", "w_gate_up", "w_down", "ln2_g", "ln2_b"]


def _seg_rows(nelem):
    return -(-nelem // 1024) * 8


def _pack(named, names):
    parts = []
    for name in names:
        flat = named[name].reshape(-1).astype(F32)
        rows = _seg_rows(flat.shape[0])
        parts.append(jnp.pad(flat, (0, rows * 128 - flat.shape[0])).reshape(rows, 128))
    return jnp.concatenate(parts, axis=0)


def _unpack(packed, shapes, names):
    out, r0 = {}, 0
    for name in names:
        nelem = math.prod(shapes[name])
        rows = _seg_rows(nelem)
        out[name] = packed[r0:r0 + rows].reshape(-1)[:nelem].reshape(shapes[name])
        r0 += rows
    return out


def _t5_bucket_map():
    qi = jnp.arange(BLK)[:, None]
    si = jnp.arange(2 * BLK)[None, :]
    n = jnp.maximum(qi + BLK - si, 0)
    max_exact = N_BUCKETS // 2
    nf = jnp.maximum(n, max_exact).astype(F32)
    large = max_exact + (jnp.log(nf / max_exact) / math.log(MAX_DISTANCE / max_exact)
                         * (N_BUCKETS - max_exact)).astype(jnp.int32)
    large = jnp.minimum(large, N_BUCKETS - 1)
    return jnp.where(n < max_exact, n, large).astype(jnp.int32)


def kernel(x, c, rel_bias, w_ada, b_ada, w_in, b_in, attn_sinks, gmlp_ln_g, gmlp_ln_b, gmlp_w_s, gmlp_b_s, attn_out_g, gmlp_out_g, w_out, ln1_g, ln1_b, w_gate_up, w_down, ln2_g, ln2_b, loss_target, m_rel_bias, m_w_ada, m_b_ada, m_w_in, m_b_in, m_attn_sinks, m_gmlp_ln_g, m_gmlp_ln_b, m_gmlp_w_s, m_gmlp_b_s, m_attn_out_g, m_gmlp_out_g, m_w_out, m_ln1_g, m_ln1_b, m_w_gate_up, m_w_down, m_ln2_g, m_ln2_b, v_rel_bias, v_w_ada, v_b_ada, v_w_in, v_b_in, v_attn_sinks, v_gmlp_ln_g, v_gmlp_ln_b, v_gmlp_w_s, v_gmlp_b_s, v_attn_out_g, v_gmlp_out_g, v_w_out, v_ln1_g, v_ln1_b, v_w_gate_up, v_w_down, v_ln2_g, v_ln2_b):
    wts = dict(rel_bias=rel_bias, w_ada=w_ada, b_ada=b_ada, w_in=w_in, b_in=b_in, attn_sinks=attn_sinks,
               gmlp_ln_g=gmlp_ln_g, gmlp_ln_b=gmlp_ln_b, gmlp_w_s=gmlp_w_s, gmlp_b_s=gmlp_b_s,
               attn_out_g=attn_out_g, gmlp_out_g=gmlp_out_g, w_out=w_out, ln1_g=ln1_g, ln1_b=ln1_b,
               w_gate_up=w_gate_up, w_down=w_down, ln2_g=ln2_g, ln2_b=ln2_b)
    mom_m = dict(rel_bias=m_rel_bias, w_ada=m_w_ada, b_ada=m_b_ada, w_in=m_w_in, b_in=m_b_in,
                 attn_sinks=m_attn_sinks, gmlp_ln_g=m_gmlp_ln_g, gmlp_ln_b=m_gmlp_ln_b, gmlp_w_s=m_gmlp_w_s,
                 gmlp_b_s=m_gmlp_b_s, attn_out_g=m_attn_out_g, gmlp_out_g=m_gmlp_out_g, w_out=m_w_out,
                 ln1_g=m_ln1_g, ln1_b=m_ln1_b, w_gate_up=m_w_gate_up, w_down=m_w_down, ln2_g=m_ln2_g,
                 ln2_b=m_ln2_b)
    mom_v = dict(rel_bias=v_rel_bias, w_ada=v_w_ada, b_ada=v_b_ada, w_in=v_w_in, b_in=v_b_in,
                 attn_sinks=v_attn_sinks, gmlp_ln_g=v_gmlp_ln_g, gmlp_ln_b=v_gmlp_ln_b, gmlp_w_s=v_gmlp_w_s,
                 gmlp_b_s=v_gmlp_b_s, attn_out_g=v_attn_out_g, gmlp_out_g=v_gmlp_out_g, w_out=v_w_out,
                 ln1_g=v_ln1_g, ln1_b=v_ln1_b, w_gate_up=v_w_gate_up, w_down=v_w_down, ln2_g=v_ln2_g,
                 ln2_b=v_ln2_b)

    t = x.shape[1]
    tm = min(512, t)
    tn_ff = D_FF // 2
    tk_tok = min(1024, t)
    me = 4 * lax.axis_index("x") + 2 * lax.axis_index("y") + lax.axis_index("c")
    xs = x[0]
    target = loss_target[0]

    c_g, w_in_g = _exchange("gather_in", [jnp.broadcast_to(c, (8, D_MODEL)), w_in[0].T.astype(BF16)], (False, False))
    c_all = c_g[:, 0, :]
    w_in_t = w_in_g.reshape(IN_W, D_MODEL)

    ncol = w_ada.shape[2]
    b_cols = lax.dynamic_slice(b_ada, (0, me * ncol), (1, ncol))
    mod_part = _mod_partial(c_all, w_ada[0], b_cols)
    (mod_g,) = _exchange("gather_mod", [mod_part], (False,))
    mod = lax.dynamic_slice(mod_g, (0, me, 0), (N_DEV, 1, ncol)).reshape(1, N_DEV * ncol)
    sh1, sc1, g1, sh2, sc2, g2 = [mod[:, i * D_MODEL:(i + 1) * D_MODEL] for i in range(6)]

    bucket = _t5_bucket_map()
    bias = _bias_table(rel_bias, bucket)
    causal = jnp.tril(jnp.ones((BLK, BLK), dtype=bool))
    ws = jnp.where(causal[None], gmlp_w_s[0], 0.0).astype(BF16)
    ws_t = jnp.swapaxes(ws, 1, 2)
    bfull = jnp.repeat(gmlp_b_s[0].T, GMLP_W // N_GROUPS, axis=1)
    sinks = attn_sinks[0]

    proj, h1 = _inproj(xs, sc1, sh1, w_in_t, b_in, tm)
    (mixed,), (w_out_g, w_gu_g) = _mix_fwd(
        proj, bias, sinks, gmlp_ln_g, gmlp_ln_b, ws, bfull, attn_out_g, gmlp_out_g,
        comm=([w_out[0].astype(BF16), w_gate_up[0].T.astype(BF16)], (False, False)))
    w_out_f = w_out_g.reshape(D_MODEL, D_MODEL)
    w_gu_t = w_gu_g.reshape(2 * D_FF, D_MODEL)
    y1, x1, h2 = _outproj(mixed, w_out_f, xs, g1, ln1_g, ln1_b, sc2, sh2, tm)
    (dsu, sg, act), (w_down_g,) = _ffn_up(h2, w_gu_t, tm, tn_ff, comm=([w_down[0].astype(BF16)], (False,)))
    w_down_f = w_down_g.reshape(D_FF, D_MODEL)
    dz2, dy2, loss_p, d_ln2g, d_ln2b, d_g2 = _ffn_down(act, w_down_f, x1, target, g2, ln2_g, ln2_b, tm)
    loss = lax.psum(0.5 / D_MODEL * jnp.sum(loss_p), ("x", "y", "c"))

    slots = lambda a: a.reshape(N_DEV, -1, D_MODEL)
    dw_down = _wgrad("wgrad_down", act, dy2, tn_ff, tk_tok)
    (dgate, dup), (r_down,) = _ffn_dact(dy2, w_down_f, dsu, sg, tm, tn_ff, comm=([slots(dw_down)], (True,)))
    dz1, dy1, d_sc2, d_sh2, d_ln1g, d_ln1b, d_g1 = _ffn_dh2(dgate, dup, w_gu_t, x1, xs, y1, dz2, sc2, g1, ln1_g,
                                                           min(256, t))
    dw_gu_t = jnp.concatenate([_wgrad("wgrad_gate", dgate, h2, tn_ff, tk_tok),
                               _wgrad("wgrad_up", dup, h2, tn_ff, tk_tok)], axis=0)
    dw_out = _wgrad("wgrad_out", mixed, dy1, D_MODEL, tk_tok)
    ((dproj, dkvn, dl_acc, dsink_acc, d_lng, d_lnb, d_ws, d_bs, d_aog, d_gog), (r_gu, r_out)) = _mix_bwd(
        proj, bias, sinks, gmlp_ln_g, gmlp_ln_b, ws, ws_t, bfull, attn_out_g, gmlp_out_g, dy1, w_out_f,
        comm=([slots(dw_gu_t), slots(dw_out)], (True, True)))
    d_relb = _bias_grad(dl_acc, bucket)

    rsum = lambda a: jnp.sum(a, axis=0)
    early_g = dict(
        rel_bias=d_relb[:, 0, :N_BUCKETS].T, attn_sinks=rsum(dsink_acc)[:N_HEADS],
        gmlp_ln_g=rsum(d_lng), gmlp_ln_b=rsum(d_lnb), gmlp_w_s=jnp.where(causal[None], d_ws, 0.0),
        gmlp_b_s=jnp.sum(d_bs.reshape(BLK, N_GROUPS, GMLP_W // N_GROUPS), axis=2).T,
        attn_out_g=rsum(d_aog), gmlp_out_g=rsum(d_gog), ln1_g=rsum(d_ln1g), ln1_b=rsum(d_ln1b),
        ln2_g=rsum(d_ln2g), ln2_b=rsum(d_ln2b))
    (grad_x, dproj_b, d_bin, d_sc1, d_sh1), _ = _din(dproj, dkvn, w_in_t, xs, dz1, sc1, tm, comm=None)
    dw_in_t, (early_all,) = _wgrad("wgrad_in", dproj_b, h1, IN_W, tk_tok,
                                   comm=([_pack(early_g, SMALL_EARLY)], (False,)))
    dmod = jnp.concatenate([rsum(d_sh1), rsum(d_sc1), rsum(d_g1), rsum(d_sh2), rsum(d_sc2), rsum(d_g2)])
    late_all, r_in = _exchange("scatter_in", [_pack(dict(b_ada=dmod, b_in=rsum(d_bin)), SMALL_LATE), slots(dw_in_t)],
                               (False, True))

    small = [{}, {}, {}, {}]
    for label, names, parts in (("adam_small_early", SMALL_EARLY, early_all), ("adam_small_late", SMALL_LATE, late_all)):
        res = _adam_reduce(label, parts, _pack(wts, names), _pack(mom_m, names), _pack(mom_v, names), parts.shape[1])
        shapes = {k: wts[k].shape for k in names}
        for i in range(4):
            small[i].update(_unpack(res[i], shapes, names))

    dmod_all = late_all[:, :_seg_rows(6 * D_MODEL), :].reshape(N_DEV, 6 * D_MODEL)
    dmod_cols = lax.dynamic_slice(dmod_all, (0, me * ncol), (N_DEV, ncol))
    kpad = 128 - N_DEV
    ada = _adam_w_ada(jnp.pad(c_all.T, ((0, 0), (0, kpad))), jnp.pad(dmod_cols, ((0, kpad), (0, 0))),
                      w_ada[0], m_w_ada[0], v_w_ada[0])

    tr = lambda a: jnp.swapaxes(a, -1, -2)
    big = {}
    big["w_in"] = [tr(o)[None] for o in _adam_reduce("adam_w_in", r_in, w_in[0].T, m_w_in[0].T, v_w_in[0].T, 112)]
    big["w_out"] = [o[None] for o in _adam_reduce("adam_w_out", r_out, w_out[0], m_w_out[0], v_w_out[0], 128)]
    big["w_gate_up"] = [tr(o)[None] for o in _adam_reduce("adam_w_gu", r_gu, w_gate_up[0].T, m_w_gate_up[0].T,
                                                           v_w_gate_up[0].T, 352)]
    big["w_down"] = [o[None] for o in _adam_reduce("adam_w_down", r_down, w_down[0], m_w_down[0], v_w_down[0], 176)]
    big["w_ada"] = [o[None] for o in ada]

    outs = [[], [], [], []]
    for name in WEIGHTS:
        for i in range(4):
            outs[i].append(big[name][i] if name in big else small[i][name])
    return (loss, grad_x[None], *outs[0], *outs[1], *outs[2], *outs[3])
```

```python
import math

import jax
import jax.numpy as jnp
from jax import lax
from jax.experimental import pallas as pl
from jax.experimental.pallas import tpu as pltpu

F32 = jnp.float32
BF16 = jnp.bfloat16
MESH = pl.DeviceIdType.MESH

N_DEV = 8
D_MODEL = 1024
HEAD_DIM = 64
N_HEADS = 8
N_GROUPS = 8
ATTN_W = 512
KV_W = 128
GMLP_W = 512
IN_W = 1792
BLK = 128
N_BUCKETS = 32
MAX_DISTANCE = 128
D_FF = 2816
ALPHA = 2.0 ** 0.25
LN_EPS = 1e-5
NEG_INF = -1e30
ADAM_LR = 0.001
ADAM_B1 = 0.9
ADAM_B2 = 0.999
ADAM_EPS = 1e-08
ADAM_WD = 0.01
ADAM_STEP = 10
GELU_C0 = math.sqrt(2.0 / math.pi)
GELU_C1 = 0.044715

VMEM_LIMIT = 56 * 1024 * 1024


def _params(sem):
    return pltpu.CompilerParams(dimension_semantics=sem, vmem_limit_bytes=VMEM_LIMIT)


def _dot(a, b):
    return lax.dot_general(a, b, (((1,), (0,)), ((), ())), preferred_element_type=F32)


def _dot_nt(a, b):
    return lax.dot_general(a, b, (((1,), (1,)), ((), ())), preferred_element_type=F32)


def _dot_tn(a, b):
    return lax.dot_general(a, b, (((0,), (0,)), ((), ())), preferred_element_type=F32)


def _full(shape):
    nd = len(shape)
    return pl.BlockSpec(shape, lambda *_: (0,) * nd)


def _rowsum8(v):
    r, c = v.shape
    return jnp.sum(v.reshape(r // 8, 8, c), axis=0)


def _sigmoid(v):
    return 1.0 / (1.0 + jnp.exp(-v))


def _gelu_parts(v):
    v2 = v * v
    t = jnp.tanh(GELU_C0 * (v + GELU_C1 * v * v2))
    g = 0.5 * v * (1.0 + t)
    dg = 0.5 * (1.0 + t) + 0.5 * v * (1.0 - t * t) * (GELU_C0 * (1.0 + 3.0 * GELU_C1 * v2))
    return g, dg


def _ln_stats(z):
    mu = jnp.mean(z, axis=1, keepdims=True)
    zc = z - mu
    var = jnp.mean(zc * zc, axis=1, keepdims=True)
    rstd = lax.rsqrt(var + LN_EPS)
    return zc * rstd, rstd


def _ln_bwd(dxhat, xhat, rstd):
    m1 = jnp.mean(dxhat, axis=1, keepdims=True)
    m2 = jnp.mean(dxhat * xhat, axis=1, keepdims=True)
    return rstd * (dxhat - m1 - xhat * m2)


def _seg_mean64(v):
    r = v.shape[0]
    lo = lax.broadcasted_iota(jnp.int32, (r, 128), 1) < 64
    outs = []
    for j in range(v.shape[1] // 128):
        ch = v[:, 128 * j:128 * (j + 1)]
        s_lo = jnp.sum(jnp.where(lo, ch, 0.0), axis=1, keepdims=True)
        s_hi = jnp.sum(jnp.where(lo, 0.0, ch), axis=1, keepdims=True)
        outs.append(jnp.where(lo, s_lo, s_hi) * (1.0 / 64.0))
    return jnp.concatenate(outs, axis=1)


def _rms(a, g):
    r = lax.rsqrt(jnp.mean(a * a, axis=1, keepdims=True) + LN_EPS)
    return a * r * g, r


def _rms_bwd(dout, a, r, g):
    t = dout * g
    return r * t - a * (r * r * r) * jnp.mean(t * a, axis=1, keepdims=True)


PEER_ORDER = (1, 2, 4, 3, 5, 6, 7)


def _peer(j):
    x, y, c = lax.axis_index("x"), lax.axis_index("y"), lax.axis_index("c")
    px = 1 - x if j & 4 else x
    py = 1 - y if j & 2 else y
    pc = 1 - c if j & 1 else c
    return (px, py, pc), 4 * px + 2 * py + pc


def _exchange_copies(ins, outs, scatter, send_sems, recv_sems, loc_sems):
    me = 4 * lax.axis_index("x") + 2 * lax.axis_index("y") + lax.axis_index("c")
    n = len(ins)

    def src(k, idx):
        return ins[k].at[idx] if scatter[k] else ins[k]

    def remote(k, j, mine):
        dev, idx = _peer(j)
        return pltpu.make_async_remote_copy(
            src_ref=src(k, idx), dst_ref=outs[k].at[me if mine else idx],
            send_sem=send_sems.at[k, j - 1], recv_sem=recv_sems.at[k, j - 1],
            device_id=dev, device_id_type=MESH)

    local = [pltpu.make_async_copy(src(k, me), outs[k].at[me], loc_sems.at[k]) for k in range(n)]
    sends = [remote(k, j, True) for j in PEER_ORDER for k in range(n)]
    arrivals = [remote(k, j, False) for j in PEER_ORDER for k in range(n)]
    return local, sends, arrivals


def _exchange_start(*a):
    local, sends, _ = _exchange_copies(*a)
    for cp in local + sends:
        cp.start()


def _exchange_wait(*a):
    local, sends, arrivals = _exchange_copies(*a)
    for cp in arrivals:
        cp.wait_recv()
    for cp in sends:
        cp.wait_send()
    for cp in local:
        cp.wait()


def _exchange_shapes(arrays, scatter):
    return [jax.ShapeDtypeStruct((N_DEV,) + (a.shape[1:] if s else a.shape), a.dtype) for a, s in zip(arrays, scatter)]


def _exchange_sems(n):
    return [pltpu.SemaphoreType.DMA((n, N_DEV - 1)), pltpu.SemaphoreType.DMA((n, N_DEV - 1)),
            pltpu.SemaphoreType.DMA((n,))]


def _exchange(name, arrays, scatter):
    n = len(arrays)

    def body(*refs):
        a = (refs[:n], refs[n:2 * n], scatter) + tuple(refs[2 * n:])
        _exchange_start(*a)
        _exchange_wait(*a)

    any_spec = pl.BlockSpec(memory_space=pl.ANY)
    return pl.pallas_call(
        body, name=name, out_shape=_exchange_shapes(arrays, scatter),
        in_specs=[any_spec] * n, out_specs=[any_spec] * n, scratch_shapes=_exchange_sems(n),
    )(*arrays)


def _call(body, *, name, grid, in_specs, out_specs, out_shape, args, sem, scratch_shapes=(), comm=None):
    if comm is None:
        outs = pl.pallas_call(body, name=name, grid=grid, in_specs=list(in_specs), out_specs=list(out_specs),
                              out_shape=list(out_shape), scratch_shapes=list(scratch_shapes),
                              compiler_params=_params(sem))(*args)
        return list(outs), []
    arrays, scatter = comm
    n_in, n_out, nc, ns = len(in_specs), len(out_specs), len(arrays), len(scratch_shapes)

    def hosted(*refs):
        ins, cins = refs[:n_in], refs[n_in:n_in + nc]
        outs, couts = refs[n_in + nc:n_in + nc + n_out], refs[n_in + nc + n_out:n_in + 2 * nc + n_out]
        scratch = refs[n_in + 2 * nc + n_out:]
        ex = (cins, couts, scatter) + tuple(scratch[ns:])
        first = pl.program_id(0) == 0
        last = pl.program_id(0) == grid[0] - 1
        for ax in range(1, len(grid)):
            first = first & (pl.program_id(ax) == 0)
            last = last & (pl.program_id(ax) == grid[ax] - 1)

        @pl.when(first)
        def _():
            _exchange_start(*ex)

        body(*ins, *outs, *scratch[:ns])

        @pl.when(last)
        def _():
            _exchange_wait(*ex)

    any_spec = pl.BlockSpec(memory_space=pl.ANY)
    res = pl.pallas_call(
        hosted, name=name, grid=grid, in_specs=list(in_specs) + [any_spec] * nc,
        out_specs=list(out_specs) + [any_spec] * nc, out_shape=list(out_shape) + _exchange_shapes(arrays, scatter),
        scratch_shapes=list(scratch_shapes) + _exchange_sems(nc),
        compiler_params=_params(tuple("arbitrary" for _ in grid)))(*args, *arrays)
    return list(res[:n_out]), list(res[n_out:])


def _mod_partial(c_all, w_ada, b_ada_cols):
    def body(c_ref, w_ref, b_ref, o_ref):
        cv = c_ref[...]
        s = (cv * _sigmoid(cv)).astype(BF16)
        o_ref[...] = _dot(s, w_ref[...].astype(BF16)) + b_ref[...]

    ncol = w_ada.shape[1]
    return pl.pallas_call(
        body, name="mod_partial", out_shape=jax.ShapeDtypeStruct((N_DEV, ncol), F32),
        in_specs=[_full(c_all.shape), _full(w_ada.shape), _full(b_ada_cols.shape)],
        out_specs=_full((N_DEV, ncol)), grid=(1,), compiler_params=_params(("arbitrary",)),
    )(c_all, w_ada, b_ada_cols)


def _bias_table(rel_bias, bucket):
    def body(rb_ref, bk_ref, o_ref):
        h = pl.program_id(0)
        bk = bk_ref[...]
        acc = jnp.zeros((BLK, 2 * BLK), F32)
        for b in range(N_BUCKETS):
            acc = jnp.where(bk == b, rb_ref[b, h], acc)
        dist = (lax.broadcasted_iota(jnp.int32, (BLK, 2 * BLK), 0) + BLK
                - lax.broadcasted_iota(jnp.int32, (BLK, 2 * BLK), 1))
        o_ref[0] = jnp.where((dist >= 0) & (dist < BLK), acc, NEG_INF)

    return pl.pallas_call(
        body, name="bias_table", out_shape=jax.ShapeDtypeStruct((N_HEADS, BLK, 2 * BLK), F32),
        in_specs=[pl.BlockSpec(memory_space=pltpu.SMEM), _full((BLK, 2 * BLK))],
        out_specs=pl.BlockSpec((1, BLK, 2 * BLK), lambda h: (h, 0, 0)), grid=(N_HEADS,),
        compiler_params=_params(("arbitrary",)),
    )(rel_bias, bucket)


def _bias_grad(dl_acc, bucket):
    def body(dl_ref, bk_ref, o_ref):
        bk = bk_ref[...]
        dl = dl_ref[0]
        lane = lax.broadcasted_iota(jnp.int32, (1, 128), 1)
        row = jnp.zeros((1, 128), F32)
        for b in range(N_BUCKETS):
            s = jnp.sum(jnp.sum(jnp.where(bk == b, dl, 0.0), axis=1, keepdims=True), axis=0, keepdims=True)
            row = jnp.where(lane == b, s, row)
        o_ref[0] = row

    return pl.pallas_call(
        body, name="bias_grad", out_shape=jax.ShapeDtypeStruct((N_HEADS, 1, 128), F32),
        in_specs=[pl.BlockSpec((1, BLK, 2 * BLK), lambda h: (h, 0, 0)), _full((BLK, 2 * BLK))],
        out_specs=pl.BlockSpec((1, 1, 128), lambda h: (h, 0, 0)), grid=(N_HEADS,),
        compiler_params=_params(("arbitrary",)),
    )(dl_acc, bucket)


def _inproj(x, sc1, sh1, w_in_t, b_in, tm):
    t, d = x.shape
    n = w_in_t.shape[0]

    def body(x_ref, sc_ref, sh_ref, w_ref, b_ref, proj_ref, h_ref):
        h = (x_ref[...] * (1.0 + sc_ref[...]) + sh_ref[...]).astype(BF16)
        h_ref[...] = h
        proj_ref[...] = _dot_nt(h, w_ref[...]) + b_ref[...]

    row = lambda w: pl.BlockSpec((tm, w), lambda i: (i, 0))
    return pl.pallas_call(
        body, name="inproj", grid=(t // tm,),
        out_shape=[jax.ShapeDtypeStruct((t, n), F32), jax.ShapeDtypeStruct((t, d), BF16)],
        in_specs=[row(d), _full((1, d)), _full((1, d)), _full((n, d)), _full((1, n))],
        out_specs=[row(n), row(d)], compiler_params=_params(("parallel",)),
    )(x, sc1, sh1, w_in_t, b_in)


def _half_masks():
    lo_q = lax.broadcasted_iota(jnp.int32, (BLK, 128), 1) < 64
    lo_k = lax.broadcasted_iota(jnp.int32, (2 * BLK, 128), 1) < 64
    return lo_q, lo_k


def _head_place(h):
    return h // 2, h % 2, h // 4


def _attn_heads(q, kk, vv, bias_ref, sinks_ref, n):
    lo_q, lo_k = _half_masks()
    kkb, kksb = kk.astype(BF16), pltpu.roll(kk, 64, 1).astype(BF16)
    vvb, vvsb = vv.astype(BF16), pltpu.roll(vv, 64, 1).astype(BF16)
    n0mask = (n == 0) & (lax.broadcasted_iota(jnp.int32, (BLK, 2 * BLK), 1) < BLK)
    chunks, probs = [], []
    for j in range(4):
        qc = q[:, 128 * j:128 * (j + 1)]
        acc = jnp.zeros((BLK, 128), F32)
        for pos in range(2):
            h = 2 * j + pos
            direct = (h // 4) == pos
            mq = lo_q if pos == 0 else jnp.logical_not(lo_q)
            mk = lo_k if pos == 0 else jnp.logical_not(lo_k)
            qm = jnp.where(mq, qc, 0.0).astype(BF16)
            logit = _dot_nt(qm, kkb if direct else kksb) * (HEAD_DIM ** -0.5) + bias_ref[h]
            logit = jnp.where(n0mask, NEG_INF, logit)
            sk = sinks_ref[h]
            m = jnp.maximum(jnp.max(logit, axis=1, keepdims=True), sk)
            e = jnp.exp(logit - m)
            es = jnp.exp(sk - m)
            den = jnp.sum(e, axis=1, keepdims=True) + es
            p = e / den
            vm = jnp.where(mk, vvb if direct else vvsb, jnp.zeros_like(vvb))
            acc = acc + _dot(p.astype(BF16), vm)
            probs.append((p, es / den))
        chunks.append(acc)
    return jnp.concatenate(chunks, axis=1), probs


def _gmlp_block(gu, gv, lng, lnb, ws_ref, bfull):
    lo_q, _ = _half_masks()
    u, du = _gelu_parts(gu)
    a, da = _gelu_parts(gv)
    mu = _seg_mean64(a)
    ac = a - mu
    rstd = lax.rsqrt(_seg_mean64(ac * ac) + LN_EPS)
    vhat = ac * rstd
    vn = vhat * lng + lnb
    chunks = []
    for j in range(4):
        vc = vn[:, 128 * j:128 * (j + 1)]
        acc = jnp.zeros((BLK, 128), F32)
        for pos in range(2):
            mq = lo_q if pos == 0 else jnp.logical_not(lo_q)
            acc = acc + _dot(ws_ref[2 * j + pos], jnp.where(mq, vc, 0.0).astype(BF16))
        chunks.append(acc)
    ms = jnp.concatenate(chunks, axis=1) + bfull
    return u * ms, (u, du, da, vhat, rstd, vn, ms)


def _mix_in_specs(nb):
    return [pl.BlockSpec((BLK, IN_W), lambda n: (n, 0)),
            pl.BlockSpec((BLK, 2 * KV_W), lambda n: (jnp.maximum(n - 1, 0), ATTN_W // (2 * KV_W))),
            _full((N_HEADS, BLK, 2 * BLK)),
            pl.BlockSpec(memory_space=pltpu.SMEM),
            _full((1, GMLP_W)), _full((1, GMLP_W)),
            _full((N_GROUPS, BLK, BLK)), _full((BLK, GMLP_W)),
            _full((1, ATTN_W)), _full((1, GMLP_W))]


def _split_proj(proj_ref, kvp_ref):
    q = proj_ref[:, 0:ATTN_W]
    k = proj_ref[:, ATTN_W:ATTN_W + KV_W]
    v = proj_ref[:, ATTN_W + KV_W:ATTN_W + 2 * KV_W]
    gu = proj_ref[:, ATTN_W + 2 * KV_W:ATTN_W + 2 * KV_W + GMLP_W]
    gv = proj_ref[:, ATTN_W + 2 * KV_W + GMLP_W:IN_W]
    kk = jnp.concatenate([kvp_ref[:, 0:KV_W], k], axis=0)
    vv = jnp.concatenate([kvp_ref[:, KV_W:2 * KV_W], v], axis=0)
    return q, kk, vv, gu, gv


def _mix_fwd(proj, bias, sinks, lng, lnb, ws, bfull, aog, gog, comm):
    t = proj.shape[0]
    nb = t // BLK

    def body(proj_ref, kvp_ref, bias_ref, sinks_ref, lng_ref, lnb_ref, ws_ref, bfull_ref, aog_ref, gog_ref, out_ref):
        n = pl.program_id(0)
        q, kk, vv, gu, gv = _split_proj(proj_ref, kvp_ref)
        attn, _ = _attn_heads(q, kk, vv, bias_ref, sinks_ref, n)
        gm, _ = _gmlp_block(gu, gv, lng_ref[...], lnb_ref[...], ws_ref, bfull_ref[...])
        out_ref[:, 0:ATTN_W] = _rms(attn, aog_ref[...])[0].astype(BF16)
        out_ref[:, ATTN_W:ATTN_W + GMLP_W] = _rms(gm, gog_ref[...])[0].astype(BF16)

    return _call(
        body, name="mix_fwd", grid=(nb,), out_shape=[jax.ShapeDtypeStruct((t, D_MODEL), BF16)],
        in_specs=_mix_in_specs(nb), out_specs=[pl.BlockSpec((BLK, D_MODEL), lambda n: (n, 0))],
        sem=("parallel",), comm=comm, args=(proj, proj, bias, sinks, lng, lnb, ws, bfull, aog, gog))


def _mix_bwd(proj, bias, sinks, lng, lnb, ws, ws_t, bfull, aog, gog, dy, w_out, comm):
    t = proj.shape[0]
    nb = t // BLK

    def body(proj_ref, kvp_ref, bias_ref, sinks_ref, lng_ref, lnb_ref, ws_ref, bfull_ref, aog_ref, gog_ref,
             wst_ref, dy_ref, wout_ref,
             dproj_ref, dkvn_ref, dl_ref, dsink_ref, dlng_ref, dlnb_ref, dws_ref, dbs_ref, daog_ref, dgog_ref):
        n = pl.program_id(0)

        @pl.when(n == 0)
        def _():
            for r in (dl_ref, dsink_ref, dlng_ref, dlnb_ref, dws_ref, dbs_ref, daog_ref, dgog_ref):
                r[...] = jnp.zeros_like(r)

        lo_q, lo_k = _half_masks()
        q, kk, vv, gu, gv = _split_proj(proj_ref, kvp_ref)
        dmix = _dot_nt(dy_ref[...], wout_ref[...])
        dma, dmg = dmix[:, 0:ATTN_W], dmix[:, ATTN_W:ATTN_W + GMLP_W]

        attn, probs = _attn_heads(q, kk, vv, bias_ref, sinks_ref, n)
        aog = aog_ref[...]
        _, r_a = _rms(attn, aog)
        daog_ref[...] += _rowsum8(dma * attn * r_a)
        dattn = _rms_bwd(dma, attn, r_a, aog)

        kkb, kksb = kk.astype(BF16), pltpu.roll(kk, 64, 1).astype(BF16)
        vvb, vvsb = vv.astype(BF16), pltpu.roll(vv, 64, 1).astype(BF16)
        lane = lax.broadcasted_iota(jnp.int32, (BLK, 128), 1)
        dk_d = jnp.zeros((2 * BLK, 128), F32)
        dk_s = jnp.zeros((2 * BLK, 128), F32)
        dv_d = jnp.zeros((2 * BLK, 128), F32)
        dv_s = jnp.zeros((2 * BLK, 128), F32)
        dsink = jnp.zeros((BLK, 128), F32)
        dq_chunks = []
        for j in range(4):
            qc = q[:, 128 * j:128 * (j + 1)]
            doc = dattn[:, 128 * j:128 * (j + 1)]
            dq = jnp.zeros((BLK, 128), F32)
            for pos in range(2):
                h = 2 * j + pos
                direct = (h // 4) == pos
                mq = lo_q if pos == 0 else jnp.logical_not(lo_q)
                mk = lo_k if pos == 0 else jnp.logical_not(lo_k)
                p, psink = probs[h]
                qm = jnp.where(mq, qc, 0.0).astype(BF16)
                dom = jnp.where(mq, doc, 0.0).astype(BF16)
                dp = _dot_nt(dom, vvb if direct else vvsb)
                rs = jnp.sum(p * dp, axis=1, keepdims=True)
                dl = p * (dp - rs)
                dl_ref[h] += dl
                dsink = dsink + jnp.where(lane == h, -psink * rs, 0.0)
                dls = (dl * (HEAD_DIM ** -0.5)).astype(BF16)
                km = jnp.where(mk, kkb if direct else kksb, jnp.zeros_like(kkb))
                dq = dq + _dot(dls, km)
                dk_h = _dot_tn(dls, qm)
                dv_h = _dot_tn(p.astype(BF16), dom)
                if direct:
                    dk_d, dv_d = dk_d + dk_h, dv_d + dv_h
                else:
                    dk_s, dv_s = dk_s + dk_h, dv_s + dv_h
            dq_chunks.append(dq)
        dsink_ref[...] += dsink
        dk = dk_d + pltpu.roll(dk_s, 64, 1)
        dv = dv_d + pltpu.roll(dv_s, 64, 1)
        for j in range(4):
            dproj_ref[:, 128 * j:128 * (j + 1)] = dq_chunks[j]
        dproj_ref[:, ATTN_W:ATTN_W + KV_W] = dk[BLK:2 * BLK]
        dproj_ref[:, ATTN_W + KV_W:ATTN_W + 2 * KV_W] = dv[BLK:2 * BLK]
        dkvn_ref[:, 0:KV_W] = dk[0:BLK]
        dkvn_ref[:, KV_W:2 * KV_W] = dv[0:BLK]

        lng = lng_ref[...]
        gog = gog_ref[...]
        gm, (u, du, da, vhat, rstd, vn, ms) = _gmlp_block(gu, gv, lng, lnb_ref[...], ws_ref, bfull_ref[...])
        _, r_g = _rms(gm, gog)
        dgog_ref[...] += _rowsum8(dmg * gm * r_g)
        dgm = _rms_bwd(dmg, gm, r_g, gog)
        dproj_ref[:, ATTN_W + 2 * KV_W:ATTN_W + 2 * KV_W + GMLP_W] = dgm * ms * du
        dms = dgm * u
        dbs_ref[...] += dms
        dvn_chunks = []
        for j in range(4):
            dmc = dms[:, 128 * j:128 * (j + 1)]
            vcb = vn[:, 128 * j:128 * (j + 1)].astype(BF16)
            acc = jnp.zeros((BLK, 128), F32)
            for pos in range(2):
                g = 2 * j + pos
                mq = lo_q if pos == 0 else jnp.logical_not(lo_q)
                dm = jnp.where(mq, dmc, 0.0).astype(BF16)
                dws_ref[g] += _dot_nt(dm, vcb)
                acc = acc + _dot(wst_ref[g], dm)
            dvn_chunks.append(acc)
        dvn = jnp.concatenate(dvn_chunks, axis=1)
        dlng_ref[...] += _rowsum8(dvn * vhat)
        dlnb_ref[...] += _rowsum8(dvn)
        dvh = dvn * lng
        dact = rstd * (dvh - _seg_mean64(dvh) - vhat * _seg_mean64(dvh * vhat))
        dproj_ref[:, ATTN_W + 2 * KV_W + GMLP_W:IN_W] = dact * da

    acc8 = lambda w: jax.ShapeDtypeStruct((8, w), F32)
    out_shape = [jax.ShapeDtypeStruct((t, IN_W), F32), jax.ShapeDtypeStruct((t, 2 * KV_W), F32),
                 jax.ShapeDtypeStruct((N_HEADS, BLK, 2 * BLK), F32), jax.ShapeDtypeStruct((BLK, 128), F32),
                 acc8(GMLP_W), acc8(GMLP_W), jax.ShapeDtypeStruct((N_GROUPS, BLK, BLK), F32),
                 jax.ShapeDtypeStruct((BLK, GMLP_W), F32), acc8(ATTN_W), acc8(GMLP_W)]
    out_specs = [pl.BlockSpec((BLK, IN_W), lambda n: (n, 0)),
                 pl.BlockSpec((BLK, 2 * KV_W), lambda n: ((n + nb - 1) % nb, 0)),
                 _full((N_HEADS, BLK, 2 * BLK)), _full((BLK, 128)), _full((8, GMLP_W)), _full((8, GMLP_W)),
                 _full((N_GROUPS, BLK, BLK)), _full((BLK, GMLP_W)), _full((8, ATTN_W)), _full((8, GMLP_W))]
    in_specs = _mix_in_specs(nb) + [_full((N_GROUPS, BLK, BLK)),
                                    pl.BlockSpec((BLK, D_MODEL), lambda n: (n, 0)),
                                    _full((D_MODEL, D_MODEL))]
    return _call(
        body, name="mix_bwd", grid=(nb,), out_shape=out_shape, in_specs=in_specs, out_specs=out_specs,
        sem=("arbitrary",), comm=comm, args=(proj, proj, bias, sinks, lng, lnb, ws, bfull, aog, gog, ws_t, dy, w_out))


def _outproj(mixed, w_out, x, g1, ln1g, ln1b, sc2, sh2, tm):
    t, d = x.shape

    nt = t // tm

    def body(mx_ref, w_ref, x_ref, g1_ref, lg_ref, lb_ref, sc_ref, sh_ref, y_ref, x1_ref, h2_ref, acc_ref):
        i = pl.program_id(0)

        @pl.when(i == 0)
        def _():
            acc_ref[...] = jnp.zeros_like(acc_ref)

        slot = i % 2
        y = acc_ref[1 - slot]
        acc_ref[slot] = _dot(mx_ref[...], w_ref[...])
        xhat, _ = _ln_stats(ALPHA * x_ref[...] + g1_ref[...] * y)
        x1 = xhat * lg_ref[...] + lb_ref[...]
        y_ref[...] = y
        x1_ref[...] = x1
        h2_ref[...] = (x1 * (1.0 + sc_ref[...]) + sh_ref[...]).astype(BF16)

    ahead = pl.BlockSpec((tm, d), lambda i: (jnp.minimum(i, nt - 1), 0))
    row = pl.BlockSpec((tm, d), lambda i: (jnp.maximum(i - 1, 0), 0))
    vec = _full((1, d))
    return pl.pallas_call(
        body, name="outproj", grid=(nt + 1,),
        out_shape=[jax.ShapeDtypeStruct((t, d), F32), jax.ShapeDtypeStruct((t, d), F32),
                   jax.ShapeDtypeStruct((t, d), BF16)],
        in_specs=[ahead, _full((d, d)), row, vec, vec, vec, vec, vec], out_specs=[row, row, row],
        scratch_shapes=[pltpu.VMEM((2, tm, d), F32)], compiler_params=_params(("arbitrary",)),
    )(mixed, w_out, x, g1, ln1g, ln1b, sc2, sh2)


def _ffn_up(h2, w_gu_t, tm, tn, comm):
    t, d = h2.shape
    nff = D_FF // tn

    nt = t // tm

    def body(h_ref, wg_ref, wu_ref, dsu_ref, sg_ref, act_ref, g_acc, u_acc):
        i = pl.program_id(1)

        @pl.when((pl.program_id(0) == 0) & (i == 0))
        def _():
            g_acc[...] = jnp.zeros_like(g_acc)
            u_acc[...] = jnp.zeros_like(u_acc)

        slot = i % 2
        g = g_acc[1 - slot]
        u = u_acc[1 - slot]
        h = h_ref[...]
        g_acc[slot] = _dot_nt(h, wg_ref[...])
        u_acc[slot] = _dot_nt(h, wu_ref[...])
        s = _sigmoid(g)
        sg = g * s
        dsu_ref[...] = (u * (s * (1.0 + g * (1.0 - s)))).astype(BF16)
        sg_ref[...] = sg.astype(BF16)
        act_ref[...] = (sg * u).astype(BF16)

    out = pl.BlockSpec((tm, tn), lambda j, i: (jnp.maximum(i - 1, 0), j))
    shp = jax.ShapeDtypeStruct((t, D_FF), BF16)
    return _call(
        body, name="ffn_up", grid=(nff, nt + 1), out_shape=[shp, shp, shp],
        in_specs=[pl.BlockSpec((tm, d), lambda j, i: (jnp.minimum(i, nt - 1), 0)),
                  pl.BlockSpec((tn, d), lambda j, i: (j, 0)),
                  pl.BlockSpec((tn, d), lambda j, i: (j + nff, 0))],
        out_specs=[out, out, out], scratch_shapes=[pltpu.VMEM((2, tm, tn), F32), pltpu.VMEM((2, tm, tn), F32)],
        sem=("arbitrary", "arbitrary"), comm=comm, args=(h2, w_gu_t, w_gu_t))


def _ffn_down(act, w_down, x1, target, g2, ln2g, ln2b, tm):
    t, d = x1.shape

    nt = t // tm

    def body(act_ref, w_ref, x1_ref, tg_ref, g2_ref, lg_ref, lb_ref,
             dz_ref, dy_ref, loss_ref, dlg_ref, dlb_ref, dg2_ref, acc_ref):
        i = pl.program_id(0)

        @pl.when(i == 0)
        def _():
            for r in (loss_ref, dlg_ref, dlb_ref, dg2_ref, acc_ref):
                r[...] = jnp.zeros_like(r)

        slot = i % 2
        y2 = acc_ref[1 - slot]
        acc_ref[slot] = _dot(act_ref[...], w_ref[...])
        valid = i > 0

        def add(r, v):
            r[...] += jnp.where(valid, _rowsum8(v), 0.0)

        g2 = g2_ref[...]
        lg = lg_ref[...]
        xhat, rstd = _ln_stats(ALPHA * x1_ref[...] + g2 * y2)
        err = xhat * lg + lb_ref[...] - tg_ref[...]
        add(loss_ref, err * err)
        dx2 = err * (1.0 / d)
        add(dlg_ref, dx2 * xhat)
        add(dlb_ref, dx2)
        dz = _ln_bwd(dx2 * lg, xhat, rstd)
        add(dg2_ref, dz * y2)
        dz_ref[...] = dz
        dy_ref[...] = (g2 * dz).astype(BF16)

    row = pl.BlockSpec((tm, d), lambda i: (jnp.maximum(i - 1, 0), 0))
    vec = _full((1, d))
    acc = _full((8, d))
    acc_shape = jax.ShapeDtypeStruct((8, d), F32)
    return pl.pallas_call(
        body, name="ffn_down", grid=(nt + 1,),
        out_shape=[jax.ShapeDtypeStruct((t, d), F32), jax.ShapeDtypeStruct((t, d), BF16)] + [acc_shape] * 4,
        in_specs=[pl.BlockSpec((tm, D_FF), lambda i: (jnp.minimum(i, nt - 1), 0)), _full((D_FF, d)),
                  row, row, vec, vec, vec],
        out_specs=[row, row, acc, acc, acc, acc],
        scratch_shapes=[pltpu.VMEM((2, tm, d), F32)], compiler_params=_params(("arbitrary",)),
    )(act, w_down, x1, target, g2, ln2g, ln2b)


def _ffn_dact(dy2, w_down, dsu, sg, tm, tn, comm):
    t, d = dy2.shape

    def body(dy_ref, w_ref, dsu_ref, sg_ref, dg_ref, du_ref):
        dact = _dot_nt(dy_ref[...], w_ref[...])
        dg_ref[...] = (dact * dsu_ref[...].astype(F32)).astype(BF16)
        du_ref[...] = (dact * sg_ref[...].astype(F32)).astype(BF16)

    tile = pl.BlockSpec((tm, tn), lambda j, i: (i, j))
    shp = jax.ShapeDtypeStruct((t, D_FF), BF16)
    return _call(
        body, name="ffn_dact", grid=(D_FF // tn, t // tm), out_shape=[shp, shp],
        in_specs=[pl.BlockSpec((tm, d), lambda j, i: (i, 0)), pl.BlockSpec((tn, d), lambda j, i: (j, 0)), tile, tile],
        out_specs=[tile, tile], sem=("parallel", "parallel"), comm=comm, args=(dy2, w_down, dsu, sg))


def _ffn_dh2(dgate, dup, w_gu_t, x1, x, y, dz2, sc2, g1, ln1g, tm):
    t, d = x1.shape
    nt = t // tm

    def body(dg_ref, du_ref, w_ref, x1_ref, x_ref, y_ref, dz2_ref, sc_ref, g1_ref, lg_ref,
             dz1_ref, dy_ref, dsc_ref, dsh_ref, dlg_ref, dlb_ref, dg1_ref, acc_ref):
        i = pl.program_id(0)

        @pl.when(i == 0)
        def _():
            for r in (dsc_ref, dsh_ref, dlg_ref, dlb_ref, dg1_ref, acc_ref):
                r[...] = jnp.zeros_like(r)

        slot = i % 2
        dh2 = acc_ref[1 - slot]
        acc_ref[slot] = _dot(dg_ref[...], w_ref[0:D_FF]) + _dot(du_ref[...], w_ref[D_FF:2 * D_FF])

        valid = i > 0

        def add(r, v):
            r[...] += jnp.where(valid, _rowsum8(v), 0.0)

        x1 = x1_ref[...]
        y = y_ref[...]
        g1 = g1_ref[...]
        add(dsc_ref, dh2 * x1)
        add(dsh_ref, dh2)
        dx1 = dh2 * (1.0 + sc_ref[...]) + ALPHA * dz2_ref[...]
        xhat, rstd = _ln_stats(ALPHA * x_ref[...] + g1 * y)
        add(dlg_ref, dx1 * xhat)
        add(dlb_ref, dx1)
        dz1 = _ln_bwd(dx1 * lg_ref[...], xhat, rstd)
        add(dg1_ref, dz1 * y)
        dz1_ref[...] = dz1
        dy_ref[...] = (g1 * dz1).astype(BF16)

    ahead = pl.BlockSpec((tm, D_FF), lambda i: (jnp.minimum(i, nt - 1), 0))
    row = pl.BlockSpec((tm, d), lambda i: (jnp.maximum(i - 1, 0), 0))
    vec = _full((1, d))
    acc = _full((8, d))
    acc_shape = jax.ShapeDtypeStruct((8, d), F32)
    return pl.pallas_call(
        body, name="ffn_dh2", grid=(nt + 1,),
        out_shape=[jax.ShapeDtypeStruct((t, d), F32), jax.ShapeDtypeStruct((t, d), BF16)] + [acc_shape] * 5,
        in_specs=[ahead, ahead, _full((2 * D_FF, d)), row, row, row, row, vec, vec, vec],
        out_specs=[row, row, acc, acc, acc, acc, acc],
        scratch_shapes=[pltpu.VMEM((2, tm, d), F32)], compiler_params=_params(("arbitrary",)),
    )(dgate, dup, w_gu_t, x1, x, y, dz2, sc2, g1, ln1g)


def _din(dproj, dkvn, w_in_t, x, dz1, sc1, tm, comm):
    t, d = x.shape

    nt = t // tm

    def body(dp_ref, dkv_ref, w_ref, x_ref, dz1_ref, sc_ref, dx_ref, dpb_ref, dbin_ref, dsc_ref, dsh_ref, acc_ref):
        i = pl.program_id(0)

        @pl.when(i == 0)
        def _():
            for r in (dbin_ref, dsc_ref, dsh_ref, acc_ref):
                r[...] = jnp.zeros_like(r)

        slot = i % 2
        dh = acc_ref[1 - slot]
        dp = jnp.concatenate([dp_ref[:, 0:ATTN_W], dp_ref[:, ATTN_W:ATTN_W + 2 * KV_W] + dkv_ref[...],
                              dp_ref[:, ATTN_W + 2 * KV_W:IN_W]], axis=1)
        dbin_ref[...] += jnp.where(i < nt, _rowsum8(dp), 0.0)
        dpb = dp.astype(BF16)
        dpb_ref[...] = dpb
        acc_ref[slot] = _dot(dpb, w_ref[...])
        dsc_ref[...] += jnp.where(i > 0, _rowsum8(dh * x_ref[...]), 0.0)
        dsh_ref[...] += jnp.where(i > 0, _rowsum8(dh), 0.0)
        dx_ref[...] = dh * (1.0 + sc_ref[...]) + ALPHA * dz1_ref[...]

    ahead = lambda w: pl.BlockSpec((tm, w), lambda i: (jnp.minimum(i, nt - 1), 0))
    row = lambda w: pl.BlockSpec((tm, w), lambda i: (jnp.maximum(i - 1, 0), 0))
    return _call(
        body, name="din", grid=(nt + 1,),
        out_shape=[jax.ShapeDtypeStruct((t, d), F32), jax.ShapeDtypeStruct((t, IN_W), BF16),
                   jax.ShapeDtypeStruct((8, IN_W), F32), jax.ShapeDtypeStruct((8, d), F32),
                   jax.ShapeDtypeStruct((8, d), F32)],
        in_specs=[ahead(IN_W), ahead(2 * KV_W), _full((IN_W, d)), row(d), row(d), _full((1, d))],
        out_specs=[row(d), ahead(IN_W), _full((8, IN_W)), _full((8, d)), _full((8, d))],
        scratch_shapes=[pltpu.VMEM((2, tm, d), F32)],
        sem=("arbitrary",), comm=comm, args=(dproj, dkvn, w_in_t, x, dz1, sc1))


def _wgrad(name, a, b, tmm, tk, comm=None, a2=None):
    t, m = a.shape
    n = b.shape[1]
    nk = t // tk
    nm = m // tmm

    def body(*refs):
        a_refs, (b_ref, o_ref, acc_ref) = refs[:-3], refs[-3:]
        i, k = pl.program_id(0), pl.program_id(1)
        a_tile = a_refs[0][...] if a2 is None else jnp.where(i < nm, a_refs[0][...], a_refs[1][...])
        part = _dot_tn(a_tile, b_ref[...])

        @pl.when(k == 0)
        def _():
            acc_ref[...] = part

        @pl.when(k > 0)
        def _():
            acc_ref[...] += part

        @pl.when(k == nk - 1)
        def _():
            o_ref[...] = acc_ref[...].astype(BF16)

    if a2 is None:
        a_specs, a_args, n_tiles = [pl.BlockSpec((tk, tmm), lambda i, k: (k, i))], (a,), nm
    else:
        a_specs = [pl.BlockSpec((tk, tmm), lambda i, k: (jnp.where(i < nm, k, 0), jnp.minimum(i, nm - 1))),
                   pl.BlockSpec((tk, tmm), lambda i, k: (jnp.where(i < nm, 0, k), jnp.maximum(i - nm, 0)))]
        a_args, n_tiles = (a, a2), 2 * nm
    (out,), got = _call(
        body, name=name, grid=(n_tiles, nk), out_shape=[jax.ShapeDtypeStruct((n_tiles * tmm, n), BF16)],
        in_specs=a_specs + [pl.BlockSpec((tk, n), lambda i, k: (k, 0))],
        out_specs=[pl.BlockSpec((tmm, n), lambda i, k: (i, 0))],
        scratch_shapes=[pltpu.VMEM((tmm, n), F32)], sem=("parallel", "arbitrary"), comm=comm, args=a_args + (b,))
    return out if comm is None else (out, got)


def _adamw(w, g, m, v):
    m = ADAM_B1 * m + (1.0 - ADAM_B1) * g
    v = ADAM_B2 * v + (1.0 - ADAM_B2) * (g * g)
    m_hat = m / (1.0 - ADAM_B1 ** ADAM_STEP)
    v_hat = v / (1.0 - ADAM_B2 ** ADAM_STEP)
    delta = -ADAM_LR * (m_hat / (jnp.sqrt(v_hat) + ADAM_EPS) + ADAM_WD * w)
    return delta, m, v


def _adam_reduce(name, parts, w, m, v, tr):
    r, cdim = w.shape

    def body(p_ref, w_ref, m_ref, v_ref, g_ref, d_ref, mo_ref, vo_ref):
        g = p_ref[0].astype(F32)
        for s in range(1, N_DEV):
            g = g + p_ref[s].astype(F32)
        d_ref[...], mo_ref[...], vo_ref[...] = _adamw(w_ref[...], g, m_ref[...], v_ref[...])
        g_ref[...] = g

    tile = pl.BlockSpec((tr, cdim), lambda i: (i, 0))
    shp = jax.ShapeDtypeStruct((r, cdim), F32)
    return pl.pallas_call(
        body, name=name, grid=(r // tr,), out_shape=[shp] * 4,
        in_specs=[pl.BlockSpec((N_DEV, tr, cdim), lambda i: (0, i, 0)), tile, tile, tile],
        out_specs=[tile] * 4, compiler_params=_params(("parallel",)),
    )(parts, w, m, v)


def _adam_w_ada(c_all_t, dmod_cols, w, m, v):
    def body(ct_ref, dm_ref, w_ref, m_ref, v_ref, g_ref, d_ref, mo_ref, vo_ref):
        ct = ct_ref[...]
        s = (ct * _sigmoid(ct)).astype(BF16)
        g = _dot(s, dm_ref[...].astype(BF16))
        d_ref[...], mo_ref[...], vo_ref[...] = _adamw(w_ref[...], g, m_ref[...], v_ref[...])
        g_ref[...] = g

    shp = jax.ShapeDtypeStruct(w.shape, F32)
    return pl.pallas_call(
        body, name="adam_w_ada", grid=(1,), out_shape=[shp] * 4,
        in_specs=[_full(c_all_t.shape), _full(dmod_cols.shape)] + [_full(w.shape)] * 3,
        out_specs=[_full(w.shape)] * 4, compiler_params=_params(("arbitrary",)),
    )(c_all_t, dmod_cols, w, m, v)


SMALL_EARLY = ["rel_bias", "attn_sinks", "gmlp_ln_g", "gmlp_ln_b", "gmlp_w_s", "gmlp_b_s",
               "attn_out_g", "gmlp_out_g", "ln1_g", "ln1_b", "ln2_g", "ln2_b"]
SMALL_LATE = ["b_ada", "b_in"]
WEIGHTS = ["rel_bias", "w_ada", "b_ada", "w_in", "b_in", "attn_sinks", "gmlp_ln_g", "gmlp_ln_b", "gmlp_w_s",
           "gmlp_b_s", "attn_out_g", "gmlp_out_g", "w_out", "ln1_g", "ln1_b", "w_gate_up", "w_down", "ln2_g", "ln2_b"]


def _seg_rows(nelem):
    return -(-nelem // 1024) * 8


def _pack(named, names):
    parts = []
    for name in names:
        flat = named[name].reshape(-1).astype(F32)
        rows = _seg_rows(flat.shape[0])
        parts.append(jnp.pad(flat, (0, rows * 128 - flat.shape[0])).reshape(rows, 128))
    return jnp.concatenate(parts, axis=0)


def _unpack(packed, shapes, names):
    out, r0 = {}, 0
    for name in names:
        nelem = math.prod(shapes[name])
        rows = _seg_rows(nelem)
        out[name] = packed[r0:r0 + rows].reshape(-1)[:nelem].reshape(shapes[name])
        r0 += rows
    return out


def _t5_bucket_map():
    qi = jnp.arange(BLK)[:, None]
    si = jnp.arange(2 * BLK)[None, :]
    n = jnp.maximum(qi + BLK - si, 0)
    max_exact = N_BUCKETS // 2
    nf = jnp.maximum(n, max_exact).astype(F32)
    large = max_exact + (jnp.log(nf / max_exact) / math.log(MAX_DISTANCE / max_exact)
                         * (N_BUCKETS - max_exact)).astype(jnp.int32)
    large = jnp.minimum(large, N_BUCKETS - 1)
    return jnp.where(n < max_exact, n, large).astype(jnp.int32)


def kernel(x, c, rel_bias, w_ada, b_ada, w_in, b_in, attn_sinks, gmlp_ln_g, gmlp_ln_b, gmlp_w_s, gmlp_b_s, attn_out_g, gmlp_out_g, w_out, ln1_g, ln1_b, w_gate_up, w_down, ln2_g, ln2_b, loss_target, m_rel_bias, m_w_ada, m_b_ada, m_w_in, m_b_in, m_attn_sinks, m_gmlp_ln_g, m_gmlp_ln_b, m_gmlp_w_s, m_gmlp_b_s, m_attn_out_g, m_gmlp_out_g, m_w_out, m_ln1_g, m_ln1_b, m_w_gate_up, m_w_down, m_ln2_g, m_ln2_b, v_rel_bias, v_w_ada, v_b_ada, v_w_in, v_b_in, v_attn_sinks, v_gmlp_ln_g, v_gmlp_ln_b, v_gmlp_w_s, v_gmlp_b_s, v_attn_out_g, v_gmlp_out_g, v_w_out, v_ln1_g, v_ln1_b, v_w_gate_up, v_w_down, v_ln2_g, v_ln2_b):
    wts = dict(rel_bias=rel_bias, w_ada=w_ada, b_ada=b_ada, w_in=w_in, b_in=b_in, attn_sinks=attn_sinks,
               gmlp_ln_g=gmlp_ln_g, gmlp_ln_b=gmlp_ln_b, gmlp_w_s=gmlp_w_s, gmlp_b_s=gmlp_b_s,
               attn_out_g=attn_out_g, gmlp_out_g=gmlp_out_g, w_out=w_out, ln1_g=ln1_g, ln1_b=ln1_b,
               w_gate_up=w_gate_up, w_down=w_down, ln2_g=ln2_g, ln2_b=ln2_b)
    mom_m = dict(rel_bias=m_rel_bias, w_ada=m_w_ada, b_ada=m_b_ada, w_in=m_w_in, b_in=m_b_in,
                 attn_sinks=m_attn_sinks, gmlp_ln_g=m_gmlp_ln_g, gmlp_ln_b=m_gmlp_ln_b, gmlp_w_s=m_gmlp_w_s,
                 gmlp_b_s=m_gmlp_b_s, attn_out_g=m_attn_out_g, gmlp_out_g=m_gmlp_out_g, w_out=m_w_out,
                 ln1_g=m_ln1_g, ln1_b=m_ln1_b, w_gate_up=m_w_gate_up, w_down=m_w_down, ln2_g=m_ln2_g,
                 ln2_b=m_ln2_b)
    mom_v = dict(rel_bias=v_rel_bias, w_ada=v_w_ada, b_ada=v_b_ada, w_in=v_w_in, b_in=v_b_in,
                 attn_sinks=v_attn_sinks, gmlp_ln_g=v_gmlp_ln_g, gmlp_ln_b=v_gmlp_ln_b, gmlp_w_s=v_gmlp_w_s,
                 gmlp_b_s=v_gmlp_b_s, attn_out_g=v_attn_out_g, gmlp_out_g=v_gmlp_out_g, w_out=v_w_out,
                 ln1_g=v_ln1_g, ln1_b=v_ln1_b, w_gate_up=v_w_gate_up, w_down=v_w_down, ln2_g=v_ln2_g,
                 ln2_b=v_ln2_b)

    t = x.shape[1]
    tm = min(512, t)
    tn_ff = D_FF // 2
    tk_tok = min(1024, t)
    me = 4 * lax.axis_index("x") + 2 * lax.axis_index("y") + lax.axis_index("c")
    xs = x[0]
    target = loss_target[0]

    c_g, w_in_g = _exchange("gather_in", [jnp.broadcast_to(c, (8, D_MODEL)), w_in[0].T.astype(BF16)], (False, False))
    c_all = c_g[:, 0, :]
    w_in_t = w_in_g.reshape(IN_W, D_MODEL)

    ncol = w_ada.shape[2]
    b_cols = lax.dynamic_slice(b_ada, (0, me * ncol), (1, ncol))
    mod_part = _mod_partial(c_all, w_ada[0], b_cols)
    (mod_g,) = _exchange("gather_mod", [mod_part], (False,))
    mod = lax.dynamic_slice(mod_g, (0, me, 0), (N_DEV, 1, ncol)).reshape(1, N_DEV * ncol)
    sh1, sc1, g1, sh2, sc2, g2 = [mod[:, i * D_MODEL:(i + 1) * D_MODEL] for i in range(6)]

    bucket = _t5_bucket_map()
    bias = _bias_table(rel_bias, bucket)
    causal = jnp.tril(jnp.ones((BLK, BLK), dtype=bool))
    ws = jnp.where(causal[None], gmlp_w_s[0], 0.0).astype(BF16)
    ws_t = jnp.swapaxes(ws, 1, 2)
    bfull = jnp.repeat(gmlp_b_s[0].T, GMLP_W // N_GROUPS, axis=1)
    sinks = attn_sinks[0]

    proj, h1 = _inproj(xs, sc1, sh1, w_in_t, b_in, tm)
    (mixed,), (w_out_g, w_gu_g) = _mix_fwd(
        proj, bias, sinks, gmlp_ln_g, gmlp_ln_b, ws, bfull, attn_out_g, gmlp_out_g,
        comm=([w_out[0].astype(BF16), w_gate_up[0].T.astype(BF16)], (False, False)))
    w_out_f = w_out_g.reshape(D_MODEL, D_MODEL)
    w_gu_t = w_gu_g.reshape(2 * D_FF, D_MODEL)
    y1, x1, h2 = _outproj(mixed, w_out_f, xs, g1, ln1_g, ln1_b, sc2, sh2, tm)
    (dsu, sg, act), (w_down_g,) = _ffn_up(h2, w_gu_t, tm, tn_ff, comm=([w_down[0].astype(BF16)], (False,)))
    w_down_f = w_down_g.reshape(D_FF, D_MODEL)
    dz2, dy2, loss_p, d_ln2g, d_ln2b, d_g2 = _ffn_down(act, w_down_f, x1, target, g2, ln2_g, ln2_b, tm)
    loss = lax.psum(0.5 / D_MODEL * jnp.sum(loss_p), ("x", "y", "c"))

    slots = lambda a: a.reshape(N_DEV, -1, D_MODEL)
    dw_down = _wgrad("wgrad_down", act, dy2, tn_ff, tk_tok)
    (dgate, dup), (r_down,) = _ffn_dact(dy2, w_down_f, dsu, sg, tm, tn_ff, comm=([slots(dw_down)], (True,)))
    dz1, dy1, d_sc2, d_sh2, d_ln1g, d_ln1b, d_g1 = _ffn_dh2(dgate, dup, w_gu_t, x1, xs, y1, dz2, sc2, g1, ln1_g,
                                                           min(256, t))
    dw_gu_t = _wgrad("wgrad_gate_up", dgate, h2, tn_ff, tk_tok, a2=dup)
    dw_out = _wgrad("wgrad_out", mixed, dy1, D_MODEL, tk_tok)
    ((dproj, dkvn, dl_acc, dsink_acc, d_lng, d_lnb, d_ws, d_bs, d_aog, d_gog), (r_gu, r_out)) = _mix_bwd(
        proj, bias, sinks, gmlp_ln_g, gmlp_ln_b, ws, ws_t, bfull, attn_out_g, gmlp_out_g, dy1, w_out_f,
        comm=([slots(dw_gu_t), slots(dw_out)], (True, True)))
    d_relb = _bias_grad(dl_acc, bucket)

    rsum = lambda a: jnp.sum(a, axis=0)
    early_g = dict(
        rel_bias=d_relb[:, 0, :N_BUCKETS].T, attn_sinks=rsum(dsink_acc)[:N_HEADS],
        gmlp_ln_g=rsum(d_lng), gmlp_ln_b=rsum(d_lnb), gmlp_w_s=jnp.where(causal[None], d_ws, 0.0),
        gmlp_b_s=jnp.sum(d_bs.reshape(BLK, N_GROUPS, GMLP_W // N_GROUPS), axis=2).T,
        attn_out_g=rsum(d_aog), gmlp_out_g=rsum(d_gog), ln1_g=rsum(d_ln1g), ln1_b=rsum(d_ln1b),
        ln2_g=rsum(d_ln2g), ln2_b=rsum(d_ln2b))
    (grad_x, dproj_b, d_bin, d_sc1, d_sh1), _ = _din(dproj, dkvn, w_in_t, xs, dz1, sc1, tm, comm=None)
    dw_in_t, (early_all,) = _wgrad("wgrad_in", dproj_b, h1, IN_W, tk_tok,
                                   comm=([_pack(early_g, SMALL_EARLY)], (False,)))
    dmod = jnp.concatenate([rsum(d_sh1), rsum(d_sc1), rsum(d_g1), rsum(d_sh2), rsum(d_sc2), rsum(d_g2)])
    late_all, r_in = _exchange("scatter_in", [_pack(dict(b_ada=dmod, b_in=rsum(d_bin)), SMALL_LATE), slots(dw_in_t)],
                               (False, True))

    small = [{}, {}, {}, {}]
    for label, names, parts in (("adam_small_early", SMALL_EARLY, early_all), ("adam_small_late", SMALL_LATE, late_all)):
        res = _adam_reduce(label, parts, _pack(wts, names), _pack(mom_m, names), _pack(mom_v, names), parts.shape[1])
        shapes = {k: wts[k].shape for k in names}
        for i in range(4):
            small[i].update(_unpack(res[i], shapes, names))

    dmod_all = late_all[:, :_seg_rows(6 * D_MODEL), :].reshape(N_DEV, 6 * D_MODEL)
    dmod_cols = lax.dynamic_slice(dmod_all, (0, me * ncol), (N_DEV, ncol))
    kpad = 128 - N_DEV
    ada = _adam_w_ada(jnp.pad(c_all.T, ((0, 0), (0, kpad))), jnp.pad(dmod_cols, ((0, kpad), (0, 0))),
                      w_ada[0], m_w_ada[0], v_w_ada[0])

    tr = lambda a: jnp.swapaxes(a, -1, -2)
    big = {}
    big["w_in"] = [tr(o)[None] for o in _adam_reduce("adam_w_in", r_in, w_in[0].T, m_w_in[0].T, v_w_in[0].T, 112)]
    big["w_out"] = [o[None] for o in _adam_reduce("adam_w_out", r_out, w_out[0], m_w_out[0], v_w_out[0], 128)]
    big["w_gate_up"] = [tr(o)[None] for o in _adam_reduce("adam_w_gu", r_gu, w_gate_up[0].T, m_w_gate_up[0].T,
                                                           v_w_gate_up[0].T, 352)]
    big["w_down"] = [o[None] for o in _adam_reduce("adam_w_down", r_down, w_down[0], m_w_down[0], v_w_down[0], 176)]
    big["w_ada"] = [o[None] for o in ada]

    outs = [[], [], [], []]
    for name in WEIGHTS:
        for i in range(4):
            outs[i].append(big[name][i] if name in big else small[i][name])
    return (loss, grad_x[None], *outs[0], *outs[1], *outs[2], *outs[3])
```

```python
import math

import jax
import jax.numpy as jnp
from jax import lax
from jax.experimental import pallas as pl
from jax.experimental.pallas import tpu as pltpu

F32 = jnp.float32
BF16 = jnp.bfloat16
MESH = pl.DeviceIdType.MESH

N_DEV = 8
D_MODEL = 1024
HEAD_DIM = 64
N_HEADS = 8
N_GROUPS = 8
ATTN_W = 512
KV_W = 128
GMLP_W = 512
IN_W = 1792
BLK = 128
N_BUCKETS = 32
MAX_DISTANCE = 128
D_FF = 2816
ALPHA = 2.0 ** 0.25
LN_EPS = 1e-5
NEG_INF = -1e30
ADAM_LR = 0.001
ADAM_B1 = 0.9
ADAM_B2 = 0.999
ADAM_EPS = 1e-08
ADAM_WD = 0.01
ADAM_STEP = 10
GELU_C0 = math.sqrt(2.0 / math.pi)
GELU_C1 = 0.044715

VMEM_LIMIT = 56 * 1024 * 1024


def _params(sem):
    return pltpu.CompilerParams(dimension_semantics=sem, vmem_limit_bytes=VMEM_LIMIT)


def _dot(a, b):
    return lax.dot_general(a, b, (((1,), (0,)), ((), ())), preferred_element_type=F32)


def _dot_nt(a, b):
    return lax.dot_general(a, b, (((1,), (1,)), ((), ())), preferred_element_type=F32)


def _dot_tn(a, b):
    return lax.dot_general(a, b, (((0,), (0,)), ((), ())), preferred_element_type=F32)


def _full(shape):
    nd = len(shape)
    return pl.BlockSpec(shape, lambda *_: (0,) * nd)


def _rowsum8(v):
    r, c = v.shape
    return jnp.sum(v.reshape(r // 8, 8, c), axis=0)


def _sigmoid(v):
    return 1.0 / (1.0 + jnp.exp(-v))


def _gelu_parts(v):
    v2 = v * v
    t = jnp.tanh(GELU_C0 * (v + GELU_C1 * v * v2))
    g = 0.5 * v * (1.0 + t)
    dg = 0.5 * (1.0 + t) + 0.5 * v * (1.0 - t * t) * (GELU_C0 * (1.0 + 3.0 * GELU_C1 * v2))
    return g, dg


def _ln_stats(z):
    mu = jnp.mean(z, axis=1, keepdims=True)
    zc = z - mu
    var = jnp.mean(zc * zc, axis=1, keepdims=True)
    rstd = lax.rsqrt(var + LN_EPS)
    return zc * rstd, rstd


def _ln_bwd(dxhat, xhat, rstd):
    m1 = jnp.mean(dxhat, axis=1, keepdims=True)
    m2 = jnp.mean(dxhat * xhat, axis=1, keepdims=True)
    return rstd * (dxhat - m1 - xhat * m2)


def _seg_mean64(v):
    r = v.shape[0]
    lo = lax.broadcasted_iota(jnp.int32, (r, 128), 1) < 64
    outs = []
    for j in range(v.shape[1] // 128):
        ch = v[:, 128 * j:128 * (j + 1)]
        s_lo = jnp.sum(jnp.where(lo, ch, 0.0), axis=1, keepdims=True)
        s_hi = jnp.sum(jnp.where(lo, 0.0, ch), axis=1, keepdims=True)
        outs.append(jnp.where(lo, s_lo, s_hi) * (1.0 / 64.0))
    return jnp.concatenate(outs, axis=1)


def _rms(a, g):
    r = lax.rsqrt(jnp.mean(a * a, axis=1, keepdims=True) + LN_EPS)
    return a * r * g, r


def _rms_bwd(dout, a, r, g):
    t = dout * g
    return r * t - a * (r * r * r) * jnp.mean(t * a, axis=1, keepdims=True)


PEER_ORDER = (1, 2, 4, 3, 5, 6, 7)


def _peer(j):
    x, y, c = lax.axis_index("x"), lax.axis_index("y"), lax.axis_index("c")
    px = 1 - x if j & 4 else x
    py = 1 - y if j & 2 else y
    pc = 1 - c if j & 1 else c
    return (px, py, pc), 4 * px + 2 * py + pc


def _exchange_copies(ins, outs, scatter, send_sems, recv_sems, loc_sems):
    me = 4 * lax.axis_index("x") + 2 * lax.axis_index("y") + lax.axis_index("c")
    n = len(ins)

    def src(k, idx):
        return ins[k].at[idx] if scatter[k] else ins[k]

    def remote(k, j, mine):
        dev, idx = _peer(j)
        return pltpu.make_async_remote_copy(
            src_ref=src(k, idx), dst_ref=outs[k].at[me if mine else idx],
            send_sem=send_sems.at[k, j - 1], recv_sem=recv_sems.at[k, j - 1],
            device_id=dev, device_id_type=MESH)

    local = [pltpu.make_async_copy(src(k, me), outs[k].at[me], loc_sems.at[k]) for k in range(n)]
    sends = [remote(k, j, True) for j in PEER_ORDER for k in range(n)]
    arrivals = [remote(k, j, False) for j in PEER_ORDER for k in range(n)]
    return local, sends, arrivals


def _exchange_start(*a):
    local, sends, _ = _exchange_copies(*a)
    for cp in local + sends:
        cp.start()


def _exchange_wait(*a):
    local, sends, arrivals = _exchange_copies(*a)
    for cp in arrivals:
        cp.wait_recv()
    for cp in sends:
        cp.wait_send()
    for cp in local:
        cp.wait()


def _exchange_shapes(arrays, scatter):
    return [jax.ShapeDtypeStruct((N_DEV,) + (a.shape[1:] if s else a.shape), a.dtype) for a, s in zip(arrays, scatter)]


def _exchange_sems(n):
    return [pltpu.SemaphoreType.DMA((n, N_DEV - 1)), pltpu.SemaphoreType.DMA((n, N_DEV - 1)),
            pltpu.SemaphoreType.DMA((n,))]


def _exchange(name, arrays, scatter):
    n = len(arrays)

    def body(*refs):
        a = (refs[:n], refs[n:2 * n], scatter) + tuple(refs[2 * n:])
        _exchange_start(*a)
        _exchange_wait(*a)

    any_spec = pl.BlockSpec(memory_space=pl.ANY)
    return pl.pallas_call(
        body, name=name, out_shape=_exchange_shapes(arrays, scatter),
        in_specs=[any_spec] * n, out_specs=[any_spec] * n, scratch_shapes=_exchange_sems(n),
    )(*arrays)


def _call(body, *, name, grid, in_specs, out_specs, out_shape, args, sem, scratch_shapes=(), comm=None):
    if comm is None:
        outs = pl.pallas_call(body, name=name, grid=grid, in_specs=list(in_specs), out_specs=list(out_specs),
                              out_shape=list(out_shape), scratch_shapes=list(scratch_shapes),
                              compiler_params=_params(sem))(*args)
        return list(outs), []
    arrays, scatter = comm
    n_in, n_out, nc, ns = len(in_specs), len(out_specs), len(arrays), len(scratch_shapes)

    def hosted(*refs):
        ins, cins = refs[:n_in], refs[n_in:n_in + nc]
        outs, couts = refs[n_in + nc:n_in + nc + n_out], refs[n_in + nc + n_out:n_in + 2 * nc + n_out]
        scratch = refs[n_in + 2 * nc + n_out:]
        ex = (cins, couts, scatter) + tuple(scratch[ns:])
        first = pl.program_id(0) == 0
        last = pl.program_id(0) == grid[0] - 1
        for ax in range(1, len(grid)):
            first = first & (pl.program_id(ax) == 0)
            last = last & (pl.program_id(ax) == grid[ax] - 1)

        @pl.when(first)
        def _():
            _exchange_start(*ex)

        body(*ins, *outs, *scratch[:ns])

        @pl.when(last)
        def _():
            _exchange_wait(*ex)

    any_spec = pl.BlockSpec(memory_space=pl.ANY)
    res = pl.pallas_call(
        hosted, name=name, grid=grid, in_specs=list(in_specs) + [any_spec] * nc,
        out_specs=list(out_specs) + [any_spec] * nc, out_shape=list(out_shape) + _exchange_shapes(arrays, scatter),
        scratch_shapes=list(scratch_shapes) + _exchange_sems(nc),
        compiler_params=_params(tuple("arbitrary" for _ in grid)))(*args, *arrays)
    return list(res[:n_out]), list(res[n_out:])


def _mod_partial(c_all, w_ada, b_ada_cols):
    def body(c_ref, w_ref, b_ref, o_ref):
        cv = c_ref[...]
        s = (cv * _sigmoid(cv)).astype(BF16)
        o_ref[...] = _dot(s, w_ref[...].astype(BF16)) + b_ref[...]

    ncol = w_ada.shape[1]
    return pl.pallas_call(
        body, name="mod_partial", out_shape=jax.ShapeDtypeStruct((N_DEV, ncol), F32),
        in_specs=[_full(c_all.shape), _full(w_ada.shape), _full(b_ada_cols.shape)],
        out_specs=_full((N_DEV, ncol)), grid=(1,), compiler_params=_params(("arbitrary",)),
    )(c_all, w_ada, b_ada_cols)


def _bias_table(rel_bias, bucket):
    def body(rb_ref, bk_ref, o_ref):
        h = pl.program_id(0)
        bk = bk_ref[...]
        acc = jnp.zeros((BLK, 2 * BLK), F32)
        for b in range(N_BUCKETS):
            acc = jnp.where(bk == b, rb_ref[b, h], acc)
        dist = (lax.broadcasted_iota(jnp.int32, (BLK, 2 * BLK), 0) + BLK
                - lax.broadcasted_iota(jnp.int32, (BLK, 2 * BLK), 1))
        o_ref[0] = jnp.where((dist >= 0) & (dist < BLK), acc, NEG_INF)

    return pl.pallas_call(
        body, name="bias_table", out_shape=jax.ShapeDtypeStruct((N_HEADS, BLK, 2 * BLK), F32),
        in_specs=[pl.BlockSpec(memory_space=pltpu.SMEM), _full((BLK, 2 * BLK))],
        out_specs=pl.BlockSpec((1, BLK, 2 * BLK), lambda h: (h, 0, 0)), grid=(N_HEADS,),
        compiler_params=_params(("arbitrary",)),
    )(rel_bias, bucket)


def _bias_grad(dl_acc, bucket):
    def body(dl_ref, bk_ref, o_ref):
        bk = bk_ref[...]
        dl = dl_ref[0]
        lane = lax.broadcasted_iota(jnp.int32, (1, 128), 1)
        row = jnp.zeros((1, 128), F32)
        for b in range(N_BUCKETS):
            s = jnp.sum(jnp.sum(jnp.where(bk == b, dl, 0.0), axis=1, keepdims=True), axis=0, keepdims=True)
            row = jnp.where(lane == b, s, row)
        o_ref[0] = row

    return pl.pallas_call(
        body, name="bias_grad", out_shape=jax.ShapeDtypeStruct((N_HEADS, 1, 128), F32),
        in_specs=[pl.BlockSpec((1, BLK, 2 * BLK), lambda h: (h, 0, 0)), _full((BLK, 2 * BLK))],
        out_specs=pl.BlockSpec((1, 1, 128), lambda h: (h, 0, 0)), grid=(N_HEADS,),
        compiler_params=_params(("arbitrary",)),
    )(dl_acc, bucket)


def _inproj(x, sc1, sh1, w_in_t, b_in, tm):
    t, d = x.shape
    n = w_in_t.shape[0]

    def body(x_ref, sc_ref, sh_ref, w_ref, b_ref, proj_ref, h_ref):
        h = (x_ref[...] * (1.0 + sc_ref[...]) + sh_ref[...]).astype(BF16)
        h_ref[...] = h
        proj_ref[...] = _dot_nt(h, w_ref[...]) + b_ref[...]

    row = lambda w: pl.BlockSpec((tm, w), lambda i: (i, 0))
    return pl.pallas_call(
        body, name="inproj", grid=(t // tm,),
        out_shape=[jax.ShapeDtypeStruct((t, n), F32), jax.ShapeDtypeStruct((t, d), BF16)],
        in_specs=[row(d), _full((1, d)), _full((1, d)), _full((n, d)), _full((1, n))],
        out_specs=[row(n), row(d)], compiler_params=_params(("parallel",)),
    )(x, sc1, sh1, w_in_t, b_in)


def _half_masks():
    lo_q = lax.broadcasted_iota(jnp.int32, (BLK, 128), 1) < 64
    lo_k = lax.broadcasted_iota(jnp.int32, (2 * BLK, 128), 1) < 64
    return lo_q, lo_k


def _head_place(h):
    return h // 2, h % 2, h // 4


def _attn_heads(q, kk, vv, bias_ref, sinks_ref, n):
    lo_q, lo_k = _half_masks()
    kkb, kksb = kk.astype(BF16), pltpu.roll(kk, 64, 1).astype(BF16)
    vvb, vvsb = vv.astype(BF16), pltpu.roll(vv, 64, 1).astype(BF16)
    n0mask = (n == 0) & (lax.broadcasted_iota(jnp.int32, (BLK, 2 * BLK), 1) < BLK)
    chunks, probs = [], []
    for j in range(4):
        qc = q[:, 128 * j:128 * (j + 1)]
        acc = jnp.zeros((BLK, 128), F32)
        for pos in range(2):
            h = 2 * j + pos
            direct = (h // 4) == pos
            mq = lo_q if pos == 0 else jnp.logical_not(lo_q)
            mk = lo_k if pos == 0 else jnp.logical_not(lo_k)
            qm = jnp.where(mq, qc, 0.0).astype(BF16)
            logit = _dot_nt(qm, kkb if direct else kksb) * (HEAD_DIM ** -0.5) + bias_ref[h]
            logit = jnp.where(n0mask, NEG_INF, logit)
            sk = sinks_ref[h]
            m = jnp.maximum(jnp.max(logit, axis=1, keepdims=True), sk)
            e = jnp.exp(logit - m)
            es = jnp.exp(sk - m)
            den = jnp.sum(e, axis=1, keepdims=True) + es
            p = e / den
            vm = jnp.where(mk, vvb if direct else vvsb, jnp.zeros_like(vvb))
            acc = acc + _dot(p.astype(BF16), vm)
            probs.append((p, es / den))
        chunks.append(acc)
    return jnp.concatenate(chunks, axis=1), probs


def _gmlp_block(gu, gv, lng, lnb, ws_ref, bfull):
    lo_q, _ = _half_masks()
    u, du = _gelu_parts(gu)
    a, da = _gelu_parts(gv)
    mu = _seg_mean64(a)
    ac = a - mu
    rstd = lax.rsqrt(_seg_mean64(ac * ac) + LN_EPS)
    vhat = ac * rstd
    vn = vhat * lng + lnb
    chunks = []
    for j in range(4):
        vc = vn[:, 128 * j:128 * (j + 1)]
        acc = jnp.zeros((BLK, 128), F32)
        for pos in range(2):
            mq = lo_q if pos == 0 else jnp.logical_not(lo_q)
            acc = acc + _dot(ws_ref[2 * j + pos], jnp.where(mq, vc, 0.0).astype(BF16))
        chunks.append(acc)
    ms = jnp.concatenate(chunks, axis=1) + bfull
    return u * ms, (u, du, da, vhat, rstd, vn, ms)


def _mix_in_specs(nb):
    return [pl.BlockSpec((BLK, IN_W), lambda n: (n, 0)),
            pl.BlockSpec((BLK, 2 * KV_W), lambda n: (jnp.maximum(n - 1, 0), ATTN_W // (2 * KV_W))),
            _full((N_HEADS, BLK, 2 * BLK)),
            pl.BlockSpec(memory_space=pltpu.SMEM),
            _full((1, GMLP_W)), _full((1, GMLP_W)),
            _full((N_GROUPS, BLK, BLK)), _full((BLK, GMLP_W)),
            _full((1, ATTN_W)), _full((1, GMLP_W))]


def _split_proj(proj_ref, kvp_ref):
    q = proj_ref[:, 0:ATTN_W]
    k = proj_ref[:, ATTN_W:ATTN_W + KV_W]
    v = proj_ref[:, ATTN_W + KV_W:ATTN_W + 2 * KV_W]
    gu = proj_ref[:, ATTN_W + 2 * KV_W:ATTN_W + 2 * KV_W + GMLP_W]
    gv = proj_ref[:, ATTN_W + 2 * KV_W + GMLP_W:IN_W]
    kk = jnp.concatenate([kvp_ref[:, 0:KV_W], k], axis=0)
    vv = jnp.concatenate([kvp_ref[:, KV_W:2 * KV_W], v], axis=0)
    return q, kk, vv, gu, gv


def _mix_fwd(proj, bias, sinks, lng, lnb, ws, bfull, aog, gog, comm):
    t = proj.shape[0]
    nb = t // BLK

    def body(proj_ref, kvp_ref, bias_ref, sinks_ref, lng_ref, lnb_ref, ws_ref, bfull_ref, aog_ref, gog_ref, out_ref):
        n = pl.program_id(0)
        q, kk, vv, gu, gv = _split_proj(proj_ref, kvp_ref)
        attn, _ = _attn_heads(q, kk, vv, bias_ref, sinks_ref, n)
        gm, _ = _gmlp_block(gu, gv, lng_ref[...], lnb_ref[...], ws_ref, bfull_ref[...])
        out_ref[:, 0:ATTN_W] = _rms(attn, aog_ref[...])[0].astype(BF16)
        out_ref[:, ATTN_W:ATTN_W + GMLP_W] = _rms(gm, gog_ref[...])[0].astype(BF16)

    return _call(
        body, name="mix_fwd", grid=(nb,), out_shape=[jax.ShapeDtypeStruct((t, D_MODEL), BF16)],
        in_specs=_mix_in_specs(nb), out_specs=[pl.BlockSpec((BLK, D_MODEL), lambda n: (n, 0))],
        sem=("parallel",), comm=comm, args=(proj, proj, bias, sinks, lng, lnb, ws, bfull, aog, gog))


def _mix_bwd(proj, bias, sinks, lng, lnb, ws, ws_t, bfull, aog, gog, dy, w_out, comm):
    t = proj.shape[0]
    nb = t // BLK

    def body(proj_ref, kvp_ref, bias_ref, sinks_ref, lng_ref, lnb_ref, ws_ref, bfull_ref, aog_ref, gog_ref,
             wst_ref, dy_ref, wout_ref,
             dproj_ref, dkvn_ref, dl_ref, dsink_ref, dlng_ref, dlnb_ref, dws_ref, dbs_ref, daog_ref, dgog_ref):
        n = pl.program_id(0)

        @pl.when(n == 0)
        def _():
            for r in (dl_ref, dsink_ref, dlng_ref, dlnb_ref, dws_ref, dbs_ref, daog_ref, dgog_ref):
                r[...] = jnp.zeros_like(r)

        lo_q, lo_k = _half_masks()
        q, kk, vv, gu, gv = _split_proj(proj_ref, kvp_ref)
        dmix = _dot_nt(dy_ref[...], wout_ref[...])
        dma, dmg = dmix[:, 0:ATTN_W], dmix[:, ATTN_W:ATTN_W + GMLP_W]

        attn, probs = _attn_heads(q, kk, vv, bias_ref, sinks_ref, n)
        aog = aog_ref[...]
        _, r_a = _rms(attn, aog)
        daog_ref[...] += _rowsum8(dma * attn * r_a)
        dattn = _rms_bwd(dma, attn, r_a, aog)

        kkb, kksb = kk.astype(BF16), pltpu.roll(kk, 64, 1).astype(BF16)
        vvb, vvsb = vv.astype(BF16), pltpu.roll(vv, 64, 1).astype(BF16)
        lane = lax.broadcasted_iota(jnp.int32, (BLK, 128), 1)
        dk_d = jnp.zeros((2 * BLK, 128), F32)
        dk_s = jnp.zeros((2 * BLK, 128), F32)
        dv_d = jnp.zeros((2 * BLK, 128), F32)
        dv_s = jnp.zeros((2 * BLK, 128), F32)
        dsink = jnp.zeros((BLK, 128), F32)
        dq_chunks = []
        for j in range(4):
            qc = q[:, 128 * j:128 * (j + 1)]
            doc = dattn[:, 128 * j:128 * (j + 1)]
            dq = jnp.zeros((BLK, 128), F32)
            for pos in range(2):
                h = 2 * j + pos
                direct = (h // 4) == pos
                mq = lo_q if pos == 0 else jnp.logical_not(lo_q)
                mk = lo_k if pos == 0 else jnp.logical_not(lo_k)
                p, psink = probs[h]
                qm = jnp.where(mq, qc, 0.0).astype(BF16)
                dom = jnp.where(mq, doc, 0.0).astype(BF16)
                dp = _dot_nt(dom, vvb if direct else vvsb)
                rs = jnp.sum(p * dp, axis=1, keepdims=True)
                dl = p * (dp - rs)
                dl_ref[h] += dl
                dsink = dsink + jnp.where(lane == h, -psink * rs, 0.0)
                dls = (dl * (HEAD_DIM ** -0.5)).astype(BF16)
                km = jnp.where(mk, kkb if direct else kksb, jnp.zeros_like(kkb))
                dq = dq + _dot(dls, km)
                dk_h = _dot_tn(dls, qm)
                dv_h = _dot_tn(p.astype(BF16), dom)
                if direct:
                    dk_d, dv_d = dk_d + dk_h, dv_d + dv_h
                else:
                    dk_s, dv_s = dk_s + dk_h, dv_s + dv_h
            dq_chunks.append(dq)
        dsink_ref[...] += dsink
        dk = dk_d + pltpu.roll(dk_s, 64, 1)
        dv = dv_d + pltpu.roll(dv_s, 64, 1)
        for j in range(4):
            dproj_ref[:, 128 * j:128 * (j + 1)] = dq_chunks[j]
        dproj_ref[:, ATTN_W:ATTN_W + KV_W] = dk[BLK:2 * BLK]
        dproj_ref[:, ATTN_W + KV_W:ATTN_W + 2 * KV_W] = dv[BLK:2 * BLK]
        dkvn_ref[:, 0:KV_W] = dk[0:BLK]
        dkvn_ref[:, KV_W:2 * KV_W] = dv[0:BLK]

        lng = lng_ref[...]
        gog = gog_ref[...]
        gm, (u, du, da, vhat, rstd, vn, ms) = _gmlp_block(gu, gv, lng, lnb_ref[...], ws_ref, bfull_ref[...])
        _, r_g = _rms(gm, gog)
        dgog_ref[...] += _rowsum8(dmg * gm * r_g)
        dgm = _rms_bwd(dmg, gm, r_g, gog)
        dproj_ref[:, ATTN_W + 2 * KV_W:ATTN_W + 2 * KV_W + GMLP_W] = dgm * ms * du
        dms = dgm * u
        dbs_ref[...] += dms
        dvn_chunks = []
        for j in range(4):
            dmc = dms[:, 128 * j:128 * (j + 1)]
            vcb = vn[:, 128 * j:128 * (j + 1)].astype(BF16)
            acc = jnp.zeros((BLK, 128), F32)
            for pos in range(2):
                g = 2 * j + pos
                mq = lo_q if pos == 0 else jnp.logical_not(lo_q)
                dm = jnp.where(mq, dmc, 0.0).astype(BF16)
                dws_ref[g] += _dot_nt(dm, vcb)
                acc = acc + _dot(wst_ref[g], dm)
            dvn_chunks.append(acc)
        dvn = jnp.concatenate(dvn_chunks, axis=1)
        dlng_ref[...] += _rowsum8(dvn * vhat)
        dlnb_ref[...] += _rowsum8(dvn)
        dvh = dvn * lng
        dact = rstd * (dvh - _seg_mean64(dvh) - vhat * _seg_mean64(dvh * vhat))
        dproj_ref[:, ATTN_W + 2 * KV_W + GMLP_W:IN_W] = dact * da

    acc8 = lambda w: jax.ShapeDtypeStruct((8, w), F32)
    out_shape = [jax.ShapeDtypeStruct((t, IN_W), F32), jax.ShapeDtypeStruct((t, 2 * KV_W), F32),
                 jax.ShapeDtypeStruct((N_HEADS, BLK, 2 * BLK), F32), jax.ShapeDtypeStruct((BLK, 128), F32),
                 acc8(GMLP_W), acc8(GMLP_W), jax.ShapeDtypeStruct((N_GROUPS, BLK, BLK), F32),
                 jax.ShapeDtypeStruct((BLK, GMLP_W), F32), acc8(ATTN_W), acc8(GMLP_W)]
    out_specs = [pl.BlockSpec((BLK, IN_W), lambda n: (n, 0)),
                 pl.BlockSpec((BLK, 2 * KV_W), lambda n: ((n + nb - 1) % nb, 0)),
                 _full((N_HEADS, BLK, 2 * BLK)), _full((BLK, 128)), _full((8, GMLP_W)), _full((8, GMLP_W)),
                 _full((N_GROUPS, BLK, BLK)), _full((BLK, GMLP_W)), _full((8, ATTN_W)), _full((8, GMLP_W))]
    in_specs = _mix_in_specs(nb) + [_full((N_GROUPS, BLK, BLK)),
                                    pl.BlockSpec((BLK, D_MODEL), lambda n: (n, 0)),
                                    _full((D_MODEL, D_MODEL))]
    return _call(
        body, name="mix_bwd", grid=(nb,), out_shape=out_shape, in_specs=in_specs, out_specs=out_specs,
        sem=("arbitrary",), comm=comm, args=(proj, proj, bias, sinks, lng, lnb, ws, bfull, aog, gog, ws_t, dy, w_out))


HALF = 64
ROWS = 32


def _lane_lo(rows):
    return lax.broadcasted_iota(jnp.int32, (rows, 128), 1) < 64


def _mix_stage_kv(proj_ref, kvp_ref, s):
    lo = _lane_lo(2 * BLK)
    for name, col in (("k", ATTN_W), ("v", ATTN_W + KV_W)):
        cur = jnp.concatenate([kvp_ref[:, col - ATTN_W:col - ATTN_W + KV_W], proj_ref[:, col:col + KV_W]], axis=0)
        plain, swapped = cur.astype(BF16), pltpu.roll(cur, 64, 1).astype(BF16)
        zero = jnp.zeros_like(plain)
        for g in range(2):
            dup = jnp.where(lo, plain, swapped) if g == 0 else jnp.where(lo, swapped, plain)
            s[name + "d"][g] = dup
            s[name + "m"][g] = jnp.concatenate([jnp.where(lo, dup, zero), jnp.where(lo, zero, dup)], axis=0)


def _group_rows(ref, g):
    return ref[4 * g:4 * g + 4].reshape(4 * BLK, ref.shape[2])


def _pair_rows(ref, g):
    return jnp.concatenate([jnp.concatenate([ref[4 * g + 2 * c], ref[4 * g + 2 * c + 1]], axis=1) for c in range(2)],
                           axis=0)


def _mask_heads(src_ref, dst_ref):
    lo = _lane_lo(BLK)
    for j in range(4):
        chunk = src_ref[:, 128 * j:128 * (j + 1)]
        dst_ref[2 * j] = jnp.where(lo, chunk, 0.0).astype(BF16)
        dst_ref[2 * j + 1] = jnp.where(lo, 0.0, chunk).astype(BF16)


def _mix_stage_attn(proj_ref, bias_ref, sinks_ref, n, s, keep):
    _mask_heads(proj_ref, s["qm"])
    for g in range(2):
        s["lg"][g] = _dot_nt(_group_rows(s["qm"], g), s["kd"][g])
    n0mask = (n == 0) & (lax.broadcasted_iota(jnp.int32, (HALF, 2 * BLK), 1) < BLK)
    for h in range(N_HEADS):
        sk = sinks_ref[h]
        for hf in range(BLK // HALF):
            rows = slice(HALF * hf, HALF * (hf + 1))
            grows = slice(BLK * (h % 4) + HALF * hf, BLK * (h % 4) + HALF * (hf + 1))
            logit = s["lg"][h // 4, grows, :] * (HEAD_DIM ** -0.5) + bias_ref[h, rows, :]
            logit = jnp.where(n0mask, NEG_INF, logit)
            m = jnp.maximum(jnp.max(logit, axis=1, keepdims=True), sk)
            e = jnp.exp(logit - m)
            es = jnp.exp(sk - m)
            inv = 1.0 / (jnp.sum(e, axis=1, keepdims=True) + es)
            p = e * inv
            s["pb"][h, rows, :] = p.astype(BF16)
            if keep:
                s["p"][h, rows, :] = p
                s["psink"][h, rows, :] = es * inv
    for g in range(2):
        out = _dot(_pair_rows(s["pb"], g), s["vm"][g])
        s["attn"][:, 256 * g:256 * g + 128] = out[0:BLK]
        s["attn"][:, 256 * g + 128:256 * g + 256] = out[BLK:2 * BLK]


def _mix_stage_gmlp_pre(proj_ref, lng, lnb, s, keep):
    c0 = ATTN_W + 2 * KV_W
    for r0 in range(0, BLK, ROWS):
        rows = slice(r0, r0 + ROWS)
        u, du = _gelu_parts(proj_ref[rows, c0:c0 + GMLP_W])
        a, da = _gelu_parts(proj_ref[rows, c0 + GMLP_W:c0 + 2 * GMLP_W])
        ac = a - _seg_mean64(a)
        rstd = lax.rsqrt(_seg_mean64(ac * ac) + LN_EPS)
        vhat = ac * rstd
        s["u"][rows, :] = u
        s["vnb"][rows, :] = (vhat * lng + lnb).astype(BF16)
        if keep:
            s["du"][rows, :] = du
            s["da"][rows, :] = da
            s["vhat"][rows, :] = vhat
            s["rstd"][rows, :] = rstd


def _stack_halves(chunk):
    lo = _lane_lo(BLK)
    zero = jnp.zeros_like(chunk)
    return jnp.concatenate([jnp.where(lo, chunk, zero), jnp.where(lo, zero, chunk)], axis=0)


def _mix_stage_gmlp_mix(ws2_ref, bfull_ref, s):
    for j in range(4):
        cols = slice(128 * j, 128 * (j + 1))
        s["ms"][:, cols] = _dot(ws2_ref[j], _stack_halves(s["vnb"][:, cols])) + bfull_ref[:, cols]


def _mix_scratch(keep):
    f32 = lambda *shape: pltpu.VMEM(shape, F32)
    b16 = lambda *shape: pltpu.VMEM(shape, BF16)
    names = dict(kd=b16(2, 2 * BLK, 128), vd=b16(2, 2 * BLK, 128), km=b16(2, 4 * BLK, 128), vm=b16(2, 4 * BLK, 128),
                 qm=b16(N_HEADS, BLK, 128), lg=f32(2, 4 * BLK, 2 * BLK), pb=b16(N_HEADS, BLK, 2 * BLK),
                 attn=f32(BLK, ATTN_W), u=f32(BLK, GMLP_W), vnb=b16(BLK, GMLP_W), ms=f32(BLK, GMLP_W))
    if keep:
        names.update(dom=b16(N_HEADS, BLK, 128), p=f32(N_HEADS, BLK, 2 * BLK),
                     dls=b16(N_HEADS, BLK, 2 * BLK), psink=f32(N_HEADS, BLK, 1),
                     dattn=f32(BLK, ATTN_W), dmix=f32(BLK, D_MODEL), du=f32(BLK, GMLP_W), da=f32(BLK, GMLP_W),
                     vhat=f32(BLK, GMLP_W), rstd=f32(BLK, GMLP_W), dmsb=b16(BLK, GMLP_W), dvn=f32(BLK, GMLP_W))
    return list(names), list(names.values())


def _mix_specs():
    return [pl.BlockSpec((BLK, IN_W), lambda n: (n, 0)),
            pl.BlockSpec((BLK, 2 * KV_W), lambda n: (jnp.maximum(n - 1, 0), ATTN_W // (2 * KV_W))),
            _full((N_HEADS, BLK, 2 * BLK)),
            pl.BlockSpec(memory_space=pltpu.SMEM),
            _full((1, GMLP_W)), _full((1, GMLP_W)),
            _full((N_GROUPS // 2, BLK, 2 * BLK)), _full((BLK, GMLP_W)),
            _full((1, ATTN_W)), _full((1, GMLP_W))]


def _mix_fwd(proj, bias, sinks, lng, lnb, ws2, bfull, aog, gog, comm):
    t = proj.shape[0]
    names, shapes = _mix_scratch(False)

    def body(proj_ref, kvp_ref, bias_ref, sinks_ref, lng_ref, lnb_ref, ws2_ref, bfull_ref, aog_ref, gog_ref,
             out_ref, *scratch):
        s = dict(zip(names, scratch))
        n = pl.program_id(0)
        _mix_stage_kv(proj_ref, kvp_ref, s)
        _mix_stage_attn(proj_ref, bias_ref, sinks_ref, n, s, False)
        _mix_stage_gmlp_pre(proj_ref, lng_ref[...], lnb_ref[...], s, False)
        _mix_stage_gmlp_mix(ws2_ref, bfull_ref, s)
        for r0 in range(0, BLK, ROWS):
            rows = slice(r0, r0 + ROWS)
            out_ref[rows, 0:ATTN_W] = _rms(s["attn"][rows, :], aog_ref[...])[0].astype(BF16)
            out_ref[rows, ATTN_W:ATTN_W + GMLP_W] = _rms(s["u"][rows, :] * s["ms"][rows, :], gog_ref[...])[0].astype(BF16)

    return _call(
        body, name="mix_fwd", grid=(t // BLK,), out_shape=[jax.ShapeDtypeStruct((t, D_MODEL), BF16)],
        in_specs=_mix_specs(), out_specs=[pl.BlockSpec((BLK, D_MODEL), lambda n: (n, 0))], scratch_shapes=shapes,
        sem=("parallel",), comm=comm, args=(proj, proj, bias, sinks, lng, lnb, ws2, bfull, aog, gog))


def _mix_bwd(proj, bias, sinks, lng, lnb, ws2, wst2, bfull, aog, gog, dy, w_out, comm):
    t = proj.shape[0]
    nb = t // BLK
    names, shapes = _mix_scratch(True)
    c_gu = ATTN_W + 2 * KV_W

    def body(proj_ref, kvp_ref, bias_ref, sinks_ref, lng_ref, lnb_ref, ws2_ref, bfull_ref, aog_ref, gog_ref,
             wst2_ref, dy_ref, wout_ref,
             dproj_ref, dkvn_ref, dl_ref, dsink_ref, dlng_ref, dlnb_ref, dws_ref, dbs_ref, daog_ref, dgog_ref,
             *scratch):
        s = dict(zip(names, scratch))
        n = pl.program_id(0)

        @pl.when(n == 0)
        def _():
            for r in (dl_ref, dsink_ref, dlng_ref, dlnb_ref, dws_ref, dbs_ref, daog_ref, dgog_ref):
                r[...] = jnp.zeros_like(r)

        s["dmix"][...] = _dot_nt(dy_ref[...], wout_ref[...])
        _mix_stage_kv(proj_ref, kvp_ref, s)
        _mix_stage_attn(proj_ref, bias_ref, sinks_ref, n, s, True)
        lng = lng_ref[...]
        _mix_stage_gmlp_pre(proj_ref, lng, lnb_ref[...], s, True)
        _mix_stage_gmlp_mix(ws2_ref, bfull_ref, s)

        aog, gog = aog_ref[...], gog_ref[...]
        for r0 in range(0, BLK, ROWS):
            rows = slice(r0, r0 + ROWS)
            attn, dma = s["attn"][rows, :], s["dmix"][rows, 0:ATTN_W]
            _, r_a = _rms(attn, aog)
            daog_ref[...] += _rowsum8(dma * attn * r_a)
            s["dattn"][rows, :] = _rms_bwd(dma, attn, r_a, aog)
            u, ms, dmg = s["u"][rows, :], s["ms"][rows, :], s["dmix"][rows, ATTN_W:ATTN_W + GMLP_W]
            gm = u * ms
            _, r_g = _rms(gm, gog)
            dgog_ref[...] += _rowsum8(dmg * gm * r_g)
            dgm = _rms_bwd(dmg, gm, r_g, gog)
            dproj_ref[rows, c_gu:c_gu + GMLP_W] = dgm * ms * s["du"][rows, :]
            dms = dgm * u
            dbs_ref[rows, :] += dms
            s["dmsb"][rows, :] = dms.astype(BF16)

        _mask_heads(s["dattn"], s["dom"])
        for g in range(2):
            s["lg"][g] = _dot_nt(_group_rows(s["dom"], g), s["vd"][g])
        lane = lax.broadcasted_iota(jnp.int32, (HALF, 128), 1)
        for hf in range(BLK // HALF):
            rows = slice(HALF * hf, HALF * (hf + 1))
            dsink = jnp.zeros((HALF, 128), F32)
            for h in range(N_HEADS):
                grows = slice(BLK * (h % 4) + HALF * hf, BLK * (h % 4) + HALF * (hf + 1))
                dp = s["lg"][h // 4, grows, :]
                p = s["p"][h, rows, :]
                rs = jnp.sum(p * dp, axis=1, keepdims=True)
                dl = p * (dp - rs)
                dl_ref[h, rows, :] += dl
                dsink = dsink + jnp.where(lane == h, -s["psink"][h, rows, :] * rs, 0.0)
                s["dls"][h, rows, :] = (dl * (HEAD_DIM ** -0.5)).astype(BF16)
            dsink_ref[rows, :] += dsink
        for g in range(2):
            dq = _dot(_pair_rows(s["dls"], g), s["km"][g])
            dproj_ref[:, 256 * g:256 * g + 128] = dq[0:BLK]
            dproj_ref[:, 256 * g + 128:256 * g + 256] = dq[BLK:2 * BLK]
        lo_k = _lane_lo(2 * BLK)
        for col, lhs, rhs in ((0, "dls", "qm"), (KV_W, "pb", "dom")):
            raw = [_dot_tn(_group_rows(s[lhs], g), _group_rows(s[rhs], g)) for g in range(2)]
            both = [r + pltpu.roll(r, 64, 1) for r in raw]
            dkv = jnp.where(lo_k, both[0], both[1])
            dproj_ref[:, ATTN_W + col:ATTN_W + col + KV_W] = dkv[BLK:2 * BLK]
            dkvn_ref[:, col:col + KV_W] = dkv[0:BLK]

        for j in range(4):
            cols = slice(128 * j, 128 * (j + 1))
            dm2 = _stack_halves(s["dmsb"][:, cols])
            vnb = s["vnb"][:, cols]
            for pos in range(2):
                dws_ref[2 * j + pos] += _dot_nt(dm2[BLK * pos:BLK * (pos + 1)], vnb)
            s["dvn"][:, cols] = _dot(wst2_ref[j], dm2)
        for r0 in range(0, BLK, ROWS):
            rows = slice(r0, r0 + ROWS)
            dvn, vhat = s["dvn"][rows, :], s["vhat"][rows, :]
            dlng_ref[...] += _rowsum8(dvn * vhat)
            dlnb_ref[...] += _rowsum8(dvn)
            dvh = dvn * lng
            dact = s["rstd"][rows, :] * (dvh - _seg_mean64(dvh) - vhat * _seg_mean64(dvh * vhat))
            dproj_ref[rows, c_gu + GMLP_W:IN_W] = dact * s["da"][rows, :]

    acc8 = lambda w: jax.ShapeDtypeStruct((8, w), F32)
    out_shape = [jax.ShapeDtypeStruct((t, IN_W), F32), jax.ShapeDtypeStruct((t, 2 * KV_W), F32),
                 jax.ShapeDtypeStruct((N_HEADS, BLK, 2 * BLK), F32), jax.ShapeDtypeStruct((BLK, 128), F32),
                 acc8(GMLP_W), acc8(GMLP_W), jax.ShapeDtypeStruct((N_GROUPS, BLK, BLK), F32),
                 jax.ShapeDtypeStruct((BLK, GMLP_W), F32), acc8(ATTN_W), acc8(GMLP_W)]
    out_specs = [pl.BlockSpec((BLK, IN_W), lambda n: (n, 0)),
                 pl.BlockSpec((BLK, 2 * KV_W), lambda n: ((n + nb - 1) % nb, 0)),
                 _full((N_HEADS, BLK, 2 * BLK)), _full((BLK, 128)), _full((8, GMLP_W)), _full((8, GMLP_W)),
                 _full((N_GROUPS, BLK, BLK)), _full((BLK, GMLP_W)), _full((8, ATTN_W)), _full((8, GMLP_W))]
    in_specs = _mix_specs() + [_full((N_GROUPS // 2, BLK, 2 * BLK)),
                               pl.BlockSpec((BLK, D_MODEL), lambda n: (n, 0)),
                               _full((D_MODEL, D_MODEL))]
    return _call(
        body, name="mix_bwd", grid=(nb,), out_shape=out_shape, in_specs=in_specs, out_specs=out_specs,
        scratch_shapes=shapes, sem=("arbitrary",), comm=comm,
        args=(proj, proj, bias, sinks, lng, lnb, ws2, bfull, aog, gog, wst2, dy, w_out))


def _outproj(mixed, w_out, x, g1, ln1g, ln1b, sc2, sh2, tm):
    t, d = x.shape

    def body(mx_ref, w_ref, x_ref, g1_ref, lg_ref, lb_ref, sc_ref, sh_ref, y_ref, x1_ref, h2_ref):
        y = _dot(mx_ref[...], w_ref[...])
        xhat, _ = _ln_stats(ALPHA * x_ref[...] + g1_ref[...] * y)
        x1 = xhat * lg_ref[...] + lb_ref[...]
        y_ref[...] = y
        x1_ref[...] = x1
        h2_ref[...] = (x1 * (1.0 + sc_ref[...]) + sh_ref[...]).astype(BF16)

    row = pl.BlockSpec((tm, d), lambda i: (i, 0))
    vec = _full((1, d))
    return pl.pallas_call(
        body, name="outproj", grid=(t // tm,),
        out_shape=[jax.ShapeDtypeStruct((t, d), F32), jax.ShapeDtypeStruct((t, d), F32),
                   jax.ShapeDtypeStruct((t, d), BF16)],
        in_specs=[row, _full((d, d)), row, vec, vec, vec, vec, vec], out_specs=[row, row, row],
        compiler_params=_params(("parallel",)),
    )(mixed, w_out, x, g1, ln1g, ln1b, sc2, sh2)


def _ffn_up(h2, w_gu_t, tm, tn, comm):
    t, d = h2.shape
    nff = D_FF // tn

    def body(h_ref, wg_ref, wu_ref, dsu_ref, sg_ref, act_ref):
        h = h_ref[...]
        g = _dot_nt(h, wg_ref[...])
        u = _dot_nt(h, wu_ref[...])
        s = _sigmoid(g)
        sg = g * s
        dsu_ref[...] = (u * (s * (1.0 + g * (1.0 - s)))).astype(BF16)
        sg_ref[...] = sg.astype(BF16)
        act_ref[...] = (sg * u).astype(BF16)

    out = pl.BlockSpec((tm, tn), lambda j, i: (i, j))
    shp = jax.ShapeDtypeStruct((t, D_FF), BF16)
    return _call(
        body, name="ffn_up", grid=(nff, t // tm), out_shape=[shp, shp, shp],
        in_specs=[pl.BlockSpec((tm, d), lambda j, i: (i, 0)),
                  pl.BlockSpec((tn, d), lambda j, i: (j, 0)),
                  pl.BlockSpec((tn, d), lambda j, i: (j + nff, 0))],
        out_specs=[out, out, out], sem=("parallel", "parallel"), comm=comm, args=(h2, w_gu_t, w_gu_t))


def _ffn_down(act, w_down, x1, target, g2, ln2g, ln2b, tm):
    t, d = x1.shape

    def body(act_ref, w_ref, x1_ref, tg_ref, g2_ref, lg_ref, lb_ref,
             dz_ref, dy_ref, loss_ref, dlg_ref, dlb_ref, dg2_ref):
        @pl.when(pl.program_id(0) == 0)
        def _():
            for r in (loss_ref, dlg_ref, dlb_ref, dg2_ref):
                r[...] = jnp.zeros_like(r)

        y2 = _dot(act_ref[...], w_ref[...])
        g2 = g2_ref[...]
        lg = lg_ref[...]
        xhat, rstd = _ln_stats(ALPHA * x1_ref[...] + g2 * y2)
        err = xhat * lg + lb_ref[...] - tg_ref[...]
        loss_ref[...] += _rowsum8(err * err)
        dx2 = err * (1.0 / d)
        dlg_ref[...] += _rowsum8(dx2 * xhat)
        dlb_ref[...] += _rowsum8(dx2)
        dz = _ln_bwd(dx2 * lg, xhat, rstd)
        dg2_ref[...] += _rowsum8(dz * y2)
        dz_ref[...] = dz
        dy_ref[...] = (g2 * dz).astype(BF16)

    row = pl.BlockSpec((tm, d), lambda i: (i, 0))
    vec = _full((1, d))
    acc = _full((8, d))
    acc_shape = jax.ShapeDtypeStruct((8, d), F32)
    return pl.pallas_call(
        body, name="ffn_down", grid=(t // tm,),
        out_shape=[jax.ShapeDtypeStruct((t, d), F32), jax.ShapeDtypeStruct((t, d), BF16)] + [acc_shape] * 4,
        in_specs=[pl.BlockSpec((tm, D_FF), lambda i: (i, 0)), _full((D_FF, d)), row, row, vec, vec, vec],
        out_specs=[row, row, acc, acc, acc, acc], compiler_params=_params(("arbitrary",)),
    )(act, w_down, x1, target, g2, ln2g, ln2b)


def _ffn_dact(dy2, w_down, dsu, sg, tm, tn, comm):
    t, d = dy2.shape

    def body(dy_ref, w_ref, dsu_ref, sg_ref, dg_ref, du_ref):
        dact = _dot_nt(dy_ref[...], w_ref[...])
        dg_ref[...] = (dact * dsu_ref[...].astype(F32)).astype(BF16)
        du_ref[...] = (dact * sg_ref[...].astype(F32)).astype(BF16)

    tile = pl.BlockSpec((tm, tn), lambda j, i: (i, j))
    shp = jax.ShapeDtypeStruct((t, D_FF), BF16)
    return _call(
        body, name="ffn_dact", grid=(D_FF // tn, t // tm), out_shape=[shp, shp],
        in_specs=[pl.BlockSpec((tm, d), lambda j, i: (i, 0)), pl.BlockSpec((tn, d), lambda j, i: (j, 0)), tile, tile],
        out_specs=[tile, tile], sem=("parallel", "parallel"), comm=comm, args=(dy2, w_down, dsu, sg))


def _ffn_dh2(dgate, dup, w_gu_t, x1, x, y, dz2, sc2, g1, ln1g, tm):
    t, d = x1.shape
    nt = t // tm

    def body(dg_ref, du_ref, w_ref, x1_ref, x_ref, y_ref, dz2_ref, sc_ref, g1_ref, lg_ref,
             dz1_ref, dy_ref, dsc_ref, dsh_ref, dlg_ref, dlb_ref, dg1_ref, acc_ref):
        i = pl.program_id(0)

        @pl.when(i == 0)
        def _():
            for r in (dsc_ref, dsh_ref, dlg_ref, dlb_ref, dg1_ref, acc_ref):
                r[...] = jnp.zeros_like(r)

        slot = i % 2
        dh2 = acc_ref[1 - slot]
        acc_ref[slot] = _dot(dg_ref[...], w_ref[0:D_FF]) + _dot(du_ref[...], w_ref[D_FF:2 * D_FF])

        valid = i > 0

        def add(r, v):
            r[...] += jnp.where(valid, _rowsum8(v), 0.0)

        x1 = x1_ref[...]
        y = y_ref[...]
        g1 = g1_ref[...]
        add(dsc_ref, dh2 * x1)
        add(dsh_ref, dh2)
        dx1 = dh2 * (1.0 + sc_ref[...]) + ALPHA * dz2_ref[...]
        xhat, rstd = _ln_stats(ALPHA * x_ref[...] + g1 * y)
        add(dlg_ref, dx1 * xhat)
        add(dlb_ref, dx1)
        dz1 = _ln_bwd(dx1 * lg_ref[...], xhat, rstd)
        add(dg1_ref, dz1 * y)
        dz1_ref[...] = dz1
        dy_ref[...] = (g1 * dz1).astype(BF16)

    ahead = pl.BlockSpec((tm, D_FF), lambda i: (jnp.minimum(i, nt - 1), 0))
    row = pl.BlockSpec((tm, d), lambda i: (jnp.maximum(i - 1, 0), 0))
    vec = _full((1, d))
    acc = _full((8, d))
    acc_shape = jax.ShapeDtypeStruct((8, d), F32)
    return pl.pallas_call(
        body, name="ffn_dh2", grid=(nt + 1,),
        out_shape=[jax.ShapeDtypeStruct((t, d), F32), jax.ShapeDtypeStruct((t, d), BF16)] + [acc_shape] * 5,
        in_specs=[ahead, ahead, _full((2 * D_FF, d)), row, row, row, row, vec, vec, vec],
        out_specs=[row, row, acc, acc, acc, acc, acc],
        scratch_shapes=[pltpu.VMEM((2, tm, d), F32)], compiler_params=_params(("arbitrary",)),
    )(dgate, dup, w_gu_t, x1, x, y, dz2, sc2, g1, ln1g)


def _din(dproj, dkvn, w_in_t, x, dz1, sc1, tm, comm):
    t, d = x.shape

    def body(dp_ref, dkv_ref, w_ref, x_ref, dz1_ref, sc_ref, dx_ref, dpb_ref, dbin_ref, dsc_ref, dsh_ref):
        @pl.when(pl.program_id(0) == 0)
        def _():
            for r in (dbin_ref, dsc_ref, dsh_ref):
                r[...] = jnp.zeros_like(r)

        dp = jnp.concatenate([dp_ref[:, 0:ATTN_W], dp_ref[:, ATTN_W:ATTN_W + 2 * KV_W] + dkv_ref[...],
                              dp_ref[:, ATTN_W + 2 * KV_W:IN_W]], axis=1)
        dbin_ref[...] += _rowsum8(dp)
        dpb = dp.astype(BF16)
        dpb_ref[...] = dpb
        dh = _dot(dpb, w_ref[...])
        dsc_ref[...] += _rowsum8(dh * x_ref[...])
        dsh_ref[...] += _rowsum8(dh)
        dx_ref[...] = dh * (1.0 + sc_ref[...]) + ALPHA * dz1_ref[...]

    row = lambda w: pl.BlockSpec((tm, w), lambda i: (i, 0))
    return _call(
        body, name="din", grid=(t // tm,),
        out_shape=[jax.ShapeDtypeStruct((t, d), F32), jax.ShapeDtypeStruct((t, IN_W), BF16),
                   jax.ShapeDtypeStruct((8, IN_W), F32), jax.ShapeDtypeStruct((8, d), F32),
                   jax.ShapeDtypeStruct((8, d), F32)],
        in_specs=[row(IN_W), row(2 * KV_W), _full((IN_W, d)), row(d), row(d), _full((1, d))],
        out_specs=[row(d), row(IN_W), _full((8, IN_W)), _full((8, d)), _full((8, d))],
        sem=("arbitrary",), comm=comm, args=(dproj, dkvn, w_in_t, x, dz1, sc1))


def _wgrad(name, a, b, tmm, tk, comm=None, a2=None):
    t, m = a.shape
    n = b.shape[1]
    nk = t // tk
    nm = m // tmm

    def body(*refs):
        a_refs, (b_ref, o_ref, acc_ref) = refs[:-3], refs[-3:]
        i, k = pl.program_id(0), pl.program_id(1)
        a_tile = a_refs[0][...] if a2 is None else jnp.where(i < nm, a_refs[0][...], a_refs[1][...])
        part = _dot_tn(a_tile, b_ref[...])

        @pl.when(k == 0)
        def _():
            acc_ref[...] = part

        @pl.when(k > 0)
        def _():
            acc_ref[...] += part

        @pl.when(k == nk - 1)
        def _():
            o_ref[...] = acc_ref[...].astype(BF16)

    if a2 is None:
        a_specs, a_args, n_tiles = [pl.BlockSpec((tk, tmm), lambda i, k: (k, i))], (a,), nm
    else:
        a_specs = [pl.BlockSpec((tk, tmm), lambda i, k: (jnp.where(i < nm, k, 0), jnp.minimum(i, nm - 1))),
                   pl.BlockSpec((tk, tmm), lambda i, k: (jnp.where(i < nm, 0, k), jnp.maximum(i - nm, 0)))]
        a_args, n_tiles = (a, a2), 2 * nm
    (out,), got = _call(
        body, name=name, grid=(n_tiles, nk), out_shape=[jax.ShapeDtypeStruct((n_tiles * tmm, n), BF16)],
        in_specs=a_specs + [pl.BlockSpec((tk, n), lambda i, k: (k, 0))],
        out_specs=[pl.BlockSpec((tmm, n), lambda i, k: (i, 0))],
        scratch_shapes=[pltpu.VMEM((tmm, n), F32)], sem=("parallel", "arbitrary"), comm=comm, args=a_args + (b,))
    return out if comm is None else (out, got)


def _adamw(w, g, m, v):
    m = ADAM_B1 * m + (1.0 - ADAM_B1) * g
    v = ADAM_B2 * v + (1.0 - ADAM_B2) * (g * g)
    m_hat = m / (1.0 - ADAM_B1 ** ADAM_STEP)
    v_hat = v / (1.0 - ADAM_B2 ** ADAM_STEP)
    delta = -ADAM_LR * (m_hat / (jnp.sqrt(v_hat) + ADAM_EPS) + ADAM_WD * w)
    return delta, m, v


def _adam_reduce(name, parts, w, m, v, tr):
    r, cdim = w.shape

    def body(p_ref, w_ref, m_ref, v_ref, g_ref, d_ref, mo_ref, vo_ref):
        g = p_ref[0].astype(F32)
        for s in range(1, N_DEV):
            g = g + p_ref[s].astype(F32)
        d_ref[...], mo_ref[...], vo_ref[...] = _adamw(w_ref[...], g, m_ref[...], v_ref[...])
        g_ref[...] = g

    tile = pl.BlockSpec((tr, cdim), lambda i: (i, 0))
    shp = jax.ShapeDtypeStruct((r, cdim), F32)
    return pl.pallas_call(
        body, name=name, grid=(r // tr,), out_shape=[shp] * 4,
        in_specs=[pl.BlockSpec((N_DEV, tr, cdim), lambda i: (0, i, 0)), tile, tile, tile],
        out_specs=[tile] * 4, compiler_params=_params(("parallel",)),
    )(parts, w, m, v)


def _adam_w_ada(c_all_t, dmod_cols, w, m, v):
    def body(ct_ref, dm_ref, w_ref, m_ref, v_ref, g_ref, d_ref, mo_ref, vo_ref):
        ct = ct_ref[...]
        s = (ct * _sigmoid(ct)).astype(BF16)
        g = _dot(s, dm_ref[...].astype(BF16))
        d_ref[...], mo_ref[...], vo_ref[...] = _adamw(w_ref[...], g, m_ref[...], v_ref[...])
        g_ref[...] = g

    shp = jax.ShapeDtypeStruct(w.shape, F32)
    return pl.pallas_call(
        body, name="adam_w_ada", grid=(1,), out_shape=[shp] * 4,
        in_specs=[_full(c_all_t.shape), _full(dmod_cols.shape)] + [_full(w.shape)] * 3,
        out_specs=[_full(w.shape)] * 4, compiler_params=_params(("arbitrary",)),
    )(c_all_t, dmod_cols, w, m, v)


SMALL_EARLY = ["rel_bias", "attn_sinks", "gmlp_ln_g", "gmlp_ln_b", "gmlp_w_s", "gmlp_b_s",
               "attn_out_g", "gmlp_out_g", "ln1_g", "ln1_b", "ln2_g", "ln2_b"]
SMALL_LATE = ["b_ada", "b_in"]
WEIGHTS = ["rel_bias", "w_ada", "b_ada", "w_in", "b_in", "attn_sinks", "gmlp_ln_g", "gmlp_ln_b", "gmlp_w_s",
           "gmlp_b_s", "attn_out_g", "gmlp_out_g", "w_out", "ln1_g", "ln1_b", "w_gate_up", "w_down", "ln2_g", "ln2_b"]


def _seg_rows(nelem):
    return -(-nelem // 1024) * 8


def _pack(named, names):
    parts = []
    for name in names:
        flat = named[name].reshape(-1).astype(F32)
        rows = _seg_rows(flat.shape[0])
        parts.append(jnp.pad(flat, (0, rows * 128 - flat.shape[0])).reshape(rows, 128))
    return jnp.concatenate(parts, axis=0)


def _unpack(packed, shapes, names):
    out, r0 = {}, 0
    for name in names:
        nelem = math.prod(shapes[name])
        rows = _seg_rows(nelem)
        out[name] = packed[r0:r0 + rows].reshape(-1)[:nelem].reshape(shapes[name])
        r0 += rows
    return out


def _t5_bucket_map():
    qi = jnp.arange(BLK)[:, None]
    si = jnp.arange(2 * BLK)[None, :]
    n = jnp.maximum(qi + BLK - si, 0)
    max_exact = N_BUCKETS // 2
    nf = jnp.maximum(n, max_exact).astype(F32)
    large = max_exact + (jnp.log(nf / max_exact) / math.log(MAX_DISTANCE / max_exact)
                         * (N_BUCKETS - max_exact)).astype(jnp.int32)
    large = jnp.minimum(large, N_BUCKETS - 1)
    return jnp.where(n < max_exact, n, large).astype(jnp.int32)


def kernel(x, c, rel_bias, w_ada, b_ada, w_in, b_in, attn_sinks, gmlp_ln_g, gmlp_ln_b, gmlp_w_s, gmlp_b_s, attn_out_g, gmlp_out_g, w_out, ln1_g, ln1_b, w_gate_up, w_down, ln2_g, ln2_b, loss_target, m_rel_bias, m_w_ada, m_b_ada, m_w_in, m_b_in, m_attn_sinks, m_gmlp_ln_g, m_gmlp_ln_b, m_gmlp_w_s, m_gmlp_b_s, m_attn_out_g, m_gmlp_out_g, m_w_out, m_ln1_g, m_ln1_b, m_w_gate_up, m_w_down, m_ln2_g, m_ln2_b, v_rel_bias, v_w_ada, v_b_ada, v_w_in, v_b_in, v_attn_sinks, v_gmlp_ln_g, v_gmlp_ln_b, v_gmlp_w_s, v_gmlp_b_s, v_attn_out_g, v_gmlp_out_g, v_w_out, v_ln1_g, v_ln1_b, v_w_gate_up, v_w_down, v_ln2_g, v_ln2_b):
    wts = dict(rel_bias=rel_bias, w_ada=w_ada, b_ada=b_ada, w_in=w_in, b_in=b_in, attn_sinks=attn_sinks,
               gmlp_ln_g=gmlp_ln_g, gmlp_ln_b=gmlp_ln_b, gmlp_w_s=gmlp_w_s, gmlp_b_s=gmlp_b_s,
               attn_out_g=attn_out_g, gmlp_out_g=gmlp_out_g, w_out=w_out, ln1_g=ln1_g, ln1_b=ln1_b,
               w_gate_up=w_gate_up, w_down=w_down, ln2_g=ln2_g, ln2_b=ln2_b)
    mom_m = dict(rel_bias=m_rel_bias, w_ada=m_w_ada, b_ada=m_b_ada, w_in=m_w_in, b_in=m_b_in,
                 attn_sinks=m_attn_sinks, gmlp_ln_g=m_gmlp_ln_g, gmlp_ln_b=m_gmlp_ln_b, gmlp_w_s=m_gmlp_w_s,
                 gmlp_b_s=m_gmlp_b_s, attn_out_g=m_attn_out_g, gmlp_out_g=m_gmlp_out_g, w_out=m_w_out,
                 ln1_g=m_ln1_g, ln1_b=m_ln1_b, w_gate_up=m_w_gate_up, w_down=m_w_down, ln2_g=m_ln2_g,
                 ln2_b=m_ln2_b)
    mom_v = dict(rel_bias=v_rel_bias, w_ada=v_w_ada, b_ada=v_b_ada, w_in=v_w_in, b_in=v_b_in,
                 attn_sinks=v_attn_sinks, gmlp_ln_g=v_gmlp_ln_g, gmlp_ln_b=v_gmlp_ln_b, gmlp_w_s=v_gmlp_w_s,
                 gmlp_b_s=v_gmlp_b_s, attn_out_g=v_attn_out_g, gmlp_out_g=v_gmlp_out_g, w_out=v_w_out,
                 ln1_g=v_ln1_g, ln1_b=v_ln1_b, w_gate_up=v_w_gate_up, w_down=v_w_down, ln2_g=v_ln2_g,
                 ln2_b=v_ln2_b)

    t = x.shape[1]
    tm = min(512, t)
    tn_ff = D_FF // 2
    tk_tok = min(1024, t)
    me = 4 * lax.axis_index("x") + 2 * lax.axis_index("y") + lax.axis_index("c")
    xs = x[0]
    target = loss_target[0]

    c_g, w_in_g = _exchange("gather_in", [jnp.broadcast_to(c, (8, D_MODEL)), w_in[0].T.astype(BF16)], (False, False))
    c_all = c_g[:, 0, :]
    w_in_t = w_in_g.reshape(IN_W, D_MODEL)

    ncol = w_ada.shape[2]
    b_cols = lax.dynamic_slice(b_ada, (0, me * ncol), (1, ncol))
    mod_part = _mod_partial(c_all, w_ada[0], b_cols)
    (mod_g,) = _exchange("gather_mod", [mod_part], (False,))
    mod = lax.dynamic_slice(mod_g, (0, me, 0), (N_DEV, 1, ncol)).reshape(1, N_DEV * ncol)
    sh1, sc1, g1, sh2, sc2, g2 = [mod[:, i * D_MODEL:(i + 1) * D_MODEL] for i in range(6)]

    bucket = _t5_bucket_map()
    bias = _bias_table(rel_bias, bucket)
    causal = jnp.tril(jnp.ones((BLK, BLK), dtype=bool))
    ws = jnp.where(causal[None], gmlp_w_s[0], 0.0).astype(BF16)
    pair = lambda w: jnp.concatenate([w[0::2], w[1::2]], axis=2)
    ws2, wst2 = pair(ws), pair(jnp.swapaxes(ws, 1, 2))
    bfull = jnp.repeat(gmlp_b_s[0].T, GMLP_W // N_GROUPS, axis=1)
    sinks = attn_sinks[0]

    proj, h1 = _inproj(xs, sc1, sh1, w_in_t, b_in, tm)
    (mixed,), (w_out_g, w_gu_g) = _mix_fwd(
        proj, bias, sinks, gmlp_ln_g, gmlp_ln_b, ws2, bfull, attn_out_g, gmlp_out_g,
        comm=([w_out[0].astype(BF16), w_gate_up[0].T.astype(BF16)], (False, False)))
    w_out_f = w_out_g.reshape(D_MODEL, D_MODEL)
    w_gu_t = w_gu_g.reshape(2 * D_FF, D_MODEL)
    y1, x1, h2 = _outproj(mixed, w_out_f, xs, g1, ln1_g, ln1_b, sc2, sh2, tm)
    (dsu, sg, act), (w_down_g,) = _ffn_up(h2, w_gu_t, tm, tn_ff, comm=([w_down[0].astype(BF16)], (False,)))
    w_down_f = w_down_g.reshape(D_FF, D_MODEL)
    dz2, dy2, loss_p, d_ln2g, d_ln2b, d_g2 = _ffn_down(act, w_down_f, x1, target, g2, ln2_g, ln2_b, tm)
    loss = lax.psum(0.5 / D_MODEL * jnp.sum(loss_p), ("x", "y", "c"))

    slots = lambda a: a.reshape(N_DEV, -1, D_MODEL)
    dw_down = _wgrad("wgrad_down", act, dy2, tn_ff, tk_tok)
    (dgate, dup), (r_down,) = _ffn_dact(dy2, w_down_f, dsu, sg, tm, tn_ff, comm=([slots(dw_down)], (True,)))
    dz1, dy1, d_sc2, d_sh2, d_ln1g, d_ln1b, d_g1 = _ffn_dh2(dgate, dup, w_gu_t, x1, xs, y1, dz2, sc2, g1, ln1_g,
                                                           min(256, t))
    dw_gu_t = _wgrad("wgrad_gate_up", dgate, h2, tn_ff, tk_tok, a2=dup)
    dw_out = _wgrad("wgrad_out", mixed, dy1, D_MODEL, tk_tok)
    ((dproj, dkvn, dl_acc, dsink_acc, d_lng, d_lnb, d_ws, d_bs, d_aog, d_gog), (r_gu, r_out)) = _mix_bwd(
        proj, bias, sinks, gmlp_ln_g, gmlp_ln_b, ws2, wst2, bfull, attn_out_g, gmlp_out_g, dy1, w_out_f,
        comm=([slots(dw_gu_t), slots(dw_out)], (True, True)))
    d_relb = _bias_grad(dl_acc, bucket)

    rsum = lambda a: jnp.sum(a, axis=0)
    early_g = dict(
        rel_bias=d_relb[:, 0, :N_BUCKETS].T, attn_sinks=rsum(dsink_acc)[:N_HEADS],
        gmlp_ln_g=rsum(d_lng), gmlp_ln_b=rsum(d_lnb), gmlp_w_s=jnp.where(causal[None], d_ws, 0.0),
        gmlp_b_s=jnp.sum(d_bs.reshape(BLK, N_GROUPS, GMLP_W // N_GROUPS), axis=2).T,
        attn_out_g=rsum(d_aog), gmlp_out_g=rsum(d_gog), ln1_g=rsum(d_ln1g), ln1_b=rsum(d_ln1b),
        ln2_g=rsum(d_ln2g), ln2_b=rsum(d_ln2b))
    (grad_x, dproj_b, d_bin, d_sc1, d_sh1), _ = _din(dproj, dkvn, w_in_t, xs, dz1, sc1, tm, comm=None)
    dw_in_t, (early_all,) = _wgrad("wgrad_in", dproj_b, h1, IN_W, tk_tok,
                                   comm=([_pack(early_g, SMALL_EARLY)], (False,)))
    dmod = jnp.concatenate([rsum(d_sh1), rsum(d_sc1), rsum(d_g1), rsum(d_sh2), rsum(d_sc2), rsum(d_g2)])
    late_all, r_in = _exchange("scatter_in", [_pack(dict(b_ada=dmod, b_in=rsum(d_bin)), SMALL_LATE), slots(dw_in_t)],
                               (False, True))

    small = [{}, {}, {}, {}]
    for label, names, parts in (("adam_small_early", SMALL_EARLY, early_all), ("adam_small_late", SMALL_LATE, late_all)):
        res = _adam_reduce(label, parts, _pack(wts, names), _pack(mom_m, names), _pack(mom_v, names), parts.shape[1])
        shapes = {k: wts[k].shape for k in names}
        for i in range(4):
            small[i].update(_unpack(res[i], shapes, names))

    dmod_all = late_all[:, :_seg_rows(6 * D_MODEL), :].reshape(N_DEV, 6 * D_MODEL)
    dmod_cols = lax.dynamic_slice(dmod_all, (0, me * ncol), (N_DEV, ncol))
    kpad = 128 - N_DEV
    ada = _adam_w_ada(jnp.pad(c_all.T, ((0, 0), (0, kpad))), jnp.pad(dmod_cols, ((0, kpad), (0, 0))),
                      w_ada[0], m_w_ada[0], v_w_ada[0])

    tr = lambda a: jnp.swapaxes(a, -1, -2)
    big = {}
    big["w_in"] = [tr(o)[None] for o in _adam_reduce("adam_w_in", r_in, w_in[0].T, m_w_in[0].T, v_w_in[0].T, 112)]
    big["w_out"] = [o[None] for o in _adam_reduce("adam_w_out", r_out, w_out[0], m_w_out[0], v_w_out[0], 128)]
    big["w_gate_up"] = [tr(o)[None] for o in _adam_reduce("adam_w_gu", r_gu, w_gate_up[0].T, m_w_gate_up[0].T,
                                                           v_w_gate_up[0].T, 352)]
    big["w_down"] = [o[None] for o in _adam_reduce("adam_w_down", r_down, w_down[0], m_w_down[0], v_w_down[0], 176)]
    big["w_ada"] = [o[None] for o in ada]

    outs = [[], [], [], []]
    for name in WEIGHTS:
        for i in range(4):
            outs[i].append(big[name][i] if name in big else small[i][name])
    return (loss, grad_x[None], *outs[0], *outs[1], *outs[2], *outs[3])
```

```python
import math

import jax
import jax.numpy as jnp
from jax import lax
from jax.experimental import pallas as pl
from jax.experimental.pallas import tpu as pltpu

F32 = jnp.float32
BF16 = jnp.bfloat16
MESH = pl.DeviceIdType.MESH

N_DEV = 8
D_MODEL = 1024
HEAD_DIM = 64
N_HEADS = 8
N_GROUPS = 8
ATTN_W = 512
KV_W = 128
GMLP_W = 512
IN_W = 1792
BLK = 128
N_BUCKETS = 32
MAX_DISTANCE = 128
D_FF = 2816
ALPHA = 2.0 ** 0.25
LN_EPS = 1e-5
NEG_INF = -1e30
ADAM_LR = 0.001
ADAM_B1 = 0.9
ADAM_B2 = 0.999
ADAM_EPS = 1e-08
ADAM_WD = 0.01
ADAM_STEP = 10
GELU_C0 = math.sqrt(2.0 / math.pi)
GELU_C1 = 0.044715

VMEM_LIMIT = 56 * 1024 * 1024


def _params(sem):
    return pltpu.CompilerParams(dimension_semantics=sem, vmem_limit_bytes=VMEM_LIMIT)


def _dot(a, b):
    return lax.dot_general(a, b, (((1,), (0,)), ((), ())), preferred_element_type=F32)


def _dot_nt(a, b):
    return lax.dot_general(a, b, (((1,), (1,)), ((), ())), preferred_element_type=F32)


def _dot_tn(a, b):
    return lax.dot_general(a, b, (((0,), (0,)), ((), ())), preferred_element_type=F32)


def _full(shape):
    nd = len(shape)
    return pl.BlockSpec(shape, lambda *_: (0,) * nd)


TAIL_ROWS = 16


def _row_passes(n_rows, fn, rows=TAIL_ROWS):
    def step(c, carry):
        fn(pl.ds(pl.multiple_of(c * rows, rows), rows))
        return carry

    lax.fori_loop(0, n_rows // rows, step, 0, unroll=2)


def _rowsum8(v):
    r, c = v.shape
    return jnp.sum(v.reshape(r // 8, 8, c), axis=0)


def _sigmoid(v):
    return 1.0 / (1.0 + jnp.exp(-v))


def _gelu_parts(v):
    v2 = v * v
    t = jnp.tanh(GELU_C0 * (v + GELU_C1 * v * v2))
    g = 0.5 * v * (1.0 + t)
    dg = 0.5 * (1.0 + t) + 0.5 * v * (1.0 - t * t) * (GELU_C0 * (1.0 + 3.0 * GELU_C1 * v2))
    return g, dg


def _ln_stats(z):
    mu = jnp.mean(z, axis=1, keepdims=True)
    zc = z - mu
    var = jnp.mean(zc * zc, axis=1, keepdims=True)
    rstd = lax.rsqrt(var + LN_EPS)
    return zc * rstd, rstd


def _ln_bwd(dxhat, xhat, rstd):
    m1 = jnp.mean(dxhat, axis=1, keepdims=True)
    m2 = jnp.mean(dxhat * xhat, axis=1, keepdims=True)
    return rstd * (dxhat - m1 - xhat * m2)


def _seg_mean64(v):
    r = v.shape[0]
    lo = lax.broadcasted_iota(jnp.int32, (r, 128), 1) < 64
    outs = []
    for j in range(v.shape[1] // 128):
        ch = v[:, 128 * j:128 * (j + 1)]
        s_lo = jnp.sum(jnp.where(lo, ch, 0.0), axis=1, keepdims=True)
        s_hi = jnp.sum(jnp.where(lo, 0.0, ch), axis=1, keepdims=True)
        outs.append(jnp.where(lo, s_lo, s_hi) * (1.0 / 64.0))
    return jnp.concatenate(outs, axis=1)


def _rms(a, g):
    r = lax.rsqrt(jnp.mean(a * a, axis=1, keepdims=True) + LN_EPS)
    return a * r * g, r


def _rms_bwd(dout, a, r, g):
    t = dout * g
    return r * t - a * (r * r * r) * jnp.mean(t * a, axis=1, keepdims=True)


PEER_ORDER = (1, 2, 4, 3, 5, 6, 7)


def _peer(j):
    x, y, c = lax.axis_index("x"), lax.axis_index("y"), lax.axis_index("c")
    px = 1 - x if j & 4 else x
    py = 1 - y if j & 2 else y
    pc = 1 - c if j & 1 else c
    return (px, py, pc), 4 * px + 2 * py + pc


def _exchange_copies(ins, outs, scatter, send_sems, recv_sems, loc_sems):
    me = 4 * lax.axis_index("x") + 2 * lax.axis_index("y") + lax.axis_index("c")
    n = len(ins)

    def src(k, idx):
        return ins[k].at[idx] if scatter[k] else ins[k]

    def remote(k, j, mine):
        dev, idx = _peer(j)
        return pltpu.make_async_remote_copy(
            src_ref=src(k, idx), dst_ref=outs[k].at[me if mine else idx],
            send_sem=send_sems.at[k, j - 1], recv_sem=recv_sems.at[k, j - 1],
            device_id=dev, device_id_type=MESH)

    local = [pltpu.make_async_copy(src(k, me), outs[k].at[me], loc_sems.at[k]) for k in range(n)]
    sends = [remote(k, j, True) for j in PEER_ORDER for k in range(n)]
    arrivals = [remote(k, j, False) for j in PEER_ORDER for k in range(n)]
    return local, sends, arrivals


def _exchange_start(*a):
    local, sends, _ = _exchange_copies(*a)
    for cp in local + sends:
        cp.start()


def _exchange_wait(*a):
    local, sends, arrivals = _exchange_copies(*a)
    for cp in arrivals:
        cp.wait_recv()
    for cp in sends:
        cp.wait_send()
    for cp in local:
        cp.wait()


def _exchange_shapes(arrays, scatter):
    return [jax.ShapeDtypeStruct((N_DEV,) + (a.shape[1:] if s else a.shape), a.dtype) for a, s in zip(arrays, scatter)]


def _exchange_sems(n):
    return [pltpu.SemaphoreType.DMA((n, N_DEV - 1)), pltpu.SemaphoreType.DMA((n, N_DEV - 1)),
            pltpu.SemaphoreType.DMA((n,))]


def _exchange(name, arrays, scatter):
    n = len(arrays)

    def body(*refs):
        a = (refs[:n], refs[n:2 * n], scatter) + tuple(refs[2 * n:])
        _exchange_start(*a)
        _exchange_wait(*a)

    any_spec = pl.BlockSpec(memory_space=pl.ANY)
    return pl.pallas_call(
        body, name=name, out_shape=_exchange_shapes(arrays, scatter),
        in_specs=[any_spec] * n, out_specs=[any_spec] * n, scratch_shapes=_exchange_sems(n),
    )(*arrays)


def _call(body, *, name, grid, in_specs, out_specs, out_shape, args, sem, scratch_shapes=(), comm=None):
    if comm is None:
        outs = pl.pallas_call(body, name=name, grid=grid, in_specs=list(in_specs), out_specs=list(out_specs),
                              out_shape=list(out_shape), scratch_shapes=list(scratch_shapes),
                              compiler_params=_params(sem))(*args)
        return list(outs), []
    arrays, scatter = comm
    n_in, n_out, nc, ns = len(in_specs), len(out_specs), len(arrays), len(scratch_shapes)

    def hosted(*refs):
        ins, cins = refs[:n_in], refs[n_in:n_in + nc]
        outs, couts = refs[n_in + nc:n_in + nc + n_out], refs[n_in + nc + n_out:n_in + 2 * nc + n_out]
        scratch = refs[n_in + 2 * nc + n_out:]
        ex = (cins, couts, scatter) + tuple(scratch[ns:])
        first = pl.program_id(0) == 0
        last = pl.program_id(0) == grid[0] - 1
        for ax in range(1, len(grid)):
            first = first & (pl.program_id(ax) == 0)
            last = last & (pl.program_id(ax) == grid[ax] - 1)

        @pl.when(first)
        def _():
            _exchange_start(*ex)

        body(*ins, *outs, *scratch[:ns])

        @pl.when(last)
        def _():
            _exchange_wait(*ex)

    any_spec = pl.BlockSpec(memory_space=pl.ANY)
    res = pl.pallas_call(
        hosted, name=name, grid=grid, in_specs=list(in_specs) + [any_spec] * nc,
        out_specs=list(out_specs) + [any_spec] * nc, out_shape=list(out_shape) + _exchange_shapes(arrays, scatter),
        scratch_shapes=list(scratch_shapes) + _exchange_sems(nc),
        compiler_params=_params(tuple("arbitrary" for _ in grid)))(*args, *arrays)
    return list(res[:n_out]), list(res[n_out:])


def _mod_partial(c_all, w_ada, b_ada_cols):
    def body(c_ref, w_ref, b_ref, o_ref):
        cv = c_ref[...]
        s = (cv * _sigmoid(cv)).astype(BF16)
        o_ref[...] = _dot(s, w_ref[...].astype(BF16)) + b_ref[...]

    ncol = w_ada.shape[1]
    return pl.pallas_call(
        body, name="mod_partial", out_shape=jax.ShapeDtypeStruct((N_DEV, ncol), F32),
        in_specs=[_full(c_all.shape), _full(w_ada.shape), _full(b_ada_cols.shape)],
        out_specs=_full((N_DEV, ncol)), grid=(1,), compiler_params=_params(("arbitrary",)),
    )(c_all, w_ada, b_ada_cols)


def _bias_table(rel_bias, bucket):
    def body(rb_ref, bk_ref, o_ref):
        h = pl.program_id(0)
        bk = bk_ref[...]
        acc = jnp.zeros((BLK, 2 * BLK), F32)
        for b in range(N_BUCKETS):
            acc = jnp.where(bk == b, rb_ref[b, h], acc)
        dist = (lax.broadcasted_iota(jnp.int32, (BLK, 2 * BLK), 0) + BLK
                - lax.broadcasted_iota(jnp.int32, (BLK, 2 * BLK), 1))
        o_ref[0] = jnp.where((dist >= 0) & (dist < BLK), acc, NEG_INF)

    return pl.pallas_call(
        body, name="bias_table", out_shape=jax.ShapeDtypeStruct((N_HEADS, BLK, 2 * BLK), F32),
        in_specs=[pl.BlockSpec(memory_space=pltpu.SMEM), _full((BLK, 2 * BLK))],
        out_specs=pl.BlockSpec((1, BLK, 2 * BLK), lambda h: (h, 0, 0)), grid=(N_HEADS,),
        compiler_params=_params(("arbitrary",)),
    )(rel_bias, bucket)


def _bias_grad(dl_acc, bucket):
    def body(dl_ref, bk_ref, o_ref):
        bk = bk_ref[...]
        dl = dl_ref[0]
        lane = lax.broadcasted_iota(jnp.int32, (1, 128), 1)
        row = jnp.zeros((1, 128), F32)
        for b in range(N_BUCKETS):
            s = jnp.sum(jnp.sum(jnp.where(bk == b, dl, 0.0), axis=1, keepdims=True), axis=0, keepdims=True)
            row = jnp.where(lane == b, s, row)
        o_ref[0] = row

    return pl.pallas_call(
        body, name="bias_grad", out_shape=jax.ShapeDtypeStruct((N_HEADS, 1, 128), F32),
        in_specs=[pl.BlockSpec((1, BLK, 2 * BLK), lambda h: (h, 0, 0)), _full((BLK, 2 * BLK))],
        out_specs=pl.BlockSpec((1, 1, 128), lambda h: (h, 0, 0)), grid=(N_HEADS,),
        compiler_params=_params(("arbitrary",)),
    )(dl_acc, bucket)


def _inproj(x, sc1, sh1, w_in_t, b_in, tm):
    t, d = x.shape
    n = w_in_t.shape[0]

    def body(x_ref, sc_ref, sh_ref, w_ref, b_ref, proj_ref, h_ref):
        sc, sh = sc_ref[...], sh_ref[...]

        def head(rows):
            h_ref[rows, :] = (x_ref[rows, :] * (1.0 + sc) + sh).astype(BF16)

        _row_passes(tm, head)
        proj_ref[...] = _dot_nt(h_ref[...], w_ref[...]) + b_ref[...]

    row = lambda w: pl.BlockSpec((tm, w), lambda i: (i, 0))
    return pl.pallas_call(
        body, name="inproj", grid=(t // tm,),
        out_shape=[jax.ShapeDtypeStruct((t, n), F32), jax.ShapeDtypeStruct((t, d), BF16)],
        in_specs=[row(d), _full((1, d)), _full((1, d)), _full((n, d)), _full((1, n))],
        out_specs=[row(n), row(d)], compiler_params=_params(("parallel",)),
    )(x, sc1, sh1, w_in_t, b_in)


def _half_masks():
    lo_q = lax.broadcasted_iota(jnp.int32, (BLK, 128), 1) < 64
    lo_k = lax.broadcasted_iota(jnp.int32, (2 * BLK, 128), 1) < 64
    return lo_q, lo_k


def _head_place(h):
    return h // 2, h % 2, h // 4


def _attn_heads(q, kk, vv, bias_ref, sinks_ref, n):
    lo_q, lo_k = _half_masks()
    kkb, kksb = kk.astype(BF16), pltpu.roll(kk, 64, 1).astype(BF16)
    vvb, vvsb = vv.astype(BF16), pltpu.roll(vv, 64, 1).astype(BF16)
    n0mask = (n == 0) & (lax.broadcasted_iota(jnp.int32, (BLK, 2 * BLK), 1) < BLK)
    chunks, probs = [], []
    for j in range(4):
        qc = q[:, 128 * j:128 * (j + 1)]
        acc = jnp.zeros((BLK, 128), F32)
        for pos in range(2):
            h = 2 * j + pos
            direct = (h // 4) == pos
            mq = lo_q if pos == 0 else jnp.logical_not(lo_q)
            mk = lo_k if pos == 0 else jnp.logical_not(lo_k)
            qm = jnp.where(mq, qc, 0.0).astype(BF16)
            logit = _dot_nt(qm, kkb if direct else kksb) * (HEAD_DIM ** -0.5) + bias_ref[h]
            logit = jnp.where(n0mask, NEG_INF, logit)
            sk = sinks_ref[h]
            m = jnp.maximum(jnp.max(logit, axis=1, keepdims=True), sk)
            e = jnp.exp(logit - m)
            es = jnp.exp(sk - m)
            den = jnp.sum(e, axis=1, keepdims=True) + es
            p = e / den
            vm = jnp.where(mk, vvb if direct else vvsb, jnp.zeros_like(vvb))
            acc = acc + _dot(p.astype(BF16), vm)
            probs.append((p, es / den))
        chunks.append(acc)
    return jnp.concatenate(chunks, axis=1), probs


def _gmlp_block(gu, gv, lng, lnb, ws_ref, bfull):
    lo_q, _ = _half_masks()
    u, du = _gelu_parts(gu)
    a, da = _gelu_parts(gv)
    mu = _seg_mean64(a)
    ac = a - mu
    rstd = lax.rsqrt(_seg_mean64(ac * ac) + LN_EPS)
    vhat = ac * rstd
    vn = vhat * lng + lnb
    chunks = []
    for j in range(4):
        vc = vn[:, 128 * j:128 * (j + 1)]
        acc = jnp.zeros((BLK, 128), F32)
        for pos in range(2):
            mq = lo_q if pos == 0 else jnp.logical_not(lo_q)
            acc = acc + _dot(ws_ref[2 * j + pos], jnp.where(mq, vc, 0.0).astype(BF16))
        chunks.append(acc)
    ms = jnp.concatenate(chunks, axis=1) + bfull
    return u * ms, (u, du, da, vhat, rstd, vn, ms)


def _mix_in_specs(nb):
    return [pl.BlockSpec((BLK, IN_W), lambda n: (n, 0)),
            pl.BlockSpec((BLK, 2 * KV_W), lambda n: (jnp.maximum(n - 1, 0), ATTN_W // (2 * KV_W))),
            _full((N_HEADS, BLK, 2 * BLK)),
            pl.BlockSpec(memory_space=pltpu.SMEM),
            _full((1, GMLP_W)), _full((1, GMLP_W)),
            _full((N_GROUPS, BLK, BLK)), _full((BLK, GMLP_W)),
            _full((1, ATTN_W)), _full((1, GMLP_W))]


def _split_proj(proj_ref, kvp_ref):
    q = proj_ref[:, 0:ATTN_W]
    k = proj_ref[:, ATTN_W:ATTN_W + KV_W]
    v = proj_ref[:, ATTN_W + KV_W:ATTN_W + 2 * KV_W]
    gu = proj_ref[:, ATTN_W + 2 * KV_W:ATTN_W + 2 * KV_W + GMLP_W]
    gv = proj_ref[:, ATTN_W + 2 * KV_W + GMLP_W:IN_W]
    kk = jnp.concatenate([kvp_ref[:, 0:KV_W], k], axis=0)
    vv = jnp.concatenate([kvp_ref[:, KV_W:2 * KV_W], v], axis=0)
    return q, kk, vv, gu, gv


def _mix_fwd(proj, bias, sinks, lng, lnb, ws, bfull, aog, gog, comm):
    t = proj.shape[0]
    nb = t // BLK

    def body(proj_ref, kvp_ref, bias_ref, sinks_ref, lng_ref, lnb_ref, ws_ref, bfull_ref, aog_ref, gog_ref, out_ref):
        n = pl.program_id(0)
        q, kk, vv, gu, gv = _split_proj(proj_ref, kvp_ref)
        attn, _ = _attn_heads(q, kk, vv, bias_ref, sinks_ref, n)
        gm, _ = _gmlp_block(gu, gv, lng_ref[...], lnb_ref[...], ws_ref, bfull_ref[...])
        out_ref[:, 0:ATTN_W] = _rms(attn, aog_ref[...])[0].astype(BF16)
        out_ref[:, ATTN_W:ATTN_W + GMLP_W] = _rms(gm, gog_ref[...])[0].astype(BF16)

    return _call(
        body, name="mix_fwd", grid=(nb,), out_shape=[jax.ShapeDtypeStruct((t, D_MODEL), BF16)],
        in_specs=_mix_in_specs(nb), out_specs=[pl.BlockSpec((BLK, D_MODEL), lambda n: (n, 0))],
        sem=("parallel",), comm=comm, args=(proj, proj, bias, sinks, lng, lnb, ws, bfull, aog, gog))


def _mix_bwd(proj, bias, sinks, lng, lnb, ws, ws_t, bfull, aog, gog, dy, w_out, comm):
    t = proj.shape[0]
    nb = t // BLK

    def body(proj_ref, kvp_ref, bias_ref, sinks_ref, lng_ref, lnb_ref, ws_ref, bfull_ref, aog_ref, gog_ref,
             wst_ref, dy_ref, wout_ref,
             dproj_ref, dkvn_ref, dl_ref, dsink_ref, dlng_ref, dlnb_ref, dws_ref, dbs_ref, daog_ref, dgog_ref):
        n = pl.program_id(0)

        @pl.when(n == 0)
        def _():
            for r in (dl_ref, dsink_ref, dlng_ref, dlnb_ref, dws_ref, dbs_ref, daog_ref, dgog_ref):
                r[...] = jnp.zeros_like(r)

        lo_q, lo_k = _half_masks()
        q, kk, vv, gu, gv = _split_proj(proj_ref, kvp_ref)
        dmix = _dot_nt(dy_ref[...], wout_ref[...])
        dma, dmg = dmix[:, 0:ATTN_W], dmix[:, ATTN_W:ATTN_W + GMLP_W]

        attn, probs = _attn_heads(q, kk, vv, bias_ref, sinks_ref, n)
        aog = aog_ref[...]
        _, r_a = _rms(attn, aog)
        daog_ref[...] += _rowsum8(dma * attn * r_a)
        dattn = _rms_bwd(dma, attn, r_a, aog)

        kkb, kksb = kk.astype(BF16), pltpu.roll(kk, 64, 1).astype(BF16)
        vvb, vvsb = vv.astype(BF16), pltpu.roll(vv, 64, 1).astype(BF16)
        lane = lax.broadcasted_iota(jnp.int32, (BLK, 128), 1)
        dk_d = jnp.zeros((2 * BLK, 128), F32)
        dk_s = jnp.zeros((2 * BLK, 128), F32)
        dv_d = jnp.zeros((2 * BLK, 128), F32)
        dv_s = jnp.zeros((2 * BLK, 128), F32)
        dsink = jnp.zeros((BLK, 128), F32)
        dq_chunks = []
        for j in range(4):
            qc = q[:, 128 * j:128 * (j + 1)]
            doc = dattn[:, 128 * j:128 * (j + 1)]
            dq = jnp.zeros((BLK, 128), F32)
            for pos in range(2):
                h = 2 * j + pos
                direct = (h // 4) == pos
                mq = lo_q if pos == 0 else jnp.logical_not(lo_q)
                mk = lo_k if pos == 0 else jnp.logical_not(lo_k)
                p, psink = probs[h]
                qm = jnp.where(mq, qc, 0.0).astype(BF16)
                dom = jnp.where(mq, doc, 0.0).astype(BF16)
                dp = _dot_nt(dom, vvb if direct else vvsb)
                rs = jnp.sum(p * dp, axis=1, keepdims=True)
                dl = p * (dp - rs)
                dl_ref[h] += dl
                dsink = dsink + jnp.where(lane == h, -psink * rs, 0.0)
                dls = (dl * (HEAD_DIM ** -0.5)).astype(BF16)
                km = jnp.where(mk, kkb if direct else kksb, jnp.zeros_like(kkb))
                dq = dq + _dot(dls, km)
                dk_h = _dot_tn(dls, qm)
                dv_h = _dot_tn(p.astype(BF16), dom)
                if direct:
                    dk_d, dv_d = dk_d + dk_h, dv_d + dv_h
                else:
                    dk_s, dv_s = dk_s + dk_h, dv_s + dv_h
            dq_chunks.append(dq)
        dsink_ref[...] += dsink
        dk = dk_d + pltpu.roll(dk_s, 64, 1)
        dv = dv_d + pltpu.roll(dv_s, 64, 1)
        for j in range(4):
            dproj_ref[:, 128 * j:128 * (j + 1)] = dq_chunks[j]
        dproj_ref[:, ATTN_W:ATTN_W + KV_W] = dk[BLK:2 * BLK]
        dproj_ref[:, ATTN_W + KV_W:ATTN_W + 2 * KV_W] = dv[BLK:2 * BLK]
        dkvn_ref[:, 0:KV_W] = dk[0:BLK]
        dkvn_ref[:, KV_W:2 * KV_W] = dv[0:BLK]

        lng = lng_ref[...]
        gog = gog_ref[...]
        gm, (u, du, da, vhat, rstd, vn, ms) = _gmlp_block(gu, gv, lng, lnb_ref[...], ws_ref, bfull_ref[...])
        _, r_g = _rms(gm, gog)
        dgog_ref[...] += _rowsum8(dmg * gm * r_g)
        dgm = _rms_bwd(dmg, gm, r_g, gog)
        dproj_ref[:, ATTN_W + 2 * KV_W:ATTN_W + 2 * KV_W + GMLP_W] = dgm * ms * du
        dms = dgm * u
        dbs_ref[...] += dms
        dvn_chunks = []
        for j in range(4):
            dmc = dms[:, 128 * j:128 * (j + 1)]
            vcb = vn[:, 128 * j:128 * (j + 1)].astype(BF16)
            acc = jnp.zeros((BLK, 128), F32)
            for pos in range(2):
                g = 2 * j + pos
                mq = lo_q if pos == 0 else jnp.logical_not(lo_q)
                dm = jnp.where(mq, dmc, 0.0).astype(BF16)
                dws_ref[g] += _dot_nt(dm, vcb)
                acc = acc + _dot(wst_ref[g], dm)
            dvn_chunks.append(acc)
        dvn = jnp.concatenate(dvn_chunks, axis=1)
        dlng_ref[...] += _rowsum8(dvn * vhat)
        dlnb_ref[...] += _rowsum8(dvn)
        dvh = dvn * lng
        dact = rstd * (dvh - _seg_mean64(dvh) - vhat * _seg_mean64(dvh * vhat))
        dproj_ref[:, ATTN_W + 2 * KV_W + GMLP_W:IN_W] = dact * da

    acc8 = lambda w: jax.ShapeDtypeStruct((8, w), F32)
    out_shape = [jax.ShapeDtypeStruct((t, IN_W), F32), jax.ShapeDtypeStruct((t, 2 * KV_W), F32),
                 jax.ShapeDtypeStruct((N_HEADS, BLK, 2 * BLK), F32), jax.ShapeDtypeStruct((BLK, 128), F32),
                 acc8(GMLP_W), acc8(GMLP_W), jax.ShapeDtypeStruct((N_GROUPS, BLK, BLK), F32),
                 jax.ShapeDtypeStruct((BLK, GMLP_W), F32), acc8(ATTN_W), acc8(GMLP_W)]
    out_specs = [pl.BlockSpec((BLK, IN_W), lambda n: (n, 0)),
                 pl.BlockSpec((BLK, 2 * KV_W), lambda n: ((n + nb - 1) % nb, 0)),
                 _full((N_HEADS, BLK, 2 * BLK)), _full((BLK, 128)), _full((8, GMLP_W)), _full((8, GMLP_W)),
                 _full((N_GROUPS, BLK, BLK)), _full((BLK, GMLP_W)), _full((8, ATTN_W)), _full((8, GMLP_W))]
    in_specs = _mix_in_specs(nb) + [_full((N_GROUPS, BLK, BLK)),
                                    pl.BlockSpec((BLK, D_MODEL), lambda n: (n, 0)),
                                    _full((D_MODEL, D_MODEL))]
    return _call(
        body, name="mix_bwd", grid=(nb,), out_shape=out_shape, in_specs=in_specs, out_specs=out_specs,
        sem=("arbitrary",), comm=comm, args=(proj, proj, bias, sinks, lng, lnb, ws, bfull, aog, gog, ws_t, dy, w_out))


HALF = 64
ROWS = 32


def _lane_lo(rows):
    return lax.broadcasted_iota(jnp.int32, (rows, 128), 1) < 64


def _mix_stage_kv(proj_ref, kvp_ref, s):
    lo = _lane_lo(2 * BLK)
    for name, col in (("k", ATTN_W), ("v", ATTN_W + KV_W)):
        cur = jnp.concatenate([kvp_ref[:, col - ATTN_W:col - ATTN_W + KV_W], proj_ref[:, col:col + KV_W]], axis=0)
        plain, swapped = cur.astype(BF16), pltpu.roll(cur, 64, 1).astype(BF16)
        zero = jnp.zeros_like(plain)
        for g in range(2):
            dup = jnp.where(lo, plain, swapped) if g == 0 else jnp.where(lo, swapped, plain)
            s[name + "d"][g] = dup
            s[name + "m"][g] = jnp.concatenate([jnp.where(lo, dup, zero), jnp.where(lo, zero, dup)], axis=0)


def _group_rows(ref, g):
    return ref[4 * g:4 * g + 4].reshape(4 * BLK, ref.shape[2])


def _pair_rows(ref, g):
    return jnp.concatenate([jnp.concatenate([ref[4 * g + 2 * c], ref[4 * g + 2 * c + 1]], axis=1) for c in range(2)],
                           axis=0)


def _mask_heads(src_ref, dst_ref):
    lo = _lane_lo(BLK)
    for j in range(4):
        chunk = src_ref[:, 128 * j:128 * (j + 1)]
        dst_ref[2 * j] = jnp.where(lo, chunk, 0.0).astype(BF16)
        dst_ref[2 * j + 1] = jnp.where(lo, 0.0, chunk).astype(BF16)


def _mix_stage_attn(proj_ref, bias_ref, sinks_ref, n, s, keep):
    _mask_heads(proj_ref, s["qm"])
    for g in range(2):
        s["lg"][g] = _dot_nt(_group_rows(s["qm"], g), s["kd"][g])
    n0mask = (n == 0) & (lax.broadcasted_iota(jnp.int32, (HALF, 2 * BLK), 1) < BLK)
    for h in range(N_HEADS):
        sk = sinks_ref[h]
        for hf in range(BLK // HALF):
            rows = slice(HALF * hf, HALF * (hf + 1))
            grows = slice(BLK * (h % 4) + HALF * hf, BLK * (h % 4) + HALF * (hf + 1))
            logit = s["lg"][h // 4, grows, :] * (HEAD_DIM ** -0.5) + bias_ref[h, rows, :]
            logit = jnp.where(n0mask, NEG_INF, logit)
            m = jnp.maximum(jnp.max(logit, axis=1, keepdims=True), sk)
            e = jnp.exp(logit - m)
            es = jnp.exp(sk - m)
            inv = 1.0 / (jnp.sum(e, axis=1, keepdims=True) + es)
            p = e * inv
            s["pb"][h, rows, :] = p.astype(BF16)
            if keep:
                s["p"][h, rows, :] = p
                s["psink"][h, rows, :] = es * inv
    for g in range(2):
        out = _dot(_pair_rows(s["pb"], g), s["vm"][g])
        s["attn"][:, 256 * g:256 * g + 128] = out[0:BLK]
        s["attn"][:, 256 * g + 128:256 * g + 256] = out[BLK:2 * BLK]


def _mix_stage_gmlp_pre(proj_ref, lng, lnb, s, keep):
    c0 = ATTN_W + 2 * KV_W
    for r0 in range(0, BLK, ROWS):
        rows = slice(r0, r0 + ROWS)
        u, du = _gelu_parts(proj_ref[rows, c0:c0 + GMLP_W])
        a, da = _gelu_parts(proj_ref[rows, c0 + GMLP_W:c0 + 2 * GMLP_W])
        ac = a - _seg_mean64(a)
        rstd = lax.rsqrt(_seg_mean64(ac * ac) + LN_EPS)
        vhat = ac * rstd
        s["u"][rows, :] = u
        s["vnb"][rows, :] = (vhat * lng + lnb).astype(BF16)
        if keep:
            s["du"][rows, :] = du
            s["da"][rows, :] = da
            s["vhat"][rows, :] = vhat
            s["rstd"][rows, :] = rstd


def _stack_halves(chunk):
    lo = _lane_lo(BLK)
    zero = jnp.zeros_like(chunk)
    return jnp.concatenate([jnp.where(lo, chunk, zero), jnp.where(lo, zero, chunk)], axis=0)


def _mix_stage_gmlp_mix(ws2_ref, bfull_ref, s):
    for j in range(4):
        cols = slice(128 * j, 128 * (j + 1))
        s["ms"][:, cols] = _dot(ws2_ref[j], _stack_halves(s["vnb"][:, cols])) + bfull_ref[:, cols]


def _mix_scratch(keep):
    f32 = lambda *shape: pltpu.VMEM(shape, F32)
    b16 = lambda *shape: pltpu.VMEM(shape, BF16)
    names = dict(kd=b16(2, 2 * BLK, 128), vd=b16(2, 2 * BLK, 128), km=b16(2, 4 * BLK, 128), vm=b16(2, 4 * BLK, 128),
                 qm=b16(N_HEADS, BLK, 128), lg=f32(2, 4 * BLK, 2 * BLK), pb=b16(N_HEADS, BLK, 2 * BLK),
                 attn=f32(BLK, ATTN_W), u=f32(BLK, GMLP_W), vnb=b16(BLK, GMLP_W), ms=f32(BLK, GMLP_W))
    if keep:
        names.update(dom=b16(N_HEADS, BLK, 128), p=f32(N_HEADS, BLK, 2 * BLK),
                     dls=b16(N_HEADS, BLK, 2 * BLK), psink=f32(N_HEADS, BLK, 1),
                     dattn=f32(BLK, ATTN_W), dmix=f32(BLK, D_MODEL), du=f32(BLK, GMLP_W), da=f32(BLK, GMLP_W),
                     vhat=f32(BLK, GMLP_W), rstd=f32(BLK, GMLP_W), dmsb=b16(BLK, GMLP_W), dvn=f32(BLK, GMLP_W))
    return list(names), list(names.values())


def _mix_specs():
    return [pl.BlockSpec((BLK, IN_W), lambda n: (n, 0)),
            pl.BlockSpec((BLK, 2 * KV_W), lambda n: (jnp.maximum(n - 1, 0), ATTN_W // (2 * KV_W))),
            _full((N_HEADS, BLK, 2 * BLK)),
            pl.BlockSpec(memory_space=pltpu.SMEM),
            _full((1, GMLP_W)), _full((1, GMLP_W)),
            _full((N_GROUPS // 2, BLK, 2 * BLK)), _full((BLK, GMLP_W)),
            _full((1, ATTN_W)), _full((1, GMLP_W))]


def _mix_fwd(proj, bias, sinks, lng, lnb, ws2, bfull, aog, gog, comm):
    t = proj.shape[0]
    names, shapes = _mix_scratch(False)

    def body(proj_ref, kvp_ref, bias_ref, sinks_ref, lng_ref, lnb_ref, ws2_ref, bfull_ref, aog_ref, gog_ref,
             out_ref, *scratch):
        s = dict(zip(names, scratch))
        n = pl.program_id(0)
        _mix_stage_kv(proj_ref, kvp_ref, s)
        _mix_stage_attn(proj_ref, bias_ref, sinks_ref, n, s, False)
        _mix_stage_gmlp_pre(proj_ref, lng_ref[...], lnb_ref[...], s, False)
        _mix_stage_gmlp_mix(ws2_ref, bfull_ref, s)
        for r0 in range(0, BLK, ROWS):
            rows = slice(r0, r0 + ROWS)
            out_ref[rows, 0:ATTN_W] = _rms(s["attn"][rows, :], aog_ref[...])[0].astype(BF16)
            out_ref[rows, ATTN_W:ATTN_W + GMLP_W] = _rms(s["u"][rows, :] * s["ms"][rows, :], gog_ref[...])[0].astype(BF16)

    return _call(
        body, name="mix_fwd", grid=(t // BLK,), out_shape=[jax.ShapeDtypeStruct((t, D_MODEL), BF16)],
        in_specs=_mix_specs(), out_specs=[pl.BlockSpec((BLK, D_MODEL), lambda n: (n, 0))], scratch_shapes=shapes,
        sem=("parallel",), comm=comm, args=(proj, proj, bias, sinks, lng, lnb, ws2, bfull, aog, gog))


def _mix_bwd(proj, bias, sinks, lng, lnb, ws2, wst2, bfull, aog, gog, dy, w_out, comm):
    t = proj.shape[0]
    nb = t // BLK
    names, shapes = _mix_scratch(True)
    c_gu = ATTN_W + 2 * KV_W

    def body(proj_ref, kvp_ref, bias_ref, sinks_ref, lng_ref, lnb_ref, ws2_ref, bfull_ref, aog_ref, gog_ref,
             wst2_ref, dy_ref, wout_ref,
             dproj_ref, dkvn_ref, dl_ref, dsink_ref, dlng_ref, dlnb_ref, dws_ref, dbs_ref, daog_ref, dgog_ref,
             *scratch):
        s = dict(zip(names, scratch))
        n = pl.program_id(0)

        @pl.when(n == 0)
        def _():
            for r in (dl_ref, dsink_ref, dlng_ref, dlnb_ref, dws_ref, dbs_ref, daog_ref, dgog_ref):
                r[...] = jnp.zeros_like(r)

        s["dmix"][...] = _dot_nt(dy_ref[...], wout_ref[...])
        _mix_stage_kv(proj_ref, kvp_ref, s)
        _mix_stage_attn(proj_ref, bias_ref, sinks_ref, n, s, True)
        lng = lng_ref[...]
        _mix_stage_gmlp_pre(proj_ref, lng, lnb_ref[...], s, True)
        _mix_stage_gmlp_mix(ws2_ref, bfull_ref, s)

        aog, gog = aog_ref[...], gog_ref[...]
        for r0 in range(0, BLK, ROWS):
            rows = slice(r0, r0 + ROWS)
            attn, dma = s["attn"][rows, :], s["dmix"][rows, 0:ATTN_W]
            _, r_a = _rms(attn, aog)
            daog_ref[...] += _rowsum8(dma * attn * r_a)
            s["dattn"][rows, :] = _rms_bwd(dma, attn, r_a, aog)
            u, ms, dmg = s["u"][rows, :], s["ms"][rows, :], s["dmix"][rows, ATTN_W:ATTN_W + GMLP_W]
            gm = u * ms
            _, r_g = _rms(gm, gog)
            dgog_ref[...] += _rowsum8(dmg * gm * r_g)
            dgm = _rms_bwd(dmg, gm, r_g, gog)
            dproj_ref[rows, c_gu:c_gu + GMLP_W] = dgm * ms * s["du"][rows, :]
            dms = dgm * u
            dbs_ref[rows, :] += dms
            s["dmsb"][rows, :] = dms.astype(BF16)

        _mask_heads(s["dattn"], s["dom"])
        for g in range(2):
            s["lg"][g] = _dot_nt(_group_rows(s["dom"], g), s["vd"][g])
        lane = lax.broadcasted_iota(jnp.int32, (HALF, 128), 1)
        for hf in range(BLK // HALF):
            rows = slice(HALF * hf, HALF * (hf + 1))
            dsink = jnp.zeros((HALF, 128), F32)
            for h in range(N_HEADS):
                grows = slice(BLK * (h % 4) + HALF * hf, BLK * (h % 4) + HALF * (hf + 1))
                dp = s["lg"][h // 4, grows, :]
                p = s["p"][h, rows, :]
                rs = jnp.sum(p * dp, axis=1, keepdims=True)
                dl = p * (dp - rs)
                dl_ref[h, rows, :] += dl
                dsink = dsink + jnp.where(lane == h, -s["psink"][h, rows, :] * rs, 0.0)
                s["dls"][h, rows, :] = (dl * (HEAD_DIM ** -0.5)).astype(BF16)
            dsink_ref[rows, :] += dsink
        for g in range(2):
            dq = _dot(_pair_rows(s["dls"], g), s["km"][g])
            dproj_ref[:, 256 * g:256 * g + 128] = dq[0:BLK]
            dproj_ref[:, 256 * g + 128:256 * g + 256] = dq[BLK:2 * BLK]
        lo_k = _lane_lo(2 * BLK)
        for col, lhs, rhs in ((0, "dls", "qm"), (KV_W, "pb", "dom")):
            raw = [_dot_tn(_group_rows(s[lhs], g), _group_rows(s[rhs], g)) for g in range(2)]
            both = [r + pltpu.roll(r, 64, 1) for r in raw]
            dkv = jnp.where(lo_k, both[0], both[1])
            dproj_ref[:, ATTN_W + col:ATTN_W + col + KV_W] = dkv[BLK:2 * BLK]
            dkvn_ref[:, col:col + KV_W] = dkv[0:BLK]

        for j in range(4):
            cols = slice(128 * j, 128 * (j + 1))
            dm2 = _stack_halves(s["dmsb"][:, cols])
            vnb = s["vnb"][:, cols]
            for pos in range(2):
                dws_ref[2 * j + pos] += _dot_nt(dm2[BLK * pos:BLK * (pos + 1)], vnb)
            s["dvn"][:, cols] = _dot(wst2_ref[j], dm2)
        for r0 in range(0, BLK, ROWS):
            rows = slice(r0, r0 + ROWS)
            dvn, vhat = s["dvn"][rows, :], s["vhat"][rows, :]
            dlng_ref[...] += _rowsum8(dvn * vhat)
            dlnb_ref[...] += _rowsum8(dvn)
            dvh = dvn * lng
            dact = s["rstd"][rows, :] * (dvh - _seg_mean64(dvh) - vhat * _seg_mean64(dvh * vhat))
            dproj_ref[rows, c_gu + GMLP_W:IN_W] = dact * s["da"][rows, :]

    acc8 = lambda w: jax.ShapeDtypeStruct((8, w), F32)
    out_shape = [jax.ShapeDtypeStruct((t, IN_W), F32), jax.ShapeDtypeStruct((t, 2 * KV_W), F32),
                 jax.ShapeDtypeStruct((N_HEADS, BLK, 2 * BLK), F32), jax.ShapeDtypeStruct((BLK, 128), F32),
                 acc8(GMLP_W), acc8(GMLP_W), jax.ShapeDtypeStruct((N_GROUPS, BLK, BLK), F32),
                 jax.ShapeDtypeStruct((BLK, GMLP_W), F32), acc8(ATTN_W), acc8(GMLP_W)]
    out_specs = [pl.BlockSpec((BLK, IN_W), lambda n: (n, 0)),
                 pl.BlockSpec((BLK, 2 * KV_W), lambda n: ((n + nb - 1) % nb, 0)),
                 _full((N_HEADS, BLK, 2 * BLK)), _full((BLK, 128)), _full((8, GMLP_W)), _full((8, GMLP_W)),
                 _full((N_GROUPS, BLK, BLK)), _full((BLK, GMLP_W)), _full((8, ATTN_W)), _full((8, GMLP_W))]
    in_specs = _mix_specs() + [_full((N_GROUPS // 2, BLK, 2 * BLK)),
                               pl.BlockSpec((BLK, D_MODEL), lambda n: (n, 0)),
                               _full((D_MODEL, D_MODEL))]
    return _call(
        body, name="mix_bwd", grid=(nb,), out_shape=out_shape, in_specs=in_specs, out_specs=out_specs,
        scratch_shapes=shapes, sem=("arbitrary",), comm=comm,
        args=(proj, proj, bias, sinks, lng, lnb, ws2, bfull, aog, gog, wst2, dy, w_out))


def _outproj(mixed, w_out, x, g1, ln1g, ln1b, sc2, sh2, tm):
    t, d = x.shape

    def body(mx_ref, w_ref, x_ref, g1_ref, lg_ref, lb_ref, sc_ref, sh_ref, y_ref, x1_ref, h2_ref):
        y_ref[...] = _dot(mx_ref[...], w_ref[...])
        g1, lg, lb, sc, sh = g1_ref[...], lg_ref[...], lb_ref[...], sc_ref[...], sh_ref[...]

        def tail(rows):
            xhat, _ = _ln_stats(ALPHA * x_ref[rows, :] + g1 * y_ref[rows, :])
            x1 = xhat * lg + lb
            x1_ref[rows, :] = x1
            h2_ref[rows, :] = (x1 * (1.0 + sc) + sh).astype(BF16)

        _row_passes(tm, tail)

    row = pl.BlockSpec((tm, d), lambda i: (i, 0))
    vec = _full((1, d))
    return pl.pallas_call(
        body, name="outproj", grid=(t // tm,),
        out_shape=[jax.ShapeDtypeStruct((t, d), F32), jax.ShapeDtypeStruct((t, d), F32),
                   jax.ShapeDtypeStruct((t, d), BF16)],
        in_specs=[row, _full((d, d)), row, vec, vec, vec, vec, vec], out_specs=[row, row, row],
        compiler_params=_params(("parallel",)),
    )(mixed, w_out, x, g1, ln1g, ln1b, sc2, sh2)


def _ffn_up(h2, w_gu_t, tm, tn, comm):
    t, d = h2.shape
    nff = D_FF // tn

    def body(h_ref, wg_ref, wu_ref, dsu_ref, sg_ref, act_ref, g_ref, u_ref):
        h = h_ref[...]
        g_ref[...] = _dot_nt(h, wg_ref[...])
        u_ref[...] = _dot_nt(h, wu_ref[...])

        def tail(rows):
            g, u = g_ref[rows, :], u_ref[rows, :]
            s = _sigmoid(g)
            sg = g * s
            dsu_ref[rows, :] = (u * (s * (1.0 + g * (1.0 - s)))).astype(BF16)
            sg_ref[rows, :] = sg.astype(BF16)
            act_ref[rows, :] = (sg * u).astype(BF16)

        _row_passes(tm, tail)

    out = pl.BlockSpec((tm, tn), lambda j, i: (i, j))
    shp = jax.ShapeDtypeStruct((t, D_FF), BF16)
    return _call(
        body, name="ffn_up", grid=(nff, t // tm), out_shape=[shp, shp, shp],
        in_specs=[pl.BlockSpec((tm, d), lambda j, i: (i, 0)),
                  pl.BlockSpec((tn, d), lambda j, i: (j, 0)),
                  pl.BlockSpec((tn, d), lambda j, i: (j + nff, 0))],
        out_specs=[out, out, out], scratch_shapes=[pltpu.VMEM((tm, tn), F32), pltpu.VMEM((tm, tn), F32)],
        sem=("parallel", "parallel"), comm=comm, args=(h2, w_gu_t, w_gu_t))


def _ffn_down(act, w_down, x1, target, g2, ln2g, ln2b, tm):
    t, d = x1.shape

    def body(act_ref, w_ref, x1_ref, tg_ref, g2_ref, lg_ref, lb_ref,
             dz_ref, dy_ref, loss_ref, dlg_ref, dlb_ref, dg2_ref, y2_ref):
        @pl.when(pl.program_id(0) == 0)
        def _():
            for r in (loss_ref, dlg_ref, dlb_ref, dg2_ref):
                r[...] = jnp.zeros_like(r)

        y2_ref[...] = _dot(act_ref[...], w_ref[...])
        g2, lg, lb = g2_ref[...], lg_ref[...], lb_ref[...]

        def tail(rows):
            y2 = y2_ref[rows, :]
            xhat, rstd = _ln_stats(ALPHA * x1_ref[rows, :] + g2 * y2)
            err = xhat * lg + lb - tg_ref[rows, :]
            loss_ref[...] += _rowsum8(err * err)
            dx2 = err * (1.0 / d)
            dlg_ref[...] += _rowsum8(dx2 * xhat)
            dlb_ref[...] += _rowsum8(dx2)
            dz = _ln_bwd(dx2 * lg, xhat, rstd)
            dg2_ref[...] += _rowsum8(dz * y2)
            dz_ref[rows, :] = dz
            dy_ref[rows, :] = (g2 * dz).astype(BF16)

        _row_passes(tm, tail)

    row = pl.BlockSpec((tm, d), lambda i: (i, 0))
    vec = _full((1, d))
    acc = _full((8, d))
    acc_shape = jax.ShapeDtypeStruct((8, d), F32)
    return pl.pallas_call(
        body, name="ffn_down", grid=(t // tm,),
        out_shape=[jax.ShapeDtypeStruct((t, d), F32), jax.ShapeDtypeStruct((t, d), BF16)] + [acc_shape] * 4,
        in_specs=[pl.BlockSpec((tm, D_FF), lambda i: (i, 0)), _full((D_FF, d)), row, row, vec, vec, vec],
        out_specs=[row, row, acc, acc, acc, acc], scratch_shapes=[pltpu.VMEM((tm, d), F32)],
        compiler_params=_params(("arbitrary",)),
    )(act, w_down, x1, target, g2, ln2g, ln2b)


def _ffn_dact(dy2, w_down, dsu, sg, tm, tn, comm):
    t, d = dy2.shape

    def body(dy_ref, w_ref, dsu_ref, sg_ref, dg_ref, du_ref, dact_ref):
        dact_ref[...] = _dot_nt(dy_ref[...], w_ref[...])

        def tail(rows):
            dact = dact_ref[rows, :]
            dg_ref[rows, :] = (dact * dsu_ref[rows, :].astype(F32)).astype(BF16)
            du_ref[rows, :] = (dact * sg_ref[rows, :].astype(F32)).astype(BF16)

        _row_passes(tm, tail)

    tile = pl.BlockSpec((tm, tn), lambda j, i: (i, j))
    shp = jax.ShapeDtypeStruct((t, D_FF), BF16)
    return _call(
        body, name="ffn_dact", grid=(D_FF // tn, t // tm), out_shape=[shp, shp],
        in_specs=[pl.BlockSpec((tm, d), lambda j, i: (i, 0)), pl.BlockSpec((tn, d), lambda j, i: (j, 0)), tile, tile],
        out_specs=[tile, tile], scratch_shapes=[pltpu.VMEM((tm, tn), F32)],
        sem=("parallel", "parallel"), comm=comm, args=(dy2, w_down, dsu, sg))


def _ffn_dh2(dgate, dup, w_gu_t, x1, x, y, dz2, sc2, g1, ln1g, tm):
    t, d = x1.shape

    def body(dg_ref, du_ref, w_ref, x1_ref, x_ref, y_ref, dz2_ref, sc_ref, g1_ref, lg_ref,
             dz1_ref, dy_ref, dsc_ref, dsh_ref, dlg_ref, dlb_ref, dg1_ref, dh2_ref):
        @pl.when(pl.program_id(0) == 0)
        def _():
            for r in (dsc_ref, dsh_ref, dlg_ref, dlb_ref, dg1_ref):
                r[...] = jnp.zeros_like(r)

        dh2_ref[...] = _dot(dg_ref[...], w_ref[0:D_FF]) + _dot(du_ref[...], w_ref[D_FF:2 * D_FF])
        sc, g1, lg = sc_ref[...], g1_ref[...], lg_ref[...]

        def tail(rows):
            dh2, x1, y = dh2_ref[rows, :], x1_ref[rows, :], y_ref[rows, :]
            dsc_ref[...] += _rowsum8(dh2 * x1)
            dsh_ref[...] += _rowsum8(dh2)
            dx1 = dh2 * (1.0 + sc) + ALPHA * dz2_ref[rows, :]
            xhat, rstd = _ln_stats(ALPHA * x_ref[rows, :] + g1 * y)
            dlg_ref[...] += _rowsum8(dx1 * xhat)
            dlb_ref[...] += _rowsum8(dx1)
            dz1 = _ln_bwd(dx1 * lg, xhat, rstd)
            dg1_ref[...] += _rowsum8(dz1 * y)
            dz1_ref[rows, :] = dz1
            dy_ref[rows, :] = (g1 * dz1).astype(BF16)

        _row_passes(tm, tail)

    row = pl.BlockSpec((tm, d), lambda i: (i, 0))
    wide = pl.BlockSpec((tm, D_FF), lambda i: (i, 0))
    vec = _full((1, d))
    acc = _full((8, d))
    acc_shape = jax.ShapeDtypeStruct((8, d), F32)
    return pl.pallas_call(
        body, name="ffn_dh2", grid=(t // tm,),
        out_shape=[jax.ShapeDtypeStruct((t, d), F32), jax.ShapeDtypeStruct((t, d), BF16)] + [acc_shape] * 5,
        in_specs=[wide, wide, _full((2 * D_FF, d)), row, row, row, row, vec, vec, vec],
        out_specs=[row, row, acc, acc, acc, acc, acc],
        scratch_shapes=[pltpu.VMEM((tm, d), F32)], compiler_params=_params(("arbitrary",)),
    )(dgate, dup, w_gu_t, x1, x, y, dz2, sc2, g1, ln1g)


def _din(dproj, dkvn, w_in_t, x, dz1, sc1, tm, comm):
    t, d = x.shape

    def body(dp_ref, dkv_ref, w_ref, x_ref, dz1_ref, sc_ref, dx_ref, dpb_ref, dbin_ref, dsc_ref, dsh_ref, dh_ref):
        @pl.when(pl.program_id(0) == 0)
        def _():
            for r in (dbin_ref, dsc_ref, dsh_ref):
                r[...] = jnp.zeros_like(r)

        def head(rows):
            dp = jnp.concatenate([dp_ref[rows, 0:ATTN_W], dp_ref[rows, ATTN_W:ATTN_W + 2 * KV_W] + dkv_ref[rows, :],
                                  dp_ref[rows, ATTN_W + 2 * KV_W:IN_W]], axis=1)
            dbin_ref[...] += _rowsum8(dp)
            dpb_ref[rows, :] = dp.astype(BF16)

        _row_passes(tm, head)
        dh_ref[...] = _dot(dpb_ref[...], w_ref[...])
        sc = sc_ref[...]

        def tail(rows):
            dh = dh_ref[rows, :]
            dsc_ref[...] += _rowsum8(dh * x_ref[rows, :])
            dsh_ref[...] += _rowsum8(dh)
            dx_ref[rows, :] = dh * (1.0 + sc) + ALPHA * dz1_ref[rows, :]

        _row_passes(tm, tail)

    row = lambda w: pl.BlockSpec((tm, w), lambda i: (i, 0))
    return _call(
        body, name="din", grid=(t // tm,),
        out_shape=[jax.ShapeDtypeStruct((t, d), F32), jax.ShapeDtypeStruct((t, IN_W), BF16),
                   jax.ShapeDtypeStruct((8, IN_W), F32), jax.ShapeDtypeStruct((8, d), F32),
                   jax.ShapeDtypeStruct((8, d), F32)],
        in_specs=[row(IN_W), row(2 * KV_W), _full((IN_W, d)), row(d), row(d), _full((1, d))],
        out_specs=[row(d), row(IN_W), _full((8, IN_W)), _full((8, d)), _full((8, d))],
        scratch_shapes=[pltpu.VMEM((tm, d), F32)],
        sem=("arbitrary",), comm=comm, args=(dproj, dkvn, w_in_t, x, dz1, sc1))


def _wgrad(name, a, b, tmm, tk, comm=None, a2=None):
    t, m = a.shape
    n = b.shape[1]
    nk = t // tk
    nm = m // tmm

    def body(*refs):
        a_refs, (b_ref, o_ref, acc_ref) = refs[:-3], refs[-3:]
        i, k = pl.program_id(0), pl.program_id(1)
        a_tile = a_refs[0][...] if a2 is None else jnp.where(i < nm, a_refs[0][...], a_refs[1][...])
        part = _dot_tn(a_tile, b_ref[...])

        @pl.when(k == 0)
        def _():
            acc_ref[...] = part

        @pl.when(k > 0)
        def _():
            acc_ref[...] += part

        @pl.when(k == nk - 1)
        def _():
            o_ref[...] = acc_ref[...].astype(BF16)

    if a2 is None:
        a_specs, a_args, n_tiles = [pl.BlockSpec((tk, tmm), lambda i, k: (k, i))], (a,), nm
    else:
        a_specs = [pl.BlockSpec((tk, tmm), lambda i, k: (jnp.where(i < nm, k, 0), jnp.minimum(i, nm - 1))),
                   pl.BlockSpec((tk, tmm), lambda i, k: (jnp.where(i < nm, 0, k), jnp.maximum(i - nm, 0)))]
        a_args, n_tiles = (a, a2), 2 * nm
    (out,), got = _call(
        body, name=name, grid=(n_tiles, nk), out_shape=[jax.ShapeDtypeStruct((n_tiles * tmm, n), BF16)],
        in_specs=a_specs + [pl.BlockSpec((tk, n), lambda i, k: (k, 0))],
        out_specs=[pl.BlockSpec((tmm, n), lambda i, k: (i, 0))],
        scratch_shapes=[pltpu.VMEM((tmm, n), F32)], sem=("parallel", "arbitrary"), comm=comm, args=a_args + (b,))
    return out if comm is None else (out, got)


def _adamw(w, g, m, v):
    m = ADAM_B1 * m + (1.0 - ADAM_B1) * g
    v = ADAM_B2 * v + (1.0 - ADAM_B2) * (g * g)
    m_hat = m / (1.0 - ADAM_B1 ** ADAM_STEP)
    v_hat = v / (1.0 - ADAM_B2 ** ADAM_STEP)
    delta = -ADAM_LR * (m_hat / (jnp.sqrt(v_hat) + ADAM_EPS) + ADAM_WD * w)
    return delta, m, v


def _adam_reduce(name, parts, w, m, v, tr):
    r, cdim = w.shape

    def body(p_ref, w_ref, m_ref, v_ref, g_ref, d_ref, mo_ref, vo_ref):
        g = p_ref[0].astype(F32)
        for s in range(1, N_DEV):
            g = g + p_ref[s].astype(F32)
        d_ref[...], mo_ref[...], vo_ref[...] = _adamw(w_ref[...], g, m_ref[...], v_ref[...])
        g_ref[...] = g

    tile = pl.BlockSpec((tr, cdim), lambda i: (i, 0))
    shp = jax.ShapeDtypeStruct((r, cdim), F32)
    return pl.pallas_call(
        body, name=name, grid=(r // tr,), out_shape=[shp] * 4,
        in_specs=[pl.BlockSpec((N_DEV, tr, cdim), lambda i: (0, i, 0)), tile, tile, tile],
        out_specs=[tile] * 4, compiler_params=_params(("parallel",)),
    )(parts, w, m, v)


def _adam_w_ada(c_all_t, dmod_cols, w, m, v):
    def body(ct_ref, dm_ref, w_ref, m_ref, v_ref, g_ref, d_ref, mo_ref, vo_ref):
        ct = ct_ref[...]
        s = (ct * _sigmoid(ct)).astype(BF16)
        g = _dot(s, dm_ref[...].astype(BF16))
        d_ref[...], mo_ref[...], vo_ref[...] = _adamw(w_ref[...], g, m_ref[...], v_ref[...])
        g_ref[...] = g

    shp = jax.ShapeDtypeStruct(w.shape, F32)
    return pl.pallas_call(
        body, name="adam_w_ada", grid=(1,), out_shape=[shp] * 4,
        in_specs=[_full(c_all_t.shape), _full(dmod_cols.shape)] + [_full(w.shape)] * 3,
        out_specs=[_full(w.shape)] * 4, compiler_params=_params(("arbitrary",)),
    )(c_all_t, dmod_cols, w, m, v)


SMALL_EARLY = ["rel_bias", "attn_sinks", "gmlp_ln_g", "gmlp_ln_b", "gmlp_w_s", "gmlp_b_s",
               "attn_out_g", "gmlp_out_g", "ln1_g", "ln1_b", "ln2_g", "ln2_b"]
SMALL_LATE = ["b_ada", "b_in"]
WEIGHTS = ["rel_bias", "w_ada", "b_ada", "w_in", "b_in", "attn_sinks", "gmlp_ln_g", "gmlp_ln_b", "gmlp_w_s",
           "gmlp_b_s", "attn_out_g", "gmlp_out_g", "w_out", "ln1_g", "ln1_b", "w_gate_up", "w_down", "ln2_g", "ln2_b"]


def _seg_rows(nelem):
    return -(-nelem // 1024) * 8


def _pack(named, names):
    parts = []
    for name in names:
        flat = named[name].reshape(-1).astype(F32)
        rows = _seg_rows(flat.shape[0])
        parts.append(jnp.pad(flat, (0, rows * 128 - flat.shape[0])).reshape(rows, 128))
    return jnp.concatenate(parts, axis=0)


def _unpack(packed, shapes, names):
    out, r0 = {}, 0
    for name in names:
        nelem = math.prod(shapes[name])
        rows = _seg_rows(nelem)
        out[name] = packed[r0:r0 + rows].reshape(-1)[:nelem].reshape(shapes[name])
        r0 += rows
    return out


def _t5_bucket_map():
    qi = jnp.arange(BLK)[:, None]
    si = jnp.arange(2 * BLK)[None, :]
    n = jnp.maximum(qi + BLK - si, 0)
    max_exact = N_BUCKETS // 2
    nf = jnp.maximum(n, max_exact).astype(F32)
    large = max_exact + (jnp.log(nf / max_exact) / math.log(MAX_DISTANCE / max_exact)
                         * (N_BUCKETS - max_exact)).astype(jnp.int32)
    large = jnp.minimum(large, N_BUCKETS - 1)
    return jnp.where(n < max_exact, n, large).astype(jnp.int32)


def kernel(x, c, rel_bias, w_ada, b_ada, w_in, b_in, attn_sinks, gmlp_ln_g, gmlp_ln_b, gmlp_w_s, gmlp_b_s, attn_out_g, gmlp_out_g, w_out, ln1_g, ln1_b, w_gate_up, w_down, ln2_g, ln2_b, loss_target, m_rel_bias, m_w_ada, m_b_ada, m_w_in, m_b_in, m_attn_sinks, m_gmlp_ln_g, m_gmlp_ln_b, m_gmlp_w_s, m_gmlp_b_s, m_attn_out_g, m_gmlp_out_g, m_w_out, m_ln1_g, m_ln1_b, m_w_gate_up, m_w_down, m_ln2_g, m_ln2_b, v_rel_bias, v_w_ada, v_b_ada, v_w_in, v_b_in, v_attn_sinks, v_gmlp_ln_g, v_gmlp_ln_b, v_gmlp_w_s, v_gmlp_b_s, v_attn_out_g, v_gmlp_out_g, v_w_out, v_ln1_g, v_ln1_b, v_w_gate_up, v_w_down, v_ln2_g, v_ln2_b):
    wts = dict(rel_bias=rel_bias, w_ada=w_ada, b_ada=b_ada, w_in=w_in, b_in=b_in, attn_sinks=attn_sinks,
               gmlp_ln_g=gmlp_ln_g, gmlp_ln_b=gmlp_ln_b, gmlp_w_s=gmlp_w_s, gmlp_b_s=gmlp_b_s,
               attn_out_g=attn_out_g, gmlp_out_g=gmlp_out_g, w_out=w_out, ln1_g=ln1_g, ln1_b=ln1_b,
               w_gate_up=w_gate_up, w_down=w_down, ln2_g=ln2_g, ln2_b=ln2_b)
    mom_m = dict(rel_bias=m_rel_bias, w_ada=m_w_ada, b_ada=m_b_ada, w_in=m_w_in, b_in=m_b_in,
                 attn_sinks=m_attn_sinks, gmlp_ln_g=m_gmlp_ln_g, gmlp_ln_b=m_gmlp_ln_b, gmlp_w_s=m_gmlp_w_s,
                 gmlp_b_s=m_gmlp_b_s, attn_out_g=m_attn_out_g, gmlp_out_g=m_gmlp_out_g, w_out=m_w_out,
                 ln1_g=m_ln1_g, ln1_b=m_ln1_b, w_gate_up=m_w_gate_up, w_down=m_w_down, ln2_g=m_ln2_g,
                 ln2_b=m_ln2_b)
    mom_v = dict(rel_bias=v_rel_bias, w_ada=v_w_ada, b_ada=v_b_ada, w_in=v_w_in, b_in=v_b_in,
                 attn_sinks=v_attn_sinks, gmlp_ln_g=v_gmlp_ln_g, gmlp_ln_b=v_gmlp_ln_b, gmlp_w_s=v_gmlp_w_s,
                 gmlp_b_s=v_gmlp_b_s, attn_out_g=v_attn_out_g, gmlp_out_g=v_gmlp_out_g, w_out=v_w_out,
                 ln1_g=v_ln1_g, ln1_b=v_ln1_b, w_gate_up=v_w_gate_up, w_down=v_w_down, ln2_g=v_ln2_g,
                 ln2_b=v_ln2_b)

    t = x.shape[1]
    tm = min(512, t)
    tn_ff = D_FF // 2
    tk_tok = min(1024, t)
    me = 4 * lax.axis_index("x") + 2 * lax.axis_index("y") + lax.axis_index("c")
    xs = x[0]
    target = loss_target[0]

    c_g, w_in_g = _exchange("gather_in", [jnp.broadcast_to(c, (8, D_MODEL)), w_in[0].T.astype(BF16)], (False, False))
    c_all = c_g[:, 0, :]
    w_in_t = w_in_g.reshape(IN_W, D_MODEL)

    ncol = w_ada.shape[2]
    b_cols = lax.dynamic_slice(b_ada, (0, me * ncol), (1, ncol))
    mod_part = _mod_partial(c_all, w_ada[0], b_cols)
    (mod_g,) = _exchange("gather_mod", [mod_part], (False,))
    mod = lax.dynamic_slice(mod_g, (0, me, 0), (N_DEV, 1, ncol)).reshape(1, N_DEV * ncol)
    sh1, sc1, g1, sh2, sc2, g2 = [mod[:, i * D_MODEL:(i + 1) * D_MODEL] for i in range(6)]

    bucket = _t5_bucket_map()
    bias = _bias_table(rel_bias, bucket)
    causal = jnp.tril(jnp.ones((BLK, BLK), dtype=bool))
    ws = jnp.where(causal[None], gmlp_w_s[0], 0.0).astype(BF16)
    pair = lambda w: jnp.concatenate([w[0::2], w[1::2]], axis=2)
    ws2, wst2 = pair(ws), pair(jnp.swapaxes(ws, 1, 2))
    bfull = jnp.repeat(gmlp_b_s[0].T, GMLP_W // N_GROUPS, axis=1)
    sinks = attn_sinks[0]

    proj, h1 = _inproj(xs, sc1, sh1, w_in_t, b_in, tm)
    (mixed,), (w_out_g, w_gu_g) = _mix_fwd(
        proj, bias, sinks, gmlp_ln_g, gmlp_ln_b, ws2, bfull, attn_out_g, gmlp_out_g,
        comm=([w_out[0].astype(BF16), w_gate_up[0].T.astype(BF16)], (False, False)))
    w_out_f = w_out_g.reshape(D_MODEL, D_MODEL)
    w_gu_t = w_gu_g.reshape(2 * D_FF, D_MODEL)
    y1, x1, h2 = _outproj(mixed, w_out_f, xs, g1, ln1_g, ln1_b, sc2, sh2, tm)
    (dsu, sg, act), (w_down_g,) = _ffn_up(h2, w_gu_t, tm, tn_ff, comm=([w_down[0].astype(BF16)], (False,)))
    w_down_f = w_down_g.reshape(D_FF, D_MODEL)
    dz2, dy2, loss_p, d_ln2g, d_ln2b, d_g2 = _ffn_down(act, w_down_f, x1, target, g2, ln2_g, ln2_b, tm)
    loss = lax.psum(0.5 / D_MODEL * jnp.sum(loss_p), ("x", "y", "c"))

    slots = lambda a: a.reshape(N_DEV, -1, D_MODEL)
    dw_down = _wgrad("wgrad_down", act, dy2, tn_ff, tk_tok)
    (dgate, dup), (r_down,) = _ffn_dact(dy2, w_down_f, dsu, sg, tm, tn_ff, comm=([slots(dw_down)], (True,)))
    dz1, dy1, d_sc2, d_sh2, d_ln1g, d_ln1b, d_g1 = _ffn_dh2(dgate, dup, w_gu_t, x1, xs, y1, dz2, sc2, g1, ln1_g,
                                                           min(256, t))
    dw_gu_t = _wgrad("wgrad_gate_up", dgate, h2, tn_ff, tk_tok, a2=dup)
    dw_out = _wgrad("wgrad_out", mixed, dy1, D_MODEL, tk_tok)
    ((dproj, dkvn, dl_acc, dsink_acc, d_lng, d_lnb, d_ws, d_bs, d_aog, d_gog), (r_gu, r_out)) = _mix_bwd(
        proj, bias, sinks, gmlp_ln_g, gmlp_ln_b, ws2, wst2, bfull, attn_out_g, gmlp_out_g, dy1, w_out_f,
        comm=([slots(dw_gu_t), slots(dw_out)], (True, True)))
    d_relb = _bias_grad(dl_acc, bucket)

    rsum = lambda a: jnp.sum(a, axis=0)
    early_g = dict(
        rel_bias=d_relb[:, 0, :N_BUCKETS].T, attn_sinks=rsum(dsink_acc)[:N_HEADS],
        gmlp_ln_g=rsum(d_lng), gmlp_ln_b=rsum(d_lnb), gmlp_w_s=jnp.where(causal[None], d_ws, 0.0),
        gmlp_b_s=jnp.sum(d_bs.reshape(BLK, N_GROUPS, GMLP_W // N_GROUPS), axis=2).T,
        attn_out_g=rsum(d_aog), gmlp_out_g=rsum(d_gog), ln1_g=rsum(d_ln1g), ln1_b=rsum(d_ln1b),
        ln2_g=rsum(d_ln2g), ln2_b=rsum(d_ln2b))
    (grad_x, dproj_b, d_bin, d_sc1, d_sh1), _ = _din(dproj, dkvn, w_in_t, xs, dz1, sc1, tm, comm=None)
    dw_in_t, (early_all,) = _wgrad("wgrad_in", dproj_b, h1, IN_W, tk_tok,
                                   comm=([_pack(early_g, SMALL_EARLY)], (False,)))
    dmod = jnp.concatenate([rsum(d_sh1), rsum(d_sc1), rsum(d_g1), rsum(d_sh2), rsum(d_sc2), rsum(d_g2)])
    late_all, r_in = _exchange("scatter_in", [_pack(dict(b_ada=dmod, b_in=rsum(d_bin)), SMALL_LATE), slots(dw_in_t)],
                               (False, True))

    small = [{}, {}, {}, {}]
    for label, names, parts in (("adam_small_early", SMALL_EARLY, early_all), ("adam_small_late", SMALL_LATE, late_all)):
        res = _adam_reduce(label, parts, _pack(wts, names), _pack(mom_m, names), _pack(mom_v, names), parts.shape[1])
        shapes = {k: wts[k].shape for k in names}
        for i in range(4):
            small[i].update(_unpack(res[i], shapes, names))

    dmod_all = late_all[:, :_seg_rows(6 * D_MODEL), :].reshape(N_DEV, 6 * D_MODEL)
    dmod_cols = lax.dynamic_slice(dmod_all, (0, me * ncol), (N_DEV, ncol))
    kpad = 128 - N_DEV
    ada = _adam_w_ada(jnp.pad(c_all.T, ((0, 0), (0, kpad))), jnp.pad(dmod_cols, ((0, kpad), (0, 0))),
                      w_ada[0], m_w_ada[0], v_w_ada[0])

    tr = lambda a: jnp.swapaxes(a, -1, -2)
    big = {}
    big["w_in"] = [tr(o)[None] for o in _adam_reduce("adam_w_in", r_in, w_in[0].T, m_w_in[0].T, v_w_in[0].T, 112)]
    big["w_out"] = [o[None] for o in _adam_reduce("adam_w_out", r_out, w_out[0], m_w_out[0], v_w_out[0], 128)]
    big["w_gate_up"] = [tr(o)[None] for o in _adam_reduce("adam_w_gu", r_gu, w_gate_up[0].T, m_w_gate_up[0].T,
                                                           v_w_gate_up[0].T, 352)]
    big["w_down"] = [o[None] for o in _adam_reduce("adam_w_down", r_down, w_down[0], m_w_down[0], v_w_down[0], 176)]
    big["w_ada"] = [o[None] for o in ada]

    outs = [[], [], [], []]
    for name in WEIGHTS:
        for i in range(4):
            outs[i].append(big[name][i] if name in big else small[i][name])
    return (loss, grad_x[None], *outs[0], *outs[1], *outs[2], *outs[3])
```

```python
import math

import jax
import jax.numpy as jnp
from jax import lax
from jax.experimental import pallas as pl
from jax.experimental.pallas import tpu as pltpu

F32 = jnp.float32
BF16 = jnp.bfloat16
MESH = pl.DeviceIdType.MESH

N_DEV = 8
D_MODEL = 1024
HEAD_DIM = 64
N_HEADS = 8
N_GROUPS = 8
ATTN_W = 512
KV_W = 128
GMLP_W = 512
IN_W = 1792
BLK = 128
N_BUCKETS = 32
MAX_DISTANCE = 128
D_FF = 2816
ALPHA = 2.0 ** 0.25
LN_EPS = 1e-5
NEG_INF = -1e30
ADAM_LR = 0.001
ADAM_B1 = 0.9
ADAM_B2 = 0.999
ADAM_EPS = 1e-08
ADAM_WD = 0.01
ADAM_STEP = 10
GELU_C0 = math.sqrt(2.0 / math.pi)
GELU_C1 = 0.044715

VMEM_LIMIT = 56 * 1024 * 1024


def _params(sem):
    return pltpu.CompilerParams(dimension_semantics=sem, vmem_limit_bytes=VMEM_LIMIT)


def _dot(a, b):
    return lax.dot_general(a, b, (((1,), (0,)), ((), ())), preferred_element_type=F32)


def _dot_nt(a, b):
    return lax.dot_general(a, b, (((1,), (1,)), ((), ())), preferred_element_type=F32)


def _dot_tn(a, b):
    return lax.dot_general(a, b, (((0,), (0,)), ((), ())), preferred_element_type=F32)


def _full(shape):
    nd = len(shape)
    return pl.BlockSpec(shape, lambda *_: (0,) * nd)


def _rowsum8(v):
    r, c = v.shape
    return jnp.sum(v.reshape(r // 8, 8, c), axis=0)


def _sigmoid(v):
    return 1.0 / (1.0 + jnp.exp(-v))


def _gelu_parts(v):
    v2 = v * v
    t = jnp.tanh(GELU_C0 * (v + GELU_C1 * v * v2))
    g = 0.5 * v * (1.0 + t)
    dg = 0.5 * (1.0 + t) + 0.5 * v * (1.0 - t * t) * (GELU_C0 * (1.0 + 3.0 * GELU_C1 * v2))
    return g, dg


def _ln_stats(z):
    mu = jnp.mean(z, axis=1, keepdims=True)
    zc = z - mu
    var = jnp.mean(zc * zc, axis=1, keepdims=True)
    rstd = lax.rsqrt(var + LN_EPS)
    return zc * rstd, rstd


def _ln_bwd(dxhat, xhat, rstd):
    m1 = jnp.mean(dxhat, axis=1, keepdims=True)
    m2 = jnp.mean(dxhat * xhat, axis=1, keepdims=True)
    return rstd * (dxhat - m1 - xhat * m2)


def _seg_mean64(v):
    r = v.shape[0]
    lo = lax.broadcasted_iota(jnp.int32, (r, 128), 1) < 64
    outs = []
    for j in range(v.shape[1] // 128):
        ch = v[:, 128 * j:128 * (j + 1)]
        s_lo = jnp.sum(jnp.where(lo, ch, 0.0), axis=1, keepdims=True)
        s_hi = jnp.sum(jnp.where(lo, 0.0, ch), axis=1, keepdims=True)
        outs.append(jnp.where(lo, s_lo, s_hi) * (1.0 / 64.0))
    return jnp.concatenate(outs, axis=1)


def _rms(a, g):
    r = lax.rsqrt(jnp.mean(a * a, axis=1, keepdims=True) + LN_EPS)
    return a * r * g, r


def _rms_bwd(dout, a, r, g):
    t = dout * g
    return r * t - a * (r * r * r) * jnp.mean(t * a, axis=1, keepdims=True)


PEER_ORDER = (1, 2, 4, 3, 5, 6, 7)


def _peer(j):
    x, y, c = lax.axis_index("x"), lax.axis_index("y"), lax.axis_index("c")
    px = 1 - x if j & 4 else x
    py = 1 - y if j & 2 else y
    pc = 1 - c if j & 1 else c
    return (px, py, pc), 4 * px + 2 * py + pc


SIBLING = 1
CHIP_FLIPS = (4, 2, 6)


def _exchange_phase(phase, ins, outs, modes, send_sems, recv_sems, loc_sems):
    me = 4 * lax.axis_index("x") + 2 * lax.axis_index("y") + lax.axis_index("c")
    for k, mode in enumerate(modes):
        def copy(i, src, slot, dev, k=k):
            return pltpu.make_async_remote_copy(src_ref=src, dst_ref=outs[k].at[slot], send_sem=send_sems.at[k, i],
                                                recv_sem=recv_sems.at[k, i], device_id=dev, device_id_type=MESH)

        src_me = ins[k].at[me] if mode == "scatter" else ins[k]
        local = pltpu.make_async_copy(src_me, outs[k].at[me], loc_sems.at[k])
        if mode == "gather2":
            sib_dev, sib_idx = _peer(SIBLING)
            chips = [_peer(j) for j in CHIP_FLIPS]
            far = [_peer(j | SIBLING)[1] for j in CHIP_FLIPS]
            if phase == "start":
                local.start()
                copy(0, ins[k], me, sib_dev).start()
                for i, (dev, _) in enumerate(chips):
                    copy(1 + i, ins[k], me, dev).start()
            elif phase == "mid":
                for i, (dev, idx) in enumerate(chips):
                    copy(1 + i, ins[k], idx, dev).wait_recv()
                    copy(4 + i, outs[k].at[idx], idx, sib_dev).start()
            else:
                copy(0, ins[k], sib_idx, sib_dev).wait_recv()
                for i, slot in enumerate(far):
                    copy(4 + i, ins[k], slot, sib_dev).wait_recv()
                copy(0, ins[k], me, sib_dev).wait_send()
                for i, (dev, idx) in enumerate(chips):
                    copy(1 + i, ins[k], me, dev).wait_send()
                    copy(4 + i, outs[k].at[idx], idx, sib_dev).wait_send()
                local.wait()
            continue
        peers = [_peer(j) for j in PEER_ORDER]
        if phase == "start":
            local.start()
            for i, (dev, idx) in enumerate(peers):
                copy(i, ins[k].at[idx] if mode == "scatter" else ins[k], me, dev).start()
        elif phase == "end":
            for i, (dev, idx) in enumerate(peers):
                copy(i, src_me, idx, dev).wait_recv()
            for i, (dev, idx) in enumerate(peers):
                copy(i, src_me, me, dev).wait_send()
            local.wait()


def _exchange_shapes(arrays, modes):
    return [jax.ShapeDtypeStruct((N_DEV,) + (a.shape[1:] if m == "scatter" else a.shape), a.dtype)
            for a, m in zip(arrays, modes)]


def _exchange_sems(n):
    return [pltpu.SemaphoreType.DMA((n, N_DEV - 1)), pltpu.SemaphoreType.DMA((n, N_DEV - 1)),
            pltpu.SemaphoreType.DMA((n,))]


def _exchange(name, arrays, modes):
    n = len(arrays)

    def body(*refs):
        for phase in ("start", "mid", "end"):
            _exchange_phase(phase, refs[:n], refs[n:2 * n], modes, *refs[2 * n:])

    any_spec = pl.BlockSpec(memory_space=pl.ANY)
    return pl.pallas_call(
        body, name=name, out_shape=_exchange_shapes(arrays, modes),
        in_specs=[any_spec] * n, out_specs=[any_spec] * n, scratch_shapes=_exchange_sems(n),
    )(*arrays)


def _call(body, *, name, grid, in_specs, out_specs, out_shape, args, sem, scratch_shapes=(), comm=None):
    if comm is None:
        outs = pl.pallas_call(body, name=name, grid=grid, in_specs=list(in_specs), out_specs=list(out_specs),
                              out_shape=list(out_shape), scratch_shapes=list(scratch_shapes),
                              compiler_params=_params(sem))(*args)
        return list(outs), []
    arrays, modes = comm
    n_in, n_out, nc, ns = len(in_specs), len(out_specs), len(arrays), len(scratch_shapes)
    n_steps = math.prod(grid)

    def hosted(*refs):
        ins, cins = refs[:n_in], refs[n_in:n_in + nc]
        outs, couts = refs[n_in + nc:n_in + nc + n_out], refs[n_in + nc + n_out:n_in + 2 * nc + n_out]
        scratch = refs[n_in + 2 * nc + n_out:]
        ex = (cins, couts, modes) + tuple(scratch[ns:])
        step = pl.program_id(0)
        for ax in range(1, len(grid)):
            step = step * grid[ax] + pl.program_id(ax)

        @pl.when(step == 0)
        def _():
            _exchange_phase("start", *ex)

        body(*ins, *outs, *scratch[:ns])

        if "gather2" in modes:
            @pl.when(step == (3 * n_steps) // 4)
            def _():
                _exchange_phase("mid", *ex)

        @pl.when(step == n_steps - 1)
        def _():
            _exchange_phase("end", *ex)

    any_spec = pl.BlockSpec(memory_space=pl.ANY)
    res = pl.pallas_call(
        hosted, name=name, grid=grid, in_specs=list(in_specs) + [any_spec] * nc,
        out_specs=list(out_specs) + [any_spec] * nc, out_shape=list(out_shape) + _exchange_shapes(arrays, modes),
        scratch_shapes=list(scratch_shapes) + _exchange_sems(nc),
        compiler_params=_params(tuple("arbitrary" for _ in grid)))(*args, *arrays)
    return list(res[:n_out]), list(res[n_out:])


def _mod_partial(c_all, w_ada, b_ada_cols):
    def body(c_ref, w_ref, b_ref, o_ref):
        cv = c_ref[...]
        s = (cv * _sigmoid(cv)).astype(BF16)
        o_ref[...] = _dot(s, w_ref[...].astype(BF16)) + b_ref[...]

    ncol = w_ada.shape[1]
    return pl.pallas_call(
        body, name="mod_partial", out_shape=jax.ShapeDtypeStruct((N_DEV, ncol), F32),
        in_specs=[_full(c_all.shape), _full(w_ada.shape), _full(b_ada_cols.shape)],
        out_specs=_full((N_DEV, ncol)), grid=(1,), compiler_params=_params(("arbitrary",)),
    )(c_all, w_ada, b_ada_cols)


def _bias_table(rel_bias, bucket):
    def body(rb_ref, bk_ref, o_ref):
        h = pl.program_id(0)
        bk = bk_ref[...]
        acc = jnp.zeros((BLK, 2 * BLK), F32)
        for b in range(N_BUCKETS):
            acc = jnp.where(bk == b, rb_ref[b, h], acc)
        dist = (lax.broadcasted_iota(jnp.int32, (BLK, 2 * BLK), 0) + BLK
                - lax.broadcasted_iota(jnp.int32, (BLK, 2 * BLK), 1))
        o_ref[0] = jnp.where((dist >= 0) & (dist < BLK), acc, NEG_INF)

    return pl.pallas_call(
        body, name="bias_table", out_shape=jax.ShapeDtypeStruct((N_HEADS, BLK, 2 * BLK), F32),
        in_specs=[pl.BlockSpec(memory_space=pltpu.SMEM), _full((BLK, 2 * BLK))],
        out_specs=pl.BlockSpec((1, BLK, 2 * BLK), lambda h: (h, 0, 0)), grid=(N_HEADS,),
        compiler_params=_params(("arbitrary",)),
    )(rel_bias, bucket)


def _bias_grad(dl_acc, bucket):
    def body(dl_ref, bk_ref, o_ref):
        bk = bk_ref[...]
        dl = dl_ref[0]
        lane = lax.broadcasted_iota(jnp.int32, (1, 128), 1)
        row = jnp.zeros((1, 128), F32)
        for b in range(N_BUCKETS):
            s = jnp.sum(jnp.sum(jnp.where(bk == b, dl, 0.0), axis=1, keepdims=True), axis=0, keepdims=True)
            row = jnp.where(lane == b, s, row)
        o_ref[0] = row

    return pl.pallas_call(
        body, name="bias_grad", out_shape=jax.ShapeDtypeStruct((N_HEADS, 1, 128), F32),
        in_specs=[pl.BlockSpec((1, BLK, 2 * BLK), lambda h: (h, 0, 0)), _full((BLK, 2 * BLK))],
        out_specs=pl.BlockSpec((1, 1, 128), lambda h: (h, 0, 0)), grid=(N_HEADS,),
        compiler_params=_params(("arbitrary",)),
    )(dl_acc, bucket)


def _inproj(x, sc1, sh1, w_in_t, b_in, tm):
    t, d = x.shape
    n = w_in_t.shape[0]

    def body(x_ref, sc_ref, sh_ref, w_ref, b_ref, proj_ref, h_ref):
        h = (x_ref[...] * (1.0 + sc_ref[...]) + sh_ref[...]).astype(BF16)
        h_ref[...] = h
        proj_ref[...] = _dot_nt(h, w_ref[...]) + b_ref[...]

    row = lambda w: pl.BlockSpec((tm, w), lambda i: (i, 0))
    return pl.pallas_call(
        body, name="inproj", grid=(t // tm,),
        out_shape=[jax.ShapeDtypeStruct((t, n), F32), jax.ShapeDtypeStruct((t, d), BF16)],
        in_specs=[row(d), _full((1, d)), _full((1, d)), _full((n, d)), _full((1, n))],
        out_specs=[row(n), row(d)], compiler_params=_params(("parallel",)),
    )(x, sc1, sh1, w_in_t, b_in)


def _half_masks():
    lo_q = lax.broadcasted_iota(jnp.int32, (BLK, 128), 1) < 64
    lo_k = lax.broadcasted_iota(jnp.int32, (2 * BLK, 128), 1) < 64
    return lo_q, lo_k


def _head_place(h):
    return h // 2, h % 2, h // 4


def _attn_heads(q, kk, vv, bias_ref, sinks_ref, n):
    lo_q, lo_k = _half_masks()
    kkb, kksb = kk.astype(BF16), pltpu.roll(kk, 64, 1).astype(BF16)
    vvb, vvsb = vv.astype(BF16), pltpu.roll(vv, 64, 1).astype(BF16)
    n0mask = (n == 0) & (lax.broadcasted_iota(jnp.int32, (BLK, 2 * BLK), 1) < BLK)
    chunks, probs = [], []
    for j in range(4):
        qc = q[:, 128 * j:128 * (j + 1)]
        acc = jnp.zeros((BLK, 128), F32)
        for pos in range(2):
            h = 2 * j + pos
            direct = (h // 4) == pos
            mq = lo_q if pos == 0 else jnp.logical_not(lo_q)
            mk = lo_k if pos == 0 else jnp.logical_not(lo_k)
            qm = jnp.where(mq, qc, 0.0).astype(BF16)
            logit = _dot_nt(qm, kkb if direct else kksb) * (HEAD_DIM ** -0.5) + bias_ref[h]
            logit = jnp.where(n0mask, NEG_INF, logit)
            sk = sinks_ref[h]
            m = jnp.maximum(jnp.max(logit, axis=1, keepdims=True), sk)
            e = jnp.exp(logit - m)
            es = jnp.exp(sk - m)
            den = jnp.sum(e, axis=1, keepdims=True) + es
            p = e / den
            vm = jnp.where(mk, vvb if direct else vvsb, jnp.zeros_like(vvb))
            acc = acc + _dot(p.astype(BF16), vm)
            probs.append((p, es / den))
        chunks.append(acc)
    return jnp.concatenate(chunks, axis=1), probs


def _gmlp_block(gu, gv, lng, lnb, ws_ref, bfull):
    lo_q, _ = _half_masks()
    u, du = _gelu_parts(gu)
    a, da = _gelu_parts(gv)
    mu = _seg_mean64(a)
    ac = a - mu
    rstd = lax.rsqrt(_seg_mean64(ac * ac) + LN_EPS)
    vhat = ac * rstd
    vn = vhat * lng + lnb
    chunks = []
    for j in range(4):
        vc = vn[:, 128 * j:128 * (j + 1)]
        acc = jnp.zeros((BLK, 128), F32)
        for pos in range(2):
            mq = lo_q if pos == 0 else jnp.logical_not(lo_q)
            acc = acc + _dot(ws_ref[2 * j + pos], jnp.where(mq, vc, 0.0).astype(BF16))
        chunks.append(acc)
    ms = jnp.concatenate(chunks, axis=1) + bfull
    return u * ms, (u, du, da, vhat, rstd, vn, ms)


def _mix_in_specs(nb):
    return [pl.BlockSpec((BLK, IN_W), lambda n: (n, 0)),
            pl.BlockSpec((BLK, 2 * KV_W), lambda n: (jnp.maximum(n - 1, 0), ATTN_W // (2 * KV_W))),
            _full((N_HEADS, BLK, 2 * BLK)),
            pl.BlockSpec(memory_space=pltpu.SMEM),
            _full((1, GMLP_W)), _full((1, GMLP_W)),
            _full((N_GROUPS, BLK, BLK)), _full((BLK, GMLP_W)),
            _full((1, ATTN_W)), _full((1, GMLP_W))]


def _split_proj(proj_ref, kvp_ref):
    q = proj_ref[:, 0:ATTN_W]
    k = proj_ref[:, ATTN_W:ATTN_W + KV_W]
    v = proj_ref[:, ATTN_W + KV_W:ATTN_W + 2 * KV_W]
    gu = proj_ref[:, ATTN_W + 2 * KV_W:ATTN_W + 2 * KV_W + GMLP_W]
    gv = proj_ref[:, ATTN_W + 2 * KV_W + GMLP_W:IN_W]
    kk = jnp.concatenate([kvp_ref[:, 0:KV_W], k], axis=0)
    vv = jnp.concatenate([kvp_ref[:, KV_W:2 * KV_W], v], axis=0)
    return q, kk, vv, gu, gv


def _mix_fwd(proj, bias, sinks, lng, lnb, ws, bfull, aog, gog, comm):
    t = proj.shape[0]
    nb = t // BLK

    def body(proj_ref, kvp_ref, bias_ref, sinks_ref, lng_ref, lnb_ref, ws_ref, bfull_ref, aog_ref, gog_ref, out_ref):
        n = pl.program_id(0)
        q, kk, vv, gu, gv = _split_proj(proj_ref, kvp_ref)
        attn, _ = _attn_heads(q, kk, vv, bias_ref, sinks_ref, n)
        gm, _ = _gmlp_block(gu, gv, lng_ref[...], lnb_ref[...], ws_ref, bfull_ref[...])
        out_ref[:, 0:ATTN_W] = _rms(attn, aog_ref[...])[0].astype(BF16)
        out_ref[:, ATTN_W:ATTN_W + GMLP_W] = _rms(gm, gog_ref[...])[0].astype(BF16)

    return _call(
        body, name="mix_fwd", grid=(nb,), out_shape=[jax.ShapeDtypeStruct((t, D_MODEL), BF16)],
        in_specs=_mix_in_specs(nb), out_specs=[pl.BlockSpec((BLK, D_MODEL), lambda n: (n, 0))],
        sem=("parallel",), comm=comm, args=(proj, proj, bias, sinks, lng, lnb, ws, bfull, aog, gog))


def _mix_bwd(proj, bias, sinks, lng, lnb, ws, ws_t, bfull, aog, gog, dy, w_out, comm):
    t = proj.shape[0]
    nb = t // BLK

    def body(proj_ref, kvp_ref, bias_ref, sinks_ref, lng_ref, lnb_ref, ws_ref, bfull_ref, aog_ref, gog_ref,
             wst_ref, dy_ref, wout_ref,
             dproj_ref, dkvn_ref, dl_ref, dsink_ref, dlng_ref, dlnb_ref, dws_ref, dbs_ref, daog_ref, dgog_ref):
        n = pl.program_id(0)

        @pl.when(n == 0)
        def _():
            for r in (dl_ref, dsink_ref, dlng_ref, dlnb_ref, dws_ref, dbs_ref, daog_ref, dgog_ref):
                r[...] = jnp.zeros_like(r)

        lo_q, lo_k = _half_masks()
        q, kk, vv, gu, gv = _split_proj(proj_ref, kvp_ref)
        dmix = _dot_nt(dy_ref[...], wout_ref[...])
        dma, dmg = dmix[:, 0:ATTN_W], dmix[:, ATTN_W:ATTN_W + GMLP_W]

        attn, probs = _attn_heads(q, kk, vv, bias_ref, sinks_ref, n)
        aog = aog_ref[...]
        _, r_a = _rms(attn, aog)
        daog_ref[...] += _rowsum8(dma * attn * r_a)
        dattn = _rms_bwd(dma, attn, r_a, aog)

        kkb, kksb = kk.astype(BF16), pltpu.roll(kk, 64, 1).astype(BF16)
        vvb, vvsb = vv.astype(BF16), pltpu.roll(vv, 64, 1).astype(BF16)
        lane = lax.broadcasted_iota(jnp.int32, (BLK, 128), 1)
        dk_d = jnp.zeros((2 * BLK, 128), F32)
        dk_s = jnp.zeros((2 * BLK, 128), F32)
        dv_d = jnp.zeros((2 * BLK, 128), F32)
        dv_s = jnp.zeros((2 * BLK, 128), F32)
        dsink = jnp.zeros((BLK, 128), F32)
        dq_chunks = []
        for j in range(4):
            qc = q[:, 128 * j:128 * (j + 1)]
            doc = dattn[:, 128 * j:128 * (j + 1)]
            dq = jnp.zeros((BLK, 128), F32)
            for pos in range(2):
                h = 2 * j + pos
                direct = (h // 4) == pos
                mq = lo_q if pos == 0 else jnp.logical_not(lo_q)
                mk = lo_k if pos == 0 else jnp.logical_not(lo_k)
                p, psink = probs[h]
                qm = jnp.where(mq, qc, 0.0).astype(BF16)
                dom = jnp.where(mq, doc, 0.0).astype(BF16)
                dp = _dot_nt(dom, vvb if direct else vvsb)
                rs = jnp.sum(p * dp, axis=1, keepdims=True)
                dl = p * (dp - rs)
                dl_ref[h] += dl
                dsink = dsink + jnp.where(lane == h, -psink * rs, 0.0)
                dls = (dl * (HEAD_DIM ** -0.5)).astype(BF16)
                km = jnp.where(mk, kkb if direct else kksb, jnp.zeros_like(kkb))
                dq = dq + _dot(dls, km)
                dk_h = _dot_tn(dls, qm)
                dv_h = _dot_tn(p.astype(BF16), dom)
                if direct:
                    dk_d, dv_d = dk_d + dk_h, dv_d + dv_h
                else:
                    dk_s, dv_s = dk_s + dk_h, dv_s + dv_h
            dq_chunks.append(dq)
        dsink_ref[...] += dsink
        dk = dk_d + pltpu.roll(dk_s, 64, 1)
        dv = dv_d + pltpu.roll(dv_s, 64, 1)
        for j in range(4):
            dproj_ref[:, 128 * j:128 * (j + 1)] = dq_chunks[j]
        dproj_ref[:, ATTN_W:ATTN_W + KV_W] = dk[BLK:2 * BLK]
        dproj_ref[:, ATTN_W + KV_W:ATTN_W + 2 * KV_W] = dv[BLK:2 * BLK]
        dkvn_ref[:, 0:KV_W] = dk[0:BLK]
        dkvn_ref[:, KV_W:2 * KV_W] = dv[0:BLK]

        lng = lng_ref[...]
        gog = gog_ref[...]
        gm, (u, du, da, vhat, rstd, vn, ms) = _gmlp_block(gu, gv, lng, lnb_ref[...], ws_ref, bfull_ref[...])
        _, r_g = _rms(gm, gog)
        dgog_ref[...] += _rowsum8(dmg * gm * r_g)
        dgm = _rms_bwd(dmg, gm, r_g, gog)
        dproj_ref[:, ATTN_W + 2 * KV_W:ATTN_W + 2 * KV_W + GMLP_W] = dgm * ms * du
        dms = dgm * u
        dbs_ref[...] += dms
        dvn_chunks = []
        for j in range(4):
            dmc = dms[:, 128 * j:128 * (j + 1)]
            vcb = vn[:, 128 * j:128 * (j + 1)].astype(BF16)
            acc = jnp.zeros((BLK, 128), F32)
            for pos in range(2):
                g = 2 * j + pos
                mq = lo_q if pos == 0 else jnp.logical_not(lo_q)
                dm = jnp.where(mq, dmc, 0.0).astype(BF16)
                dws_ref[g] += _dot_nt(dm, vcb)
                acc = acc + _dot(wst_ref[g], dm)
            dvn_chunks.append(acc)
        dvn = jnp.concatenate(dvn_chunks, axis=1)
        dlng_ref[...] += _rowsum8(dvn * vhat)
        dlnb_ref[...] += _rowsum8(dvn)
        dvh = dvn * lng
        dact = rstd * (dvh - _seg_mean64(dvh) - vhat * _seg_mean64(dvh * vhat))
        dproj_ref[:, ATTN_W + 2 * KV_W + GMLP_W:IN_W] = dact * da

    acc8 = lambda w: jax.ShapeDtypeStruct((8, w), F32)
    out_shape = [jax.ShapeDtypeStruct((t, IN_W), F32), jax.ShapeDtypeStruct((t, 2 * KV_W), F32),
                 jax.ShapeDtypeStruct((N_HEADS, BLK, 2 * BLK), F32), jax.ShapeDtypeStruct((BLK, 128), F32),
                 acc8(GMLP_W), acc8(GMLP_W), jax.ShapeDtypeStruct((N_GROUPS, BLK, BLK), F32),
                 jax.ShapeDtypeStruct((BLK, GMLP_W), F32), acc8(ATTN_W), acc8(GMLP_W)]
    out_specs = [pl.BlockSpec((BLK, IN_W), lambda n: (n, 0)),
                 pl.BlockSpec((BLK, 2 * KV_W), lambda n: ((n + nb - 1) % nb, 0)),
                 _full((N_HEADS, BLK, 2 * BLK)), _full((BLK, 128)), _full((8, GMLP_W)), _full((8, GMLP_W)),
                 _full((N_GROUPS, BLK, BLK)), _full((BLK, GMLP_W)), _full((8, ATTN_W)), _full((8, GMLP_W))]
    in_specs = _mix_in_specs(nb) + [_full((N_GROUPS, BLK, BLK)),
                                    pl.BlockSpec((BLK, D_MODEL), lambda n: (n, 0)),
                                    _full((D_MODEL, D_MODEL))]
    return _call(
        body, name="mix_bwd", grid=(nb,), out_shape=out_shape, in_specs=in_specs, out_specs=out_specs,
        sem=("arbitrary",), comm=comm, args=(proj, proj, bias, sinks, lng, lnb, ws, bfull, aog, gog, ws_t, dy, w_out))


HALF = 64
ROWS = 32


def _lane_lo(rows):
    return lax.broadcasted_iota(jnp.int32, (rows, 128), 1) < 64


def _mix_stage_kv(proj_ref, kvp_ref, s):
    lo = _lane_lo(2 * BLK)
    for name, col in (("k", ATTN_W), ("v", ATTN_W + KV_W)):
        cur = jnp.concatenate([kvp_ref[:, col - ATTN_W:col - ATTN_W + KV_W], proj_ref[:, col:col + KV_W]], axis=0)
        plain, swapped = cur.astype(BF16), pltpu.roll(cur, 64, 1).astype(BF16)
        zero = jnp.zeros_like(plain)
        for g in range(2):
            dup = jnp.where(lo, plain, swapped) if g == 0 else jnp.where(lo, swapped, plain)
            s[name + "d"][g] = dup
            s[name + "m"][g] = jnp.concatenate([jnp.where(lo, dup, zero), jnp.where(lo, zero, dup)], axis=0)


def _group_rows(ref, g):
    return ref[4 * g:4 * g + 4].reshape(4 * BLK, ref.shape[2])


def _pair_rows(ref, g):
    return jnp.concatenate([jnp.concatenate([ref[4 * g + 2 * c], ref[4 * g + 2 * c + 1]], axis=1) for c in range(2)],
                           axis=0)


def _mask_heads(src_ref, dst_ref):
    lo = _lane_lo(BLK)
    for j in range(4):
        chunk = src_ref[:, 128 * j:128 * (j + 1)]
        dst_ref[2 * j] = jnp.where(lo, chunk, 0.0).astype(BF16)
        dst_ref[2 * j + 1] = jnp.where(lo, 0.0, chunk).astype(BF16)


def _mix_stage_attn(proj_ref, bias_ref, sinks_ref, n, s, keep):
    _mask_heads(proj_ref, s["qm"])
    for g in range(2):
        s["lg"][g] = _dot_nt(_group_rows(s["qm"], g), s["kd"][g])
    n0mask = (n == 0) & (lax.broadcasted_iota(jnp.int32, (HALF, 2 * BLK), 1) < BLK)
    for h in range(N_HEADS):
        sk = sinks_ref[h]
        for hf in range(BLK // HALF):
            rows = slice(HALF * hf, HALF * (hf + 1))
            grows = slice(BLK * (h % 4) + HALF * hf, BLK * (h % 4) + HALF * (hf + 1))
            logit = s["lg"][h // 4, grows, :] * (HEAD_DIM ** -0.5) + bias_ref[h, rows, :]
            logit = jnp.where(n0mask, NEG_INF, logit)
            m = jnp.maximum(jnp.max(logit, axis=1, keepdims=True), sk)
            e = jnp.exp(logit - m)
            es = jnp.exp(sk - m)
            inv = 1.0 / (jnp.sum(e, axis=1, keepdims=True) + es)
            p = e * inv
            s["pb"][h, rows, :] = p.astype(BF16)
            if keep:
                s["p"][h, rows, :] = p
                s["psink"][h, rows, :] = es * inv
    for g in range(2):
        out = _dot(_pair_rows(s["pb"], g), s["vm"][g])
        s["attn"][:, 256 * g:256 * g + 128] = out[0:BLK]
        s["attn"][:, 256 * g + 128:256 * g + 256] = out[BLK:2 * BLK]


def _mix_stage_gmlp_pre(proj_ref, lng, lnb, s, keep):
    c0 = ATTN_W + 2 * KV_W
    for r0 in range(0, BLK, ROWS):
        rows = slice(r0, r0 + ROWS)
        u, du = _gelu_parts(proj_ref[rows, c0:c0 + GMLP_W])
        a, da = _gelu_parts(proj_ref[rows, c0 + GMLP_W:c0 + 2 * GMLP_W])
        ac = a - _seg_mean64(a)
        rstd = lax.rsqrt(_seg_mean64(ac * ac) + LN_EPS)
        vhat = ac * rstd
        s["u"][rows, :] = u
        s["vnb"][rows, :] = (vhat * lng + lnb).astype(BF16)
        if keep:
            s["du"][rows, :] = du
            s["da"][rows, :] = da
            s["vhat"][rows, :] = vhat
            s["rstd"][rows, :] = rstd


def _stack_halves(chunk):
    lo = _lane_lo(BLK)
    zero = jnp.zeros_like(chunk)
    return jnp.concatenate([jnp.where(lo, chunk, zero), jnp.where(lo, zero, chunk)], axis=0)


def _mix_stage_gmlp_mix(ws2_ref, bfull_ref, s):
    for j in range(4):
        cols = slice(128 * j, 128 * (j + 1))
        s["ms"][:, cols] = _dot(ws2_ref[j], _stack_halves(s["vnb"][:, cols])) + bfull_ref[:, cols]


def _mix_scratch(keep):
    f32 = lambda *shape: pltpu.VMEM(shape, F32)
    b16 = lambda *shape: pltpu.VMEM(shape, BF16)
    names = dict(kd=b16(2, 2 * BLK, 128), vd=b16(2, 2 * BLK, 128), km=b16(2, 4 * BLK, 128), vm=b16(2, 4 * BLK, 128),
                 qm=b16(N_HEADS, BLK, 128), lg=f32(2, 4 * BLK, 2 * BLK), pb=b16(N_HEADS, BLK, 2 * BLK),
                 attn=f32(BLK, ATTN_W), u=f32(BLK, GMLP_W), vnb=b16(BLK, GMLP_W), ms=f32(BLK, GMLP_W))
    if keep:
        names.update(dom=b16(N_HEADS, BLK, 128), p=f32(N_HEADS, BLK, 2 * BLK),
                     dls=b16(N_HEADS, BLK, 2 * BLK), psink=f32(N_HEADS, BLK, 1),
                     dattn=f32(BLK, ATTN_W), dmix=f32(BLK, D_MODEL), du=f32(BLK, GMLP_W), da=f32(BLK, GMLP_W),
                     vhat=f32(BLK, GMLP_W), rstd=f32(BLK, GMLP_W), dmsb=b16(BLK, GMLP_W), dvn=f32(BLK, GMLP_W))
    return list(names), list(names.values())


def _mix_specs():
    return [pl.BlockSpec((BLK, IN_W), lambda n: (n, 0)),
            pl.BlockSpec((BLK, 2 * KV_W), lambda n: (jnp.maximum(n - 1, 0), ATTN_W // (2 * KV_W))),
            _full((N_HEADS, BLK, 2 * BLK)),
            pl.BlockSpec(memory_space=pltpu.SMEM),
            _full((1, GMLP_W)), _full((1, GMLP_W)),
            _full((N_GROUPS // 2, BLK, 2 * BLK)), _full((BLK, GMLP_W)),
            _full((1, ATTN_W)), _full((1, GMLP_W))]


def _mix_fwd(proj, bias, sinks, lng, lnb, ws2, bfull, aog, gog, comm):
    t = proj.shape[0]
    names, shapes = _mix_scratch(False)

    def body(proj_ref, kvp_ref, bias_ref, sinks_ref, lng_ref, lnb_ref, ws2_ref, bfull_ref, aog_ref, gog_ref,
             out_ref, *scratch):
        s = dict(zip(names, scratch))
        n = pl.program_id(0)
        _mix_stage_kv(proj_ref, kvp_ref, s)
        _mix_stage_attn(proj_ref, bias_ref, sinks_ref, n, s, False)
        _mix_stage_gmlp_pre(proj_ref, lng_ref[...], lnb_ref[...], s, False)
        _mix_stage_gmlp_mix(ws2_ref, bfull_ref, s)
        for r0 in range(0, BLK, ROWS):
            rows = slice(r0, r0 + ROWS)
            out_ref[rows, 0:ATTN_W] = _rms(s["attn"][rows, :], aog_ref[...])[0].astype(BF16)
            out_ref[rows, ATTN_W:ATTN_W + GMLP_W] = _rms(s["u"][rows, :] * s["ms"][rows, :], gog_ref[...])[0].astype(BF16)

    return _call(
        body, name="mix_fwd", grid=(t // BLK,), out_shape=[jax.ShapeDtypeStruct((t, D_MODEL), BF16)],
        in_specs=_mix_specs(), out_specs=[pl.BlockSpec((BLK, D_MODEL), lambda n: (n, 0))], scratch_shapes=shapes,
        sem=("parallel",), comm=comm, args=(proj, proj, bias, sinks, lng, lnb, ws2, bfull, aog, gog))


def _mix_bwd(proj, bias, sinks, lng, lnb, ws2, wst2, bfull, aog, gog, dy, w_out, comm):
    t = proj.shape[0]
    nb = t // BLK
    names, shapes = _mix_scratch(True)
    c_gu = ATTN_W + 2 * KV_W

    def body(proj_ref, kvp_ref, bias_ref, sinks_ref, lng_ref, lnb_ref, ws2_ref, bfull_ref, aog_ref, gog_ref,
             wst2_ref, dy_ref, wout_ref,
             dproj_ref, dkvn_ref, dl_ref, dsink_ref, dlng_ref, dlnb_ref, dws_ref, dbs_ref, daog_ref, dgog_ref,
             *scratch):
        s = dict(zip(names, scratch))
        n = pl.program_id(0)

        @pl.when(n == 0)
        def _():
            for r in (dl_ref, dsink_ref, dlng_ref, dlnb_ref, dws_ref, dbs_ref, daog_ref, dgog_ref):
                r[...] = jnp.zeros_like(r)

        s["dmix"][...] = _dot_nt(dy_ref[...], wout_ref[...])
        _mix_stage_kv(proj_ref, kvp_ref, s)
        _mix_stage_attn(proj_ref, bias_ref, sinks_ref, n, s, True)
        lng = lng_ref[...]
        _mix_stage_gmlp_pre(proj_ref, lng, lnb_ref[...], s, True)
        _mix_stage_gmlp_mix(ws2_ref, bfull_ref, s)

        aog, gog = aog_ref[...], gog_ref[...]
        for r0 in range(0, BLK, ROWS):
            rows = slice(r0, r0 + ROWS)
            attn, dma = s["attn"][rows, :], s["dmix"][rows, 0:ATTN_W]
            _, r_a = _rms(attn, aog)
            daog_ref[...] += _rowsum8(dma * attn * r_a)
            s["dattn"][rows, :] = _rms_bwd(dma, attn, r_a, aog)
            u, ms, dmg = s["u"][rows, :], s["ms"][rows, :], s["dmix"][rows, ATTN_W:ATTN_W + GMLP_W]
            gm = u * ms
            _, r_g = _rms(gm, gog)
            dgog_ref[...] += _rowsum8(dmg * gm * r_g)
            dgm = _rms_bwd(dmg, gm, r_g, gog)
            dproj_ref[rows, c_gu:c_gu + GMLP_W] = dgm * ms * s["du"][rows, :]
            dms = dgm * u
            dbs_ref[rows, :] += dms
            s["dmsb"][rows, :] = dms.astype(BF16)

        _mask_heads(s["dattn"], s["dom"])
        for g in range(2):
            s["lg"][g] = _dot_nt(_group_rows(s["dom"], g), s["vd"][g])
        lane = lax.broadcasted_iota(jnp.int32, (HALF, 128), 1)
        for hf in range(BLK // HALF):
            rows = slice(HALF * hf, HALF * (hf + 1))
            dsink = jnp.zeros((HALF, 128), F32)
            for h in range(N_HEADS):
                grows = slice(BLK * (h % 4) + HALF * hf, BLK * (h % 4) + HALF * (hf + 1))
                dp = s["lg"][h // 4, grows, :]
                p = s["p"][h, rows, :]
                rs = jnp.sum(p * dp, axis=1, keepdims=True)
                dl = p * (dp - rs)
                dl_ref[h, rows, :] += dl
                dsink = dsink + jnp.where(lane == h, -s["psink"][h, rows, :] * rs, 0.0)
                s["dls"][h, rows, :] = (dl * (HEAD_DIM ** -0.5)).astype(BF16)
            dsink_ref[rows, :] += dsink
        for g in range(2):
            dq = _dot(_pair_rows(s["dls"], g), s["km"][g])
            dproj_ref[:, 256 * g:256 * g + 128] = dq[0:BLK]
            dproj_ref[:, 256 * g + 128:256 * g + 256] = dq[BLK:2 * BLK]
        lo_k = _lane_lo(2 * BLK)
        for col, lhs, rhs in ((0, "dls", "qm"), (KV_W, "pb", "dom")):
            raw = [_dot_tn(_group_rows(s[lhs], g), _group_rows(s[rhs], g)) for g in range(2)]
            both = [r + pltpu.roll(r, 64, 1) for r in raw]
            dkv = jnp.where(lo_k, both[0], both[1])
            dproj_ref[:, ATTN_W + col:ATTN_W + col + KV_W] = dkv[BLK:2 * BLK]
            dkvn_ref[:, col:col + KV_W] = dkv[0:BLK]

        for j in range(4):
            cols = slice(128 * j, 128 * (j + 1))
            dm2 = _stack_halves(s["dmsb"][:, cols])
            vnb = s["vnb"][:, cols]
            for pos in range(2):
                dws_ref[2 * j + pos] += _dot_nt(dm2[BLK * pos:BLK * (pos + 1)], vnb)
            s["dvn"][:, cols] = _dot(wst2_ref[j], dm2)
        for r0 in range(0, BLK, ROWS):
            rows = slice(r0, r0 + ROWS)
            dvn, vhat = s["dvn"][rows, :], s["vhat"][rows, :]
            dlng_ref[...] += _rowsum8(dvn * vhat)
            dlnb_ref[...] += _rowsum8(dvn)
            dvh = dvn * lng
            dact = s["rstd"][rows, :] * (dvh - _seg_mean64(dvh) - vhat * _seg_mean64(dvh * vhat))
            dproj_ref[rows, c_gu + GMLP_W:IN_W] = dact * s["da"][rows, :]

    acc8 = lambda w: jax.ShapeDtypeStruct((8, w), F32)
    out_shape = [jax.ShapeDtypeStruct((t, IN_W), F32), jax.ShapeDtypeStruct((t, 2 * KV_W), F32),
                 jax.ShapeDtypeStruct((N_HEADS, BLK, 2 * BLK), F32), jax.ShapeDtypeStruct((BLK, 128), F32),
                 acc8(GMLP_W), acc8(GMLP_W), jax.ShapeDtypeStruct((N_GROUPS, BLK, BLK), F32),
                 jax.ShapeDtypeStruct((BLK, GMLP_W), F32), acc8(ATTN_W), acc8(GMLP_W)]
    out_specs = [pl.BlockSpec((BLK, IN_W), lambda n: (n, 0)),
                 pl.BlockSpec((BLK, 2 * KV_W), lambda n: ((n + nb - 1) % nb, 0)),
                 _full((N_HEADS, BLK, 2 * BLK)), _full((BLK, 128)), _full((8, GMLP_W)), _full((8, GMLP_W)),
                 _full((N_GROUPS, BLK, BLK)), _full((BLK, GMLP_W)), _full((8, ATTN_W)), _full((8, GMLP_W))]
    in_specs = _mix_specs() + [_full((N_GROUPS // 2, BLK, 2 * BLK)),
                               pl.BlockSpec((BLK, D_MODEL), lambda n: (n, 0)),
                               _full((D_MODEL, D_MODEL))]
    return _call(
        body, name="mix_bwd", grid=(nb,), out_shape=out_shape, in_specs=in_specs, out_specs=out_specs,
        scratch_shapes=shapes, sem=("arbitrary",), comm=comm,
        args=(proj, proj, bias, sinks, lng, lnb, ws2, bfull, aog, gog, wst2, dy, w_out))


def _outproj(mixed, w_out, x, g1, ln1g, ln1b, sc2, sh2, tm):
    t, d = x.shape

    def body(mx_ref, w_ref, x_ref, g1_ref, lg_ref, lb_ref, sc_ref, sh_ref, y_ref, x1_ref, h2_ref):
        y = _dot(mx_ref[...], w_ref[...])
        xhat, _ = _ln_stats(ALPHA * x_ref[...] + g1_ref[...] * y)
        x1 = xhat * lg_ref[...] + lb_ref[...]
        y_ref[...] = y
        x1_ref[...] = x1
        h2_ref[...] = (x1 * (1.0 + sc_ref[...]) + sh_ref[...]).astype(BF16)

    row = pl.BlockSpec((tm, d), lambda i: (i, 0))
    vec = _full((1, d))
    return pl.pallas_call(
        body, name="outproj", grid=(t // tm,),
        out_shape=[jax.ShapeDtypeStruct((t, d), F32), jax.ShapeDtypeStruct((t, d), F32),
                   jax.ShapeDtypeStruct((t, d), BF16)],
        in_specs=[row, _full((d, d)), row, vec, vec, vec, vec, vec], out_specs=[row, row, row],
        compiler_params=_params(("parallel",)),
    )(mixed, w_out, x, g1, ln1g, ln1b, sc2, sh2)


def _ffn_up(h2, w_gu_t, tm, tn, comm):
    t, d = h2.shape
    nff = D_FF // tn

    def body(h_ref, wg_ref, wu_ref, dsu_ref, sg_ref, act_ref):
        h = h_ref[...]
        g = _dot_nt(h, wg_ref[...])
        u = _dot_nt(h, wu_ref[...])
        s = _sigmoid(g)
        sg = g * s
        dsu_ref[...] = (u * (s * (1.0 + g * (1.0 - s)))).astype(BF16)
        sg_ref[...] = sg.astype(BF16)
        act_ref[...] = (sg * u).astype(BF16)

    out = pl.BlockSpec((tm, tn), lambda j, i: (i, j))
    shp = jax.ShapeDtypeStruct((t, D_FF), BF16)
    return _call(
        body, name="ffn_up", grid=(nff, t // tm), out_shape=[shp, shp, shp],
        in_specs=[pl.BlockSpec((tm, d), lambda j, i: (i, 0)),
                  pl.BlockSpec((tn, d), lambda j, i: (j, 0)),
                  pl.BlockSpec((tn, d), lambda j, i: (j + nff, 0))],
        out_specs=[out, out, out], sem=("parallel", "parallel"), comm=comm, args=(h2, w_gu_t, w_gu_t))


def _ffn_down(act, w_down, x1, target, g2, ln2g, ln2b, tm):
    t, d = x1.shape

    def body(act_ref, w_ref, x1_ref, tg_ref, g2_ref, lg_ref, lb_ref,
             dz_ref, dy_ref, loss_ref, dlg_ref, dlb_ref, dg2_ref):
        @pl.when(pl.program_id(0) == 0)
        def _():
            for r in (loss_ref, dlg_ref, dlb_ref, dg2_ref):
                r[...] = jnp.zeros_like(r)

        y2 = _dot(act_ref[...], w_ref[...])
        g2 = g2_ref[...]
        lg = lg_ref[...]
        xhat, rstd = _ln_stats(ALPHA * x1_ref[...] + g2 * y2)
        err = xhat * lg + lb_ref[...] - tg_ref[...]
        loss_ref[...] += _rowsum8(err * err)
        dx2 = err * (1.0 / d)
        dlg_ref[...] += _rowsum8(dx2 * xhat)
        dlb_ref[...] += _rowsum8(dx2)
        dz = _ln_bwd(dx2 * lg, xhat, rstd)
        dg2_ref[...] += _rowsum8(dz * y2)
        dz_ref[...] = dz
        dy_ref[...] = (g2 * dz).astype(BF16)

    row = pl.BlockSpec((tm, d), lambda i: (i, 0))
    vec = _full((1, d))
    acc = _full((8, d))
    acc_shape = jax.ShapeDtypeStruct((8, d), F32)
    return pl.pallas_call(
        body, name="ffn_down", grid=(t // tm,),
        out_shape=[jax.ShapeDtypeStruct((t, d), F32), jax.ShapeDtypeStruct((t, d), BF16)] + [acc_shape] * 4,
        in_specs=[pl.BlockSpec((tm, D_FF), lambda i: (i, 0)), _full((D_FF, d)), row, row, vec, vec, vec],
        out_specs=[row, row, acc, acc, acc, acc], compiler_params=_params(("arbitrary",)),
    )(act, w_down, x1, target, g2, ln2g, ln2b)


def _ffn_dact(dy2, w_down, dsu, sg, tm, tn, comm):
    t, d = dy2.shape

    def body(dy_ref, w_ref, dsu_ref, sg_ref, dg_ref, du_ref):
        dact = _dot_nt(dy_ref[...], w_ref[...])
        dg_ref[...] = (dact * dsu_ref[...].astype(F32)).astype(BF16)
        du_ref[...] = (dact * sg_ref[...].astype(F32)).astype(BF16)

    tile = pl.BlockSpec((tm, tn), lambda j, i: (i, j))
    shp = jax.ShapeDtypeStruct((t, D_FF), BF16)
    return _call(
        body, name="ffn_dact", grid=(D_FF // tn, t // tm), out_shape=[shp, shp],
        in_specs=[pl.BlockSpec((tm, d), lambda j, i: (i, 0)), pl.BlockSpec((tn, d), lambda j, i: (j, 0)), tile, tile],
        out_specs=[tile, tile], sem=("parallel", "parallel"), comm=comm, args=(dy2, w_down, dsu, sg))


def _ffn_dh2(dgate, dup, w_gu_t, x1, x, y, dz2, sc2, g1, ln1g, tm):
    t, d = x1.shape

    def body(dg_ref, du_ref, w_ref, x1_ref, x_ref, y_ref, dz2_ref, sc_ref, g1_ref, lg_ref,
             dz1_ref, dy_ref, dsc_ref, dsh_ref, dlg_ref, dlb_ref, dg1_ref):
        @pl.when(pl.program_id(0) == 0)
        def _():
            for r in (dsc_ref, dsh_ref, dlg_ref, dlb_ref, dg1_ref):
                r[...] = jnp.zeros_like(r)

        dh2 = _dot(dg_ref[...], w_ref[0:D_FF]) + _dot(du_ref[...], w_ref[D_FF:2 * D_FF])
        x1 = x1_ref[...]
        y = y_ref[...]
        g1 = g1_ref[...]
        dsc_ref[...] += _rowsum8(dh2 * x1)
        dsh_ref[...] += _rowsum8(dh2)
        dx1 = dh2 * (1.0 + sc_ref[...]) + ALPHA * dz2_ref[...]
        xhat, rstd = _ln_stats(ALPHA * x_ref[...] + g1 * y)
        dlg_ref[...] += _rowsum8(dx1 * xhat)
        dlb_ref[...] += _rowsum8(dx1)
        dz1 = _ln_bwd(dx1 * lg_ref[...], xhat, rstd)
        dg1_ref[...] += _rowsum8(dz1 * y)
        dz1_ref[...] = dz1
        dy_ref[...] = (g1 * dz1).astype(BF16)

    row = pl.BlockSpec((tm, d), lambda i: (i, 0))
    wide = pl.BlockSpec((tm, D_FF), lambda i: (i, 0))
    vec = _full((1, d))
    acc = _full((8, d))
    acc_shape = jax.ShapeDtypeStruct((8, d), F32)
    return pl.pallas_call(
        body, name="ffn_dh2", grid=(t // tm,),
        out_shape=[jax.ShapeDtypeStruct((t, d), F32), jax.ShapeDtypeStruct((t, d), BF16)] + [acc_shape] * 5,
        in_specs=[wide, wide, _full((2 * D_FF, d)), row, row, row, row, vec, vec, vec],
        out_specs=[row, row, acc, acc, acc, acc, acc], compiler_params=_params(("arbitrary",)),
    )(dgate, dup, w_gu_t, x1, x, y, dz2, sc2, g1, ln1g)


def _din(dproj, dkvn, w_in_t, x, dz1, sc1, tm, comm):
    t, d = x.shape

    def body(dp_ref, dkv_ref, w_ref, x_ref, dz1_ref, sc_ref, dx_ref, dpb_ref, dbin_ref, dsc_ref, dsh_ref):
        @pl.when(pl.program_id(0) == 0)
        def _():
            for r in (dbin_ref, dsc_ref, dsh_ref):
                r[...] = jnp.zeros_like(r)

        dp = jnp.concatenate([dp_ref[:, 0:ATTN_W], dp_ref[:, ATTN_W:ATTN_W + 2 * KV_W] + dkv_ref[...],
                              dp_ref[:, ATTN_W + 2 * KV_W:IN_W]], axis=1)
        dbin_ref[...] += _rowsum8(dp)
        dpb = dp.astype(BF16)
        dpb_ref[...] = dpb
        dh = _dot(dpb, w_ref[...])
        dsc_ref[...] += _rowsum8(dh * x_ref[...])
        dsh_ref[...] += _rowsum8(dh)
        dx_ref[...] = dh * (1.0 + sc_ref[...]) + ALPHA * dz1_ref[...]

    row = lambda w: pl.BlockSpec((tm, w), lambda i: (i, 0))
    return _call(
        body, name="din", grid=(t // tm,),
        out_shape=[jax.ShapeDtypeStruct((t, d), F32), jax.ShapeDtypeStruct((t, IN_W), BF16),
                   jax.ShapeDtypeStruct((8, IN_W), F32), jax.ShapeDtypeStruct((8, d), F32),
                   jax.ShapeDtypeStruct((8, d), F32)],
        in_specs=[row(IN_W), row(2 * KV_W), _full((IN_W, d)), row(d), row(d), _full((1, d))],
        out_specs=[row(d), row(IN_W), _full((8, IN_W)), _full((8, d)), _full((8, d))],
        sem=("arbitrary",), comm=comm, args=(dproj, dkvn, w_in_t, x, dz1, sc1))


def _wgrad(name, a, b, tmm, tk, comm=None, a2=None):
    t, m = a.shape
    n = b.shape[1]
    nk = t // tk
    nm = m // tmm

    def body(*refs):
        a_refs, (b_ref, o_ref, acc_ref) = refs[:-3], refs[-3:]
        i, k = pl.program_id(0), pl.program_id(1)
        a_tile = a_refs[0][...] if a2 is None else jnp.where(i < nm, a_refs[0][...], a_refs[1][...])
        part = _dot_tn(a_tile, b_ref[...])

        @pl.when(k == 0)
        def _():
            acc_ref[...] = part

        @pl.when(k > 0)
        def _():
            acc_ref[...] += part

        @pl.when(k == nk - 1)
        def _():
            o_ref[...] = acc_ref[...].astype(BF16)

    if a2 is None:
        a_specs, a_args, n_tiles = [pl.BlockSpec((tk, tmm), lambda i, k: (k, i))], (a,), nm
    else:
        a_specs = [pl.BlockSpec((tk, tmm), lambda i, k: (jnp.where(i < nm, k, 0), jnp.minimum(i, nm - 1))),
                   pl.BlockSpec((tk, tmm), lambda i, k: (jnp.where(i < nm, 0, k), jnp.maximum(i - nm, 0)))]
        a_args, n_tiles = (a, a2), 2 * nm
    (out,), got = _call(
        body, name=name, grid=(n_tiles, nk), out_shape=[jax.ShapeDtypeStruct((n_tiles * tmm, n), BF16)],
        in_specs=a_specs + [pl.BlockSpec((tk, n), lambda i, k: (k, 0))],
        out_specs=[pl.BlockSpec((tmm, n), lambda i, k: (i, 0))],
        scratch_shapes=[pltpu.VMEM((tmm, n), F32)], sem=("parallel", "arbitrary"), comm=comm, args=a_args + (b,))
    return out if comm is None else (out, got)


def _adamw(w, g, m, v):
    m = ADAM_B1 * m + (1.0 - ADAM_B1) * g
    v = ADAM_B2 * v + (1.0 - ADAM_B2) * (g * g)
    m_hat = m / (1.0 - ADAM_B1 ** ADAM_STEP)
    v_hat = v / (1.0 - ADAM_B2 ** ADAM_STEP)
    delta = -ADAM_LR * (m_hat / (jnp.sqrt(v_hat) + ADAM_EPS) + ADAM_WD * w)
    return delta, m, v


def _adam_reduce(name, parts, w, m, v, tr):
    r, cdim = w.shape

    def body(p_ref, w_ref, m_ref, v_ref, g_ref, d_ref, mo_ref, vo_ref):
        g = p_ref[0].astype(F32)
        for s in range(1, N_DEV):
            g = g + p_ref[s].astype(F32)
        d_ref[...], mo_ref[...], vo_ref[...] = _adamw(w_ref[...], g, m_ref[...], v_ref[...])
        g_ref[...] = g

    tile = pl.BlockSpec((tr, cdim), lambda i: (i, 0))
    shp = jax.ShapeDtypeStruct((r, cdim), F32)
    return pl.pallas_call(
        body, name=name, grid=(r // tr,), out_shape=[shp] * 4,
        in_specs=[pl.BlockSpec((N_DEV, tr, cdim), lambda i: (0, i, 0)), tile, tile, tile],
        out_specs=[tile] * 4, compiler_params=_params(("parallel",)),
    )(parts, w, m, v)


def _adam_w_ada(c_all_t, dmod_cols, w, m, v):
    def body(ct_ref, dm_ref, w_ref, m_ref, v_ref, g_ref, d_ref, mo_ref, vo_ref):
        ct = ct_ref[...]
        s = (ct * _sigmoid(ct)).astype(BF16)
        g = _dot(s, dm_ref[...].astype(BF16))
        d_ref[...], mo_ref[...], vo_ref[...] = _adamw(w_ref[...], g, m_ref[...], v_ref[...])
        g_ref[...] = g

    shp = jax.ShapeDtypeStruct(w.shape, F32)
    return pl.pallas_call(
        body, name="adam_w_ada", grid=(1,), out_shape=[shp] * 4,
        in_specs=[_full(c_all_t.shape), _full(dmod_cols.shape)] + [_full(w.shape)] * 3,
        out_specs=[_full(w.shape)] * 4, compiler_params=_params(("arbitrary",)),
    )(c_all_t, dmod_cols, w, m, v)


SMALL_EARLY = ["rel_bias", "attn_sinks", "gmlp_ln_g", "gmlp_ln_b", "gmlp_w_s", "gmlp_b_s",
               "attn_out_g", "gmlp_out_g", "ln1_g", "ln1_b", "ln2_g", "ln2_b"]
SMALL_LATE = ["b_ada", "b_in"]
WEIGHTS = ["rel_bias", "w_ada", "b_ada", "w_in", "b_in", "attn_sinks", "gmlp_ln_g", "gmlp_ln_b", "gmlp_w_s",
           "gmlp_b_s", "attn_out_g", "gmlp_out_g", "w_out", "ln1_g", "ln1_b", "w_gate_up", "w_down", "ln2_g", "ln2_b"]


def _seg_rows(nelem):
    return -(-nelem // 1024) * 8


def _pack(named, names):
    parts = []
    for name in names:
        flat = named[name].reshape(-1).astype(F32)
        rows = _seg_rows(flat.shape[0])
        parts.append(jnp.pad(flat, (0, rows * 128 - flat.shape[0])).reshape(rows, 128))
    return jnp.concatenate(parts, axis=0)


def _unpack(packed, shapes, names):
    out, r0 = {}, 0
    for name in names:
        nelem = math.prod(shapes[name])
        rows = _seg_rows(nelem)
        out[name] = packed[r0:r0 + rows].reshape(-1)[:nelem].reshape(shapes[name])
        r0 += rows
    return out


def _t5_bucket_map():
    qi = jnp.arange(BLK)[:, None]
    si = jnp.arange(2 * BLK)[None, :]
    n = jnp.maximum(qi + BLK - si, 0)
    max_exact = N_BUCKETS // 2
    nf = jnp.maximum(n, max_exact).astype(F32)
    large = max_exact + (jnp.log(nf / max_exact) / math.log(MAX_DISTANCE / max_exact)
                         * (N_BUCKETS - max_exact)).astype(jnp.int32)
    large = jnp.minimum(large, N_BUCKETS - 1)
    return jnp.where(n < max_exact, n, large).astype(jnp.int32)


def kernel(x, c, rel_bias, w_ada, b_ada, w_in, b_in, attn_sinks, gmlp_ln_g, gmlp_ln_b, gmlp_w_s, gmlp_b_s, attn_out_g, gmlp_out_g, w_out, ln1_g, ln1_b, w_gate_up, w_down, ln2_g, ln2_b, loss_target, m_rel_bias, m_w_ada, m_b_ada, m_w_in, m_b_in, m_attn_sinks, m_gmlp_ln_g, m_gmlp_ln_b, m_gmlp_w_s, m_gmlp_b_s, m_attn_out_g, m_gmlp_out_g, m_w_out, m_ln1_g, m_ln1_b, m_w_gate_up, m_w_down, m_ln2_g, m_ln2_b, v_rel_bias, v_w_ada, v_b_ada, v_w_in, v_b_in, v_attn_sinks, v_gmlp_ln_g, v_gmlp_ln_b, v_gmlp_w_s, v_gmlp_b_s, v_attn_out_g, v_gmlp_out_g, v_w_out, v_ln1_g, v_ln1_b, v_w_gate_up, v_w_down, v_ln2_g, v_ln2_b):
    wts = dict(rel_bias=rel_bias, w_ada=w_ada, b_ada=b_ada, w_in=w_in, b_in=b_in, attn_sinks=attn_sinks,
               gmlp_ln_g=gmlp_ln_g, gmlp_ln_b=gmlp_ln_b, gmlp_w_s=gmlp_w_s, gmlp_b_s=gmlp_b_s,
               attn_out_g=attn_out_g, gmlp_out_g=gmlp_out_g, w_out=w_out, ln1_g=ln1_g, ln1_b=ln1_b,
               w_gate_up=w_gate_up, w_down=w_down, ln2_g=ln2_g, ln2_b=ln2_b)
    mom_m = dict(rel_bias=m_rel_bias, w_ada=m_w_ada, b_ada=m_b_ada, w_in=m_w_in, b_in=m_b_in,
                 attn_sinks=m_attn_sinks, gmlp_ln_g=m_gmlp_ln_g, gmlp_ln_b=m_gmlp_ln_b, gmlp_w_s=m_gmlp_w_s,
                 gmlp_b_s=m_gmlp_b_s, attn_out_g=m_attn_out_g, gmlp_out_g=m_gmlp_out_g, w_out=m_w_out,
                 ln1_g=m_ln1_g, ln1_b=m_ln1_b, w_gate_up=m_w_gate_up, w_down=m_w_down, ln2_g=m_ln2_g,
                 ln2_b=m_ln2_b)
    mom_v = dict(rel_bias=v_rel_bias, w_ada=v_w_ada, b_ada=v_b_ada, w_in=v_w_in, b_in=v_b_in,
                 attn_sinks=v_attn_sinks, gmlp_ln_g=v_gmlp_ln_g, gmlp_ln_b=v_gmlp_ln_b, gmlp_w_s=v_gmlp_w_s,
                 gmlp_b_s=v_gmlp_b_s, attn_out_g=v_attn_out_g, gmlp_out_g=v_gmlp_out_g, w_out=v_w_out,
                 ln1_g=v_ln1_g, ln1_b=v_ln1_b, w_gate_up=v_w_gate_up, w_down=v_w_down, ln2_g=v_ln2_g,
                 ln2_b=v_ln2_b)

    t = x.shape[1]
    tm = min(512, t)
    tn_ff = D_FF // 2
    tk_tok = min(1024, t)
    me = 4 * lax.axis_index("x") + 2 * lax.axis_index("y") + lax.axis_index("c")
    xs = x[0]
    target = loss_target[0]

    c_g, w_in_g = _exchange("gather_in", [jnp.broadcast_to(c, (8, D_MODEL)), w_in[0].T.astype(BF16)],
                            ("gather", "gather2"))
    c_all = c_g[:, 0, :]
    w_in_t = w_in_g.reshape(IN_W, D_MODEL)

    ncol = w_ada.shape[2]
    b_cols = lax.dynamic_slice(b_ada, (0, me * ncol), (1, ncol))
    mod_part = _mod_partial(c_all, w_ada[0], b_cols)
    (mod_g,) = _exchange("gather_mod", [mod_part], ("gather",))
    mod = lax.dynamic_slice(mod_g, (0, me, 0), (N_DEV, 1, ncol)).reshape(1, N_DEV * ncol)
    sh1, sc1, g1, sh2, sc2, g2 = [mod[:, i * D_MODEL:(i + 1) * D_MODEL] for i in range(6)]

    bucket = _t5_bucket_map()
    bias = _bias_table(rel_bias, bucket)
    causal = jnp.tril(jnp.ones((BLK, BLK), dtype=bool))
    ws = jnp.where(causal[None], gmlp_w_s[0], 0.0).astype(BF16)
    pair = lambda w: jnp.concatenate([w[0::2], w[1::2]], axis=2)
    ws2, wst2 = pair(ws), pair(jnp.swapaxes(ws, 1, 2))
    bfull = jnp.repeat(gmlp_b_s[0].T, GMLP_W // N_GROUPS, axis=1)
    sinks = attn_sinks[0]

    proj, h1 = _inproj(xs, sc1, sh1, w_in_t, b_in, tm)
    (mixed,), (w_out_g, w_gu_g) = _mix_fwd(
        proj, bias, sinks, gmlp_ln_g, gmlp_ln_b, ws2, bfull, attn_out_g, gmlp_out_g,
        comm=([w_out[0].astype(BF16), w_gate_up[0].T.astype(BF16)], ("gather2", "gather2")))
    w_out_f = w_out_g.reshape(D_MODEL, D_MODEL)
    w_gu_t = w_gu_g.reshape(2 * D_FF, D_MODEL)
    y1, x1, h2 = _outproj(mixed, w_out_f, xs, g1, ln1_g, ln1_b, sc2, sh2, tm)
    (dsu, sg, act), (w_down_g,) = _ffn_up(h2, w_gu_t, tm, tn_ff, comm=([w_down[0].astype(BF16)], ("gather2",)))
    w_down_f = w_down_g.reshape(D_FF, D_MODEL)
    dz2, dy2, loss_p, d_ln2g, d_ln2b, d_g2 = _ffn_down(act, w_down_f, x1, target, g2, ln2_g, ln2_b, tm)
    loss = lax.psum(0.5 / D_MODEL * jnp.sum(loss_p), ("x", "y", "c"))

    slots = lambda a: a.reshape(N_DEV, -1, D_MODEL)
    dw_down = _wgrad("wgrad_down", act, dy2, tn_ff, tk_tok)
    (dgate, dup), (r_down,) = _ffn_dact(dy2, w_down_f, dsu, sg, tm, tn_ff, comm=([slots(dw_down)], ("scatter",)))
    dz1, dy1, d_sc2, d_sh2, d_ln1g, d_ln1b, d_g1 = _ffn_dh2(dgate, dup, w_gu_t, x1, xs, y1, dz2, sc2, g1, ln1_g,
                                                           min(256, t))
    dw_gu_t = _wgrad("wgrad_gate_up", dgate, h2, tn_ff, tk_tok, a2=dup)
    dw_out = _wgrad("wgrad_out", mixed, dy1, D_MODEL, tk_tok)
    ((dproj, dkvn, dl_acc, dsink_acc, d_lng, d_lnb, d_ws, d_bs, d_aog, d_gog), (r_gu, r_out)) = _mix_bwd(
        proj, bias, sinks, gmlp_ln_g, gmlp_ln_b, ws2, wst2, bfull, attn_out_g, gmlp_out_g, dy1, w_out_f,
        comm=([slots(dw_gu_t), slots(dw_out)], ("scatter", "scatter")))
    d_relb = _bias_grad(dl_acc, bucket)

    rsum = lambda a: jnp.sum(a, axis=0)
    early_g = dict(
        rel_bias=d_relb[:, 0, :N_BUCKETS].T, attn_sinks=rsum(dsink_acc)[:N_HEADS],
        gmlp_ln_g=rsum(d_lng), gmlp_ln_b=rsum(d_lnb), gmlp_w_s=jnp.where(causal[None], d_ws, 0.0),
        gmlp_b_s=jnp.sum(d_bs.reshape(BLK, N_GROUPS, GMLP_W // N_GROUPS), axis=2).T,
        attn_out_g=rsum(d_aog), gmlp_out_g=rsum(d_gog), ln1_g=rsum(d_ln1g), ln1_b=rsum(d_ln1b),
        ln2_g=rsum(d_ln2g), ln2_b=rsum(d_ln2b))
    (grad_x, dproj_b, d_bin, d_sc1, d_sh1), _ = _din(dproj, dkvn, w_in_t, xs, dz1, sc1, tm, comm=None)
    dw_in_t, (early_all,) = _wgrad("wgrad_in", dproj_b, h1, IN_W, tk_tok,
                                   comm=([_pack(early_g, SMALL_EARLY)], ("gather2",)))
    dmod = jnp.concatenate([rsum(d_sh1), rsum(d_sc1), rsum(d_g1), rsum(d_sh2), rsum(d_sc2), rsum(d_g2)])
    late_all, r_in = _exchange("scatter_in", [_pack(dict(b_ada=dmod, b_in=rsum(d_bin)), SMALL_LATE), slots(dw_in_t)],
                               ("gather", "scatter"))

    small = [{}, {}, {}, {}]
    for label, names, parts in (("adam_small_early", SMALL_EARLY, early_all), ("adam_small_late", SMALL_LATE, late_all)):
        res = _adam_reduce(label, parts, _pack(wts, names), _pack(mom_m, names), _pack(mom_v, names), parts.shape[1])
        shapes = {k: wts[k].shape for k in names}
        for i in range(4):
            small[i].update(_unpack(res[i], shapes, names))

    dmod_all = late_all[:, :_seg_rows(6 * D_MODEL), :].reshape(N_DEV, 6 * D_MODEL)
    dmod_cols = lax.dynamic_slice(dmod_all, (0, me * ncol), (N_DEV, ncol))
    kpad = 128 - N_DEV
    ada = _adam_w_ada(jnp.pad(c_all.T, ((0, 0), (0, kpad))), jnp.pad(dmod_cols, ((0, kpad), (0, 0))),
                      w_ada[0], m_w_ada[0], v_w_ada[0])

    tr = lambda a: jnp.swapaxes(a, -1, -2)
    big = {}
    big["w_in"] = [tr(o)[None] for o in _adam_reduce("adam_w_in", r_in, w_in[0].T, m_w_in[0].T, v_w_in[0].T, 112)]
    big["w_out"] = [o[None] for o in _adam_reduce("adam_w_out", r_out, w_out[0], m_w_out[0], v_w_out[0], 128)]
    big["w_gate_up"] = [tr(o)[None] for o in _adam_reduce("adam_w_gu", r_gu, w_gate_up[0].T, m_w_gate_up[0].T,
                                                           v_w_gate_up[0].T, 352)]
    big["w_down"] = [o[None] for o in _adam_reduce("adam_w_down", r_down, w_down[0], m_w_down[0], v_w_down[0], 176)]
    big["w_ada"] = [o[None] for o in ada]

    outs = [[], [], [], []]
    for name in WEIGHTS:
        for i in range(4):
            outs[i].append(big[name][i] if name in big else small[i][name])
    return (loss, grad_x[None], *outs[0], *outs[1], *outs[2], *outs[3])
```

```python
import math

import jax
import jax.numpy as jnp
from jax import lax
from jax.experimental import pallas as pl
from jax.experimental.pallas import tpu as pltpu

F32 = jnp.float32
BF16 = jnp.bfloat16
MESH = pl.DeviceIdType.MESH

N_DEV = 8
D_MODEL = 1024
HEAD_DIM = 64
N_HEADS = 8
N_GROUPS = 8
ATTN_W = 512
KV_W = 128
GMLP_W = 512
IN_W = 1792
BLK = 128
N_BUCKETS = 32
MAX_DISTANCE = 128
D_FF = 2816
ALPHA = 2.0 ** 0.25
LN_EPS = 1e-5
NEG_INF = -1e30
ADAM_LR = 0.001
ADAM_B1 = 0.9
ADAM_B2 = 0.999
ADAM_EPS = 1e-08
ADAM_WD = 0.01
ADAM_STEP = 10
GELU_C0 = math.sqrt(2.0 / math.pi)
GELU_C1 = 0.044715

VMEM_LIMIT = 56 * 1024 * 1024


def _params(sem):
    return pltpu.CompilerParams(dimension_semantics=sem, vmem_limit_bytes=VMEM_LIMIT)


def _dot(a, b):
    return lax.dot_general(a, b, (((1,), (0,)), ((), ())), preferred_element_type=F32)


def _dot_nt(a, b):
    return lax.dot_general(a, b, (((1,), (1,)), ((), ())), preferred_element_type=F32)


def _dot_tn(a, b):
    return lax.dot_general(a, b, (((0,), (0,)), ((), ())), preferred_element_type=F32)


def _full(shape):
    nd = len(shape)
    return pl.BlockSpec(shape, lambda *_: (0,) * nd)


def _rowsum8(v):
    r, c = v.shape
    return jnp.sum(v.reshape(r // 8, 8, c), axis=0)


def _sigmoid(v):
    return 1.0 / (1.0 + jnp.exp(-v))


def _gelu_parts(v):
    v2 = v * v
    t = jnp.tanh(GELU_C0 * (v + GELU_C1 * v * v2))
    g = 0.5 * v * (1.0 + t)
    dg = 0.5 * (1.0 + t) + 0.5 * v * (1.0 - t * t) * (GELU_C0 * (1.0 + 3.0 * GELU_C1 * v2))
    return g, dg


def _ln_stats(z):
    mu = jnp.mean(z, axis=1, keepdims=True)
    zc = z - mu
    var = jnp.mean(zc * zc, axis=1, keepdims=True)
    rstd = lax.rsqrt(var + LN_EPS)
    return zc * rstd, rstd


def _ln_bwd(dxhat, xhat, rstd):
    m1 = jnp.mean(dxhat, axis=1, keepdims=True)
    m2 = jnp.mean(dxhat * xhat, axis=1, keepdims=True)
    return rstd * (dxhat - m1 - xhat * m2)


def _seg_mean64(v):
    r = v.shape[0]
    lo = lax.broadcasted_iota(jnp.int32, (r, 128), 1) < 64
    outs = []
    for j in range(v.shape[1] // 128):
        ch = v[:, 128 * j:128 * (j + 1)]
        s_lo = jnp.sum(jnp.where(lo, ch, 0.0), axis=1, keepdims=True)
        s_hi = jnp.sum(jnp.where(lo, 0.0, ch), axis=1, keepdims=True)
        outs.append(jnp.where(lo, s_lo, s_hi) * (1.0 / 64.0))
    return jnp.concatenate(outs, axis=1)


def _rms(a, g):
    r = lax.rsqrt(jnp.mean(a * a, axis=1, keepdims=True) + LN_EPS)
    return a * r * g, r


def _rms_bwd(dout, a, r, g):
    t = dout * g
    return r * t - a * (r * r * r) * jnp.mean(t * a, axis=1, keepdims=True)


PEER_ORDER = (1, 2, 4, 3, 5, 6, 7)


def _peer(j):
    x, y, c = lax.axis_index("x"), lax.axis_index("y"), lax.axis_index("c")
    px = 1 - x if j & 4 else x
    py = 1 - y if j & 2 else y
    pc = 1 - c if j & 1 else c
    return (px, py, pc), 4 * px + 2 * py + pc


SIBLING = 1
CHIP_FLIPS = (4, 2, 6)


def _exchange_phase(phase, ins, outs, modes, send_sems, recv_sems, loc_sems):
    me = 4 * lax.axis_index("x") + 2 * lax.axis_index("y") + lax.axis_index("c")
    for k, mode in enumerate(modes):
        def copy(i, src, slot, dev, k=k):
            return pltpu.make_async_remote_copy(src_ref=src, dst_ref=outs[k].at[slot], send_sem=send_sems.at[k, i],
                                                recv_sem=recv_sems.at[k, i], device_id=dev, device_id_type=MESH)

        src_me = ins[k].at[me] if mode == "scatter" else ins[k]
        local = pltpu.make_async_copy(src_me, outs[k].at[me], loc_sems.at[k])
        if mode == "gather2":
            sib_dev, sib_idx = _peer(SIBLING)
            chips = [_peer(j) for j in CHIP_FLIPS]
            far = [_peer(j | SIBLING)[1] for j in CHIP_FLIPS]
            if phase == "start":
                local.start()
                copy(0, ins[k], me, sib_dev).start()
                for i, (dev, _) in enumerate(chips):
                    copy(1 + i, ins[k], me, dev).start()
            elif phase == "mid":
                for i, (dev, idx) in enumerate(chips):
                    copy(1 + i, ins[k], idx, dev).wait_recv()
                    copy(4 + i, outs[k].at[idx], idx, sib_dev).start()
            else:
                copy(0, ins[k], sib_idx, sib_dev).wait_recv()
                for i, slot in enumerate(far):
                    copy(4 + i, ins[k], slot, sib_dev).wait_recv()
                copy(0, ins[k], me, sib_dev).wait_send()
                for i, (dev, idx) in enumerate(chips):
                    copy(1 + i, ins[k], me, dev).wait_send()
                    copy(4 + i, outs[k].at[idx], idx, sib_dev).wait_send()
                local.wait()
            continue
        peers = [_peer(j) for j in PEER_ORDER]
        if phase == "start":
            local.start()
            for i, (dev, idx) in enumerate(peers):
                copy(i, ins[k].at[idx] if mode == "scatter" else ins[k], me, dev).start()
        elif phase == "end":
            for i, (dev, idx) in enumerate(peers):
                copy(i, src_me, idx, dev).wait_recv()
            for i, (dev, idx) in enumerate(peers):
                copy(i, src_me, me, dev).wait_send()
            local.wait()


def _exchange_shapes(arrays, modes):
    return [jax.ShapeDtypeStruct((N_DEV,) + (a.shape[1:] if m == "scatter" else a.shape), a.dtype)
            for a, m in zip(arrays, modes)]


def _exchange_sems(n):
    return [pltpu.SemaphoreType.DMA((n, N_DEV - 1)), pltpu.SemaphoreType.DMA((n, N_DEV - 1)),
            pltpu.SemaphoreType.DMA((n,))]


def _exchange(name, arrays, modes):
    n = len(arrays)

    def body(*refs):
        for phase in ("start", "mid", "end"):
            _exchange_phase(phase, refs[:n], refs[n:2 * n], modes, *refs[2 * n:])

    any_spec = pl.BlockSpec(memory_space=pl.ANY)
    return pl.pallas_call(
        body, name=name, out_shape=_exchange_shapes(arrays, modes),
        in_specs=[any_spec] * n, out_specs=[any_spec] * n, scratch_shapes=_exchange_sems(n),
    )(*arrays)


def _call(body, *, name, grid, in_specs, out_specs, out_shape, args, sem, scratch_shapes=(), comm=None):
    if comm is None:
        outs = pl.pallas_call(body, name=name, grid=grid, in_specs=list(in_specs), out_specs=list(out_specs),
                              out_shape=list(out_shape), scratch_shapes=list(scratch_shapes),
                              compiler_params=_params(sem))(*args)
        return list(outs), []
    arrays, modes = comm
    n_in, n_out, nc, ns = len(in_specs), len(out_specs), len(arrays), len(scratch_shapes)
    n_steps = math.prod(grid)

    def hosted(*refs):
        ins, cins = refs[:n_in], refs[n_in:n_in + nc]
        outs, couts = refs[n_in + nc:n_in + nc + n_out], refs[n_in + nc + n_out:n_in + 2 * nc + n_out]
        scratch = refs[n_in + 2 * nc + n_out:]
        ex = (cins, couts, modes) + tuple(scratch[ns:])
        step = pl.program_id(0)
        for ax in range(1, len(grid)):
            step = step * grid[ax] + pl.program_id(ax)

        @pl.when(step == 0)
        def _():
            _exchange_phase("start", *ex)

        body(*ins, *outs, *scratch[:ns])

        if "gather2" in modes:
            @pl.when(step == (3 * n_steps) // 4)
            def _():
                _exchange_phase("mid", *ex)

        @pl.when(step == n_steps - 1)
        def _():
            _exchange_phase("end", *ex)

    any_spec = pl.BlockSpec(memory_space=pl.ANY)
    res = pl.pallas_call(
        hosted, name=name, grid=grid, in_specs=list(in_specs) + [any_spec] * nc,
        out_specs=list(out_specs) + [any_spec] * nc, out_shape=list(out_shape) + _exchange_shapes(arrays, modes),
        scratch_shapes=list(scratch_shapes) + _exchange_sems(nc),
        compiler_params=_params(tuple("arbitrary" for _ in grid)))(*args, *arrays)
    return list(res[:n_out]), list(res[n_out:])


def _mod_partial(c_all, w_ada, b_ada_cols):
    def body(c_ref, w_ref, b_ref, o_ref):
        cv = c_ref[...]
        s = (cv * _sigmoid(cv)).astype(BF16)
        o_ref[...] = _dot(s, w_ref[...].astype(BF16)) + b_ref[...]

    ncol = w_ada.shape[1]
    return pl.pallas_call(
        body, name="mod_partial", out_shape=jax.ShapeDtypeStruct((N_DEV, ncol), F32),
        in_specs=[_full(c_all.shape), _full(w_ada.shape), _full(b_ada_cols.shape)],
        out_specs=_full((N_DEV, ncol)), grid=(1,), compiler_params=_params(("arbitrary",)),
    )(c_all, w_ada, b_ada_cols)


def _bias_table(rel_bias, bucket):
    def body(rb_ref, bk_ref, o_ref):
        h = pl.program_id(0)
        bk = bk_ref[...]
        acc = jnp.zeros((BLK, 2 * BLK), F32)
        for b in range(N_BUCKETS):
            acc = jnp.where(bk == b, rb_ref[b, h], acc)
        dist = (lax.broadcasted_iota(jnp.int32, (BLK, 2 * BLK), 0) + BLK
                - lax.broadcasted_iota(jnp.int32, (BLK, 2 * BLK), 1))
        o_ref[0] = jnp.where((dist >= 0) & (dist < BLK), acc, NEG_INF)

    return pl.pallas_call(
        body, name="bias_table", out_shape=jax.ShapeDtypeStruct((N_HEADS, BLK, 2 * BLK), F32),
        in_specs=[pl.BlockSpec(memory_space=pltpu.SMEM), _full((BLK, 2 * BLK))],
        out_specs=pl.BlockSpec((1, BLK, 2 * BLK), lambda h: (h, 0, 0)), grid=(N_HEADS,),
        compiler_params=_params(("arbitrary",)),
    )(rel_bias, bucket)


def _bias_grad(dl_acc, bucket):
    def body(dl_ref, bk_ref, o_ref):
        bk = bk_ref[...]
        dl = dl_ref[0]
        lane = lax.broadcasted_iota(jnp.int32, (1, 128), 1)
        row = jnp.zeros((1, 128), F32)
        for b in range(N_BUCKETS):
            s = jnp.sum(jnp.sum(jnp.where(bk == b, dl, 0.0), axis=1, keepdims=True), axis=0, keepdims=True)
            row = jnp.where(lane == b, s, row)
        o_ref[0] = row

    return pl.pallas_call(
        body, name="bias_grad", out_shape=jax.ShapeDtypeStruct((N_HEADS, 1, 128), F32),
        in_specs=[pl.BlockSpec((1, BLK, 2 * BLK), lambda h: (h, 0, 0)), _full((BLK, 2 * BLK))],
        out_specs=pl.BlockSpec((1, 1, 128), lambda h: (h, 0, 0)), grid=(N_HEADS,),
        compiler_params=_params(("arbitrary",)),
    )(dl_acc, bucket)


def _inproj(x, sc1, sh1, w_in_t, b_in, tm):
    t, d = x.shape
    n = w_in_t.shape[0]

    def body(x_ref, sc_ref, sh_ref, w_ref, b_ref, proj_ref, h_ref):
        h = (x_ref[...] * (1.0 + sc_ref[...]) + sh_ref[...]).astype(BF16)
        h_ref[...] = h
        proj_ref[...] = _dot_nt(h, w_ref[...]) + b_ref[...]

    row = lambda w: pl.BlockSpec((tm, w), lambda i: (i, 0))
    return pl.pallas_call(
        body, name="inproj", grid=(t // tm,),
        out_shape=[jax.ShapeDtypeStruct((t, n), F32), jax.ShapeDtypeStruct((t, d), BF16)],
        in_specs=[row(d), _full((1, d)), _full((1, d)), _full((n, d)), _full((1, n))],
        out_specs=[row(n), row(d)], compiler_params=_params(("parallel",)),
    )(x, sc1, sh1, w_in_t, b_in)


def _half_masks():
    lo_q = lax.broadcasted_iota(jnp.int32, (BLK, 128), 1) < 64
    lo_k = lax.broadcasted_iota(jnp.int32, (2 * BLK, 128), 1) < 64
    return lo_q, lo_k


def _head_place(h):
    return h // 2, h % 2, h // 4


def _attn_heads(q, kk, vv, bias_ref, sinks_ref, n):
    lo_q, lo_k = _half_masks()
    kkb, kksb = kk.astype(BF16), pltpu.roll(kk, 64, 1).astype(BF16)
    vvb, vvsb = vv.astype(BF16), pltpu.roll(vv, 64, 1).astype(BF16)
    n0mask = (n == 0) & (lax.broadcasted_iota(jnp.int32, (BLK, 2 * BLK), 1) < BLK)
    chunks, probs = [], []
    for j in range(4):
        qc = q[:, 128 * j:128 * (j + 1)]
        acc = jnp.zeros((BLK, 128), F32)
        for pos in range(2):
            h = 2 * j + pos
            direct = (h // 4) == pos
            mq = lo_q if pos == 0 else jnp.logical_not(lo_q)
            mk = lo_k if pos == 0 else jnp.logical_not(lo_k)
            qm = jnp.where(mq, qc, 0.0).astype(BF16)
            logit = _dot_nt(qm, kkb if direct else kksb) * (HEAD_DIM ** -0.5) + bias_ref[h]
            logit = jnp.where(n0mask, NEG_INF, logit)
            sk = sinks_ref[h]
            m = jnp.maximum(jnp.max(logit, axis=1, keepdims=True), sk)
            e = jnp.exp(logit - m)
            es = jnp.exp(sk - m)
            den = jnp.sum(e, axis=1, keepdims=True) + es
            p = e / den
            vm = jnp.where(mk, vvb if direct else vvsb, jnp.zeros_like(vvb))
            acc = acc + _dot(p.astype(BF16), vm)
            probs.append((p, es / den))
        chunks.append(acc)
    return jnp.concatenate(chunks, axis=1), probs


def _gmlp_block(gu, gv, lng, lnb, ws_ref, bfull):
    lo_q, _ = _half_masks()
    u, du = _gelu_parts(gu)
    a, da = _gelu_parts(gv)
    mu = _seg_mean64(a)
    ac = a - mu
    rstd = lax.rsqrt(_seg_mean64(ac * ac) + LN_EPS)
    vhat = ac * rstd
    vn = vhat * lng + lnb
    chunks = []
    for j in range(4):
        vc = vn[:, 128 * j:128 * (j + 1)]
        acc = jnp.zeros((BLK, 128), F32)
        for pos in range(2):
            mq = lo_q if pos == 0 else jnp.logical_not(lo_q)
            acc = acc + _dot(ws_ref[2 * j + pos], jnp.where(mq, vc, 0.0).astype(BF16))
        chunks.append(acc)
    ms = jnp.concatenate(chunks, axis=1) + bfull
    return u * ms, (u, du, da, vhat, rstd, vn, ms)


def _mix_in_specs(nb):
    return [pl.BlockSpec((BLK, IN_W), lambda n: (n, 0)),
            pl.BlockSpec((BLK, 2 * KV_W), lambda n: (jnp.maximum(n - 1, 0), ATTN_W // (2 * KV_W))),
            _full((N_HEADS, BLK, 2 * BLK)),
            pl.BlockSpec(memory_space=pltpu.SMEM),
            _full((1, GMLP_W)), _full((1, GMLP_W)),
            _full((N_GROUPS, BLK, BLK)), _full((BLK, GMLP_W)),
            _full((1, ATTN_W)), _full((1, GMLP_W))]


def _split_proj(proj_ref, kvp_ref):
    q = proj_ref[:, 0:ATTN_W]
    k = proj_ref[:, ATTN_W:ATTN_W + KV_W]
    v = proj_ref[:, ATTN_W + KV_W:ATTN_W + 2 * KV_W]
    gu = proj_ref[:, ATTN_W + 2 * KV_W:ATTN_W + 2 * KV_W + GMLP_W]
    gv = proj_ref[:, ATTN_W + 2 * KV_W + GMLP_W:IN_W]
    kk = jnp.concatenate([kvp_ref[:, 0:KV_W], k], axis=0)
    vv = jnp.concatenate([kvp_ref[:, KV_W:2 * KV_W], v], axis=0)
    return q, kk, vv, gu, gv


def _mix_fwd(proj, bias, sinks, lng, lnb, ws, bfull, aog, gog, comm):
    t = proj.shape[0]
    nb = t // BLK

    def body(proj_ref, kvp_ref, bias_ref, sinks_ref, lng_ref, lnb_ref, ws_ref, bfull_ref, aog_ref, gog_ref, out_ref):
        n = pl.program_id(0)
        q, kk, vv, gu, gv = _split_proj(proj_ref, kvp_ref)
        attn, _ = _attn_heads(q, kk, vv, bias_ref, sinks_ref, n)
        gm, _ = _gmlp_block(gu, gv, lng_ref[...], lnb_ref[...], ws_ref, bfull_ref[...])
        out_ref[:, 0:ATTN_W] = _rms(attn, aog_ref[...])[0].astype(BF16)
        out_ref[:, ATTN_W:ATTN_W + GMLP_W] = _rms(gm, gog_ref[...])[0].astype(BF16)

    return _call(
        body, name="mix_fwd", grid=(nb,), out_shape=[jax.ShapeDtypeStruct((t, D_MODEL), BF16)],
        in_specs=_mix_in_specs(nb), out_specs=[pl.BlockSpec((BLK, D_MODEL), lambda n: (n, 0))],
        sem=("parallel",), comm=comm, args=(proj, proj, bias, sinks, lng, lnb, ws, bfull, aog, gog))


def _mix_bwd(proj, bias, sinks, lng, lnb, ws, ws_t, bfull, aog, gog, dy, w_out, comm):
    t = proj.shape[0]
    nb = t // BLK

    def body(proj_ref, kvp_ref, bias_ref, sinks_ref, lng_ref, lnb_ref, ws_ref, bfull_ref, aog_ref, gog_ref,
             wst_ref, dy_ref, wout_ref,
             dproj_ref, dkvn_ref, dl_ref, dsink_ref, dlng_ref, dlnb_ref, dws_ref, dbs_ref, daog_ref, dgog_ref):
        n = pl.program_id(0)

        @pl.when(n == 0)
        def _():
            for r in (dl_ref, dsink_ref, dlng_ref, dlnb_ref, dws_ref, dbs_ref, daog_ref, dgog_ref):
                r[...] = jnp.zeros_like(r)

        lo_q, lo_k = _half_masks()
        q, kk, vv, gu, gv = _split_proj(proj_ref, kvp_ref)
        dmix = _dot_nt(dy_ref[...], wout_ref[...])
        dma, dmg = dmix[:, 0:ATTN_W], dmix[:, ATTN_W:ATTN_W + GMLP_W]

        attn, probs = _attn_heads(q, kk, vv, bias_ref, sinks_ref, n)
        aog = aog_ref[...]
        _, r_a = _rms(attn, aog)
        daog_ref[...] += _rowsum8(dma * attn * r_a)
        dattn = _rms_bwd(dma, attn, r_a, aog)

        kkb, kksb = kk.astype(BF16), pltpu.roll(kk, 64, 1).astype(BF16)
        vvb, vvsb = vv.astype(BF16), pltpu.roll(vv, 64, 1).astype(BF16)
        lane = lax.broadcasted_iota(jnp.int32, (BLK, 128), 1)
        dk_d = jnp.zeros((2 * BLK, 128), F32)
        dk_s = jnp.zeros((2 * BLK, 128), F32)
        dv_d = jnp.zeros((2 * BLK, 128), F32)
        dv_s = jnp.zeros((2 * BLK, 128), F32)
        dsink = jnp.zeros((BLK, 128), F32)
        dq_chunks = []
        for j in range(4):
            qc = q[:, 128 * j:128 * (j + 1)]
            doc = dattn[:, 128 * j:128 * (j + 1)]
            dq = jnp.zeros((BLK, 128), F32)
            for pos in range(2):
                h = 2 * j + pos
                direct = (h // 4) == pos
                mq = lo_q if pos == 0 else jnp.logical_not(lo_q)
                mk = lo_k if pos == 0 else jnp.logical_not(lo_k)
                p, psink = probs[h]
                qm = jnp.where(mq, qc, 0.0).astype(BF16)
                dom = jnp.where(mq, doc, 0.0).astype(BF16)
                dp = _dot_nt(dom, vvb if direct else vvsb)
                rs = jnp.sum(p * dp, axis=1, keepdims=True)
                dl = p * (dp - rs)
                dl_ref[h] += dl
                dsink = dsink + jnp.where(lane == h, -psink * rs, 0.0)
                dls = (dl * (HEAD_DIM ** -0.5)).astype(BF16)
                km = jnp.where(mk, kkb if direct else kksb, jnp.zeros_like(kkb))
                dq = dq + _dot(dls, km)
                dk_h = _dot_tn(dls, qm)
                dv_h = _dot_tn(p.astype(BF16), dom)
                if direct:
                    dk_d, dv_d = dk_d + dk_h, dv_d + dv_h
                else:
                    dk_s, dv_s = dk_s + dk_h, dv_s + dv_h
            dq_chunks.append(dq)
        dsink_ref[...] += dsink
        dk = dk_d + pltpu.roll(dk_s, 64, 1)
        dv = dv_d + pltpu.roll(dv_s, 64, 1)
        for j in range(4):
            dproj_ref[:, 128 * j:128 * (j + 1)] = dq_chunks[j]
        dproj_ref[:, ATTN_W:ATTN_W + KV_W] = dk[BLK:2 * BLK]
        dproj_ref[:, ATTN_W + KV_W:ATTN_W + 2 * KV_W] = dv[BLK:2 * BLK]
        dkvn_ref[:, 0:KV_W] = dk[0:BLK]
        dkvn_ref[:, KV_W:2 * KV_W] = dv[0:BLK]

        lng = lng_ref[...]
        gog = gog_ref[...]
        gm, (u, du, da, vhat, rstd, vn, ms) = _gmlp_block(gu, gv, lng, lnb_ref[...], ws_ref, bfull_ref[...])
        _, r_g = _rms(gm, gog)
        dgog_ref[...] += _rowsum8(dmg * gm * r_g)
        dgm = _rms_bwd(dmg, gm, r_g, gog)
        dproj_ref[:, ATTN_W + 2 * KV_W:ATTN_W + 2 * KV_W + GMLP_W] = dgm * ms * du
        dms = dgm * u
        dbs_ref[...] += dms
        dvn_chunks = []
        for j in range(4):
            dmc = dms[:, 128 * j:128 * (j + 1)]
            vcb = vn[:, 128 * j:128 * (j + 1)].astype(BF16)
            acc = jnp.zeros((BLK, 128), F32)
            for pos in range(2):
                g = 2 * j + pos
                mq = lo_q if pos == 0 else jnp.logical_not(lo_q)
                dm = jnp.where(mq, dmc, 0.0).astype(BF16)
                dws_ref[g] += _dot_nt(dm, vcb)
                acc = acc + _dot(wst_ref[g], dm)
            dvn_chunks.append(acc)
        dvn = jnp.concatenate(dvn_chunks, axis=1)
        dlng_ref[...] += _rowsum8(dvn * vhat)
        dlnb_ref[...] += _rowsum8(dvn)
        dvh = dvn * lng
        dact = rstd * (dvh - _seg_mean64(dvh) - vhat * _seg_mean64(dvh * vhat))
        dproj_ref[:, ATTN_W + 2 * KV_W + GMLP_W:IN_W] = dact * da

    acc8 = lambda w: jax.ShapeDtypeStruct((8, w), F32)
    out_shape = [jax.ShapeDtypeStruct((t, IN_W), F32), jax.ShapeDtypeStruct((t, 2 * KV_W), F32),
                 jax.ShapeDtypeStruct((N_HEADS, BLK, 2 * BLK), F32), jax.ShapeDtypeStruct((BLK, 128), F32),
                 acc8(GMLP_W), acc8(GMLP_W), jax.ShapeDtypeStruct((N_GROUPS, BLK, BLK), F32),
                 jax.ShapeDtypeStruct((BLK, GMLP_W), F32), acc8(ATTN_W), acc8(GMLP_W)]
    out_specs = [pl.BlockSpec((BLK, IN_W), lambda n: (n, 0)),
                 pl.BlockSpec((BLK, 2 * KV_W), lambda n: ((n + nb - 1) % nb, 0)),
                 _full((N_HEADS, BLK, 2 * BLK)), _full((BLK, 128)), _full((8, GMLP_W)), _full((8, GMLP_W)),
                 _full((N_GROUPS, BLK, BLK)), _full((BLK, GMLP_W)), _full((8, ATTN_W)), _full((8, GMLP_W))]
    in_specs = _mix_in_specs(nb) + [_full((N_GROUPS, BLK, BLK)),
                                    pl.BlockSpec((BLK, D_MODEL), lambda n: (n, 0)),
                                    _full((D_MODEL, D_MODEL))]
    return _call(
        body, name="mix_bwd", grid=(nb,), out_shape=out_shape, in_specs=in_specs, out_specs=out_specs,
        sem=("arbitrary",), comm=comm, args=(proj, proj, bias, sinks, lng, lnb, ws, bfull, aog, gog, ws_t, dy, w_out))


HALF = 64
ROWS = 32


def _lane_lo(rows):
    return lax.broadcasted_iota(jnp.int32, (rows, 128), 1) < 64


def _mix_stage_kv(proj_ref, kvp_ref, s):
    lo = _lane_lo(2 * BLK)
    for name, col in (("k", ATTN_W), ("v", ATTN_W + KV_W)):
        cur = jnp.concatenate([kvp_ref[:, col - ATTN_W:col - ATTN_W + KV_W], proj_ref[:, col:col + KV_W]], axis=0)
        plain, swapped = cur.astype(BF16), pltpu.roll(cur, 64, 1).astype(BF16)
        zero = jnp.zeros_like(plain)
        for g in range(2):
            dup = jnp.where(lo, plain, swapped) if g == 0 else jnp.where(lo, swapped, plain)
            s[name + "d"][g] = dup
            s[name + "m"][g] = jnp.concatenate([jnp.where(lo, dup, zero), jnp.where(lo, zero, dup)], axis=0)


def _group_rows(ref, g):
    return ref[4 * g:4 * g + 4].reshape(4 * BLK, ref.shape[2])


def _pair_rows(ref, g):
    return jnp.concatenate([jnp.concatenate([ref[4 * g + 2 * c], ref[4 * g + 2 * c + 1]], axis=1) for c in range(2)],
                           axis=0)


def _mask_heads(src_ref, dst_ref):
    lo = _lane_lo(BLK)
    for j in range(4):
        chunk = src_ref[:, 128 * j:128 * (j + 1)]
        dst_ref[2 * j] = jnp.where(lo, chunk, 0.0).astype(BF16)
        dst_ref[2 * j + 1] = jnp.where(lo, 0.0, chunk).astype(BF16)


def _mix_stage_attn(proj_ref, bias_ref, sinks_ref, n, s, keep):
    _mask_heads(proj_ref, s["qm"])
    for g in range(2):
        s["lg"][g] = _dot_nt(_group_rows(s["qm"], g), s["kd"][g])
    n0mask = (n == 0) & (lax.broadcasted_iota(jnp.int32, (HALF, 2 * BLK), 1) < BLK)
    for h in range(N_HEADS):
        sk = sinks_ref[h]
        for hf in range(BLK // HALF):
            rows = slice(HALF * hf, HALF * (hf + 1))
            grows = slice(BLK * (h % 4) + HALF * hf, BLK * (h % 4) + HALF * (hf + 1))
            logit = s["lg"][h // 4, grows, :] * (HEAD_DIM ** -0.5) + bias_ref[h, rows, :]
            logit = jnp.where(n0mask, NEG_INF, logit)
            m = jnp.maximum(jnp.max(logit, axis=1, keepdims=True), sk)
            e = jnp.exp(logit - m)
            es = jnp.exp(sk - m)
            inv = 1.0 / (jnp.sum(e, axis=1, keepdims=True) + es)
            p = e * inv
            s["pb"][h, rows, :] = p.astype(BF16)
            if keep:
                s["p"][h, rows, :] = p
                s["psink"][h, rows, :] = es * inv
    for g in range(2):
        out = _dot(_pair_rows(s["pb"], g), s["vm"][g])
        s["attn"][:, 256 * g:256 * g + 128] = out[0:BLK]
        s["attn"][:, 256 * g + 128:256 * g + 256] = out[BLK:2 * BLK]


def _mix_stage_gmlp_pre(proj_ref, lng, lnb, s, keep):
    c0 = ATTN_W + 2 * KV_W
    for r0 in range(0, BLK, ROWS):
        rows = slice(r0, r0 + ROWS)
        u, du = _gelu_parts(proj_ref[rows, c0:c0 + GMLP_W])
        a, da = _gelu_parts(proj_ref[rows, c0 + GMLP_W:c0 + 2 * GMLP_W])
        ac = a - _seg_mean64(a)
        rstd = lax.rsqrt(_seg_mean64(ac * ac) + LN_EPS)
        vhat = ac * rstd
        s["u"][rows, :] = u
        s["vnb"][rows, :] = (vhat * lng + lnb).astype(BF16)
        if keep:
            s["du"][rows, :] = du
            s["da"][rows, :] = da
            s["vhat"][rows, :] = vhat
            s["rstd"][rows, :] = rstd


def _stack_halves(chunk):
    lo = _lane_lo(BLK)
    zero = jnp.zeros_like(chunk)
    return jnp.concatenate([jnp.where(lo, chunk, zero), jnp.where(lo, zero, chunk)], axis=0)


def _mix_stage_gmlp_mix(ws2_ref, bfull_ref, s):
    for j in range(4):
        cols = slice(128 * j, 128 * (j + 1))
        s["ms"][:, cols] = _dot(ws2_ref[j], _stack_halves(s["vnb"][:, cols])) + bfull_ref[:, cols]


def _mix_scratch(keep):
    f32 = lambda *shape: pltpu.VMEM(shape, F32)
    b16 = lambda *shape: pltpu.VMEM(shape, BF16)
    names = dict(kd=b16(2, 2 * BLK, 128), vd=b16(2, 2 * BLK, 128), km=b16(2, 4 * BLK, 128), vm=b16(2, 4 * BLK, 128),
                 qm=b16(N_HEADS, BLK, 128), lg=f32(2, 4 * BLK, 2 * BLK), pb=b16(N_HEADS, BLK, 2 * BLK),
                 attn=f32(BLK, ATTN_W), u=f32(BLK, GMLP_W), vnb=b16(BLK, GMLP_W), ms=f32(BLK, GMLP_W))
    if keep:
        names.update(dom=b16(N_HEADS, BLK, 128), p=f32(N_HEADS, BLK, 2 * BLK),
                     dls=b16(N_HEADS, BLK, 2 * BLK), psink=f32(N_HEADS, BLK, 1),
                     dattn=f32(BLK, ATTN_W), dmix=f32(BLK, D_MODEL), du=f32(BLK, GMLP_W), da=f32(BLK, GMLP_W),
                     vhat=f32(BLK, GMLP_W), rstd=f32(BLK, GMLP_W), dmsb=b16(BLK, GMLP_W), dvn=f32(BLK, GMLP_W))
    return list(names), list(names.values())


def _mix_specs():
    return [pl.BlockSpec((BLK, IN_W), lambda n: (n, 0)),
            pl.BlockSpec((BLK, 2 * KV_W), lambda n: (jnp.maximum(n - 1, 0), ATTN_W // (2 * KV_W))),
            _full((N_HEADS, BLK, 2 * BLK)),
            pl.BlockSpec(memory_space=pltpu.SMEM),
            _full((1, GMLP_W)), _full((1, GMLP_W)),
            _full((N_GROUPS // 2, BLK, 2 * BLK)), _full((BLK, GMLP_W)),
            _full((1, ATTN_W)), _full((1, GMLP_W))]


def _mix_fwd(proj, bias, sinks, lng, lnb, ws2, bfull, aog, gog, comm):
    t = proj.shape[0]
    names, shapes = _mix_scratch(False)

    def body(proj_ref, kvp_ref, bias_ref, sinks_ref, lng_ref, lnb_ref, ws2_ref, bfull_ref, aog_ref, gog_ref,
             out_ref, *scratch):
        s = dict(zip(names, scratch))
        n = pl.program_id(0)
        _mix_stage_kv(proj_ref, kvp_ref, s)
        _mix_stage_attn(proj_ref, bias_ref, sinks_ref, n, s, False)
        _mix_stage_gmlp_pre(proj_ref, lng_ref[...], lnb_ref[...], s, False)
        _mix_stage_gmlp_mix(ws2_ref, bfull_ref, s)
        for r0 in range(0, BLK, ROWS):
            rows = slice(r0, r0 + ROWS)
            out_ref[rows, 0:ATTN_W] = _rms(s["attn"][rows, :], aog_ref[...])[0].astype(BF16)
            out_ref[rows, ATTN_W:ATTN_W + GMLP_W] = _rms(s["u"][rows, :] * s["ms"][rows, :], gog_ref[...])[0].astype(BF16)

    return _call(
        body, name="mix_fwd", grid=(t // BLK,), out_shape=[jax.ShapeDtypeStruct((t, D_MODEL), BF16)],
        in_specs=_mix_specs(), out_specs=[pl.BlockSpec((BLK, D_MODEL), lambda n: (n, 0))], scratch_shapes=shapes,
        sem=("parallel",), comm=comm, args=(proj, proj, bias, sinks, lng, lnb, ws2, bfull, aog, gog))


def _mix_bwd(proj, bias, sinks, lng, lnb, ws2, wst2, bfull, aog, gog, dy, w_out, comm):
    t = proj.shape[0]
    nb = t // BLK
    names, shapes = _mix_scratch(True)
    c_gu = ATTN_W + 2 * KV_W

    def body(proj_ref, kvp_ref, bias_ref, sinks_ref, lng_ref, lnb_ref, ws2_ref, bfull_ref, aog_ref, gog_ref,
             wst2_ref, dy_ref, wout_ref,
             dproj_ref, dkvn_ref, dl_ref, dsink_ref, dlng_ref, dlnb_ref, dws_ref, dbs_ref, daog_ref, dgog_ref,
             *scratch):
        s = dict(zip(names, scratch))
        n = pl.program_id(0)

        @pl.when(n == 0)
        def _():
            for r in (dl_ref, dsink_ref, dlng_ref, dlnb_ref, dws_ref, dbs_ref, daog_ref, dgog_ref):
                r[...] = jnp.zeros_like(r)

        s["dmix"][...] = _dot_nt(dy_ref[...], wout_ref[...])
        _mix_stage_kv(proj_ref, kvp_ref, s)
        _mix_stage_attn(proj_ref, bias_ref, sinks_ref, n, s, True)
        lng = lng_ref[...]
        _mix_stage_gmlp_pre(proj_ref, lng, lnb_ref[...], s, True)
        _mix_stage_gmlp_mix(ws2_ref, bfull_ref, s)

        aog, gog = aog_ref[...], gog_ref[...]
        for r0 in range(0, BLK, ROWS):
            rows = slice(r0, r0 + ROWS)
            attn, dma = s["attn"][rows, :], s["dmix"][rows, 0:ATTN_W]
            _, r_a = _rms(attn, aog)
            daog_ref[...] += _rowsum8(dma * attn * r_a)
            s["dattn"][rows, :] = _rms_bwd(dma, attn, r_a, aog)
            u, ms, dmg = s["u"][rows, :], s["ms"][rows, :], s["dmix"][rows, ATTN_W:ATTN_W + GMLP_W]
            gm = u * ms
            _, r_g = _rms(gm, gog)
            dgog_ref[...] += _rowsum8(dmg * gm * r_g)
            dgm = _rms_bwd(dmg, gm, r_g, gog)
            dproj_ref[rows, c_gu:c_gu + GMLP_W] = dgm * ms * s["du"][rows, :]
            dms = dgm * u
            dbs_ref[rows, :] += dms
            s["dmsb"][rows, :] = dms.astype(BF16)

        _mask_heads(s["dattn"], s["dom"])
        for g in range(2):
            s["lg"][g] = _dot_nt(_group_rows(s["dom"], g), s["vd"][g])
        lane = lax.broadcasted_iota(jnp.int32, (HALF, 128), 1)
        for hf in range(BLK // HALF):
            rows = slice(HALF * hf, HALF * (hf + 1))
            dsink = jnp.zeros((HALF, 128), F32)
            for h in range(N_HEADS):
                grows = slice(BLK * (h % 4) + HALF * hf, BLK * (h % 4) + HALF * (hf + 1))
                dp = s["lg"][h // 4, grows, :]
                p = s["p"][h, rows, :]
                rs = jnp.sum(p * dp, axis=1, keepdims=True)
                dl = p * (dp - rs)
                dl_ref[h, rows, :] += dl
                dsink = dsink + jnp.where(lane == h, -s["psink"][h, rows, :] * rs, 0.0)
                s["dls"][h, rows, :] = (dl * (HEAD_DIM ** -0.5)).astype(BF16)
            dsink_ref[rows, :] += dsink
        for g in range(2):
            dq = _dot(_pair_rows(s["dls"], g), s["km"][g])
            dproj_ref[:, 256 * g:256 * g + 128] = dq[0:BLK]
            dproj_ref[:, 256 * g + 128:256 * g + 256] = dq[BLK:2 * BLK]
        lo_k = _lane_lo(2 * BLK)
        for col, lhs, rhs in ((0, "dls", "qm"), (KV_W, "pb", "dom")):
            raw = [_dot_tn(_group_rows(s[lhs], g), _group_rows(s[rhs], g)) for g in range(2)]
            both = [r + pltpu.roll(r, 64, 1) for r in raw]
            dkv = jnp.where(lo_k, both[0], both[1])
            dproj_ref[:, ATTN_W + col:ATTN_W + col + KV_W] = dkv[BLK:2 * BLK]
            dkvn_ref[:, col:col + KV_W] = dkv[0:BLK]

        for j in range(4):
            cols = slice(128 * j, 128 * (j + 1))
            dm2 = _stack_halves(s["dmsb"][:, cols])
            vnb = s["vnb"][:, cols]
            dws2 = _dot_nt(dm2, vnb)
            dws_ref[2 * j] += dws2[0:BLK]
            dws_ref[2 * j + 1] += dws2[BLK:2 * BLK]
            s["dvn"][:, cols] = _dot(wst2_ref[j], dm2)
        for r0 in range(0, BLK, ROWS):
            rows = slice(r0, r0 + ROWS)
            dvn, vhat = s["dvn"][rows, :], s["vhat"][rows, :]
            dlng_ref[...] += _rowsum8(dvn * vhat)
            dlnb_ref[...] += _rowsum8(dvn)
            dvh = dvn * lng
            dact = s["rstd"][rows, :] * (dvh - _seg_mean64(dvh) - vhat * _seg_mean64(dvh * vhat))
            dproj_ref[rows, c_gu + GMLP_W:IN_W] = dact * s["da"][rows, :]

    acc8 = lambda w: jax.ShapeDtypeStruct((8, w), F32)
    out_shape = [jax.ShapeDtypeStruct((t, IN_W), F32), jax.ShapeDtypeStruct((t, 2 * KV_W), F32),
                 jax.ShapeDtypeStruct((N_HEADS, BLK, 2 * BLK), F32), jax.ShapeDtypeStruct((BLK, 128), F32),
                 acc8(GMLP_W), acc8(GMLP_W), jax.ShapeDtypeStruct((N_GROUPS, BLK, BLK), F32),
                 jax.ShapeDtypeStruct((BLK, GMLP_W), F32), acc8(ATTN_W), acc8(GMLP_W)]
    out_specs = [pl.BlockSpec((BLK, IN_W), lambda n: (n, 0)),
                 pl.BlockSpec((BLK, 2 * KV_W), lambda n: ((n + nb - 1) % nb, 0)),
                 _full((N_HEADS, BLK, 2 * BLK)), _full((BLK, 128)), _full((8, GMLP_W)), _full((8, GMLP_W)),
                 _full((N_GROUPS, BLK, BLK)), _full((BLK, GMLP_W)), _full((8, ATTN_W)), _full((8, GMLP_W))]
    in_specs = _mix_specs() + [_full((N_GROUPS // 2, BLK, 2 * BLK)),
                               pl.BlockSpec((BLK, D_MODEL), lambda n: (n, 0)),
                               _full((D_MODEL, D_MODEL))]
    return _call(
        body, name="mix_bwd", grid=(nb,), out_shape=out_shape, in_specs=in_specs, out_specs=out_specs,
        scratch_shapes=shapes, sem=("arbitrary",), comm=comm,
        args=(proj, proj, bias, sinks, lng, lnb, ws2, bfull, aog, gog, wst2, dy, w_out))


def _outproj(mixed, w_out, x, g1, ln1g, ln1b, sc2, sh2, tm):
    t, d = x.shape

    def body(mx_ref, w_ref, x_ref, g1_ref, lg_ref, lb_ref, sc_ref, sh_ref, y_ref, x1_ref, h2_ref):
        y = _dot(mx_ref[...], w_ref[...])
        xhat, _ = _ln_stats(ALPHA * x_ref[...] + g1_ref[...] * y)
        x1 = xhat * lg_ref[...] + lb_ref[...]
        y_ref[...] = y
        x1_ref[...] = x1
        h2_ref[...] = (x1 * (1.0 + sc_ref[...]) + sh_ref[...]).astype(BF16)

    row = pl.BlockSpec((tm, d), lambda i: (i, 0))
    vec = _full((1, d))
    return pl.pallas_call(
        body, name="outproj", grid=(t // tm,),
        out_shape=[jax.ShapeDtypeStruct((t, d), F32), jax.ShapeDtypeStruct((t, d), F32),
                   jax.ShapeDtypeStruct((t, d), BF16)],
        in_specs=[row, _full((d, d)), row, vec, vec, vec, vec, vec], out_specs=[row, row, row],
        compiler_params=_params(("parallel",)),
    )(mixed, w_out, x, g1, ln1g, ln1b, sc2, sh2)


def _ffn_up(h2, w_gu_t, tm, tn, comm):
    t, d = h2.shape
    nff = D_FF // tn

    def body(h_ref, wg_ref, wu_ref, dsu_ref, sg_ref, act_ref):
        h = h_ref[...]
        g = _dot_nt(h, wg_ref[...])
        u = _dot_nt(h, wu_ref[...])
        s = _sigmoid(g)
        sg = g * s
        dsu_ref[...] = (u * (s * (1.0 + g * (1.0 - s)))).astype(BF16)
        sg_ref[...] = sg.astype(BF16)
        act_ref[...] = (sg * u).astype(BF16)

    out = pl.BlockSpec((tm, tn), lambda j, i: (i, j))
    shp = jax.ShapeDtypeStruct((t, D_FF), BF16)
    return _call(
        body, name="ffn_up", grid=(nff, t // tm), out_shape=[shp, shp, shp],
        in_specs=[pl.BlockSpec((tm, d), lambda j, i: (i, 0)),
                  pl.BlockSpec((tn, d), lambda j, i: (j, 0)),
                  pl.BlockSpec((tn, d), lambda j, i: (j + nff, 0))],
        out_specs=[out, out, out], sem=("parallel", "parallel"), comm=comm, args=(h2, w_gu_t, w_gu_t))


def _ffn_down(act, w_down, x1, target, g2, ln2g, ln2b, tm):
    t, d = x1.shape

    def body(act_ref, w_ref, x1_ref, tg_ref, g2_ref, lg_ref, lb_ref,
             dz_ref, dy_ref, loss_ref, dlg_ref, dlb_ref, dg2_ref):
        @pl.when(pl.program_id(0) == 0)
        def _():
            for r in (loss_ref, dlg_ref, dlb_ref, dg2_ref):
                r[...] = jnp.zeros_like(r)

        g2 = g2_ref[...]
        lg = lg_ref[...]
        for rows in (slice(0, tm // 2), slice(tm // 2, tm)):
            y2 = _dot(act_ref[rows, :], w_ref[...])
            xhat, rstd = _ln_stats(ALPHA * x1_ref[rows, :] + g2 * y2)
            err = xhat * lg + lb_ref[...] - tg_ref[rows, :]
            loss_ref[...] += _rowsum8(err * err)
            dx2 = err * (1.0 / d)
            dlg_ref[...] += _rowsum8(dx2 * xhat)
            dlb_ref[...] += _rowsum8(dx2)
            dz = _ln_bwd(dx2 * lg, xhat, rstd)
            dg2_ref[...] += _rowsum8(dz * y2)
            dz_ref[rows, :] = dz
            dy_ref[rows, :] = (g2 * dz).astype(BF16)

    row = pl.BlockSpec((tm, d), lambda i: (i, 0))
    vec = _full((1, d))
    acc = _full((8, d))
    acc_shape = jax.ShapeDtypeStruct((8, d), F32)
    return pl.pallas_call(
        body, name="ffn_down", grid=(t // tm,),
        out_shape=[jax.ShapeDtypeStruct((t, d), F32), jax.ShapeDtypeStruct((t, d), BF16)] + [acc_shape] * 4,
        in_specs=[pl.BlockSpec((tm, D_FF), lambda i: (i, 0)), _full((D_FF, d)), row, row, vec, vec, vec],
        out_specs=[row, row, acc, acc, acc, acc], compiler_params=_params(("arbitrary",)),
    )(act, w_down, x1, target, g2, ln2g, ln2b)


def _resident(shape):
    nd = len(shape)
    return pl.BlockSpec(shape, lambda *_: (0,) * nd, pipeline_mode=pl.Buffered(1))


def _ffn_bwd(dy2, w_down, dsu, sg, w_gu_t, x1, x, y, dz2, sc2, g1, ln1g, tm, comm):
    t, d = x1.shape

    def body(dy2_ref, wd_ref, dsu_ref, sg_ref, w_ref, x1_ref, x_ref, y_ref, dz2_ref, sc_ref, g1_ref, lg_ref,
             dg_ref, du_ref, dz1_ref, dy_ref, dsc_ref, dsh_ref, dlg_ref, dlb_ref, dg1_ref):
        @pl.when(pl.program_id(0) == 0)
        def _():
            for r in (dsc_ref, dsh_ref, dlg_ref, dlb_ref, dg1_ref):
                r[...] = jnp.zeros_like(r)

        dact = _dot_nt(dy2_ref[...], wd_ref[...])
        dg = (dact * dsu_ref[...].astype(F32)).astype(BF16)
        du = (dact * sg_ref[...].astype(F32)).astype(BF16)
        dg_ref[...] = dg
        du_ref[...] = du
        dh2 = _dot(dg, w_ref[0:D_FF]) + _dot(du, w_ref[D_FF:2 * D_FF])
        x1 = x1_ref[...]
        y = y_ref[...]
        g1 = g1_ref[...]
        dsc_ref[...] += _rowsum8(dh2 * x1)
        dsh_ref[...] += _rowsum8(dh2)
        dx1 = dh2 * (1.0 + sc_ref[...]) + ALPHA * dz2_ref[...]
        xhat, rstd = _ln_stats(ALPHA * x_ref[...] + g1 * y)
        dlg_ref[...] += _rowsum8(dx1 * xhat)
        dlb_ref[...] += _rowsum8(dx1)
        dz1 = _ln_bwd(dx1 * lg_ref[...], xhat, rstd)
        dg1_ref[...] += _rowsum8(dz1 * y)
        dz1_ref[...] = dz1
        dy_ref[...] = (g1 * dz1).astype(BF16)

    row = pl.BlockSpec((tm, d), lambda i: (i, 0))
    wide = pl.BlockSpec((tm, D_FF), lambda i: (i, 0))
    vec = _full((1, d))
    acc = _full((8, d))
    acc_shape = jax.ShapeDtypeStruct((8, d), F32)
    wide_shape = jax.ShapeDtypeStruct((t, D_FF), BF16)
    return _call(
        body, name="ffn_bwd", grid=(t // tm,),
        out_shape=[wide_shape, wide_shape, jax.ShapeDtypeStruct((t, d), F32), jax.ShapeDtypeStruct((t, d), BF16)]
        + [acc_shape] * 5,
        in_specs=[row, _resident((D_FF, d)), wide, wide, _resident((2 * D_FF, d)), row, row, row, row, vec, vec, vec],
        out_specs=[wide, wide, row, row, acc, acc, acc, acc, acc], sem=("arbitrary",), comm=comm,
        args=(dy2, w_down, dsu, sg, w_gu_t, x1, x, y, dz2, sc2, g1, ln1g))


def _din(dproj, dkvn, w_in_t, x, dz1, sc1, tm, comm):
    t, d = x.shape

    def body(dp_ref, dkv_ref, w_ref, x_ref, dz1_ref, sc_ref, dx_ref, dpb_ref, dbin_ref, dsc_ref, dsh_ref):
        @pl.when(pl.program_id(0) == 0)
        def _():
            for r in (dbin_ref, dsc_ref, dsh_ref):
                r[...] = jnp.zeros_like(r)

        dp = jnp.concatenate([dp_ref[:, 0:ATTN_W], dp_ref[:, ATTN_W:ATTN_W + 2 * KV_W] + dkv_ref[...],
                              dp_ref[:, ATTN_W + 2 * KV_W:IN_W]], axis=1)
        dbin_ref[...] += _rowsum8(dp)
        dpb = dp.astype(BF16)
        dpb_ref[...] = dpb
        dh = _dot(dpb, w_ref[...])
        dsc_ref[...] += _rowsum8(dh * x_ref[...])
        dsh_ref[...] += _rowsum8(dh)
        dx_ref[...] = dh * (1.0 + sc_ref[...]) + ALPHA * dz1_ref[...]

    row = lambda w: pl.BlockSpec((tm, w), lambda i: (i, 0))
    return _call(
        body, name="din", grid=(t // tm,),
        out_shape=[jax.ShapeDtypeStruct((t, d), F32), jax.ShapeDtypeStruct((t, IN_W), BF16),
                   jax.ShapeDtypeStruct((8, IN_W), F32), jax.ShapeDtypeStruct((8, d), F32),
                   jax.ShapeDtypeStruct((8, d), F32)],
        in_specs=[row(IN_W), row(2 * KV_W), _full((IN_W, d)), row(d), row(d), _full((1, d))],
        out_specs=[row(d), row(IN_W), _full((8, IN_W)), _full((8, d)), _full((8, d))],
        sem=("arbitrary",), comm=comm, args=(dproj, dkvn, w_in_t, x, dz1, sc1))


def _wgrad(name, a, b, tmm, tk, comm=None, a2=None):
    t, m = a.shape
    n = b.shape[1]
    nk = t // tk
    nm = m // tmm

    def body(*refs):
        a_refs, (b_ref, o_ref, acc_ref) = refs[:-3], refs[-3:]
        i, k = pl.program_id(0), pl.program_id(1)
        a_tile = a_refs[0][...] if a2 is None else jnp.where(i < nm, a_refs[0][...], a_refs[1][...])
        part = _dot_tn(a_tile, b_ref[...])

        @pl.when(k == 0)
        def _():
            acc_ref[...] = part

        @pl.when(k > 0)
        def _():
            acc_ref[...] += part

        @pl.when(k == nk - 1)
        def _():
            o_ref[...] = acc_ref[...].astype(BF16)

    if a2 is None:
        a_specs, a_args, n_tiles = [pl.BlockSpec((tk, tmm), lambda i, k: (k, i))], (a,), nm
    else:
        a_specs = [pl.BlockSpec((tk, tmm), lambda i, k: (jnp.where(i < nm, k, 0), jnp.minimum(i, nm - 1))),
                   pl.BlockSpec((tk, tmm), lambda i, k: (jnp.where(i < nm, 0, k), jnp.maximum(i - nm, 0)))]
        a_args, n_tiles = (a, a2), 2 * nm
    (out,), got = _call(
        body, name=name, grid=(n_tiles, nk), out_shape=[jax.ShapeDtypeStruct((n_tiles * tmm, n), BF16)],
        in_specs=a_specs + [pl.BlockSpec((tk, n), lambda i, k: (k, 0))],
        out_specs=[pl.BlockSpec((tmm, n), lambda i, k: (i, 0))],
        scratch_shapes=[pltpu.VMEM((tmm, n), F32)], sem=("parallel", "arbitrary"), comm=comm, args=a_args + (b,))
    return out if comm is None else (out, got)


def _adamw(w, g, m, v):
    m = ADAM_B1 * m + (1.0 - ADAM_B1) * g
    v = ADAM_B2 * v + (1.0 - ADAM_B2) * (g * g)
    m_hat = m / (1.0 - ADAM_B1 ** ADAM_STEP)
    v_hat = v / (1.0 - ADAM_B2 ** ADAM_STEP)
    delta = -ADAM_LR * (m_hat / (jnp.sqrt(v_hat) + ADAM_EPS) + ADAM_WD * w)
    return delta, m, v


def _adam_reduce(name, parts, w, m, v, tr):
    r, cdim = w.shape

    def body(p_ref, w_ref, m_ref, v_ref, g_ref, d_ref, mo_ref, vo_ref):
        g = p_ref[0].astype(F32)
        for s in range(1, N_DEV):
            g = g + p_ref[s].astype(F32)
        d_ref[...], mo_ref[...], vo_ref[...] = _adamw(w_ref[...], g, m_ref[...], v_ref[...])
        g_ref[...] = g

    tile = pl.BlockSpec((tr, cdim), lambda i: (i, 0))
    shp = jax.ShapeDtypeStruct((r, cdim), F32)
    return pl.pallas_call(
        body, name=name, grid=(r // tr,), out_shape=[shp] * 4,
        in_specs=[pl.BlockSpec((N_DEV, tr, cdim), lambda i: (0, i, 0)), tile, tile, tile],
        out_specs=[tile] * 4, compiler_params=_params(("parallel",)),
    )(parts, w, m, v)


def _adam_w_ada(c_all_t, dmod_cols, w, m, v):
    def body(ct_ref, dm_ref, w_ref, m_ref, v_ref, g_ref, d_ref, mo_ref, vo_ref):
        ct = ct_ref[...]
        s = (ct * _sigmoid(ct)).astype(BF16)
        g = _dot(s, dm_ref[...].astype(BF16))
        d_ref[...], mo_ref[...], vo_ref[...] = _adamw(w_ref[...], g, m_ref[...], v_ref[...])
        g_ref[...] = g

    shp = jax.ShapeDtypeStruct(w.shape, F32)
    return pl.pallas_call(
        body, name="adam_w_ada", grid=(1,), out_shape=[shp] * 4,
        in_specs=[_full(c_all_t.shape), _full(dmod_cols.shape)] + [_full(w.shape)] * 3,
        out_specs=[_full(w.shape)] * 4, compiler_params=_params(("arbitrary",)),
    )(c_all_t, dmod_cols, w, m, v)


SMALL_EARLY = ["rel_bias", "attn_sinks", "gmlp_ln_g", "gmlp_ln_b", "gmlp_w_s", "gmlp_b_s",
               "attn_out_g", "gmlp_out_g", "ln1_g", "ln1_b", "ln2_g", "ln2_b"]
SMALL_LATE = ["b_ada", "b_in"]
WEIGHTS = ["rel_bias", "w_ada", "b_ada", "w_in", "b_in", "attn_sinks", "gmlp_ln_g", "gmlp_ln_b", "gmlp_w_s",
           "gmlp_b_s", "attn_out_g", "gmlp_out_g", "w_out", "ln1_g", "ln1_b", "w_gate_up", "w_down", "ln2_g", "ln2_b"]


def _seg_rows(nelem):
    return -(-nelem // 1024) * 8


def _pack(named, names):
    parts = []
    for name in names:
        flat = named[name].reshape(-1).astype(F32)
        rows = _seg_rows(flat.shape[0])
        parts.append(jnp.pad(flat, (0, rows * 128 - flat.shape[0])).reshape(rows, 128))
    return jnp.concatenate(parts, axis=0)


def _unpack(packed, shapes, names):
    out, r0 = {}, 0
    for name in names:
        nelem = math.prod(shapes[name])
        rows = _seg_rows(nelem)
        out[name] = packed[r0:r0 + rows].reshape(-1)[:nelem].reshape(shapes[name])
        r0 += rows
    return out


def _t5_bucket_map():
    qi = jnp.arange(BLK)[:, None]
    si = jnp.arange(2 * BLK)[None, :]
    n = jnp.maximum(qi + BLK - si, 0)
    max_exact = N_BUCKETS // 2
    nf = jnp.maximum(n, max_exact).astype(F32)
    large = max_exact + (jnp.log(nf / max_exact) / math.log(MAX_DISTANCE / max_exact)
                         * (N_BUCKETS - max_exact)).astype(jnp.int32)
    large = jnp.minimum(large, N_BUCKETS - 1)
    return jnp.where(n < max_exact, n, large).astype(jnp.int32)


def kernel(x, c, rel_bias, w_ada, b_ada, w_in, b_in, attn_sinks, gmlp_ln_g, gmlp_ln_b, gmlp_w_s, gmlp_b_s, attn_out_g, gmlp_out_g, w_out, ln1_g, ln1_b, w_gate_up, w_down, ln2_g, ln2_b, loss_target, m_rel_bias, m_w_ada, m_b_ada, m_w_in, m_b_in, m_attn_sinks, m_gmlp_ln_g, m_gmlp_ln_b, m_gmlp_w_s, m_gmlp_b_s, m_attn_out_g, m_gmlp_out_g, m_w_out, m_ln1_g, m_ln1_b, m_w_gate_up, m_w_down, m_ln2_g, m_ln2_b, v_rel_bias, v_w_ada, v_b_ada, v_w_in, v_b_in, v_attn_sinks, v_gmlp_ln_g, v_gmlp_ln_b, v_gmlp_w_s, v_gmlp_b_s, v_attn_out_g, v_gmlp_out_g, v_w_out, v_ln1_g, v_ln1_b, v_w_gate_up, v_w_down, v_ln2_g, v_ln2_b):
    wts = dict(rel_bias=rel_bias, w_ada=w_ada, b_ada=b_ada, w_in=w_in, b_in=b_in, attn_sinks=attn_sinks,
               gmlp_ln_g=gmlp_ln_g, gmlp_ln_b=gmlp_ln_b, gmlp_w_s=gmlp_w_s, gmlp_b_s=gmlp_b_s,
               attn_out_g=attn_out_g, gmlp_out_g=gmlp_out_g, w_out=w_out, ln1_g=ln1_g, ln1_b=ln1_b,
               w_gate_up=w_gate_up, w_down=w_down, ln2_g=ln2_g, ln2_b=ln2_b)
    mom_m = dict(rel_bias=m_rel_bias, w_ada=m_w_ada, b_ada=m_b_ada, w_in=m_w_in, b_in=m_b_in,
                 attn_sinks=m_attn_sinks, gmlp_ln_g=m_gmlp_ln_g, gmlp_ln_b=m_gmlp_ln_b, gmlp_w_s=m_gmlp_w_s,
                 gmlp_b_s=m_gmlp_b_s, attn_out_g=m_attn_out_g, gmlp_out_g=m_gmlp_out_g, w_out=m_w_out,
                 ln1_g=m_ln1_g, ln1_b=m_ln1_b, w_gate_up=m_w_gate_up, w_down=m_w_down, ln2_g=m_ln2_g,
                 ln2_b=m_ln2_b)
    mom_v = dict(rel_bias=v_rel_bias, w_ada=v_w_ada, b_ada=v_b_ada, w_in=v_w_in, b_in=v_b_in,
                 attn_sinks=v_attn_sinks, gmlp_ln_g=v_gmlp_ln_g, gmlp_ln_b=v_gmlp_ln_b, gmlp_w_s=v_gmlp_w_s,
                 gmlp_b_s=v_gmlp_b_s, attn_out_g=v_attn_out_g, gmlp_out_g=v_gmlp_out_g, w_out=v_w_out,
                 ln1_g=v_ln1_g, ln1_b=v_ln1_b, w_gate_up=v_w_gate_up, w_down=v_w_down, ln2_g=v_ln2_g,
                 ln2_b=v_ln2_b)

    t = x.shape[1]
    tm = min(512, t)
    tn_ff = D_FF // 2
    tk_tok = min(1024, t)
    me = 4 * lax.axis_index("x") + 2 * lax.axis_index("y") + lax.axis_index("c")
    xs = x[0]
    target = loss_target[0]

    c_g, w_in_g = _exchange("gather_in", [jnp.broadcast_to(c, (8, D_MODEL)), w_in[0].T.astype(BF16)],
                            ("gather", "gather2"))
    c_all = c_g[:, 0, :]
    w_in_t = w_in_g.reshape(IN_W, D_MODEL)

    ncol = w_ada.shape[2]
    b_cols = lax.dynamic_slice(b_ada, (0, me * ncol), (1, ncol))
    mod_part = _mod_partial(c_all, w_ada[0], b_cols)
    (mod_g,) = _exchange("gather_mod", [mod_part], ("gather",))
    mod = lax.dynamic_slice(mod_g, (0, me, 0), (N_DEV, 1, ncol)).reshape(1, N_DEV * ncol)
    sh1, sc1, g1, sh2, sc2, g2 = [mod[:, i * D_MODEL:(i + 1) * D_MODEL] for i in range(6)]

    bucket = _t5_bucket_map()
    bias = _bias_table(rel_bias, bucket)
    causal = jnp.tril(jnp.ones((BLK, BLK), dtype=bool))
    ws = jnp.where(causal[None], gmlp_w_s[0], 0.0).astype(BF16)
    pair = lambda w: jnp.concatenate([w[0::2], w[1::2]], axis=2)
    ws2, wst2 = pair(ws), pair(jnp.swapaxes(ws, 1, 2))
    bfull = jnp.repeat(gmlp_b_s[0].T, GMLP_W // N_GROUPS, axis=1)
    sinks = attn_sinks[0]

    proj, h1 = _inproj(xs, sc1, sh1, w_in_t, b_in, tm)
    (mixed,), (w_out_g, w_gu_g) = _mix_fwd(
        proj, bias, sinks, gmlp_ln_g, gmlp_ln_b, ws2, bfull, attn_out_g, gmlp_out_g,
        comm=([w_out[0].astype(BF16), w_gate_up[0].T.astype(BF16)], ("gather2", "gather2")))
    w_out_f = w_out_g.reshape(D_MODEL, D_MODEL)
    w_gu_t = w_gu_g.reshape(2 * D_FF, D_MODEL)
    y1, x1, h2 = _outproj(mixed, w_out_f, xs, g1, ln1_g, ln1_b, sc2, sh2, tm)
    (dsu, sg, act), (w_down_g,) = _ffn_up(h2, w_gu_t, tm, tn_ff, comm=([w_down[0].astype(BF16)], ("gather2",)))
    w_down_f = w_down_g.reshape(D_FF, D_MODEL)
    dz2, dy2, loss_p, d_ln2g, d_ln2b, d_g2 = _ffn_down(act, w_down_f, x1, target, g2, ln2_g, ln2_b, tm)
    loss = lax.psum(0.5 / D_MODEL * jnp.sum(loss_p), ("x", "y", "c"))

    slots = lambda a: a.reshape(N_DEV, -1, D_MODEL)
    dw_down = _wgrad("wgrad_down", act, dy2, tn_ff, tk_tok)
    (dgate, dup, dz1, dy1, d_sc2, d_sh2, d_ln1g, d_ln1b, d_g1), (r_down,) = _ffn_bwd(
        dy2, w_down_f, dsu, sg, w_gu_t, x1, xs, y1, dz2, sc2, g1, ln1_g, min(256, t),
        comm=([slots(dw_down)], ("scatter",)))
    dw_gu_t = _wgrad("wgrad_gate_up", dgate, h2, tn_ff, tk_tok, a2=dup)
    dw_out = _wgrad("wgrad_out", mixed, dy1, D_MODEL, tk_tok)
    ((dproj, dkvn, dl_acc, dsink_acc, d_lng, d_lnb, d_ws, d_bs, d_aog, d_gog), (r_gu, r_out)) = _mix_bwd(
        proj, bias, sinks, gmlp_ln_g, gmlp_ln_b, ws2, wst2, bfull, attn_out_g, gmlp_out_g, dy1, w_out_f,
        comm=([slots(dw_gu_t), slots(dw_out)], ("scatter", "scatter")))
    d_relb = _bias_grad(dl_acc, bucket)

    rsum = lambda a: jnp.sum(a, axis=0)
    early_g = dict(
        rel_bias=d_relb[:, 0, :N_BUCKETS].T, attn_sinks=rsum(dsink_acc)[:N_HEADS],
        gmlp_ln_g=rsum(d_lng), gmlp_ln_b=rsum(d_lnb), gmlp_w_s=jnp.where(causal[None], d_ws, 0.0),
        gmlp_b_s=jnp.sum(d_bs.reshape(BLK, N_GROUPS, GMLP_W // N_GROUPS), axis=2).T,
        attn_out_g=rsum(d_aog), gmlp_out_g=rsum(d_gog), ln1_g=rsum(d_ln1g), ln1_b=rsum(d_ln1b),
        ln2_g=rsum(d_ln2g), ln2_b=rsum(d_ln2b))
    (grad_x, dproj_b, d_bin, d_sc1, d_sh1), _ = _din(dproj, dkvn, w_in_t, xs, dz1, sc1, tm, comm=None)
    dw_in_t, (early_all,) = _wgrad("wgrad_in", dproj_b, h1, IN_W, tk_tok,
                                   comm=([_pack(early_g, SMALL_EARLY)], ("gather2",)))
    dmod = jnp.concatenate([rsum(d_sh1), rsum(d_sc1), rsum(d_g1), rsum(d_sh2), rsum(d_sc2), rsum(d_g2)])
    late_all, r_in = _exchange("scatter_in", [_pack(dict(b_ada=dmod, b_in=rsum(d_bin)), SMALL_LATE), slots(dw_in_t)],
                               ("gather", "scatter"))

    small = [{}, {}, {}, {}]
    for label, names, parts in (("adam_small_early", SMALL_EARLY, early_all), ("adam_small_late", SMALL_LATE, late_all)):
        res = _adam_reduce(label, parts, _pack(wts, names), _pack(mom_m, names), _pack(mom_v, names), parts.shape[1])
        shapes = {k: wts[k].shape for k in names}
        for i in range(4):
            small[i].update(_unpack(res[i], shapes, names))

    dmod_all = late_all[:, :_seg_rows(6 * D_MODEL), :].reshape(N_DEV, 6 * D_MODEL)
    dmod_cols = lax.dynamic_slice(dmod_all, (0, me * ncol), (N_DEV, ncol))
    kpad = 128 - N_DEV
    ada = _adam_w_ada(jnp.pad(c_all.T, ((0, 0), (0, kpad))), jnp.pad(dmod_cols, ((0, kpad), (0, 0))),
                      w_ada[0], m_w_ada[0], v_w_ada[0])

    tr = lambda a: jnp.swapaxes(a, -1, -2)
    big = {}
    big["w_in"] = [tr(o)[None] for o in _adam_reduce("adam_w_in", r_in, w_in[0].T, m_w_in[0].T, v_w_in[0].T, 112)]
    big["w_out"] = [o[None] for o in _adam_reduce("adam_w_out", r_out, w_out[0], m_w_out[0], v_w_out[0], 128)]
    big["w_gate_up"] = [tr(o)[None] for o in _adam_reduce("adam_w_gu", r_gu, w_gate_up[0].T, m_w_gate_up[0].T,
                                                           v_w_gate_up[0].T, 352)]
    big["w_down"] = [o[None] for o in _adam_reduce("adam_w_down", r_down, w_down[0], m_w_down[0], v_w_down[0], 176)]
    big["w_ada"] = [o[None] for o in ada]

    outs = [[], [], [], []]
    for name in WEIGHTS:
        for i in range(4):
            outs[i].append(big[name][i] if name in big else small[i][name])
    return (loss, grad_x[None], *outs[0], *outs[1], *outs[2], *outs[3])
```

```python
import math

import jax
import jax.numpy as jnp
from jax import lax
from jax.experimental import pallas as pl
from jax.experimental.pallas import tpu as pltpu

F32 = jnp.float32
BF16 = jnp.bfloat16
MESH = pl.DeviceIdType.MESH

N_DEV = 8
D_MODEL = 1024
HEAD_DIM = 64
N_HEADS = 8
N_GROUPS = 8
ATTN_W = 512
KV_W = 128
GMLP_W = 512
IN_W = 1792
BLK = 128
N_BUCKETS = 32
MAX_DISTANCE = 128
D_FF = 2816
ALPHA = 2.0 ** 0.25
LN_EPS = 1e-5
NEG_INF = -1e30
ADAM_LR = 0.001
ADAM_B1 = 0.9
ADAM_B2 = 0.999
ADAM_EPS = 1e-08
ADAM_WD = 0.01
ADAM_STEP = 10
GELU_C0 = math.sqrt(2.0 / math.pi)
GELU_C1 = 0.044715

VMEM_LIMIT = 56 * 1024 * 1024


def _params(sem):
    return pltpu.CompilerParams(dimension_semantics=sem, vmem_limit_bytes=VMEM_LIMIT)


def _dot(a, b):
    return lax.dot_general(a, b, (((1,), (0,)), ((), ())), preferred_element_type=F32)


def _dot_nt(a, b):
    return lax.dot_general(a, b, (((1,), (1,)), ((), ())), preferred_element_type=F32)


def _dot_tn(a, b):
    return lax.dot_general(a, b, (((0,), (0,)), ((), ())), preferred_element_type=F32)


def _full(shape):
    nd = len(shape)
    return pl.BlockSpec(shape, lambda *_: (0,) * nd)


def _rowsum8(v):
    r, c = v.shape
    return jnp.sum(v.reshape(r // 8, 8, c), axis=0)


def _sigmoid(v):
    return 1.0 / (1.0 + jnp.exp(-v))


def _gelu_parts(v):
    v2 = v * v
    t = jnp.tanh(GELU_C0 * (v + GELU_C1 * v * v2))
    g = 0.5 * v * (1.0 + t)
    dg = 0.5 * (1.0 + t) + 0.5 * v * (1.0 - t * t) * (GELU_C0 * (1.0 + 3.0 * GELU_C1 * v2))
    return g, dg


def _ln_stats(z):
    mu = jnp.mean(z, axis=1, keepdims=True)
    zc = z - mu
    var = jnp.mean(zc * zc, axis=1, keepdims=True)
    rstd = lax.rsqrt(var + LN_EPS)
    return zc * rstd, rstd


def _ln_bwd(dxhat, xhat, rstd):
    m1 = jnp.mean(dxhat, axis=1, keepdims=True)
    m2 = jnp.mean(dxhat * xhat, axis=1, keepdims=True)
    return rstd * (dxhat - m1 - xhat * m2)


def _seg_mean64(v):
    r = v.shape[0]
    lo = lax.broadcasted_iota(jnp.int32, (r, 128), 1) < 64
    outs = []
    for j in range(v.shape[1] // 128):
        ch = v[:, 128 * j:128 * (j + 1)]
        s_lo = jnp.sum(jnp.where(lo, ch, 0.0), axis=1, keepdims=True)
        s_hi = jnp.sum(jnp.where(lo, 0.0, ch), axis=1, keepdims=True)
        outs.append(jnp.where(lo, s_lo, s_hi) * (1.0 / 64.0))
    return jnp.concatenate(outs, axis=1)


def _rms(a, g):
    r = lax.rsqrt(jnp.mean(a * a, axis=1, keepdims=True) + LN_EPS)
    return a * r * g, r


def _rms_bwd(dout, a, r, g):
    t = dout * g
    return r * t - a * (r * r * r) * jnp.mean(t * a, axis=1, keepdims=True)


PEER_ORDER = (1, 2, 4, 3, 5, 6, 7)


def _peer(j):
    x, y, c = lax.axis_index("x"), lax.axis_index("y"), lax.axis_index("c")
    px = 1 - x if j & 4 else x
    py = 1 - y if j & 2 else y
    pc = 1 - c if j & 1 else c
    return (px, py, pc), 4 * px + 2 * py + pc


SIBLING = 1
CHIP_FLIPS = (4, 2, 6)


def _exchange_phase(phase, ins, outs, modes, send_sems, recv_sems, loc_sems):
    me = 4 * lax.axis_index("x") + 2 * lax.axis_index("y") + lax.axis_index("c")
    for k, mode in enumerate(modes):
        def copy(i, src, slot, dev, k=k):
            return pltpu.make_async_remote_copy(src_ref=src, dst_ref=outs[k].at[slot], send_sem=send_sems.at[k, i],
                                                recv_sem=recv_sems.at[k, i], device_id=dev, device_id_type=MESH)

        src_me = ins[k].at[me] if mode == "scatter" else ins[k]
        local = pltpu.make_async_copy(src_me, outs[k].at[me], loc_sems.at[k])
        if mode == "gather2":
            sib_dev, sib_idx = _peer(SIBLING)
            chips = [_peer(j) for j in CHIP_FLIPS]
            far = [_peer(j | SIBLING)[1] for j in CHIP_FLIPS]
            if phase == "start":
                local.start()
                copy(0, ins[k], me, sib_dev).start()
                for i, (dev, _) in enumerate(chips):
                    copy(1 + i, ins[k], me, dev).start()
            elif phase == "mid":
                for i, (dev, idx) in enumerate(chips):
                    copy(1 + i, ins[k], idx, dev).wait_recv()
                    copy(4 + i, outs[k].at[idx], idx, sib_dev).start()
            else:
                copy(0, ins[k], sib_idx, sib_dev).wait_recv()
                for i, slot in enumerate(far):
                    copy(4 + i, ins[k], slot, sib_dev).wait_recv()
                copy(0, ins[k], me, sib_dev).wait_send()
                for i, (dev, idx) in enumerate(chips):
                    copy(1 + i, ins[k], me, dev).wait_send()
                    copy(4 + i, outs[k].at[idx], idx, sib_dev).wait_send()
                local.wait()
            continue
        peers = [_peer(j) for j in PEER_ORDER]
        if phase == "start":
            local.start()
            for i, (dev, idx) in enumerate(peers):
                copy(i, ins[k].at[idx] if mode == "scatter" else ins[k], me, dev).start()
        elif phase == "end":
            for i, (dev, idx) in enumerate(peers):
                copy(i, src_me, idx, dev).wait_recv()
            for i, (dev, idx) in enumerate(peers):
                copy(i, src_me, me, dev).wait_send()
            local.wait()


def _exchange_shapes(arrays, modes):
    return [jax.ShapeDtypeStruct((N_DEV,) + (a.shape[1:] if m == "scatter" else a.shape), a.dtype)
            for a, m in zip(arrays, modes)]


def _exchange_sems(n):
    return [pltpu.SemaphoreType.DMA((n, N_DEV - 1)), pltpu.SemaphoreType.DMA((n, N_DEV - 1)),
            pltpu.SemaphoreType.DMA((n,))]


def _exchange(name, arrays, modes):
    n = len(arrays)

    def body(*refs):
        for phase in ("start", "mid", "end"):
            _exchange_phase(phase, refs[:n], refs[n:2 * n], modes, *refs[2 * n:])

    any_spec = pl.BlockSpec(memory_space=pl.ANY)
    return pl.pallas_call(
        body, name=name, out_shape=_exchange_shapes(arrays, modes),
        in_specs=[any_spec] * n, out_specs=[any_spec] * n, scratch_shapes=_exchange_sems(n),
    )(*arrays)


def _call(body, *, name, grid, in_specs, out_specs, out_shape, args, sem, scratch_shapes=(), comm=None):
    if comm is None:
        outs = pl.pallas_call(body, name=name, grid=grid, in_specs=list(in_specs), out_specs=list(out_specs),
                              out_shape=list(out_shape), scratch_shapes=list(scratch_shapes),
                              compiler_params=_params(sem))(*args)
        return list(outs), []
    arrays, modes = comm
    n_in, n_out, nc, ns = len(in_specs), len(out_specs), len(arrays), len(scratch_shapes)
    n_steps = math.prod(grid)

    def hosted(*refs):
        ins, cins = refs[:n_in], refs[n_in:n_in + nc]
        outs, couts = refs[n_in + nc:n_in + nc + n_out], refs[n_in + nc + n_out:n_in + 2 * nc + n_out]
        scratch = refs[n_in + 2 * nc + n_out:]
        ex = (cins, couts, modes) + tuple(scratch[ns:])
        step = pl.program_id(0)
        for ax in range(1, len(grid)):
            step = step * grid[ax] + pl.program_id(ax)

        @pl.when(step == 0)
        def _():
            _exchange_phase("start", *ex)

        body(*ins, *outs, *scratch[:ns])

        if "gather2" in modes:
            @pl.when(step == (3 * n_steps) // 4)
            def _():
                _exchange_phase("mid", *ex)

        @pl.when(step == n_steps - 1)
        def _():
            _exchange_phase("end", *ex)

    any_spec = pl.BlockSpec(memory_space=pl.ANY)
    res = pl.pallas_call(
        hosted, name=name, grid=grid, in_specs=list(in_specs) + [any_spec] * nc,
        out_specs=list(out_specs) + [any_spec] * nc, out_shape=list(out_shape) + _exchange_shapes(arrays, modes),
        scratch_shapes=list(scratch_shapes) + _exchange_sems(nc),
        compiler_params=_params(tuple("arbitrary" for _ in grid)))(*args, *arrays)
    return list(res[:n_out]), list(res[n_out:])


def _mod_partial(c_all, w_ada, b_ada_cols):
    def body(c_ref, w_ref, b_ref, o_ref):
        cv = c_ref[...]
        s = (cv * _sigmoid(cv)).astype(BF16)
        o_ref[...] = _dot(s, w_ref[...].astype(BF16)) + b_ref[...]

    ncol = w_ada.shape[1]
    return pl.pallas_call(
        body, name="mod_partial", out_shape=jax.ShapeDtypeStruct((N_DEV, ncol), F32),
        in_specs=[_full(c_all.shape), _full(w_ada.shape), _full(b_ada_cols.shape)],
        out_specs=_full((N_DEV, ncol)), grid=(1,), compiler_params=_params(("arbitrary",)),
    )(c_all, w_ada, b_ada_cols)


def _bias_table(rel_bias, bucket):
    def body(rb_ref, bk_ref, o_ref):
        h = pl.program_id(0)
        bk = bk_ref[...]
        acc = jnp.zeros((BLK, 2 * BLK), F32)
        for b in range(N_BUCKETS):
            acc = jnp.where(bk == b, rb_ref[b, h], acc)
        dist = (lax.broadcasted_iota(jnp.int32, (BLK, 2 * BLK), 0) + BLK
                - lax.broadcasted_iota(jnp.int32, (BLK, 2 * BLK), 1))
        o_ref[0] = jnp.where((dist >= 0) & (dist < BLK), acc, NEG_INF)

    return pl.pallas_call(
        body, name="bias_table", out_shape=jax.ShapeDtypeStruct((N_HEADS, BLK, 2 * BLK), F32),
        in_specs=[pl.BlockSpec(memory_space=pltpu.SMEM), _full((BLK, 2 * BLK))],
        out_specs=pl.BlockSpec((1, BLK, 2 * BLK), lambda h: (h, 0, 0)), grid=(N_HEADS,),
        compiler_params=_params(("arbitrary",)),
    )(rel_bias, bucket)


def _bias_grad(dl_acc, bucket):
    def body(dl_ref, bk_ref, o_ref):
        bk = bk_ref[...]
        dl = dl_ref[0]
        lane = lax.broadcasted_iota(jnp.int32, (1, 128), 1)
        row = jnp.zeros((1, 128), F32)
        for b in range(N_BUCKETS):
            s = jnp.sum(jnp.sum(jnp.where(bk == b, dl, 0.0), axis=1, keepdims=True), axis=0, keepdims=True)
            row = jnp.where(lane == b, s, row)
        o_ref[0] = row

    return pl.pallas_call(
        body, name="bias_grad", out_shape=jax.ShapeDtypeStruct((N_HEADS, 1, 128), F32),
        in_specs=[pl.BlockSpec((1, BLK, 2 * BLK), lambda h: (h, 0, 0)), _full((BLK, 2 * BLK))],
        out_specs=pl.BlockSpec((1, 1, 128), lambda h: (h, 0, 0)), grid=(N_HEADS,),
        compiler_params=_params(("arbitrary",)),
    )(dl_acc, bucket)


def _inproj(x, sc1, sh1, w_in_t, b_in, tm):
    t, d = x.shape
    n = w_in_t.shape[0]

    def body(x_ref, sc_ref, sh_ref, w_ref, b_ref, proj_ref, h_ref):
        h = (x_ref[...] * (1.0 + sc_ref[...]) + sh_ref[...]).astype(BF16)
        h_ref[...] = h
        proj_ref[...] = _dot_nt(h, w_ref[...]) + b_ref[...]

    row = lambda w: pl.BlockSpec((tm, w), lambda i: (i, 0))
    return pl.pallas_call(
        body, name="inproj", grid=(t // tm,),
        out_shape=[jax.ShapeDtypeStruct((t, n), F32), jax.ShapeDtypeStruct((t, d), BF16)],
        in_specs=[row(d), _full((1, d)), _full((1, d)), _full((n, d)), _full((1, n))],
        out_specs=[row(n), row(d)], compiler_params=_params(("parallel",)),
    )(x, sc1, sh1, w_in_t, b_in)


def _half_masks():
    lo_q = lax.broadcasted_iota(jnp.int32, (BLK, 128), 1) < 64
    lo_k = lax.broadcasted_iota(jnp.int32, (2 * BLK, 128), 1) < 64
    return lo_q, lo_k


def _head_place(h):
    return h // 2, h % 2, h // 4


def _attn_heads(q, kk, vv, bias_ref, sinks_ref, n):
    lo_q, lo_k = _half_masks()
    kkb, kksb = kk.astype(BF16), pltpu.roll(kk, 64, 1).astype(BF16)
    vvb, vvsb = vv.astype(BF16), pltpu.roll(vv, 64, 1).astype(BF16)
    n0mask = (n == 0) & (lax.broadcasted_iota(jnp.int32, (BLK, 2 * BLK), 1) < BLK)
    chunks, probs = [], []
    for j in range(4):
        qc = q[:, 128 * j:128 * (j + 1)]
        acc = jnp.zeros((BLK, 128), F32)
        for pos in range(2):
            h = 2 * j + pos
            direct = (h // 4) == pos
            mq = lo_q if pos == 0 else jnp.logical_not(lo_q)
            mk = lo_k if pos == 0 else jnp.logical_not(lo_k)
            qm = jnp.where(mq, qc, 0.0).astype(BF16)
            logit = _dot_nt(qm, kkb if direct else kksb) * (HEAD_DIM ** -0.5) + bias_ref[h]
            logit = jnp.where(n0mask, NEG_INF, logit)
            sk = sinks_ref[h]
            m = jnp.maximum(jnp.max(logit, axis=1, keepdims=True), sk)
            e = jnp.exp(logit - m)
            es = jnp.exp(sk - m)
            den = jnp.sum(e, axis=1, keepdims=True) + es
            p = e / den
            vm = jnp.where(mk, vvb if direct else vvsb, jnp.zeros_like(vvb))
            acc = acc + _dot(p.astype(BF16), vm)
            probs.append((p, es / den))
        chunks.append(acc)
    return jnp.concatenate(chunks, axis=1), probs


def _gmlp_block(gu, gv, lng, lnb, ws_ref, bfull):
    lo_q, _ = _half_masks()
    u, du = _gelu_parts(gu)
    a, da = _gelu_parts(gv)
    mu = _seg_mean64(a)
    ac = a - mu
    rstd = lax.rsqrt(_seg_mean64(ac * ac) + LN_EPS)
    vhat = ac * rstd
    vn = vhat * lng + lnb
    chunks = []
    for j in range(4):
        vc = vn[:, 128 * j:128 * (j + 1)]
        acc = jnp.zeros((BLK, 128), F32)
        for pos in range(2):
            mq = lo_q if pos == 0 else jnp.logical_not(lo_q)
            acc = acc + _dot(ws_ref[2 * j + pos], jnp.where(mq, vc, 0.0).astype(BF16))
        chunks.append(acc)
    ms = jnp.concatenate(chunks, axis=1) + bfull
    return u * ms, (u, du, da, vhat, rstd, vn, ms)


def _mix_in_specs(nb):
    return [pl.BlockSpec((BLK, IN_W), lambda n: (n, 0)),
            pl.BlockSpec((BLK, 2 * KV_W), lambda n: (jnp.maximum(n - 1, 0), ATTN_W // (2 * KV_W))),
            _full((N_HEADS, BLK, 2 * BLK)),
            pl.BlockSpec(memory_space=pltpu.SMEM),
            _full((1, GMLP_W)), _full((1, GMLP_W)),
            _full((N_GROUPS, BLK, BLK)), _full((BLK, GMLP_W)),
            _full((1, ATTN_W)), _full((1, GMLP_W))]


def _split_proj(proj_ref, kvp_ref):
    q = proj_ref[:, 0:ATTN_W]
    k = proj_ref[:, ATTN_W:ATTN_W + KV_W]
    v = proj_ref[:, ATTN_W + KV_W:ATTN_W + 2 * KV_W]
    gu = proj_ref[:, ATTN_W + 2 * KV_W:ATTN_W + 2 * KV_W + GMLP_W]
    gv = proj_ref[:, ATTN_W + 2 * KV_W + GMLP_W:IN_W]
    kk = jnp.concatenate([kvp_ref[:, 0:KV_W], k], axis=0)
    vv = jnp.concatenate([kvp_ref[:, KV_W:2 * KV_W], v], axis=0)
    return q, kk, vv, gu, gv


def _mix_fwd(proj, bias, sinks, lng, lnb, ws, bfull, aog, gog, comm):
    t = proj.shape[0]
    nb = t // BLK

    def body(proj_ref, kvp_ref, bias_ref, sinks_ref, lng_ref, lnb_ref, ws_ref, bfull_ref, aog_ref, gog_ref, out_ref):
        n = pl.program_id(0)
        q, kk, vv, gu, gv = _split_proj(proj_ref, kvp_ref)
        attn, _ = _attn_heads(q, kk, vv, bias_ref, sinks_ref, n)
        gm, _ = _gmlp_block(gu, gv, lng_ref[...], lnb_ref[...], ws_ref, bfull_ref[...])
        out_ref[:, 0:ATTN_W] = _rms(attn, aog_ref[...])[0].astype(BF16)
        out_ref[:, ATTN_W:ATTN_W + GMLP_W] = _rms(gm, gog_ref[...])[0].astype(BF16)

    return _call(
        body, name="mix_fwd", grid=(nb,), out_shape=[jax.ShapeDtypeStruct((t, D_MODEL), BF16)],
        in_specs=_mix_in_specs(nb), out_specs=[pl.BlockSpec((BLK, D_MODEL), lambda n: (n, 0))],
        sem=("parallel",), comm=comm, args=(proj, proj, bias, sinks, lng, lnb, ws, bfull, aog, gog))


def _mix_bwd(proj, bias, sinks, lng, lnb, ws, ws_t, bfull, aog, gog, dy, w_out, comm):
    t = proj.shape[0]
    nb = t // BLK

    def body(proj_ref, kvp_ref, bias_ref, sinks_ref, lng_ref, lnb_ref, ws_ref, bfull_ref, aog_ref, gog_ref,
             wst_ref, dy_ref, wout_ref,
             dproj_ref, dkvn_ref, dl_ref, dsink_ref, dlng_ref, dlnb_ref, dws_ref, dbs_ref, daog_ref, dgog_ref):
        n = pl.program_id(0)

        @pl.when(n == 0)
        def _():
            for r in (dl_ref, dsink_ref, dlng_ref, dlnb_ref, dws_ref, dbs_ref, daog_ref, dgog_ref):
                r[...] = jnp.zeros_like(r)

        lo_q, lo_k = _half_masks()
        q, kk, vv, gu, gv = _split_proj(proj_ref, kvp_ref)
        dmix = _dot_nt(dy_ref[...], wout_ref[...])
        dma, dmg = dmix[:, 0:ATTN_W], dmix[:, ATTN_W:ATTN_W + GMLP_W]

        attn, probs = _attn_heads(q, kk, vv, bias_ref, sinks_ref, n)
        aog = aog_ref[...]
        _, r_a = _rms(attn, aog)
        daog_ref[...] += _rowsum8(dma * attn * r_a)
        dattn = _rms_bwd(dma, attn, r_a, aog)

        kkb, kksb = kk.astype(BF16), pltpu.roll(kk, 64, 1).astype(BF16)
        vvb, vvsb = vv.astype(BF16), pltpu.roll(vv, 64, 1).astype(BF16)
        lane = lax.broadcasted_iota(jnp.int32, (BLK, 128), 1)
        dk_d = jnp.zeros((2 * BLK, 128), F32)
        dk_s = jnp.zeros((2 * BLK, 128), F32)
        dv_d = jnp.zeros((2 * BLK, 128), F32)
        dv_s = jnp.zeros((2 * BLK, 128), F32)
        dsink = jnp.zeros((BLK, 128), F32)
        dq_chunks = []
        for j in range(4):
            qc = q[:, 128 * j:128 * (j + 1)]
            doc = dattn[:, 128 * j:128 * (j + 1)]
            dq = jnp.zeros((BLK, 128), F32)
            for pos in range(2):
                h = 2 * j + pos
                direct = (h // 4) == pos
                mq = lo_q if pos == 0 else jnp.logical_not(lo_q)
                mk = lo_k if pos == 0 else jnp.logical_not(lo_k)
                p, psink = probs[h]
                qm = jnp.where(mq, qc, 0.0).astype(BF16)
                dom = jnp.where(mq, doc, 0.0).astype(BF16)
                dp = _dot_nt(dom, vvb if direct else vvsb)
                rs = jnp.sum(p * dp, axis=1, keepdims=True)
                dl = p * (dp - rs)
                dl_ref[h] += dl
                dsink = dsink + jnp.where(lane == h, -psink * rs, 0.0)
                dls = (dl * (HEAD_DIM ** -0.5)).astype(BF16)
                km = jnp.where(mk, kkb if direct else kksb, jnp.zeros_like(kkb))
                dq = dq + _dot(dls, km)
                dk_h = _dot_tn(dls, qm)
                dv_h = _dot_tn(p.astype(BF16), dom)
                if direct:
                    dk_d, dv_d = dk_d + dk_h, dv_d + dv_h
                else:
                    dk_s, dv_s = dk_s + dk_h, dv_s + dv_h
            dq_chunks.append(dq)
        dsink_ref[...] += dsink
        dk = dk_d + pltpu.roll(dk_s, 64, 1)
        dv = dv_d + pltpu.roll(dv_s, 64, 1)
        for j in range(4):
            dproj_ref[:, 128 * j:128 * (j + 1)] = dq_chunks[j]
        dproj_ref[:, ATTN_W:ATTN_W + KV_W] = dk[BLK:2 * BLK]
        dproj_ref[:, ATTN_W + KV_W:ATTN_W + 2 * KV_W] = dv[BLK:2 * BLK]
        dkvn_ref[:, 0:KV_W] = dk[0:BLK]
        dkvn_ref[:, KV_W:2 * KV_W] = dv[0:BLK]

        lng = lng_ref[...]
        gog = gog_ref[...]
        gm, (u, du, da, vhat, rstd, vn, ms) = _gmlp_block(gu, gv, lng, lnb_ref[...], ws_ref, bfull_ref[...])
        _, r_g = _rms(gm, gog)
        dgog_ref[...] += _rowsum8(dmg * gm * r_g)
        dgm = _rms_bwd(dmg, gm, r_g, gog)
        dproj_ref[:, ATTN_W + 2 * KV_W:ATTN_W + 2 * KV_W + GMLP_W] = dgm * ms * du
        dms = dgm * u
        dbs_ref[...] += dms
        dvn_chunks = []
        for j in range(4):
            dmc = dms[:, 128 * j:128 * (j + 1)]
            vcb = vn[:, 128 * j:128 * (j + 1)].astype(BF16)
            acc = jnp.zeros((BLK, 128), F32)
            for pos in range(2):
                g = 2 * j + pos
                mq = lo_q if pos == 0 else jnp.logical_not(lo_q)
                dm = jnp.where(mq, dmc, 0.0).astype(BF16)
                dws_ref[g] += _dot_nt(dm, vcb)
                acc = acc + _dot(wst_ref[g], dm)
            dvn_chunks.append(acc)
        dvn = jnp.concatenate(dvn_chunks, axis=1)
        dlng_ref[...] += _rowsum8(dvn * vhat)
        dlnb_ref[...] += _rowsum8(dvn)
        dvh = dvn * lng
        dact = rstd * (dvh - _seg_mean64(dvh) - vhat * _seg_mean64(dvh * vhat))
        dproj_ref[:, ATTN_W + 2 * KV_W + GMLP_W:IN_W] = dact * da

    acc8 = lambda w: jax.ShapeDtypeStruct((8, w), F32)
    out_shape = [jax.ShapeDtypeStruct((t, IN_W), F32), jax.ShapeDtypeStruct((t, 2 * KV_W), F32),
                 jax.ShapeDtypeStruct((N_HEADS, BLK, 2 * BLK), F32), jax.ShapeDtypeStruct((BLK, 128), F32),
                 acc8(GMLP_W), acc8(GMLP_W), jax.ShapeDtypeStruct((N_GROUPS, BLK, BLK), F32),
                 jax.ShapeDtypeStruct((BLK, GMLP_W), F32), acc8(ATTN_W), acc8(GMLP_W)]
    out_specs = [pl.BlockSpec((BLK, IN_W), lambda n: (n, 0)),
                 pl.BlockSpec((BLK, 2 * KV_W), lambda n: ((n + nb - 1) % nb, 0)),
                 _full((N_HEADS, BLK, 2 * BLK)), _full((BLK, 128)), _full((8, GMLP_W)), _full((8, GMLP_W)),
                 _full((N_GROUPS, BLK, BLK)), _full((BLK, GMLP_W)), _full((8, ATTN_W)), _full((8, GMLP_W))]
    in_specs = _mix_in_specs(nb) + [_full((N_GROUPS, BLK, BLK)),
                                    pl.BlockSpec((BLK, D_MODEL), lambda n: (n, 0)),
                                    _full((D_MODEL, D_MODEL))]
    return _call(
        body, name="mix_bwd", grid=(nb,), out_shape=out_shape, in_specs=in_specs, out_specs=out_specs,
        sem=("arbitrary",), comm=comm, args=(proj, proj, bias, sinks, lng, lnb, ws, bfull, aog, gog, ws_t, dy, w_out))


HALF = 64
ROWS = 32


def _lane_lo(rows):
    return lax.broadcasted_iota(jnp.int32, (rows, 128), 1) < 64


def _mix_stage_kv(proj_ref, kvp_ref, s):
    lo = _lane_lo(2 * BLK)
    for name, col in (("k", ATTN_W), ("v", ATTN_W + KV_W)):
        cur = jnp.concatenate([kvp_ref[:, col - ATTN_W:col - ATTN_W + KV_W], proj_ref[:, col:col + KV_W]], axis=0)
        plain, swapped = cur.astype(BF16), pltpu.roll(cur, 64, 1).astype(BF16)
        zero = jnp.zeros_like(plain)
        for g in range(2):
            dup = jnp.where(lo, plain, swapped) if g == 0 else jnp.where(lo, swapped, plain)
            s[name + "d"][g] = dup
            s[name + "m"][g] = jnp.concatenate([jnp.where(lo, dup, zero), jnp.where(lo, zero, dup)], axis=0)


def _group_rows(ref, g):
    return ref[4 * g:4 * g + 4].reshape(4 * BLK, ref.shape[2])


def _pair_rows(ref, g):
    return jnp.concatenate([jnp.concatenate([ref[4 * g + 2 * c], ref[4 * g + 2 * c + 1]], axis=1) for c in range(2)],
                           axis=0)


def _mask_heads(src_ref, dst_ref):
    lo = _lane_lo(BLK)
    for j in range(4):
        chunk = src_ref[:, 128 * j:128 * (j + 1)]
        dst_ref[2 * j] = jnp.where(lo, chunk, 0.0).astype(BF16)
        dst_ref[2 * j + 1] = jnp.where(lo, 0.0, chunk).astype(BF16)


def _mix_stage_attn(proj_ref, bias_ref, sinks_ref, n, s, keep):
    _mask_heads(proj_ref, s["qm"])
    for g in range(2):
        s["lg"][g] = _dot_nt(_group_rows(s["qm"], g), s["kd"][g])
    n0mask = (n == 0) & (lax.broadcasted_iota(jnp.int32, (HALF, 2 * BLK), 1) < BLK)
    for h in range(N_HEADS):
        sk = sinks_ref[h]
        for hf in range(BLK // HALF):
            rows = slice(HALF * hf, HALF * (hf + 1))
            grows = slice(BLK * (h % 4) + HALF * hf, BLK * (h % 4) + HALF * (hf + 1))
            logit = s["lg"][h // 4, grows, :] * (HEAD_DIM ** -0.5) + bias_ref[h, rows, :]
            logit = jnp.where(n0mask, NEG_INF, logit)
            m = jnp.maximum(jnp.max(logit, axis=1, keepdims=True), sk)
            e = jnp.exp(logit - m)
            es = jnp.exp(sk - m)
            inv = 1.0 / (jnp.sum(e, axis=1, keepdims=True) + es)
            p = e * inv
            s["pb"][h, rows, :] = p.astype(BF16)
            if keep:
                s["p"][h, rows, :] = p
                s["psink"][h, rows, :] = es * inv
    for g in range(2):
        out = _dot(_pair_rows(s["pb"], g), s["vm"][g])
        s["attn"][:, 256 * g:256 * g + 128] = out[0:BLK]
        s["attn"][:, 256 * g + 128:256 * g + 256] = out[BLK:2 * BLK]


def _mix_stage_gmlp_pre(proj_ref, lng, lnb, s, keep):
    c0 = ATTN_W + 2 * KV_W
    for r0 in range(0, BLK, ROWS):
        rows = slice(r0, r0 + ROWS)
        u, du = _gelu_parts(proj_ref[rows, c0:c0 + GMLP_W])
        a, da = _gelu_parts(proj_ref[rows, c0 + GMLP_W:c0 + 2 * GMLP_W])
        ac = a - _seg_mean64(a)
        rstd = lax.rsqrt(_seg_mean64(ac * ac) + LN_EPS)
        vhat = ac * rstd
        s["u"][rows, :] = u
        s["vnb"][rows, :] = (vhat * lng + lnb).astype(BF16)
        if keep:
            s["du"][rows, :] = du
            s["da"][rows, :] = da
            s["vhat"][rows, :] = vhat
            s["rstd"][rows, :] = rstd


def _stack_halves(chunk):
    lo = _lane_lo(BLK)
    zero = jnp.zeros_like(chunk)
    return jnp.concatenate([jnp.where(lo, chunk, zero), jnp.where(lo, zero, chunk)], axis=0)


def _mix_stage_gmlp_mix(ws2_ref, bfull_ref, s):
    for j in range(4):
        cols = slice(128 * j, 128 * (j + 1))
        s["ms"][:, cols] = _dot(ws2_ref[j], _stack_halves(s["vnb"][:, cols])) + bfull_ref[:, cols]


def _mix_scratch(keep):
    f32 = lambda *shape: pltpu.VMEM(shape, F32)
    b16 = lambda *shape: pltpu.VMEM(shape, BF16)
    names = dict(kd=b16(2, 2 * BLK, 128), vd=b16(2, 2 * BLK, 128), km=b16(2, 4 * BLK, 128), vm=b16(2, 4 * BLK, 128),
                 qm=b16(N_HEADS, BLK, 128), lg=f32(2, 4 * BLK, 2 * BLK), pb=b16(N_HEADS, BLK, 2 * BLK),
                 attn=f32(BLK, ATTN_W), u=f32(BLK, GMLP_W), vnb=b16(BLK, GMLP_W), ms=f32(BLK, GMLP_W))
    if keep:
        names.update(dom=b16(N_HEADS, BLK, 128), p=f32(N_HEADS, BLK, 2 * BLK),
                     dls=b16(N_HEADS, BLK, 2 * BLK), psink=f32(N_HEADS, BLK, 1),
                     dattn=f32(BLK, ATTN_W), dmix=f32(BLK, D_MODEL), du=f32(BLK, GMLP_W), da=f32(BLK, GMLP_W),
                     vhat=f32(BLK, GMLP_W), rstd=f32(BLK, GMLP_W), dmsb=b16(BLK, GMLP_W), dvn=f32(BLK, GMLP_W))
    return list(names), list(names.values())


def _mix_specs():
    return [pl.BlockSpec((BLK, IN_W), lambda n: (n, 0)),
            pl.BlockSpec((BLK, 2 * KV_W), lambda n: (jnp.maximum(n - 1, 0), ATTN_W // (2 * KV_W))),
            _full((N_HEADS, BLK, 2 * BLK)),
            pl.BlockSpec(memory_space=pltpu.SMEM),
            _full((1, GMLP_W)), _full((1, GMLP_W)),
            _full((N_GROUPS // 2, BLK, 2 * BLK)), _full((BLK, GMLP_W)),
            _full((1, ATTN_W)), _full((1, GMLP_W))]


def _mix_fwd(proj, bias, sinks, lng, lnb, ws2, bfull, aog, gog, comm):
    t = proj.shape[0]
    names, shapes = _mix_scratch(False)

    def body(proj_ref, kvp_ref, bias_ref, sinks_ref, lng_ref, lnb_ref, ws2_ref, bfull_ref, aog_ref, gog_ref,
             out_ref, *scratch):
        s = dict(zip(names, scratch))
        n = pl.program_id(0)
        _mix_stage_kv(proj_ref, kvp_ref, s)
        _mix_stage_attn(proj_ref, bias_ref, sinks_ref, n, s, False)
        _mix_stage_gmlp_pre(proj_ref, lng_ref[...], lnb_ref[...], s, False)
        _mix_stage_gmlp_mix(ws2_ref, bfull_ref, s)
        for r0 in range(0, BLK, ROWS):
            rows = slice(r0, r0 + ROWS)
            out_ref[rows, 0:ATTN_W] = _rms(s["attn"][rows, :], aog_ref[...])[0].astype(BF16)
            out_ref[rows, ATTN_W:ATTN_W + GMLP_W] = _rms(s["u"][rows, :] * s["ms"][rows, :], gog_ref[...])[0].astype(BF16)

    return _call(
        body, name="mix_fwd", grid=(t // BLK,), out_shape=[jax.ShapeDtypeStruct((t, D_MODEL), BF16)],
        in_specs=_mix_specs(), out_specs=[pl.BlockSpec((BLK, D_MODEL), lambda n: (n, 0))], scratch_shapes=shapes,
        sem=("parallel",), comm=comm, args=(proj, proj, bias, sinks, lng, lnb, ws2, bfull, aog, gog))


def _mix_bwd(proj, bias, sinks, lng, lnb, ws2, wst2, bfull, aog, gog, dy, w_out, comm):
    t = proj.shape[0]
    nb = t // BLK
    names, shapes = _mix_scratch(True)
    c_gu = ATTN_W + 2 * KV_W

    def body(proj_ref, kvp_ref, bias_ref, sinks_ref, lng_ref, lnb_ref, ws2_ref, bfull_ref, aog_ref, gog_ref,
             wst2_ref, dy_ref, wout_ref,
             dproj_ref, dkvn_ref, dl_ref, dsink_ref, dlng_ref, dlnb_ref, dws_ref, dbs_ref, daog_ref, dgog_ref,
             *scratch):
        s = dict(zip(names, scratch))
        n = pl.program_id(0)

        @pl.when(n == 0)
        def _():
            for r in (dl_ref, dsink_ref, dlng_ref, dlnb_ref, dws_ref, dbs_ref, daog_ref, dgog_ref):
                r[...] = jnp.zeros_like(r)

        s["dmix"][...] = _dot_nt(dy_ref[...], wout_ref[...])
        _mix_stage_kv(proj_ref, kvp_ref, s)
        _mix_stage_attn(proj_ref, bias_ref, sinks_ref, n, s, True)
        lng = lng_ref[...]
        _mix_stage_gmlp_pre(proj_ref, lng, lnb_ref[...], s, True)
        _mix_stage_gmlp_mix(ws2_ref, bfull_ref, s)

        aog, gog = aog_ref[...], gog_ref[...]
        for r0 in range(0, BLK, ROWS):
            rows = slice(r0, r0 + ROWS)
            attn, dma = s["attn"][rows, :], s["dmix"][rows, 0:ATTN_W]
            _, r_a = _rms(attn, aog)
            daog_ref[...] += _rowsum8(dma * attn * r_a)
            s["dattn"][rows, :] = _rms_bwd(dma, attn, r_a, aog)
            u, ms, dmg = s["u"][rows, :], s["ms"][rows, :], s["dmix"][rows, ATTN_W:ATTN_W + GMLP_W]
            gm = u * ms
            _, r_g = _rms(gm, gog)
            dgog_ref[...] += _rowsum8(dmg * gm * r_g)
            dgm = _rms_bwd(dmg, gm, r_g, gog)
            dproj_ref[rows, c_gu:c_gu + GMLP_W] = dgm * ms * s["du"][rows, :]
            dms = dgm * u
            dbs_ref[rows, :] += dms
            s["dmsb"][rows, :] = dms.astype(BF16)

        _mask_heads(s["dattn"], s["dom"])
        for g in range(2):
            s["lg"][g] = _dot_nt(_group_rows(s["dom"], g), s["vd"][g])
        lane = lax.broadcasted_iota(jnp.int32, (HALF, 128), 1)
        for hf in range(BLK // HALF):
            rows = slice(HALF * hf, HALF * (hf + 1))
            dsink = jnp.zeros((HALF, 128), F32)
            for h in range(N_HEADS):
                grows = slice(BLK * (h % 4) + HALF * hf, BLK * (h % 4) + HALF * (hf + 1))
                dp = s["lg"][h // 4, grows, :]
                p = s["p"][h, rows, :]
                rs = jnp.sum(p * dp, axis=1, keepdims=True)
                dl = p * (dp - rs)
                dl_ref[h, rows, :] += dl
                dsink = dsink + jnp.where(lane == h, -s["psink"][h, rows, :] * rs, 0.0)
                s["dls"][h, rows, :] = (dl * (HEAD_DIM ** -0.5)).astype(BF16)
            dsink_ref[rows, :] += dsink
        for g in range(2):
            dq = _dot(_pair_rows(s["dls"], g), s["km"][g])
            dproj_ref[:, 256 * g:256 * g + 128] = dq[0:BLK]
            dproj_ref[:, 256 * g + 128:256 * g + 256] = dq[BLK:2 * BLK]
        lo_k = _lane_lo(2 * BLK)
        for col, lhs, rhs in ((0, "dls", "qm"), (KV_W, "pb", "dom")):
            raw = [_dot_tn(_group_rows(s[lhs], g), _group_rows(s[rhs], g)) for g in range(2)]
            both = [r + pltpu.roll(r, 64, 1) for r in raw]
            dkv = jnp.where(lo_k, both[0], both[1])
            dproj_ref[:, ATTN_W + col:ATTN_W + col + KV_W] = dkv[BLK:2 * BLK]
            dkvn_ref[:, col:col + KV_W] = dkv[0:BLK]

        for j in range(4):
            cols = slice(128 * j, 128 * (j + 1))
            dm2 = _stack_halves(s["dmsb"][:, cols])
            vnb = s["vnb"][:, cols]
            dws2 = _dot_nt(dm2, vnb)
            dws_ref[2 * j] += dws2[0:BLK]
            dws_ref[2 * j + 1] += dws2[BLK:2 * BLK]
            s["dvn"][:, cols] = _dot(wst2_ref[j], dm2)
        for r0 in range(0, BLK, ROWS):
            rows = slice(r0, r0 + ROWS)
            dvn, vhat = s["dvn"][rows, :], s["vhat"][rows, :]
            dlng_ref[...] += _rowsum8(dvn * vhat)
            dlnb_ref[...] += _rowsum8(dvn)
            dvh = dvn * lng
            dact = s["rstd"][rows, :] * (dvh - _seg_mean64(dvh) - vhat * _seg_mean64(dvh * vhat))
            dproj_ref[rows, c_gu + GMLP_W:IN_W] = dact * s["da"][rows, :]

    acc8 = lambda w: jax.ShapeDtypeStruct((8, w), F32)
    out_shape = [jax.ShapeDtypeStruct((t, IN_W), F32), jax.ShapeDtypeStruct((t, 2 * KV_W), F32),
                 jax.ShapeDtypeStruct((N_HEADS, BLK, 2 * BLK), F32), jax.ShapeDtypeStruct((BLK, 128), F32),
                 acc8(GMLP_W), acc8(GMLP_W), jax.ShapeDtypeStruct((N_GROUPS, BLK, BLK), F32),
                 jax.ShapeDtypeStruct((BLK, GMLP_W), F32), acc8(ATTN_W), acc8(GMLP_W)]
    out_specs = [pl.BlockSpec((BLK, IN_W), lambda n: (n, 0)),
                 pl.BlockSpec((BLK, 2 * KV_W), lambda n: ((n + nb - 1) % nb, 0)),
                 _full((N_HEADS, BLK, 2 * BLK)), _full((BLK, 128)), _full((8, GMLP_W)), _full((8, GMLP_W)),
                 _full((N_GROUPS, BLK, BLK)), _full((BLK, GMLP_W)), _full((8, ATTN_W)), _full((8, GMLP_W))]
    in_specs = _mix_specs() + [_full((N_GROUPS // 2, BLK, 2 * BLK)),
                               pl.BlockSpec((BLK, D_MODEL), lambda n: (n, 0)),
                               _full((D_MODEL, D_MODEL))]
    return _call(
        body, name="mix_bwd", grid=(nb,), out_shape=out_shape, in_specs=in_specs, out_specs=out_specs,
        scratch_shapes=shapes, sem=("arbitrary",), comm=comm,
        args=(proj, proj, bias, sinks, lng, lnb, ws2, bfull, aog, gog, wst2, dy, w_out))


def _outproj(mixed, w_out, x, g1, ln1g, ln1b, sc2, sh2, tm, comm):
    t, d = x.shape

    def body(mx_ref, w_ref, x_ref, g1_ref, lg_ref, lb_ref, sc_ref, sh_ref, y_ref, x1_ref, h2_ref):
        y = _dot(mx_ref[...], w_ref[...])
        xhat, _ = _ln_stats(ALPHA * x_ref[...] + g1_ref[...] * y)
        x1 = xhat * lg_ref[...] + lb_ref[...]
        y_ref[...] = y
        x1_ref[...] = x1
        h2_ref[...] = (x1 * (1.0 + sc_ref[...]) + sh_ref[...]).astype(BF16)

    row = pl.BlockSpec((tm, d), lambda i: (i, 0))
    vec = _full((1, d))
    return _call(
        body, name="outproj", grid=(t // tm,),
        out_shape=[jax.ShapeDtypeStruct((t, d), F32), jax.ShapeDtypeStruct((t, d), F32),
                   jax.ShapeDtypeStruct((t, d), BF16)],
        in_specs=[row, _full((d, d)), row, vec, vec, vec, vec, vec], out_specs=[row, row, row],
        sem=("parallel",), comm=comm, args=(mixed, w_out, x, g1, ln1g, ln1b, sc2, sh2))


def _ffn_fwd(h2, w_gu_t, w_down, x1, target, g2, ln2g, ln2b, tm):
    t, d = x1.shape

    def body(h_ref, w_ref, wd_ref, x1_ref, tg_ref, g2_ref, lg_ref, lb_ref,
             dsu_ref, sg_ref, act_ref, dz_ref, dy_ref, loss_ref, dlg_ref, dlb_ref, dg2_ref):
        @pl.when(pl.program_id(0) == 0)
        def _():
            for r in (loss_ref, dlg_ref, dlb_ref, dg2_ref):
                r[...] = jnp.zeros_like(r)

        h = h_ref[...]
        g = _dot_nt(h, w_ref[0:D_FF])
        u = _dot_nt(h, w_ref[D_FF:2 * D_FF])
        s = _sigmoid(g)
        sg = g * s
        act = (sg * u).astype(BF16)
        dsu_ref[...] = (u * (s * (1.0 + g * (1.0 - s)))).astype(BF16)
        sg_ref[...] = sg.astype(BF16)
        act_ref[...] = act
        y2 = _dot(act, wd_ref[...])
        g2 = g2_ref[...]
        lg = lg_ref[...]
        xhat, rstd = _ln_stats(ALPHA * x1_ref[...] + g2 * y2)
        err = xhat * lg + lb_ref[...] - tg_ref[...]
        loss_ref[...] += _rowsum8(err * err)
        dx2 = err * (1.0 / d)
        dlg_ref[...] += _rowsum8(dx2 * xhat)
        dlb_ref[...] += _rowsum8(dx2)
        dz = _ln_bwd(dx2 * lg, xhat, rstd)
        dg2_ref[...] += _rowsum8(dz * y2)
        dz_ref[...] = dz
        dy_ref[...] = (g2 * dz).astype(BF16)

    row = pl.BlockSpec((tm, d), lambda i: (i, 0))
    wide = pl.BlockSpec((tm, D_FF), lambda i: (i, 0))
    vec = _full((1, d))
    acc = _full((8, d))
    acc_shape = jax.ShapeDtypeStruct((8, d), F32)
    wide_shape = jax.ShapeDtypeStruct((t, D_FF), BF16)
    return pl.pallas_call(
        body, name="ffn_fwd", grid=(t // tm,),
        out_shape=[wide_shape] * 3 + [jax.ShapeDtypeStruct((t, d), F32), jax.ShapeDtypeStruct((t, d), BF16)]
        + [acc_shape] * 4,
        in_specs=[row, _resident((2 * D_FF, d)), _resident((D_FF, d)), row, row, vec, vec, vec],
        out_specs=[wide] * 3 + [row, row, acc, acc, acc, acc], compiler_params=_params(("arbitrary",)),
    )(h2, w_gu_t, w_down, x1, target, g2, ln2g, ln2b)


def _resident(shape):
    nd = len(shape)
    return pl.BlockSpec(shape, lambda *_: (0,) * nd, pipeline_mode=pl.Buffered(1))


def _ffn_bwd(dy2, w_down, dsu, sg, w_gu_t, x1, x, y, dz2, sc2, g1, ln1g, tm, comm):
    t, d = x1.shape

    def body(dy2_ref, wd_ref, dsu_ref, sg_ref, w_ref, x1_ref, x_ref, y_ref, dz2_ref, sc_ref, g1_ref, lg_ref,
             dg_ref, du_ref, dz1_ref, dy_ref, dsc_ref, dsh_ref, dlg_ref, dlb_ref, dg1_ref):
        @pl.when(pl.program_id(0) == 0)
        def _():
            for r in (dsc_ref, dsh_ref, dlg_ref, dlb_ref, dg1_ref):
                r[...] = jnp.zeros_like(r)

        dact = _dot_nt(dy2_ref[...], wd_ref[...])
        dg = (dact * dsu_ref[...].astype(F32)).astype(BF16)
        du = (dact * sg_ref[...].astype(F32)).astype(BF16)
        dg_ref[...] = dg
        du_ref[...] = du
        dh2 = _dot(dg, w_ref[0:D_FF]) + _dot(du, w_ref[D_FF:2 * D_FF])
        x1 = x1_ref[...]
        y = y_ref[...]
        g1 = g1_ref[...]
        dsc_ref[...] += _rowsum8(dh2 * x1)
        dsh_ref[...] += _rowsum8(dh2)
        dx1 = dh2 * (1.0 + sc_ref[...]) + ALPHA * dz2_ref[...]
        xhat, rstd = _ln_stats(ALPHA * x_ref[...] + g1 * y)
        dlg_ref[...] += _rowsum8(dx1 * xhat)
        dlb_ref[...] += _rowsum8(dx1)
        dz1 = _ln_bwd(dx1 * lg_ref[...], xhat, rstd)
        dg1_ref[...] += _rowsum8(dz1 * y)
        dz1_ref[...] = dz1
        dy_ref[...] = (g1 * dz1).astype(BF16)

    row = pl.BlockSpec((tm, d), lambda i: (i, 0))
    wide = pl.BlockSpec((tm, D_FF), lambda i: (i, 0))
    vec = _full((1, d))
    acc = _full((8, d))
    acc_shape = jax.ShapeDtypeStruct((8, d), F32)
    wide_shape = jax.ShapeDtypeStruct((t, D_FF), BF16)
    return _call(
        body, name="ffn_bwd", grid=(t // tm,),
        out_shape=[wide_shape, wide_shape, jax.ShapeDtypeStruct((t, d), F32), jax.ShapeDtypeStruct((t, d), BF16)]
        + [acc_shape] * 5,
        in_specs=[row, _resident((D_FF, d)), wide, wide, _resident((2 * D_FF, d)), row, row, row, row, vec, vec, vec],
        out_specs=[wide, wide, row, row, acc, acc, acc, acc, acc], sem=("arbitrary",), comm=comm,
        args=(dy2, w_down, dsu, sg, w_gu_t, x1, x, y, dz2, sc2, g1, ln1g))


def _din(dproj, dkvn, w_in_t, x, dz1, sc1, tm, comm):
    t, d = x.shape

    def body(dp_ref, dkv_ref, w_ref, x_ref, dz1_ref, sc_ref, dx_ref, dpb_ref, dbin_ref, dsc_ref, dsh_ref):
        @pl.when(pl.program_id(0) == 0)
        def _():
            for r in (dbin_ref, dsc_ref, dsh_ref):
                r[...] = jnp.zeros_like(r)

        dp = jnp.concatenate([dp_ref[:, 0:ATTN_W], dp_ref[:, ATTN_W:ATTN_W + 2 * KV_W] + dkv_ref[...],
                              dp_ref[:, ATTN_W + 2 * KV_W:IN_W]], axis=1)
        dbin_ref[...] += _rowsum8(dp)
        dpb = dp.astype(BF16)
        dpb_ref[...] = dpb
        dh = _dot(dpb, w_ref[...])
        dsc_ref[...] += _rowsum8(dh * x_ref[...])
        dsh_ref[...] += _rowsum8(dh)
        dx_ref[...] = dh * (1.0 + sc_ref[...]) + ALPHA * dz1_ref[...]

    row = lambda w: pl.BlockSpec((tm, w), lambda i: (i, 0))
    return _call(
        body, name="din", grid=(t // tm,),
        out_shape=[jax.ShapeDtypeStruct((t, d), F32), jax.ShapeDtypeStruct((t, IN_W), BF16),
                   jax.ShapeDtypeStruct((8, IN_W), F32), jax.ShapeDtypeStruct((8, d), F32),
                   jax.ShapeDtypeStruct((8, d), F32)],
        in_specs=[row(IN_W), row(2 * KV_W), _full((IN_W, d)), row(d), row(d), _full((1, d))],
        out_specs=[row(d), row(IN_W), _full((8, IN_W)), _full((8, d)), _full((8, d))],
        sem=("arbitrary",), comm=comm, args=(dproj, dkvn, w_in_t, x, dz1, sc1))


def _wgrad(name, a, b, tmm, tk, comm=None, a2=None):
    t, m = a.shape
    n = b.shape[1]
    nk = t // tk
    nm = m // tmm

    def body(*refs):
        a_refs, (b_ref, o_ref, acc_ref) = refs[:-3], refs[-3:]
        i, k = pl.program_id(0), pl.program_id(1)
        a_tile = a_refs[0][...] if a2 is None else jnp.where(i < nm, a_refs[0][...], a_refs[1][...])
        part = _dot_tn(a_tile, b_ref[...])

        @pl.when(k == 0)
        def _():
            acc_ref[...] = part

        @pl.when(k > 0)
        def _():
            acc_ref[...] += part

        @pl.when(k == nk - 1)
        def _():
            o_ref[...] = acc_ref[...].astype(BF16)

    if a2 is None:
        a_specs, a_args, n_tiles = [pl.BlockSpec((tk, tmm), lambda i, k: (k, i))], (a,), nm
    else:
        a_specs = [pl.BlockSpec((tk, tmm), lambda i, k: (jnp.where(i < nm, k, 0), jnp.minimum(i, nm - 1))),
                   pl.BlockSpec((tk, tmm), lambda i, k: (jnp.where(i < nm, 0, k), jnp.maximum(i - nm, 0)))]
        a_args, n_tiles = (a, a2), 2 * nm
    (out,), got = _call(
        body, name=name, grid=(n_tiles, nk), out_shape=[jax.ShapeDtypeStruct((n_tiles * tmm, n), BF16)],
        in_specs=a_specs + [pl.BlockSpec((tk, n), lambda i, k: (k, 0))],
        out_specs=[pl.BlockSpec((tmm, n), lambda i, k: (i, 0))],
        scratch_shapes=[pltpu.VMEM((tmm, n), F32)], sem=("parallel", "arbitrary"), comm=comm, args=a_args + (b,))
    return out if comm is None else (out, got)


def _adamw(w, g, m, v):
    m = ADAM_B1 * m + (1.0 - ADAM_B1) * g
    v = ADAM_B2 * v + (1.0 - ADAM_B2) * (g * g)
    m_hat = m / (1.0 - ADAM_B1 ** ADAM_STEP)
    v_hat = v / (1.0 - ADAM_B2 ** ADAM_STEP)
    delta = -ADAM_LR * (m_hat / (jnp.sqrt(v_hat) + ADAM_EPS) + ADAM_WD * w)
    return delta, m, v


def _adam_reduce(name, parts, w, m, v, tr):
    r, cdim = w.shape

    def body(p_ref, w_ref, m_ref, v_ref, g_ref, d_ref, mo_ref, vo_ref):
        g = p_ref[0].astype(F32)
        for s in range(1, N_DEV):
            g = g + p_ref[s].astype(F32)
        d_ref[...], mo_ref[...], vo_ref[...] = _adamw(w_ref[...], g, m_ref[...], v_ref[...])
        g_ref[...] = g

    tile = pl.BlockSpec((tr, cdim), lambda i: (i, 0))
    shp = jax.ShapeDtypeStruct((r, cdim), F32)
    return pl.pallas_call(
        body, name=name, grid=(r // tr,), out_shape=[shp] * 4,
        in_specs=[pl.BlockSpec((N_DEV, tr, cdim), lambda i: (0, i, 0)), tile, tile, tile],
        out_specs=[tile] * 4, compiler_params=_params(("parallel",)),
    )(parts, w, m, v)


def _adam_w_ada(c_all_t, dmod_cols, w, m, v):
    def body(ct_ref, dm_ref, w_ref, m_ref, v_ref, g_ref, d_ref, mo_ref, vo_ref):
        ct = ct_ref[...]
        s = (ct * _sigmoid(ct)).astype(BF16)
        g = _dot(s, dm_ref[...].astype(BF16))
        d_ref[...], mo_ref[...], vo_ref[...] = _adamw(w_ref[...], g, m_ref[...], v_ref[...])
        g_ref[...] = g

    shp = jax.ShapeDtypeStruct(w.shape, F32)
    return pl.pallas_call(
        body, name="adam_w_ada", grid=(1,), out_shape=[shp] * 4,
        in_specs=[_full(c_all_t.shape), _full(dmod_cols.shape)] + [_full(w.shape)] * 3,
        out_specs=[_full(w.shape)] * 4, compiler_params=_params(("arbitrary",)),
    )(c_all_t, dmod_cols, w, m, v)


SMALL_EARLY = ["rel_bias", "attn_sinks", "gmlp_ln_g", "gmlp_ln_b", "gmlp_w_s", "gmlp_b_s",
               "attn_out_g", "gmlp_out_g", "ln1_g", "ln1_b", "ln2_g", "ln2_b"]
SMALL_LATE = ["b_ada", "b_in"]
WEIGHTS = ["rel_bias", "w_ada", "b_ada", "w_in", "b_in", "attn_sinks", "gmlp_ln_g", "gmlp_ln_b", "gmlp_w_s",
           "gmlp_b_s", "attn_out_g", "gmlp_out_g", "w_out", "ln1_g", "ln1_b", "w_gate_up", "w_down", "ln2_g", "ln2_b"]


def _seg_rows(nelem):
    return -(-nelem // 1024) * 8


def _pack(named, names):
    parts = []
    for name in names:
        flat = named[name].reshape(-1).astype(F32)
        rows = _seg_rows(flat.shape[0])
        parts.append(jnp.pad(flat, (0, rows * 128 - flat.shape[0])).reshape(rows, 128))
    return jnp.concatenate(parts, axis=0)


def _unpack(packed, shapes, names):
    out, r0 = {}, 0
    for name in names:
        nelem = math.prod(shapes[name])
        rows = _seg_rows(nelem)
        out[name] = packed[r0:r0 + rows].reshape(-1)[:nelem].reshape(shapes[name])
        r0 += rows
    return out


def _t5_bucket_map():
    qi = jnp.arange(BLK)[:, None]
    si = jnp.arange(2 * BLK)[None, :]
    n = jnp.maximum(qi + BLK - si, 0)
    max_exact = N_BUCKETS // 2
    nf = jnp.maximum(n, max_exact).astype(F32)
    large = max_exact + (jnp.log(nf / max_exact) / math.log(MAX_DISTANCE / max_exact)
                         * (N_BUCKETS - max_exact)).astype(jnp.int32)
    large = jnp.minimum(large, N_BUCKETS - 1)
    return jnp.where(n < max_exact, n, large).astype(jnp.int32)


def kernel(x, c, rel_bias, w_ada, b_ada, w_in, b_in, attn_sinks, gmlp_ln_g, gmlp_ln_b, gmlp_w_s, gmlp_b_s, attn_out_g, gmlp_out_g, w_out, ln1_g, ln1_b, w_gate_up, w_down, ln2_g, ln2_b, loss_target, m_rel_bias, m_w_ada, m_b_ada, m_w_in, m_b_in, m_attn_sinks, m_gmlp_ln_g, m_gmlp_ln_b, m_gmlp_w_s, m_gmlp_b_s, m_attn_out_g, m_gmlp_out_g, m_w_out, m_ln1_g, m_ln1_b, m_w_gate_up, m_w_down, m_ln2_g, m_ln2_b, v_rel_bias, v_w_ada, v_b_ada, v_w_in, v_b_in, v_attn_sinks, v_gmlp_ln_g, v_gmlp_ln_b, v_gmlp_w_s, v_gmlp_b_s, v_attn_out_g, v_gmlp_out_g, v_w_out, v_ln1_g, v_ln1_b, v_w_gate_up, v_w_down, v_ln2_g, v_ln2_b):
    wts = dict(rel_bias=rel_bias, w_ada=w_ada, b_ada=b_ada, w_in=w_in, b_in=b_in, attn_sinks=attn_sinks,
               gmlp_ln_g=gmlp_ln_g, gmlp_ln_b=gmlp_ln_b, gmlp_w_s=gmlp_w_s, gmlp_b_s=gmlp_b_s,
               attn_out_g=attn_out_g, gmlp_out_g=gmlp_out_g, w_out=w_out, ln1_g=ln1_g, ln1_b=ln1_b,
               w_gate_up=w_gate_up, w_down=w_down, ln2_g=ln2_g, ln2_b=ln2_b)
    mom_m = dict(rel_bias=m_rel_bias, w_ada=m_w_ada, b_ada=m_b_ada, w_in=m_w_in, b_in=m_b_in,
                 attn_sinks=m_attn_sinks, gmlp_ln_g=m_gmlp_ln_g, gmlp_ln_b=m_gmlp_ln_b, gmlp_w_s=m_gmlp_w_s,
                 gmlp_b_s=m_gmlp_b_s, attn_out_g=m_attn_out_g, gmlp_out_g=m_gmlp_out_g, w_out=m_w_out,
                 ln1_g=m_ln1_g, ln1_b=m_ln1_b, w_gate_up=m_w_gate_up, w_down=m_w_down, ln2_g=m_ln2_g,
                 ln2_b=m_ln2_b)
    mom_v = dict(rel_bias=v_rel_bias, w_ada=v_w_ada, b_ada=v_b_ada, w_in=v_w_in, b_in=v_b_in,
                 attn_sinks=v_attn_sinks, gmlp_ln_g=v_gmlp_ln_g, gmlp_ln_b=v_gmlp_ln_b, gmlp_w_s=v_gmlp_w_s,
                 gmlp_b_s=v_gmlp_b_s, attn_out_g=v_attn_out_g, gmlp_out_g=v_gmlp_out_g, w_out=v_w_out,
                 ln1_g=v_ln1_g, ln1_b=v_ln1_b, w_gate_up=v_w_gate_up, w_down=v_w_down, ln2_g=v_ln2_g,
                 ln2_b=v_ln2_b)

    t = x.shape[1]
    tm = min(512, t)
    tn_ff = D_FF // 2
    tk_tok = min(1024, t)
    me = 4 * lax.axis_index("x") + 2 * lax.axis_index("y") + lax.axis_index("c")
    xs = x[0]
    target = loss_target[0]

    c_g, w_in_g = _exchange("gather_in", [jnp.broadcast_to(c, (8, D_MODEL)), w_in[0].T.astype(BF16)],
                            ("gather", "gather2"))
    c_all = c_g[:, 0, :]
    w_in_t = w_in_g.reshape(IN_W, D_MODEL)

    ncol = w_ada.shape[2]
    b_cols = lax.dynamic_slice(b_ada, (0, me * ncol), (1, ncol))
    mod_part = _mod_partial(c_all, w_ada[0], b_cols)
    (mod_g,) = _exchange("gather_mod", [mod_part], ("gather",))
    mod = lax.dynamic_slice(mod_g, (0, me, 0), (N_DEV, 1, ncol)).reshape(1, N_DEV * ncol)
    sh1, sc1, g1, sh2, sc2, g2 = [mod[:, i * D_MODEL:(i + 1) * D_MODEL] for i in range(6)]

    bucket = _t5_bucket_map()
    bias = _bias_table(rel_bias, bucket)
    causal = jnp.tril(jnp.ones((BLK, BLK), dtype=bool))
    ws = jnp.where(causal[None], gmlp_w_s[0], 0.0).astype(BF16)
    pair = lambda w: jnp.concatenate([w[0::2], w[1::2]], axis=2)
    ws2, wst2 = pair(ws), pair(jnp.swapaxes(ws, 1, 2))
    bfull = jnp.repeat(gmlp_b_s[0].T, GMLP_W // N_GROUPS, axis=1)
    sinks = attn_sinks[0]

    proj, h1 = _inproj(xs, sc1, sh1, w_in_t, b_in, tm)
    (mixed,), (w_out_g, w_gu_g) = _mix_fwd(
        proj, bias, sinks, gmlp_ln_g, gmlp_ln_b, ws2, bfull, attn_out_g, gmlp_out_g,
        comm=([w_out[0].astype(BF16), w_gate_up[0].T.astype(BF16)], ("gather2", "gather2")))
    w_out_f = w_out_g.reshape(D_MODEL, D_MODEL)
    w_gu_t = w_gu_g.reshape(2 * D_FF, D_MODEL)
    (y1, x1, h2), (w_down_g,) = _outproj(mixed, w_out_f, xs, g1, ln1_g, ln1_b, sc2, sh2, tm,
                                         comm=([w_down[0].astype(BF16)], ("gather2",)))
    w_down_f = w_down_g.reshape(D_FF, D_MODEL)
    dsu, sg, act, dz2, dy2, loss_p, d_ln2g, d_ln2b, d_g2 = _ffn_fwd(h2, w_gu_t, w_down_f, x1, target, g2, ln2_g, ln2_b,
                                                                    min(256, t))
    loss = lax.psum(0.5 / D_MODEL * jnp.sum(loss_p), ("x", "y", "c"))

    slots = lambda a: a.reshape(N_DEV, -1, D_MODEL)
    dw_down = _wgrad("wgrad_down", act, dy2, tn_ff, tk_tok)
    (dgate, dup, dz1, dy1, d_sc2, d_sh2, d_ln1g, d_ln1b, d_g1), (r_down,) = _ffn_bwd(
        dy2, w_down_f, dsu, sg, w_gu_t, x1, xs, y1, dz2, sc2, g1, ln1_g, min(256, t),
        comm=([slots(dw_down)], ("scatter",)))
    dw_gu_t = _wgrad("wgrad_gate_up", dgate, h2, tn_ff, tk_tok, a2=dup)
    dw_out = _wgrad("wgrad_out", mixed, dy1, D_MODEL, tk_tok)
    ((dproj, dkvn, dl_acc, dsink_acc, d_lng, d_lnb, d_ws, d_bs, d_aog, d_gog), (r_gu, r_out)) = _mix_bwd(
        proj, bias, sinks, gmlp_ln_g, gmlp_ln_b, ws2, wst2, bfull, attn_out_g, gmlp_out_g, dy1, w_out_f,
        comm=([slots(dw_gu_t), slots(dw_out)], ("scatter", "scatter")))
    d_relb = _bias_grad(dl_acc, bucket)

    rsum = lambda a: jnp.sum(a, axis=0)
    early_g = dict(
        rel_bias=d_relb[:, 0, :N_BUCKETS].T, attn_sinks=rsum(dsink_acc)[:N_HEADS],
        gmlp_ln_g=rsum(d_lng), gmlp_ln_b=rsum(d_lnb), gmlp_w_s=jnp.where(causal[None], d_ws, 0.0),
        gmlp_b_s=jnp.sum(d_bs.reshape(BLK, N_GROUPS, GMLP_W // N_GROUPS), axis=2).T,
        attn_out_g=rsum(d_aog), gmlp_out_g=rsum(d_gog), ln1_g=rsum(d_ln1g), ln1_b=rsum(d_ln1b),
        ln2_g=rsum(d_ln2g), ln2_b=rsum(d_ln2b))
    (grad_x, dproj_b, d_bin, d_sc1, d_sh1), _ = _din(dproj, dkvn, w_in_t, xs, dz1, sc1, tm, comm=None)
    dw_in_t, (early_all,) = _wgrad("wgrad_in", dproj_b, h1, IN_W, tk_tok,
                                   comm=([_pack(early_g, SMALL_EARLY)], ("gather2",)))
    dmod = jnp.concatenate([rsum(d_sh1), rsum(d_sc1), rsum(d_g1), rsum(d_sh2), rsum(d_sc2), rsum(d_g2)])
    late_all, r_in = _exchange("scatter_in", [_pack(dict(b_ada=dmod, b_in=rsum(d_bin)), SMALL_LATE), slots(dw_in_t)],
                               ("gather", "scatter"))

    small = [{}, {}, {}, {}]
    for label, names, parts in (("adam_small_early", SMALL_EARLY, early_all), ("adam_small_late", SMALL_LATE, late_all)):
        res = _adam_reduce(label, parts, _pack(wts, names), _pack(mom_m, names), _pack(mom_v, names), parts.shape[1])
        shapes = {k: wts[k].shape for k in names}
        for i in range(4):
            small[i].update(_unpack(res[i], shapes, names))

    dmod_all = late_all[:, :_seg_rows(6 * D_MODEL), :].reshape(N_DEV, 6 * D_MODEL)
    dmod_cols = lax.dynamic_slice(dmod_all, (0, me * ncol), (N_DEV, ncol))
    kpad = 128 - N_DEV
    ada = _adam_w_ada(jnp.pad(c_all.T, ((0, 0), (0, kpad))), jnp.pad(dmod_cols, ((0, kpad), (0, 0))),
                      w_ada[0], m_w_ada[0], v_w_ada[0])

    tr = lambda a: jnp.swapaxes(a, -1, -2)
    big = {}
    big["w_in"] = [tr(o)[None] for o in _adam_reduce("adam_w_in", r_in, w_in[0].T, m_w_in[0].T, v_w_in[0].T, 112)]
    big["w_out"] = [o[None] for o in _adam_reduce("adam_w_out", r_out, w_out[0], m_w_out[0], v_w_out[0], 128)]
    big["w_gate_up"] = [tr(o)[None] for o in _adam_reduce("adam_w_gu", r_gu, w_gate_up[0].T, m_w_gate_up[0].T,
                                                           v_w_gate_up[0].T, 352)]
    big["w_down"] = [o[None] for o in _adam_reduce("adam_w_down", r_down, w_down[0], m_w_down[0], v_w_down[0], 176)]
    big["w_ada"] = [o[None] for o in ada]

    outs = [[], [], [], []]
    for name in WEIGHTS:
        for i in range(4):
            outs[i].append(big[name][i] if name in big else small[i][name])
    return (loss, grad_x[None], *outs[0], *outs[1], *outs[2], *outs[3])
```

```python
import math

import jax
import jax.numpy as jnp
from jax import lax
from jax.experimental import pallas as pl
from jax.experimental.pallas import tpu as pltpu

F32 = jnp.float32
BF16 = jnp.bfloat16
MESH = pl.DeviceIdType.MESH

N_DEV = 8
D_MODEL = 1024
HEAD_DIM = 64
N_HEADS = 8
N_GROUPS = 8
ATTN_W = 512
KV_W = 128
GMLP_W = 512
IN_W = 1792
BLK = 128
N_BUCKETS = 32
MAX_DISTANCE = 128
D_FF = 2816
ALPHA = 2.0 ** 0.25
LN_EPS = 1e-5
NEG_INF = -1e30
ADAM_LR = 0.001
ADAM_B1 = 0.9
ADAM_B2 = 0.999
ADAM_EPS = 1e-08
ADAM_WD = 0.01
ADAM_STEP = 10
GELU_C0 = math.sqrt(2.0 / math.pi)
GELU_C1 = 0.044715

VMEM_LIMIT = 56 * 1024 * 1024


def _params(sem):
    return pltpu.CompilerParams(dimension_semantics=sem, vmem_limit_bytes=VMEM_LIMIT)


def _dot(a, b):
    return lax.dot_general(a, b, (((1,), (0,)), ((), ())), preferred_element_type=F32)


def _dot_nt(a, b):
    return lax.dot_general(a, b, (((1,), (1,)), ((), ())), preferred_element_type=F32)


def _dot_tn(a, b):
    return lax.dot_general(a, b, (((0,), (0,)), ((), ())), preferred_element_type=F32)


def _full(shape):
    nd = len(shape)
    return pl.BlockSpec(shape, lambda *_: (0,) * nd)


def _rowsum8(v):
    r, c = v.shape
    return jnp.sum(v.reshape(r // 8, 8, c), axis=0)


def _sigmoid(v):
    return 1.0 / (1.0 + jnp.exp(-v))


def _gelu_parts(v):
    v2 = v * v
    t = jnp.tanh(GELU_C0 * (v + GELU_C1 * v * v2))
    g = 0.5 * v * (1.0 + t)
    dg = 0.5 * (1.0 + t) + 0.5 * v * (1.0 - t * t) * (GELU_C0 * (1.0 + 3.0 * GELU_C1 * v2))
    return g, dg


def _ln_stats(z):
    mu = jnp.mean(z, axis=1, keepdims=True)
    zc = z - mu
    var = jnp.mean(zc * zc, axis=1, keepdims=True)
    rstd = lax.rsqrt(var + LN_EPS)
    return zc * rstd, rstd


def _ln_bwd(dxhat, xhat, rstd):
    m1 = jnp.mean(dxhat, axis=1, keepdims=True)
    m2 = jnp.mean(dxhat * xhat, axis=1, keepdims=True)
    return rstd * (dxhat - m1 - xhat * m2)


def _seg_mean64(v):
    r = v.shape[0]
    lo = lax.broadcasted_iota(jnp.int32, (r, 128), 1) < 64
    outs = []
    for j in range(v.shape[1] // 128):
        ch = v[:, 128 * j:128 * (j + 1)]
        s_lo = jnp.sum(jnp.where(lo, ch, 0.0), axis=1, keepdims=True)
        s_hi = jnp.sum(jnp.where(lo, 0.0, ch), axis=1, keepdims=True)
        outs.append(jnp.where(lo, s_lo, s_hi) * (1.0 / 64.0))
    return jnp.concatenate(outs, axis=1)


def _rms(a, g):
    r = lax.rsqrt(jnp.mean(a * a, axis=1, keepdims=True) + LN_EPS)
    return a * r * g, r


def _rms_bwd(dout, a, r, g):
    t = dout * g
    return r * t - a * (r * r * r) * jnp.mean(t * a, axis=1, keepdims=True)


PEER_ORDER = (1, 2, 4, 3, 5, 6, 7)


def _peer(j):
    x, y, c = lax.axis_index("x"), lax.axis_index("y"), lax.axis_index("c")
    px = 1 - x if j & 4 else x
    py = 1 - y if j & 2 else y
    pc = 1 - c if j & 1 else c
    return (px, py, pc), 4 * px + 2 * py + pc


SIBLING = 1
CHIP_FLIPS = (4, 2, 6)


def _exchange_phase(phase, ins, outs, modes, send_sems, recv_sems, loc_sems):
    me = 4 * lax.axis_index("x") + 2 * lax.axis_index("y") + lax.axis_index("c")
    for k, mode in enumerate(modes):
        def copy(i, src, slot, dev, k=k):
            return pltpu.make_async_remote_copy(src_ref=src, dst_ref=outs[k].at[slot], send_sem=send_sems.at[k, i],
                                                recv_sem=recv_sems.at[k, i], device_id=dev, device_id_type=MESH)

        src_me = ins[k].at[me] if mode == "scatter" else ins[k]
        local = pltpu.make_async_copy(src_me, outs[k].at[me], loc_sems.at[k])
        if mode == "gather2":
            sib_dev, sib_idx = _peer(SIBLING)
            chips = [_peer(j) for j in CHIP_FLIPS]
            far = [_peer(j | SIBLING)[1] for j in CHIP_FLIPS]
            if phase == "start":
                local.start()
                copy(0, ins[k], me, sib_dev).start()
                for i, (dev, _) in enumerate(chips):
                    copy(1 + i, ins[k], me, dev).start()
            elif phase == "mid":
                for i, (dev, idx) in enumerate(chips):
                    copy(1 + i, ins[k], idx, dev).wait_recv()
                    copy(4 + i, outs[k].at[idx], idx, sib_dev).start()
            else:
                copy(0, ins[k], sib_idx, sib_dev).wait_recv()
                for i, slot in enumerate(far):
                    copy(4 + i, ins[k], slot, sib_dev).wait_recv()
                copy(0, ins[k], me, sib_dev).wait_send()
                for i, (dev, idx) in enumerate(chips):
                    copy(1 + i, ins[k], me, dev).wait_send()
                    copy(4 + i, outs[k].at[idx], idx, sib_dev).wait_send()
                local.wait()
            continue
        peers = [_peer(j) for j in PEER_ORDER]
        if phase == "start":
            local.start()
            for i, (dev, idx) in enumerate(peers):
                copy(i, ins[k].at[idx] if mode == "scatter" else ins[k], me, dev).start()
        elif phase == "end":
            for i, (dev, idx) in enumerate(peers):
                copy(i, src_me, idx, dev).wait_recv()
            for i, (dev, idx) in enumerate(peers):
                copy(i, src_me, me, dev).wait_send()
            local.wait()


def _exchange_shapes(arrays, modes):
    return [jax.ShapeDtypeStruct((N_DEV,) + (a.shape[1:] if m == "scatter" else a.shape), a.dtype)
            for a, m in zip(arrays, modes)]


def _exchange_sems(n):
    return [pltpu.SemaphoreType.DMA((n, N_DEV - 1)), pltpu.SemaphoreType.DMA((n, N_DEV - 1)),
            pltpu.SemaphoreType.DMA((n,))]


def _exchange(name, arrays, modes):
    n = len(arrays)

    def body(*refs):
        for phase in ("start", "mid", "end"):
            _exchange_phase(phase, refs[:n], refs[n:2 * n], modes, *refs[2 * n:])

    any_spec = pl.BlockSpec(memory_space=pl.ANY)
    return pl.pallas_call(
        body, name=name, out_shape=_exchange_shapes(arrays, modes),
        in_specs=[any_spec] * n, out_specs=[any_spec] * n, scratch_shapes=_exchange_sems(n),
    )(*arrays)


def _call(body, *, name, grid, in_specs, out_specs, out_shape, args, sem, scratch_shapes=(), comm=None):
    if comm is None:
        outs = pl.pallas_call(body, name=name, grid=grid, in_specs=list(in_specs), out_specs=list(out_specs),
                              out_shape=list(out_shape), scratch_shapes=list(scratch_shapes),
                              compiler_params=_params(sem))(*args)
        return list(outs), []
    arrays, modes = comm
    n_in, n_out, nc, ns = len(in_specs), len(out_specs), len(arrays), len(scratch_shapes)
    n_steps = math.prod(grid)

    def hosted(*refs):
        ins, cins = refs[:n_in], refs[n_in:n_in + nc]
        outs, couts = refs[n_in + nc:n_in + nc + n_out], refs[n_in + nc + n_out:n_in + 2 * nc + n_out]
        scratch = refs[n_in + 2 * nc + n_out:]
        ex = (cins, couts, modes) + tuple(scratch[ns:])
        step = pl.program_id(0)
        for ax in range(1, len(grid)):
            step = step * grid[ax] + pl.program_id(ax)

        @pl.when(step == 0)
        def _():
            _exchange_phase("start", *ex)

        body(*ins, *outs, *scratch[:ns])

        if "gather2" in modes:
            @pl.when(step == (3 * n_steps) // 4)
            def _():
                _exchange_phase("mid", *ex)

        @pl.when(step == n_steps - 1)
        def _():
            _exchange_phase("end", *ex)

    any_spec = pl.BlockSpec(memory_space=pl.ANY)
    res = pl.pallas_call(
        hosted, name=name, grid=grid, in_specs=list(in_specs) + [any_spec] * nc,
        out_specs=list(out_specs) + [any_spec] * nc, out_shape=list(out_shape) + _exchange_shapes(arrays, modes),
        scratch_shapes=list(scratch_shapes) + _exchange_sems(nc),
        compiler_params=_params(tuple("arbitrary" for _ in grid)))(*args, *arrays)
    return list(res[:n_out]), list(res[n_out:])


def _mod_partial(c_all, w_ada, b_ada_cols):
    def body(c_ref, w_ref, b_ref, o_ref):
        cv = c_ref[...]
        s = (cv * _sigmoid(cv)).astype(BF16)
        o_ref[...] = _dot(s, w_ref[...].astype(BF16)) + b_ref[...]

    ncol = w_ada.shape[1]
    return pl.pallas_call(
        body, name="mod_partial", out_shape=jax.ShapeDtypeStruct((N_DEV, ncol), F32),
        in_specs=[_full(c_all.shape), _full(w_ada.shape), _full(b_ada_cols.shape)],
        out_specs=_full((N_DEV, ncol)), grid=(1,), compiler_params=_params(("arbitrary",)),
    )(c_all, w_ada, b_ada_cols)


def _bias_table(rel_bias, bucket):
    def body(rb_ref, bk_ref, o_ref):
        h = pl.program_id(0)
        bk = bk_ref[...]
        acc = jnp.zeros((BLK, 2 * BLK), F32)
        for b in range(N_BUCKETS):
            acc = jnp.where(bk == b, rb_ref[b, h], acc)
        dist = (lax.broadcasted_iota(jnp.int32, (BLK, 2 * BLK), 0) + BLK
                - lax.broadcasted_iota(jnp.int32, (BLK, 2 * BLK), 1))
        o_ref[0] = jnp.where((dist >= 0) & (dist < BLK), acc, NEG_INF)

    return pl.pallas_call(
        body, name="bias_table", out_shape=jax.ShapeDtypeStruct((N_HEADS, BLK, 2 * BLK), F32),
        in_specs=[pl.BlockSpec(memory_space=pltpu.SMEM), _full((BLK, 2 * BLK))],
        out_specs=pl.BlockSpec((1, BLK, 2 * BLK), lambda h: (h, 0, 0)), grid=(N_HEADS,),
        compiler_params=_params(("arbitrary",)),
    )(rel_bias, bucket)


def _bias_grad(dl_acc, bucket):
    def body(dl_ref, bk_ref, o_ref):
        bk = bk_ref[...]
        dl = dl_ref[0]
        lane = lax.broadcasted_iota(jnp.int32, (1, 128), 1)
        row = jnp.zeros((1, 128), F32)
        for b in range(N_BUCKETS):
            s = jnp.sum(jnp.sum(jnp.where(bk == b, dl, 0.0), axis=1, keepdims=True), axis=0, keepdims=True)
            row = jnp.where(lane == b, s, row)
        o_ref[0] = row

    return pl.pallas_call(
        body, name="bias_grad", out_shape=jax.ShapeDtypeStruct((N_HEADS, 1, 128), F32),
        in_specs=[pl.BlockSpec((1, BLK, 2 * BLK), lambda h: (h, 0, 0)), _full((BLK, 2 * BLK))],
        out_specs=pl.BlockSpec((1, 1, 128), lambda h: (h, 0, 0)), grid=(N_HEADS,),
        compiler_params=_params(("arbitrary",)),
    )(dl_acc, bucket)


def _inproj(x, sc1, sh1, w_in_t, b_in, tm):
    t, d = x.shape
    n = w_in_t.shape[0]

    def body(x_ref, sc_ref, sh_ref, w_ref, b_ref, proj_ref, h_ref):
        h = (x_ref[...] * (1.0 + sc_ref[...]) + sh_ref[...]).astype(BF16)
        h_ref[...] = h
        proj_ref[...] = _dot_nt(h, w_ref[...]) + b_ref[...]

    row = lambda w: pl.BlockSpec((tm, w), lambda i: (i, 0))
    return pl.pallas_call(
        body, name="inproj", grid=(t // tm,),
        out_shape=[jax.ShapeDtypeStruct((t, n), F32), jax.ShapeDtypeStruct((t, d), BF16)],
        in_specs=[row(d), _full((1, d)), _full((1, d)), _full((n, d)), _full((1, n))],
        out_specs=[row(n), row(d)], compiler_params=_params(("parallel",)),
    )(x, sc1, sh1, w_in_t, b_in)


def _half_masks():
    lo_q = lax.broadcasted_iota(jnp.int32, (BLK, 128), 1) < 64
    lo_k = lax.broadcasted_iota(jnp.int32, (2 * BLK, 128), 1) < 64
    return lo_q, lo_k


def _head_place(h):
    return h // 2, h % 2, h // 4


def _attn_heads(q, kk, vv, bias_ref, sinks_ref, n):
    lo_q, lo_k = _half_masks()
    kkb, kksb = kk.astype(BF16), pltpu.roll(kk, 64, 1).astype(BF16)
    vvb, vvsb = vv.astype(BF16), pltpu.roll(vv, 64, 1).astype(BF16)
    n0mask = (n == 0) & (lax.broadcasted_iota(jnp.int32, (BLK, 2 * BLK), 1) < BLK)
    chunks, probs = [], []
    for j in range(4):
        qc = q[:, 128 * j:128 * (j + 1)]
        acc = jnp.zeros((BLK, 128), F32)
        for pos in range(2):
            h = 2 * j + pos
            direct = (h // 4) == pos
            mq = lo_q if pos == 0 else jnp.logical_not(lo_q)
            mk = lo_k if pos == 0 else jnp.logical_not(lo_k)
            qm = jnp.where(mq, qc, 0.0).astype(BF16)
            logit = _dot_nt(qm, kkb if direct else kksb) * (HEAD_DIM ** -0.5) + bias_ref[h]
            logit = jnp.where(n0mask, NEG_INF, logit)
            sk = sinks_ref[h]
            m = jnp.maximum(jnp.max(logit, axis=1, keepdims=True), sk)
            e = jnp.exp(logit - m)
            es = jnp.exp(sk - m)
            den = jnp.sum(e, axis=1, keepdims=True) + es
            p = e / den
            vm = jnp.where(mk, vvb if direct else vvsb, jnp.zeros_like(vvb))
            acc = acc + _dot(p.astype(BF16), vm)
            probs.append((p, es / den))
        chunks.append(acc)
    return jnp.concatenate(chunks, axis=1), probs


def _gmlp_block(gu, gv, lng, lnb, ws_ref, bfull):
    lo_q, _ = _half_masks()
    u, du = _gelu_parts(gu)
    a, da = _gelu_parts(gv)
    mu = _seg_mean64(a)
    ac = a - mu
    rstd = lax.rsqrt(_seg_mean64(ac * ac) + LN_EPS)
    vhat = ac * rstd
    vn = vhat * lng + lnb
    chunks = []
    for j in range(4):
        vc = vn[:, 128 * j:128 * (j + 1)]
        acc = jnp.zeros((BLK, 128), F32)
        for pos in range(2):
            mq = lo_q if pos == 0 else jnp.logical_not(lo_q)
            acc = acc + _dot(ws_ref[2 * j + pos], jnp.where(mq, vc, 0.0).astype(BF16))
        chunks.append(acc)
    ms = jnp.concatenate(chunks, axis=1) + bfull
    return u * ms, (u, du, da, vhat, rstd, vn, ms)


def _mix_in_specs(nb):
    return [pl.BlockSpec((BLK, IN_W), lambda n: (n, 0)),
            pl.BlockSpec((BLK, 2 * KV_W), lambda n: (jnp.maximum(n - 1, 0), ATTN_W // (2 * KV_W))),
            _full((N_HEADS, BLK, 2 * BLK)),
            pl.BlockSpec(memory_space=pltpu.SMEM),
            _full((1, GMLP_W)), _full((1, GMLP_W)),
            _full((N_GROUPS, BLK, BLK)), _full((BLK, GMLP_W)),
            _full((1, ATTN_W)), _full((1, GMLP_W))]


def _split_proj(proj_ref, kvp_ref):
    q = proj_ref[:, 0:ATTN_W]
    k = proj_ref[:, ATTN_W:ATTN_W + KV_W]
    v = proj_ref[:, ATTN_W + KV_W:ATTN_W + 2 * KV_W]
    gu = proj_ref[:, ATTN_W + 2 * KV_W:ATTN_W + 2 * KV_W + GMLP_W]
    gv = proj_ref[:, ATTN_W + 2 * KV_W + GMLP_W:IN_W]
    kk = jnp.concatenate([kvp_ref[:, 0:KV_W], k], axis=0)
    vv = jnp.concatenate([kvp_ref[:, KV_W:2 * KV_W], v], axis=0)
    return q, kk, vv, gu, gv


def _mix_fwd(proj, bias, sinks, lng, lnb, ws, bfull, aog, gog, comm):
    t = proj.shape[0]
    nb = t // BLK

    def body(proj_ref, kvp_ref, bias_ref, sinks_ref, lng_ref, lnb_ref, ws_ref, bfull_ref, aog_ref, gog_ref, out_ref):
        n = pl.program_id(0)
        q, kk, vv, gu, gv = _split_proj(proj_ref, kvp_ref)
        attn, _ = _attn_heads(q, kk, vv, bias_ref, sinks_ref, n)
        gm, _ = _gmlp_block(gu, gv, lng_ref[...], lnb_ref[...], ws_ref, bfull_ref[...])
        out_ref[:, 0:ATTN_W] = _rms(attn, aog_ref[...])[0].astype(BF16)
        out_ref[:, ATTN_W:ATTN_W + GMLP_W] = _rms(gm, gog_ref[...])[0].astype(BF16)

    return _call(
        body, name="mix_fwd", grid=(nb,), out_shape=[jax.ShapeDtypeStruct((t, D_MODEL), BF16)],
        in_specs=_mix_in_specs(nb), out_specs=[pl.BlockSpec((BLK, D_MODEL), lambda n: (n, 0))],
        sem=("parallel",), comm=comm, args=(proj, proj, bias, sinks, lng, lnb, ws, bfull, aog, gog))


def _mix_bwd(proj, bias, sinks, lng, lnb, ws, ws_t, bfull, aog, gog, dy, w_out, comm):
    t = proj.shape[0]
    nb = t // BLK

    def body(proj_ref, kvp_ref, bias_ref, sinks_ref, lng_ref, lnb_ref, ws_ref, bfull_ref, aog_ref, gog_ref,
             wst_ref, dy_ref, wout_ref,
             dproj_ref, dkvn_ref, dl_ref, dsink_ref, dlng_ref, dlnb_ref, dws_ref, dbs_ref, daog_ref, dgog_ref):
        n = pl.program_id(0)

        @pl.when(n == 0)
        def _():
            for r in (dl_ref, dsink_ref, dlng_ref, dlnb_ref, dws_ref, dbs_ref, daog_ref, dgog_ref):
                r[...] = jnp.zeros_like(r)

        lo_q, lo_k = _half_masks()
        q, kk, vv, gu, gv = _split_proj(proj_ref, kvp_ref)
        dmix = _dot_nt(dy_ref[...], wout_ref[...])
        dma, dmg = dmix[:, 0:ATTN_W], dmix[:, ATTN_W:ATTN_W + GMLP_W]

        attn, probs = _attn_heads(q, kk, vv, bias_ref, sinks_ref, n)
        aog = aog_ref[...]
        _, r_a = _rms(attn, aog)
        daog_ref[...] += _rowsum8(dma * attn * r_a)
        dattn = _rms_bwd(dma, attn, r_a, aog)

        kkb, kksb = kk.astype(BF16), pltpu.roll(kk, 64, 1).astype(BF16)
        vvb, vvsb = vv.astype(BF16), pltpu.roll(vv, 64, 1).astype(BF16)
        lane = lax.broadcasted_iota(jnp.int32, (BLK, 128), 1)
        dk_d = jnp.zeros((2 * BLK, 128), F32)
        dk_s = jnp.zeros((2 * BLK, 128), F32)
        dv_d = jnp.zeros((2 * BLK, 128), F32)
        dv_s = jnp.zeros((2 * BLK, 128), F32)
        dsink = jnp.zeros((BLK, 128), F32)
        dq_chunks = []
        for j in range(4):
            qc = q[:, 128 * j:128 * (j + 1)]
            doc = dattn[:, 128 * j:128 * (j + 1)]
            dq = jnp.zeros((BLK, 128), F32)
            for pos in range(2):
                h = 2 * j + pos
                direct = (h // 4) == pos
                mq = lo_q if pos == 0 else jnp.logical_not(lo_q)
                mk = lo_k if pos == 0 else jnp.logical_not(lo_k)
                p, psink = probs[h]
                qm = jnp.where(mq, qc, 0.0).astype(BF16)
                dom = jnp.where(mq, doc, 0.0).astype(BF16)
                dp = _dot_nt(dom, vvb if direct else vvsb)
                rs = jnp.sum(p * dp, axis=1, keepdims=True)
                dl = p * (dp - rs)
                dl_ref[h] += dl
                dsink = dsink + jnp.where(lane == h, -psink * rs, 0.0)
                dls = (dl * (HEAD_DIM ** -0.5)).astype(BF16)
                km = jnp.where(mk, kkb if direct else kksb, jnp.zeros_like(kkb))
                dq = dq + _dot(dls, km)
                dk_h = _dot_tn(dls, qm)
                dv_h = _dot_tn(p.astype(BF16), dom)
                if direct:
                    dk_d, dv_d = dk_d + dk_h, dv_d + dv_h
                else:
                    dk_s, dv_s = dk_s + dk_h, dv_s + dv_h
            dq_chunks.append(dq)
        dsink_ref[...] += dsink
        dk = dk_d + pltpu.roll(dk_s, 64, 1)
        dv = dv_d + pltpu.roll(dv_s, 64, 1)
        for j in range(4):
            dproj_ref[:, 128 * j:128 * (j + 1)] = dq_chunks[j]
        dproj_ref[:, ATTN_W:ATTN_W + KV_W] = dk[BLK:2 * BLK]
        dproj_ref[:, ATTN_W + KV_W:ATTN_W + 2 * KV_W] = dv[BLK:2 * BLK]
        dkvn_ref[:, 0:KV_W] = dk[0:BLK]
        dkvn_ref[:, KV_W:2 * KV_W] = dv[0:BLK]

        lng = lng_ref[...]
        gog = gog_ref[...]
        gm, (u, du, da, vhat, rstd, vn, ms) = _gmlp_block(gu, gv, lng, lnb_ref[...], ws_ref, bfull_ref[...])
        _, r_g = _rms(gm, gog)
        dgog_ref[...] += _rowsum8(dmg * gm * r_g)
        dgm = _rms_bwd(dmg, gm, r_g, gog)
        dproj_ref[:, ATTN_W + 2 * KV_W:ATTN_W + 2 * KV_W + GMLP_W] = dgm * ms * du
        dms = dgm * u
        dbs_ref[...] += dms
        dvn_chunks = []
        for j in range(4):
            dmc = dms[:, 128 * j:128 * (j + 1)]
            vcb = vn[:, 128 * j:128 * (j + 1)].astype(BF16)
            acc = jnp.zeros((BLK, 128), F32)
            for pos in range(2):
                g = 2 * j + pos
                mq = lo_q if pos == 0 else jnp.logical_not(lo_q)
                dm = jnp.where(mq, dmc, 0.0).astype(BF16)
                dws_ref[g] += _dot_nt(dm, vcb)
                acc = acc + _dot(wst_ref[g], dm)
            dvn_chunks.append(acc)
        dvn = jnp.concatenate(dvn_chunks, axis=1)
        dlng_ref[...] += _rowsum8(dvn * vhat)
        dlnb_ref[...] += _rowsum8(dvn)
        dvh = dvn * lng
        dact = rstd * (dvh - _seg_mean64(dvh) - vhat * _seg_mean64(dvh * vhat))
        dproj_ref[:, ATTN_W + 2 * KV_W + GMLP_W:IN_W] = dact * da

    acc8 = lambda w: jax.ShapeDtypeStruct((8, w), F32)
    out_shape = [jax.ShapeDtypeStruct((t, IN_W), F32), jax.ShapeDtypeStruct((t, 2 * KV_W), F32),
                 jax.ShapeDtypeStruct((N_HEADS, BLK, 2 * BLK), F32), jax.ShapeDtypeStruct((BLK, 128), F32),
                 acc8(GMLP_W), acc8(GMLP_W), jax.ShapeDtypeStruct((N_GROUPS, BLK, BLK), F32),
                 jax.ShapeDtypeStruct((BLK, GMLP_W), F32), acc8(ATTN_W), acc8(GMLP_W)]
    out_specs = [pl.BlockSpec((BLK, IN_W), lambda n: (n, 0)),
                 pl.BlockSpec((BLK, 2 * KV_W), lambda n: ((n + nb - 1) % nb, 0)),
                 _full((N_HEADS, BLK, 2 * BLK)), _full((BLK, 128)), _full((8, GMLP_W)), _full((8, GMLP_W)),
                 _full((N_GROUPS, BLK, BLK)), _full((BLK, GMLP_W)), _full((8, ATTN_W)), _full((8, GMLP_W))]
    in_specs = _mix_in_specs(nb) + [_full((N_GROUPS, BLK, BLK)),
                                    pl.BlockSpec((BLK, D_MODEL), lambda n: (n, 0)),
                                    _full((D_MODEL, D_MODEL))]
    return _call(
        body, name="mix_bwd", grid=(nb,), out_shape=out_shape, in_specs=in_specs, out_specs=out_specs,
        sem=("arbitrary",), comm=comm, args=(proj, proj, bias, sinks, lng, lnb, ws, bfull, aog, gog, ws_t, dy, w_out))


HALF = 64
ROWS = 32


def _lane_lo(rows):
    return lax.broadcasted_iota(jnp.int32, (rows, 128), 1) < 64


def _mix_stage_kv(proj_ref, kvp_ref, s):
    lo = _lane_lo(2 * BLK)
    for name, col in (("k", ATTN_W), ("v", ATTN_W + KV_W)):
        cur = jnp.concatenate([kvp_ref[:, col - ATTN_W:col - ATTN_W + KV_W], proj_ref[:, col:col + KV_W]], axis=0)
        plain, swapped = cur.astype(BF16), pltpu.roll(cur, 64, 1).astype(BF16)
        zero = jnp.zeros_like(plain)
        for g in range(2):
            dup = jnp.where(lo, plain, swapped) if g == 0 else jnp.where(lo, swapped, plain)
            s[name + "d"][g] = dup
            s[name + "m"][g] = jnp.concatenate([jnp.where(lo, dup, zero), jnp.where(lo, zero, dup)], axis=0)


def _group_rows(ref, g):
    return ref[4 * g:4 * g + 4].reshape(4 * BLK, ref.shape[2])


def _pair_rows(ref, g):
    return jnp.concatenate([jnp.concatenate([ref[4 * g + 2 * c], ref[4 * g + 2 * c + 1]], axis=1) for c in range(2)],
                           axis=0)


def _mask_heads(src_ref, dst_ref):
    lo = _lane_lo(BLK)
    for j in range(4):
        chunk = src_ref[:, 128 * j:128 * (j + 1)]
        dst_ref[2 * j] = jnp.where(lo, chunk, 0.0).astype(BF16)
        dst_ref[2 * j + 1] = jnp.where(lo, 0.0, chunk).astype(BF16)


def _mix_stage_attn(proj_ref, bias_ref, sinks_ref, n, s, keep):
    _mask_heads(proj_ref, s["qm"])
    for g in range(2):
        s["lg"][g] = _dot_nt(_group_rows(s["qm"], g), s["kd"][g])
    n0mask = (n == 0) & (lax.broadcasted_iota(jnp.int32, (HALF, 2 * BLK), 1) < BLK)
    for h in range(N_HEADS):
        sk = sinks_ref[h]
        for hf in range(BLK // HALF):
            rows = slice(HALF * hf, HALF * (hf + 1))
            grows = slice(BLK * (h % 4) + HALF * hf, BLK * (h % 4) + HALF * (hf + 1))
            logit = s["lg"][h // 4, grows, :] * (HEAD_DIM ** -0.5) + bias_ref[h, rows, :]
            logit = jnp.where(n0mask, NEG_INF, logit)
            m = jnp.maximum(jnp.max(logit, axis=1, keepdims=True), sk)
            e = jnp.exp(logit - m)
            es = jnp.exp(sk - m)
            inv = 1.0 / (jnp.sum(e, axis=1, keepdims=True) + es)
            p = e * inv
            s["pb"][h, rows, :] = p.astype(BF16)
            if keep:
                s["p"][h, rows, :] = p
                s["psink"][h, rows, :] = es * inv
    for g in range(2):
        out = _dot(_pair_rows(s["pb"], g), s["vm"][g])
        s["attn"][:, 256 * g:256 * g + 128] = out[0:BLK]
        s["attn"][:, 256 * g + 128:256 * g + 256] = out[BLK:2 * BLK]


def _mix_stage_gmlp_pre(proj_ref, lng, lnb, s, keep):
    c0 = ATTN_W + 2 * KV_W
    for r0 in range(0, BLK, ROWS):
        rows = slice(r0, r0 + ROWS)
        u, du = _gelu_parts(proj_ref[rows, c0:c0 + GMLP_W])
        a, da = _gelu_parts(proj_ref[rows, c0 + GMLP_W:c0 + 2 * GMLP_W])
        ac = a - _seg_mean64(a)
        rstd = lax.rsqrt(_seg_mean64(ac * ac) + LN_EPS)
        vhat = ac * rstd
        s["u"][rows, :] = u
        s["vnb"][rows, :] = (vhat * lng + lnb).astype(BF16)
        if keep:
            s["du"][rows, :] = du
            s["da"][rows, :] = da
            s["vhat"][rows, :] = vhat
            s["rstd"][rows, :] = rstd


def _stack_halves(chunk):
    lo = _lane_lo(BLK)
    zero = jnp.zeros_like(chunk)
    return jnp.concatenate([jnp.where(lo, chunk, zero), jnp.where(lo, zero, chunk)], axis=0)


def _mix_stage_gmlp_mix(ws2_ref, bfull_ref, s):
    for j in range(4):
        cols = slice(128 * j, 128 * (j + 1))
        s["ms"][:, cols] = _dot(ws2_ref[j], _stack_halves(s["vnb"][:, cols])) + bfull_ref[:, cols]


def _mix_scratch(keep):
    f32 = lambda *shape: pltpu.VMEM(shape, F32)
    b16 = lambda *shape: pltpu.VMEM(shape, BF16)
    names = dict(kd=b16(2, 2 * BLK, 128), vd=b16(2, 2 * BLK, 128), km=b16(2, 4 * BLK, 128), vm=b16(2, 4 * BLK, 128),
                 qm=b16(N_HEADS, BLK, 128), lg=f32(2, 4 * BLK, 2 * BLK), pb=b16(N_HEADS, BLK, 2 * BLK),
                 attn=f32(BLK, ATTN_W), u=f32(BLK, GMLP_W), vnb=b16(BLK, GMLP_W), ms=f32(BLK, GMLP_W))
    if keep:
        names.update(dom=b16(N_HEADS, BLK, 128), p=f32(N_HEADS, BLK, 2 * BLK),
                     dls=b16(N_HEADS, BLK, 2 * BLK), psink=f32(N_HEADS, BLK, 1),
                     dattn=f32(BLK, ATTN_W), dmix=f32(BLK, D_MODEL), du=f32(BLK, GMLP_W), da=f32(BLK, GMLP_W),
                     vhat=f32(BLK, GMLP_W), rstd=f32(BLK, GMLP_W), dmsb=b16(BLK, GMLP_W), dvn=f32(BLK, GMLP_W))
    return list(names), list(names.values())


def _mix_specs():
    return [pl.BlockSpec((BLK, IN_W), lambda n: (n, 0)),
            pl.BlockSpec((BLK, 2 * KV_W), lambda n: (jnp.maximum(n - 1, 0), ATTN_W // (2 * KV_W))),
            _full((N_HEADS, BLK, 2 * BLK)),
            pl.BlockSpec(memory_space=pltpu.SMEM),
            _full((1, GMLP_W)), _full((1, GMLP_W)),
            _full((N_GROUPS // 2, BLK, 2 * BLK)), _full((BLK, GMLP_W)),
            _full((1, ATTN_W)), _full((1, GMLP_W))]


def _mix_fwd(proj, bias, sinks, lng, lnb, ws2, bfull, aog, gog, comm):
    t = proj.shape[0]
    names, shapes = _mix_scratch(False)

    def body(proj_ref, kvp_ref, bias_ref, sinks_ref, lng_ref, lnb_ref, ws2_ref, bfull_ref, aog_ref, gog_ref,
             out_ref, *scratch):
        s = dict(zip(names, scratch))
        n = pl.program_id(0)
        _mix_stage_kv(proj_ref, kvp_ref, s)
        _mix_stage_attn(proj_ref, bias_ref, sinks_ref, n, s, False)
        _mix_stage_gmlp_pre(proj_ref, lng_ref[...], lnb_ref[...], s, False)
        _mix_stage_gmlp_mix(ws2_ref, bfull_ref, s)
        for r0 in range(0, BLK, ROWS):
            rows = slice(r0, r0 + ROWS)
            out_ref[rows, 0:ATTN_W] = _rms(s["attn"][rows, :], aog_ref[...])[0].astype(BF16)
            out_ref[rows, ATTN_W:ATTN_W + GMLP_W] = _rms(s["u"][rows, :] * s["ms"][rows, :], gog_ref[...])[0].astype(BF16)

    return _call(
        body, name="mix_fwd", grid=(t // BLK,), out_shape=[jax.ShapeDtypeStruct((t, D_MODEL), BF16)],
        in_specs=_mix_specs(), out_specs=[pl.BlockSpec((BLK, D_MODEL), lambda n: (n, 0))], scratch_shapes=shapes,
        sem=("parallel",), comm=comm, args=(proj, proj, bias, sinks, lng, lnb, ws2, bfull, aog, gog))


def _mix_bwd(proj, bias, sinks, lng, lnb, ws2, wst2, bfull, aog, gog, dy, w_out, comm):
    t = proj.shape[0]
    nb = t // BLK
    names, shapes = _mix_scratch(True)
    c_gu = ATTN_W + 2 * KV_W

    def body(proj_ref, kvp_ref, bias_ref, sinks_ref, lng_ref, lnb_ref, ws2_ref, bfull_ref, aog_ref, gog_ref,
             wst2_ref, dy_ref, wout_ref,
             dproj_ref, dkvn_ref, dl_ref, dsink_ref, dlng_ref, dlnb_ref, dws_ref, dbs_ref, daog_ref, dgog_ref,
             *scratch):
        s = dict(zip(names, scratch))
        n = pl.program_id(0)

        @pl.when(n == 0)
        def _():
            for r in (dl_ref, dsink_ref, dlng_ref, dlnb_ref, dws_ref, dbs_ref, daog_ref, dgog_ref):
                r[...] = jnp.zeros_like(r)

        s["dmix"][...] = _dot_nt(dy_ref[...], wout_ref[...])
        _mix_stage_kv(proj_ref, kvp_ref, s)
        _mix_stage_attn(proj_ref, bias_ref, sinks_ref, n, s, True)
        lng = lng_ref[...]
        _mix_stage_gmlp_pre(proj_ref, lng, lnb_ref[...], s, True)
        _mix_stage_gmlp_mix(ws2_ref, bfull_ref, s)

        aog, gog = aog_ref[...], gog_ref[...]
        for r0 in range(0, BLK, ROWS):
            rows = slice(r0, r0 + ROWS)
            attn, dma = s["attn"][rows, :], s["dmix"][rows, 0:ATTN_W]
            _, r_a = _rms(attn, aog)
            daog_ref[...] += _rowsum8(dma * attn * r_a)
            s["dattn"][rows, :] = _rms_bwd(dma, attn, r_a, aog)
            u, ms, dmg = s["u"][rows, :], s["ms"][rows, :], s["dmix"][rows, ATTN_W:ATTN_W + GMLP_W]
            gm = u * ms
            _, r_g = _rms(gm, gog)
            dgog_ref[...] += _rowsum8(dmg * gm * r_g)
            dgm = _rms_bwd(dmg, gm, r_g, gog)
            dproj_ref[rows, c_gu:c_gu + GMLP_W] = dgm * ms * s["du"][rows, :]
            dms = dgm * u
            dbs_ref[rows, :] += dms
            s["dmsb"][rows, :] = dms.astype(BF16)

        _mask_heads(s["dattn"], s["dom"])
        for g in range(2):
            s["lg"][g] = _dot_nt(_group_rows(s["dom"], g), s["vd"][g])
        lane = lax.broadcasted_iota(jnp.int32, (HALF, 128), 1)
        for hf in range(BLK // HALF):
            rows = slice(HALF * hf, HALF * (hf + 1))
            dsink = jnp.zeros((HALF, 128), F32)
            for h in range(N_HEADS):
                grows = slice(BLK * (h % 4) + HALF * hf, BLK * (h % 4) + HALF * (hf + 1))
                dp = s["lg"][h // 4, grows, :]
                p = s["p"][h, rows, :]
                rs = jnp.sum(p * dp, axis=1, keepdims=True)
                dl = p * (dp - rs)
                dl_ref[h, rows, :] += dl
                dsink = dsink + jnp.where(lane == h, -s["psink"][h, rows, :] * rs, 0.0)
                s["dls"][h, rows, :] = (dl * (HEAD_DIM ** -0.5)).astype(BF16)
            dsink_ref[rows, :] += dsink
        for g in range(2):
            dq = _dot(_pair_rows(s["dls"], g), s["km"][g])
            dproj_ref[:, 256 * g:256 * g + 128] = dq[0:BLK]
            dproj_ref[:, 256 * g + 128:256 * g + 256] = dq[BLK:2 * BLK]
        lo_k = _lane_lo(2 * BLK)
        for col, lhs, rhs in ((0, "dls", "qm"), (KV_W, "pb", "dom")):
            raw = [_dot_tn(_group_rows(s[lhs], g), _group_rows(s[rhs], g)) for g in range(2)]
            both = [r + pltpu.roll(r, 64, 1) for r in raw]
            dkv = jnp.where(lo_k, both[0], both[1])
            dproj_ref[:, ATTN_W + col:ATTN_W + col + KV_W] = dkv[BLK:2 * BLK]
            dkvn_ref[:, col:col + KV_W] = dkv[0:BLK]

        for j in range(4):
            cols = slice(128 * j, 128 * (j + 1))
            dm2 = _stack_halves(s["dmsb"][:, cols])
            vnb = s["vnb"][:, cols]
            dws2 = _dot_nt(dm2, vnb)
            dws_ref[2 * j] += dws2[0:BLK]
            dws_ref[2 * j + 1] += dws2[BLK:2 * BLK]
            s["dvn"][:, cols] = _dot(wst2_ref[j], dm2)
        for r0 in range(0, BLK, ROWS):
            rows = slice(r0, r0 + ROWS)
            dvn, vhat = s["dvn"][rows, :], s["vhat"][rows, :]
            dlng_ref[...] += _rowsum8(dvn * vhat)
            dlnb_ref[...] += _rowsum8(dvn)
            dvh = dvn * lng
            dact = s["rstd"][rows, :] * (dvh - _seg_mean64(dvh) - vhat * _seg_mean64(dvh * vhat))
            dproj_ref[rows, c_gu + GMLP_W:IN_W] = dact * s["da"][rows, :]

    acc8 = lambda w: jax.ShapeDtypeStruct((8, w), F32)
    out_shape = [jax.ShapeDtypeStruct((t, IN_W), F32), jax.ShapeDtypeStruct((t, 2 * KV_W), F32),
                 jax.ShapeDtypeStruct((N_HEADS, BLK, 2 * BLK), F32), jax.ShapeDtypeStruct((BLK, 128), F32),
                 acc8(GMLP_W), acc8(GMLP_W), jax.ShapeDtypeStruct((N_GROUPS, BLK, BLK), F32),
                 jax.ShapeDtypeStruct((BLK, GMLP_W), F32), acc8(ATTN_W), acc8(GMLP_W)]
    out_specs = [pl.BlockSpec((BLK, IN_W), lambda n: (n, 0)),
                 pl.BlockSpec((BLK, 2 * KV_W), lambda n: ((n + nb - 1) % nb, 0)),
                 _full((N_HEADS, BLK, 2 * BLK)), _full((BLK, 128)), _full((8, GMLP_W)), _full((8, GMLP_W)),
                 _full((N_GROUPS, BLK, BLK)), _full((BLK, GMLP_W)), _full((8, ATTN_W)), _full((8, GMLP_W))]
    in_specs = _mix_specs() + [_full((N_GROUPS // 2, BLK, 2 * BLK)),
                               pl.BlockSpec((BLK, D_MODEL), lambda n: (n, 0)),
                               _full((D_MODEL, D_MODEL))]
    return _call(
        body, name="mix_bwd", grid=(nb,), out_shape=out_shape, in_specs=in_specs, out_specs=out_specs,
        scratch_shapes=shapes, sem=("arbitrary",), comm=comm,
        args=(proj, proj, bias, sinks, lng, lnb, ws2, bfull, aog, gog, wst2, dy, w_out))


def _outproj(mixed, w_out, x, g1, ln1g, ln1b, sc2, sh2, tm, comm):
    t, d = x.shape

    def body(mx_ref, w_ref, x_ref, g1_ref, lg_ref, lb_ref, sc_ref, sh_ref, y_ref, x1_ref, h2_ref):
        y = _dot(mx_ref[...], w_ref[...])
        xhat, _ = _ln_stats(ALPHA * x_ref[...] + g1_ref[...] * y)
        x1 = xhat * lg_ref[...] + lb_ref[...]
        y_ref[...] = y
        x1_ref[...] = x1
        h2_ref[...] = (x1 * (1.0 + sc_ref[...]) + sh_ref[...]).astype(BF16)

    row = pl.BlockSpec((tm, d), lambda i: (i, 0))
    vec = _full((1, d))
    return _call(
        body, name="outproj", grid=(t // tm,),
        out_shape=[jax.ShapeDtypeStruct((t, d), F32), jax.ShapeDtypeStruct((t, d), F32),
                   jax.ShapeDtypeStruct((t, d), BF16)],
        in_specs=[row, _full((d, d)), row, vec, vec, vec, vec, vec], out_specs=[row, row, row],
        sem=("parallel",), comm=comm, args=(mixed, w_out, x, g1, ln1g, ln1b, sc2, sh2))


def _ffn_fwd(h2, w_gu_t, w_down, x1, target, g2, ln2g, ln2b, tm):
    t, d = x1.shape

    def body(h_ref, w_ref, wd_ref, x1_ref, tg_ref, g2_ref, lg_ref, lb_ref,
             dsu_ref, sg_ref, act_ref, dz_ref, dy_ref, loss_ref, dlg_ref, dlb_ref, dg2_ref):
        @pl.when(pl.program_id(0) == 0)
        def _():
            for r in (loss_ref, dlg_ref, dlb_ref, dg2_ref):
                r[...] = jnp.zeros_like(r)

        h = h_ref[...]
        g = _dot_nt(h, w_ref[0:D_FF])
        u = _dot_nt(h, w_ref[D_FF:2 * D_FF])
        s = _sigmoid(g)
        sg = g * s
        act = (sg * u).astype(BF16)
        dsu_ref[...] = (u * (s * (1.0 + g * (1.0 - s)))).astype(BF16)
        sg_ref[...] = sg.astype(BF16)
        act_ref[...] = act
        y2 = _dot(act, wd_ref[...])
        g2 = g2_ref[...]
        lg = lg_ref[...]
        xhat, rstd = _ln_stats(ALPHA * x1_ref[...] + g2 * y2)
        err = xhat * lg + lb_ref[...] - tg_ref[...]
        loss_ref[...] += _rowsum8(err * err)
        dx2 = err * (1.0 / d)
        dlg_ref[...] += _rowsum8(dx2 * xhat)
        dlb_ref[...] += _rowsum8(dx2)
        dz = _ln_bwd(dx2 * lg, xhat, rstd)
        dg2_ref[...] += _rowsum8(dz * y2)
        dz_ref[...] = dz
        dy_ref[...] = (g2 * dz).astype(BF16)

    row = pl.BlockSpec((tm, d), lambda i: (i, 0))
    wide = pl.BlockSpec((tm, D_FF), lambda i: (i, 0))
    vec = _full((1, d))
    acc = _full((8, d))
    acc_shape = jax.ShapeDtypeStruct((8, d), F32)
    wide_shape = jax.ShapeDtypeStruct((t, D_FF), BF16)
    return pl.pallas_call(
        body, name="ffn_fwd", grid=(t // tm,),
        out_shape=[wide_shape] * 3 + [jax.ShapeDtypeStruct((t, d), F32), jax.ShapeDtypeStruct((t, d), BF16)]
        + [acc_shape] * 4,
        in_specs=[row, _resident((2 * D_FF, d)), _resident((D_FF, d)), row, row, vec, vec, vec],
        out_specs=[wide] * 3 + [row, row, acc, acc, acc, acc], compiler_params=_params(("arbitrary",)),
    )(h2, w_gu_t, w_down, x1, target, g2, ln2g, ln2b)


def _resident(shape):
    nd = len(shape)
    return pl.BlockSpec(shape, lambda *_: (0,) * nd, pipeline_mode=pl.Buffered(1))


def _ffn_bwd(dy2, w_down, dsu, sg, w_gu_t, x1, x, y, dz2, sc2, g1, ln1g, tm, comm):
    t, d = x1.shape

    def body(dy2_ref, wd_ref, dsu_ref, sg_ref, w_ref, x1_ref, x_ref, y_ref, dz2_ref, sc_ref, g1_ref, lg_ref,
             dg_ref, du_ref, dz1_ref, dy_ref, dsc_ref, dsh_ref, dlg_ref, dlb_ref, dg1_ref):
        @pl.when(pl.program_id(0) == 0)
        def _():
            for r in (dsc_ref, dsh_ref, dlg_ref, dlb_ref, dg1_ref):
                r[...] = jnp.zeros_like(r)

        dact = _dot_nt(dy2_ref[...], wd_ref[...])
        dg = (dact * dsu_ref[...].astype(F32)).astype(BF16)
        du = (dact * sg_ref[...].astype(F32)).astype(BF16)
        dg_ref[...] = dg
        du_ref[...] = du
        dh2 = _dot(dg, w_ref[0:D_FF]) + _dot(du, w_ref[D_FF:2 * D_FF])
        x1 = x1_ref[...]
        y = y_ref[...]
        g1 = g1_ref[...]
        dsc_ref[...] += _rowsum8(dh2 * x1)
        dsh_ref[...] += _rowsum8(dh2)
        dx1 = dh2 * (1.0 + sc_ref[...]) + ALPHA * dz2_ref[...]
        xhat, rstd = _ln_stats(ALPHA * x_ref[...] + g1 * y)
        dlg_ref[...] += _rowsum8(dx1 * xhat)
        dlb_ref[...] += _rowsum8(dx1)
        dz1 = _ln_bwd(dx1 * lg_ref[...], xhat, rstd)
        dg1_ref[...] += _rowsum8(dz1 * y)
        dz1_ref[...] = dz1
        dy_ref[...] = (g1 * dz1).astype(BF16)

    row = pl.BlockSpec((tm, d), lambda i: (i, 0))
    wide = pl.BlockSpec((tm, D_FF), lambda i: (i, 0))
    vec = _full((1, d))
    acc = _full((8, d))
    acc_shape = jax.ShapeDtypeStruct((8, d), F32)
    wide_shape = jax.ShapeDtypeStruct((t, D_FF), BF16)
    return _call(
        body, name="ffn_bwd", grid=(t // tm,),
        out_shape=[wide_shape, wide_shape, jax.ShapeDtypeStruct((t, d), F32), jax.ShapeDtypeStruct((t, d), BF16)]
        + [acc_shape] * 5,
        in_specs=[row, _resident((D_FF, d)), wide, wide, _resident((2 * D_FF, d)), row, row, row, row, vec, vec, vec],
        out_specs=[wide, wide, row, row, acc, acc, acc, acc, acc], sem=("arbitrary",), comm=comm,
        args=(dy2, w_down, dsu, sg, w_gu_t, x1, x, y, dz2, sc2, g1, ln1g))


def _din(dproj, dkvn, w_in_t, x, dz1, sc1, tm, comm):
    t, d = x.shape

    def body(dp_ref, dkv_ref, w_ref, x_ref, dz1_ref, sc_ref, dx_ref, dpb_ref, dbin_ref, dsc_ref, dsh_ref):
        @pl.when(pl.program_id(0) == 0)
        def _():
            for r in (dbin_ref, dsc_ref, dsh_ref):
                r[...] = jnp.zeros_like(r)

        dp = jnp.concatenate([dp_ref[:, 0:ATTN_W], dp_ref[:, ATTN_W:ATTN_W + 2 * KV_W] + dkv_ref[...],
                              dp_ref[:, ATTN_W + 2 * KV_W:IN_W]], axis=1)
        dbin_ref[...] += _rowsum8(dp)
        dpb = dp.astype(BF16)
        dpb_ref[...] = dpb
        dh = _dot(dpb, w_ref[...])
        dsc_ref[...] += _rowsum8(dh * x_ref[...])
        dsh_ref[...] += _rowsum8(dh)
        dx_ref[...] = dh * (1.0 + sc_ref[...]) + ALPHA * dz1_ref[...]

    row = lambda w: pl.BlockSpec((tm, w), lambda i: (i, 0))
    return _call(
        body, name="din", grid=(t // tm,),
        out_shape=[jax.ShapeDtypeStruct((t, d), F32), jax.ShapeDtypeStruct((t, IN_W), BF16),
                   jax.ShapeDtypeStruct((8, IN_W), F32), jax.ShapeDtypeStruct((8, d), F32),
                   jax.ShapeDtypeStruct((8, d), F32)],
        in_specs=[row(IN_W), row(2 * KV_W), _full((IN_W, d)), row(d), row(d), _full((1, d))],
        out_specs=[row(d), row(IN_W), _full((8, IN_W)), _full((8, d)), _full((8, d))],
        sem=("arbitrary",), comm=comm, args=(dproj, dkvn, w_in_t, x, dz1, sc1))


def _wgrad(name, a, b, tmm, tk, comm=None, a2=None):
    t, m = a.shape
    n = b.shape[1]
    nk = t // tk
    nm = m // tmm

    def body(*refs):
        a_refs, (b_ref, o_ref, acc_ref) = refs[:-3], refs[-3:]
        i, k = pl.program_id(0), pl.program_id(1)

        @pl.when(k == 0)
        def _():
            acc_ref[...] = jnp.zeros_like(acc_ref)

        a_tile = a_refs[0][...] if a2 is None else jnp.where(i < nm, a_refs[0][...], a_refs[1][...])
        acc_ref[...] += _dot_tn(a_tile, b_ref[...])

        @pl.when(k == nk - 1)
        def _():
            o_ref[...] = acc_ref[...].astype(BF16)

    if a2 is None:
        a_specs, a_args, n_tiles = [pl.BlockSpec((tk, tmm), lambda i, k: (k, i))], (a,), nm
    else:
        a_specs = [pl.BlockSpec((tk, tmm), lambda i, k: (jnp.where(i < nm, k, 0), jnp.minimum(i, nm - 1))),
                   pl.BlockSpec((tk, tmm), lambda i, k: (jnp.where(i < nm, 0, k), jnp.maximum(i - nm, 0)))]
        a_args, n_tiles = (a, a2), 2 * nm
    (out,), got = _call(
        body, name=name, grid=(n_tiles, nk), out_shape=[jax.ShapeDtypeStruct((n_tiles * tmm, n), BF16)],
        in_specs=a_specs + [pl.BlockSpec((tk, n), lambda i, k: (k, 0))],
        out_specs=[pl.BlockSpec((tmm, n), lambda i, k: (i, 0))],
        scratch_shapes=[pltpu.VMEM((tmm, n), F32)], sem=("parallel", "arbitrary"), comm=comm, args=a_args + (b,))
    return out if comm is None else (out, got)


def _adamw(w, g, m, v):
    m = ADAM_B1 * m + (1.0 - ADAM_B1) * g
    v = ADAM_B2 * v + (1.0 - ADAM_B2) * (g * g)
    m_hat = m / (1.0 - ADAM_B1 ** ADAM_STEP)
    v_hat = v / (1.0 - ADAM_B2 ** ADAM_STEP)
    delta = -ADAM_LR * (m_hat / (jnp.sqrt(v_hat) + ADAM_EPS) + ADAM_WD * w)
    return delta, m, v


def _adam_reduce(name, parts, w, m, v, tr):
    r, cdim = w.shape

    def body(p_ref, w_ref, m_ref, v_ref, g_ref, d_ref, mo_ref, vo_ref):
        g = p_ref[0].astype(F32)
        for s in range(1, N_DEV):
            g = g + p_ref[s].astype(F32)
        d_ref[...], mo_ref[...], vo_ref[...] = _adamw(w_ref[...], g, m_ref[...], v_ref[...])
        g_ref[...] = g

    tile = pl.BlockSpec((tr, cdim), lambda i: (i, 0))
    shp = jax.ShapeDtypeStruct((r, cdim), F32)
    return pl.pallas_call(
        body, name=name, grid=(r // tr,), out_shape=[shp] * 4,
        in_specs=[pl.BlockSpec((N_DEV, tr, cdim), lambda i: (0, i, 0)), tile, tile, tile],
        out_specs=[tile] * 4, compiler_params=_params(("parallel",)),
    )(parts, w, m, v)


def _adam_w_ada(c_all_t, dmod_cols, w, m, v):
    def body(ct_ref, dm_ref, w_ref, m_ref, v_ref, g_ref, d_ref, mo_ref, vo_ref):
        ct = ct_ref[...]
        s = (ct * _sigmoid(ct)).astype(BF16)
        g = _dot(s, dm_ref[...].astype(BF16))
        d_ref[...], mo_ref[...], vo_ref[...] = _adamw(w_ref[...], g, m_ref[...], v_ref[...])
        g_ref[...] = g

    shp = jax.ShapeDtypeStruct(w.shape, F32)
    return pl.pallas_call(
        body, name="adam_w_ada", grid=(1,), out_shape=[shp] * 4,
        in_specs=[_full(c_all_t.shape), _full(dmod_cols.shape)] + [_full(w.shape)] * 3,
        out_specs=[_full(w.shape)] * 4, compiler_params=_params(("arbitrary",)),
    )(c_all_t, dmod_cols, w, m, v)


SMALL_EARLY = ["rel_bias", "attn_sinks", "gmlp_ln_g", "gmlp_ln_b", "gmlp_w_s", "gmlp_b_s",
               "attn_out_g", "gmlp_out_g", "ln1_g", "ln1_b", "ln2_g", "ln2_b"]
SMALL_LATE = ["b_ada", "b_in", "loss"]
WEIGHTS = ["rel_bias", "w_ada", "b_ada", "w_in", "b_in", "attn_sinks", "gmlp_ln_g", "gmlp_ln_b", "gmlp_w_s",
           "gmlp_b_s", "attn_out_g", "gmlp_out_g", "w_out", "ln1_g", "ln1_b", "w_gate_up", "w_down", "ln2_g", "ln2_b"]


def _seg_rows(nelem):
    return -(-nelem // 1024) * 8


def _pack(named, names):
    parts = []
    for name in names:
        flat = named[name].reshape(-1).astype(F32)
        rows = _seg_rows(flat.shape[0])
        parts.append(jnp.pad(flat, (0, rows * 128 - flat.shape[0])).reshape(rows, 128))
    return jnp.concatenate(parts, axis=0)


def _unpack(packed, shapes, names):
    out, r0 = {}, 0
    for name in names:
        nelem = math.prod(shapes[name])
        rows = _seg_rows(nelem)
        out[name] = packed[r0:r0 + rows].reshape(-1)[:nelem].reshape(shapes[name])
        r0 += rows
    return out


def _t5_bucket_map():
    qi = jnp.arange(BLK)[:, None]
    si = jnp.arange(2 * BLK)[None, :]
    n = jnp.maximum(qi + BLK - si, 0)
    max_exact = N_BUCKETS // 2
    nf = jnp.maximum(n, max_exact).astype(F32)
    large = max_exact + (jnp.log(nf / max_exact) / math.log(MAX_DISTANCE / max_exact)
                         * (N_BUCKETS - max_exact)).astype(jnp.int32)
    large = jnp.minimum(large, N_BUCKETS - 1)
    return jnp.where(n < max_exact, n, large).astype(jnp.int32)


def kernel(x, c, rel_bias, w_ada, b_ada, w_in, b_in, attn_sinks, gmlp_ln_g, gmlp_ln_b, gmlp_w_s, gmlp_b_s, attn_out_g, gmlp_out_g, w_out, ln1_g, ln1_b, w_gate_up, w_down, ln2_g, ln2_b, loss_target, m_rel_bias, m_w_ada, m_b_ada, m_w_in, m_b_in, m_attn_sinks, m_gmlp_ln_g, m_gmlp_ln_b, m_gmlp_w_s, m_gmlp_b_s, m_attn_out_g, m_gmlp_out_g, m_w_out, m_ln1_g, m_ln1_b, m_w_gate_up, m_w_down, m_ln2_g, m_ln2_b, v_rel_bias, v_w_ada, v_b_ada, v_w_in, v_b_in, v_attn_sinks, v_gmlp_ln_g, v_gmlp_ln_b, v_gmlp_w_s, v_gmlp_b_s, v_attn_out_g, v_gmlp_out_g, v_w_out, v_ln1_g, v_ln1_b, v_w_gate_up, v_w_down, v_ln2_g, v_ln2_b):
    wts = dict(rel_bias=rel_bias, w_ada=w_ada, b_ada=b_ada, w_in=w_in, b_in=b_in, attn_sinks=attn_sinks,
               gmlp_ln_g=gmlp_ln_g, gmlp_ln_b=gmlp_ln_b, gmlp_w_s=gmlp_w_s, gmlp_b_s=gmlp_b_s,
               attn_out_g=attn_out_g, gmlp_out_g=gmlp_out_g, w_out=w_out, ln1_g=ln1_g, ln1_b=ln1_b,
               w_gate_up=w_gate_up, w_down=w_down, ln2_g=ln2_g, ln2_b=ln2_b)
    mom_m = dict(rel_bias=m_rel_bias, w_ada=m_w_ada, b_ada=m_b_ada, w_in=m_w_in, b_in=m_b_in,
                 attn_sinks=m_attn_sinks, gmlp_ln_g=m_gmlp_ln_g, gmlp_ln_b=m_gmlp_ln_b, gmlp_w_s=m_gmlp_w_s,
                 gmlp_b_s=m_gmlp_b_s, attn_out_g=m_attn_out_g, gmlp_out_g=m_gmlp_out_g, w_out=m_w_out,
                 ln1_g=m_ln1_g, ln1_b=m_ln1_b, w_gate_up=m_w_gate_up, w_down=m_w_down, ln2_g=m_ln2_g,
                 ln2_b=m_ln2_b)
    mom_v = dict(rel_bias=v_rel_bias, w_ada=v_w_ada, b_ada=v_b_ada, w_in=v_w_in, b_in=v_b_in,
                 attn_sinks=v_attn_sinks, gmlp_ln_g=v_gmlp_ln_g, gmlp_ln_b=v_gmlp_ln_b, gmlp_w_s=v_gmlp_w_s,
                 gmlp_b_s=v_gmlp_b_s, attn_out_g=v_attn_out_g, gmlp_out_g=v_gmlp_out_g, w_out=v_w_out,
                 ln1_g=v_ln1_g, ln1_b=v_ln1_b, w_gate_up=v_w_gate_up, w_down=v_w_down, ln2_g=v_ln2_g,
                 ln2_b=v_ln2_b)

    t = x.shape[1]
    tm = min(512, t)
    tn_ff = D_FF // 2
    tk_long, tk_short = min(4096, t), min(2048, t)
    me = 4 * lax.axis_index("x") + 2 * lax.axis_index("y") + lax.axis_index("c")
    xs = x[0]
    target = loss_target[0]

    c_g, w_in_g = _exchange("gather_in", [jnp.broadcast_to(c, (8, D_MODEL)), w_in[0].T.astype(BF16)],
                            ("gather", "gather2"))
    c_all = c_g[:, 0, :]
    w_in_t = w_in_g.reshape(IN_W, D_MODEL)

    ncol = w_ada.shape[2]
    b_cols = lax.dynamic_slice(b_ada, (0, me * ncol), (1, ncol))
    mod_part = _mod_partial(c_all, w_ada[0], b_cols)
    (mod_g,) = _exchange("gather_mod", [mod_part], ("gather",))
    mod = lax.dynamic_slice(mod_g, (0, me, 0), (N_DEV, 1, ncol)).reshape(1, N_DEV * ncol)
    sh1, sc1, g1, sh2, sc2, g2 = [mod[:, i * D_MODEL:(i + 1) * D_MODEL] for i in range(6)]

    bucket = _t5_bucket_map()
    bias = _bias_table(rel_bias, bucket)
    causal = jnp.tril(jnp.ones((BLK, BLK), dtype=bool))
    ws = jnp.where(causal[None], gmlp_w_s[0], 0.0).astype(BF16)
    pair = lambda w: jnp.concatenate([w[0::2], w[1::2]], axis=2)
    ws2, wst2 = pair(ws), pair(jnp.swapaxes(ws, 1, 2))
    bfull = jnp.repeat(gmlp_b_s[0].T, GMLP_W // N_GROUPS, axis=1)
    sinks = attn_sinks[0]

    proj, h1 = _inproj(xs, sc1, sh1, w_in_t, b_in, tm)
    (mixed,), (w_out_g, w_gu_g) = _mix_fwd(
        proj, bias, sinks, gmlp_ln_g, gmlp_ln_b, ws2, bfull, attn_out_g, gmlp_out_g,
        comm=([w_out[0].astype(BF16), w_gate_up[0].T.astype(BF16)], ("gather2", "gather2")))
    w_out_f = w_out_g.reshape(D_MODEL, D_MODEL)
    w_gu_t = w_gu_g.reshape(2 * D_FF, D_MODEL)
    (y1, x1, h2), (w_down_g,) = _outproj(mixed, w_out_f, xs, g1, ln1_g, ln1_b, sc2, sh2, tm,
                                         comm=([w_down[0].astype(BF16)], ("gather2",)))
    w_down_f = w_down_g.reshape(D_FF, D_MODEL)
    dsu, sg, act, dz2, dy2, loss_p, d_ln2g, d_ln2b, d_g2 = _ffn_fwd(h2, w_gu_t, w_down_f, x1, target, g2, ln2_g, ln2_b,
                                                                    min(256, t))

    slots = lambda a: a.reshape(N_DEV, -1, D_MODEL)
    dw_down = _wgrad("wgrad_down", act, dy2, tn_ff, tk_short)
    (dgate, dup, dz1, dy1, d_sc2, d_sh2, d_ln1g, d_ln1b, d_g1), (r_down,) = _ffn_bwd(
        dy2, w_down_f, dsu, sg, w_gu_t, x1, xs, y1, dz2, sc2, g1, ln1_g, min(256, t),
        comm=([slots(dw_down)], ("scatter",)))
    dw_gu_t = _wgrad("wgrad_gate_up", dgate, h2, tn_ff, tk_short, a2=dup)
    dw_out = _wgrad("wgrad_out", mixed, dy1, D_MODEL, tk_long)
    ((dproj, dkvn, dl_acc, dsink_acc, d_lng, d_lnb, d_ws, d_bs, d_aog, d_gog), (r_gu, r_out)) = _mix_bwd(
        proj, bias, sinks, gmlp_ln_g, gmlp_ln_b, ws2, wst2, bfull, attn_out_g, gmlp_out_g, dy1, w_out_f,
        comm=([slots(dw_gu_t), slots(dw_out)], ("scatter", "scatter")))
    d_relb = _bias_grad(dl_acc, bucket)

    rsum = lambda a: jnp.sum(a, axis=0)
    early_g = dict(
        rel_bias=d_relb[:, 0, :N_BUCKETS].T, attn_sinks=rsum(dsink_acc)[:N_HEADS],
        gmlp_ln_g=rsum(d_lng), gmlp_ln_b=rsum(d_lnb), gmlp_w_s=jnp.where(causal[None], d_ws, 0.0),
        gmlp_b_s=jnp.sum(d_bs.reshape(BLK, N_GROUPS, GMLP_W // N_GROUPS), axis=2).T,
        attn_out_g=rsum(d_aog), gmlp_out_g=rsum(d_gog), ln1_g=rsum(d_ln1g), ln1_b=rsum(d_ln1b),
        ln2_g=rsum(d_ln2g), ln2_b=rsum(d_ln2b))
    (grad_x, dproj_b, d_bin, d_sc1, d_sh1), _ = _din(dproj, dkvn, w_in_t, xs, dz1, sc1, tm, comm=None)
    dw_in_t, (early_all,) = _wgrad("wgrad_in", dproj_b, h1, IN_W // 2, tk_long,
                                   comm=([_pack(early_g, SMALL_EARLY)], ("gather2",)))
    dmod = jnp.concatenate([rsum(d_sh1), rsum(d_sc1), rsum(d_g1), rsum(d_sh2), rsum(d_sc2), rsum(d_g2)])
    late_g = dict(b_ada=dmod, b_in=rsum(d_bin), loss=(0.5 / D_MODEL * jnp.sum(loss_p)).reshape(1))
    late_all, r_in = _exchange("scatter_in", [_pack(late_g, SMALL_LATE), slots(dw_in_t)], ("gather", "scatter"))

    small = [{}, {}, {}, {}]
    for label, names, parts in (("adam_small_early", SMALL_EARLY, early_all), ("adam_small_late", SMALL_LATE, late_all)):
        with_loss = lambda tree: dict(tree, loss=jnp.zeros((1,), F32))
        res = _adam_reduce(label, parts, _pack(with_loss(wts), names), _pack(with_loss(mom_m), names),
                           _pack(with_loss(mom_v), names), parts.shape[1])
        shapes = {k: with_loss(wts)[k].shape for k in names}
        for i in range(4):
            small[i].update(_unpack(res[i], shapes, names))
    loss = small[0]["loss"].reshape(())

    dmod_all = late_all[:, :_seg_rows(6 * D_MODEL), :].reshape(N_DEV, 6 * D_MODEL)
    dmod_cols = lax.dynamic_slice(dmod_all, (0, me * ncol), (N_DEV, ncol))
    kpad = 128 - N_DEV
    ada = _adam_w_ada(jnp.pad(c_all.T, ((0, 0), (0, kpad))), jnp.pad(dmod_cols, ((0, kpad), (0, 0))),
                      w_ada[0], m_w_ada[0], v_w_ada[0])

    tr = lambda a: jnp.swapaxes(a, -1, -2)
    big = {}
    big["w_in"] = [tr(o)[None] for o in _adam_reduce("adam_w_in", r_in, w_in[0].T, m_w_in[0].T, v_w_in[0].T, 112)]
    big["w_out"] = [o[None] for o in _adam_reduce("adam_w_out", r_out, w_out[0], m_w_out[0], v_w_out[0], 128)]
    big["w_gate_up"] = [tr(o)[None] for o in _adam_reduce("adam_w_gu", r_gu, w_gate_up[0].T, m_w_gate_up[0].T,
                                                           v_w_gate_up[0].T, 352)]
    big["w_down"] = [o[None] for o in _adam_reduce("adam_w_down", r_down, w_down[0], m_w_down[0], v_w_down[0], 176)]
    big["w_ada"] = [o[None] for o in ada]

    outs = [[], [], [], []]
    for name in WEIGHTS:
        for i in range(4):
            outs[i].append(big[name][i] if name in big else small[i][name])
    return (loss, grad_x[None], *outs[0], *outs[1], *outs[2], *outs[3])
```

```python
import math

import jax
import jax.numpy as jnp
from jax import lax
from jax.experimental import pallas as pl
from jax.experimental.pallas import tpu as pltpu

F32 = jnp.float32
BF16 = jnp.bfloat16
MESH = pl.DeviceIdType.MESH

N_DEV = 8
D_MODEL = 1024
HEAD_DIM = 64
N_HEADS = 8
N_GROUPS = 8
ATTN_W = 512
KV_W = 128
GMLP_W = 512
IN_W = 1792
BLK = 128
N_BUCKETS = 32
MAX_DISTANCE = 128
D_FF = 2816
ALPHA = 2.0 ** 0.25
LN_EPS = 1e-5
NEG_INF = -1e30
ADAM_LR = 0.001
ADAM_B1 = 0.9
ADAM_B2 = 0.999
ADAM_EPS = 1e-08
ADAM_WD = 0.01
ADAM_STEP = 10
GELU_C0 = math.sqrt(2.0 / math.pi)
GELU_C1 = 0.044715

VMEM_LIMIT = 56 * 1024 * 1024


def _params(sem):
    return pltpu.CompilerParams(dimension_semantics=sem, vmem_limit_bytes=VMEM_LIMIT)


def _dot(a, b):
    return lax.dot_general(a, b, (((1,), (0,)), ((), ())), preferred_element_type=F32)


def _dot_nt(a, b):
    return lax.dot_general(a, b, (((1,), (1,)), ((), ())), preferred_element_type=F32)


def _dot_tn(a, b):
    return lax.dot_general(a, b, (((0,), (0,)), ((), ())), preferred_element_type=F32)


def _full(shape):
    nd = len(shape)
    return pl.BlockSpec(shape, lambda *_: (0,) * nd)


def _rowsum8(v):
    r, c = v.shape
    return jnp.sum(v.reshape(r // 8, 8, c), axis=0)


def _sigmoid(v):
    return 1.0 / (1.0 + jnp.exp(-v))


def _gelu_parts(v):
    v2 = v * v
    t = jnp.tanh(GELU_C0 * (v + GELU_C1 * v * v2))
    g = 0.5 * v * (1.0 + t)
    dg = 0.5 * (1.0 + t) + 0.5 * v * (1.0 - t * t) * (GELU_C0 * (1.0 + 3.0 * GELU_C1 * v2))
    return g, dg


def _ln_stats(z):
    mu = jnp.mean(z, axis=1, keepdims=True)
    zc = z - mu
    var = jnp.mean(zc * zc, axis=1, keepdims=True)
    rstd = lax.rsqrt(var + LN_EPS)
    return zc * rstd, rstd


def _ln_bwd(dxhat, xhat, rstd):
    m1 = jnp.mean(dxhat, axis=1, keepdims=True)
    m2 = jnp.mean(dxhat * xhat, axis=1, keepdims=True)
    return rstd * (dxhat - m1 - xhat * m2)


def _seg_mean64(v):
    r = v.shape[0]
    lo = lax.broadcasted_iota(jnp.int32, (r, 128), 1) < 64
    outs = []
    for j in range(v.shape[1] // 128):
        ch = v[:, 128 * j:128 * (j + 1)]
        s_lo = jnp.sum(jnp.where(lo, ch, 0.0), axis=1, keepdims=True)
        s_hi = jnp.sum(jnp.where(lo, 0.0, ch), axis=1, keepdims=True)
        outs.append(jnp.where(lo, s_lo, s_hi) * (1.0 / 64.0))
    return jnp.concatenate(outs, axis=1)


def _rms(a, g):
    r = lax.rsqrt(jnp.mean(a * a, axis=1, keepdims=True) + LN_EPS)
    return a * r * g, r


def _rms_bwd(dout, a, r, g):
    t = dout * g
    return r * t - a * (r * r * r) * jnp.mean(t * a, axis=1, keepdims=True)


PEER_ORDER = (1, 2, 4, 3, 5, 6, 7)


def _peer(j):
    x, y, c = lax.axis_index("x"), lax.axis_index("y"), lax.axis_index("c")
    px = 1 - x if j & 4 else x
    py = 1 - y if j & 2 else y
    pc = 1 - c if j & 1 else c
    return (px, py, pc), 4 * px + 2 * py + pc


SIBLING = 1
CHIP_FLIPS = (4, 2, 6)


def _exchange_phase(phase, ins, outs, modes, send_sems, recv_sems, loc_sems):
    me = 4 * lax.axis_index("x") + 2 * lax.axis_index("y") + lax.axis_index("c")
    for k, mode in enumerate(modes):
        def copy(i, src, slot, dev, k=k):
            return pltpu.make_async_remote_copy(src_ref=src, dst_ref=outs[k].at[slot], send_sem=send_sems.at[k, i],
                                                recv_sem=recv_sems.at[k, i], device_id=dev, device_id_type=MESH)

        src_me = ins[k].at[me] if mode == "scatter" else ins[k]
        local = pltpu.make_async_copy(src_me, outs[k].at[me], loc_sems.at[k])
        if mode == "gather2":
            sib_dev, sib_idx = _peer(SIBLING)
            chips = [_peer(j) for j in CHIP_FLIPS]
            far = [_peer(j | SIBLING)[1] for j in CHIP_FLIPS]
            if phase == "start":
                local.start()
                copy(0, ins[k], me, sib_dev).start()
                for i, (dev, _) in enumerate(chips):
                    copy(1 + i, ins[k], me, dev).start()
            elif phase == "mid":
                for i, (dev, idx) in enumerate(chips):
                    copy(1 + i, ins[k], idx, dev).wait_recv()
                    copy(4 + i, outs[k].at[idx], idx, sib_dev).start()
            else:
                copy(0, ins[k], sib_idx, sib_dev).wait_recv()
                for i, slot in enumerate(far):
                    copy(4 + i, ins[k], slot, sib_dev).wait_recv()
                copy(0, ins[k], me, sib_dev).wait_send()
                for i, (dev, idx) in enumerate(chips):
                    copy(1 + i, ins[k], me, dev).wait_send()
                    copy(4 + i, outs[k].at[idx], idx, sib_dev).wait_send()
                local.wait()
            continue
        peers = [_peer(j) for j in PEER_ORDER]
        if phase == "start":
            local.start()
            for i, (dev, idx) in enumerate(peers):
                copy(i, ins[k].at[idx] if mode == "scatter" else ins[k], me, dev).start()
        elif phase == "end":
            for i, (dev, idx) in enumerate(peers):
                copy(i, src_me, idx, dev).wait_recv()
            for i, (dev, idx) in enumerate(peers):
                copy(i, src_me, me, dev).wait_send()
            local.wait()


def _exchange_shapes(arrays, modes):
    return [jax.ShapeDtypeStruct((N_DEV,) + (a.shape[1:] if m == "scatter" else a.shape), a.dtype)
            for a, m in zip(arrays, modes)]


def _exchange_sems(n):
    return [pltpu.SemaphoreType.DMA((n, N_DEV - 1)), pltpu.SemaphoreType.DMA((n, N_DEV - 1)),
            pltpu.SemaphoreType.DMA((n,))]


def _exchange(name, arrays, modes):
    n = len(arrays)

    def body(*refs):
        for phase in ("start", "mid", "end"):
            _exchange_phase(phase, refs[:n], refs[n:2 * n], modes, *refs[2 * n:])

    any_spec = pl.BlockSpec(memory_space=pl.ANY)
    return pl.pallas_call(
        body, name=name, out_shape=_exchange_shapes(arrays, modes),
        in_specs=[any_spec] * n, out_specs=[any_spec] * n, scratch_shapes=_exchange_sems(n),
    )(*arrays)


def _call(body, *, name, grid, in_specs, out_specs, out_shape, args, sem, scratch_shapes=(), comm=None):
    if comm is None:
        outs = pl.pallas_call(body, name=name, grid=grid, in_specs=list(in_specs), out_specs=list(out_specs),
                              out_shape=list(out_shape), scratch_shapes=list(scratch_shapes),
                              compiler_params=_params(sem))(*args)
        return list(outs), []
    arrays, modes = comm
    n_in, n_out, nc, ns = len(in_specs), len(out_specs), len(arrays), len(scratch_shapes)
    n_steps = math.prod(grid)

    def hosted(*refs):
        ins, cins = refs[:n_in], refs[n_in:n_in + nc]
        outs, couts = refs[n_in + nc:n_in + nc + n_out], refs[n_in + nc + n_out:n_in + 2 * nc + n_out]
        scratch = refs[n_in + 2 * nc + n_out:]
        ex = (cins, couts, modes) + tuple(scratch[ns:])
        step = pl.program_id(0)
        for ax in range(1, len(grid)):
            step = step * grid[ax] + pl.program_id(ax)

        @pl.when(step == 0)
        def _():
            _exchange_phase("start", *ex)

        body(*ins, *outs, *scratch[:ns])

        if "gather2" in modes:
            @pl.when(step == (3 * n_steps) // 4)
            def _():
                _exchange_phase("mid", *ex)

        @pl.when(step == n_steps - 1)
        def _():
            _exchange_phase("end", *ex)

    any_spec = pl.BlockSpec(memory_space=pl.ANY)
    res = pl.pallas_call(
        hosted, name=name, grid=grid, in_specs=list(in_specs) + [any_spec] * nc,
        out_specs=list(out_specs) + [any_spec] * nc, out_shape=list(out_shape) + _exchange_shapes(arrays, modes),
        scratch_shapes=list(scratch_shapes) + _exchange_sems(nc),
        compiler_params=_params(tuple("arbitrary" for _ in grid)))(*args, *arrays)
    return list(res[:n_out]), list(res[n_out:])


def _mod_partial(c_all, w_ada, b_ada_cols):
    def body(c_ref, w_ref, b_ref, o_ref):
        cv = c_ref[...]
        s = (cv * _sigmoid(cv)).astype(BF16)
        o_ref[...] = _dot(s, w_ref[...].astype(BF16)) + b_ref[...]

    ncol = w_ada.shape[1]
    return pl.pallas_call(
        body, name="mod_partial", out_shape=jax.ShapeDtypeStruct((N_DEV, ncol), F32),
        in_specs=[_full(c_all.shape), _full(w_ada.shape), _full(b_ada_cols.shape)],
        out_specs=_full((N_DEV, ncol)), grid=(1,), compiler_params=_params(("arbitrary",)),
    )(c_all, w_ada, b_ada_cols)


def _bias_table(rel_bias, bucket):
    def body(rb_ref, bk_ref, o_ref):
        h = pl.program_id(0)
        bk = bk_ref[...]
        acc = jnp.zeros((BLK, 2 * BLK), F32)
        for b in range(N_BUCKETS):
            acc = jnp.where(bk == b, rb_ref[b, h], acc)
        dist = (lax.broadcasted_iota(jnp.int32, (BLK, 2 * BLK), 0) + BLK
                - lax.broadcasted_iota(jnp.int32, (BLK, 2 * BLK), 1))
        o_ref[0] = jnp.where((dist >= 0) & (dist < BLK), acc, NEG_INF)

    return pl.pallas_call(
        body, name="bias_table", out_shape=jax.ShapeDtypeStruct((N_HEADS, BLK, 2 * BLK), F32),
        in_specs=[pl.BlockSpec(memory_space=pltpu.SMEM), _full((BLK, 2 * BLK))],
        out_specs=pl.BlockSpec((1, BLK, 2 * BLK), lambda h: (h, 0, 0)), grid=(N_HEADS,),
        compiler_params=_params(("arbitrary",)),
    )(rel_bias, bucket)


def _bias_grad(dl_acc, bucket):
    def body(dl_ref, bk_ref, o_ref):
        bk = bk_ref[...]
        dl = dl_ref[0]
        lane = lax.broadcasted_iota(jnp.int32, (1, 128), 1)
        row = jnp.zeros((1, 128), F32)
        for b in range(N_BUCKETS):
            s = jnp.sum(jnp.sum(jnp.where(bk == b, dl, 0.0), axis=1, keepdims=True), axis=0, keepdims=True)
            row = jnp.where(lane == b, s, row)
        o_ref[0] = row

    return pl.pallas_call(
        body, name="bias_grad", out_shape=jax.ShapeDtypeStruct((N_HEADS, 1, 128), F32),
        in_specs=[pl.BlockSpec((1, BLK, 2 * BLK), lambda h: (h, 0, 0)), _full((BLK, 2 * BLK))],
        out_specs=pl.BlockSpec((1, 1, 128), lambda h: (h, 0, 0)), grid=(N_HEADS,),
        compiler_params=_params(("arbitrary",)),
    )(dl_acc, bucket)


def _inproj(x, sc1, sh1, w_in_t, b_in, tm):
    t, d = x.shape
    n = w_in_t.shape[0]

    def body(x_ref, sc_ref, sh_ref, w_ref, b_ref, proj_ref, h_ref):
        h = (x_ref[...] * (1.0 + sc_ref[...]) + sh_ref[...]).astype(BF16)
        h_ref[...] = h
        proj_ref[...] = _dot_nt(h, w_ref[...]) + b_ref[...]

    row = lambda w: pl.BlockSpec((tm, w), lambda i: (i, 0))
    return pl.pallas_call(
        body, name="inproj", grid=(t // tm,),
        out_shape=[jax.ShapeDtypeStruct((t, n), F32), jax.ShapeDtypeStruct((t, d), BF16)],
        in_specs=[row(d), _full((1, d)), _full((1, d)), _full((n, d)), _full((1, n))],
        out_specs=[row(n), row(d)], compiler_params=_params(("parallel",)),
    )(x, sc1, sh1, w_in_t, b_in)


def _half_masks():
    lo_q = lax.broadcasted_iota(jnp.int32, (BLK, 128), 1) < 64
    lo_k = lax.broadcasted_iota(jnp.int32, (2 * BLK, 128), 1) < 64
    return lo_q, lo_k


def _head_place(h):
    return h // 2, h % 2, h // 4


def _attn_heads(q, kk, vv, bias_ref, sinks_ref, n):
    lo_q, lo_k = _half_masks()
    kkb, kksb = kk.astype(BF16), pltpu.roll(kk, 64, 1).astype(BF16)
    vvb, vvsb = vv.astype(BF16), pltpu.roll(vv, 64, 1).astype(BF16)
    n0mask = (n == 0) & (lax.broadcasted_iota(jnp.int32, (BLK, 2 * BLK), 1) < BLK)
    chunks, probs = [], []
    for j in range(4):
        qc = q[:, 128 * j:128 * (j + 1)]
        acc = jnp.zeros((BLK, 128), F32)
        for pos in range(2):
            h = 2 * j + pos
            direct = (h // 4) == pos
            mq = lo_q if pos == 0 else jnp.logical_not(lo_q)
            mk = lo_k if pos == 0 else jnp.logical_not(lo_k)
            qm = jnp.where(mq, qc, 0.0).astype(BF16)
            logit = _dot_nt(qm, kkb if direct else kksb) * (HEAD_DIM ** -0.5) + bias_ref[h]
            logit = jnp.where(n0mask, NEG_INF, logit)
            sk = sinks_ref[h]
            m = jnp.maximum(jnp.max(logit, axis=1, keepdims=True), sk)
            e = jnp.exp(logit - m)
            es = jnp.exp(sk - m)
            den = jnp.sum(e, axis=1, keepdims=True) + es
            p = e / den
            vm = jnp.where(mk, vvb if direct else vvsb, jnp.zeros_like(vvb))
            acc = acc + _dot(p.astype(BF16), vm)
            probs.append((p, es / den))
        chunks.append(acc)
    return jnp.concatenate(chunks, axis=1), probs


def _gmlp_block(gu, gv, lng, lnb, ws_ref, bfull):
    lo_q, _ = _half_masks()
    u, du = _gelu_parts(gu)
    a, da = _gelu_parts(gv)
    mu = _seg_mean64(a)
    ac = a - mu
    rstd = lax.rsqrt(_seg_mean64(ac * ac) + LN_EPS)
    vhat = ac * rstd
    vn = vhat * lng + lnb
    chunks = []
    for j in range(4):
        vc = vn[:, 128 * j:128 * (j + 1)]
        acc = jnp.zeros((BLK, 128), F32)
        for pos in range(2):
            mq = lo_q if pos == 0 else jnp.logical_not(lo_q)
            acc = acc + _dot(ws_ref[2 * j + pos], jnp.where(mq, vc, 0.0).astype(BF16))
        chunks.append(acc)
    ms = jnp.concatenate(chunks, axis=1) + bfull
    return u * ms, (u, du, da, vhat, rstd, vn, ms)


def _mix_in_specs(nb):
    return [pl.BlockSpec((BLK, IN_W), lambda n: (n, 0)),
            pl.BlockSpec((BLK, 2 * KV_W), lambda n: (jnp.maximum(n - 1, 0), ATTN_W // (2 * KV_W))),
            _full((N_HEADS, BLK, 2 * BLK)),
            pl.BlockSpec(memory_space=pltpu.SMEM),
            _full((1, GMLP_W)), _full((1, GMLP_W)),
            _full((N_GROUPS, BLK, BLK)), _full((BLK, GMLP_W)),
            _full((1, ATTN_W)), _full((1, GMLP_W))]


def _split_proj(proj_ref, kvp_ref):
    q = proj_ref[:, 0:ATTN_W]
    k = proj_ref[:, ATTN_W:ATTN_W + KV_W]
    v = proj_ref[:, ATTN_W + KV_W:ATTN_W + 2 * KV_W]
    gu = proj_ref[:, ATTN_W + 2 * KV_W:ATTN_W + 2 * KV_W + GMLP_W]
    gv = proj_ref[:, ATTN_W + 2 * KV_W + GMLP_W:IN_W]
    kk = jnp.concatenate([kvp_ref[:, 0:KV_W], k], axis=0)
    vv = jnp.concatenate([kvp_ref[:, KV_W:2 * KV_W], v], axis=0)
    return q, kk, vv, gu, gv


def _mix_fwd(proj, bias, sinks, lng, lnb, ws, bfull, aog, gog, comm):
    t = proj.shape[0]
    nb = t // BLK

    def body(proj_ref, kvp_ref, bias_ref, sinks_ref, lng_ref, lnb_ref, ws_ref, bfull_ref, aog_ref, gog_ref, out_ref):
        n = pl.program_id(0)
        q, kk, vv, gu, gv = _split_proj(proj_ref, kvp_ref)
        attn, _ = _attn_heads(q, kk, vv, bias_ref, sinks_ref, n)
        gm, _ = _gmlp_block(gu, gv, lng_ref[...], lnb_ref[...], ws_ref, bfull_ref[...])
        out_ref[:, 0:ATTN_W] = _rms(attn, aog_ref[...])[0].astype(BF16)
        out_ref[:, ATTN_W:ATTN_W + GMLP_W] = _rms(gm, gog_ref[...])[0].astype(BF16)

    return _call(
        body, name="mix_fwd", grid=(nb,), out_shape=[jax.ShapeDtypeStruct((t, D_MODEL), BF16)],
        in_specs=_mix_in_specs(nb), out_specs=[pl.BlockSpec((BLK, D_MODEL), lambda n: (n, 0))],
        sem=("parallel",), comm=comm, args=(proj, proj, bias, sinks, lng, lnb, ws, bfull, aog, gog))


def _mix_bwd(proj, bias, sinks, lng, lnb, ws, ws_t, bfull, aog, gog, dy, w_out, comm):
    t = proj.shape[0]
    nb = t // BLK

    def body(proj_ref, kvp_ref, bias_ref, sinks_ref, lng_ref, lnb_ref, ws_ref, bfull_ref, aog_ref, gog_ref,
             wst_ref, dy_ref, wout_ref,
             dproj_ref, dkvn_ref, dl_ref, dsink_ref, dlng_ref, dlnb_ref, dws_ref, dbs_ref, daog_ref, dgog_ref):
        n = pl.program_id(0)

        @pl.when(n == 0)
        def _():
            for r in (dl_ref, dsink_ref, dlng_ref, dlnb_ref, dws_ref, dbs_ref, daog_ref, dgog_ref):
                r[...] = jnp.zeros_like(r)

        lo_q, lo_k = _half_masks()
        q, kk, vv, gu, gv = _split_proj(proj_ref, kvp_ref)
        dmix = _dot_nt(dy_ref[...], wout_ref[...])
        dma, dmg = dmix[:, 0:ATTN_W], dmix[:, ATTN_W:ATTN_W + GMLP_W]

        attn, probs = _attn_heads(q, kk, vv, bias_ref, sinks_ref, n)
        aog = aog_ref[...]
        _, r_a = _rms(attn, aog)
        daog_ref[...] += _rowsum8(dma * attn * r_a)
        dattn = _rms_bwd(dma, attn, r_a, aog)

        kkb, kksb = kk.astype(BF16), pltpu.roll(kk, 64, 1).astype(BF16)
        vvb, vvsb = vv.astype(BF16), pltpu.roll(vv, 64, 1).astype(BF16)
        lane = lax.broadcasted_iota(jnp.int32, (BLK, 128), 1)
        dk_d = jnp.zeros((2 * BLK, 128), F32)
        dk_s = jnp.zeros((2 * BLK, 128), F32)
        dv_d = jnp.zeros((2 * BLK, 128), F32)
        dv_s = jnp.zeros((2 * BLK, 128), F32)
        dsink = jnp.zeros((BLK, 128), F32)
        dq_chunks = []
        for j in range(4):
            qc = q[:, 128 * j:128 * (j + 1)]
            doc = dattn[:, 128 * j:128 * (j + 1)]
            dq = jnp.zeros((BLK, 128), F32)
            for pos in range(2):
                h = 2 * j + pos
                direct = (h // 4) == pos
                mq = lo_q if pos == 0 else jnp.logical_not(lo_q)
                mk = lo_k if pos == 0 else jnp.logical_not(lo_k)
                p, psink = probs[h]
                qm = jnp.where(mq, qc, 0.0).astype(BF16)
                dom = jnp.where(mq, doc, 0.0).astype(BF16)
                dp = _dot_nt(dom, vvb if direct else vvsb)
                rs = jnp.sum(p * dp, axis=1, keepdims=True)
                dl = p * (dp - rs)
                dl_ref[h] += dl
                dsink = dsink + jnp.where(lane == h, -psink * rs, 0.0)
                dls = (dl * (HEAD_DIM ** -0.5)).astype(BF16)
                km = jnp.where(mk, kkb if direct else kksb, jnp.zeros_like(kkb))
                dq = dq + _dot(dls, km)
                dk_h = _dot_tn(dls, qm)
                dv_h = _dot_tn(p.astype(BF16), dom)
                if direct:
                    dk_d, dv_d = dk_d + dk_h, dv_d + dv_h
                else:
                    dk_s, dv_s = dk_s + dk_h, dv_s + dv_h
            dq_chunks.append(dq)
        dsink_ref[...] += dsink
        dk = dk_d + pltpu.roll(dk_s, 64, 1)
        dv = dv_d + pltpu.roll(dv_s, 64, 1)
        for j in range(4):
            dproj_ref[:, 128 * j:128 * (j + 1)] = dq_chunks[j]
        dproj_ref[:, ATTN_W:ATTN_W + KV_W] = dk[BLK:2 * BLK]
        dproj_ref[:, ATTN_W + KV_W:ATTN_W + 2 * KV_W] = dv[BLK:2 * BLK]
        dkvn_ref[:, 0:KV_W] = dk[0:BLK]
        dkvn_ref[:, KV_W:2 * KV_W] = dv[0:BLK]

        lng = lng_ref[...]
        gog = gog_ref[...]
        gm, (u, du, da, vhat, rstd, vn, ms) = _gmlp_block(gu, gv, lng, lnb_ref[...], ws_ref, bfull_ref[...])
        _, r_g = _rms(gm, gog)
        dgog_ref[...] += _rowsum8(dmg * gm * r_g)
        dgm = _rms_bwd(dmg, gm, r_g, gog)
        dproj_ref[:, ATTN_W + 2 * KV_W:ATTN_W + 2 * KV_W + GMLP_W] = dgm * ms * du
        dms = dgm * u
        dbs_ref[...] += dms
        dvn_chunks = []
        for j in range(4):
            dmc = dms[:, 128 * j:128 * (j + 1)]
            vcb = vn[:, 128 * j:128 * (j + 1)].astype(BF16)
            acc = jnp.zeros((BLK, 128), F32)
            for pos in range(2):
                g = 2 * j + pos
                mq = lo_q if pos == 0 else jnp.logical_not(lo_q)
                dm = jnp.where(mq, dmc, 0.0).astype(BF16)
                dws_ref[g] += _dot_nt(dm, vcb)
                acc = acc + _dot(wst_ref[g], dm)
            dvn_chunks.append(acc)
        dvn = jnp.concatenate(dvn_chunks, axis=1)
        dlng_ref[...] += _rowsum8(dvn * vhat)
        dlnb_ref[...] += _rowsum8(dvn)
        dvh = dvn * lng
        dact = rstd * (dvh - _seg_mean64(dvh) - vhat * _seg_mean64(dvh * vhat))
        dproj_ref[:, ATTN_W + 2 * KV_W + GMLP_W:IN_W] = dact * da

    acc8 = lambda w: jax.ShapeDtypeStruct((8, w), F32)
    out_shape = [jax.ShapeDtypeStruct((t, IN_W), F32), jax.ShapeDtypeStruct((t, 2 * KV_W), F32),
                 jax.ShapeDtypeStruct((N_HEADS, BLK, 2 * BLK), F32), jax.ShapeDtypeStruct((BLK, 128), F32),
                 acc8(GMLP_W), acc8(GMLP_W), jax.ShapeDtypeStruct((N_GROUPS, BLK, BLK), F32),
                 jax.ShapeDtypeStruct((BLK, GMLP_W), F32), acc8(ATTN_W), acc8(GMLP_W)]
    out_specs = [pl.BlockSpec((BLK, IN_W), lambda n: (n, 0)),
                 pl.BlockSpec((BLK, 2 * KV_W), lambda n: ((n + nb - 1) % nb, 0)),
                 _full((N_HEADS, BLK, 2 * BLK)), _full((BLK, 128)), _full((8, GMLP_W)), _full((8, GMLP_W)),
                 _full((N_GROUPS, BLK, BLK)), _full((BLK, GMLP_W)), _full((8, ATTN_W)), _full((8, GMLP_W))]
    in_specs = _mix_in_specs(nb) + [_full((N_GROUPS, BLK, BLK)),
                                    pl.BlockSpec((BLK, D_MODEL), lambda n: (n, 0)),
                                    _full((D_MODEL, D_MODEL))]
    return _call(
        body, name="mix_bwd", grid=(nb,), out_shape=out_shape, in_specs=in_specs, out_specs=out_specs,
        sem=("arbitrary",), comm=comm, args=(proj, proj, bias, sinks, lng, lnb, ws, bfull, aog, gog, ws_t, dy, w_out))


HALF = 64
ROWS = 32


def _lane_lo(rows):
    return lax.broadcasted_iota(jnp.int32, (rows, 128), 1) < 64


def _mix_stage_kv(proj_ref, kvp_ref, s):
    lo = _lane_lo(2 * BLK)
    for name, col in (("k", ATTN_W), ("v", ATTN_W + KV_W)):
        cur = jnp.concatenate([kvp_ref[:, col - ATTN_W:col - ATTN_W + KV_W], proj_ref[:, col:col + KV_W]], axis=0)
        plain, swapped = cur.astype(BF16), pltpu.roll(cur, 64, 1).astype(BF16)
        zero = jnp.zeros_like(plain)
        for g in range(2):
            dup = jnp.where(lo, plain, swapped) if g == 0 else jnp.where(lo, swapped, plain)
            s[name + "d"][g] = dup
            s[name + "m"][g] = jnp.concatenate([jnp.where(lo, dup, zero), jnp.where(lo, zero, dup)], axis=0)


def _group_rows(ref, g):
    return ref[4 * g:4 * g + 4].reshape(4 * BLK, ref.shape[2])


def _pair_rows(ref, g):
    return jnp.concatenate([jnp.concatenate([ref[4 * g + 2 * c], ref[4 * g + 2 * c + 1]], axis=1) for c in range(2)],
                           axis=0)


def _mask_heads(src_ref, dst_ref):
    lo = _lane_lo(BLK)
    for j in range(4):
        chunk = src_ref[:, 128 * j:128 * (j + 1)]
        dst_ref[2 * j] = jnp.where(lo, chunk, 0.0).astype(BF16)
        dst_ref[2 * j + 1] = jnp.where(lo, 0.0, chunk).astype(BF16)


def _mix_stage_attn(proj_ref, bias_ref, sinks_ref, n, s):
    _mask_heads(proj_ref, s["qm"])
    for g in range(2):
        s["lg"][g] = _dot_nt(_group_rows(s["qm"], g), s["kd"][g])
    n0mask = (n == 0) & (lax.broadcasted_iota(jnp.int32, (HALF, 2 * BLK), 1) < BLK)
    lane = lax.broadcasted_iota(jnp.int32, (HALF, 128), 1)
    for hf in range(BLK // HALF):
        rows = slice(HALF * hf, HALF * (hf + 1))
        psink = jnp.zeros((HALF, 128), F32)
        for h in range(N_HEADS):
            sk = sinks_ref[h]
            grows = slice(BLK * (h % 4) + HALF * hf, BLK * (h % 4) + HALF * (hf + 1))
            logit = s["lg"][h // 4, grows, :] * (HEAD_DIM ** -0.5) + bias_ref[h, rows, :]
            logit = jnp.where(n0mask, NEG_INF, logit)
            m = jnp.maximum(jnp.max(logit, axis=1, keepdims=True), sk)
            e = jnp.exp(logit - m)
            es = jnp.exp(sk - m)
            inv = 1.0 / (jnp.sum(e, axis=1, keepdims=True) + es)
            p = e * inv
            s["p"][h, rows, :] = p
            s["pb"][h, rows, :] = p.astype(BF16)
            psink = jnp.where(lane == h, es * inv, psink)
        s["psink"][rows, :] = psink
    for g in range(2):
        out = _dot(_pair_rows(s["pb"], g), s["vm"][g])
        s["attn"][:, 256 * g:256 * g + 128] = out[0:BLK]
        s["attn"][:, 256 * g + 128:256 * g + 256] = out[BLK:2 * BLK]


def _mix_stage_gmlp_pre(proj_ref, lng, lnb, s, keep):
    c0 = ATTN_W + 2 * KV_W
    for r0 in range(0, BLK, ROWS):
        rows = slice(r0, r0 + ROWS)
        u, du = _gelu_parts(proj_ref[rows, c0:c0 + GMLP_W])
        a, da = _gelu_parts(proj_ref[rows, c0 + GMLP_W:c0 + 2 * GMLP_W])
        ac = a - _seg_mean64(a)
        rstd = lax.rsqrt(_seg_mean64(ac * ac) + LN_EPS)
        vhat = ac * rstd
        s["u"][rows, :] = u
        s["vnb"][rows, :] = (vhat * lng + lnb).astype(BF16)
        if keep:
            s["du"][rows, :] = du
            s["da"][rows, :] = da
            s["vhat"][rows, :] = vhat
            s["rstd"][rows, :] = rstd


def _stack_halves(chunk):
    lo = _lane_lo(BLK)
    zero = jnp.zeros_like(chunk)
    return jnp.concatenate([jnp.where(lo, chunk, zero), jnp.where(lo, zero, chunk)], axis=0)


def _mix_stage_gmlp_mix(ws2_ref, bfull_ref, s):
    for j in range(4):
        cols = slice(128 * j, 128 * (j + 1))
        s["ms"][:, cols] = _dot(ws2_ref[j], _stack_halves(s["vnb"][:, cols])) + bfull_ref[:, cols]


def _mix_scratch(keep):
    f32 = lambda *shape: pltpu.VMEM(shape, F32)
    b16 = lambda *shape: pltpu.VMEM(shape, BF16)
    names = dict(kd=b16(2, 2 * BLK, 128), vd=b16(2, 2 * BLK, 128), km=b16(2, 4 * BLK, 128), vm=b16(2, 4 * BLK, 128),
                 qm=b16(N_HEADS, BLK, 128), lg=f32(2, 4 * BLK, 2 * BLK), pb=b16(N_HEADS, BLK, 2 * BLK),
                 u=f32(BLK, GMLP_W), vnb=b16(BLK, GMLP_W), ms=f32(BLK, GMLP_W))
    if keep:
        names.update(dom=b16(N_HEADS, BLK, 128), dls=b16(N_HEADS, BLK, 2 * BLK),
                     dattn=f32(BLK, ATTN_W), dmix=f32(BLK, D_MODEL), du=f32(BLK, GMLP_W), da=f32(BLK, GMLP_W),
                     vhat=f32(BLK, GMLP_W), rstd=f32(BLK, GMLP_W), dmsb=b16(BLK, GMLP_W), dvn=f32(BLK, GMLP_W))
    return list(names), list(names.values())


def _mix_specs():
    return [pl.BlockSpec((BLK, IN_W), lambda n: (n, 0)),
            pl.BlockSpec((BLK, 2 * KV_W), lambda n: (jnp.maximum(n - 1, 0), ATTN_W // (2 * KV_W))),
            _full((N_HEADS, BLK, 2 * BLK)),
            pl.BlockSpec(memory_space=pltpu.SMEM),
            _full((1, GMLP_W)), _full((1, GMLP_W)),
            _full((N_GROUPS // 2, BLK, 2 * BLK)), _full((BLK, GMLP_W)),
            _full((1, ATTN_W)), _full((1, GMLP_W))]


def _kept_shapes(t):
    return [(N_HEADS, t, 2 * BLK), (t, 128), (t, ATTN_W)]


def _kept_specs():
    return [pl.BlockSpec((N_HEADS, BLK, 2 * BLK), lambda n: (0, n, 0)), pl.BlockSpec((BLK, 128), lambda n: (n, 0)),
            pl.BlockSpec((BLK, ATTN_W), lambda n: (n, 0))]


def _mix_fwd(proj, bias, sinks, lng, lnb, ws2, bfull, aog, gog, comm):
    t = proj.shape[0]
    names, shapes = _mix_scratch(False)

    def body(proj_ref, kvp_ref, bias_ref, sinks_ref, lng_ref, lnb_ref, ws2_ref, bfull_ref, aog_ref, gog_ref,
             out_ref, p_ref, psink_ref, attn_ref, *scratch):
        s = dict(zip(names, scratch), p=p_ref, psink=psink_ref, attn=attn_ref)
        n = pl.program_id(0)
        _mix_stage_kv(proj_ref, kvp_ref, s)
        _mix_stage_attn(proj_ref, bias_ref, sinks_ref, n, s)
        _mix_stage_gmlp_pre(proj_ref, lng_ref[...], lnb_ref[...], s, False)
        _mix_stage_gmlp_mix(ws2_ref, bfull_ref, s)
        for r0 in range(0, BLK, ROWS):
            rows = slice(r0, r0 + ROWS)
            out_ref[rows, 0:ATTN_W] = _rms(s["attn"][rows, :], aog_ref[...])[0].astype(BF16)
            out_ref[rows, ATTN_W:ATTN_W + GMLP_W] = _rms(s["u"][rows, :] * s["ms"][rows, :], gog_ref[...])[0].astype(BF16)

    return _call(
        body, name="mix_fwd", grid=(t // BLK,),
        out_shape=[jax.ShapeDtypeStruct((t, D_MODEL), BF16)] + [jax.ShapeDtypeStruct(sh, F32) for sh in _kept_shapes(t)],
        in_specs=_mix_specs(), out_specs=[pl.BlockSpec((BLK, D_MODEL), lambda n: (n, 0))] + _kept_specs(),
        scratch_shapes=shapes,
        sem=("parallel",), comm=comm, args=(proj, proj, bias, sinks, lng, lnb, ws2, bfull, aog, gog))


def _mix_bwd(proj, bias, sinks, lng, lnb, ws2, wst2, bfull, aog, gog, dy, w_out, kept, comm):
    t = proj.shape[0]
    nb = t // BLK
    names, shapes = _mix_scratch(True)
    c_gu = ATTN_W + 2 * KV_W

    def body(proj_ref, kvp_ref, bias_ref, sinks_ref, lng_ref, lnb_ref, ws2_ref, bfull_ref, aog_ref, gog_ref,
             wst2_ref, dy_ref, wout_ref, p_ref, psink_ref, attn_ref,
             dproj_ref, dkvn_ref, dl_ref, dsink_ref, dlng_ref, dlnb_ref, dws_ref, dbs_ref, daog_ref, dgog_ref,
             *scratch):
        s = dict(zip(names, scratch), p=p_ref, psink=psink_ref, attn=attn_ref)
        n = pl.program_id(0)

        @pl.when(n == 0)
        def _():
            for r in (dl_ref, dsink_ref, dlng_ref, dlnb_ref, dws_ref, dbs_ref, daog_ref, dgog_ref):
                r[...] = jnp.zeros_like(r)

        s["dmix"][...] = _dot_nt(dy_ref[...], wout_ref[...])
        _mix_stage_kv(proj_ref, kvp_ref, s)
        _mask_heads(proj_ref, s["qm"])
        lng = lng_ref[...]
        _mix_stage_gmlp_pre(proj_ref, lng, lnb_ref[...], s, True)
        _mix_stage_gmlp_mix(ws2_ref, bfull_ref, s)

        aog, gog = aog_ref[...], gog_ref[...]
        for r0 in range(0, BLK, ROWS):
            rows = slice(r0, r0 + ROWS)
            attn, dma = s["attn"][rows, :], s["dmix"][rows, 0:ATTN_W]
            _, r_a = _rms(attn, aog)
            daog_ref[...] += _rowsum8(dma * attn * r_a)
            s["dattn"][rows, :] = _rms_bwd(dma, attn, r_a, aog)
            u, ms, dmg = s["u"][rows, :], s["ms"][rows, :], s["dmix"][rows, ATTN_W:ATTN_W + GMLP_W]
            gm = u * ms
            _, r_g = _rms(gm, gog)
            dgog_ref[...] += _rowsum8(dmg * gm * r_g)
            dgm = _rms_bwd(dmg, gm, r_g, gog)
            dproj_ref[rows, c_gu:c_gu + GMLP_W] = dgm * ms * s["du"][rows, :]
            dms = dgm * u
            dbs_ref[rows, :] += dms
            s["dmsb"][rows, :] = dms.astype(BF16)

        _mask_heads(s["dattn"], s["dom"])
        for g in range(2):
            s["lg"][g] = _dot_nt(_group_rows(s["dom"], g), s["vd"][g])
        lane = lax.broadcasted_iota(jnp.int32, (HALF, 128), 1)
        for hf in range(BLK // HALF):
            rows = slice(HALF * hf, HALF * (hf + 1))
            dsink = jnp.zeros((HALF, 128), F32)
            for h in range(N_HEADS):
                grows = slice(BLK * (h % 4) + HALF * hf, BLK * (h % 4) + HALF * (hf + 1))
                dp = s["lg"][h // 4, grows, :]
                p = s["p"][h, rows, :]
                s["pb"][h, rows, :] = p.astype(BF16)
                rs = jnp.sum(p * dp, axis=1, keepdims=True)
                dl = p * (dp - rs)
                dl_ref[h, rows, :] += dl
                dsink = dsink + jnp.where(lane == h, -s["psink"][rows, :] * rs, 0.0)
                s["dls"][h, rows, :] = (dl * (HEAD_DIM ** -0.5)).astype(BF16)
            dsink_ref[rows, :] += dsink
        for g in range(2):
            dq = _dot(_pair_rows(s["dls"], g), s["km"][g])
            dproj_ref[:, 256 * g:256 * g + 128] = dq[0:BLK]
            dproj_ref[:, 256 * g + 128:256 * g + 256] = dq[BLK:2 * BLK]
        lo_k = _lane_lo(2 * BLK)
        for col, lhs, rhs in ((0, "dls", "qm"), (KV_W, "pb", "dom")):
            raw = [_dot_tn(_group_rows(s[lhs], g), _group_rows(s[rhs], g)) for g in range(2)]
            both = [r + pltpu.roll(r, 64, 1) for r in raw]
            dkv = jnp.where(lo_k, both[0], both[1])
            dproj_ref[:, ATTN_W + col:ATTN_W + col + KV_W] = dkv[BLK:2 * BLK]
            dkvn_ref[:, col:col + KV_W] = dkv[0:BLK]

        for j in range(4):
            cols = slice(128 * j, 128 * (j + 1))
            dm2 = _stack_halves(s["dmsb"][:, cols])
            vnb = s["vnb"][:, cols]
            dws2 = _dot_nt(dm2, vnb)
            dws_ref[2 * j] += dws2[0:BLK]
            dws_ref[2 * j + 1] += dws2[BLK:2 * BLK]
            s["dvn"][:, cols] = _dot(wst2_ref[j], dm2)
        for r0 in range(0, BLK, ROWS):
            rows = slice(r0, r0 + ROWS)
            dvn, vhat = s["dvn"][rows, :], s["vhat"][rows, :]
            dlng_ref[...] += _rowsum8(dvn * vhat)
            dlnb_ref[...] += _rowsum8(dvn)
            dvh = dvn * lng
            dact = s["rstd"][rows, :] * (dvh - _seg_mean64(dvh) - vhat * _seg_mean64(dvh * vhat))
            dproj_ref[rows, c_gu + GMLP_W:IN_W] = dact * s["da"][rows, :]

    acc8 = lambda w: jax.ShapeDtypeStruct((8, w), F32)
    out_shape = [jax.ShapeDtypeStruct((t, IN_W), F32), jax.ShapeDtypeStruct((t, 2 * KV_W), F32),
                 jax.ShapeDtypeStruct((N_HEADS, BLK, 2 * BLK), F32), jax.ShapeDtypeStruct((BLK, 128), F32),
                 acc8(GMLP_W), acc8(GMLP_W), jax.ShapeDtypeStruct((N_GROUPS, BLK, BLK), F32),
                 jax.ShapeDtypeStruct((BLK, GMLP_W), F32), acc8(ATTN_W), acc8(GMLP_W)]
    out_specs = [pl.BlockSpec((BLK, IN_W), lambda n: (n, 0)),
                 pl.BlockSpec((BLK, 2 * KV_W), lambda n: ((n + nb - 1) % nb, 0)),
                 _full((N_HEADS, BLK, 2 * BLK)), _full((BLK, 128)), _full((8, GMLP_W)), _full((8, GMLP_W)),
                 _full((N_GROUPS, BLK, BLK)), _full((BLK, GMLP_W)), _full((8, ATTN_W)), _full((8, GMLP_W))]
    in_specs = _mix_specs() + [_full((N_GROUPS // 2, BLK, 2 * BLK)),
                               pl.BlockSpec((BLK, D_MODEL), lambda n: (n, 0)),
                               _full((D_MODEL, D_MODEL))] + _kept_specs()
    return _call(
        body, name="mix_bwd", grid=(nb,), out_shape=out_shape, in_specs=in_specs, out_specs=out_specs,
        scratch_shapes=shapes, sem=("arbitrary",), comm=comm,
        args=(proj, proj, bias, sinks, lng, lnb, ws2, bfull, aog, gog, wst2, dy, w_out, *kept))


def _outproj(mixed, w_out, x, g1, ln1g, ln1b, sc2, sh2, tm, comm):
    t, d = x.shape

    def body(mx_ref, w_ref, x_ref, g1_ref, lg_ref, lb_ref, sc_ref, sh_ref, y_ref, x1_ref, h2_ref):
        y = _dot(mx_ref[...], w_ref[...])
        xhat, _ = _ln_stats(ALPHA * x_ref[...] + g1_ref[...] * y)
        x1 = xhat * lg_ref[...] + lb_ref[...]
        y_ref[...] = y
        x1_ref[...] = x1
        h2_ref[...] = (x1 * (1.0 + sc_ref[...]) + sh_ref[...]).astype(BF16)

    row = pl.BlockSpec((tm, d), lambda i: (i, 0))
    vec = _full((1, d))
    return _call(
        body, name="outproj", grid=(t // tm,),
        out_shape=[jax.ShapeDtypeStruct((t, d), F32), jax.ShapeDtypeStruct((t, d), F32),
                   jax.ShapeDtypeStruct((t, d), BF16)],
        in_specs=[row, _full((d, d)), row, vec, vec, vec, vec, vec], out_specs=[row, row, row],
        sem=("parallel",), comm=comm, args=(mixed, w_out, x, g1, ln1g, ln1b, sc2, sh2))


def _ffn_fwd(h2, w_gu_t, w_down, x1, target, g2, ln2g, ln2b, tm):
    t, d = x1.shape

    def body(h_ref, w_ref, wd_ref, x1_ref, tg_ref, g2_ref, lg_ref, lb_ref,
             dsu_ref, sg_ref, act_ref, dz_ref, dy_ref, loss_ref, dlg_ref, dlb_ref, dg2_ref):
        @pl.when(pl.program_id(0) == 0)
        def _():
            for r in (loss_ref, dlg_ref, dlb_ref, dg2_ref):
                r[...] = jnp.zeros_like(r)

        h = h_ref[...]
        g = _dot_nt(h, w_ref[0:D_FF])
        u = _dot_nt(h, w_ref[D_FF:2 * D_FF])
        s = _sigmoid(g)
        sg = g * s
        act = (sg * u).astype(BF16)
        dsu_ref[...] = (u * (s * (1.0 + g * (1.0 - s)))).astype(BF16)
        sg_ref[...] = sg.astype(BF16)
        act_ref[...] = act
        y2 = _dot(act, wd_ref[...])
        g2 = g2_ref[...]
        lg = lg_ref[...]
        xhat, rstd = _ln_stats(ALPHA * x1_ref[...] + g2 * y2)
        err = xhat * lg + lb_ref[...] - tg_ref[...]
        loss_ref[...] += _rowsum8(err * err)
        dx2 = err * (1.0 / d)
        dlg_ref[...] += _rowsum8(dx2 * xhat)
        dlb_ref[...] += _rowsum8(dx2)
        dz = _ln_bwd(dx2 * lg, xhat, rstd)
        dg2_ref[...] += _rowsum8(dz * y2)
        dz_ref[...] = dz
        dy_ref[...] = (g2 * dz).astype(BF16)

    row = pl.BlockSpec((tm, d), lambda i: (i, 0))
    wide = pl.BlockSpec((tm, D_FF), lambda i: (i, 0))
    vec = _full((1, d))
    acc = _full((8, d))
    acc_shape = jax.ShapeDtypeStruct((8, d), F32)
    wide_shape = jax.ShapeDtypeStruct((t, D_FF), BF16)
    return pl.pallas_call(
        body, name="ffn_fwd", grid=(t // tm,),
        out_shape=[wide_shape] * 3 + [jax.ShapeDtypeStruct((t, d), F32), jax.ShapeDtypeStruct((t, d), BF16)]
        + [acc_shape] * 4,
        in_specs=[row, _resident((2 * D_FF, d)), _resident((D_FF, d)), row, row, vec, vec, vec],
        out_specs=[wide] * 3 + [row, row, acc, acc, acc, acc], compiler_params=_params(("arbitrary",)),
    )(h2, w_gu_t, w_down, x1, target, g2, ln2g, ln2b)


def _resident(shape):
    nd = len(shape)
    return pl.BlockSpec(shape, lambda *_: (0,) * nd, pipeline_mode=pl.Buffered(1))


def _ffn_bwd(dy2, w_down, dsu, sg, w_gu_t, x1, x, y, dz2, sc2, g1, ln1g, tm, comm):
    t, d = x1.shape

    def body(dy2_ref, wd_ref, dsu_ref, sg_ref, w_ref, x1_ref, x_ref, y_ref, dz2_ref, sc_ref, g1_ref, lg_ref,
             dg_ref, du_ref, dz1_ref, dy_ref, dsc_ref, dsh_ref, dlg_ref, dlb_ref, dg1_ref):
        @pl.when(pl.program_id(0) == 0)
        def _():
            for r in (dsc_ref, dsh_ref, dlg_ref, dlb_ref, dg1_ref):
                r[...] = jnp.zeros_like(r)

        dact = _dot_nt(dy2_ref[...], wd_ref[...])
        dg = (dact * dsu_ref[...].astype(F32)).astype(BF16)
        du = (dact * sg_ref[...].astype(F32)).astype(BF16)
        dg_ref[...] = dg
        du_ref[...] = du
        dh2 = _dot(dg, w_ref[0:D_FF]) + _dot(du, w_ref[D_FF:2 * D_FF])
        x1 = x1_ref[...]
        y = y_ref[...]
        g1 = g1_ref[...]
        dsc_ref[...] += _rowsum8(dh2 * x1)
        dsh_ref[...] += _rowsum8(dh2)
        dx1 = dh2 * (1.0 + sc_ref[...]) + ALPHA * dz2_ref[...]
        xhat, rstd = _ln_stats(ALPHA * x_ref[...] + g1 * y)
        dlg_ref[...] += _rowsum8(dx1 * xhat)
        dlb_ref[...] += _rowsum8(dx1)
        dz1 = _ln_bwd(dx1 * lg_ref[...], xhat, rstd)
        dg1_ref[...] += _rowsum8(dz1 * y)
        dz1_ref[...] = dz1
        dy_ref[...] = (g1 * dz1).astype(BF16)

    row = pl.BlockSpec((tm, d), lambda i: (i, 0))
    wide = pl.BlockSpec((tm, D_FF), lambda i: (i, 0))
    vec = _full((1, d))
    acc = _full((8, d))
    acc_shape = jax.ShapeDtypeStruct((8, d), F32)
    wide_shape = jax.ShapeDtypeStruct((t, D_FF), BF16)
    return _call(
        body, name="ffn_bwd", grid=(t // tm,),
        out_shape=[wide_shape, wide_shape, jax.ShapeDtypeStruct((t, d), F32), jax.ShapeDtypeStruct((t, d), BF16)]
        + [acc_shape] * 5,
        in_specs=[row, _resident((D_FF, d)), wide, wide, _resident((2 * D_FF, d)), row, row, row, row, vec, vec, vec],
        out_specs=[wide, wide, row, row, acc, acc, acc, acc, acc], sem=("arbitrary",), comm=comm,
        args=(dy2, w_down, dsu, sg, w_gu_t, x1, x, y, dz2, sc2, g1, ln1g))


def _din(dproj, dkvn, w_in_t, x, dz1, sc1, tm, comm):
    t, d = x.shape

    def body(dp_ref, dkv_ref, w_ref, x_ref, dz1_ref, sc_ref, dx_ref, dpb_ref, dbin_ref, dsc_ref, dsh_ref):
        @pl.when(pl.program_id(0) == 0)
        def _():
            for r in (dbin_ref, dsc_ref, dsh_ref):
                r[...] = jnp.zeros_like(r)

        dp = jnp.concatenate([dp_ref[:, 0:ATTN_W], dp_ref[:, ATTN_W:ATTN_W + 2 * KV_W] + dkv_ref[...],
                              dp_ref[:, ATTN_W + 2 * KV_W:IN_W]], axis=1)
        dbin_ref[...] += _rowsum8(dp)
        dpb = dp.astype(BF16)
        dpb_ref[...] = dpb
        dh = _dot(dpb, w_ref[...])
        dsc_ref[...] += _rowsum8(dh * x_ref[...])
        dsh_ref[...] += _rowsum8(dh)
        dx_ref[...] = dh * (1.0 + sc_ref[...]) + ALPHA * dz1_ref[...]

    row = lambda w: pl.BlockSpec((tm, w), lambda i: (i, 0))
    return _call(
        body, name="din", grid=(t // tm,),
        out_shape=[jax.ShapeDtypeStruct((t, d), F32), jax.ShapeDtypeStruct((t, IN_W), BF16),
                   jax.ShapeDtypeStruct((8, IN_W), F32), jax.ShapeDtypeStruct((8, d), F32),
                   jax.ShapeDtypeStruct((8, d), F32)],
        in_specs=[row(IN_W), row(2 * KV_W), _full((IN_W, d)), row(d), row(d), _full((1, d))],
        out_specs=[row(d), row(IN_W), _full((8, IN_W)), _full((8, d)), _full((8, d))],
        sem=("arbitrary",), comm=comm, args=(dproj, dkvn, w_in_t, x, dz1, sc1))


def _wgrad(name, a, b, tmm, tk, comm=None, a2=None):
    t, m = a.shape
    n = b.shape[1]
    nk = t // tk
    nm = m // tmm

    def body(*refs):
        a_refs, (b_ref, o_ref, acc_ref) = refs[:-3], refs[-3:]
        i, k = pl.program_id(0), pl.program_id(1)

        @pl.when(k == 0)
        def _():
            acc_ref[...] = jnp.zeros_like(acc_ref)

        a_tile = a_refs[0][...] if a2 is None else jnp.where(i < nm, a_refs[0][...], a_refs[1][...])
        acc_ref[...] += _dot_tn(a_tile, b_ref[...])

        @pl.when(k == nk - 1)
        def _():
            o_ref[...] = acc_ref[...].astype(BF16)

    if a2 is None:
        a_specs, a_args, n_tiles = [pl.BlockSpec((tk, tmm), lambda i, k: (k, i))], (a,), nm
    else:
        a_specs = [pl.BlockSpec((tk, tmm), lambda i, k: (jnp.where(i < nm, k, 0), jnp.minimum(i, nm - 1))),
                   pl.BlockSpec((tk, tmm), lambda i, k: (jnp.where(i < nm, 0, k), jnp.maximum(i - nm, 0)))]
        a_args, n_tiles = (a, a2), 2 * nm
    (out,), got = _call(
        body, name=name, grid=(n_tiles, nk), out_shape=[jax.ShapeDtypeStruct((n_tiles * tmm, n), BF16)],
        in_specs=a_specs + [pl.BlockSpec((tk, n), lambda i, k: (k, 0))],
        out_specs=[pl.BlockSpec((tmm, n), lambda i, k: (i, 0))],
        scratch_shapes=[pltpu.VMEM((tmm, n), F32)], sem=("parallel", "arbitrary"), comm=comm, args=a_args + (b,))
    return out if comm is None else (out, got)


def _adamw(w, g, m, v):
    m = ADAM_B1 * m + (1.0 - ADAM_B1) * g
    v = ADAM_B2 * v + (1.0 - ADAM_B2) * (g * g)
    m_hat = m / (1.0 - ADAM_B1 ** ADAM_STEP)
    v_hat = v / (1.0 - ADAM_B2 ** ADAM_STEP)
    delta = -ADAM_LR * (m_hat / (jnp.sqrt(v_hat) + ADAM_EPS) + ADAM_WD * w)
    return delta, m, v


def _adam_reduce(name, parts, w, m, v, tr):
    r, cdim = w.shape

    def body(p_ref, w_ref, m_ref, v_ref, g_ref, d_ref, mo_ref, vo_ref):
        g = p_ref[0].astype(F32)
        for s in range(1, N_DEV):
            g = g + p_ref[s].astype(F32)
        d_ref[...], mo_ref[...], vo_ref[...] = _adamw(w_ref[...], g, m_ref[...], v_ref[...])
        g_ref[...] = g

    tile = pl.BlockSpec((tr, cdim), lambda i: (i, 0))
    shp = jax.ShapeDtypeStruct((r, cdim), F32)
    return pl.pallas_call(
        body, name=name, grid=(r // tr,), out_shape=[shp] * 4,
        in_specs=[pl.BlockSpec((N_DEV, tr, cdim), lambda i: (0, i, 0)), tile, tile, tile],
        out_specs=[tile] * 4, compiler_params=_params(("parallel",)),
    )(parts, w, m, v)


def _adam_w_ada(c_all_t, dmod_cols, w, m, v):
    def body(ct_ref, dm_ref, w_ref, m_ref, v_ref, g_ref, d_ref, mo_ref, vo_ref):
        ct = ct_ref[...]
        s = (ct * _sigmoid(ct)).astype(BF16)
        g = _dot(s, dm_ref[...].astype(BF16))
        d_ref[...], mo_ref[...], vo_ref[...] = _adamw(w_ref[...], g, m_ref[...], v_ref[...])
        g_ref[...] = g

    shp = jax.ShapeDtypeStruct(w.shape, F32)
    return pl.pallas_call(
        body, name="adam_w_ada", grid=(1,), out_shape=[shp] * 4,
        in_specs=[_full(c_all_t.shape), _full(dmod_cols.shape)] + [_full(w.shape)] * 3,
        out_specs=[_full(w.shape)] * 4, compiler_params=_params(("arbitrary",)),
    )(c_all_t, dmod_cols, w, m, v)


SMALL_EARLY = ["rel_bias", "attn_sinks", "gmlp_ln_g", "gmlp_ln_b", "gmlp_w_s", "gmlp_b_s",
               "attn_out_g", "gmlp_out_g", "ln1_g", "ln1_b", "ln2_g", "ln2_b"]
SMALL_LATE = ["b_ada", "b_in", "loss"]
WEIGHTS = ["rel_bias", "w_ada", "b_ada", "w_in", "b_in", "attn_sinks", "gmlp_ln_g", "gmlp_ln_b", "gmlp_w_s",
           "gmlp_b_s", "attn_out_g", "gmlp_out_g", "w_out", "ln1_g", "ln1_b", "w_gate_up", "w_down", "ln2_g", "ln2_b"]


def _seg_rows(nelem):
    return -(-nelem // 1024) * 8


def _pack(named, names):
    parts = []
    for name in names:
        flat = named[name].reshape(-1).astype(F32)
        rows = _seg_rows(flat.shape[0])
        parts.append(jnp.pad(flat, (0, rows * 128 - flat.shape[0])).reshape(rows, 128))
    return jnp.concatenate(parts, axis=0)


def _unpack(packed, shapes, names):
    out, r0 = {}, 0
    for name in names:
        nelem = math.prod(shapes[name])
        rows = _seg_rows(nelem)
        out[name] = packed[r0:r0 + rows].reshape(-1)[:nelem].reshape(shapes[name])
        r0 += rows
    return out


def _t5_bucket_map():
    qi = jnp.arange(BLK)[:, None]
    si = jnp.arange(2 * BLK)[None, :]
    n = jnp.maximum(qi + BLK - si, 0)
    max_exact = N_BUCKETS // 2
    nf = jnp.maximum(n, max_exact).astype(F32)
    large = max_exact + (jnp.log(nf / max_exact) / math.log(MAX_DISTANCE / max_exact)
                         * (N_BUCKETS - max_exact)).astype(jnp.int32)
    large = jnp.minimum(large, N_BUCKETS - 1)
    return jnp.where(n < max_exact, n, large).astype(jnp.int32)


def kernel(x, c, rel_bias, w_ada, b_ada, w_in, b_in, attn_sinks, gmlp_ln_g, gmlp_ln_b, gmlp_w_s, gmlp_b_s, attn_out_g, gmlp_out_g, w_out, ln1_g, ln1_b, w_gate_up, w_down, ln2_g, ln2_b, loss_target, m_rel_bias, m_w_ada, m_b_ada, m_w_in, m_b_in, m_attn_sinks, m_gmlp_ln_g, m_gmlp_ln_b, m_gmlp_w_s, m_gmlp_b_s, m_attn_out_g, m_gmlp_out_g, m_w_out, m_ln1_g, m_ln1_b, m_w_gate_up, m_w_down, m_ln2_g, m_ln2_b, v_rel_bias, v_w_ada, v_b_ada, v_w_in, v_b_in, v_attn_sinks, v_gmlp_ln_g, v_gmlp_ln_b, v_gmlp_w_s, v_gmlp_b_s, v_attn_out_g, v_gmlp_out_g, v_w_out, v_ln1_g, v_ln1_b, v_w_gate_up, v_w_down, v_ln2_g, v_ln2_b):
    wts = dict(rel_bias=rel_bias, w_ada=w_ada, b_ada=b_ada, w_in=w_in, b_in=b_in, attn_sinks=attn_sinks,
               gmlp_ln_g=gmlp_ln_g, gmlp_ln_b=gmlp_ln_b, gmlp_w_s=gmlp_w_s, gmlp_b_s=gmlp_b_s,
               attn_out_g=attn_out_g, gmlp_out_g=gmlp_out_g, w_out=w_out, ln1_g=ln1_g, ln1_b=ln1_b,
               w_gate_up=w_gate_up, w_down=w_down, ln2_g=ln2_g, ln2_b=ln2_b)
    mom_m = dict(rel_bias=m_rel_bias, w_ada=m_w_ada, b_ada=m_b_ada, w_in=m_w_in, b_in=m_b_in,
                 attn_sinks=m_attn_sinks, gmlp_ln_g=m_gmlp_ln_g, gmlp_ln_b=m_gmlp_ln_b, gmlp_w_s=m_gmlp_w_s,
                 gmlp_b_s=m_gmlp_b_s, attn_out_g=m_attn_out_g, gmlp_out_g=m_gmlp_out_g, w_out=m_w_out,
                 ln1_g=m_ln1_g, ln1_b=m_ln1_b, w_gate_up=m_w_gate_up, w_down=m_w_down, ln2_g=m_ln2_g,
                 ln2_b=m_ln2_b)
    mom_v = dict(rel_bias=v_rel_bias, w_ada=v_w_ada, b_ada=v_b_ada, w_in=v_w_in, b_in=v_b_in,
                 attn_sinks=v_attn_sinks, gmlp_ln_g=v_gmlp_ln_g, gmlp_ln_b=v_gmlp_ln_b, gmlp_w_s=v_gmlp_w_s,
                 gmlp_b_s=v_gmlp_b_s, attn_out_g=v_attn_out_g, gmlp_out_g=v_gmlp_out_g, w_out=v_w_out,
                 ln1_g=v_ln1_g, ln1_b=v_ln1_b, w_gate_up=v_w_gate_up, w_down=v_w_down, ln2_g=v_ln2_g,
                 ln2_b=v_ln2_b)

    t = x.shape[1]
    tm = min(512, t)
    tn_ff = D_FF // 2
    tk_long, tk_short = min(4096, t), min(2048, t)
    me = 4 * lax.axis_index("x") + 2 * lax.axis_index("y") + lax.axis_index("c")
    xs = x[0]
    target = loss_target[0]

    c_g, w_in_g = _exchange("gather_in", [jnp.broadcast_to(c, (8, D_MODEL)), w_in[0].T.astype(BF16)],
                            ("gather", "gather2"))
    c_all = c_g[:, 0, :]
    w_in_t = w_in_g.reshape(IN_W, D_MODEL)

    ncol = w_ada.shape[2]
    b_cols = lax.dynamic_slice(b_ada, (0, me * ncol), (1, ncol))
    mod_part = _mod_partial(c_all, w_ada[0], b_cols)
    (mod_g,) = _exchange("gather_mod", [mod_part], ("gather",))
    mod = lax.dynamic_slice(mod_g, (0, me, 0), (N_DEV, 1, ncol)).reshape(1, N_DEV * ncol)
    sh1, sc1, g1, sh2, sc2, g2 = [mod[:, i * D_MODEL:(i + 1) * D_MODEL] for i in range(6)]

    bucket = _t5_bucket_map()
    bias = _bias_table(rel_bias, bucket)
    causal = jnp.tril(jnp.ones((BLK, BLK), dtype=bool))
    ws = jnp.where(causal[None], gmlp_w_s[0], 0.0).astype(BF16)
    pair = lambda w: jnp.concatenate([w[0::2], w[1::2]], axis=2)
    ws2, wst2 = pair(ws), pair(jnp.swapaxes(ws, 1, 2))
    bfull = jnp.repeat(gmlp_b_s[0].T, GMLP_W // N_GROUPS, axis=1)
    sinks = attn_sinks[0]

    proj, h1 = _inproj(xs, sc1, sh1, w_in_t, b_in, tm)
    (mixed, *kept), (w_out_g, w_gu_g) = _mix_fwd(
        proj, bias, sinks, gmlp_ln_g, gmlp_ln_b, ws2, bfull, attn_out_g, gmlp_out_g,
        comm=([w_out[0].astype(BF16), w_gate_up[0].T.astype(BF16)], ("gather2", "gather2")))
    w_out_f = w_out_g.reshape(D_MODEL, D_MODEL)
    w_gu_t = w_gu_g.reshape(2 * D_FF, D_MODEL)
    (y1, x1, h2), (w_down_g,) = _outproj(mixed, w_out_f, xs, g1, ln1_g, ln1_b, sc2, sh2, tm,
                                         comm=([w_down[0].astype(BF16)], ("gather2",)))
    w_down_f = w_down_g.reshape(D_FF, D_MODEL)
    dsu, sg, act, dz2, dy2, loss_p, d_ln2g, d_ln2b, d_g2 = _ffn_fwd(h2, w_gu_t, w_down_f, x1, target, g2, ln2_g, ln2_b,
                                                                    min(256, t))

    slots = lambda a: a.reshape(N_DEV, -1, D_MODEL)
    dw_down = _wgrad("wgrad_down", act, dy2, tn_ff, tk_short)
    (dgate, dup, dz1, dy1, d_sc2, d_sh2, d_ln1g, d_ln1b, d_g1), (r_down,) = _ffn_bwd(
        dy2, w_down_f, dsu, sg, w_gu_t, x1, xs, y1, dz2, sc2, g1, ln1_g, min(256, t),
        comm=([slots(dw_down)], ("scatter",)))
    dw_gu_t = _wgrad("wgrad_gate_up", dgate, h2, tn_ff, tk_short, a2=dup)
    dw_out = _wgrad("wgrad_out", mixed, dy1, D_MODEL, tk_long)
    ((dproj, dkvn, dl_acc, dsink_acc, d_lng, d_lnb, d_ws, d_bs, d_aog, d_gog), (r_gu, r_out)) = _mix_bwd(
        proj, bias, sinks, gmlp_ln_g, gmlp_ln_b, ws2, wst2, bfull, attn_out_g, gmlp_out_g, dy1, w_out_f, kept,
        comm=([slots(dw_gu_t), slots(dw_out)], ("scatter", "scatter")))
    d_relb = _bias_grad(dl_acc, bucket)

    rsum = lambda a: jnp.sum(a, axis=0)
    early_g = dict(
        rel_bias=d_relb[:, 0, :N_BUCKETS].T, attn_sinks=rsum(dsink_acc)[:N_HEADS],
        gmlp_ln_g=rsum(d_lng), gmlp_ln_b=rsum(d_lnb), gmlp_w_s=jnp.where(causal[None], d_ws, 0.0),
        gmlp_b_s=jnp.sum(d_bs.reshape(BLK, N_GROUPS, GMLP_W // N_GROUPS), axis=2).T,
        attn_out_g=rsum(d_aog), gmlp_out_g=rsum(d_gog), ln1_g=rsum(d_ln1g), ln1_b=rsum(d_ln1b),
        ln2_g=rsum(d_ln2g), ln2_b=rsum(d_ln2b))
    (grad_x, dproj_b, d_bin, d_sc1, d_sh1), _ = _din(dproj, dkvn, w_in_t, xs, dz1, sc1, tm, comm=None)
    dw_in_t, (early_all,) = _wgrad("wgrad_in", dproj_b, h1, IN_W // 2, tk_long,
                                   comm=([_pack(early_g, SMALL_EARLY)], ("gather2",)))
    dmod = jnp.concatenate([rsum(d_sh1), rsum(d_sc1), rsum(d_g1), rsum(d_sh2), rsum(d_sc2), rsum(d_g2)])
    late_g = dict(b_ada=dmod, b_in=rsum(d_bin), loss=(0.5 / D_MODEL * jnp.sum(loss_p)).reshape(1))
    late_all, r_in = _exchange("scatter_in", [_pack(late_g, SMALL_LATE), slots(dw_in_t)], ("gather", "scatter"))

    small = [{}, {}, {}, {}]
    for label, names, parts in (("adam_small_early", SMALL_EARLY, early_all), ("adam_small_late", SMALL_LATE, late_all)):
        with_loss = lambda tree: dict(tree, loss=jnp.zeros((1,), F32))
        res = _adam_reduce(label, parts, _pack(with_loss(wts), names), _pack(with_loss(mom_m), names),
                           _pack(with_loss(mom_v), names), parts.shape[1])
        shapes = {k: with_loss(wts)[k].shape for k in names}
        for i in range(4):
            small[i].update(_unpack(res[i], shapes, names))
    loss = small[0]["loss"].reshape(())

    dmod_all = late_all[:, :_seg_rows(6 * D_MODEL), :].reshape(N_DEV, 6 * D_MODEL)
    dmod_cols = lax.dynamic_slice(dmod_all, (0, me * ncol), (N_DEV, ncol))
    kpad = 128 - N_DEV
    ada = _adam_w_ada(jnp.pad(c_all.T, ((0, 0), (0, kpad))), jnp.pad(dmod_cols, ((0, kpad), (0, 0))),
                      w_ada[0], m_w_ada[0], v_w_ada[0])

    tr = lambda a: jnp.swapaxes(a, -1, -2)
    big = {}
    big["w_in"] = [tr(o)[None] for o in _adam_reduce("adam_w_in", r_in, w_in[0].T, m_w_in[0].T, v_w_in[0].T, 112)]
    big["w_out"] = [o[None] for o in _adam_reduce("adam_w_out", r_out, w_out[0], m_w_out[0], v_w_out[0], 128)]
    big["w_gate_up"] = [tr(o)[None] for o in _adam_reduce("adam_w_gu", r_gu, w_gate_up[0].T, m_w_gate_up[0].T,
                                                           v_w_gate_up[0].T, 352)]
    big["w_down"] = [o[None] for o in _adam_reduce("adam_w_down", r_down, w_down[0], m_w_down[0], v_w_down[0], 176)]
    big["w_ada"] = [o[None] for o in ada]

    outs = [[], [], [], []]
    for name in WEIGHTS:
        for i in range(4):
            outs[i].append(big[name][i] if name in big else small[i][name])
    return (loss, grad_x[None], *outs[0], *outs[1], *outs[2], *outs[3])
```

```python
import math

import jax
import jax.numpy as jnp
from jax import lax
from jax.experimental import pallas as pl
from jax.experimental.pallas import tpu as pltpu

F32 = jnp.float32
BF16 = jnp.bfloat16
MESH = pl.DeviceIdType.MESH

N_DEV = 8
D_MODEL = 1024
HEAD_DIM = 64
N_HEADS = 8
N_GROUPS = 8
ATTN_W = 512
KV_W = 128
GMLP_W = 512
IN_W = 1792
BLK = 128
N_BUCKETS = 32
MAX_DISTANCE = 128
D_FF = 2816
ALPHA = 2.0 ** 0.25
LN_EPS = 1e-5
NEG_INF = -1e30
ADAM_LR = 0.001
ADAM_B1 = 0.9
ADAM_B2 = 0.999
ADAM_EPS = 1e-08
ADAM_WD = 0.01
ADAM_STEP = 10
GELU_C0 = math.sqrt(2.0 / math.pi)
GELU_C1 = 0.044715

VMEM_LIMIT = 56 * 1024 * 1024


def _params(sem):
    return pltpu.CompilerParams(dimension_semantics=sem, vmem_limit_bytes=VMEM_LIMIT)


def _dot(a, b):
    return lax.dot_general(a, b, (((1,), (0,)), ((), ())), preferred_element_type=F32)


def _dot_nt(a, b):
    return lax.dot_general(a, b, (((1,), (1,)), ((), ())), preferred_element_type=F32)


def _dot_tn(a, b):
    return lax.dot_general(a, b, (((0,), (0,)), ((), ())), preferred_element_type=F32)


def _full(shape):
    nd = len(shape)
    return pl.BlockSpec(shape, lambda *_: (0,) * nd)


def _rowsum8(v):
    r, c = v.shape
    return jnp.sum(v.reshape(r // 8, 8, c), axis=0)


def _sigmoid(v):
    return 1.0 / (1.0 + jnp.exp(-v))


def _gelu_parts(v):
    v2 = v * v
    t = jnp.tanh(GELU_C0 * (v + GELU_C1 * v * v2))
    g = 0.5 * v * (1.0 + t)
    dg = 0.5 * (1.0 + t) + 0.5 * v * (1.0 - t * t) * (GELU_C0 * (1.0 + 3.0 * GELU_C1 * v2))
    return g, dg


def _ln_stats(z):
    mu = jnp.mean(z, axis=1, keepdims=True)
    zc = z - mu
    var = jnp.mean(zc * zc, axis=1, keepdims=True)
    rstd = lax.rsqrt(var + LN_EPS)
    return zc * rstd, rstd


def _ln_bwd(dxhat, xhat, rstd):
    m1 = jnp.mean(dxhat, axis=1, keepdims=True)
    m2 = jnp.mean(dxhat * xhat, axis=1, keepdims=True)
    return rstd * (dxhat - m1 - xhat * m2)


def _seg_mean64(v):
    r = v.shape[0]
    lo = lax.broadcasted_iota(jnp.int32, (r, 128), 1) < 64
    outs = []
    for j in range(v.shape[1] // 128):
        ch = v[:, 128 * j:128 * (j + 1)]
        s_lo = jnp.sum(jnp.where(lo, ch, 0.0), axis=1, keepdims=True)
        s_hi = jnp.sum(jnp.where(lo, 0.0, ch), axis=1, keepdims=True)
        outs.append(jnp.where(lo, s_lo, s_hi) * (1.0 / 64.0))
    return jnp.concatenate(outs, axis=1)


def _rms(a, g):
    r = lax.rsqrt(jnp.mean(a * a, axis=1, keepdims=True) + LN_EPS)
    return a * r * g, r


def _rms_bwd(dout, a, r, g):
    t = dout * g
    return r * t - a * (r * r * r) * jnp.mean(t * a, axis=1, keepdims=True)


PEER_ORDER = (1, 2, 4, 3, 5, 6, 7)


def _peer(j):
    x, y, c = lax.axis_index("x"), lax.axis_index("y"), lax.axis_index("c")
    px = 1 - x if j & 4 else x
    py = 1 - y if j & 2 else y
    pc = 1 - c if j & 1 else c
    return (px, py, pc), 4 * px + 2 * py + pc


SIBLING = 1
CHIP_FLIPS = (4, 2, 6)


def _exchange_phase(phase, ins, outs, modes, send_sems, recv_sems, loc_sems):
    me = 4 * lax.axis_index("x") + 2 * lax.axis_index("y") + lax.axis_index("c")
    for k, mode in enumerate(modes):
        def copy(i, src, slot, dev, k=k):
            return pltpu.make_async_remote_copy(src_ref=src, dst_ref=outs[k].at[slot], send_sem=send_sems.at[k, i],
                                                recv_sem=recv_sems.at[k, i], device_id=dev, device_id_type=MESH)

        src_me = ins[k].at[me] if mode == "scatter" else ins[k]
        local = pltpu.make_async_copy(src_me, outs[k].at[me], loc_sems.at[k])
        if mode == "gather2":
            sib_dev, sib_idx = _peer(SIBLING)
            chips = [_peer(j) for j in CHIP_FLIPS]
            far = [_peer(j | SIBLING)[1] for j in CHIP_FLIPS]
            if phase == "start":
                local.start()
                copy(0, ins[k], me, sib_dev).start()
                for i, (dev, _) in enumerate(chips):
                    copy(1 + i, ins[k], me, dev).start()
            elif phase == "mid":
                for i, (dev, idx) in enumerate(chips):
                    copy(1 + i, ins[k], idx, dev).wait_recv()
                    copy(4 + i, outs[k].at[idx], idx, sib_dev).start()
            else:
                copy(0, ins[k], sib_idx, sib_dev).wait_recv()
                for i, slot in enumerate(far):
                    copy(4 + i, ins[k], slot, sib_dev).wait_recv()
                copy(0, ins[k], me, sib_dev).wait_send()
                for i, (dev, idx) in enumerate(chips):
                    copy(1 + i, ins[k], me, dev).wait_send()
                    copy(4 + i, outs[k].at[idx], idx, sib_dev).wait_send()
                local.wait()
            continue
        peers = [_peer(j) for j in PEER_ORDER]
        if phase == "start":
            local.start()
            for i, (dev, idx) in enumerate(peers):
                copy(i, ins[k].at[idx] if mode == "scatter" else ins[k], me, dev).start()
        elif phase == "end":
            for i, (dev, idx) in enumerate(peers):
                copy(i, src_me, idx, dev).wait_recv()
            for i, (dev, idx) in enumerate(peers):
                copy(i, src_me, me, dev).wait_send()
            local.wait()


def _exchange_shapes(arrays, modes):
    return [jax.ShapeDtypeStruct((N_DEV,) + (a.shape[1:] if m == "scatter" else a.shape), a.dtype)
            for a, m in zip(arrays, modes)]


def _exchange_sems(n):
    return [pltpu.SemaphoreType.DMA((n, N_DEV - 1)), pltpu.SemaphoreType.DMA((n, N_DEV - 1)),
            pltpu.SemaphoreType.DMA((n,))]


def _exchange(name, arrays, modes):
    n = len(arrays)

    def body(*refs):
        for phase in ("start", "mid", "end"):
            _exchange_phase(phase, refs[:n], refs[n:2 * n], modes, *refs[2 * n:])

    any_spec = pl.BlockSpec(memory_space=pl.ANY)
    return pl.pallas_call(
        body, name=name, out_shape=_exchange_shapes(arrays, modes),
        in_specs=[any_spec] * n, out_specs=[any_spec] * n, scratch_shapes=_exchange_sems(n),
    )(*arrays)


def _call(body, *, name, grid, in_specs, out_specs, out_shape, args, sem, scratch_shapes=(), comm=None):
    if comm is None:
        outs = pl.pallas_call(body, name=name, grid=grid, in_specs=list(in_specs), out_specs=list(out_specs),
                              out_shape=list(out_shape), scratch_shapes=list(scratch_shapes),
                              compiler_params=_params(sem))(*args)
        return list(outs), []
    arrays, modes = comm
    n_in, n_out, nc, ns = len(in_specs), len(out_specs), len(arrays), len(scratch_shapes)
    n_steps = math.prod(grid)

    def hosted(*refs):
        ins, cins = refs[:n_in], refs[n_in:n_in + nc]
        outs, couts = refs[n_in + nc:n_in + nc + n_out], refs[n_in + nc + n_out:n_in + 2 * nc + n_out]
        scratch = refs[n_in + 2 * nc + n_out:]
        ex = (cins, couts, modes) + tuple(scratch[ns:])
        step = pl.program_id(0)
        for ax in range(1, len(grid)):
            step = step * grid[ax] + pl.program_id(ax)

        @pl.when(step == 0)
        def _():
            _exchange_phase("start", *ex)

        body(*ins, *outs, *scratch[:ns])

        if "gather2" in modes:
            @pl.when(step == (3 * n_steps) // 4)
            def _():
                _exchange_phase("mid", *ex)

        @pl.when(step == n_steps - 1)
        def _():
            _exchange_phase("end", *ex)

    any_spec = pl.BlockSpec(memory_space=pl.ANY)
    res = pl.pallas_call(
        hosted, name=name, grid=grid, in_specs=list(in_specs) + [any_spec] * nc,
        out_specs=list(out_specs) + [any_spec] * nc, out_shape=list(out_shape) + _exchange_shapes(arrays, modes),
        scratch_shapes=list(scratch_shapes) + _exchange_sems(nc),
        compiler_params=_params(tuple("arbitrary" for _ in grid)))(*args, *arrays)
    return list(res[:n_out]), list(res[n_out:])


def _mod_partial(c_all, w_ada, b_ada_cols):
    def body(c_ref, w_ref, b_ref, o_ref):
        cv = c_ref[...]
        s = (cv * _sigmoid(cv)).astype(BF16)
        o_ref[...] = _dot(s, w_ref[...].astype(BF16)) + b_ref[...]

    ncol = w_ada.shape[1]
    return pl.pallas_call(
        body, name="mod_partial", out_shape=jax.ShapeDtypeStruct((N_DEV, ncol), F32),
        in_specs=[_full(c_all.shape), _full(w_ada.shape), _full(b_ada_cols.shape)],
        out_specs=_full((N_DEV, ncol)), grid=(1,), compiler_params=_params(("arbitrary",)),
    )(c_all, w_ada, b_ada_cols)


def _bias_table(rel_bias, bucket):
    def body(rb_ref, bk_ref, o_ref):
        h = pl.program_id(0)
        bk = bk_ref[...]
        acc = jnp.zeros((BLK, 2 * BLK), F32)
        for b in range(N_BUCKETS):
            acc = jnp.where(bk == b, rb_ref[b, h], acc)
        dist = (lax.broadcasted_iota(jnp.int32, (BLK, 2 * BLK), 0) + BLK
                - lax.broadcasted_iota(jnp.int32, (BLK, 2 * BLK), 1))
        o_ref[0] = jnp.where((dist >= 0) & (dist < BLK), acc, NEG_INF)

    return pl.pallas_call(
        body, name="bias_table", out_shape=jax.ShapeDtypeStruct((N_HEADS, BLK, 2 * BLK), F32),
        in_specs=[pl.BlockSpec(memory_space=pltpu.SMEM), _full((BLK, 2 * BLK))],
        out_specs=pl.BlockSpec((1, BLK, 2 * BLK), lambda h: (h, 0, 0)), grid=(N_HEADS,),
        compiler_params=_params(("arbitrary",)),
    )(rel_bias, bucket)


def _bias_grad(dl_acc, bucket):
    def body(dl_ref, bk_ref, o_ref):
        bk = bk_ref[...]
        dl = dl_ref[0]
        lane = lax.broadcasted_iota(jnp.int32, (1, 128), 1)
        row = jnp.zeros((1, 128), F32)
        for b in range(N_BUCKETS):
            s = jnp.sum(jnp.sum(jnp.where(bk == b, dl, 0.0), axis=1, keepdims=True), axis=0, keepdims=True)
            row = jnp.where(lane == b, s, row)
        o_ref[0] = row

    return pl.pallas_call(
        body, name="bias_grad", out_shape=jax.ShapeDtypeStruct((N_HEADS, 1, 128), F32),
        in_specs=[pl.BlockSpec((1, BLK, 2 * BLK), lambda h: (h, 0, 0)), _full((BLK, 2 * BLK))],
        out_specs=pl.BlockSpec((1, 1, 128), lambda h: (h, 0, 0)), grid=(N_HEADS,),
        compiler_params=_params(("arbitrary",)),
    )(dl_acc, bucket)


def _inproj(x, sc1, sh1, w_in_t, b_in, tm):
    t, d = x.shape
    n = w_in_t.shape[0]

    def body(x_ref, sc_ref, sh_ref, w_ref, b_ref, proj_ref, h_ref):
        h = (x_ref[...] * (1.0 + sc_ref[...]) + sh_ref[...]).astype(BF16)
        h_ref[...] = h
        proj_ref[...] = _dot_nt(h, w_ref[...]) + b_ref[...]

    row = lambda w: pl.BlockSpec((tm, w), lambda i: (i, 0))
    return pl.pallas_call(
        body, name="inproj", grid=(t // tm,),
        out_shape=[jax.ShapeDtypeStruct((t, n), F32), jax.ShapeDtypeStruct((t, d), BF16)],
        in_specs=[row(d), _full((1, d)), _full((1, d)), _full((n, d)), _full((1, n))],
        out_specs=[row(n), row(d)], compiler_params=_params(("parallel",)),
    )(x, sc1, sh1, w_in_t, b_in)


HALF = 64
ROWS = 32


def _lane_lo(rows):
    return lax.broadcasted_iota(jnp.int32, (rows, 128), 1) < 64


def _mix_stage_kv(proj_ref, kvp_ref, s):
    lo = _lane_lo(2 * BLK)
    for name, col in (("k", ATTN_W), ("v", ATTN_W + KV_W)):
        cur = jnp.concatenate([kvp_ref[:, col - ATTN_W:col - ATTN_W + KV_W], proj_ref[:, col:col + KV_W]], axis=0)
        plain, swapped = cur.astype(BF16), pltpu.roll(cur, 64, 1).astype(BF16)
        zero = jnp.zeros_like(plain)
        for g in range(2):
            dup = jnp.where(lo, plain, swapped) if g == 0 else jnp.where(lo, swapped, plain)
            s[name + "d"][g] = dup
            s[name + "m"][g] = jnp.concatenate([jnp.where(lo, dup, zero), jnp.where(lo, zero, dup)], axis=0)


def _group_rows(ref, g):
    return ref[4 * g:4 * g + 4].reshape(4 * BLK, ref.shape[2])


def _pair_rows(ref, g):
    return jnp.concatenate([jnp.concatenate([ref[4 * g + 2 * c], ref[4 * g + 2 * c + 1]], axis=1) for c in range(2)],
                           axis=0)


def _mask_heads(src_ref, dst_ref):
    lo = _lane_lo(BLK)
    for j in range(4):
        chunk = src_ref[:, 128 * j:128 * (j + 1)]
        dst_ref[2 * j] = jnp.where(lo, chunk, 0.0).astype(BF16)
        dst_ref[2 * j + 1] = jnp.where(lo, 0.0, chunk).astype(BF16)


def _mix_stage_attn(proj_ref, bias_ref, sinks_ref, n, s):
    _mask_heads(proj_ref, s["qm"])
    for g in range(2):
        s["lg"][g] = _dot_nt(_group_rows(s["qm"], g), s["kd"][g])
    n0mask = (n == 0) & (lax.broadcasted_iota(jnp.int32, (HALF, 2 * BLK), 1) < BLK)
    lane = lax.broadcasted_iota(jnp.int32, (HALF, 128), 1)
    for hf in range(BLK // HALF):
        rows = slice(HALF * hf, HALF * (hf + 1))
        psink = jnp.zeros((HALF, 128), F32)
        for h in range(N_HEADS):
            sk = sinks_ref[h]
            grows = slice(BLK * (h % 4) + HALF * hf, BLK * (h % 4) + HALF * (hf + 1))
            logit = s["lg"][h // 4, grows, :] * (HEAD_DIM ** -0.5) + bias_ref[h, rows, :]
            logit = jnp.where(n0mask, NEG_INF, logit)
            m = jnp.maximum(jnp.max(logit, axis=1, keepdims=True), sk)
            e = jnp.exp(logit - m)
            es = jnp.exp(sk - m)
            inv = 1.0 / (jnp.sum(e, axis=1, keepdims=True) + es)
            p = e * inv
            s["p"][h, rows, :] = p
            s["pb"][h, rows, :] = p.astype(BF16)
            psink = jnp.where(lane == h, es * inv, psink)
        s["psink"][rows, :] = psink
    for g in range(2):
        out = _dot(_pair_rows(s["pb"], g), s["vm"][g])
        s["attn"][:, 256 * g:256 * g + 128] = out[0:BLK]
        s["attn"][:, 256 * g + 128:256 * g + 256] = out[BLK:2 * BLK]


def _mix_stage_gmlp_pre(proj_ref, lng, lnb, s, keep):
    c0 = ATTN_W + 2 * KV_W
    for r0 in range(0, BLK, ROWS):
        rows = slice(r0, r0 + ROWS)
        u, du = _gelu_parts(proj_ref[rows, c0:c0 + GMLP_W])
        a, da = _gelu_parts(proj_ref[rows, c0 + GMLP_W:c0 + 2 * GMLP_W])
        ac = a - _seg_mean64(a)
        rstd = lax.rsqrt(_seg_mean64(ac * ac) + LN_EPS)
        vhat = ac * rstd
        s["u"][rows, :] = u
        s["vnb"][rows, :] = (vhat * lng + lnb).astype(BF16)
        if keep:
            s["du"][rows, :] = du
            s["da"][rows, :] = da
            s["vhat"][rows, :] = vhat
            s["rstd"][rows, :] = rstd


def _stack_halves(chunk):
    lo = _lane_lo(BLK)
    zero = jnp.zeros_like(chunk)
    return jnp.concatenate([jnp.where(lo, chunk, zero), jnp.where(lo, zero, chunk)], axis=0)


def _mix_stage_gmlp_mix(ws2_ref, bfull_ref, s):
    for j in range(4):
        cols = slice(128 * j, 128 * (j + 1))
        s["ms"][:, cols] = _dot(ws2_ref[j], _stack_halves(s["vnb"][:, cols])) + bfull_ref[:, cols]


def _mix_scratch(keep):
    f32 = lambda *shape: pltpu.VMEM(shape, F32)
    b16 = lambda *shape: pltpu.VMEM(shape, BF16)
    names = dict(kd=b16(2, 2 * BLK, 128), vd=b16(2, 2 * BLK, 128), km=b16(2, 4 * BLK, 128), vm=b16(2, 4 * BLK, 128),
                 qm=b16(N_HEADS, BLK, 128), lg=f32(2, 4 * BLK, 2 * BLK), pb=b16(N_HEADS, BLK, 2 * BLK),
                 u=f32(BLK, GMLP_W), vnb=b16(BLK, GMLP_W), ms=f32(BLK, GMLP_W))
    if keep:
        names.update(dom=b16(N_HEADS, BLK, 128), dls=b16(N_HEADS, BLK, 2 * BLK),
                     dattn=f32(BLK, ATTN_W), dmix=f32(BLK, D_MODEL), du=f32(BLK, GMLP_W), da=f32(BLK, GMLP_W),
                     vhat=f32(BLK, GMLP_W), rstd=f32(BLK, GMLP_W), dmsb=b16(BLK, GMLP_W), dvn=f32(BLK, GMLP_W))
    return list(names), list(names.values())


def _mix_specs(with_logit_inputs):
    logit_inputs = [_full((N_HEADS, BLK, 2 * BLK)), pl.BlockSpec(memory_space=pltpu.SMEM)] if with_logit_inputs else []
    return [pl.BlockSpec((BLK, IN_W), lambda n: (n, 0)),
            pl.BlockSpec((BLK, 2 * KV_W), lambda n: (jnp.maximum(n - 1, 0), ATTN_W // (2 * KV_W)))] + logit_inputs + [
            _full((1, GMLP_W)), _full((1, GMLP_W)),
            _full((N_GROUPS // 2, BLK, 2 * BLK)), _full((BLK, GMLP_W)),
            _full((1, ATTN_W)), _full((1, GMLP_W))]


KEPT = [("p", (N_HEADS, BLK, 2 * BLK), F32), ("psink", (BLK, 128), F32), ("attn", (BLK, ATTN_W), F32)]


def _kept_shapes(t):
    full = lambda blk: (blk[0], t, blk[2]) if len(blk) == 3 else (t, blk[1])
    return [jax.ShapeDtypeStruct(full(blk), dt) for _, blk, dt in KEPT]


def _kept_specs():
    return [pl.BlockSpec(blk, (lambda n: (0, n, 0)) if len(blk) == 3 else (lambda n: (n, 0))) for _, blk, _ in KEPT]


def _mix_fwd(proj, bias, sinks, lng, lnb, ws2, bfull, aog, gog, comm):
    t = proj.shape[0]
    names, shapes = _mix_scratch(False)

    def body(proj_ref, kvp_ref, bias_ref, sinks_ref, lng_ref, lnb_ref, ws2_ref, bfull_ref, aog_ref, gog_ref,
             out_ref, *rest):
        s = dict(zip([name for name, _, _ in KEPT] + names, rest))
        n = pl.program_id(0)
        _mix_stage_kv(proj_ref, kvp_ref, s)
        _mix_stage_attn(proj_ref, bias_ref, sinks_ref, n, s)
        _mix_stage_gmlp_pre(proj_ref, lng_ref[...], lnb_ref[...], s, False)
        _mix_stage_gmlp_mix(ws2_ref, bfull_ref, s)
        for r0 in range(0, BLK, ROWS):
            rows = slice(r0, r0 + ROWS)
            out_ref[rows, 0:ATTN_W] = _rms(s["attn"][rows, :], aog_ref[...])[0].astype(BF16)
            out_ref[rows, ATTN_W:ATTN_W + GMLP_W] = _rms(s["u"][rows, :] * s["ms"][rows, :], gog_ref[...])[0].astype(BF16)

    return _call(
        body, name="mix_fwd", grid=(t // BLK,),
        out_shape=[jax.ShapeDtypeStruct((t, D_MODEL), BF16)] + _kept_shapes(t),
        in_specs=_mix_specs(True), out_specs=[pl.BlockSpec((BLK, D_MODEL), lambda n: (n, 0))] + _kept_specs(),
        scratch_shapes=shapes,
        sem=("parallel",), comm=comm, args=(proj, proj, bias, sinks, lng, lnb, ws2, bfull, aog, gog))


def _mix_bwd(proj, lng, lnb, ws2, wst2, bfull, aog, gog, dy, w_out, kept, comm):
    t = proj.shape[0]
    nb = t // BLK
    names, shapes = _mix_scratch(True)
    c_gu = ATTN_W + 2 * KV_W

    def body(proj_ref, kvp_ref, lng_ref, lnb_ref, ws2_ref, bfull_ref, aog_ref, gog_ref,
             wst2_ref, dy_ref, wout_ref, *rest):
        n_kept = len(KEPT)
        s = dict(zip([name for name, _, _ in KEPT], rest[:n_kept]))
        (dproj_ref, dkvn_ref, dl_ref, dsink_ref, dlng_ref, dlnb_ref, dws_ref, dbs_ref, daog_ref,
         dgog_ref) = rest[n_kept:n_kept + 10]
        s.update(zip(names, rest[n_kept + 10:]))
        n = pl.program_id(0)

        @pl.when(n == 0)
        def _():
            for r in (dl_ref, dsink_ref, dlng_ref, dlnb_ref, dws_ref, dbs_ref, daog_ref, dgog_ref):
                r[...] = jnp.zeros_like(r)

        s["dmix"][...] = _dot_nt(dy_ref[...], wout_ref[...])
        _mix_stage_kv(proj_ref, kvp_ref, s)
        _mask_heads(proj_ref, s["qm"])
        lng = lng_ref[...]
        _mix_stage_gmlp_pre(proj_ref, lng, lnb_ref[...], s, True)
        _mix_stage_gmlp_mix(ws2_ref, bfull_ref, s)

        aog, gog = aog_ref[...], gog_ref[...]
        for r0 in range(0, BLK, ROWS):
            rows = slice(r0, r0 + ROWS)
            attn, dma = s["attn"][rows, :], s["dmix"][rows, 0:ATTN_W]
            _, r_a = _rms(attn, aog)
            daog_ref[...] += _rowsum8(dma * attn * r_a)
            s["dattn"][rows, :] = _rms_bwd(dma, attn, r_a, aog)
            u, ms, dmg = s["u"][rows, :], s["ms"][rows, :], s["dmix"][rows, ATTN_W:ATTN_W + GMLP_W]
            gm = u * ms
            _, r_g = _rms(gm, gog)
            dgog_ref[...] += _rowsum8(dmg * gm * r_g)
            dgm = _rms_bwd(dmg, gm, r_g, gog)
            dproj_ref[rows, c_gu:c_gu + GMLP_W] = dgm * ms * s["du"][rows, :]
            dms = dgm * u
            dbs_ref[rows, :] += dms
            s["dmsb"][rows, :] = dms.astype(BF16)

        _mask_heads(s["dattn"], s["dom"])
        for g in range(2):
            s["lg"][g] = _dot_nt(_group_rows(s["dom"], g), s["vd"][g])
        lane = lax.broadcasted_iota(jnp.int32, (HALF, 128), 1)
        for hf in range(BLK // HALF):
            rows = slice(HALF * hf, HALF * (hf + 1))
            dsink = jnp.zeros((HALF, 128), F32)
            for h in range(N_HEADS):
                grows = slice(BLK * (h % 4) + HALF * hf, BLK * (h % 4) + HALF * (hf + 1))
                dp = s["lg"][h // 4, grows, :]
                p = s["p"][h, rows, :]
                s["pb"][h, rows, :] = p.astype(BF16)
                rs = jnp.sum(p * dp, axis=1, keepdims=True)
                dl = p * (dp - rs)
                dl_ref[h, rows, :] += dl
                dsink = dsink + jnp.where(lane == h, -s["psink"][rows, :] * rs, 0.0)
                s["dls"][h, rows, :] = (dl * (HEAD_DIM ** -0.5)).astype(BF16)
            dsink_ref[rows, :] += dsink
        for g in range(2):
            dq = _dot(_pair_rows(s["dls"], g), s["km"][g])
            dproj_ref[:, 256 * g:256 * g + 128] = dq[0:BLK]
            dproj_ref[:, 256 * g + 128:256 * g + 256] = dq[BLK:2 * BLK]
        lo_k = _lane_lo(2 * BLK)
        for col, lhs, rhs in ((0, "dls", "qm"), (KV_W, "pb", "dom")):
            raw = [_dot_tn(_group_rows(s[lhs], g), _group_rows(s[rhs], g)) for g in range(2)]
            both = [r + pltpu.roll(r, 64, 1) for r in raw]
            dkv = jnp.where(lo_k, both[0], both[1])
            dproj_ref[:, ATTN_W + col:ATTN_W + col + KV_W] = dkv[BLK:2 * BLK]
            dkvn_ref[:, col:col + KV_W] = dkv[0:BLK]

        for j in range(4):
            cols = slice(128 * j, 128 * (j + 1))
            dm2 = _stack_halves(s["dmsb"][:, cols])
            vnb = s["vnb"][:, cols]
            dws2 = _dot_nt(dm2, vnb)
            dws_ref[2 * j] += dws2[0:BLK]
            dws_ref[2 * j + 1] += dws2[BLK:2 * BLK]
            s["dvn"][:, cols] = _dot(wst2_ref[j], dm2)
        for r0 in range(0, BLK, ROWS):
            rows = slice(r0, r0 + ROWS)
            dvn, vhat = s["dvn"][rows, :], s["vhat"][rows, :]
            dlng_ref[...] += _rowsum8(dvn * vhat)
            dlnb_ref[...] += _rowsum8(dvn)
            dvh = dvn * lng
            dact = s["rstd"][rows, :] * (dvh - _seg_mean64(dvh) - vhat * _seg_mean64(dvh * vhat))
            dproj_ref[rows, c_gu + GMLP_W:IN_W] = dact * s["da"][rows, :]

    acc8 = lambda w: jax.ShapeDtypeStruct((8, w), F32)
    out_shape = [jax.ShapeDtypeStruct((t, IN_W), F32), jax.ShapeDtypeStruct((t, 2 * KV_W), F32),
                 jax.ShapeDtypeStruct((N_HEADS, BLK, 2 * BLK), F32), jax.ShapeDtypeStruct((BLK, 128), F32),
                 acc8(GMLP_W), acc8(GMLP_W), jax.ShapeDtypeStruct((N_GROUPS, BLK, BLK), F32),
                 jax.ShapeDtypeStruct((BLK, GMLP_W), F32), acc8(ATTN_W), acc8(GMLP_W)]
    out_specs = [pl.BlockSpec((BLK, IN_W), lambda n: (n, 0)),
                 pl.BlockSpec((BLK, 2 * KV_W), lambda n: ((n + nb - 1) % nb, 0)),
                 _full((N_HEADS, BLK, 2 * BLK)), _full((BLK, 128)), _full((8, GMLP_W)), _full((8, GMLP_W)),
                 _full((N_GROUPS, BLK, BLK)), _full((BLK, GMLP_W)), _full((8, ATTN_W)), _full((8, GMLP_W))]
    in_specs = _mix_specs(False) + [_full((N_GROUPS // 2, BLK, 2 * BLK)),
                               pl.BlockSpec((BLK, D_MODEL), lambda n: (n, 0)),
                               _full((D_MODEL, D_MODEL))] + _kept_specs()
    return _call(
        body, name="mix_bwd", grid=(nb,), out_shape=out_shape, in_specs=in_specs, out_specs=out_specs,
        scratch_shapes=shapes, sem=("arbitrary",), comm=comm,
        args=(proj, proj, lng, lnb, ws2, bfull, aog, gog, wst2, dy, w_out, *kept))


def _outproj(mixed, w_out, x, g1, ln1g, ln1b, sc2, sh2, tm, comm):
    t, d = x.shape

    def body(mx_ref, w_ref, x_ref, g1_ref, lg_ref, lb_ref, sc_ref, sh_ref, y_ref, x1_ref, h2_ref):
        y = _dot(mx_ref[...], w_ref[...])
        xhat, _ = _ln_stats(ALPHA * x_ref[...] + g1_ref[...] * y)
        x1 = xhat * lg_ref[...] + lb_ref[...]
        y_ref[...] = y
        x1_ref[...] = x1
        h2_ref[...] = (x1 * (1.0 + sc_ref[...]) + sh_ref[...]).astype(BF16)

    row = pl.BlockSpec((tm, d), lambda i: (i, 0))
    vec = _full((1, d))
    return _call(
        body, name="outproj", grid=(t // tm,),
        out_shape=[jax.ShapeDtypeStruct((t, d), F32), jax.ShapeDtypeStruct((t, d), F32),
                   jax.ShapeDtypeStruct((t, d), BF16)],
        in_specs=[row, _full((d, d)), row, vec, vec, vec, vec, vec], out_specs=[row, row, row],
        sem=("parallel",), comm=comm, args=(mixed, w_out, x, g1, ln1g, ln1b, sc2, sh2))


def _ffn_fwd(h2, w_gu_t, w_down, x1, target, g2, ln2g, ln2b, tm):
    t, d = x1.shape

    def body(h_ref, w_ref, wd_ref, x1_ref, tg_ref, g2_ref, lg_ref, lb_ref,
             dsu_ref, sg_ref, act_ref, dz_ref, dy_ref, loss_ref, dlg_ref, dlb_ref, dg2_ref):
        @pl.when(pl.program_id(0) == 0)
        def _():
            for r in (loss_ref, dlg_ref, dlb_ref, dg2_ref):
                r[...] = jnp.zeros_like(r)

        h = h_ref[...]
        g = _dot_nt(h, w_ref[0:D_FF])
        u = _dot_nt(h, w_ref[D_FF:2 * D_FF])
        s = _sigmoid(g)
        sg = g * s
        act = (sg * u).astype(BF16)
        dsu_ref[...] = (u * (s * (1.0 + g * (1.0 - s)))).astype(BF16)
        sg_ref[...] = sg.astype(BF16)
        act_ref[...] = act
        y2 = _dot(act, wd_ref[...])
        g2 = g2_ref[...]
        lg = lg_ref[...]
        xhat, rstd = _ln_stats(ALPHA * x1_ref[...] + g2 * y2)
        err = xhat * lg + lb_ref[...] - tg_ref[...]
        loss_ref[...] += _rowsum8(err * err)
        dx2 = err * (1.0 / d)
        dlg_ref[...] += _rowsum8(dx2 * xhat)
        dlb_ref[...] += _rowsum8(dx2)
        dz = _ln_bwd(dx2 * lg, xhat, rstd)
        dg2_ref[...] += _rowsum8(dz * y2)
        dz_ref[...] = dz
        dy_ref[...] = (g2 * dz).astype(BF16)

    row = pl.BlockSpec((tm, d), lambda i: (i, 0))
    wide = pl.BlockSpec((tm, D_FF), lambda i: (i, 0))
    vec = _full((1, d))
    acc = _full((8, d))
    acc_shape = jax.ShapeDtypeStruct((8, d), F32)
    wide_shape = jax.ShapeDtypeStruct((t, D_FF), BF16)
    return pl.pallas_call(
        body, name="ffn_fwd", grid=(t // tm,),
        out_shape=[wide_shape] * 3 + [jax.ShapeDtypeStruct((t, d), F32), jax.ShapeDtypeStruct((t, d), BF16)]
        + [acc_shape] * 4,
        in_specs=[row, _resident((2 * D_FF, d)), _resident((D_FF, d)), row, row, vec, vec, vec],
        out_specs=[wide] * 3 + [row, row, acc, acc, acc, acc], compiler_params=_params(("arbitrary",)),
    )(h2, w_gu_t, w_down, x1, target, g2, ln2g, ln2b)


def _resident(shape):
    nd = len(shape)
    return pl.BlockSpec(shape, lambda *_: (0,) * nd, pipeline_mode=pl.Buffered(1))


def _ffn_bwd(dy2, w_down, dsu, sg, w_gu_t, x1, x, y, dz2, sc2, g1, ln1g, tm, comm):
    t, d = x1.shape

    def body(dy2_ref, wd_ref, dsu_ref, sg_ref, w_ref, x1_ref, x_ref, y_ref, dz2_ref, sc_ref, g1_ref, lg_ref,
             dg_ref, du_ref, dz1_ref, dy_ref, dsc_ref, dsh_ref, dlg_ref, dlb_ref, dg1_ref):
        @pl.when(pl.program_id(0) == 0)
        def _():
            for r in (dsc_ref, dsh_ref, dlg_ref, dlb_ref, dg1_ref):
                r[...] = jnp.zeros_like(r)

        dact = _dot_nt(dy2_ref[...], wd_ref[...])
        dg = (dact * dsu_ref[...].astype(F32)).astype(BF16)
        du = (dact * sg_ref[...].astype(F32)).astype(BF16)
        dg_ref[...] = dg
        du_ref[...] = du
        dh2 = _dot(dg, w_ref[0:D_FF]) + _dot(du, w_ref[D_FF:2 * D_FF])
        x1 = x1_ref[...]
        y = y_ref[...]
        g1 = g1_ref[...]
        dsc_ref[...] += _rowsum8(dh2 * x1)
        dsh_ref[...] += _rowsum8(dh2)
        dx1 = dh2 * (1.0 + sc_ref[...]) + ALPHA * dz2_ref[...]
        xhat, rstd = _ln_stats(ALPHA * x_ref[...] + g1 * y)
        dlg_ref[...] += _rowsum8(dx1 * xhat)
        dlb_ref[...] += _rowsum8(dx1)
        dz1 = _ln_bwd(dx1 * lg_ref[...], xhat, rstd)
        dg1_ref[...] += _rowsum8(dz1 * y)
        dz1_ref[...] = dz1
        dy_ref[...] = (g1 * dz1).astype(BF16)

    row = pl.BlockSpec((tm, d), lambda i: (i, 0))
    wide = pl.BlockSpec((tm, D_FF), lambda i: (i, 0))
    vec = _full((1, d))
    acc = _full((8, d))
    acc_shape = jax.ShapeDtypeStruct((8, d), F32)
    wide_shape = jax.ShapeDtypeStruct((t, D_FF), BF16)
    return _call(
        body, name="ffn_bwd", grid=(t // tm,),
        out_shape=[wide_shape, wide_shape, jax.ShapeDtypeStruct((t, d), F32), jax.ShapeDtypeStruct((t, d), BF16)]
        + [acc_shape] * 5,
        in_specs=[row, _resident((D_FF, d)), wide, wide, _resident((2 * D_FF, d)), row, row, row, row, vec, vec, vec],
        out_specs=[wide, wide, row, row, acc, acc, acc, acc, acc], sem=("arbitrary",), comm=comm,
        args=(dy2, w_down, dsu, sg, w_gu_t, x1, x, y, dz2, sc2, g1, ln1g))


def _din(dproj, dkvn, w_in_t, x, dz1, sc1, tm, comm):
    t, d = x.shape

    def body(dp_ref, dkv_ref, w_ref, x_ref, dz1_ref, sc_ref, dx_ref, dpb_ref, dbin_ref, dsc_ref, dsh_ref):
        @pl.when(pl.program_id(0) == 0)
        def _():
            for r in (dbin_ref, dsc_ref, dsh_ref):
                r[...] = jnp.zeros_like(r)

        dp = jnp.concatenate([dp_ref[:, 0:ATTN_W], dp_ref[:, ATTN_W:ATTN_W + 2 * KV_W] + dkv_ref[...],
                              dp_ref[:, ATTN_W + 2 * KV_W:IN_W]], axis=1)
        dbin_ref[...] += _rowsum8(dp)
        dpb = dp.astype(BF16)
        dpb_ref[...] = dpb
        dh = _dot(dpb, w_ref[...])
        dsc_ref[...] += _rowsum8(dh * x_ref[...])
        dsh_ref[...] += _rowsum8(dh)
        dx_ref[...] = dh * (1.0 + sc_ref[...]) + ALPHA * dz1_ref[...]

    row = lambda w: pl.BlockSpec((tm, w), lambda i: (i, 0))
    return _call(
        body, name="din", grid=(t // tm,),
        out_shape=[jax.ShapeDtypeStruct((t, d), F32), jax.ShapeDtypeStruct((t, IN_W), BF16),
                   jax.ShapeDtypeStruct((8, IN_W), F32), jax.ShapeDtypeStruct((8, d), F32),
                   jax.ShapeDtypeStruct((8, d), F32)],
        in_specs=[row(IN_W), row(2 * KV_W), _full((IN_W, d)), row(d), row(d), _full((1, d))],
        out_specs=[row(d), row(IN_W), _full((8, IN_W)), _full((8, d)), _full((8, d))],
        sem=("arbitrary",), comm=comm, args=(dproj, dkvn, w_in_t, x, dz1, sc1))


def _wgrad(name, a, b, tmm, tk, comm=None, a2=None):
    t, m = a.shape
    n = b.shape[1]
    nk = t // tk
    nm = m // tmm

    def body(*refs):
        a_refs, (b_ref, o_ref, acc_ref) = refs[:-3], refs[-3:]
        i, k = pl.program_id(0), pl.program_id(1)

        @pl.when(k == 0)
        def _():
            acc_ref[...] = jnp.zeros_like(acc_ref)

        a_tile = a_refs[0][...] if a2 is None else jnp.where(i < nm, a_refs[0][...], a_refs[1][...])
        acc_ref[...] += _dot_tn(a_tile, b_ref[...])

        @pl.when(k == nk - 1)
        def _():
            o_ref[...] = acc_ref[...].astype(BF16)

    if a2 is None:
        a_specs, a_args, n_tiles = [pl.BlockSpec((tk, tmm), lambda i, k: (k, i))], (a,), nm
    else:
        a_specs = [pl.BlockSpec((tk, tmm), lambda i, k: (jnp.where(i < nm, k, 0), jnp.minimum(i, nm - 1))),
                   pl.BlockSpec((tk, tmm), lambda i, k: (jnp.where(i < nm, 0, k), jnp.maximum(i - nm, 0)))]
        a_args, n_tiles = (a, a2), 2 * nm
    (out,), got = _call(
        body, name=name, grid=(n_tiles, nk), out_shape=[jax.ShapeDtypeStruct((n_tiles * tmm, n), BF16)],
        in_specs=a_specs + [pl.BlockSpec((tk, n), lambda i, k: (k, 0))],
        out_specs=[pl.BlockSpec((tmm, n), lambda i, k: (i, 0))],
        scratch_shapes=[pltpu.VMEM((tmm, n), F32)], sem=("parallel", "arbitrary"), comm=comm, args=a_args + (b,))
    return out if comm is None else (out, got)


def _adamw(w, g, m, v):
    m = ADAM_B1 * m + (1.0 - ADAM_B1) * g
    v = ADAM_B2 * v + (1.0 - ADAM_B2) * (g * g)
    m_hat = m / (1.0 - ADAM_B1 ** ADAM_STEP)
    v_hat = v / (1.0 - ADAM_B2 ** ADAM_STEP)
    delta = -ADAM_LR * (m_hat / (jnp.sqrt(v_hat) + ADAM_EPS) + ADAM_WD * w)
    return delta, m, v


def _adam_reduce(name, parts, w, m, v, tr):
    r, cdim = w.shape

    def body(p_ref, w_ref, m_ref, v_ref, g_ref, d_ref, mo_ref, vo_ref):
        g = p_ref[0].astype(F32)
        for s in range(1, N_DEV):
            g = g + p_ref[s].astype(F32)
        d_ref[...], mo_ref[...], vo_ref[...] = _adamw(w_ref[...], g, m_ref[...], v_ref[...])
        g_ref[...] = g

    tile = pl.BlockSpec((tr, cdim), lambda i: (i, 0))
    shp = jax.ShapeDtypeStruct((r, cdim), F32)
    return pl.pallas_call(
        body, name=name, grid=(r // tr,), out_shape=[shp] * 4,
        in_specs=[pl.BlockSpec((N_DEV, tr, cdim), lambda i: (0, i, 0)), tile, tile, tile],
        out_specs=[tile] * 4, compiler_params=_params(("parallel",)),
    )(parts, w, m, v)


def _adam_w_ada(c_all_t, dmod_cols, w, m, v):
    def body(ct_ref, dm_ref, w_ref, m_ref, v_ref, g_ref, d_ref, mo_ref, vo_ref):
        ct = ct_ref[...]
        s = (ct * _sigmoid(ct)).astype(BF16)
        g = _dot(s, dm_ref[...].astype(BF16))
        d_ref[...], mo_ref[...], vo_ref[...] = _adamw(w_ref[...], g, m_ref[...], v_ref[...])
        g_ref[...] = g

    shp = jax.ShapeDtypeStruct(w.shape, F32)
    return pl.pallas_call(
        body, name="adam_w_ada", grid=(1,), out_shape=[shp] * 4,
        in_specs=[_full(c_all_t.shape), _full(dmod_cols.shape)] + [_full(w.shape)] * 3,
        out_specs=[_full(w.shape)] * 4, compiler_params=_params(("arbitrary",)),
    )(c_all_t, dmod_cols, w, m, v)


SMALL_EARLY = ["rel_bias", "attn_sinks", "gmlp_ln_g", "gmlp_ln_b", "gmlp_w_s", "gmlp_b_s",
               "attn_out_g", "gmlp_out_g", "ln1_g", "ln1_b", "ln2_g", "ln2_b"]
SMALL_LATE = ["b_ada", "b_in", "loss"]
WEIGHTS = ["rel_bias", "w_ada", "b_ada", "w_in", "b_in", "attn_sinks", "gmlp_ln_g", "gmlp_ln_b", "gmlp_w_s",
           "gmlp_b_s", "attn_out_g", "gmlp_out_g", "w_out", "ln1_g", "ln1_b", "w_gate_up", "w_down", "ln2_g", "ln2_b"]


def _seg_rows(nelem):
    return -(-nelem // 1024) * 8


def _pack(named, names):
    parts = []
    for name in names:
        flat = named[name].reshape(-1).astype(F32)
        rows = _seg_rows(flat.shape[0])
        parts.append(jnp.pad(flat, (0, rows * 128 - flat.shape[0])).reshape(rows, 128))
    return jnp.concatenate(parts, axis=0)


def _unpack(packed, shapes, names):
    out, r0 = {}, 0
    for name in names:
        nelem = math.prod(shapes[name])
        rows = _seg_rows(nelem)
        out[name] = packed[r0:r0 + rows].reshape(-1)[:nelem].reshape(shapes[name])
        r0 += rows
    return out


def _t5_bucket_map():
    qi = jnp.arange(BLK)[:, None]
    si = jnp.arange(2 * BLK)[None, :]
    n = jnp.maximum(qi + BLK - si, 0)
    max_exact = N_BUCKETS // 2
    nf = jnp.maximum(n, max_exact).astype(F32)
    large = max_exact + (jnp.log(nf / max_exact) / math.log(MAX_DISTANCE / max_exact)
                         * (N_BUCKETS - max_exact)).astype(jnp.int32)
    large = jnp.minimum(large, N_BUCKETS - 1)
    return jnp.where(n < max_exact, n, large).astype(jnp.int32)


def kernel(x, c, rel_bias, w_ada, b_ada, w_in, b_in, attn_sinks, gmlp_ln_g, gmlp_ln_b, gmlp_w_s, gmlp_b_s, attn_out_g, gmlp_out_g, w_out, ln1_g, ln1_b, w_gate_up, w_down, ln2_g, ln2_b, loss_target, m_rel_bias, m_w_ada, m_b_ada, m_w_in, m_b_in, m_attn_sinks, m_gmlp_ln_g, m_gmlp_ln_b, m_gmlp_w_s, m_gmlp_b_s, m_attn_out_g, m_gmlp_out_g, m_w_out, m_ln1_g, m_ln1_b, m_w_gate_up, m_w_down, m_ln2_g, m_ln2_b, v_rel_bias, v_w_ada, v_b_ada, v_w_in, v_b_in, v_attn_sinks, v_gmlp_ln_g, v_gmlp_ln_b, v_gmlp_w_s, v_gmlp_b_s, v_attn_out_g, v_gmlp_out_g, v_w_out, v_ln1_g, v_ln1_b, v_w_gate_up, v_w_down, v_ln2_g, v_ln2_b):
    wts = dict(rel_bias=rel_bias, w_ada=w_ada, b_ada=b_ada, w_in=w_in, b_in=b_in, attn_sinks=attn_sinks,
               gmlp_ln_g=gmlp_ln_g, gmlp_ln_b=gmlp_ln_b, gmlp_w_s=gmlp_w_s, gmlp_b_s=gmlp_b_s,
               attn_out_g=attn_out_g, gmlp_out_g=gmlp_out_g, w_out=w_out, ln1_g=ln1_g, ln1_b=ln1_b,
               w_gate_up=w_gate_up, w_down=w_down, ln2_g=ln2_g, ln2_b=ln2_b)
    mom_m = dict(rel_bias=m_rel_bias, w_ada=m_w_ada, b_ada=m_b_ada, w_in=m_w_in, b_in=m_b_in,
                 attn_sinks=m_attn_sinks, gmlp_ln_g=m_gmlp_ln_g, gmlp_ln_b=m_gmlp_ln_b, gmlp_w_s=m_gmlp_w_s,
                 gmlp_b_s=m_gmlp_b_s, attn_out_g=m_attn_out_g, gmlp_out_g=m_gmlp_out_g, w_out=m_w_out,
                 ln1_g=m_ln1_g, ln1_b=m_ln1_b, w_gate_up=m_w_gate_up, w_down=m_w_down, ln2_g=m_ln2_g,
                 ln2_b=m_ln2_b)
    mom_v = dict(rel_bias=v_rel_bias, w_ada=v_w_ada, b_ada=v_b_ada, w_in=v_w_in, b_in=v_b_in,
                 attn_sinks=v_attn_sinks, gmlp_ln_g=v_gmlp_ln_g, gmlp_ln_b=v_gmlp_ln_b, gmlp_w_s=v_gmlp_w_s,
                 gmlp_b_s=v_gmlp_b_s, attn_out_g=v_attn_out_g, gmlp_out_g=v_gmlp_out_g, w_out=v_w_out,
                 ln1_g=v_ln1_g, ln1_b=v_ln1_b, w_gate_up=v_w_gate_up, w_down=v_w_down, ln2_g=v_ln2_g,
                 ln2_b=v_ln2_b)

    t = x.shape[1]
    tm = min(512, t)
    tn_ff = D_FF // 2
    tk_long, tk_short = min(4096, t), min(2048, t)
    me = 4 * lax.axis_index("x") + 2 * lax.axis_index("y") + lax.axis_index("c")
    xs = x[0]
    target = loss_target[0]

    c_g, w_in_g = _exchange("gather_in", [jnp.broadcast_to(c, (8, D_MODEL)), w_in[0].T.astype(BF16)],
                            ("gather", "gather2"))
    c_all = c_g[:, 0, :]
    w_in_t = w_in_g.reshape(IN_W, D_MODEL)

    ncol = w_ada.shape[2]
    b_cols = lax.dynamic_slice(b_ada, (0, me * ncol), (1, ncol))
    mod_part = _mod_partial(c_all, w_ada[0], b_cols)
    (mod_g,) = _exchange("gather_mod", [mod_part], ("gather",))
    mod = lax.dynamic_slice(mod_g, (0, me, 0), (N_DEV, 1, ncol)).reshape(1, N_DEV * ncol)
    sh1, sc1, g1, sh2, sc2, g2 = [mod[:, i * D_MODEL:(i + 1) * D_MODEL] for i in range(6)]

    bucket = _t5_bucket_map()
    bias = _bias_table(rel_bias, bucket)
    causal = jnp.tril(jnp.ones((BLK, BLK), dtype=bool))
    ws = jnp.where(causal[None], gmlp_w_s[0], 0.0).astype(BF16)
    pair = lambda w: jnp.concatenate([w[0::2], w[1::2]], axis=2)
    ws2, wst2 = pair(ws), pair(jnp.swapaxes(ws, 1, 2))
    bfull = jnp.repeat(gmlp_b_s[0].T, GMLP_W // N_GROUPS, axis=1)
    sinks = attn_sinks[0]

    proj, h1 = _inproj(xs, sc1, sh1, w_in_t, b_in, tm)
    (mixed, *kept), (w_out_g, w_gu_g) = _mix_fwd(
        proj, bias, sinks, gmlp_ln_g, gmlp_ln_b, ws2, bfull, attn_out_g, gmlp_out_g,
        comm=([w_out[0].astype(BF16), w_gate_up[0].T.astype(BF16)], ("gather2", "gather2")))
    w_out_f = w_out_g.reshape(D_MODEL, D_MODEL)
    w_gu_t = w_gu_g.reshape(2 * D_FF, D_MODEL)
    (y1, x1, h2), (w_down_g,) = _outproj(mixed, w_out_f, xs, g1, ln1_g, ln1_b, sc2, sh2, tm,
                                         comm=([w_down[0].astype(BF16)], ("gather2",)))
    w_down_f = w_down_g.reshape(D_FF, D_MODEL)
    dsu, sg, act, dz2, dy2, loss_p, d_ln2g, d_ln2b, d_g2 = _ffn_fwd(h2, w_gu_t, w_down_f, x1, target, g2, ln2_g, ln2_b,
                                                                    min(256, t))

    slots = lambda a: a.reshape(N_DEV, -1, D_MODEL)
    dw_down = _wgrad("wgrad_down", act, dy2, tn_ff, tk_short)
    (dgate, dup, dz1, dy1, d_sc2, d_sh2, d_ln1g, d_ln1b, d_g1), (r_down,) = _ffn_bwd(
        dy2, w_down_f, dsu, sg, w_gu_t, x1, xs, y1, dz2, sc2, g1, ln1_g, min(256, t),
        comm=([slots(dw_down)], ("scatter",)))
    dw_gu_t = _wgrad("wgrad_gate_up", dgate, h2, tn_ff, tk_short, a2=dup)
    dw_out = _wgrad("wgrad_out", mixed, dy1, D_MODEL, tk_long)
    ((dproj, dkvn, dl_acc, dsink_acc, d_lng, d_lnb, d_ws, d_bs, d_aog, d_gog), (r_gu, r_out)) = _mix_bwd(
        proj, gmlp_ln_g, gmlp_ln_b, ws2, wst2, bfull, attn_out_g, gmlp_out_g, dy1, w_out_f, kept,
        comm=([slots(dw_gu_t), slots(dw_out)], ("scatter", "scatter")))
    d_relb = _bias_grad(dl_acc, bucket)

    rsum = lambda a: jnp.sum(a, axis=0)
    early_g = dict(
        rel_bias=d_relb[:, 0, :N_BUCKETS].T, attn_sinks=rsum(dsink_acc)[:N_HEADS],
        gmlp_ln_g=rsum(d_lng), gmlp_ln_b=rsum(d_lnb), gmlp_w_s=jnp.where(causal[None], d_ws, 0.0),
        gmlp_b_s=jnp.sum(d_bs.reshape(BLK, N_GROUPS, GMLP_W // N_GROUPS), axis=2).T,
        attn_out_g=rsum(d_aog), gmlp_out_g=rsum(d_gog), ln1_g=rsum(d_ln1g), ln1_b=rsum(d_ln1b),
        ln2_g=rsum(d_ln2g), ln2_b=rsum(d_ln2b))
    (grad_x, dproj_b, d_bin, d_sc1, d_sh1), _ = _din(dproj, dkvn, w_in_t, xs, dz1, sc1, tm, comm=None)
    dw_in_t, (early_all,) = _wgrad("wgrad_in", dproj_b, h1, IN_W // 2, tk_long,
                                   comm=([_pack(early_g, SMALL_EARLY)], ("gather2",)))
    dmod = jnp.concatenate([rsum(d_sh1), rsum(d_sc1), rsum(d_g1), rsum(d_sh2), rsum(d_sc2), rsum(d_g2)])
    late_g = dict(b_ada=dmod, b_in=rsum(d_bin), loss=(0.5 / D_MODEL * jnp.sum(loss_p)).reshape(1))
    late_all, r_in = _exchange("scatter_in", [_pack(late_g, SMALL_LATE), slots(dw_in_t)], ("gather", "scatter"))

    small = [{}, {}, {}, {}]
    for label, names, parts in (("adam_small_early", SMALL_EARLY, early_all), ("adam_small_late", SMALL_LATE, late_all)):
        with_loss = lambda tree: dict(tree, loss=jnp.zeros((1,), F32))
        res = _adam_reduce(label, parts, _pack(with_loss(wts), names), _pack(with_loss(mom_m), names),
                           _pack(with_loss(mom_v), names), parts.shape[1])
        shapes = {k: with_loss(wts)[k].shape for k in names}
        for i in range(4):
            small[i].update(_unpack(res[i], shapes, names))
    loss = small[0]["loss"].reshape(())

    dmod_all = late_all[:, :_seg_rows(6 * D_MODEL), :].reshape(N_DEV, 6 * D_MODEL)
    dmod_cols = lax.dynamic_slice(dmod_all, (0, me * ncol), (N_DEV, ncol))
    kpad = 128 - N_DEV
    ada = _adam_w_ada(jnp.pad(c_all.T, ((0, 0), (0, kpad))), jnp.pad(dmod_cols, ((0, kpad), (0, 0))),
                      w_ada[0], m_w_ada[0], v_w_ada[0])

    tr = lambda a: jnp.swapaxes(a, -1, -2)
    big = {}
    big["w_in"] = [tr(o)[None] for o in _adam_reduce("adam_w_in", r_in, w_in[0].T, m_w_in[0].T, v_w_in[0].T, 112)]
    big["w_out"] = [o[None] for o in _adam_reduce("adam_w_out", r_out, w_out[0], m_w_out[0], v_w_out[0], 128)]
    big["w_gate_up"] = [tr(o)[None] for o in _adam_reduce("adam_w_gu", r_gu, w_gate_up[0].T, m_w_gate_up[0].T,
                                                           v_w_gate_up[0].T, 352)]
    big["w_down"] = [o[None] for o in _adam_reduce("adam_w_down", r_down, w_down[0], m_w_down[0], v_w_down[0], 176)]
    big["w_ada"] = [o[None] for o in ada]

    outs = [[], [], [], []]
    for name in WEIGHTS:
        for i in range(4):
            outs[i].append(big[name][i] if name in big else small[i][name])
    return (loss, grad_x[None], *outs[0], *outs[1], *outs[2], *outs[3])
```

```python
import math

import jax
import jax.numpy as jnp
from jax import lax
from jax.experimental import pallas as pl
from jax.experimental.pallas import tpu as pltpu

F32 = jnp.float32
BF16 = jnp.bfloat16
MESH = pl.DeviceIdType.MESH

N_DEV = 8
D_MODEL = 1024
HEAD_DIM = 64
N_HEADS = 8
N_GROUPS = 8
ATTN_W = 512
KV_W = 128
GMLP_W = 512
IN_W = 1792
BLK = 128
N_BUCKETS = 32
MAX_DISTANCE = 128
D_FF = 2816
ALPHA = 2.0 ** 0.25
LN_EPS = 1e-5
NEG_INF = -1e30
ADAM_LR = 0.001
ADAM_B1 = 0.9
ADAM_B2 = 0.999
ADAM_EPS = 1e-08
ADAM_WD = 0.01
ADAM_STEP = 10
GELU_C0 = math.sqrt(2.0 / math.pi)
GELU_C1 = 0.044715

VMEM_LIMIT = 56 * 1024 * 1024


def _params(sem):
    return pltpu.CompilerParams(dimension_semantics=sem, vmem_limit_bytes=VMEM_LIMIT)


def _dot(a, b):
    return lax.dot_general(a, b, (((1,), (0,)), ((), ())), preferred_element_type=F32)


def _dot_nt(a, b):
    return lax.dot_general(a, b, (((1,), (1,)), ((), ())), preferred_element_type=F32)


def _dot_tn(a, b):
    return lax.dot_general(a, b, (((0,), (0,)), ((), ())), preferred_element_type=F32)


def _full(shape):
    nd = len(shape)
    return pl.BlockSpec(shape, lambda *_: (0,) * nd)


def _rowsum8(v):
    r, c = v.shape
    return jnp.sum(v.reshape(r // 8, 8, c), axis=0)


def _sigmoid(v):
    return 1.0 / (1.0 + jnp.exp(-v))


def _gelu_parts(v):
    v2 = v * v
    t = jnp.tanh(GELU_C0 * (v + GELU_C1 * v * v2))
    g = 0.5 * v * (1.0 + t)
    dg = 0.5 * (1.0 + t) + 0.5 * v * (1.0 - t * t) * (GELU_C0 * (1.0 + 3.0 * GELU_C1 * v2))
    return g, dg


def _ln_stats(z):
    mu = jnp.mean(z, axis=1, keepdims=True)
    zc = z - mu
    var = jnp.mean(zc * zc, axis=1, keepdims=True)
    rstd = lax.rsqrt(var + LN_EPS)
    return zc * rstd, rstd


def _ln_bwd(dxhat, xhat, rstd):
    m1 = jnp.mean(dxhat, axis=1, keepdims=True)
    m2 = jnp.mean(dxhat * xhat, axis=1, keepdims=True)
    return rstd * (dxhat - m1 - xhat * m2)


def _seg_mean64(v):
    r = v.shape[0]
    lo = lax.broadcasted_iota(jnp.int32, (r, 128), 1) < 64
    outs = []
    for j in range(v.shape[1] // 128):
        ch = v[:, 128 * j:128 * (j + 1)]
        s_lo = jnp.sum(jnp.where(lo, ch, 0.0), axis=1, keepdims=True)
        s_hi = jnp.sum(jnp.where(lo, 0.0, ch), axis=1, keepdims=True)
        outs.append(jnp.where(lo, s_lo, s_hi) * (1.0 / 64.0))
    return jnp.concatenate(outs, axis=1)


def _rms(a, g):
    r = lax.rsqrt(jnp.mean(a * a, axis=1, keepdims=True) + LN_EPS)
    return a * r * g, r


def _rms_bwd(dout, a, r, g):
    t = dout * g
    return r * t - a * (r * r * r) * jnp.mean(t * a, axis=1, keepdims=True)


PEER_ORDER = (1, 2, 4, 3, 5, 6, 7)


def _peer(j):
    x, y, c = lax.axis_index("x"), lax.axis_index("y"), lax.axis_index("c")
    px = 1 - x if j & 4 else x
    py = 1 - y if j & 2 else y
    pc = 1 - c if j & 1 else c
    return (px, py, pc), 4 * px + 2 * py + pc


SIBLING = 1
CHIP_FLIPS = (4, 2, 6)


def _exchange_phase(phase, ins, outs, modes, send_sems, recv_sems, loc_sems):
    me = 4 * lax.axis_index("x") + 2 * lax.axis_index("y") + lax.axis_index("c")
    for k, mode in enumerate(modes):
        def copy(i, src, slot, dev, k=k):
            return pltpu.make_async_remote_copy(src_ref=src, dst_ref=outs[k].at[slot], send_sem=send_sems.at[k, i],
                                                recv_sem=recv_sems.at[k, i], device_id=dev, device_id_type=MESH)

        src_me = ins[k].at[me] if mode == "scatter" else ins[k]
        local = pltpu.make_async_copy(src_me, outs[k].at[me], loc_sems.at[k])
        if mode == "gather2":
            sib_dev, sib_idx = _peer(SIBLING)
            chips = [_peer(j) for j in CHIP_FLIPS]
            far = [_peer(j | SIBLING)[1] for j in CHIP_FLIPS]
            if phase == "start":
                local.start()
                copy(0, ins[k], me, sib_dev).start()
                for i, (dev, _) in enumerate(chips):
                    copy(1 + i, ins[k], me, dev).start()
            elif phase == "mid":
                for i, (dev, idx) in enumerate(chips):
                    copy(1 + i, ins[k], idx, dev).wait_recv()
                    copy(4 + i, outs[k].at[idx], idx, sib_dev).start()
            else:
                copy(0, ins[k], sib_idx, sib_dev).wait_recv()
                for i, slot in enumerate(far):
                    copy(4 + i, ins[k], slot, sib_dev).wait_recv()
                copy(0, ins[k], me, sib_dev).wait_send()
                for i, (dev, idx) in enumerate(chips):
                    copy(1 + i, ins[k], me, dev).wait_send()
                    copy(4 + i, outs[k].at[idx], idx, sib_dev).wait_send()
                local.wait()
            continue
        peers = [_peer(j) for j in PEER_ORDER]
        if phase == "start":
            local.start()
            for i, (dev, idx) in enumerate(peers):
                copy(i, ins[k].at[idx] if mode == "scatter" else ins[k], me, dev).start()
        elif phase == "end":
            for i, (dev, idx) in enumerate(peers):
                copy(i, src_me, idx, dev).wait_recv()
            for i, (dev, idx) in enumerate(peers):
                copy(i, src_me, me, dev).wait_send()
            local.wait()


def _exchange_shapes(arrays, modes):
    return [jax.ShapeDtypeStruct((N_DEV,) + (a.shape[1:] if m == "scatter" else a.shape), a.dtype)
            for a, m in zip(arrays, modes)]


def _exchange_sems(n):
    return [pltpu.SemaphoreType.DMA((n, N_DEV - 1)), pltpu.SemaphoreType.DMA((n, N_DEV - 1)),
            pltpu.SemaphoreType.DMA((n,))]


def _exchange(name, arrays, modes):
    n = len(arrays)

    def body(*refs):
        for phase in ("start", "mid", "end"):
            _exchange_phase(phase, refs[:n], refs[n:2 * n], modes, *refs[2 * n:])

    any_spec = pl.BlockSpec(memory_space=pl.ANY)
    return pl.pallas_call(
        body, name=name, out_shape=_exchange_shapes(arrays, modes),
        in_specs=[any_spec] * n, out_specs=[any_spec] * n, scratch_shapes=_exchange_sems(n),
    )(*arrays)


def _call(body, *, name, grid, in_specs, out_specs, out_shape, args, sem, scratch_shapes=(), comm=None):
    if comm is None:
        outs = pl.pallas_call(body, name=name, grid=grid, in_specs=list(in_specs), out_specs=list(out_specs),
                              out_shape=list(out_shape), scratch_shapes=list(scratch_shapes),
                              compiler_params=_params(sem))(*args)
        return list(outs), []
    arrays, modes = comm
    n_in, n_out, nc, ns = len(in_specs), len(out_specs), len(arrays), len(scratch_shapes)
    n_steps = math.prod(grid)

    def hosted(*refs):
        ins, cins = refs[:n_in], refs[n_in:n_in + nc]
        outs, couts = refs[n_in + nc:n_in + nc + n_out], refs[n_in + nc + n_out:n_in + 2 * nc + n_out]
        scratch = refs[n_in + 2 * nc + n_out:]
        ex = (cins, couts, modes) + tuple(scratch[ns:])
        step = pl.program_id(0)
        for ax in range(1, len(grid)):
            step = step * grid[ax] + pl.program_id(ax)

        @pl.when(step == 0)
        def _():
            _exchange_phase("start", *ex)

        body(*ins, *outs, *scratch[:ns])

        if "gather2" in modes:
            @pl.when(step == (3 * n_steps) // 4)
            def _():
                _exchange_phase("mid", *ex)

        @pl.when(step == n_steps - 1)
        def _():
            _exchange_phase("end", *ex)

    any_spec = pl.BlockSpec(memory_space=pl.ANY)
    res = pl.pallas_call(
        hosted, name=name, grid=grid, in_specs=list(in_specs) + [any_spec] * nc,
        out_specs=list(out_specs) + [any_spec] * nc, out_shape=list(out_shape) + _exchange_shapes(arrays, modes),
        scratch_shapes=list(scratch_shapes) + _exchange_sems(nc),
        compiler_params=_params(tuple("arbitrary" for _ in grid)))(*args, *arrays)
    return list(res[:n_out]), list(res[n_out:])


def _mod_partial(c_all, w_ada, b_ada_cols):
    def body(c_ref, w_ref, b_ref, o_ref):
        cv = c_ref[...]
        s = (cv * _sigmoid(cv)).astype(BF16)
        o_ref[...] = _dot(s, w_ref[...].astype(BF16)) + b_ref[...]

    ncol = w_ada.shape[1]
    return pl.pallas_call(
        body, name="mod_partial", out_shape=jax.ShapeDtypeStruct((N_DEV, ncol), F32),
        in_specs=[_full(c_all.shape), _full(w_ada.shape), _full(b_ada_cols.shape)],
        out_specs=_full((N_DEV, ncol)), grid=(1,), compiler_params=_params(("arbitrary",)),
    )(c_all, w_ada, b_ada_cols)


def _bias_table(rel_bias, bucket):
    def body(rb_ref, bk_ref, o_ref):
        h = pl.program_id(0)
        bk = bk_ref[...]
        acc = jnp.zeros((BLK, 2 * BLK), F32)
        for b in range(N_BUCKETS):
            acc = jnp.where(bk == b, rb_ref[b, h], acc)
        dist = (lax.broadcasted_iota(jnp.int32, (BLK, 2 * BLK), 0) + BLK
                - lax.broadcasted_iota(jnp.int32, (BLK, 2 * BLK), 1))
        o_ref[0] = jnp.where((dist >= 0) & (dist < BLK), acc, NEG_INF)

    return pl.pallas_call(
        body, name="bias_table", out_shape=jax.ShapeDtypeStruct((N_HEADS, BLK, 2 * BLK), F32),
        in_specs=[pl.BlockSpec(memory_space=pltpu.SMEM), _full((BLK, 2 * BLK))],
        out_specs=pl.BlockSpec((1, BLK, 2 * BLK), lambda h: (h, 0, 0)), grid=(N_HEADS,),
        compiler_params=_params(("arbitrary",)),
    )(rel_bias, bucket)


def _bias_grad(dl_acc, bucket):
    def body(dl_ref, bk_ref, o_ref):
        bk = bk_ref[...]
        dl = dl_ref[0]
        lane = lax.broadcasted_iota(jnp.int32, (1, 128), 1)
        row = jnp.zeros((1, 128), F32)
        for b in range(N_BUCKETS):
            s = jnp.sum(jnp.sum(jnp.where(bk == b, dl, 0.0), axis=1, keepdims=True), axis=0, keepdims=True)
            row = jnp.where(lane == b, s, row)
        o_ref[0] = row

    return pl.pallas_call(
        body, name="bias_grad", out_shape=jax.ShapeDtypeStruct((N_HEADS, 1, 128), F32),
        in_specs=[pl.BlockSpec((1, BLK, 2 * BLK), lambda h: (h, 0, 0)), _full((BLK, 2 * BLK))],
        out_specs=pl.BlockSpec((1, 1, 128), lambda h: (h, 0, 0)), grid=(N_HEADS,),
        compiler_params=_params(("arbitrary",)),
    )(dl_acc, bucket)


def _inproj(x, sc1, sh1, w_in_t, b_in, tm):
    t, d = x.shape
    n = w_in_t.shape[0]

    def body(x_ref, sc_ref, sh_ref, w_ref, b_ref, proj_ref, h_ref):
        h = (x_ref[...] * (1.0 + sc_ref[...]) + sh_ref[...]).astype(BF16)
        h_ref[...] = h
        proj_ref[...] = _dot_nt(h, w_ref[...]) + b_ref[...]

    row = lambda w: pl.BlockSpec((tm, w), lambda i: (i, 0))
    return pl.pallas_call(
        body, name="inproj", grid=(t // tm,),
        out_shape=[jax.ShapeDtypeStruct((t, n), F32), jax.ShapeDtypeStruct((t, d), BF16)],
        in_specs=[row(d), _full((1, d)), _full((1, d)), _full((n, d)), _full((1, n))],
        out_specs=[row(n), row(d)], compiler_params=_params(("parallel",)),
    )(x, sc1, sh1, w_in_t, b_in)


HALF = 64
ROWS = 32


def _lane_lo(rows):
    return lax.broadcasted_iota(jnp.int32, (rows, 128), 1) < 64


def _mix_stage_kv(proj_ref, kvp_ref, s):
    lo = _lane_lo(2 * BLK)
    for name, col in (("k", ATTN_W), ("v", ATTN_W + KV_W)):
        cur = jnp.concatenate([kvp_ref[:, col - ATTN_W:col - ATTN_W + KV_W], proj_ref[:, col:col + KV_W]], axis=0)
        plain, swapped = cur.astype(BF16), pltpu.roll(cur, 64, 1).astype(BF16)
        zero = jnp.zeros_like(plain)
        for g in range(2):
            dup = jnp.where(lo, plain, swapped) if g == 0 else jnp.where(lo, swapped, plain)
            s[name + "d"][g] = dup
            s[name + "m"][g] = jnp.concatenate([jnp.where(lo, dup, zero), jnp.where(lo, zero, dup)], axis=0)


def _group_rows(ref, g):
    return ref[4 * g:4 * g + 4].reshape(4 * BLK, ref.shape[2])


def _pair_rows(ref, g):
    return jnp.concatenate([jnp.concatenate([ref[4 * g + 2 * c], ref[4 * g + 2 * c + 1]], axis=1) for c in range(2)],
                           axis=0)


def _mask_heads(src_ref, dst_ref):
    lo = _lane_lo(BLK)
    for j in range(4):
        chunk = src_ref[:, 128 * j:128 * (j + 1)]
        dst_ref[2 * j] = jnp.where(lo, chunk, 0.0).astype(BF16)
        dst_ref[2 * j + 1] = jnp.where(lo, 0.0, chunk).astype(BF16)


def _mix_stage_attn(proj_ref, bias_ref, sinks_ref, n, s):
    _mask_heads(proj_ref, s["qm"])
    for g in range(2):
        s["lg"][g] = _dot_nt(_group_rows(s["qm"], g), s["kd"][g])
    n0mask = (n == 0) & (lax.broadcasted_iota(jnp.int32, (HALF, 2 * BLK), 1) < BLK)
    lane = lax.broadcasted_iota(jnp.int32, (HALF, 128), 1)
    for hf in range(BLK // HALF):
        rows = slice(HALF * hf, HALF * (hf + 1))
        psink = jnp.zeros((HALF, 128), F32)
        for h in range(N_HEADS):
            sk = sinks_ref[h]
            grows = slice(BLK * (h % 4) + HALF * hf, BLK * (h % 4) + HALF * (hf + 1))
            logit = s["lg"][h // 4, grows, :] * (HEAD_DIM ** -0.5) + bias_ref[h, rows, :]
            logit = jnp.where(n0mask, NEG_INF, logit)
            m = jnp.maximum(jnp.max(logit, axis=1, keepdims=True), sk)
            e = jnp.exp(logit - m)
            es = jnp.exp(sk - m)
            inv = 1.0 / (jnp.sum(e, axis=1, keepdims=True) + es)
            p = e * inv
            s["p"][h, rows, :] = p
            s["pb"][h, rows, :] = p.astype(BF16)
            psink = jnp.where(lane == h, es * inv, psink)
        s["psink"][rows, :] = psink
    for g in range(2):
        out = _dot(_pair_rows(s["pb"], g), s["vm"][g])
        s["attn"][:, 256 * g:256 * g + 128] = out[0:BLK]
        s["attn"][:, 256 * g + 128:256 * g + 256] = out[BLK:2 * BLK]


def _mix_stage_gmlp_pre(proj_ref, lng, lnb, s, keep):
    c0 = ATTN_W + 2 * KV_W
    for r0 in range(0, BLK, ROWS):
        rows = slice(r0, r0 + ROWS)
        u, du = _gelu_parts(proj_ref[rows, c0:c0 + GMLP_W])
        a, da = _gelu_parts(proj_ref[rows, c0 + GMLP_W:c0 + 2 * GMLP_W])
        ac = a - _seg_mean64(a)
        rstd = lax.rsqrt(_seg_mean64(ac * ac) + LN_EPS)
        vhat = ac * rstd
        s["u"][rows, :] = u
        s["vnb"][rows, :] = (vhat * lng + lnb).astype(BF16)
        if keep:
            s["du"][rows, :] = du
            s["da"][rows, :] = da
            s["vhat"][rows, :] = vhat
            s["rstd"][rows, :] = rstd


def _stack_halves(chunk):
    lo = _lane_lo(BLK)
    zero = jnp.zeros_like(chunk)
    return jnp.concatenate([jnp.where(lo, chunk, zero), jnp.where(lo, zero, chunk)], axis=0)


def _mix_stage_gmlp_mix(ws2_ref, bfull_ref, s):
    for j in range(4):
        cols = slice(128 * j, 128 * (j + 1))
        s["ms"][:, cols] = _dot(ws2_ref[j], _stack_halves(s["vnb"][:, cols])) + bfull_ref[:, cols]


def _mix_scratch(keep):
    f32 = lambda *shape: pltpu.VMEM(shape, F32)
    b16 = lambda *shape: pltpu.VMEM(shape, BF16)
    names = dict(kd=b16(2, 2 * BLK, 128), vd=b16(2, 2 * BLK, 128), km=b16(2, 4 * BLK, 128), vm=b16(2, 4 * BLK, 128),
                 qm=b16(N_HEADS, BLK, 128), lg=f32(2, 4 * BLK, 2 * BLK), pb=b16(N_HEADS, BLK, 2 * BLK),
                 u=f32(BLK, GMLP_W), vnb=b16(BLK, GMLP_W), ms=f32(BLK, GMLP_W))
    if keep:
        names.update(dom=b16(N_HEADS, BLK, 128), dls=b16(N_HEADS, BLK, 2 * BLK),
                     dattn=f32(BLK, ATTN_W), dmix=f32(BLK, D_MODEL), du=f32(BLK, GMLP_W), da=f32(BLK, GMLP_W),
                     vhat=f32(BLK, GMLP_W), rstd=f32(BLK, GMLP_W), dmsb=b16(BLK, GMLP_W), dvn=f32(BLK, GMLP_W))
    return list(names), list(names.values())


def _mix_specs(with_logit_inputs):
    logit_inputs = [_full((N_HEADS, BLK, 2 * BLK)), pl.BlockSpec(memory_space=pltpu.SMEM)] if with_logit_inputs else []
    return [pl.BlockSpec((BLK, IN_W), lambda n: (n, 0)),
            pl.BlockSpec((BLK, 2 * KV_W), lambda n: (jnp.maximum(n - 1, 0), ATTN_W // (2 * KV_W)))] + logit_inputs + [
            _full((1, GMLP_W)), _full((1, GMLP_W)),
            _full((N_GROUPS // 2, BLK, 2 * BLK)), _full((BLK, GMLP_W)),
            _full((1, ATTN_W)), _full((1, GMLP_W))]


KEPT = [("p", (N_HEADS, BLK, 2 * BLK), F32), ("psink", (BLK, 128), F32), ("attn", (BLK, ATTN_W), F32)]


def _kept_shapes(t):
    full = lambda blk: (blk[0], t, blk[2]) if len(blk) == 3 else (t, blk[1])
    return [jax.ShapeDtypeStruct(full(blk), dt) for _, blk, dt in KEPT]


def _kept_specs():
    return [pl.BlockSpec(blk, (lambda n: (0, n, 0)) if len(blk) == 3 else (lambda n: (n, 0))) for _, blk, _ in KEPT]


def _mix_fwd(proj, bias, sinks, lng, lnb, ws2, bfull, aog, gog, comm):
    t = proj.shape[0]
    names, shapes = _mix_scratch(False)

    def body(proj_ref, kvp_ref, bias_ref, sinks_ref, lng_ref, lnb_ref, ws2_ref, bfull_ref, aog_ref, gog_ref,
             out_ref, *rest):
        s = dict(zip([name for name, _, _ in KEPT] + names, rest))
        n = pl.program_id(0)
        _mix_stage_kv(proj_ref, kvp_ref, s)
        _mix_stage_attn(proj_ref, bias_ref, sinks_ref, n, s)
        _mix_stage_gmlp_pre(proj_ref, lng_ref[...], lnb_ref[...], s, False)
        _mix_stage_gmlp_mix(ws2_ref, bfull_ref, s)
        for r0 in range(0, BLK, ROWS):
            rows = slice(r0, r0 + ROWS)
            out_ref[rows, 0:ATTN_W] = _rms(s["attn"][rows, :], aog_ref[...])[0].astype(BF16)
            out_ref[rows, ATTN_W:ATTN_W + GMLP_W] = _rms(s["u"][rows, :] * s["ms"][rows, :], gog_ref[...])[0].astype(BF16)

    return _call(
        body, name="mix_fwd", grid=(t // BLK,),
        out_shape=[jax.ShapeDtypeStruct((t, D_MODEL), BF16)] + _kept_shapes(t),
        in_specs=_mix_specs(True), out_specs=[pl.BlockSpec((BLK, D_MODEL), lambda n: (n, 0))] + _kept_specs(),
        scratch_shapes=shapes,
        sem=("parallel",), comm=comm, args=(proj, proj, bias, sinks, lng, lnb, ws2, bfull, aog, gog))


def _mix_bwd(proj, lng, lnb, ws2, wst2, bfull, aog, gog, dy, w_out, kept, comm):
    t = proj.shape[0]
    nb = t // BLK
    names, shapes = _mix_scratch(True)
    c_gu = ATTN_W + 2 * KV_W

    def body(proj_ref, kvp_ref, lng_ref, lnb_ref, ws2_ref, bfull_ref, aog_ref, gog_ref,
             wst2_ref, dy_ref, wout_ref, *rest):
        n_kept = len(KEPT)
        s = dict(zip([name for name, _, _ in KEPT], rest[:n_kept]))
        (dproj_ref, dkvn_ref, dl_ref, dsink_ref, dlng_ref, dlnb_ref, dws_ref, dbs_ref, daog_ref,
         dgog_ref) = rest[n_kept:n_kept + 10]
        s.update(zip(names, rest[n_kept + 10:]))
        n = pl.program_id(0)

        @pl.when(n == 0)
        def _():
            for r in (dl_ref, dsink_ref, dlng_ref, dlnb_ref, dws_ref, dbs_ref, daog_ref, dgog_ref):
                r[...] = jnp.zeros_like(r)

        s["dmix"][...] = _dot_nt(dy_ref[...], wout_ref[...])
        _mix_stage_kv(proj_ref, kvp_ref, s)
        _mask_heads(proj_ref, s["qm"])
        lng = lng_ref[...]
        _mix_stage_gmlp_pre(proj_ref, lng, lnb_ref[...], s, True)
        _mix_stage_gmlp_mix(ws2_ref, bfull_ref, s)

        aog, gog = aog_ref[...], gog_ref[...]
        for r0 in range(0, BLK, ROWS):
            rows = slice(r0, r0 + ROWS)
            attn, dma = s["attn"][rows, :], s["dmix"][rows, 0:ATTN_W]
            _, r_a = _rms(attn, aog)
            daog_ref[...] += _rowsum8(dma * attn * r_a)
            s["dattn"][rows, :] = _rms_bwd(dma, attn, r_a, aog)
            u, ms, dmg = s["u"][rows, :], s["ms"][rows, :], s["dmix"][rows, ATTN_W:ATTN_W + GMLP_W]
            gm = u * ms
            _, r_g = _rms(gm, gog)
            dgog_ref[...] += _rowsum8(dmg * gm * r_g)
            dgm = _rms_bwd(dmg, gm, r_g, gog)
            dproj_ref[rows, c_gu:c_gu + GMLP_W] = dgm * ms * s["du"][rows, :]
            dms = dgm * u
            dbs_ref[rows, :] += dms
            s["dmsb"][rows, :] = dms.astype(BF16)

        _mask_heads(s["dattn"], s["dom"])
        for g in range(2):
            s["lg"][g] = _dot_nt(_group_rows(s["dom"], g), s["vd"][g])
        lane = lax.broadcasted_iota(jnp.int32, (HALF, 128), 1)
        for hf in range(BLK // HALF):
            rows = slice(HALF * hf, HALF * (hf + 1))
            dsink = jnp.zeros((HALF, 128), F32)
            for h in range(N_HEADS):
                grows = slice(BLK * (h % 4) + HALF * hf, BLK * (h % 4) + HALF * (hf + 1))
                dp = s["lg"][h // 4, grows, :]
                p = s["p"][h, rows, :]
                s["pb"][h, rows, :] = p.astype(BF16)
                rs = jnp.sum(p * dp, axis=1, keepdims=True)
                dl = p * (dp - rs)
                dl_ref[h, rows, :] += dl
                dsink = dsink + jnp.where(lane == h, -s["psink"][rows, :] * rs, 0.0)
                s["dls"][h, rows, :] = (dl * (HEAD_DIM ** -0.5)).astype(BF16)
            dsink_ref[rows, :] += dsink
        for g in range(2):
            dq = _dot(_pair_rows(s["dls"], g), s["km"][g])
            dproj_ref[:, 256 * g:256 * g + 128] = dq[0:BLK]
            dproj_ref[:, 256 * g + 128:256 * g + 256] = dq[BLK:2 * BLK]
        lo_k = _lane_lo(2 * BLK)
        for col, lhs, rhs in ((0, "dls", "qm"), (KV_W, "pb", "dom")):
            raw = [_dot_tn(_group_rows(s[lhs], g), _group_rows(s[rhs], g)) for g in range(2)]
            both = [r + pltpu.roll(r, 64, 1) for r in raw]
            dkv = jnp.where(lo_k, both[0], both[1])
            dproj_ref[:, ATTN_W + col:ATTN_W + col + KV_W] = dkv[BLK:2 * BLK]
            dkvn_ref[:, col:col + KV_W] = dkv[0:BLK]

        for j in range(4):
            cols = slice(128 * j, 128 * (j + 1))
            dm2 = _stack_halves(s["dmsb"][:, cols])
            vnb = s["vnb"][:, cols]
            dws2 = _dot_nt(dm2, vnb)
            dws_ref[2 * j] += dws2[0:BLK]
            dws_ref[2 * j + 1] += dws2[BLK:2 * BLK]
            s["dvn"][:, cols] = _dot(wst2_ref[j], dm2)
        for r0 in range(0, BLK, ROWS):
            rows = slice(r0, r0 + ROWS)
            dvn, vhat = s["dvn"][rows, :], s["vhat"][rows, :]
            dlng_ref[...] += _rowsum8(dvn * vhat)
            dlnb_ref[...] += _rowsum8(dvn)
            dvh = dvn * lng
            dact = s["rstd"][rows, :] * (dvh - _seg_mean64(dvh) - vhat * _seg_mean64(dvh * vhat))
            dproj_ref[rows, c_gu + GMLP_W:IN_W] = dact * s["da"][rows, :]

    acc8 = lambda w: jax.ShapeDtypeStruct((8, w), F32)
    out_shape = [jax.ShapeDtypeStruct((t, IN_W), F32), jax.ShapeDtypeStruct((t, 2 * KV_W), F32),
                 jax.ShapeDtypeStruct((N_HEADS, BLK, 2 * BLK), F32), jax.ShapeDtypeStruct((BLK, 128), F32),
                 acc8(GMLP_W), acc8(GMLP_W), jax.ShapeDtypeStruct((N_GROUPS, BLK, BLK), F32),
                 jax.ShapeDtypeStruct((BLK, GMLP_W), F32), acc8(ATTN_W), acc8(GMLP_W)]
    out_specs = [pl.BlockSpec((BLK, IN_W), lambda n: (n, 0)),
                 pl.BlockSpec((BLK, 2 * KV_W), lambda n: ((n + nb - 1) % nb, 0)),
                 _full((N_HEADS, BLK, 2 * BLK)), _full((BLK, 128)), _full((8, GMLP_W)), _full((8, GMLP_W)),
                 _full((N_GROUPS, BLK, BLK)), _full((BLK, GMLP_W)), _full((8, ATTN_W)), _full((8, GMLP_W))]
    in_specs = _mix_specs(False) + [_full((N_GROUPS // 2, BLK, 2 * BLK)),
                               pl.BlockSpec((BLK, D_MODEL), lambda n: (n, 0)),
                               _full((D_MODEL, D_MODEL))] + _kept_specs()
    return _call(
        body, name="mix_bwd", grid=(nb,), out_shape=out_shape, in_specs=in_specs, out_specs=out_specs,
        scratch_shapes=shapes, sem=("arbitrary",), comm=comm,
        args=(proj, proj, lng, lnb, ws2, bfull, aog, gog, wst2, dy, w_out, *kept))


def _outproj(mixed, w_out, x, g1, ln1g, ln1b, sc2, sh2, tm, comm):
    t, d = x.shape

    def body(mx_ref, w_ref, x_ref, g1_ref, lg_ref, lb_ref, sc_ref, sh_ref, y_ref, x1_ref, h2_ref):
        y = _dot(mx_ref[...], w_ref[...])
        xhat, _ = _ln_stats(ALPHA * x_ref[...] + g1_ref[...] * y)
        x1 = xhat * lg_ref[...] + lb_ref[...]
        y_ref[...] = y
        x1_ref[...] = x1
        h2_ref[...] = (x1 * (1.0 + sc_ref[...]) + sh_ref[...]).astype(BF16)

    row = pl.BlockSpec((tm, d), lambda i: (i, 0))
    vec = _full((1, d))
    return _call(
        body, name="outproj", grid=(t // tm,),
        out_shape=[jax.ShapeDtypeStruct((t, d), F32), jax.ShapeDtypeStruct((t, d), F32),
                   jax.ShapeDtypeStruct((t, d), BF16)],
        in_specs=[row, _full((d, d)), row, vec, vec, vec, vec, vec], out_specs=[row, row, row],
        sem=("parallel",), comm=comm, args=(mixed, w_out, x, g1, ln1g, ln1b, sc2, sh2))


def _ffn_fwd(h2, w_gu_t, w_down, x1, target, g2, ln2g, ln2b, tm):
    t, d = x1.shape

    def body(h_ref, w_ref, wd_ref, x1_ref, tg_ref, g2_ref, lg_ref, lb_ref,
             dsu_ref, sg_ref, act_ref, dz_ref, dy_ref, loss_ref, dlg_ref, dlb_ref, dg2_ref):
        @pl.when(pl.program_id(0) == 0)
        def _():
            for r in (loss_ref, dlg_ref, dlb_ref, dg2_ref):
                r[...] = jnp.zeros_like(r)

        h = h_ref[...]
        g = _dot_nt(h, w_ref[0:D_FF])
        u = _dot_nt(h, w_ref[D_FF:2 * D_FF])
        s = _sigmoid(g)
        sg = g * s
        act = (sg * u).astype(BF16)
        dsu_ref[...] = (u * (s * (1.0 + g * (1.0 - s)))).astype(BF16)
        sg_ref[...] = sg.astype(BF16)
        act_ref[...] = act
        y2 = _dot(act, wd_ref[...])
        g2 = g2_ref[...]
        lg = lg_ref[...]
        xhat, rstd = _ln_stats(ALPHA * x1_ref[...] + g2 * y2)
        err = xhat * lg + lb_ref[...] - tg_ref[...]
        loss_ref[...] += _rowsum8(err * err)
        dx2 = err * (1.0 / d)
        dlg_ref[...] += _rowsum8(dx2 * xhat)
        dlb_ref[...] += _rowsum8(dx2)
        dz = _ln_bwd(dx2 * lg, xhat, rstd)
        dg2_ref[...] += _rowsum8(dz * y2)
        dz_ref[...] = dz
        dy_ref[...] = (g2 * dz).astype(BF16)

    row = pl.BlockSpec((tm, d), lambda i: (i, 0))
    wide = pl.BlockSpec((tm, D_FF), lambda i: (i, 0))
    vec = _full((1, d))
    acc = _full((8, d))
    acc_shape = jax.ShapeDtypeStruct((8, d), F32)
    wide_shape = jax.ShapeDtypeStruct((t, D_FF), BF16)
    return pl.pallas_call(
        body, name="ffn_fwd", grid=(t // tm,),
        out_shape=[wide_shape] * 3 + [jax.ShapeDtypeStruct((t, d), F32), jax.ShapeDtypeStruct((t, d), BF16)]
        + [acc_shape] * 4,
        in_specs=[row, _resident((2 * D_FF, d)), _resident((D_FF, d)), row, row, vec, vec, vec],
        out_specs=[wide] * 3 + [row, row, acc, acc, acc, acc], compiler_params=_params(("arbitrary",)),
    )(h2, w_gu_t, w_down, x1, target, g2, ln2g, ln2b)


def _resident(shape):
    nd = len(shape)
    return pl.BlockSpec(shape, lambda *_: (0,) * nd, pipeline_mode=pl.Buffered(1))


def _ffn_bwd(dy2, w_down, dsu, sg, w_gu_t, x1, x, y, dz2, sc2, g1, ln1g, tm, comm):
    t, d = x1.shape

    def body(dy2_ref, wd_ref, dsu_ref, sg_ref, w_ref, x1_ref, x_ref, y_ref, dz2_ref, sc_ref, g1_ref, lg_ref,
             dg_ref, du_ref, dz1_ref, dy_ref, dsc_ref, dsh_ref, dlg_ref, dlb_ref, dg1_ref):
        @pl.when(pl.program_id(0) == 0)
        def _():
            for r in (dsc_ref, dsh_ref, dlg_ref, dlb_ref, dg1_ref):
                r[...] = jnp.zeros_like(r)

        dact = _dot_nt(dy2_ref[...], wd_ref[...])
        dg = (dact * dsu_ref[...].astype(F32)).astype(BF16)
        du = (dact * sg_ref[...].astype(F32)).astype(BF16)
        dg_ref[...] = dg
        du_ref[...] = du
        dh2 = _dot(dg, w_ref[0:D_FF]) + _dot(du, w_ref[D_FF:2 * D_FF])
        x1 = x1_ref[...]
        y = y_ref[...]
        g1 = g1_ref[...]
        dsc_ref[...] += _rowsum8(dh2 * x1)
        dsh_ref[...] += _rowsum8(dh2)
        dx1 = dh2 * (1.0 + sc_ref[...]) + ALPHA * dz2_ref[...]
        xhat, rstd = _ln_stats(ALPHA * x_ref[...] + g1 * y)
        dlg_ref[...] += _rowsum8(dx1 * xhat)
        dlb_ref[...] += _rowsum8(dx1)
        dz1 = _ln_bwd(dx1 * lg_ref[...], xhat, rstd)
        dg1_ref[...] += _rowsum8(dz1 * y)
        dz1_ref[...] = dz1
        dy_ref[...] = (g1 * dz1).astype(BF16)

    row = pl.BlockSpec((tm, d), lambda i: (i, 0))
    wide = pl.BlockSpec((tm, D_FF), lambda i: (i, 0))
    vec = _full((1, d))
    acc = _full((8, d))
    acc_shape = jax.ShapeDtypeStruct((8, d), F32)
    wide_shape = jax.ShapeDtypeStruct((t, D_FF), BF16)
    return _call(
        body, name="ffn_bwd", grid=(t // tm,),
        out_shape=[wide_shape, wide_shape, jax.ShapeDtypeStruct((t, d), F32), jax.ShapeDtypeStruct((t, d), BF16)]
        + [acc_shape] * 5,
        in_specs=[row, _resident((D_FF, d)), wide, wide, _resident((2 * D_FF, d)), row, row, row, row, vec, vec, vec],
        out_specs=[wide, wide, row, row, acc, acc, acc, acc, acc], sem=("arbitrary",), comm=comm,
        args=(dy2, w_down, dsu, sg, w_gu_t, x1, x, y, dz2, sc2, g1, ln1g))


def _din(dproj, dkvn, w_in_t, x, dz1, sc1, tm, comm):
    t, d = x.shape

    def body(dp_ref, dkv_ref, w_ref, x_ref, dz1_ref, sc_ref, dx_ref, dpb_ref, dbin_ref, dsc_ref, dsh_ref):
        @pl.when(pl.program_id(0) == 0)
        def _():
            for r in (dbin_ref, dsc_ref, dsh_ref):
                r[...] = jnp.zeros_like(r)

        dp = jnp.concatenate([dp_ref[:, 0:ATTN_W], dp_ref[:, ATTN_W:ATTN_W + 2 * KV_W] + dkv_ref[...],
                              dp_ref[:, ATTN_W + 2 * KV_W:IN_W]], axis=1)
        dbin_ref[...] += _rowsum8(dp)
        dpb = dp.astype(BF16)
        dpb_ref[...] = dpb
        dh = _dot(dpb, w_ref[...])
        dsc_ref[...] += _rowsum8(dh * x_ref[...])
        dsh_ref[...] += _rowsum8(dh)
        dx_ref[...] = dh * (1.0 + sc_ref[...]) + ALPHA * dz1_ref[...]

    row = lambda w: pl.BlockSpec((tm, w), lambda i: (i, 0))
    return _call(
        body, name="din", grid=(t // tm,),
        out_shape=[jax.ShapeDtypeStruct((t, d), F32), jax.ShapeDtypeStruct((t, IN_W), BF16),
                   jax.ShapeDtypeStruct((8, IN_W), F32), jax.ShapeDtypeStruct((8, d), F32),
                   jax.ShapeDtypeStruct((8, d), F32)],
        in_specs=[row(IN_W), row(2 * KV_W), _full((IN_W, d)), row(d), row(d), _full((1, d))],
        out_specs=[row(d), row(IN_W), _full((8, IN_W)), _full((8, d)), _full((8, d))],
        sem=("arbitrary",), comm=comm, args=(dproj, dkvn, w_in_t, x, dz1, sc1))


def _wgrad(name, a, b, tmm, tk, comm=None, a2=None):
    t, m = a.shape
    n = b.shape[1]
    nk = t // tk
    nm = m // tmm

    def body(*refs):
        a_refs, (b_ref, o_ref, acc_ref) = refs[:-3], refs[-3:]
        i, k = pl.program_id(0), pl.program_id(1)

        @pl.when(k == 0)
        def _():
            acc_ref[...] = jnp.zeros_like(acc_ref)

        a_tile = a_refs[0][...] if a2 is None else jnp.where(i < nm, a_refs[0][...], a_refs[1][...])
        acc_ref[...] += _dot_tn(a_tile, b_ref[...])

        @pl.when(k == nk - 1)
        def _():
            o_ref[...] = acc_ref[...].astype(BF16)

    if a2 is None:
        a_specs, a_args, n_tiles = [pl.BlockSpec((tk, tmm), lambda i, k: (k, i))], (a,), nm
    else:
        a_specs = [pl.BlockSpec((tk, tmm), lambda i, k: (jnp.where(i < nm, k, 0), jnp.minimum(i, nm - 1))),
                   pl.BlockSpec((tk, tmm), lambda i, k: (jnp.where(i < nm, 0, k), jnp.maximum(i - nm, 0)))]
        a_args, n_tiles = (a, a2), 2 * nm
    (out,), got = _call(
        body, name=name, grid=(n_tiles, nk), out_shape=[jax.ShapeDtypeStruct((n_tiles * tmm, n), BF16)],
        in_specs=a_specs + [pl.BlockSpec((tk, n), lambda i, k: (k, 0))],
        out_specs=[pl.BlockSpec((tmm, n), lambda i, k: (i, 0))],
        scratch_shapes=[pltpu.VMEM((tmm, n), F32)], sem=("parallel", "arbitrary"), comm=comm, args=a_args + (b,))
    return out if comm is None else (out, got)


def _adamw(w, g, m, v):
    m = ADAM_B1 * m + (1.0 - ADAM_B1) * g
    v = ADAM_B2 * v + (1.0 - ADAM_B2) * (g * g)
    m_hat = m / (1.0 - ADAM_B1 ** ADAM_STEP)
    v_hat = v / (1.0 - ADAM_B2 ** ADAM_STEP)
    delta = -ADAM_LR * (m_hat / (jnp.sqrt(v_hat) + ADAM_EPS) + ADAM_WD * w)
    return delta, m, v


def _adam_reduce(name, parts, w, m, v, tr):
    r, cdim = w.shape

    def body(p_ref, w_ref, m_ref, v_ref, g_ref, d_ref, mo_ref, vo_ref):
        g = p_ref[0].astype(F32)
        for s in range(1, N_DEV):
            g = g + p_ref[s].astype(F32)
        d_ref[...], mo_ref[...], vo_ref[...] = _adamw(w_ref[...], g, m_ref[...], v_ref[...])
        g_ref[...] = g

    tile = pl.BlockSpec((tr, cdim), lambda i: (i, 0))
    shp = jax.ShapeDtypeStruct((r, cdim), F32)
    return pl.pallas_call(
        body, name=name, grid=(r // tr,), out_shape=[shp] * 4,
        in_specs=[pl.BlockSpec((N_DEV, tr, cdim), lambda i: (0, i, 0)), tile, tile, tile],
        out_specs=[tile] * 4, compiler_params=_params(("parallel",)),
    )(parts, w, m, v)


def _adam_w_ada(c_all_t, dmod_cols, w, m, v):
    def body(ct_ref, dm_ref, w_ref, m_ref, v_ref, g_ref, d_ref, mo_ref, vo_ref):
        ct = ct_ref[...]
        s = (ct * _sigmoid(ct)).astype(BF16)
        g = _dot(s, dm_ref[...].astype(BF16))
        d_ref[...], mo_ref[...], vo_ref[...] = _adamw(w_ref[...], g, m_ref[...], v_ref[...])
        g_ref[...] = g

    shp = jax.ShapeDtypeStruct(w.shape, F32)
    return pl.pallas_call(
        body, name="adam_w_ada", grid=(1,), out_shape=[shp] * 4,
        in_specs=[_full(c_all_t.shape), _full(dmod_cols.shape)] + [_full(w.shape)] * 3,
        out_specs=[_full(w.shape)] * 4, compiler_params=_params(("arbitrary",)),
    )(c_all_t, dmod_cols, w, m, v)


SMALL_EARLY = ["rel_bias", "attn_sinks", "gmlp_ln_g", "gmlp_ln_b", "gmlp_w_s", "gmlp_b_s",
               "attn_out_g", "gmlp_out_g", "ln1_g", "ln1_b", "ln2_g", "ln2_b"]
SMALL_LATE = ["b_ada", "b_in", "loss"]
WEIGHTS = ["rel_bias", "w_ada", "b_ada", "w_in", "b_in", "attn_sinks", "gmlp_ln_g", "gmlp_ln_b", "gmlp_w_s",
           "gmlp_b_s", "attn_out_g", "gmlp_out_g", "w_out", "ln1_g", "ln1_b", "w_gate_up", "w_down", "ln2_g", "ln2_b"]


def _seg_rows(nelem):
    return -(-nelem // 1024) * 8


def _pack(named, names):
    parts = []
    for name in names:
        flat = named[name].reshape(-1).astype(F32)
        rows = _seg_rows(flat.shape[0])
        parts.append(jnp.pad(flat, (0, rows * 128 - flat.shape[0])).reshape(rows, 128))
    return jnp.concatenate(parts, axis=0)


def _adam_small(name, parts, names, wts, mom_m, mom_v):
    params = [n for n in names if n in wts]

    def view(n):
        nelem = math.prod(wts[n].shape)
        return (nelem // 128, 128) if nelem % 128 == 0 else (1, nelem)

    offsets, r0 = {}, 0
    for n in names:
        offsets[n] = r0
        r0 += _seg_rows(math.prod(wts[n].shape) if n in wts else 1)

    def body(*refs):
        p_ref, ins, outs = refs[0], refs[1:1 + 3 * len(params)], refs[1 + 3 * len(params):]

        def total(n, rows, lanes):
            o = offsets[n]
            g = p_ref[0, o:o + rows, 0:lanes]
            for s in range(1, N_DEV):
                g = g + p_ref[s, o:o + rows, 0:lanes]
            return g

        for i, n in enumerate(params):
            g = total(n, *view(n))
            w_ref, m_ref, v_ref = ins[3 * i:3 * i + 3]
            g_ref, d_ref, mo_ref, vo_ref = outs[4 * i:4 * i + 4]
            d_ref[...], mo_ref[...], vo_ref[...] = _adamw(w_ref[...], g, m_ref[...], v_ref[...])
            g_ref[...] = g
        for j, n in enumerate(n for n in names if n not in wts):
            outs[4 * len(params) + j][...] = total(n, 8, 128)

    args, in_specs, out_shape = [parts], [_full(parts.shape)], []
    for n in params:
        args += [t[n].reshape(view(n)) for t in (wts, mom_m, mom_v)]
        in_specs += [_full(view(n))] * 3
        out_shape += [jax.ShapeDtypeStruct(view(n), F32)] * 4
    out_shape += [jax.ShapeDtypeStruct((8, 128), F32) for n in names if n not in wts]
    res = pl.pallas_call(
        body, name=name, grid=(1,), out_shape=out_shape, in_specs=in_specs,
        out_specs=[_full(s.shape) for s in out_shape], compiler_params=_params(("arbitrary",)),
    )(*args)
    done = {n: tuple(r.reshape(wts[n].shape) for r in res[4 * i:4 * i + 4]) for i, n in enumerate(params)}
    sums = {n: res[4 * len(params) + j] for j, n in enumerate(n for n in names if n not in wts)}
    return done, sums


def _t5_bucket_map():
    qi = jnp.arange(BLK)[:, None]
    si = jnp.arange(2 * BLK)[None, :]
    n = jnp.maximum(qi + BLK - si, 0)
    max_exact = N_BUCKETS // 2
    nf = jnp.maximum(n, max_exact).astype(F32)
    large = max_exact + (jnp.log(nf / max_exact) / math.log(MAX_DISTANCE / max_exact)
                         * (N_BUCKETS - max_exact)).astype(jnp.int32)
    large = jnp.minimum(large, N_BUCKETS - 1)
    return jnp.where(n < max_exact, n, large).astype(jnp.int32)


def kernel(x, c, rel_bias, w_ada, b_ada, w_in, b_in, attn_sinks, gmlp_ln_g, gmlp_ln_b, gmlp_w_s, gmlp_b_s, attn_out_g, gmlp_out_g, w_out, ln1_g, ln1_b, w_gate_up, w_down, ln2_g, ln2_b, loss_target, m_rel_bias, m_w_ada, m_b_ada, m_w_in, m_b_in, m_attn_sinks, m_gmlp_ln_g, m_gmlp_ln_b, m_gmlp_w_s, m_gmlp_b_s, m_attn_out_g, m_gmlp_out_g, m_w_out, m_ln1_g, m_ln1_b, m_w_gate_up, m_w_down, m_ln2_g, m_ln2_b, v_rel_bias, v_w_ada, v_b_ada, v_w_in, v_b_in, v_attn_sinks, v_gmlp_ln_g, v_gmlp_ln_b, v_gmlp_w_s, v_gmlp_b_s, v_attn_out_g, v_gmlp_out_g, v_w_out, v_ln1_g, v_ln1_b, v_w_gate_up, v_w_down, v_ln2_g, v_ln2_b):
    wts = dict(rel_bias=rel_bias, w_ada=w_ada, b_ada=b_ada, w_in=w_in, b_in=b_in, attn_sinks=attn_sinks,
               gmlp_ln_g=gmlp_ln_g, gmlp_ln_b=gmlp_ln_b, gmlp_w_s=gmlp_w_s, gmlp_b_s=gmlp_b_s,
               attn_out_g=attn_out_g, gmlp_out_g=gmlp_out_g, w_out=w_out, ln1_g=ln1_g, ln1_b=ln1_b,
               w_gate_up=w_gate_up, w_down=w_down, ln2_g=ln2_g, ln2_b=ln2_b)
    mom_m = dict(rel_bias=m_rel_bias, w_ada=m_w_ada, b_ada=m_b_ada, w_in=m_w_in, b_in=m_b_in,
                 attn_sinks=m_attn_sinks, gmlp_ln_g=m_gmlp_ln_g, gmlp_ln_b=m_gmlp_ln_b, gmlp_w_s=m_gmlp_w_s,
                 gmlp_b_s=m_gmlp_b_s, attn_out_g=m_attn_out_g, gmlp_out_g=m_gmlp_out_g, w_out=m_w_out,
                 ln1_g=m_ln1_g, ln1_b=m_ln1_b, w_gate_up=m_w_gate_up, w_down=m_w_down, ln2_g=m_ln2_g,
                 ln2_b=m_ln2_b)
    mom_v = dict(rel_bias=v_rel_bias, w_ada=v_w_ada, b_ada=v_b_ada, w_in=v_w_in, b_in=v_b_in,
                 attn_sinks=v_attn_sinks, gmlp_ln_g=v_gmlp_ln_g, gmlp_ln_b=v_gmlp_ln_b, gmlp_w_s=v_gmlp_w_s,
                 gmlp_b_s=v_gmlp_b_s, attn_out_g=v_attn_out_g, gmlp_out_g=v_gmlp_out_g, w_out=v_w_out,
                 ln1_g=v_ln1_g, ln1_b=v_ln1_b, w_gate_up=v_w_gate_up, w_down=v_w_down, ln2_g=v_ln2_g,
                 ln2_b=v_ln2_b)

    t = x.shape[1]
    tm = min(512, t)
    tn_ff = D_FF // 2
    tk_long, tk_short = min(4096, t), min(2048, t)
    me = 4 * lax.axis_index("x") + 2 * lax.axis_index("y") + lax.axis_index("c")
    xs = x[0]
    target = loss_target[0]

    c_g, w_in_g = _exchange("gather_in", [jnp.broadcast_to(c, (8, D_MODEL)), w_in[0].T.astype(BF16)],
                            ("gather", "gather2"))
    c_all = c_g[:, 0, :]
    w_in_t = w_in_g.reshape(IN_W, D_MODEL)

    ncol = w_ada.shape[2]
    b_cols = lax.dynamic_slice(b_ada, (0, me * ncol), (1, ncol))
    mod_part = _mod_partial(c_all, w_ada[0], b_cols)
    (mod_g,) = _exchange("gather_mod", [mod_part], ("gather",))
    mod = lax.dynamic_slice(mod_g, (0, me, 0), (N_DEV, 1, ncol)).reshape(1, N_DEV * ncol)
    sh1, sc1, g1, sh2, sc2, g2 = [mod[:, i * D_MODEL:(i + 1) * D_MODEL] for i in range(6)]

    bucket = _t5_bucket_map()
    bias = _bias_table(rel_bias, bucket)
    causal = jnp.tril(jnp.ones((BLK, BLK), dtype=bool))
    ws = jnp.where(causal[None], gmlp_w_s[0], 0.0).astype(BF16)
    pair = lambda w: jnp.concatenate([w[0::2], w[1::2]], axis=2)
    ws2, wst2 = pair(ws), pair(jnp.swapaxes(ws, 1, 2))
    bfull = jnp.repeat(gmlp_b_s[0].T, GMLP_W // N_GROUPS, axis=1)
    sinks = attn_sinks[0]

    proj, h1 = _inproj(xs, sc1, sh1, w_in_t, b_in, tm)
    (mixed, *kept), (w_out_g, w_gu_g) = _mix_fwd(
        proj, bias, sinks, gmlp_ln_g, gmlp_ln_b, ws2, bfull, attn_out_g, gmlp_out_g,
        comm=([w_out[0].astype(BF16), w_gate_up[0].T.astype(BF16)], ("gather2", "gather2")))
    w_out_f = w_out_g.reshape(D_MODEL, D_MODEL)
    w_gu_t = w_gu_g.reshape(2 * D_FF, D_MODEL)
    (y1, x1, h2), (w_down_g,) = _outproj(mixed, w_out_f, xs, g1, ln1_g, ln1_b, sc2, sh2, tm,
                                         comm=([w_down[0].astype(BF16)], ("gather2",)))
    w_down_f = w_down_g.reshape(D_FF, D_MODEL)
    dsu, sg, act, dz2, dy2, loss_p, d_ln2g, d_ln2b, d_g2 = _ffn_fwd(h2, w_gu_t, w_down_f, x1, target, g2, ln2_g, ln2_b,
                                                                    min(256, t))

    slots = lambda a: a.reshape(N_DEV, -1, D_MODEL)
    dw_down = _wgrad("wgrad_down", act, dy2, tn_ff, tk_short)
    (dgate, dup, dz1, dy1, d_sc2, d_sh2, d_ln1g, d_ln1b, d_g1), (r_down,) = _ffn_bwd(
        dy2, w_down_f, dsu, sg, w_gu_t, x1, xs, y1, dz2, sc2, g1, ln1_g, min(256, t),
        comm=([slots(dw_down)], ("scatter",)))
    dw_gu_t = _wgrad("wgrad_gate_up", dgate, h2, tn_ff, tk_short, a2=dup)
    dw_out = _wgrad("wgrad_out", mixed, dy1, D_MODEL, tk_long)
    ((dproj, dkvn, dl_acc, dsink_acc, d_lng, d_lnb, d_ws, d_bs, d_aog, d_gog), (r_gu, r_out)) = _mix_bwd(
        proj, gmlp_ln_g, gmlp_ln_b, ws2, wst2, bfull, attn_out_g, gmlp_out_g, dy1, w_out_f, kept,
        comm=([slots(dw_gu_t), slots(dw_out)], ("scatter", "scatter")))
    d_relb = _bias_grad(dl_acc, bucket)

    rsum = lambda a: jnp.sum(a, axis=0)
    early_g = dict(
        rel_bias=d_relb[:, 0, :N_BUCKETS].T, attn_sinks=rsum(dsink_acc)[:N_HEADS],
        gmlp_ln_g=rsum(d_lng), gmlp_ln_b=rsum(d_lnb), gmlp_w_s=jnp.where(causal[None], d_ws, 0.0),
        gmlp_b_s=jnp.sum(d_bs.reshape(BLK, N_GROUPS, GMLP_W // N_GROUPS), axis=2).T,
        attn_out_g=rsum(d_aog), gmlp_out_g=rsum(d_gog), ln1_g=rsum(d_ln1g), ln1_b=rsum(d_ln1b),
        ln2_g=rsum(d_ln2g), ln2_b=rsum(d_ln2b))
    (grad_x, dproj_b, d_bin, d_sc1, d_sh1), _ = _din(dproj, dkvn, w_in_t, xs, dz1, sc1, tm, comm=None)
    dw_in_t, (early_all,) = _wgrad("wgrad_in", dproj_b, h1, IN_W // 2, tk_long,
                                   comm=([_pack(early_g, SMALL_EARLY)], ("gather2",)))
    dmod = jnp.concatenate([rsum(d_sh1), rsum(d_sc1), rsum(d_g1), rsum(d_sh2), rsum(d_sc2), rsum(d_g2)])
    late_g = dict(b_ada=dmod, b_in=rsum(d_bin), loss=(0.5 / D_MODEL * jnp.sum(loss_p)).reshape(1))
    late_all, r_in = _exchange("scatter_in", [_pack(late_g, SMALL_LATE), slots(dw_in_t)], ("gather", "scatter"))

    small, _ = _adam_small("adam_small_early", early_all, SMALL_EARLY, wts, mom_m, mom_v)
    small_late, sums = _adam_small("adam_small_late", late_all, SMALL_LATE, wts, mom_m, mom_v)
    small.update(small_late)
    loss = sums["loss"][0, 0]

    dmod_all = late_all[:, :_seg_rows(6 * D_MODEL), :].reshape(N_DEV, 6 * D_MODEL)
    dmod_cols = lax.dynamic_slice(dmod_all, (0, me * ncol), (N_DEV, ncol))
    kpad = 128 - N_DEV
    ada = _adam_w_ada(jnp.pad(c_all.T, ((0, 0), (0, kpad))), jnp.pad(dmod_cols, ((0, kpad), (0, 0))),
                      w_ada[0], m_w_ada[0], v_w_ada[0])

    tr = lambda a: jnp.swapaxes(a, -1, -2)
    big = {}
    big["w_in"] = [tr(o)[None] for o in _adam_reduce("adam_w_in", r_in, w_in[0].T, m_w_in[0].T, v_w_in[0].T, 112)]
    big["w_out"] = [o[None] for o in _adam_reduce("adam_w_out", r_out, w_out[0], m_w_out[0], v_w_out[0], 128)]
    big["w_gate_up"] = [tr(o)[None] for o in _adam_reduce("adam_w_gu", r_gu, w_gate_up[0].T, m_w_gate_up[0].T,
                                                           v_w_gate_up[0].T, 352)]
    big["w_down"] = [o[None] for o in _adam_reduce("adam_w_down", r_down, w_down[0], m_w_down[0], v_w_down[0], 176)]
    big["w_ada"] = [o[None] for o in ada]

    outs = [[], [], [], []]
    for name in WEIGHTS:
        for i in range(4):
            outs[i].append(big[name][i] if name in big else small[name][i])
    return (loss, grad_x[None], *outs[0], *outs[1], *outs[2], *outs[3])
```

```python
import math

import jax
import jax.numpy as jnp
from jax import lax
from jax.experimental import pallas as pl
from jax.experimental.pallas import tpu as pltpu

F32 = jnp.float32
BF16 = jnp.bfloat16
MESH = pl.DeviceIdType.MESH

N_DEV = 8
D_MODEL = 1024
HEAD_DIM = 64
N_HEADS = 8
N_GROUPS = 8
ATTN_W = 512
KV_W = 128
GMLP_W = 512
IN_W = 1792
BLK = 128
N_BUCKETS = 32
MAX_DISTANCE = 128
D_FF = 2816
ALPHA = 2.0 ** 0.25
LN_EPS = 1e-5
NEG_INF = -1e30
ADAM_LR = 0.001
ADAM_B1 = 0.9
ADAM_B2 = 0.999
ADAM_EPS = 1e-08
ADAM_WD = 0.01
ADAM_STEP = 10
GELU_C0 = math.sqrt(2.0 / math.pi)
GELU_C1 = 0.044715

VMEM_LIMIT = 56 * 1024 * 1024


def _params(sem):
    return pltpu.CompilerParams(dimension_semantics=sem, vmem_limit_bytes=VMEM_LIMIT)


def _dot(a, b):
    return lax.dot_general(a, b, (((1,), (0,)), ((), ())), preferred_element_type=F32)


def _dot_nt(a, b):
    return lax.dot_general(a, b, (((1,), (1,)), ((), ())), preferred_element_type=F32)


def _dot_tn(a, b):
    return lax.dot_general(a, b, (((0,), (0,)), ((), ())), preferred_element_type=F32)


def _full(shape):
    nd = len(shape)
    return pl.BlockSpec(shape, lambda *_: (0,) * nd)


def _rowsum8(v):
    r, c = v.shape
    return jnp.sum(v.reshape(r // 8, 8, c), axis=0)


def _sigmoid(v):
    return 1.0 / (1.0 + jnp.exp(-v))


def _gelu_parts(v):
    v2 = v * v
    t = jnp.tanh(GELU_C0 * (v + GELU_C1 * v * v2))
    g = 0.5 * v * (1.0 + t)
    dg = 0.5 * (1.0 + t) + 0.5 * v * (1.0 - t * t) * (GELU_C0 * (1.0 + 3.0 * GELU_C1 * v2))
    return g, dg


def _ln_stats(z):
    mu = jnp.mean(z, axis=1, keepdims=True)
    zc = z - mu
    var = jnp.mean(zc * zc, axis=1, keepdims=True)
    rstd = lax.rsqrt(var + LN_EPS)
    return zc * rstd, rstd


def _ln_bwd(dxhat, xhat, rstd):
    m1 = jnp.mean(dxhat, axis=1, keepdims=True)
    m2 = jnp.mean(dxhat * xhat, axis=1, keepdims=True)
    return rstd * (dxhat - m1 - xhat * m2)


def _seg_mean64(v):
    r = v.shape[0]
    lo = lax.broadcasted_iota(jnp.int32, (r, 128), 1) < 64
    outs = []
    for j in range(v.shape[1] // 128):
        ch = v[:, 128 * j:128 * (j + 1)]
        s_lo = jnp.sum(jnp.where(lo, ch, 0.0), axis=1, keepdims=True)
        s_hi = jnp.sum(jnp.where(lo, 0.0, ch), axis=1, keepdims=True)
        outs.append(jnp.where(lo, s_lo, s_hi) * (1.0 / 64.0))
    return jnp.concatenate(outs, axis=1)


def _rms(a, g):
    r = lax.rsqrt(jnp.mean(a * a, axis=1, keepdims=True) + LN_EPS)
    return a * r * g, r


def _rms_bwd(dout, a, r, g):
    t = dout * g
    return r * t - a * (r * r * r) * jnp.mean(t * a, axis=1, keepdims=True)


PEER_ORDER = (1, 2, 4, 3, 5, 6, 7)


def _peer(j):
    x, y, c = lax.axis_index("x"), lax.axis_index("y"), lax.axis_index("c")
    px = 1 - x if j & 4 else x
    py = 1 - y if j & 2 else y
    pc = 1 - c if j & 1 else c
    return (px, py, pc), 4 * px + 2 * py + pc


SIBLING = 1
CHIP_FLIPS = (4, 2, 6)


def _exchange_phase(phase, ins, outs, modes, send_sems, recv_sems, loc_sems):
    me = 4 * lax.axis_index("x") + 2 * lax.axis_index("y") + lax.axis_index("c")
    for k, mode in enumerate(modes):
        def copy(i, src, slot, dev, k=k):
            return pltpu.make_async_remote_copy(src_ref=src, dst_ref=outs[k].at[slot], send_sem=send_sems.at[k, i],
                                                recv_sem=recv_sems.at[k, i], device_id=dev, device_id_type=MESH)

        src_me = ins[k].at[me] if mode == "scatter" else ins[k]
        local = pltpu.make_async_copy(src_me, outs[k].at[me], loc_sems.at[k])
        if mode == "gather2":
            sib_dev, sib_idx = _peer(SIBLING)
            chips = [_peer(j) for j in CHIP_FLIPS]
            far = [_peer(j | SIBLING)[1] for j in CHIP_FLIPS]
            if phase == "start":
                local.start()
                copy(0, ins[k], me, sib_dev).start()
                for i, (dev, _) in enumerate(chips):
                    copy(1 + i, ins[k], me, dev).start()
            elif phase == "mid":
                for i, (dev, idx) in enumerate(chips):
                    copy(1 + i, ins[k], idx, dev).wait_recv()
                    copy(4 + i, outs[k].at[idx], idx, sib_dev).start()
            else:
                copy(0, ins[k], sib_idx, sib_dev).wait_recv()
                for i, slot in enumerate(far):
                    copy(4 + i, ins[k], slot, sib_dev).wait_recv()
                copy(0, ins[k], me, sib_dev).wait_send()
                for i, (dev, idx) in enumerate(chips):
                    copy(1 + i, ins[k], me, dev).wait_send()
                    copy(4 + i, outs[k].at[idx], idx, sib_dev).wait_send()
                local.wait()
            continue
        peers = [_peer(j) for j in PEER_ORDER]
        if phase == "start":
            local.start()
            for i, (dev, idx) in enumerate(peers):
                copy(i, ins[k].at[idx] if mode == "scatter" else ins[k], me, dev).start()
        elif phase == "end":
            for i, (dev, idx) in enumerate(peers):
                copy(i, src_me, idx, dev).wait_recv()
            for i, (dev, idx) in enumerate(peers):
                copy(i, src_me, me, dev).wait_send()
            local.wait()


def _exchange_shapes(arrays, modes):
    return [jax.ShapeDtypeStruct((N_DEV,) + (a.shape[1:] if m == "scatter" else a.shape), a.dtype)
            for a, m in zip(arrays, modes)]


def _exchange_sems(n):
    return [pltpu.SemaphoreType.DMA((n, N_DEV - 1)), pltpu.SemaphoreType.DMA((n, N_DEV - 1)),
            pltpu.SemaphoreType.DMA((n,))]


def _exchange(name, arrays, modes):
    n = len(arrays)

    def body(*refs):
        for phase in ("start", "mid", "end"):
            _exchange_phase(phase, refs[:n], refs[n:2 * n], modes, *refs[2 * n:])

    any_spec = pl.BlockSpec(memory_space=pl.ANY)
    return pl.pallas_call(
        body, name=name, out_shape=_exchange_shapes(arrays, modes),
        in_specs=[any_spec] * n, out_specs=[any_spec] * n, scratch_shapes=_exchange_sems(n),
    )(*arrays)


def _call(body, *, name, grid, in_specs, out_specs, out_shape, args, sem, scratch_shapes=(), comm=None):
    if comm is None:
        outs = pl.pallas_call(body, name=name, grid=grid, in_specs=list(in_specs), out_specs=list(out_specs),
                              out_shape=list(out_shape), scratch_shapes=list(scratch_shapes),
                              compiler_params=_params(sem))(*args)
        return list(outs), []
    arrays, modes = comm
    n_in, n_out, nc, ns = len(in_specs), len(out_specs), len(arrays), len(scratch_shapes)
    n_steps = math.prod(grid)

    def hosted(*refs):
        ins, cins = refs[:n_in], refs[n_in:n_in + nc]
        outs, couts = refs[n_in + nc:n_in + nc + n_out], refs[n_in + nc + n_out:n_in + 2 * nc + n_out]
        scratch = refs[n_in + 2 * nc + n_out:]
        ex = (cins, couts, modes) + tuple(scratch[ns:])
        step = pl.program_id(0)
        for ax in range(1, len(grid)):
            step = step * grid[ax] + pl.program_id(ax)

        @pl.when(step == 0)
        def _():
            _exchange_phase("start", *ex)

        body(*ins, *outs, *scratch[:ns])

        if "gather2" in modes:
            @pl.when(step == (3 * n_steps) // 4)
            def _():
                _exchange_phase("mid", *ex)

        @pl.when(step == n_steps - 1)
        def _():
            _exchange_phase("end", *ex)

    any_spec = pl.BlockSpec(memory_space=pl.ANY)
    res = pl.pallas_call(
        hosted, name=name, grid=grid, in_specs=list(in_specs) + [any_spec] * nc,
        out_specs=list(out_specs) + [any_spec] * nc, out_shape=list(out_shape) + _exchange_shapes(arrays, modes),
        scratch_shapes=list(scratch_shapes) + _exchange_sems(nc),
        compiler_params=_params(tuple("arbitrary" for _ in grid)))(*args, *arrays)
    return list(res[:n_out]), list(res[n_out:])


def _mod_partial(c_all, w_ada, b_ada_cols):
    def body(c_ref, w_ref, b_ref, o_ref):
        cv = c_ref[...]
        s = (cv * _sigmoid(cv)).astype(BF16)
        o_ref[...] = _dot(s, w_ref[...].astype(BF16)) + b_ref[...]

    ncol = w_ada.shape[1]
    return pl.pallas_call(
        body, name="mod_partial", out_shape=jax.ShapeDtypeStruct((N_DEV, ncol), F32),
        in_specs=[_full(c_all.shape), _full(w_ada.shape), _full(b_ada_cols.shape)],
        out_specs=_full((N_DEV, ncol)), grid=(1,), compiler_params=_params(("arbitrary",)),
    )(c_all, w_ada, b_ada_cols)


def _bias_table(rel_bias, bucket, comm):
    def body(rb_ref, bk_ref, o_ref):
        h = pl.program_id(0)
        bk = bk_ref[...]
        acc = jnp.zeros((BLK, 2 * BLK), F32)
        for b in range(N_BUCKETS):
            acc = jnp.where(bk == b, rb_ref[b, h], acc)
        dist = (lax.broadcasted_iota(jnp.int32, (BLK, 2 * BLK), 0) + BLK
                - lax.broadcasted_iota(jnp.int32, (BLK, 2 * BLK), 1))
        o_ref[0] = jnp.where((dist >= 0) & (dist < BLK), acc, NEG_INF)

    return _call(
        body, name="bias_table", out_shape=[jax.ShapeDtypeStruct((N_HEADS, BLK, 2 * BLK), F32)],
        in_specs=[pl.BlockSpec(memory_space=pltpu.SMEM), _full((BLK, 2 * BLK))],
        out_specs=[pl.BlockSpec((1, BLK, 2 * BLK), lambda h: (h, 0, 0))], grid=(N_HEADS,),
        sem=("arbitrary",), comm=comm, args=(rel_bias, bucket))


def _bias_grad(dl_acc, bucket):
    def body(dl_ref, bk_ref, o_ref):
        bk = bk_ref[...]
        dl = dl_ref[0]
        lane = lax.broadcasted_iota(jnp.int32, (1, 128), 1)
        row = jnp.zeros((1, 128), F32)
        for b in range(N_BUCKETS):
            s = jnp.sum(jnp.sum(jnp.where(bk == b, dl, 0.0), axis=1, keepdims=True), axis=0, keepdims=True)
            row = jnp.where(lane == b, s, row)
        o_ref[0] = row

    return pl.pallas_call(
        body, name="bias_grad", out_shape=jax.ShapeDtypeStruct((N_HEADS, 1, 128), F32),
        in_specs=[pl.BlockSpec((1, BLK, 2 * BLK), lambda h: (h, 0, 0)), _full((BLK, 2 * BLK))],
        out_specs=pl.BlockSpec((1, 1, 128), lambda h: (h, 0, 0)), grid=(N_HEADS,),
        compiler_params=_params(("arbitrary",)),
    )(dl_acc, bucket)


def _inproj(x, sc1, sh1, w_in_t, b_in, tm):
    t, d = x.shape
    n = w_in_t.shape[0]

    def body(x_ref, sc_ref, sh_ref, w_ref, b_ref, proj_ref, h_ref):
        h = (x_ref[...] * (1.0 + sc_ref[...]) + sh_ref[...]).astype(BF16)
        h_ref[...] = h
        proj_ref[...] = _dot_nt(h, w_ref[...]) + b_ref[...]

    row = lambda w: pl.BlockSpec((tm, w), lambda i: (i, 0))
    return pl.pallas_call(
        body, name="inproj", grid=(t // tm,),
        out_shape=[jax.ShapeDtypeStruct((t, n), F32), jax.ShapeDtypeStruct((t, d), BF16)],
        in_specs=[row(d), _full((1, d)), _full((1, d)), _full((n, d)), _full((1, n))],
        out_specs=[row(n), row(d)], compiler_params=_params(("parallel",)),
    )(x, sc1, sh1, w_in_t, b_in)


HALF = 64
ROWS = 32


def _lane_lo(rows):
    return lax.broadcasted_iota(jnp.int32, (rows, 128), 1) < 64


def _mix_stage_kv(proj_ref, kvp_ref, s):
    lo = _lane_lo(2 * BLK)
    for name, col in (("k", ATTN_W), ("v", ATTN_W + KV_W)):
        cur = jnp.concatenate([kvp_ref[:, col - ATTN_W:col - ATTN_W + KV_W], proj_ref[:, col:col + KV_W]], axis=0)
        plain, swapped = cur.astype(BF16), pltpu.roll(cur, 64, 1).astype(BF16)
        zero = jnp.zeros_like(plain)
        for g in range(2):
            dup = jnp.where(lo, plain, swapped) if g == 0 else jnp.where(lo, swapped, plain)
            s[name + "d"][g] = dup
            s[name + "m"][g] = jnp.concatenate([jnp.where(lo, dup, zero), jnp.where(lo, zero, dup)], axis=0)


def _group_rows(ref, g):
    return ref[4 * g:4 * g + 4].reshape(4 * BLK, ref.shape[2])


def _pair_rows(ref, g):
    return jnp.concatenate([jnp.concatenate([ref[4 * g + 2 * c], ref[4 * g + 2 * c + 1]], axis=1) for c in range(2)],
                           axis=0)


def _mask_heads(src_ref, dst_ref):
    lo = _lane_lo(BLK)
    for j in range(4):
        chunk = src_ref[:, 128 * j:128 * (j + 1)]
        dst_ref[2 * j] = jnp.where(lo, chunk, 0.0).astype(BF16)
        dst_ref[2 * j + 1] = jnp.where(lo, 0.0, chunk).astype(BF16)


def _mix_stage_attn(proj_ref, bias_ref, sinks_ref, n, s):
    _mask_heads(proj_ref, s["qm"])
    for g in range(2):
        s["lg"][g] = _dot_nt(_group_rows(s["qm"], g), s["kd"][g])
    n0mask = (n == 0) & (lax.broadcasted_iota(jnp.int32, (HALF, 2 * BLK), 1) < BLK)
    lane = lax.broadcasted_iota(jnp.int32, (HALF, 128), 1)
    for hf in range(BLK // HALF):
        rows = slice(HALF * hf, HALF * (hf + 1))
        psink = jnp.zeros((HALF, 128), F32)
        for h in range(N_HEADS):
            sk = sinks_ref[h]
            grows = slice(BLK * (h % 4) + HALF * hf, BLK * (h % 4) + HALF * (hf + 1))
            logit = s["lg"][h // 4, grows, :] * (HEAD_DIM ** -0.5) + bias_ref[h, rows, :]
            logit = jnp.where(n0mask, NEG_INF, logit)
            m = jnp.maximum(jnp.max(logit, axis=1, keepdims=True), sk)
            e = jnp.exp(logit - m)
            es = jnp.exp(sk - m)
            inv = 1.0 / (jnp.sum(e, axis=1, keepdims=True) + es)
            p = e * inv
            s["p"][h, rows, :] = p
            s["pb"][h, rows, :] = p.astype(BF16)
            psink = jnp.where(lane == h, es * inv, psink)
        s["psink"][rows, :] = psink
    for g in range(2):
        out = _dot(_pair_rows(s["pb"], g), s["vm"][g])
        s["attn"][:, 256 * g:256 * g + 128] = out[0:BLK]
        s["attn"][:, 256 * g + 128:256 * g + 256] = out[BLK:2 * BLK]


def _mix_stage_gmlp_pre(proj_ref, lng, lnb, s, keep):
    c0 = ATTN_W + 2 * KV_W
    for r0 in range(0, BLK, ROWS):
        rows = slice(r0, r0 + ROWS)
        u, du = _gelu_parts(proj_ref[rows, c0:c0 + GMLP_W])
        a, da = _gelu_parts(proj_ref[rows, c0 + GMLP_W:c0 + 2 * GMLP_W])
        ac = a - _seg_mean64(a)
        rstd = lax.rsqrt(_seg_mean64(ac * ac) + LN_EPS)
        vhat = ac * rstd
        s["u"][rows, :] = u
        s["vnb"][rows, :] = (vhat * lng + lnb).astype(BF16)
        if keep:
            s["du"][rows, :] = du
            s["da"][rows, :] = da
            s["vhat"][rows, :] = vhat
            s["rstd"][rows, :] = rstd


def _stack_halves(chunk):
    lo = _lane_lo(BLK)
    zero = jnp.zeros_like(chunk)
    return jnp.concatenate([jnp.where(lo, chunk, zero), jnp.where(lo, zero, chunk)], axis=0)


def _mix_stage_gmlp_mix(ws2_ref, bfull_ref, s):
    for j in range(4):
        cols = slice(128 * j, 128 * (j + 1))
        s["ms"][:, cols] = _dot(ws2_ref[j], _stack_halves(s["vnb"][:, cols])) + bfull_ref[:, cols]


def _mix_scratch(keep):
    f32 = lambda *shape: pltpu.VMEM(shape, F32)
    b16 = lambda *shape: pltpu.VMEM(shape, BF16)
    names = dict(kd=b16(2, 2 * BLK, 128), vd=b16(2, 2 * BLK, 128), km=b16(2, 4 * BLK, 128), vm=b16(2, 4 * BLK, 128),
                 qm=b16(N_HEADS, BLK, 128), lg=f32(2, 4 * BLK, 2 * BLK), pb=b16(N_HEADS, BLK, 2 * BLK),
                 u=f32(BLK, GMLP_W), vnb=b16(BLK, GMLP_W), ms=f32(BLK, GMLP_W))
    if keep:
        names.update(dom=b16(N_HEADS, BLK, 128), dls=b16(N_HEADS, BLK, 2 * BLK),
                     dattn=f32(BLK, ATTN_W), dmix=f32(BLK, D_MODEL), du=f32(BLK, GMLP_W), da=f32(BLK, GMLP_W),
                     vhat=f32(BLK, GMLP_W), rstd=f32(BLK, GMLP_W), dmsb=b16(BLK, GMLP_W), dvn=f32(BLK, GMLP_W))
    return list(names), list(names.values())


def _mix_specs(with_logit_inputs):
    logit_inputs = [_full((N_HEADS, BLK, 2 * BLK)), pl.BlockSpec(memory_space=pltpu.SMEM)] if with_logit_inputs else []
    return [pl.BlockSpec((BLK, IN_W), lambda n: (n, 0)),
            pl.BlockSpec((BLK, 2 * KV_W), lambda n: (jnp.maximum(n - 1, 0), ATTN_W // (2 * KV_W)))] + logit_inputs + [
            _full((1, GMLP_W)), _full((1, GMLP_W)),
            _full((N_GROUPS // 2, BLK, 2 * BLK)), _full((BLK, GMLP_W)),
            _full((1, ATTN_W)), _full((1, GMLP_W))]


KEPT = [("p", (N_HEADS, BLK, 2 * BLK), F32), ("psink", (BLK, 128), F32), ("attn", (BLK, ATTN_W), F32)]


def _kept_shapes(t):
    full = lambda blk: (blk[0], t, blk[2]) if len(blk) == 3 else (t, blk[1])
    return [jax.ShapeDtypeStruct(full(blk), dt) for _, blk, dt in KEPT]


def _kept_specs():
    return [pl.BlockSpec(blk, (lambda n: (0, n, 0)) if len(blk) == 3 else (lambda n: (n, 0))) for _, blk, _ in KEPT]


def _mix_fwd(proj, bias, sinks, lng, lnb, ws2, bfull, aog, gog, comm):
    t = proj.shape[0]
    names, shapes = _mix_scratch(False)

    def body(proj_ref, kvp_ref, bias_ref, sinks_ref, lng_ref, lnb_ref, ws2_ref, bfull_ref, aog_ref, gog_ref,
             out_ref, *rest):
        s = dict(zip([name for name, _, _ in KEPT] + names, rest))
        n = pl.program_id(0)
        _mix_stage_kv(proj_ref, kvp_ref, s)
        _mix_stage_attn(proj_ref, bias_ref, sinks_ref, n, s)
        _mix_stage_gmlp_pre(proj_ref, lng_ref[...], lnb_ref[...], s, False)
        _mix_stage_gmlp_mix(ws2_ref, bfull_ref, s)
        for r0 in range(0, BLK, ROWS):
            rows = slice(r0, r0 + ROWS)
            out_ref[rows, 0:ATTN_W] = _rms(s["attn"][rows, :], aog_ref[...])[0].astype(BF16)
            out_ref[rows, ATTN_W:ATTN_W + GMLP_W] = _rms(s["u"][rows, :] * s["ms"][rows, :], gog_ref[...])[0].astype(BF16)

    return _call(
        body, name="mix_fwd", grid=(t // BLK,),
        out_shape=[jax.ShapeDtypeStruct((t, D_MODEL), BF16)] + _kept_shapes(t),
        in_specs=_mix_specs(True), out_specs=[pl.BlockSpec((BLK, D_MODEL), lambda n: (n, 0))] + _kept_specs(),
        scratch_shapes=shapes,
        sem=("parallel",), comm=comm, args=(proj, proj, bias, sinks, lng, lnb, ws2, bfull, aog, gog))


def _mix_bwd(proj, lng, lnb, ws2, wst2, bfull, aog, gog, dy, w_out, kept, comm):
    t = proj.shape[0]
    nb = t // BLK
    names, shapes = _mix_scratch(True)
    c_gu = ATTN_W + 2 * KV_W

    def body(proj_ref, kvp_ref, lng_ref, lnb_ref, ws2_ref, bfull_ref, aog_ref, gog_ref,
             wst2_ref, dy_ref, wout_ref, *rest):
        n_kept = len(KEPT)
        s = dict(zip([name for name, _, _ in KEPT], rest[:n_kept]))
        (dproj_ref, dkvn_ref, dl_ref, dsink_ref, dlng_ref, dlnb_ref, dws_ref, dbs_ref, daog_ref,
         dgog_ref) = rest[n_kept:n_kept + 10]
        s.update(zip(names, rest[n_kept + 10:]))
        n = pl.program_id(0)

        @pl.when(n == 0)
        def _():
            for r in (dl_ref, dsink_ref, dlng_ref, dlnb_ref, dws_ref, dbs_ref, daog_ref, dgog_ref):
                r[...] = jnp.zeros_like(r)

        s["dmix"][...] = _dot(dy_ref[...], wout_ref[...])
        _mix_stage_kv(proj_ref, kvp_ref, s)
        _mask_heads(proj_ref, s["qm"])
        lng = lng_ref[...]
        _mix_stage_gmlp_pre(proj_ref, lng, lnb_ref[...], s, True)
        _mix_stage_gmlp_mix(ws2_ref, bfull_ref, s)

        aog, gog = aog_ref[...], gog_ref[...]
        for r0 in range(0, BLK, ROWS):
            rows = slice(r0, r0 + ROWS)
            attn, dma = s["attn"][rows, :], s["dmix"][rows, 0:ATTN_W]
            _, r_a = _rms(attn, aog)
            daog_ref[...] += _rowsum8(dma * attn * r_a)
            s["dattn"][rows, :] = _rms_bwd(dma, attn, r_a, aog)
            u, ms, dmg = s["u"][rows, :], s["ms"][rows, :], s["dmix"][rows, ATTN_W:ATTN_W + GMLP_W]
            gm = u * ms
            _, r_g = _rms(gm, gog)
            dgog_ref[...] += _rowsum8(dmg * gm * r_g)
            dgm = _rms_bwd(dmg, gm, r_g, gog)
            dproj_ref[rows, c_gu:c_gu + GMLP_W] = dgm * ms * s["du"][rows, :]
            dms = dgm * u
            dbs_ref[rows, :] += dms
            s["dmsb"][rows, :] = dms.astype(BF16)

        _mask_heads(s["dattn"], s["dom"])
        for g in range(2):
            s["lg"][g] = _dot_nt(_group_rows(s["dom"], g), s["vd"][g])
        lane = lax.broadcasted_iota(jnp.int32, (HALF, 128), 1)
        for hf in range(BLK // HALF):
            rows = slice(HALF * hf, HALF * (hf + 1))
            dsink = jnp.zeros((HALF, 128), F32)
            for h in range(N_HEADS):
                grows = slice(BLK * (h % 4) + HALF * hf, BLK * (h % 4) + HALF * (hf + 1))
                dp = s["lg"][h // 4, grows, :]
                p = s["p"][h, rows, :]
                s["pb"][h, rows, :] = p.astype(BF16)
                rs = jnp.sum(p * dp, axis=1, keepdims=True)
                dl = p * (dp - rs)
                dl_ref[h, rows, :] += dl
                dsink = dsink + jnp.where(lane == h, -s["psink"][rows, :] * rs, 0.0)
                s["dls"][h, rows, :] = (dl * (HEAD_DIM ** -0.5)).astype(BF16)
            dsink_ref[rows, :] += dsink
        for g in range(2):
            dq = _dot(_pair_rows(s["dls"], g), s["km"][g])
            dproj_ref[:, 256 * g:256 * g + 128] = dq[0:BLK]
            dproj_ref[:, 256 * g + 128:256 * g + 256] = dq[BLK:2 * BLK]
        lo_k = _lane_lo(2 * BLK)
        for col, lhs, rhs in ((0, "dls", "qm"), (KV_W, "pb", "dom")):
            raw = [_dot_tn(_group_rows(s[lhs], g), _group_rows(s[rhs], g)) for g in range(2)]
            both = [r + pltpu.roll(r, 64, 1) for r in raw]
            dkv = jnp.where(lo_k, both[0], both[1])
            dproj_ref[:, ATTN_W + col:ATTN_W + col + KV_W] = dkv[BLK:2 * BLK]
            dkvn_ref[:, col:col + KV_W] = dkv[0:BLK]

        for j in range(4):
            cols = slice(128 * j, 128 * (j + 1))
            dm2 = _stack_halves(s["dmsb"][:, cols])
            vnb = s["vnb"][:, cols]
            dws2 = _dot_nt(dm2, vnb)
            dws_ref[2 * j] += dws2[0:BLK]
            dws_ref[2 * j + 1] += dws2[BLK:2 * BLK]
            s["dvn"][:, cols] = _dot(wst2_ref[j], dm2)
        for r0 in range(0, BLK, ROWS):
            rows = slice(r0, r0 + ROWS)
            dvn, vhat = s["dvn"][rows, :], s["vhat"][rows, :]
            dlng_ref[...] += _rowsum8(dvn * vhat)
            dlnb_ref[...] += _rowsum8(dvn)
            dvh = dvn * lng
            dact = s["rstd"][rows, :] * (dvh - _seg_mean64(dvh) - vhat * _seg_mean64(dvh * vhat))
            dproj_ref[rows, c_gu + GMLP_W:IN_W] = dact * s["da"][rows, :]

    acc8 = lambda w: jax.ShapeDtypeStruct((8, w), F32)
    out_shape = [jax.ShapeDtypeStruct((t, IN_W), F32), jax.ShapeDtypeStruct((t, 2 * KV_W), F32),
                 jax.ShapeDtypeStruct((N_HEADS, BLK, 2 * BLK), F32), jax.ShapeDtypeStruct((BLK, 128), F32),
                 acc8(GMLP_W), acc8(GMLP_W), jax.ShapeDtypeStruct((N_GROUPS, BLK, BLK), F32),
                 jax.ShapeDtypeStruct((BLK, GMLP_W), F32), acc8(ATTN_W), acc8(GMLP_W)]
    out_specs = [pl.BlockSpec((BLK, IN_W), lambda n: (n, 0)),
                 pl.BlockSpec((BLK, 2 * KV_W), lambda n: ((n + nb - 1) % nb, 0)),
                 _full((N_HEADS, BLK, 2 * BLK)), _full((BLK, 128)), _full((8, GMLP_W)), _full((8, GMLP_W)),
                 _full((N_GROUPS, BLK, BLK)), _full((BLK, GMLP_W)), _full((8, ATTN_W)), _full((8, GMLP_W))]
    in_specs = _mix_specs(False) + [_full((N_GROUPS // 2, BLK, 2 * BLK)),
                               pl.BlockSpec((BLK, D_MODEL), lambda n: (n, 0)),
                               _full((D_MODEL, D_MODEL))] + _kept_specs()
    return _call(
        body, name="mix_bwd", grid=(nb,), out_shape=out_shape, in_specs=in_specs, out_specs=out_specs,
        scratch_shapes=shapes, sem=("arbitrary",), comm=comm,
        args=(proj, proj, lng, lnb, ws2, bfull, aog, gog, wst2, dy, w_out, *kept))


def _outproj(mixed, w_out, x, g1, ln1g, ln1b, sc2, sh2, tm, comm):
    t, d = x.shape

    def body(mx_ref, w_ref, x_ref, g1_ref, lg_ref, lb_ref, sc_ref, sh_ref, y_ref, x1_ref, h2_ref):
        y = _dot(mx_ref[...], w_ref[...])
        xhat, _ = _ln_stats(ALPHA * x_ref[...] + g1_ref[...] * y)
        x1 = xhat * lg_ref[...] + lb_ref[...]
        y_ref[...] = y
        x1_ref[...] = x1
        h2_ref[...] = (x1 * (1.0 + sc_ref[...]) + sh_ref[...]).astype(BF16)

    row = pl.BlockSpec((tm, d), lambda i: (i, 0))
    vec = _full((1, d))
    return _call(
        body, name="outproj", grid=(t // tm,),
        out_shape=[jax.ShapeDtypeStruct((t, d), F32), jax.ShapeDtypeStruct((t, d), F32),
                   jax.ShapeDtypeStruct((t, d), BF16)],
        in_specs=[row, _full((d, d)), row, vec, vec, vec, vec, vec], out_specs=[row, row, row],
        sem=("parallel",), comm=comm, args=(mixed, w_out, x, g1, ln1g, ln1b, sc2, sh2))


def _ffn_fwd(h2, w_gu_t, w_down, x1, target, g2, ln2g, ln2b, tm):
    t, d = x1.shape

    def body(h_ref, w_ref, wd_ref, x1_ref, tg_ref, g2_ref, lg_ref, lb_ref,
             dsu_ref, sg_ref, act_ref, dz_ref, dy_ref, loss_ref, dlg_ref, dlb_ref, dg2_ref):
        @pl.when(pl.program_id(0) == 0)
        def _():
            for r in (loss_ref, dlg_ref, dlb_ref, dg2_ref):
                r[...] = jnp.zeros_like(r)

        h = h_ref[...]
        g = _dot_nt(h, w_ref[0:D_FF])
        u = _dot_nt(h, w_ref[D_FF:2 * D_FF])
        s = _sigmoid(g)
        sg = g * s
        act = (sg * u).astype(BF16)
        dsu_ref[...] = (u * (s * (1.0 + g * (1.0 - s)))).astype(BF16)
        sg_ref[...] = sg.astype(BF16)
        act_ref[...] = act
        y2 = _dot(act, wd_ref[...])
        g2 = g2_ref[...]
        lg = lg_ref[...]
        xhat, rstd = _ln_stats(ALPHA * x1_ref[...] + g2 * y2)
        err = xhat * lg + lb_ref[...] - tg_ref[...]
        loss_ref[...] += _rowsum8(err * err)
        dx2 = err * (1.0 / d)
        dlg_ref[...] += _rowsum8(dx2 * xhat)
        dlb_ref[...] += _rowsum8(dx2)
        dz = _ln_bwd(dx2 * lg, xhat, rstd)
        dg2_ref[...] += _rowsum8(dz * y2)
        dz_ref[...] = dz
        dy_ref[...] = (g2 * dz).astype(BF16)

    row = pl.BlockSpec((tm, d), lambda i: (i, 0))
    wide = pl.BlockSpec((tm, D_FF), lambda i: (i, 0))
    vec = _full((1, d))
    acc = _full((8, d))
    acc_shape = jax.ShapeDtypeStruct((8, d), F32)
    wide_shape = jax.ShapeDtypeStruct((t, D_FF), BF16)
    return pl.pallas_call(
        body, name="ffn_fwd", grid=(t // tm,),
        out_shape=[wide_shape] * 3 + [jax.ShapeDtypeStruct((t, d), F32), jax.ShapeDtypeStruct((t, d), BF16)]
        + [acc_shape] * 4,
        in_specs=[row, _resident((2 * D_FF, d)), _resident((D_FF, d)), row, row, vec, vec, vec],
        out_specs=[wide] * 3 + [row, row, acc, acc, acc, acc], compiler_params=_params(("arbitrary",)),
    )(h2, w_gu_t, w_down, x1, target, g2, ln2g, ln2b)


def _resident(shape):
    nd = len(shape)
    return pl.BlockSpec(shape, lambda *_: (0,) * nd, pipeline_mode=pl.Buffered(1))


def _ffn_bwd(dy2, w_down, dsu, sg, w_gu_t, x1, x, y, dz2, sc2, g1, ln1g, tm, comm):
    t, d = x1.shape

    def body(dy2_ref, wd_ref, dsu_ref, sg_ref, w_ref, x1_ref, x_ref, y_ref, dz2_ref, sc_ref, g1_ref, lg_ref,
             dg_ref, du_ref, dz1_ref, dy_ref, dsc_ref, dsh_ref, dlg_ref, dlb_ref, dg1_ref):
        @pl.when(pl.program_id(0) == 0)
        def _():
            for r in (dsc_ref, dsh_ref, dlg_ref, dlb_ref, dg1_ref):
                r[...] = jnp.zeros_like(r)

        dact = _dot_nt(dy2_ref[...], wd_ref[...])
        dg = (dact * dsu_ref[...].astype(F32)).astype(BF16)
        du = (dact * sg_ref[...].astype(F32)).astype(BF16)
        dg_ref[...] = dg
        du_ref[...] = du
        dh2 = _dot(dg, w_ref[0:D_FF]) + _dot(du, w_ref[D_FF:2 * D_FF])
        x1 = x1_ref[...]
        y = y_ref[...]
        g1 = g1_ref[...]
        dsc_ref[...] += _rowsum8(dh2 * x1)
        dsh_ref[...] += _rowsum8(dh2)
        dx1 = dh2 * (1.0 + sc_ref[...]) + ALPHA * dz2_ref[...]
        xhat, rstd = _ln_stats(ALPHA * x_ref[...] + g1 * y)
        dlg_ref[...] += _rowsum8(dx1 * xhat)
        dlb_ref[...] += _rowsum8(dx1)
        dz1 = _ln_bwd(dx1 * lg_ref[...], xhat, rstd)
        dg1_ref[...] += _rowsum8(dz1 * y)
        dz1_ref[...] = dz1
        dy_ref[...] = (g1 * dz1).astype(BF16)

    row = pl.BlockSpec((tm, d), lambda i: (i, 0))
    wide = pl.BlockSpec((tm, D_FF), lambda i: (i, 0))
    vec = _full((1, d))
    acc = _full((8, d))
    acc_shape = jax.ShapeDtypeStruct((8, d), F32)
    wide_shape = jax.ShapeDtypeStruct((t, D_FF), BF16)
    return _call(
        body, name="ffn_bwd", grid=(t // tm,),
        out_shape=[wide_shape, wide_shape, jax.ShapeDtypeStruct((t, d), F32), jax.ShapeDtypeStruct((t, d), BF16)]
        + [acc_shape] * 5,
        in_specs=[row, _resident((D_FF, d)), wide, wide, _resident((2 * D_FF, d)), row, row, row, row, vec, vec, vec],
        out_specs=[wide, wide, row, row, acc, acc, acc, acc, acc], sem=("arbitrary",), comm=comm,
        args=(dy2, w_down, dsu, sg, w_gu_t, x1, x, y, dz2, sc2, g1, ln1g))


def _din(dproj, dkvn, w_in_t, x, dz1, sc1, tm, comm):
    t, d = x.shape

    def body(dp_ref, dkv_ref, w_ref, x_ref, dz1_ref, sc_ref, dx_ref, dpb_ref, dbin_ref, dsc_ref, dsh_ref):
        @pl.when(pl.program_id(0) == 0)
        def _():
            for r in (dbin_ref, dsc_ref, dsh_ref):
                r[...] = jnp.zeros_like(r)

        dp = jnp.concatenate([dp_ref[:, 0:ATTN_W], dp_ref[:, ATTN_W:ATTN_W + 2 * KV_W] + dkv_ref[...],
                              dp_ref[:, ATTN_W + 2 * KV_W:IN_W]], axis=1)
        dbin_ref[...] += _rowsum8(dp)
        dpb = dp.astype(BF16)
        dpb_ref[...] = dpb
        dh = _dot(dpb, w_ref[...])
        dsc_ref[...] += _rowsum8(dh * x_ref[...])
        dsh_ref[...] += _rowsum8(dh)
        dx_ref[...] = dh * (1.0 + sc_ref[...]) + ALPHA * dz1_ref[...]

    row = lambda w: pl.BlockSpec((tm, w), lambda i: (i, 0))
    return _call(
        body, name="din", grid=(t // tm,),
        out_shape=[jax.ShapeDtypeStruct((t, d), F32), jax.ShapeDtypeStruct((t, IN_W), BF16),
                   jax.ShapeDtypeStruct((8, IN_W), F32), jax.ShapeDtypeStruct((8, d), F32),
                   jax.ShapeDtypeStruct((8, d), F32)],
        in_specs=[row(IN_W), row(2 * KV_W), _full((IN_W, d)), row(d), row(d), _full((1, d))],
        out_specs=[row(d), row(IN_W), _full((8, IN_W)), _full((8, d)), _full((8, d))],
        sem=("arbitrary",), comm=comm, args=(dproj, dkvn, w_in_t, x, dz1, sc1))


def _wgrad(name, a, b, tmm, tk, comm=None, a2=None):
    t, m = a.shape
    n = b.shape[1]
    nk = t // tk
    nm = m // tmm

    def body(*refs):
        a_refs, (b_ref, o_ref, acc_ref) = refs[:-3], refs[-3:]
        i, k = pl.program_id(0), pl.program_id(1)

        @pl.when(k == 0)
        def _():
            acc_ref[...] = jnp.zeros_like(acc_ref)

        a_tile = a_refs[0][...] if a2 is None else jnp.where(i < nm, a_refs[0][...], a_refs[1][...])
        acc_ref[...] += _dot_tn(a_tile, b_ref[...])

        @pl.when(k == nk - 1)
        def _():
            o_ref[...] = acc_ref[...].astype(BF16)

    if a2 is None:
        a_specs, a_args, n_tiles = [pl.BlockSpec((tk, tmm), lambda i, k: (k, i))], (a,), nm
    else:
        a_specs = [pl.BlockSpec((tk, tmm), lambda i, k: (jnp.where(i < nm, k, 0), jnp.minimum(i, nm - 1))),
                   pl.BlockSpec((tk, tmm), lambda i, k: (jnp.where(i < nm, 0, k), jnp.maximum(i - nm, 0)))]
        a_args, n_tiles = (a, a2), 2 * nm
    (out,), got = _call(
        body, name=name, grid=(n_tiles, nk), out_shape=[jax.ShapeDtypeStruct((n_tiles * tmm, n), BF16)],
        in_specs=a_specs + [pl.BlockSpec((tk, n), lambda i, k: (k, 0))],
        out_specs=[pl.BlockSpec((tmm, n), lambda i, k: (i, 0))],
        scratch_shapes=[pltpu.VMEM((tmm, n), F32)], sem=("parallel", "arbitrary"), comm=comm, args=a_args + (b,))
    return out if comm is None else (out, got)


def _adamw(w, g, m, v):
    m = ADAM_B1 * m + (1.0 - ADAM_B1) * g
    v = ADAM_B2 * v + (1.0 - ADAM_B2) * (g * g)
    m_hat = m / (1.0 - ADAM_B1 ** ADAM_STEP)
    v_hat = v / (1.0 - ADAM_B2 ** ADAM_STEP)
    delta = -ADAM_LR * (m_hat / (jnp.sqrt(v_hat) + ADAM_EPS) + ADAM_WD * w)
    return delta, m, v


def _adam_reduce(name, parts, w, m, v, tr, comm=None):
    r, cdim = w.shape

    def body(p_ref, w_ref, m_ref, v_ref, g_ref, d_ref, mo_ref, vo_ref):
        g = p_ref[0].astype(F32)
        for s in range(1, N_DEV):
            g = g + p_ref[s].astype(F32)
        d_ref[...], mo_ref[...], vo_ref[...] = _adamw(w_ref[...], g, m_ref[...], v_ref[...])
        g_ref[...] = g

    tile = pl.BlockSpec((tr, cdim), lambda i: (i, 0))
    shp = jax.ShapeDtypeStruct((r, cdim), F32)
    res, got = _call(
        body, name=name, grid=(r // tr,), out_shape=[shp] * 4,
        in_specs=[pl.BlockSpec((N_DEV, tr, cdim), lambda i: (0, i, 0)), tile, tile, tile],
        out_specs=[tile] * 4, sem=("parallel",), comm=comm, args=(parts, w, m, v))
    return res if comm is None else (res, got)


def _adam_w_ada(c_all_t, dmod_cols, w, m, v):
    def body(ct_ref, dm_ref, w_ref, m_ref, v_ref, g_ref, d_ref, mo_ref, vo_ref):
        ct = ct_ref[...]
        s = (ct * _sigmoid(ct)).astype(BF16)
        g = _dot(s, dm_ref[...].astype(BF16))
        d_ref[...], mo_ref[...], vo_ref[...] = _adamw(w_ref[...], g, m_ref[...], v_ref[...])
        g_ref[...] = g

    shp = jax.ShapeDtypeStruct(w.shape, F32)
    return pl.pallas_call(
        body, name="adam_w_ada", grid=(1,), out_shape=[shp] * 4,
        in_specs=[_full(c_all_t.shape), _full(dmod_cols.shape)] + [_full(w.shape)] * 3,
        out_specs=[_full(w.shape)] * 4, compiler_params=_params(("arbitrary",)),
    )(c_all_t, dmod_cols, w, m, v)


SMALL_EARLY = ["rel_bias", "attn_sinks", "gmlp_ln_g", "gmlp_ln_b", "gmlp_w_s", "gmlp_b_s",
               "attn_out_g", "gmlp_out_g", "ln1_g", "ln1_b", "ln2_g", "ln2_b"]
SMALL_LATE = ["b_ada", "b_in", "loss"]
WEIGHTS = ["rel_bias", "w_ada", "b_ada", "w_in", "b_in", "attn_sinks", "gmlp_ln_g", "gmlp_ln_b", "gmlp_w_s",
           "gmlp_b_s", "attn_out_g", "gmlp_out_g", "w_out", "ln1_g", "ln1_b", "w_gate_up", "w_down", "ln2_g", "ln2_b"]


def _seg_rows(nelem):
    return -(-nelem // 1024) * 8


def _pack(named, names):
    parts = []
    for name in names:
        flat = named[name].reshape(-1).astype(F32)
        rows = _seg_rows(flat.shape[0])
        parts.append(jnp.pad(flat, (0, rows * 128 - flat.shape[0])).reshape(rows, 128))
    return jnp.concatenate(parts, axis=0)


def _adam_small(name, parts, names, wts, mom_m, mom_v):
    params = [n for n in names if n in wts]

    def view(n):
        nelem = math.prod(wts[n].shape)
        return (nelem // 128, 128) if nelem % 128 == 0 else (1, nelem)

    offsets, r0 = {}, 0
    for n in names:
        offsets[n] = r0
        r0 += _seg_rows(math.prod(wts[n].shape) if n in wts else 1)

    def body(*refs):
        p_ref, ins, outs = refs[0], refs[1:1 + 3 * len(params)], refs[1 + 3 * len(params):]

        def total(n, rows, lanes):
            o = offsets[n]
            g = p_ref[0, o:o + rows, 0:lanes]
            for s in range(1, N_DEV):
                g = g + p_ref[s, o:o + rows, 0:lanes]
            return g

        for i, n in enumerate(params):
            g = total(n, *view(n))
            w_ref, m_ref, v_ref = ins[3 * i:3 * i + 3]
            g_ref, d_ref, mo_ref, vo_ref = outs[4 * i:4 * i + 4]
            d_ref[...], mo_ref[...], vo_ref[...] = _adamw(w_ref[...], g, m_ref[...], v_ref[...])
            g_ref[...] = g
        for j, n in enumerate(n for n in names if n not in wts):
            outs[4 * len(params) + j][...] = total(n, 8, 128)

    args, in_specs, out_shape = [parts], [_full(parts.shape)], []
    for n in params:
        args += [t[n].reshape(view(n)) for t in (wts, mom_m, mom_v)]
        in_specs += [_full(view(n))] * 3
        out_shape += [jax.ShapeDtypeStruct(view(n), F32)] * 4
    out_shape += [jax.ShapeDtypeStruct((8, 128), F32) for n in names if n not in wts]
    res = pl.pallas_call(
        body, name=name, grid=(1,), out_shape=out_shape, in_specs=in_specs,
        out_specs=[_full(s.shape) for s in out_shape], compiler_params=_params(("arbitrary",)),
    )(*args)
    done = {n: tuple(r.reshape(wts[n].shape) for r in res[4 * i:4 * i + 4]) for i, n in enumerate(params)}
    sums = {n: res[4 * len(params) + j] for j, n in enumerate(n for n in names if n not in wts)}
    return done, sums


def _t5_bucket_map():
    qi = jnp.arange(BLK)[:, None]
    si = jnp.arange(2 * BLK)[None, :]
    n = jnp.maximum(qi + BLK - si, 0)
    max_exact = N_BUCKETS // 2
    nf = jnp.maximum(n, max_exact).astype(F32)
    large = max_exact + (jnp.log(nf / max_exact) / math.log(MAX_DISTANCE / max_exact)
                         * (N_BUCKETS - max_exact)).astype(jnp.int32)
    large = jnp.minimum(large, N_BUCKETS - 1)
    return jnp.where(n < max_exact, n, large).astype(jnp.int32)


def kernel(x, c, rel_bias, w_ada, b_ada, w_in, b_in, attn_sinks, gmlp_ln_g, gmlp_ln_b, gmlp_w_s, gmlp_b_s, attn_out_g, gmlp_out_g, w_out, ln1_g, ln1_b, w_gate_up, w_down, ln2_g, ln2_b, loss_target, m_rel_bias, m_w_ada, m_b_ada, m_w_in, m_b_in, m_attn_sinks, m_gmlp_ln_g, m_gmlp_ln_b, m_gmlp_w_s, m_gmlp_b_s, m_attn_out_g, m_gmlp_out_g, m_w_out, m_ln1_g, m_ln1_b, m_w_gate_up, m_w_down, m_ln2_g, m_ln2_b, v_rel_bias, v_w_ada, v_b_ada, v_w_in, v_b_in, v_attn_sinks, v_gmlp_ln_g, v_gmlp_ln_b, v_gmlp_w_s, v_gmlp_b_s, v_attn_out_g, v_gmlp_out_g, v_w_out, v_ln1_g, v_ln1_b, v_w_gate_up, v_w_down, v_ln2_g, v_ln2_b):
    wts = dict(rel_bias=rel_bias, w_ada=w_ada, b_ada=b_ada, w_in=w_in, b_in=b_in, attn_sinks=attn_sinks,
               gmlp_ln_g=gmlp_ln_g, gmlp_ln_b=gmlp_ln_b, gmlp_w_s=gmlp_w_s, gmlp_b_s=gmlp_b_s,
               attn_out_g=attn_out_g, gmlp_out_g=gmlp_out_g, w_out=w_out, ln1_g=ln1_g, ln1_b=ln1_b,
               w_gate_up=w_gate_up, w_down=w_down, ln2_g=ln2_g, ln2_b=ln2_b)
    mom_m = dict(rel_bias=m_rel_bias, w_ada=m_w_ada, b_ada=m_b_ada, w_in=m_w_in, b_in=m_b_in,
                 attn_sinks=m_attn_sinks, gmlp_ln_g=m_gmlp_ln_g, gmlp_ln_b=m_gmlp_ln_b, gmlp_w_s=m_gmlp_w_s,
                 gmlp_b_s=m_gmlp_b_s, attn_out_g=m_attn_out_g, gmlp_out_g=m_gmlp_out_g, w_out=m_w_out,
                 ln1_g=m_ln1_g, ln1_b=m_ln1_b, w_gate_up=m_w_gate_up, w_down=m_w_down, ln2_g=m_ln2_g,
                 ln2_b=m_ln2_b)
    mom_v = dict(rel_bias=v_rel_bias, w_ada=v_w_ada, b_ada=v_b_ada, w_in=v_w_in, b_in=v_b_in,
                 attn_sinks=v_attn_sinks, gmlp_ln_g=v_gmlp_ln_g, gmlp_ln_b=v_gmlp_ln_b, gmlp_w_s=v_gmlp_w_s,
                 gmlp_b_s=v_gmlp_b_s, attn_out_g=v_attn_out_g, gmlp_out_g=v_gmlp_out_g, w_out=v_w_out,
                 ln1_g=v_ln1_g, ln1_b=v_ln1_b, w_gate_up=v_w_gate_up, w_down=v_w_down, ln2_g=v_ln2_g,
                 ln2_b=v_ln2_b)

    t = x.shape[1]
    tm = min(512, t)
    tn_ff = D_FF // 2
    tk_long, tk_short = min(4096, t), min(2048, t)
    me = 4 * lax.axis_index("x") + 2 * lax.axis_index("y") + lax.axis_index("c")
    xs = x[0]
    target = loss_target[0]

    c_g, w_in_g = _exchange("gather_in", [jnp.broadcast_to(c, (8, D_MODEL)), w_in[0].T.astype(BF16)],
                            ("gather", "gather2"))
    c_all = c_g[:, 0, :]
    w_in_t = w_in_g.reshape(IN_W, D_MODEL)

    ncol = w_ada.shape[2]
    b_cols = lax.dynamic_slice(b_ada, (0, me * ncol), (1, ncol))
    mod_part = _mod_partial(c_all, w_ada[0], b_cols)
    bucket = _t5_bucket_map()
    (bias,), (mod_g,) = _bias_table(rel_bias, bucket, comm=([mod_part], ("gather",)))
    mod = lax.dynamic_slice(mod_g, (0, me, 0), (N_DEV, 1, ncol)).reshape(1, N_DEV * ncol)
    sh1, sc1, g1, sh2, sc2, g2 = [mod[:, i * D_MODEL:(i + 1) * D_MODEL] for i in range(6)]

    causal = jnp.tril(jnp.ones((BLK, BLK), dtype=bool))
    ws = jnp.where(causal[None], gmlp_w_s[0], 0.0).astype(BF16)
    pair = lambda w: jnp.concatenate([w[0::2], w[1::2]], axis=2)
    ws2, wst2 = pair(ws), pair(jnp.swapaxes(ws, 1, 2))
    bfull = jnp.repeat(gmlp_b_s[0].T, GMLP_W // N_GROUPS, axis=1)
    sinks = attn_sinks[0]

    proj, h1 = _inproj(xs, sc1, sh1, w_in_t, b_in, tm)
    (mixed, *kept), (w_out_g, w_gu_g) = _mix_fwd(
        proj, bias, sinks, gmlp_ln_g, gmlp_ln_b, ws2, bfull, attn_out_g, gmlp_out_g,
        comm=([w_out[0].astype(BF16), w_gate_up[0].T.astype(BF16)], ("gather2", "gather2")))
    w_out_f = w_out_g.reshape(D_MODEL, D_MODEL)
    w_gu_t = w_gu_g.reshape(2 * D_FF, D_MODEL)
    (y1, x1, h2), (w_down_g,) = _outproj(mixed, w_out_f, xs, g1, ln1_g, ln1_b, sc2, sh2, tm,
                                         comm=([w_down[0].astype(BF16)], ("gather2",)))
    w_down_f = w_down_g.reshape(D_FF, D_MODEL)
    dsu, sg, act, dz2, dy2, loss_p, d_ln2g, d_ln2b, d_g2 = _ffn_fwd(h2, w_gu_t, w_down_f, x1, target, g2, ln2_g, ln2_b,
                                                                    min(256, t))

    slots = lambda a: a.reshape(N_DEV, -1, D_MODEL)
    dw_down = _wgrad("wgrad_down", act, dy2, tn_ff, tk_short)
    (dgate, dup, dz1, dy1, d_sc2, d_sh2, d_ln1g, d_ln1b, d_g1), (r_down,) = _ffn_bwd(
        dy2, w_down_f, dsu, sg, w_gu_t, x1, xs, y1, dz2, sc2, g1, ln1_g, min(256, t),
        comm=([slots(dw_down)], ("scatter",)))
    dw_gu_t = _wgrad("wgrad_gate_up", dgate, h2, tn_ff, tk_short, a2=dup)
    dw_out = _wgrad("wgrad_out", mixed, dy1, D_MODEL, tk_long)
    ((dproj, dkvn, dl_acc, dsink_acc, d_lng, d_lnb, d_ws, d_bs, d_aog, d_gog), (r_gu, r_out)) = _mix_bwd(
        proj, gmlp_ln_g, gmlp_ln_b, ws2, wst2, bfull, attn_out_g, gmlp_out_g, dy1, w_out_f.T, kept,
        comm=([slots(dw_gu_t), slots(dw_out)], ("scatter", "scatter")))
    d_relb = _bias_grad(dl_acc, bucket)

    rsum = lambda a: jnp.sum(a, axis=0)
    early_g = dict(
        rel_bias=d_relb[:, 0, :N_BUCKETS].T, attn_sinks=rsum(dsink_acc)[:N_HEADS],
        gmlp_ln_g=rsum(d_lng), gmlp_ln_b=rsum(d_lnb), gmlp_w_s=jnp.where(causal[None], d_ws, 0.0),
        gmlp_b_s=jnp.sum(d_bs.reshape(BLK, N_GROUPS, GMLP_W // N_GROUPS), axis=2).T,
        attn_out_g=rsum(d_aog), gmlp_out_g=rsum(d_gog), ln1_g=rsum(d_ln1g), ln1_b=rsum(d_ln1b),
        ln2_g=rsum(d_ln2g), ln2_b=rsum(d_ln2b))
    (grad_x, dproj_b, d_bin, d_sc1, d_sh1), _ = _din(dproj, dkvn, w_in_t, xs, dz1, sc1, tm, comm=None)
    dw_in_t, (early_all,) = _wgrad("wgrad_in", dproj_b, h1, IN_W // 2, tk_long,
                                   comm=([_pack(early_g, SMALL_EARLY)], ("gather2",)))
    dmod = jnp.concatenate([rsum(d_sh1), rsum(d_sc1), rsum(d_g1), rsum(d_sh2), rsum(d_sc2), rsum(d_g2)])
    late_g = dict(b_ada=dmod, b_in=rsum(d_bin), loss=(0.5 / D_MODEL * jnp.sum(loss_p)).reshape(1))
    tr = lambda a: jnp.swapaxes(a, -1, -2)
    big = {}
    res_gu, (late_all, r_in) = _adam_reduce(
        "adam_w_gu", r_gu, tr(w_gate_up[0]), tr(m_w_gate_up[0]), tr(v_w_gate_up[0]), 176,
        comm=([_pack(late_g, SMALL_LATE), slots(dw_in_t)], ("gather", "scatter")))
    big["w_gate_up"] = [tr(o)[None] for o in res_gu]

    small, _ = _adam_small("adam_small_early", early_all, SMALL_EARLY, wts, mom_m, mom_v)
    small_late, sums = _adam_small("adam_small_late", late_all, SMALL_LATE, wts, mom_m, mom_v)
    small.update(small_late)
    loss = sums["loss"][0, 0]

    dmod_all = late_all[:, :_seg_rows(6 * D_MODEL), :].reshape(N_DEV, 6 * D_MODEL)
    dmod_cols = lax.dynamic_slice(dmod_all, (0, me * ncol), (N_DEV, ncol))
    kpad = 128 - N_DEV
    ada = _adam_w_ada(jnp.pad(c_all.T, ((0, 0), (0, kpad))), jnp.pad(dmod_cols, ((0, kpad), (0, 0))),
                      w_ada[0], m_w_ada[0], v_w_ada[0])

    big["w_in"] = [tr(o)[None] for o in _adam_reduce("adam_w_in", r_in, w_in[0].T, m_w_in[0].T, v_w_in[0].T, 112)]
    big["w_out"] = [o[None] for o in _adam_reduce("adam_w_out", r_out, w_out[0], m_w_out[0], v_w_out[0], 128)]
    big["w_down"] = [o[None] for o in _adam_reduce("adam_w_down", r_down, w_down[0], m_w_down[0], v_w_down[0], 176)]
    big["w_ada"] = [o[None] for o in ada]

    outs = [[], [], [], []]
    for name in WEIGHTS:
        for i in range(4):
            outs[i].append(big[name][i] if name in big else small[name][i])
    return (loss, grad_x[None], *outs[0], *outs[1], *outs[2], *outs[3])
```

```python
import math

import jax
import jax.numpy as jnp
from jax import lax
from jax.experimental import pallas as pl
from jax.experimental.pallas import tpu as pltpu

F32 = jnp.float32
BF16 = jnp.bfloat16
MESH = pl.DeviceIdType.MESH

N_DEV = 8
D_MODEL = 1024
HEAD_DIM = 64
N_HEADS = 8
N_GROUPS = 8
ATTN_W = 512
KV_W = 128
GMLP_W = 512
IN_W = 1792
BLK = 128
N_BUCKETS = 32
MAX_DISTANCE = 128
D_FF = 2816
ALPHA = 2.0 ** 0.25
LN_EPS = 1e-5
NEG_INF = -1e30
ADAM_LR = 0.001
ADAM_B1 = 0.9
ADAM_B2 = 0.999
ADAM_EPS = 1e-08
ADAM_WD = 0.01
ADAM_STEP = 10
GELU_C0 = math.sqrt(2.0 / math.pi)
GELU_C1 = 0.044715

VMEM_LIMIT = 56 * 1024 * 1024


def _params(sem):
    return pltpu.CompilerParams(dimension_semantics=sem, vmem_limit_bytes=VMEM_LIMIT)


def _dot(a, b):
    return lax.dot_general(a, b, (((1,), (0,)), ((), ())), preferred_element_type=F32)


def _dot_nt(a, b):
    return lax.dot_general(a, b, (((1,), (1,)), ((), ())), preferred_element_type=F32)


def _dot_tn(a, b):
    return lax.dot_general(a, b, (((0,), (0,)), ((), ())), preferred_element_type=F32)


def _full(shape):
    nd = len(shape)
    return pl.BlockSpec(shape, lambda *_: (0,) * nd)


def _rowsum8(v):
    r, c = v.shape
    return jnp.sum(v.reshape(r // 8, 8, c), axis=0)


def _sigmoid(v):
    return 1.0 / (1.0 + jnp.exp(-v))


def _gelu_parts(v):
    v2 = v * v
    t = jnp.tanh(GELU_C0 * (v + GELU_C1 * v * v2))
    g = 0.5 * v * (1.0 + t)
    dg = 0.5 * (1.0 + t) + 0.5 * v * (1.0 - t * t) * (GELU_C0 * (1.0 + 3.0 * GELU_C1 * v2))
    return g, dg


def _ln_stats(z):
    mu = jnp.mean(z, axis=1, keepdims=True)
    zc = z - mu
    var = jnp.mean(zc * zc, axis=1, keepdims=True)
    rstd = lax.rsqrt(var + LN_EPS)
    return zc * rstd, rstd


def _ln_bwd(dxhat, xhat, rstd):
    m1 = jnp.mean(dxhat, axis=1, keepdims=True)
    m2 = jnp.mean(dxhat * xhat, axis=1, keepdims=True)
    return rstd * (dxhat - m1 - xhat * m2)


def _seg_mean64(v):
    r = v.shape[0]
    lo = lax.broadcasted_iota(jnp.int32, (r, 128), 1) < 64
    outs = []
    for j in range(v.shape[1] // 128):
        ch = v[:, 128 * j:128 * (j + 1)]
        s_lo = jnp.sum(jnp.where(lo, ch, 0.0), axis=1, keepdims=True)
        s_hi = jnp.sum(jnp.where(lo, 0.0, ch), axis=1, keepdims=True)
        outs.append(jnp.where(lo, s_lo, s_hi) * (1.0 / 64.0))
    return jnp.concatenate(outs, axis=1)


def _rms(a, g):
    r = lax.rsqrt(jnp.mean(a * a, axis=1, keepdims=True) + LN_EPS)
    return a * r * g, r


def _rms_bwd(dout, a, r, g):
    t = dout * g
    return r * t - a * (r * r * r) * jnp.mean(t * a, axis=1, keepdims=True)


PEER_ORDER = (1, 2, 4, 3, 5, 6, 7)


def _peer(j):
    x, y, c = lax.axis_index("x"), lax.axis_index("y"), lax.axis_index("c")
    px = 1 - x if j & 4 else x
    py = 1 - y if j & 2 else y
    pc = 1 - c if j & 1 else c
    return (px, py, pc), 4 * px + 2 * py + pc


SIBLING = 1
CHIP_FLIPS = (4, 2, 6)


def _exchange_phase(phase, ins, outs, modes, send_sems, recv_sems, loc_sems):
    me = 4 * lax.axis_index("x") + 2 * lax.axis_index("y") + lax.axis_index("c")
    for k, mode in enumerate(modes):
        def copy(i, src, slot, dev, k=k):
            return pltpu.make_async_remote_copy(src_ref=src, dst_ref=outs[k].at[slot], send_sem=send_sems.at[k, i],
                                                recv_sem=recv_sems.at[k, i], device_id=dev, device_id_type=MESH)

        src_me = ins[k].at[me] if mode == "scatter" else ins[k]
        local = pltpu.make_async_copy(src_me, outs[k].at[me], loc_sems.at[k])
        if mode == "gather2":
            sib_dev, sib_idx = _peer(SIBLING)
            chips = [_peer(j) for j in CHIP_FLIPS]
            far = [_peer(j | SIBLING)[1] for j in CHIP_FLIPS]
            if phase == "start":
                local.start()
                copy(0, ins[k], me, sib_dev).start()
                for i, (dev, _) in enumerate(chips):
                    copy(1 + i, ins[k], me, dev).start()
            elif phase == "mid":
                for i, (dev, idx) in enumerate(chips):
                    copy(1 + i, ins[k], idx, dev).wait_recv()
                    copy(4 + i, outs[k].at[idx], idx, sib_dev).start()
            else:
                copy(0, ins[k], sib_idx, sib_dev).wait_recv()
                for i, slot in enumerate(far):
                    copy(4 + i, ins[k], slot, sib_dev).wait_recv()
                copy(0, ins[k], me, sib_dev).wait_send()
                for i, (dev, idx) in enumerate(chips):
                    copy(1 + i, ins[k], me, dev).wait_send()
                    copy(4 + i, outs[k].at[idx], idx, sib_dev).wait_send()
                local.wait()
            continue
        peers = [_peer(j) for j in PEER_ORDER]
        if phase == "start":
            local.start()
            for i, (dev, idx) in enumerate(peers):
                copy(i, ins[k].at[idx] if mode == "scatter" else ins[k], me, dev).start()
        elif phase == "end":
            for i, (dev, idx) in enumerate(peers):
                copy(i, src_me, idx, dev).wait_recv()
            for i, (dev, idx) in enumerate(peers):
                copy(i, src_me, me, dev).wait_send()
            local.wait()


def _exchange_shapes(arrays, modes):
    return [jax.ShapeDtypeStruct((N_DEV,) + (a.shape[1:] if m == "scatter" else a.shape), a.dtype)
            for a, m in zip(arrays, modes)]


def _exchange_sems(n):
    return [pltpu.SemaphoreType.DMA((n, N_DEV - 1)), pltpu.SemaphoreType.DMA((n, N_DEV - 1)),
            pltpu.SemaphoreType.DMA((n,))]


def _exchange(name, arrays, modes):
    n = len(arrays)

    def body(*refs):
        for phase in ("start", "mid", "end"):
            _exchange_phase(phase, refs[:n], refs[n:2 * n], modes, *refs[2 * n:])

    any_spec = pl.BlockSpec(memory_space=pl.ANY)
    return pl.pallas_call(
        body, name=name, out_shape=_exchange_shapes(arrays, modes),
        in_specs=[any_spec] * n, out_specs=[any_spec] * n, scratch_shapes=_exchange_sems(n),
    )(*arrays)


def _call(body, *, name, grid, in_specs, out_specs, out_shape, args, sem, scratch_shapes=(), comm=None):
    if comm is None:
        outs = pl.pallas_call(body, name=name, grid=grid, in_specs=list(in_specs), out_specs=list(out_specs),
                              out_shape=list(out_shape), scratch_shapes=list(scratch_shapes),
                              compiler_params=_params(sem))(*args)
        return list(outs), []
    arrays, modes = comm
    n_in, n_out, nc, ns = len(in_specs), len(out_specs), len(arrays), len(scratch_shapes)
    n_steps = math.prod(grid)

    def hosted(*refs):
        ins, cins = refs[:n_in], refs[n_in:n_in + nc]
        outs, couts = refs[n_in + nc:n_in + nc + n_out], refs[n_in + nc + n_out:n_in + 2 * nc + n_out]
        scratch = refs[n_in + 2 * nc + n_out:]
        ex = (cins, couts, modes) + tuple(scratch[ns:])
        step = pl.program_id(0)
        for ax in range(1, len(grid)):
            step = step * grid[ax] + pl.program_id(ax)

        @pl.when(step == 0)
        def _():
            _exchange_phase("start", *ex)

        body(*ins, *outs, *scratch[:ns])

        if "gather2" in modes:
            @pl.when(step == (3 * n_steps) // 4)
            def _():
                _exchange_phase("mid", *ex)

        @pl.when(step == n_steps - 1)
        def _():
            _exchange_phase("end", *ex)

    any_spec = pl.BlockSpec(memory_space=pl.ANY)
    res = pl.pallas_call(
        hosted, name=name, grid=grid, in_specs=list(in_specs) + [any_spec] * nc,
        out_specs=list(out_specs) + [any_spec] * nc, out_shape=list(out_shape) + _exchange_shapes(arrays, modes),
        scratch_shapes=list(scratch_shapes) + _exchange_sems(nc),
        compiler_params=_params(tuple("arbitrary" for _ in grid)))(*args, *arrays)
    return list(res[:n_out]), list(res[n_out:])


def _mod_partial(c_all, w_ada, b_ada_cols):
    def body(c_ref, w_ref, b_ref, o_ref):
        cv = c_ref[...]
        s = (cv * _sigmoid(cv)).astype(BF16)
        o_ref[...] = _dot(s, w_ref[...].astype(BF16)) + b_ref[...]

    ncol = w_ada.shape[1]
    return pl.pallas_call(
        body, name="mod_partial", out_shape=jax.ShapeDtypeStruct((N_DEV, ncol), F32),
        in_specs=[_full(c_all.shape), _full(w_ada.shape), _full(b_ada_cols.shape)],
        out_specs=_full((N_DEV, ncol)), grid=(1,), compiler_params=_params(("arbitrary",)),
    )(c_all, w_ada, b_ada_cols)


def _bias_table(rel_bias, bucket, comm):
    def body(rb_ref, bk_ref, o_ref):
        h = pl.program_id(0)
        bk = bk_ref[...]
        acc = jnp.zeros((BLK, 2 * BLK), F32)
        for b in range(N_BUCKETS):
            acc = jnp.where(bk == b, rb_ref[b, h], acc)
        dist = (lax.broadcasted_iota(jnp.int32, (BLK, 2 * BLK), 0) + BLK
                - lax.broadcasted_iota(jnp.int32, (BLK, 2 * BLK), 1))
        o_ref[0] = jnp.where((dist >= 0) & (dist < BLK), acc, NEG_INF)

    return _call(
        body, name="bias_table", out_shape=[jax.ShapeDtypeStruct((N_HEADS, BLK, 2 * BLK), F32)],
        in_specs=[pl.BlockSpec(memory_space=pltpu.SMEM), _full((BLK, 2 * BLK))],
        out_specs=[pl.BlockSpec((1, BLK, 2 * BLK), lambda h: (h, 0, 0))], grid=(N_HEADS,),
        sem=("arbitrary",), comm=comm, args=(rel_bias, bucket))


def _bias_grad(dl_acc, bucket):
    def body(dl_ref, bk_ref, o_ref):
        bk = bk_ref[...]
        dl = dl_ref[0]
        lane = lax.broadcasted_iota(jnp.int32, (1, 128), 1)
        row = jnp.zeros((1, 128), F32)
        for b in range(N_BUCKETS):
            s = jnp.sum(jnp.sum(jnp.where(bk == b, dl, 0.0), axis=1, keepdims=True), axis=0, keepdims=True)
            row = jnp.where(lane == b, s, row)
        o_ref[0] = row

    return pl.pallas_call(
        body, name="bias_grad", out_shape=jax.ShapeDtypeStruct((N_HEADS, 1, 128), F32),
        in_specs=[pl.BlockSpec((1, BLK, 2 * BLK), lambda h: (h, 0, 0)), _full((BLK, 2 * BLK))],
        out_specs=pl.BlockSpec((1, 1, 128), lambda h: (h, 0, 0)), grid=(N_HEADS,),
        compiler_params=_params(("arbitrary",)),
    )(dl_acc, bucket)


def _inproj(x, sc1, sh1, w_in_t, b_in, tm, comm):
    t, d = x.shape
    n = w_in_t.shape[0]

    def body(x_ref, sc_ref, sh_ref, w_ref, b_ref, proj_ref, h_ref):
        h = (x_ref[...] * (1.0 + sc_ref[...]) + sh_ref[...]).astype(BF16)
        h_ref[...] = h
        proj_ref[...] = _dot_nt(h, w_ref[...]) + b_ref[...]

    row = lambda w: pl.BlockSpec((tm, w), lambda i: (i, 0))
    return _call(
        body, name="inproj", grid=(t // tm,),
        out_shape=[jax.ShapeDtypeStruct((t, n), F32), jax.ShapeDtypeStruct((t, d), BF16)],
        in_specs=[row(d), _full((1, d)), _full((1, d)), _full((n, d)), _full((1, n))],
        out_specs=[row(n), row(d)], sem=("parallel",), comm=comm, args=(x, sc1, sh1, w_in_t, b_in))


HALF = 64
ROWS = 32


def _lane_lo(rows):
    return lax.broadcasted_iota(jnp.int32, (rows, 128), 1) < 64


def _mix_stage_kv(proj_ref, kvp_ref, s):
    lo = _lane_lo(2 * BLK)
    for name, col in (("k", ATTN_W), ("v", ATTN_W + KV_W)):
        cur = jnp.concatenate([kvp_ref[:, col - ATTN_W:col - ATTN_W + KV_W], proj_ref[:, col:col + KV_W]], axis=0)
        plain, swapped = cur.astype(BF16), pltpu.roll(cur, 64, 1).astype(BF16)
        zero = jnp.zeros_like(plain)
        for g in range(2):
            dup = jnp.where(lo, plain, swapped) if g == 0 else jnp.where(lo, swapped, plain)
            s[name + "d"][g] = dup
            s[name + "m"][g] = jnp.concatenate([jnp.where(lo, dup, zero), jnp.where(lo, zero, dup)], axis=0)


def _group_rows(ref, g):
    return ref[4 * g:4 * g + 4].reshape(4 * BLK, ref.shape[2])


def _pair_rows(ref, g):
    return jnp.concatenate([jnp.concatenate([ref[4 * g + 2 * c], ref[4 * g + 2 * c + 1]], axis=1) for c in range(2)],
                           axis=0)


def _mask_heads(src_ref, dst_ref):
    lo = _lane_lo(BLK)
    for j in range(4):
        chunk = src_ref[:, 128 * j:128 * (j + 1)]
        dst_ref[2 * j] = jnp.where(lo, chunk, 0.0).astype(BF16)
        dst_ref[2 * j + 1] = jnp.where(lo, 0.0, chunk).astype(BF16)


def _mix_stage_attn(proj_ref, bias_ref, sinks_ref, n, s):
    _mask_heads(proj_ref, s["qm"])
    for g in range(2):
        s["lg"][g] = _dot_nt(_group_rows(s["qm"], g), s["kd"][g])
    n0mask = (n == 0) & (lax.broadcasted_iota(jnp.int32, (HALF, 2 * BLK), 1) < BLK)
    lane = lax.broadcasted_iota(jnp.int32, (HALF, 128), 1)
    for hf in range(BLK // HALF):
        rows = slice(HALF * hf, HALF * (hf + 1))
        psink = jnp.zeros((HALF, 128), F32)
        for h in range(N_HEADS):
            sk = sinks_ref[h]
            grows = slice(BLK * (h % 4) + HALF * hf, BLK * (h % 4) + HALF * (hf + 1))
            logit = s["lg"][h // 4, grows, :] * (HEAD_DIM ** -0.5) + bias_ref[h, rows, :]
            logit = jnp.where(n0mask, NEG_INF, logit)
            m = jnp.maximum(jnp.max(logit, axis=1, keepdims=True), sk)
            e = jnp.exp(logit - m)
            es = jnp.exp(sk - m)
            inv = 1.0 / (jnp.sum(e, axis=1, keepdims=True) + es)
            p = e * inv
            s["p"][h, rows, :] = p
            s["pb"][h, rows, :] = p.astype(BF16)
            psink = jnp.where(lane == h, es * inv, psink)
        s["psink"][rows, :] = psink
    for g in range(2):
        out = _dot(_pair_rows(s["pb"], g), s["vm"][g])
        s["attn"][:, 256 * g:256 * g + 128] = out[0:BLK]
        s["attn"][:, 256 * g + 128:256 * g + 256] = out[BLK:2 * BLK]


def _mix_stage_gmlp_pre(proj_ref, lng, lnb, s, keep):
    c0 = ATTN_W + 2 * KV_W
    for r0 in range(0, BLK, ROWS):
        rows = slice(r0, r0 + ROWS)
        u, du = _gelu_parts(proj_ref[rows, c0:c0 + GMLP_W])
        a, da = _gelu_parts(proj_ref[rows, c0 + GMLP_W:c0 + 2 * GMLP_W])
        ac = a - _seg_mean64(a)
        rstd = lax.rsqrt(_seg_mean64(ac * ac) + LN_EPS)
        vhat = ac * rstd
        s["u"][rows, :] = u
        s["vnb"][rows, :] = (vhat * lng + lnb).astype(BF16)
        if keep:
            s["du"][rows, :] = du
            s["da"][rows, :] = da
            s["vhat"][rows, :] = vhat
            s["rstd"][rows, :] = rstd


def _stack_halves(chunk):
    lo = _lane_lo(BLK)
    zero = jnp.zeros_like(chunk)
    return jnp.concatenate([jnp.where(lo, chunk, zero), jnp.where(lo, zero, chunk)], axis=0)


def _mix_stage_gmlp_mix(ws2_ref, bfull_ref, s):
    for j in range(4):
        cols = slice(128 * j, 128 * (j + 1))
        s["ms"][:, cols] = _dot(ws2_ref[j], _stack_halves(s["vnb"][:, cols])) + bfull_ref[:, cols]


def _mix_scratch(keep):
    f32 = lambda *shape: pltpu.VMEM(shape, F32)
    b16 = lambda *shape: pltpu.VMEM(shape, BF16)
    names = dict(kd=b16(2, 2 * BLK, 128), vd=b16(2, 2 * BLK, 128), km=b16(2, 4 * BLK, 128), vm=b16(2, 4 * BLK, 128),
                 qm=b16(N_HEADS, BLK, 128), lg=f32(2, 4 * BLK, 2 * BLK), pb=b16(N_HEADS, BLK, 2 * BLK),
                 u=f32(BLK, GMLP_W), vnb=b16(BLK, GMLP_W), ms=f32(BLK, GMLP_W))
    if keep:
        names.update(dom=b16(N_HEADS, BLK, 128), dls=b16(N_HEADS, BLK, 2 * BLK),
                     dattn=f32(BLK, ATTN_W), dmix=f32(BLK, D_MODEL), du=f32(BLK, GMLP_W), da=f32(BLK, GMLP_W),
                     vhat=f32(BLK, GMLP_W), rstd=f32(BLK, GMLP_W), dmsb=b16(BLK, GMLP_W), dvn=f32(BLK, GMLP_W))
    return list(names), list(names.values())


def _mix_specs(with_logit_inputs):
    logit_inputs = [_full((N_HEADS, BLK, 2 * BLK)), pl.BlockSpec(memory_space=pltpu.SMEM)] if with_logit_inputs else []
    return [pl.BlockSpec((BLK, IN_W), lambda n: (n, 0)),
            pl.BlockSpec((BLK, 2 * KV_W), lambda n: (jnp.maximum(n - 1, 0), ATTN_W // (2 * KV_W)))] + logit_inputs + [
            _full((1, GMLP_W)), _full((1, GMLP_W)),
            _full((N_GROUPS // 2, BLK, 2 * BLK)), _full((BLK, GMLP_W)),
            _full((1, ATTN_W)), _full((1, GMLP_W))]


KEPT = [("p", (N_HEADS, BLK, 2 * BLK), F32), ("psink", (BLK, 128), F32), ("attn", (BLK, ATTN_W), F32)]


def _kept_shapes(t):
    full = lambda blk: (blk[0], t, blk[2]) if len(blk) == 3 else (t, blk[1])
    return [jax.ShapeDtypeStruct(full(blk), dt) for _, blk, dt in KEPT]


def _kept_specs():
    return [pl.BlockSpec(blk, (lambda n: (0, n, 0)) if len(blk) == 3 else (lambda n: (n, 0))) for _, blk, _ in KEPT]


def _mix_fwd(proj, bias, sinks, lng, lnb, ws2, bfull, aog, gog, comm):
    t = proj.shape[0]
    names, shapes = _mix_scratch(False)

    def body(proj_ref, kvp_ref, bias_ref, sinks_ref, lng_ref, lnb_ref, ws2_ref, bfull_ref, aog_ref, gog_ref,
             out_ref, *rest):
        s = dict(zip([name for name, _, _ in KEPT] + names, rest))
        n = pl.program_id(0)
        _mix_stage_kv(proj_ref, kvp_ref, s)
        _mix_stage_attn(proj_ref, bias_ref, sinks_ref, n, s)
        _mix_stage_gmlp_pre(proj_ref, lng_ref[...], lnb_ref[...], s, False)
        _mix_stage_gmlp_mix(ws2_ref, bfull_ref, s)
        for r0 in range(0, BLK, ROWS):
            rows = slice(r0, r0 + ROWS)
            out_ref[rows, 0:ATTN_W] = _rms(s["attn"][rows, :], aog_ref[...])[0].astype(BF16)
            out_ref[rows, ATTN_W:ATTN_W + GMLP_W] = _rms(s["u"][rows, :] * s["ms"][rows, :], gog_ref[...])[0].astype(BF16)

    return _call(
        body, name="mix_fwd", grid=(t // BLK,),
        out_shape=[jax.ShapeDtypeStruct((t, D_MODEL), BF16)] + _kept_shapes(t),
        in_specs=_mix_specs(True), out_specs=[pl.BlockSpec((BLK, D_MODEL), lambda n: (n, 0))] + _kept_specs(),
        scratch_shapes=shapes,
        sem=("parallel",), comm=comm, args=(proj, proj, bias, sinks, lng, lnb, ws2, bfull, aog, gog))


def _mix_bwd(proj, lng, lnb, ws2, wst2, bfull, aog, gog, dy, w_out, kept, comm):
    t = proj.shape[0]
    nb = t // BLK
    names, shapes = _mix_scratch(True)
    c_gu = ATTN_W + 2 * KV_W

    def body(proj_ref, kvp_ref, lng_ref, lnb_ref, ws2_ref, bfull_ref, aog_ref, gog_ref,
             wst2_ref, dy_ref, wout_ref, *rest):
        n_kept = len(KEPT)
        s = dict(zip([name for name, _, _ in KEPT], rest[:n_kept]))
        (dproj_ref, dkvn_ref, dl_ref, dsink_ref, dlng_ref, dlnb_ref, dws_ref, dbs_ref, daog_ref,
         dgog_ref) = rest[n_kept:n_kept + 10]
        s.update(zip(names, rest[n_kept + 10:]))
        n = pl.program_id(0)

        @pl.when(n == 0)
        def _():
            for r in (dl_ref, dsink_ref, dlng_ref, dlnb_ref, dws_ref, dbs_ref, daog_ref, dgog_ref):
                r[...] = jnp.zeros_like(r)

        s["dmix"][...] = _dot(dy_ref[...], wout_ref[...])
        _mix_stage_kv(proj_ref, kvp_ref, s)
        _mask_heads(proj_ref, s["qm"])
        lng = lng_ref[...]
        _mix_stage_gmlp_pre(proj_ref, lng, lnb_ref[...], s, True)
        _mix_stage_gmlp_mix(ws2_ref, bfull_ref, s)

        aog, gog = aog_ref[...], gog_ref[...]
        for r0 in range(0, BLK, ROWS):
            rows = slice(r0, r0 + ROWS)
            attn, dma = s["attn"][rows, :], s["dmix"][rows, 0:ATTN_W]
            _, r_a = _rms(attn, aog)
            daog_ref[...] += _rowsum8(dma * attn * r_a)
            s["dattn"][rows, :] = _rms_bwd(dma, attn, r_a, aog)
            u, ms, dmg = s["u"][rows, :], s["ms"][rows, :], s["dmix"][rows, ATTN_W:ATTN_W + GMLP_W]
            gm = u * ms
            _, r_g = _rms(gm, gog)
            dgog_ref[...] += _rowsum8(dmg * gm * r_g)
            dgm = _rms_bwd(dmg, gm, r_g, gog)
            dproj_ref[rows, c_gu:c_gu + GMLP_W] = dgm * ms * s["du"][rows, :]
            dms = dgm * u
            dbs_ref[rows, :] += dms
            s["dmsb"][rows, :] = dms.astype(BF16)

        _mask_heads(s["dattn"], s["dom"])
        for g in range(2):
            s["lg"][g] = _dot_nt(_group_rows(s["dom"], g), s["vd"][g])
        lane = lax.broadcasted_iota(jnp.int32, (HALF, 128), 1)
        for hf in range(BLK // HALF):
            rows = slice(HALF * hf, HALF * (hf + 1))
            dsink = jnp.zeros((HALF, 128), F32)
            for h in range(N_HEADS):
                grows = slice(BLK * (h % 4) + HALF * hf, BLK * (h % 4) + HALF * (hf + 1))
                dp = s["lg"][h // 4, grows, :]
                p = s["p"][h, rows, :]
                s["pb"][h, rows, :] = p.astype(BF16)
                rs = jnp.sum(p * dp, axis=1, keepdims=True)
                dl = p * (dp - rs)
                dl_ref[h, rows, :] += dl
                dsink = dsink + jnp.where(lane == h, -s["psink"][rows, :] * rs, 0.0)
                s["dls"][h, rows, :] = (dl * (HEAD_DIM ** -0.5)).astype(BF16)
            dsink_ref[rows, :] += dsink
        for g in range(2):
            dq = _dot(_pair_rows(s["dls"], g), s["km"][g])
            dproj_ref[:, 256 * g:256 * g + 128] = dq[0:BLK]
            dproj_ref[:, 256 * g + 128:256 * g + 256] = dq[BLK:2 * BLK]
        lo_k = _lane_lo(2 * BLK)
        for col, lhs, rhs in ((0, "dls", "qm"), (KV_W, "pb", "dom")):
            raw = [_dot_tn(_group_rows(s[lhs], g), _group_rows(s[rhs], g)) for g in range(2)]
            both = [r + pltpu.roll(r, 64, 1) for r in raw]
            dkv = jnp.where(lo_k, both[0], both[1])
            dproj_ref[:, ATTN_W + col:ATTN_W + col + KV_W] = dkv[BLK:2 * BLK]
            dkvn_ref[:, col:col + KV_W] = dkv[0:BLK]

        for j in range(4):
            cols = slice(128 * j, 128 * (j + 1))
            dm2 = _stack_halves(s["dmsb"][:, cols])
            vnb = s["vnb"][:, cols]
            dws2 = _dot_nt(dm2, vnb)
            dws_ref[2 * j] += dws2[0:BLK]
            dws_ref[2 * j + 1] += dws2[BLK:2 * BLK]
            s["dvn"][:, cols] = _dot(wst2_ref[j], dm2)
        for r0 in range(0, BLK, ROWS):
            rows = slice(r0, r0 + ROWS)
            dvn, vhat = s["dvn"][rows, :], s["vhat"][rows, :]
            dlng_ref[...] += _rowsum8(dvn * vhat)
            dlnb_ref[...] += _rowsum8(dvn)
            dvh = dvn * lng
            dact = s["rstd"][rows, :] * (dvh - _seg_mean64(dvh) - vhat * _seg_mean64(dvh * vhat))
            dproj_ref[rows, c_gu + GMLP_W:IN_W] = dact * s["da"][rows, :]

    acc8 = lambda w: jax.ShapeDtypeStruct((8, w), F32)
    out_shape = [jax.ShapeDtypeStruct((t, IN_W), F32), jax.ShapeDtypeStruct((t, 2 * KV_W), F32),
                 jax.ShapeDtypeStruct((N_HEADS, BLK, 2 * BLK), F32), jax.ShapeDtypeStruct((BLK, 128), F32),
                 acc8(GMLP_W), acc8(GMLP_W), jax.ShapeDtypeStruct((N_GROUPS, BLK, BLK), F32),
                 jax.ShapeDtypeStruct((BLK, GMLP_W), F32), acc8(ATTN_W), acc8(GMLP_W)]
    out_specs = [pl.BlockSpec((BLK, IN_W), lambda n: (n, 0)),
                 pl.BlockSpec((BLK, 2 * KV_W), lambda n: ((n + nb - 1) % nb, 0)),
                 _full((N_HEADS, BLK, 2 * BLK)), _full((BLK, 128)), _full((8, GMLP_W)), _full((8, GMLP_W)),
                 _full((N_GROUPS, BLK, BLK)), _full((BLK, GMLP_W)), _full((8, ATTN_W)), _full((8, GMLP_W))]
    in_specs = _mix_specs(False) + [_full((N_GROUPS // 2, BLK, 2 * BLK)),
                               pl.BlockSpec((BLK, D_MODEL), lambda n: (n, 0)),
                               _full((D_MODEL, D_MODEL))] + _kept_specs()
    return _call(
        body, name="mix_bwd", grid=(nb,), out_shape=out_shape, in_specs=in_specs, out_specs=out_specs,
        scratch_shapes=shapes, sem=("arbitrary",), comm=comm,
        args=(proj, proj, lng, lnb, ws2, bfull, aog, gog, wst2, dy, w_out, *kept))


def _outproj(mixed, w_out, x, g1, ln1g, ln1b, sc2, sh2, tm, comm):
    t, d = x.shape

    def body(mx_ref, w_ref, x_ref, g1_ref, lg_ref, lb_ref, sc_ref, sh_ref, y_ref, x1_ref, h2_ref):
        y = _dot(mx_ref[...], w_ref[...])
        xhat, _ = _ln_stats(ALPHA * x_ref[...] + g1_ref[...] * y)
        x1 = xhat * lg_ref[...] + lb_ref[...]
        y_ref[...] = y
        x1_ref[...] = x1
        h2_ref[...] = (x1 * (1.0 + sc_ref[...]) + sh_ref[...]).astype(BF16)

    row = pl.BlockSpec((tm, d), lambda i: (i, 0))
    vec = _full((1, d))
    return _call(
        body, name="outproj", grid=(t // tm,),
        out_shape=[jax.ShapeDtypeStruct((t, d), F32), jax.ShapeDtypeStruct((t, d), F32),
                   jax.ShapeDtypeStruct((t, d), BF16)],
        in_specs=[row, _full((d, d)), row, vec, vec, vec, vec, vec], out_specs=[row, row, row],
        sem=("parallel",), comm=comm, args=(mixed, w_out, x, g1, ln1g, ln1b, sc2, sh2))


def _ffn_fwd(h2, w_gu_t, w_down, x1, target, g2, ln2g, ln2b, tm):
    t, d = x1.shape

    def body(h_ref, w_ref, wd_ref, x1_ref, tg_ref, g2_ref, lg_ref, lb_ref,
             dsu_ref, sg_ref, act_ref, dz_ref, dy_ref, loss_ref, dlg_ref, dlb_ref, dg2_ref):
        @pl.when(pl.program_id(0) == 0)
        def _():
            for r in (loss_ref, dlg_ref, dlb_ref, dg2_ref):
                r[...] = jnp.zeros_like(r)

        h = h_ref[...]
        g = _dot_nt(h, w_ref[0:D_FF])
        u = _dot_nt(h, w_ref[D_FF:2 * D_FF])
        s = _sigmoid(g)
        sg = g * s
        act = (sg * u).astype(BF16)
        dsu_ref[...] = (u * (s * (1.0 + g * (1.0 - s)))).astype(BF16)
        sg_ref[...] = sg.astype(BF16)
        act_ref[...] = act
        y2 = _dot(act, wd_ref[...])
        g2 = g2_ref[...]
        lg = lg_ref[...]
        xhat, rstd = _ln_stats(ALPHA * x1_ref[...] + g2 * y2)
        err = xhat * lg + lb_ref[...] - tg_ref[...]
        loss_ref[...] += _rowsum8(err * err)
        dx2 = err * (1.0 / d)
        dlg_ref[...] += _rowsum8(dx2 * xhat)
        dlb_ref[...] += _rowsum8(dx2)
        dz = _ln_bwd(dx2 * lg, xhat, rstd)
        dg2_ref[...] += _rowsum8(dz * y2)
        dz_ref[...] = dz
        dy_ref[...] = (g2 * dz).astype(BF16)

    row = pl.BlockSpec((tm, d), lambda i: (i, 0))
    wide = pl.BlockSpec((tm, D_FF), lambda i: (i, 0))
    vec = _full((1, d))
    acc = _full((8, d))
    acc_shape = jax.ShapeDtypeStruct((8, d), F32)
    wide_shape = jax.ShapeDtypeStruct((t, D_FF), BF16)
    return pl.pallas_call(
        body, name="ffn_fwd", grid=(t // tm,),
        out_shape=[wide_shape] * 3 + [jax.ShapeDtypeStruct((t, d), F32), jax.ShapeDtypeStruct((t, d), BF16)]
        + [acc_shape] * 4,
        in_specs=[row, _resident((2 * D_FF, d)), _resident((D_FF, d)), row, row, vec, vec, vec],
        out_specs=[wide] * 3 + [row, row, acc, acc, acc, acc], compiler_params=_params(("arbitrary",)),
    )(h2, w_gu_t, w_down, x1, target, g2, ln2g, ln2b)


def _resident(shape):
    nd = len(shape)
    return pl.BlockSpec(shape, lambda *_: (0,) * nd, pipeline_mode=pl.Buffered(1))


def _ffn_bwd(dy2, w_down, dsu, sg, w_gu_t, x1, x, y, dz2, sc2, g1, ln1g, tm, comm):
    t, d = x1.shape

    def body(dy2_ref, wd_ref, dsu_ref, sg_ref, w_ref, x1_ref, x_ref, y_ref, dz2_ref, sc_ref, g1_ref, lg_ref,
             dg_ref, du_ref, dz1_ref, dy_ref, dsc_ref, dsh_ref, dlg_ref, dlb_ref, dg1_ref):
        @pl.when(pl.program_id(0) == 0)
        def _():
            for r in (dsc_ref, dsh_ref, dlg_ref, dlb_ref, dg1_ref):
                r[...] = jnp.zeros_like(r)

        dact = _dot_nt(dy2_ref[...], wd_ref[...])
        dg = (dact * dsu_ref[...].astype(F32)).astype(BF16)
        du = (dact * sg_ref[...].astype(F32)).astype(BF16)
        dg_ref[...] = dg
        du_ref[...] = du
        dh2 = _dot(dg, w_ref[0:D_FF]) + _dot(du, w_ref[D_FF:2 * D_FF])
        x1 = x1_ref[...]
        y = y_ref[...]
        g1 = g1_ref[...]
        dsc_ref[...] += _rowsum8(dh2 * x1)
        dsh_ref[...] += _rowsum8(dh2)
        dx1 = dh2 * (1.0 + sc_ref[...]) + ALPHA * dz2_ref[...]
        xhat, rstd = _ln_stats(ALPHA * x_ref[...] + g1 * y)
        dlg_ref[...] += _rowsum8(dx1 * xhat)
        dlb_ref[...] += _rowsum8(dx1)
        dz1 = _ln_bwd(dx1 * lg_ref[...], xhat, rstd)
        dg1_ref[...] += _rowsum8(dz1 * y)
        dz1_ref[...] = dz1
        dy_ref[...] = (g1 * dz1).astype(BF16)

    row = pl.BlockSpec((tm, d), lambda i: (i, 0))
    wide = pl.BlockSpec((tm, D_FF), lambda i: (i, 0))
    vec = _full((1, d))
    acc = _full((8, d))
    acc_shape = jax.ShapeDtypeStruct((8, d), F32)
    wide_shape = jax.ShapeDtypeStruct((t, D_FF), BF16)
    return _call(
        body, name="ffn_bwd", grid=(t // tm,),
        out_shape=[wide_shape, wide_shape, jax.ShapeDtypeStruct((t, d), F32), jax.ShapeDtypeStruct((t, d), BF16)]
        + [acc_shape] * 5,
        in_specs=[row, _resident((D_FF, d)), wide, wide, _resident((2 * D_FF, d)), row, row, row, row, vec, vec, vec],
        out_specs=[wide, wide, row, row, acc, acc, acc, acc, acc], sem=("arbitrary",), comm=comm,
        args=(dy2, w_down, dsu, sg, w_gu_t, x1, x, y, dz2, sc2, g1, ln1g))


def _din(dproj, dkvn, w_in_t, x, dz1, sc1, tm, comm):
    t, d = x.shape

    def body(dp_ref, dkv_ref, w_ref, x_ref, dz1_ref, sc_ref, dx_ref, dpb_ref, dbin_ref, dsc_ref, dsh_ref):
        @pl.when(pl.program_id(0) == 0)
        def _():
            for r in (dbin_ref, dsc_ref, dsh_ref):
                r[...] = jnp.zeros_like(r)

        dp = jnp.concatenate([dp_ref[:, 0:ATTN_W], dp_ref[:, ATTN_W:ATTN_W + 2 * KV_W] + dkv_ref[...],
                              dp_ref[:, ATTN_W + 2 * KV_W:IN_W]], axis=1)
        dbin_ref[...] += _rowsum8(dp)
        dpb = dp.astype(BF16)
        dpb_ref[...] = dpb
        dh = _dot(dpb, w_ref[...])
        dsc_ref[...] += _rowsum8(dh * x_ref[...])
        dsh_ref[...] += _rowsum8(dh)
        dx_ref[...] = dh * (1.0 + sc_ref[...]) + ALPHA * dz1_ref[...]

    row = lambda w: pl.BlockSpec((tm, w), lambda i: (i, 0))
    return _call(
        body, name="din", grid=(t // tm,),
        out_shape=[jax.ShapeDtypeStruct((t, d), F32), jax.ShapeDtypeStruct((t, IN_W), BF16),
                   jax.ShapeDtypeStruct((8, IN_W), F32), jax.ShapeDtypeStruct((8, d), F32),
                   jax.ShapeDtypeStruct((8, d), F32)],
        in_specs=[row(IN_W), row(2 * KV_W), _full((IN_W, d)), row(d), row(d), _full((1, d))],
        out_specs=[row(d), row(IN_W), _full((8, IN_W)), _full((8, d)), _full((8, d))],
        sem=("arbitrary",), comm=comm, args=(dproj, dkvn, w_in_t, x, dz1, sc1))


def _wgrad(name, a, b, tmm, tk, comm=None, a2=None):
    t, m = a.shape
    n = b.shape[1]
    nk = t // tk
    nm = m // tmm

    def body(*refs):
        a_refs, (b_ref, o_ref, acc_ref) = refs[:-3], refs[-3:]
        i, k = pl.program_id(0), pl.program_id(1)

        @pl.when(k == 0)
        def _():
            acc_ref[...] = jnp.zeros_like(acc_ref)

        a_tile = a_refs[0][...] if a2 is None else jnp.where(i < nm, a_refs[0][...], a_refs[1][...])
        acc_ref[...] += _dot_tn(a_tile, b_ref[...])

        @pl.when(k == nk - 1)
        def _():
            o_ref[...] = acc_ref[...].astype(BF16)

    if a2 is None:
        a_specs, a_args, n_tiles = [pl.BlockSpec((tk, tmm), lambda i, k: (k, i))], (a,), nm
    else:
        a_specs = [pl.BlockSpec((tk, tmm), lambda i, k: (jnp.where(i < nm, k, 0), jnp.minimum(i, nm - 1))),
                   pl.BlockSpec((tk, tmm), lambda i, k: (jnp.where(i < nm, 0, k), jnp.maximum(i - nm, 0)))]
        a_args, n_tiles = (a, a2), 2 * nm
    (out,), got = _call(
        body, name=name, grid=(n_tiles, nk), out_shape=[jax.ShapeDtypeStruct((n_tiles * tmm, n), BF16)],
        in_specs=a_specs + [pl.BlockSpec((tk, n), lambda i, k: (k, 0))],
        out_specs=[pl.BlockSpec((tmm, n), lambda i, k: (i, 0))],
        scratch_shapes=[pltpu.VMEM((tmm, n), F32)], sem=("parallel", "arbitrary"), comm=comm, args=a_args + (b,))
    return out if comm is None else (out, got)


def _adamw(w, g, m, v):
    m = ADAM_B1 * m + (1.0 - ADAM_B1) * g
    v = ADAM_B2 * v + (1.0 - ADAM_B2) * (g * g)
    m_hat = m / (1.0 - ADAM_B1 ** ADAM_STEP)
    v_hat = v / (1.0 - ADAM_B2 ** ADAM_STEP)
    delta = -ADAM_LR * (m_hat / (jnp.sqrt(v_hat) + ADAM_EPS) + ADAM_WD * w)
    return delta, m, v


def _adam_reduce(name, parts, w, m, v, tr, comm=None):
    r, cdim = w.shape

    def body(p_ref, w_ref, m_ref, v_ref, g_ref, d_ref, mo_ref, vo_ref):
        g = p_ref[0].astype(F32)
        for s in range(1, N_DEV):
            g = g + p_ref[s].astype(F32)
        d_ref[...], mo_ref[...], vo_ref[...] = _adamw(w_ref[...], g, m_ref[...], v_ref[...])
        g_ref[...] = g

    tile = pl.BlockSpec((tr, cdim), lambda i: (i, 0))
    shp = jax.ShapeDtypeStruct((r, cdim), F32)
    res, got = _call(
        body, name=name, grid=(r // tr,), out_shape=[shp] * 4,
        in_specs=[pl.BlockSpec((N_DEV, tr, cdim), lambda i: (0, i, 0)), tile, tile, tile],
        out_specs=[tile] * 4, sem=("parallel",), comm=comm, args=(parts, w, m, v))
    return res if comm is None else (res, got)


def _adam_w_ada(c_all_t, dmod_cols, w, m, v):
    def body(ct_ref, dm_ref, w_ref, m_ref, v_ref, g_ref, d_ref, mo_ref, vo_ref):
        ct = ct_ref[...]
        s = (ct * _sigmoid(ct)).astype(BF16)
        g = _dot(s, dm_ref[...].astype(BF16))
        d_ref[...], mo_ref[...], vo_ref[...] = _adamw(w_ref[...], g, m_ref[...], v_ref[...])
        g_ref[...] = g

    shp = jax.ShapeDtypeStruct(w.shape, F32)
    return pl.pallas_call(
        body, name="adam_w_ada", grid=(1,), out_shape=[shp] * 4,
        in_specs=[_full(c_all_t.shape), _full(dmod_cols.shape)] + [_full(w.shape)] * 3,
        out_specs=[_full(w.shape)] * 4, compiler_params=_params(("arbitrary",)),
    )(c_all_t, dmod_cols, w, m, v)


SMALL_EARLY = ["rel_bias", "attn_sinks", "gmlp_ln_g", "gmlp_ln_b", "gmlp_w_s", "gmlp_b_s",
               "attn_out_g", "gmlp_out_g", "ln1_g", "ln1_b", "ln2_g", "ln2_b"]
SMALL_LATE = ["b_ada", "b_in", "loss"]
WEIGHTS = ["rel_bias", "w_ada", "b_ada", "w_in", "b_in", "attn_sinks", "gmlp_ln_g", "gmlp_ln_b", "gmlp_w_s",
           "gmlp_b_s", "attn_out_g", "gmlp_out_g", "w_out", "ln1_g", "ln1_b", "w_gate_up", "w_down", "ln2_g", "ln2_b"]


def _seg_rows(nelem):
    return -(-nelem // 1024) * 8


def _pack(named, names):
    parts = []
    for name in names:
        flat = named[name].reshape(-1).astype(F32)
        rows = _seg_rows(flat.shape[0])
        parts.append(jnp.pad(flat, (0, rows * 128 - flat.shape[0])).reshape(rows, 128))
    return jnp.concatenate(parts, axis=0)


def _adam_small(name, parts, names, wts, mom_m, mom_v):
    params = [n for n in names if n in wts]

    def view(n):
        nelem = math.prod(wts[n].shape)
        return (nelem // 128, 128) if nelem % 128 == 0 else (1, nelem)

    offsets, r0 = {}, 0
    for n in names:
        offsets[n] = r0
        r0 += _seg_rows(math.prod(wts[n].shape) if n in wts else 1)

    def body(*refs):
        p_ref, ins, outs = refs[0], refs[1:1 + 3 * len(params)], refs[1 + 3 * len(params):]

        def total(n, rows, lanes):
            o = offsets[n]
            g = p_ref[0, o:o + rows, 0:lanes]
            for s in range(1, N_DEV):
                g = g + p_ref[s, o:o + rows, 0:lanes]
            return g

        for i, n in enumerate(params):
            g = total(n, *view(n))
            w_ref, m_ref, v_ref = ins[3 * i:3 * i + 3]
            g_ref, d_ref, mo_ref, vo_ref = outs[4 * i:4 * i + 4]
            d_ref[...], mo_ref[...], vo_ref[...] = _adamw(w_ref[...], g, m_ref[...], v_ref[...])
            g_ref[...] = g
        for j, n in enumerate(n for n in names if n not in wts):
            outs[4 * len(params) + j][...] = total(n, 8, 128)

    args, in_specs, out_shape = [parts], [_full(parts.shape)], []
    for n in params:
        args += [t[n].reshape(view(n)) for t in (wts, mom_m, mom_v)]
        in_specs += [_full(view(n))] * 3
        out_shape += [jax.ShapeDtypeStruct(view(n), F32)] * 4
    out_shape += [jax.ShapeDtypeStruct((8, 128), F32) for n in names if n not in wts]
    res = pl.pallas_call(
        body, name=name, grid=(1,), out_shape=out_shape, in_specs=in_specs,
        out_specs=[_full(s.shape) for s in out_shape], compiler_params=_params(("arbitrary",)),
    )(*args)
    done = {n: tuple(r.reshape(wts[n].shape) for r in res[4 * i:4 * i + 4]) for i, n in enumerate(params)}
    sums = {n: res[4 * len(params) + j] for j, n in enumerate(n for n in names if n not in wts)}
    return done, sums


def _t5_bucket_map():
    qi = jnp.arange(BLK)[:, None]
    si = jnp.arange(2 * BLK)[None, :]
    n = jnp.maximum(qi + BLK - si, 0)
    max_exact = N_BUCKETS // 2
    nf = jnp.maximum(n, max_exact).astype(F32)
    large = max_exact + (jnp.log(nf / max_exact) / math.log(MAX_DISTANCE / max_exact)
                         * (N_BUCKETS - max_exact)).astype(jnp.int32)
    large = jnp.minimum(large, N_BUCKETS - 1)
    return jnp.where(n < max_exact, n, large).astype(jnp.int32)


def kernel(x, c, rel_bias, w_ada, b_ada, w_in, b_in, attn_sinks, gmlp_ln_g, gmlp_ln_b, gmlp_w_s, gmlp_b_s, attn_out_g, gmlp_out_g, w_out, ln1_g, ln1_b, w_gate_up, w_down, ln2_g, ln2_b, loss_target, m_rel_bias, m_w_ada, m_b_ada, m_w_in, m_b_in, m_attn_sinks, m_gmlp_ln_g, m_gmlp_ln_b, m_gmlp_w_s, m_gmlp_b_s, m_attn_out_g, m_gmlp_out_g, m_w_out, m_ln1_g, m_ln1_b, m_w_gate_up, m_w_down, m_ln2_g, m_ln2_b, v_rel_bias, v_w_ada, v_b_ada, v_w_in, v_b_in, v_attn_sinks, v_gmlp_ln_g, v_gmlp_ln_b, v_gmlp_w_s, v_gmlp_b_s, v_attn_out_g, v_gmlp_out_g, v_w_out, v_ln1_g, v_ln1_b, v_w_gate_up, v_w_down, v_ln2_g, v_ln2_b):
    wts = dict(rel_bias=rel_bias, w_ada=w_ada, b_ada=b_ada, w_in=w_in, b_in=b_in, attn_sinks=attn_sinks,
               gmlp_ln_g=gmlp_ln_g, gmlp_ln_b=gmlp_ln_b, gmlp_w_s=gmlp_w_s, gmlp_b_s=gmlp_b_s,
               attn_out_g=attn_out_g, gmlp_out_g=gmlp_out_g, w_out=w_out, ln1_g=ln1_g, ln1_b=ln1_b,
               w_gate_up=w_gate_up, w_down=w_down, ln2_g=ln2_g, ln2_b=ln2_b)
    mom_m = dict(rel_bias=m_rel_bias, w_ada=m_w_ada, b_ada=m_b_ada, w_in=m_w_in, b_in=m_b_in,
                 attn_sinks=m_attn_sinks, gmlp_ln_g=m_gmlp_ln_g, gmlp_ln_b=m_gmlp_ln_b, gmlp_w_s=m_gmlp_w_s,
                 gmlp_b_s=m_gmlp_b_s, attn_out_g=m_attn_out_g, gmlp_out_g=m_gmlp_out_g, w_out=m_w_out,
                 ln1_g=m_ln1_g, ln1_b=m_ln1_b, w_gate_up=m_w_gate_up, w_down=m_w_down, ln2_g=m_ln2_g,
                 ln2_b=m_ln2_b)
    mom_v = dict(rel_bias=v_rel_bias, w_ada=v_w_ada, b_ada=v_b_ada, w_in=v_w_in, b_in=v_b_in,
                 attn_sinks=v_attn_sinks, gmlp_ln_g=v_gmlp_ln_g, gmlp_ln_b=v_gmlp_ln_b, gmlp_w_s=v_gmlp_w_s,
                 gmlp_b_s=v_gmlp_b_s, attn_out_g=v_attn_out_g, gmlp_out_g=v_gmlp_out_g, w_out=v_w_out,
                 ln1_g=v_ln1_g, ln1_b=v_ln1_b, w_gate_up=v_w_gate_up, w_down=v_w_down, ln2_g=v_ln2_g,
                 ln2_b=v_ln2_b)

    t = x.shape[1]
    tm = min(512, t)
    tn_ff = D_FF // 2
    tk_long, tk_short = min(4096, t), min(2048, t)
    me = 4 * lax.axis_index("x") + 2 * lax.axis_index("y") + lax.axis_index("c")
    xs = x[0]
    target = loss_target[0]

    c_g, w_in_g = _exchange("gather_in", [jnp.broadcast_to(c, (8, D_MODEL)), w_in[0].T.astype(BF16)],
                            ("gather", "gather2"))
    c_all = c_g[:, 0, :]
    w_in_t = w_in_g.reshape(IN_W, D_MODEL)

    ncol = w_ada.shape[2]
    b_cols = lax.dynamic_slice(b_ada, (0, me * ncol), (1, ncol))
    mod_part = _mod_partial(c_all, w_ada[0], b_cols)
    (mod_g,) = _exchange("gather_mod", [mod_part], ("gather",))
    mod = lax.dynamic_slice(mod_g, (0, me, 0), (N_DEV, 1, ncol)).reshape(1, N_DEV * ncol)
    sh1, sc1, g1, sh2, sc2, g2 = [mod[:, i * D_MODEL:(i + 1) * D_MODEL] for i in range(6)]

    bucket = _t5_bucket_map()
    (bias,), _ = _bias_table(rel_bias, bucket, comm=None)
    causal = jnp.tril(jnp.ones((BLK, BLK), dtype=bool))
    ws = jnp.where(causal[None], gmlp_w_s[0], 0.0).astype(BF16)
    pair = lambda w: jnp.concatenate([w[0::2], w[1::2]], axis=2)
    ws2, wst2 = pair(ws), pair(jnp.swapaxes(ws, 1, 2))
    bfull = jnp.repeat(gmlp_b_s[0].T, GMLP_W // N_GROUPS, axis=1)
    sinks = attn_sinks[0]

    (proj, h1), (w_down_g,) = _inproj(xs, sc1, sh1, w_in_t, b_in, tm, comm=([w_down[0].astype(BF16)], ("gather2",)))
    (mixed, *kept), (w_out_g, w_gu_g) = _mix_fwd(
        proj, bias, sinks, gmlp_ln_g, gmlp_ln_b, ws2, bfull, attn_out_g, gmlp_out_g,
        comm=([w_out[0].astype(BF16), w_gate_up[0].T.astype(BF16)], ("gather2", "gather2")))
    w_out_f = w_out_g.reshape(D_MODEL, D_MODEL)
    w_gu_t = w_gu_g.reshape(2 * D_FF, D_MODEL)
    (y1, x1, h2), _ = _outproj(mixed, w_out_f, xs, g1, ln1_g, ln1_b, sc2, sh2, tm, comm=None)
    w_down_f = w_down_g.reshape(D_FF, D_MODEL)
    dsu, sg, act, dz2, dy2, loss_p, d_ln2g, d_ln2b, d_g2 = _ffn_fwd(h2, w_gu_t, w_down_f, x1, target, g2, ln2_g, ln2_b,
                                                                    min(256, t))

    slots = lambda a: a.reshape(N_DEV, -1, D_MODEL)
    dw_down = _wgrad("wgrad_down", act, dy2, tn_ff, tk_short)
    (dgate, dup, dz1, dy1, d_sc2, d_sh2, d_ln1g, d_ln1b, d_g1), (r_down,) = _ffn_bwd(
        dy2, w_down_f, dsu, sg, w_gu_t, x1, xs, y1, dz2, sc2, g1, ln1_g, min(256, t),
        comm=([slots(dw_down)], ("scatter",)))
    dw_gu_t = _wgrad("wgrad_gate_up", dgate, h2, tn_ff, tk_short, a2=dup)
    dw_out = _wgrad("wgrad_out", mixed, dy1, D_MODEL, tk_long)
    ((dproj, dkvn, dl_acc, dsink_acc, d_lng, d_lnb, d_ws, d_bs, d_aog, d_gog), (r_gu, r_out)) = _mix_bwd(
        proj, gmlp_ln_g, gmlp_ln_b, ws2, wst2, bfull, attn_out_g, gmlp_out_g, dy1, w_out_f.T, kept,
        comm=([slots(dw_gu_t), slots(dw_out)], ("scatter", "scatter")))
    d_relb = _bias_grad(dl_acc, bucket)

    rsum = lambda a: jnp.sum(a, axis=0)
    early_g = dict(
        rel_bias=d_relb[:, 0, :N_BUCKETS].T, attn_sinks=rsum(dsink_acc)[:N_HEADS],
        gmlp_ln_g=rsum(d_lng), gmlp_ln_b=rsum(d_lnb), gmlp_w_s=jnp.where(causal[None], d_ws, 0.0),
        gmlp_b_s=jnp.sum(d_bs.reshape(BLK, N_GROUPS, GMLP_W // N_GROUPS), axis=2).T,
        attn_out_g=rsum(d_aog), gmlp_out_g=rsum(d_gog), ln1_g=rsum(d_ln1g), ln1_b=rsum(d_ln1b),
        ln2_g=rsum(d_ln2g), ln2_b=rsum(d_ln2b))
    (grad_x, dproj_b, d_bin, d_sc1, d_sh1), _ = _din(dproj, dkvn, w_in_t, xs, dz1, sc1, tm, comm=None)
    dw_in_t, (early_all,) = _wgrad("wgrad_in", dproj_b, h1, IN_W // 2, tk_long,
                                   comm=([_pack(early_g, SMALL_EARLY)], ("gather2",)))
    dmod = jnp.concatenate([rsum(d_sh1), rsum(d_sc1), rsum(d_g1), rsum(d_sh2), rsum(d_sc2), rsum(d_g2)])
    late_g = dict(b_ada=dmod, b_in=rsum(d_bin), loss=(0.5 / D_MODEL * jnp.sum(loss_p)).reshape(1))
    late_all, r_in = _exchange("scatter_in", [_pack(late_g, SMALL_LATE), slots(dw_in_t)], ("gather", "scatter"))

    small, _ = _adam_small("adam_small_early", early_all, SMALL_EARLY, wts, mom_m, mom_v)
    small_late, sums = _adam_small("adam_small_late", late_all, SMALL_LATE, wts, mom_m, mom_v)
    small.update(small_late)
    loss = sums["loss"][0, 0]

    dmod_all = late_all[:, :_seg_rows(6 * D_MODEL), :].reshape(N_DEV, 6 * D_MODEL)
    dmod_cols = lax.dynamic_slice(dmod_all, (0, me * ncol), (N_DEV, ncol))
    kpad = 128 - N_DEV
    ada = _adam_w_ada(jnp.pad(c_all.T, ((0, 0), (0, kpad))), jnp.pad(dmod_cols, ((0, kpad), (0, 0))),
                      w_ada[0], m_w_ada[0], v_w_ada[0])

    tr = lambda a: jnp.swapaxes(a, -1, -2)
    big = {}
    big["w_in"] = [tr(o)[None] for o in _adam_reduce("adam_w_in", r_in, w_in[0].T, m_w_in[0].T, v_w_in[0].T, 112)]
    big["w_out"] = [o[None] for o in _adam_reduce("adam_w_out", r_out, w_out[0], m_w_out[0], v_w_out[0], 128)]
    big["w_gate_up"] = [tr(o)[None] for o in _adam_reduce("adam_w_gu", r_gu, w_gate_up[0].T, m_w_gate_up[0].T,
                                                           v_w_gate_up[0].T, 352)]
    big["w_down"] = [o[None] for o in _adam_reduce("adam_w_down", r_down, w_down[0], m_w_down[0], v_w_down[0], 176)]
    big["w_ada"] = [o[None] for o in ada]

    outs = [[], [], [], []]
    for name in WEIGHTS:
        for i in range(4):
            outs[i].append(big[name][i] if name in big else small[name][i])
    return (loss, grad_x[None], *outs[0], *outs[1], *outs[2], *outs[3])
```

```python
import math

import jax
import jax.numpy as jnp
from jax import lax
from jax.experimental import pallas as pl
from jax.experimental.pallas import tpu as pltpu

F32 = jnp.float32
BF16 = jnp.bfloat16
MESH = pl.DeviceIdType.MESH

N_DEV = 8
D_MODEL = 1024
HEAD_DIM = 64
N_HEADS = 8
N_GROUPS = 8
ATTN_W = 512
KV_W = 128
GMLP_W = 512
IN_W = 1792
BLK = 128
N_BUCKETS = 32
MAX_DISTANCE = 128
D_FF = 2816
ALPHA = 2.0 ** 0.25
LN_EPS = 1e-5
NEG_INF = -1e30
ADAM_LR = 0.001
ADAM_B1 = 0.9
ADAM_B2 = 0.999
ADAM_EPS = 1e-08
ADAM_WD = 0.01
ADAM_STEP = 10
GELU_C0 = math.sqrt(2.0 / math.pi)
GELU_C1 = 0.044715

VMEM_LIMIT = 56 * 1024 * 1024


def _params(sem):
    return pltpu.CompilerParams(dimension_semantics=sem, vmem_limit_bytes=VMEM_LIMIT)


def _dot(a, b):
    return lax.dot_general(a, b, (((1,), (0,)), ((), ())), preferred_element_type=F32)


def _dot_nt(a, b):
    return lax.dot_general(a, b, (((1,), (1,)), ((), ())), preferred_element_type=F32)


def _dot_tn(a, b):
    return lax.dot_general(a, b, (((0,), (0,)), ((), ())), preferred_element_type=F32)


def _full(shape):
    nd = len(shape)
    return pl.BlockSpec(shape, lambda *_: (0,) * nd)


def _rowsum8(v):
    r, c = v.shape
    return jnp.sum(v.reshape(r // 8, 8, c), axis=0)


def _sigmoid(v):
    return 1.0 / (1.0 + jnp.exp(-v))


def _gelu_parts(v):
    v2 = v * v
    t = jnp.tanh(GELU_C0 * (v + GELU_C1 * v * v2))
    g = 0.5 * v * (1.0 + t)
    dg = 0.5 * (1.0 + t) + 0.5 * v * (1.0 - t * t) * (GELU_C0 * (1.0 + 3.0 * GELU_C1 * v2))
    return g, dg


def _ln_stats(z):
    mu = jnp.mean(z, axis=1, keepdims=True)
    zc = z - mu
    var = jnp.mean(zc * zc, axis=1, keepdims=True)
    rstd = lax.rsqrt(var + LN_EPS)
    return zc * rstd, rstd


def _ln_bwd(dxhat, xhat, rstd):
    m1 = jnp.mean(dxhat, axis=1, keepdims=True)
    m2 = jnp.mean(dxhat * xhat, axis=1, keepdims=True)
    return rstd * (dxhat - m1 - xhat * m2)


def _seg_mean64(v):
    r = v.shape[0]
    lo = lax.broadcasted_iota(jnp.int32, (r, 128), 1) < 64
    outs = []
    for j in range(v.shape[1] // 128):
        ch = v[:, 128 * j:128 * (j + 1)]
        s_lo = jnp.sum(jnp.where(lo, ch, 0.0), axis=1, keepdims=True)
        s_hi = jnp.sum(jnp.where(lo, 0.0, ch), axis=1, keepdims=True)
        outs.append(jnp.where(lo, s_lo, s_hi) * (1.0 / 64.0))
    return jnp.concatenate(outs, axis=1)


def _rms(a, g):
    r = lax.rsqrt(jnp.mean(a * a, axis=1, keepdims=True) + LN_EPS)
    return a * r * g, r


def _rms_bwd(dout, a, r, g):
    t = dout * g
    return r * t - a * (r * r * r) * jnp.mean(t * a, axis=1, keepdims=True)


PEER_ORDER = (1, 2, 4, 3, 5, 6, 7)


def _peer(j):
    x, y, c = lax.axis_index("x"), lax.axis_index("y"), lax.axis_index("c")
    px = 1 - x if j & 4 else x
    py = 1 - y if j & 2 else y
    pc = 1 - c if j & 1 else c
    return (px, py, pc), 4 * px + 2 * py + pc


SIBLING = 1
CHIP_FLIPS = (4, 2, 6)


def _exchange_phase(phase, ins, outs, modes, send_sems, recv_sems, loc_sems):
    me = 4 * lax.axis_index("x") + 2 * lax.axis_index("y") + lax.axis_index("c")
    for k, mode in enumerate(modes):
        def copy(i, src, slot, dev, k=k):
            return pltpu.make_async_remote_copy(src_ref=src, dst_ref=outs[k].at[slot], send_sem=send_sems.at[k, i],
                                                recv_sem=recv_sems.at[k, i], device_id=dev, device_id_type=MESH)

        src_me = ins[k].at[me] if mode == "scatter" else ins[k]
        local = pltpu.make_async_copy(src_me, outs[k].at[me], loc_sems.at[k])
        if mode == "gather2":
            sib_dev, sib_idx = _peer(SIBLING)
            chips = [_peer(j) for j in CHIP_FLIPS]
            far = [_peer(j | SIBLING)[1] for j in CHIP_FLIPS]
            if phase == "start":
                local.start()
                copy(0, ins[k], me, sib_dev).start()
                for i, (dev, _) in enumerate(chips):
                    copy(1 + i, ins[k], me, dev).start()
            elif phase == "mid":
                for i, (dev, idx) in enumerate(chips):
                    copy(1 + i, ins[k], idx, dev).wait_recv()
                    copy(4 + i, outs[k].at[idx], idx, sib_dev).start()
            else:
                copy(0, ins[k], sib_idx, sib_dev).wait_recv()
                for i, slot in enumerate(far):
                    copy(4 + i, ins[k], slot, sib_dev).wait_recv()
                copy(0, ins[k], me, sib_dev).wait_send()
                for i, (dev, idx) in enumerate(chips):
                    copy(1 + i, ins[k], me, dev).wait_send()
                    copy(4 + i, outs[k].at[idx], idx, sib_dev).wait_send()
                local.wait()
            continue
        peers = [_peer(j) for j in PEER_ORDER]
        if phase == "start":
            local.start()
            for i, (dev, idx) in enumerate(peers):
                copy(i, ins[k].at[idx] if mode == "scatter" else ins[k], me, dev).start()
        elif phase == "end":
            for i, (dev, idx) in enumerate(peers):
                copy(i, src_me, idx, dev).wait_recv()
            for i, (dev, idx) in enumerate(peers):
                copy(i, src_me, me, dev).wait_send()
            local.wait()


def _exchange_shapes(arrays, modes):
    return [jax.ShapeDtypeStruct((N_DEV,) + (a.shape[1:] if m == "scatter" else a.shape), a.dtype)
            for a, m in zip(arrays, modes)]


def _exchange_sems(n):
    return [pltpu.SemaphoreType.DMA((n, N_DEV - 1)), pltpu.SemaphoreType.DMA((n, N_DEV - 1)),
            pltpu.SemaphoreType.DMA((n,))]


def _exchange(name, arrays, modes):
    n = len(arrays)

    def body(*refs):
        for phase in ("start", "mid", "end"):
            _exchange_phase(phase, refs[:n], refs[n:2 * n], modes, *refs[2 * n:])

    any_spec = pl.BlockSpec(memory_space=pl.ANY)
    return pl.pallas_call(
        body, name=name, out_shape=_exchange_shapes(arrays, modes),
        in_specs=[any_spec] * n, out_specs=[any_spec] * n, scratch_shapes=_exchange_sems(n),
    )(*arrays)


def _call(body, *, name, grid, in_specs, out_specs, out_shape, args, sem, scratch_shapes=(), comm=None):
    if comm is None:
        outs = pl.pallas_call(body, name=name, grid=grid, in_specs=list(in_specs), out_specs=list(out_specs),
                              out_shape=list(out_shape), scratch_shapes=list(scratch_shapes),
                              compiler_params=_params(sem))(*args)
        return list(outs), []
    arrays, modes = comm
    n_in, n_out, nc, ns = len(in_specs), len(out_specs), len(arrays), len(scratch_shapes)
    n_steps = math.prod(grid)

    def hosted(*refs):
        ins, cins = refs[:n_in], refs[n_in:n_in + nc]
        outs, couts = refs[n_in + nc:n_in + nc + n_out], refs[n_in + nc + n_out:n_in + 2 * nc + n_out]
        scratch = refs[n_in + 2 * nc + n_out:]
        ex = (cins, couts, modes) + tuple(scratch[ns:])
        step = pl.program_id(0)
        for ax in range(1, len(grid)):
            step = step * grid[ax] + pl.program_id(ax)

        @pl.when(step == 0)
        def _():
            _exchange_phase("start", *ex)

        body(*ins, *outs, *scratch[:ns])

        if "gather2" in modes:
            @pl.when(step == (3 * n_steps) // 4)
            def _():
                _exchange_phase("mid", *ex)

        @pl.when(step == n_steps - 1)
        def _():
            _exchange_phase("end", *ex)

    any_spec = pl.BlockSpec(memory_space=pl.ANY)
    res = pl.pallas_call(
        hosted, name=name, grid=grid, in_specs=list(in_specs) + [any_spec] * nc,
        out_specs=list(out_specs) + [any_spec] * nc, out_shape=list(out_shape) + _exchange_shapes(arrays, modes),
        scratch_shapes=list(scratch_shapes) + _exchange_sems(nc),
        compiler_params=_params(tuple("arbitrary" for _ in grid)))(*args, *arrays)
    return list(res[:n_out]), list(res[n_out:])


def _mod_partial(c_all, w_ada, b_ada_cols):
    def body(c_ref, w_ref, b_ref, o_ref):
        cv = c_ref[...]
        s = (cv * _sigmoid(cv)).astype(BF16)
        o_ref[...] = _dot(s, w_ref[...].astype(BF16)) + b_ref[...]

    ncol = w_ada.shape[1]
    return pl.pallas_call(
        body, name="mod_partial", out_shape=jax.ShapeDtypeStruct((N_DEV, ncol), F32),
        in_specs=[_full(c_all.shape), _full(w_ada.shape), _full(b_ada_cols.shape)],
        out_specs=_full((N_DEV, ncol)), grid=(1,), compiler_params=_params(("arbitrary",)),
    )(c_all, w_ada, b_ada_cols)


def _bias_table(rel_bias, bucket, comm):
    def body(rb_ref, bk_ref, o_ref):
        h = pl.program_id(0)
        bk = bk_ref[...]
        acc = jnp.zeros((BLK, 2 * BLK), F32)
        for b in range(N_BUCKETS):
            acc = jnp.where(bk == b, rb_ref[b, h], acc)
        dist = (lax.broadcasted_iota(jnp.int32, (BLK, 2 * BLK), 0) + BLK
                - lax.broadcasted_iota(jnp.int32, (BLK, 2 * BLK), 1))
        o_ref[0] = jnp.where((dist >= 0) & (dist < BLK), acc, NEG_INF)

    return _call(
        body, name="bias_table", out_shape=[jax.ShapeDtypeStruct((N_HEADS, BLK, 2 * BLK), F32)],
        in_specs=[pl.BlockSpec(memory_space=pltpu.SMEM), _full((BLK, 2 * BLK))],
        out_specs=[pl.BlockSpec((1, BLK, 2 * BLK), lambda h: (h, 0, 0))], grid=(N_HEADS,),
        sem=("arbitrary",), comm=comm, args=(rel_bias, bucket))


def _bias_grad(dl_acc, bucket):
    def body(dl_ref, bk_ref, o_ref):
        bk = bk_ref[...]
        dl = dl_ref[0]
        lane = lax.broadcasted_iota(jnp.int32, (1, 128), 1)
        row = jnp.zeros((1, 128), F32)
        for b in range(N_BUCKETS):
            s = jnp.sum(jnp.sum(jnp.where(bk == b, dl, 0.0), axis=1, keepdims=True), axis=0, keepdims=True)
            row = jnp.where(lane == b, s, row)
        o_ref[0] = row

    return pl.pallas_call(
        body, name="bias_grad", out_shape=jax.ShapeDtypeStruct((N_HEADS, 1, 128), F32),
        in_specs=[pl.BlockSpec((1, BLK, 2 * BLK), lambda h: (h, 0, 0)), _full((BLK, 2 * BLK))],
        out_specs=pl.BlockSpec((1, 1, 128), lambda h: (h, 0, 0)), grid=(N_HEADS,),
        compiler_params=_params(("arbitrary",)),
    )(dl_acc, bucket)


def _inproj(x, sc1, sh1, w_in_t, b_in, tm, comm):
    t, d = x.shape
    n = w_in_t.shape[0]

    def body(x_ref, sc_ref, sh_ref, w_ref, b_ref, proj_ref, h_ref):
        h = (x_ref[...] * (1.0 + sc_ref[...]) + sh_ref[...]).astype(BF16)
        h_ref[...] = h
        proj_ref[...] = _dot_nt(h, w_ref[...]) + b_ref[...]

    row = lambda w: pl.BlockSpec((tm, w), lambda i: (i, 0))
    return _call(
        body, name="inproj", grid=(t // tm,),
        out_shape=[jax.ShapeDtypeStruct((t, n), F32), jax.ShapeDtypeStruct((t, d), BF16)],
        in_specs=[row(d), _full((1, d)), _full((1, d)), _full((n, d)), _full((1, n))],
        out_specs=[row(n), row(d)], sem=("parallel",), comm=comm, args=(x, sc1, sh1, w_in_t, b_in))


HALF = 64
ROWS = 32


def _lane_lo(rows):
    return lax.broadcasted_iota(jnp.int32, (rows, 128), 1) < 64


def _mix_stage_kv(proj_ref, kvp_ref, s):
    lo = _lane_lo(2 * BLK)
    for name, col in (("k", ATTN_W), ("v", ATTN_W + KV_W)):
        cur = jnp.concatenate([kvp_ref[:, col - ATTN_W:col - ATTN_W + KV_W], proj_ref[:, col:col + KV_W]], axis=0)
        plain, swapped = cur.astype(BF16), pltpu.roll(cur, 64, 1).astype(BF16)
        zero = jnp.zeros_like(plain)
        for g in range(2):
            dup = jnp.where(lo, plain, swapped) if g == 0 else jnp.where(lo, swapped, plain)
            s[name + "d"][g] = dup
            s[name + "m"][g] = jnp.concatenate([jnp.where(lo, dup, zero), jnp.where(lo, zero, dup)], axis=0)


def _group_rows(ref, g):
    return ref[4 * g:4 * g + 4].reshape(4 * BLK, ref.shape[2])


def _pair_rows(ref, g):
    return jnp.concatenate([jnp.concatenate([ref[4 * g + 2 * c], ref[4 * g + 2 * c + 1]], axis=1) for c in range(2)],
                           axis=0)


def _mask_heads(src_ref, dst_ref):
    lo = _lane_lo(BLK)
    for j in range(4):
        chunk = src_ref[:, 128 * j:128 * (j + 1)]
        dst_ref[2 * j] = jnp.where(lo, chunk, 0.0).astype(BF16)
        dst_ref[2 * j + 1] = jnp.where(lo, 0.0, chunk).astype(BF16)


def _mix_stage_attn(proj_ref, bias_ref, sinks_ref, n, s):
    _mask_heads(proj_ref, s["qm"])
    for g in range(2):
        s["lg"][g] = _dot_nt(_group_rows(s["qm"], g), s["kd"][g])
    n0mask = (n == 0) & (lax.broadcasted_iota(jnp.int32, (HALF, 2 * BLK), 1) < BLK)
    lane = lax.broadcasted_iota(jnp.int32, (HALF, 128), 1)
    for hf in range(BLK // HALF):
        rows = slice(HALF * hf, HALF * (hf + 1))
        psink = jnp.zeros((HALF, 128), F32)
        for h in range(N_HEADS):
            sk = sinks_ref[h]
            grows = slice(BLK * (h % 4) + HALF * hf, BLK * (h % 4) + HALF * (hf + 1))
            logit = s["lg"][h // 4, grows, :] * (HEAD_DIM ** -0.5) + bias_ref[h, rows, :]
            logit = jnp.where(n0mask, NEG_INF, logit)
            m = jnp.maximum(jnp.max(logit, axis=1, keepdims=True), sk)
            e = jnp.exp(logit - m)
            es = jnp.exp(sk - m)
            inv = 1.0 / (jnp.sum(e, axis=1, keepdims=True) + es)
            p = e * inv
            s["p"][h, rows, :] = p
            s["pb"][h, rows, :] = p.astype(BF16)
            psink = jnp.where(lane == h, es * inv, psink)
        s["psink"][rows, :] = psink
    for g in range(2):
        out = _dot(_pair_rows(s["pb"], g), s["vm"][g])
        s["attn"][:, 256 * g:256 * g + 128] = out[0:BLK]
        s["attn"][:, 256 * g + 128:256 * g + 256] = out[BLK:2 * BLK]


def _mix_stage_gmlp_pre(proj_ref, lng, lnb, s, keep):
    c0 = ATTN_W + 2 * KV_W
    for r0 in range(0, BLK, ROWS):
        rows = slice(r0, r0 + ROWS)
        u, du = _gelu_parts(proj_ref[rows, c0:c0 + GMLP_W])
        a, da = _gelu_parts(proj_ref[rows, c0 + GMLP_W:c0 + 2 * GMLP_W])
        ac = a - _seg_mean64(a)
        rstd = lax.rsqrt(_seg_mean64(ac * ac) + LN_EPS)
        vhat = ac * rstd
        s["u"][rows, :] = u
        s["vnb"][rows, :] = (vhat * lng + lnb).astype(BF16)
        if keep:
            s["du"][rows, :] = du
            s["da"][rows, :] = da
            s["vhat"][rows, :] = vhat
            s["rstd"][rows, :] = rstd


def _stack_halves(chunk):
    lo = _lane_lo(BLK)
    zero = jnp.zeros_like(chunk)
    return jnp.concatenate([jnp.where(lo, chunk, zero), jnp.where(lo, zero, chunk)], axis=0)


def _mix_stage_gmlp_mix(ws2_ref, bfull_ref, s):
    for j in range(4):
        cols = slice(128 * j, 128 * (j + 1))
        s["ms"][:, cols] = _dot(ws2_ref[j], _stack_halves(s["vnb"][:, cols])) + bfull_ref[:, cols]


def _mix_scratch(keep):
    f32 = lambda *shape: pltpu.VMEM(shape, F32)
    b16 = lambda *shape: pltpu.VMEM(shape, BF16)
    names = dict(kd=b16(2, 2 * BLK, 128), vd=b16(2, 2 * BLK, 128), km=b16(2, 4 * BLK, 128), vm=b16(2, 4 * BLK, 128),
                 qm=b16(N_HEADS, BLK, 128), lg=f32(2, 4 * BLK, 2 * BLK), pb=b16(N_HEADS, BLK, 2 * BLK),
                 u=f32(BLK, GMLP_W), vnb=b16(BLK, GMLP_W), ms=f32(BLK, GMLP_W))
    if keep:
        names.update(dom=b16(N_HEADS, BLK, 128), dls=b16(N_HEADS, BLK, 2 * BLK),
                     dattn=f32(BLK, ATTN_W), dmix=f32(BLK, D_MODEL), du=f32(BLK, GMLP_W), da=f32(BLK, GMLP_W),
                     vhat=f32(BLK, GMLP_W), rstd=f32(BLK, GMLP_W), dmsb=b16(BLK, GMLP_W), dvn=f32(BLK, GMLP_W))
    return list(names), list(names.values())


def _mix_specs(with_logit_inputs):
    logit_inputs = [_full((N_HEADS, BLK, 2 * BLK)), pl.BlockSpec(memory_space=pltpu.SMEM)] if with_logit_inputs else []
    return [pl.BlockSpec((BLK, IN_W), lambda n: (n, 0)),
            pl.BlockSpec((BLK, 2 * KV_W), lambda n: (jnp.maximum(n - 1, 0), ATTN_W // (2 * KV_W)))] + logit_inputs + [
            _full((1, GMLP_W)), _full((1, GMLP_W)),
            _full((N_GROUPS // 2, BLK, 2 * BLK)), _full((BLK, GMLP_W)),
            _full((1, ATTN_W)), _full((1, GMLP_W))]


KEPT = [("p", (N_HEADS, BLK, 2 * BLK), F32), ("psink", (BLK, 128), F32), ("attn", (BLK, ATTN_W), F32)]


def _kept_shapes(t):
    full = lambda blk: (blk[0], t, blk[2]) if len(blk) == 3 else (t, blk[1])
    return [jax.ShapeDtypeStruct(full(blk), dt) for _, blk, dt in KEPT]


def _kept_specs():
    return [pl.BlockSpec(blk, (lambda n: (0, n, 0)) if len(blk) == 3 else (lambda n: (n, 0))) for _, blk, _ in KEPT]


def _mix_fwd(proj, bias, sinks, lng, lnb, ws2, bfull, aog, gog, comm):
    t = proj.shape[0]
    names, shapes = _mix_scratch(False)

    def body(proj_ref, kvp_ref, bias_ref, sinks_ref, lng_ref, lnb_ref, ws2_ref, bfull_ref, aog_ref, gog_ref,
             out_ref, *rest):
        s = dict(zip([name for name, _, _ in KEPT] + names, rest))
        n = pl.program_id(0)
        _mix_stage_kv(proj_ref, kvp_ref, s)
        _mix_stage_attn(proj_ref, bias_ref, sinks_ref, n, s)
        _mix_stage_gmlp_pre(proj_ref, lng_ref[...], lnb_ref[...], s, False)
        _mix_stage_gmlp_mix(ws2_ref, bfull_ref, s)
        for r0 in range(0, BLK, ROWS):
            rows = slice(r0, r0 + ROWS)
            out_ref[rows, 0:ATTN_W] = _rms(s["attn"][rows, :], aog_ref[...])[0].astype(BF16)
            out_ref[rows, ATTN_W:ATTN_W + GMLP_W] = _rms(s["u"][rows, :] * s["ms"][rows, :], gog_ref[...])[0].astype(BF16)

    return _call(
        body, name="mix_fwd", grid=(t // BLK,),
        out_shape=[jax.ShapeDtypeStruct((t, D_MODEL), BF16)] + _kept_shapes(t),
        in_specs=_mix_specs(True), out_specs=[pl.BlockSpec((BLK, D_MODEL), lambda n: (n, 0))] + _kept_specs(),
        scratch_shapes=shapes,
        sem=("parallel",), comm=comm, args=(proj, proj, bias, sinks, lng, lnb, ws2, bfull, aog, gog))


def _mix_bwd(proj, lng, lnb, ws2, wst2, bfull, aog, gog, dy, w_out, kept, comm):
    t = proj.shape[0]
    nb = t // BLK
    names, shapes = _mix_scratch(True)
    c_gu = ATTN_W + 2 * KV_W

    def body(proj_ref, kvp_ref, lng_ref, lnb_ref, ws2_ref, bfull_ref, aog_ref, gog_ref,
             wst2_ref, dy_ref, wout_ref, *rest):
        n_kept = len(KEPT)
        s = dict(zip([name for name, _, _ in KEPT], rest[:n_kept]))
        (dproj_ref, dkvn_ref, dl_ref, dsink_ref, dlng_ref, dlnb_ref, dws_ref, dbs_ref, daog_ref,
         dgog_ref) = rest[n_kept:n_kept + 10]
        s.update(zip(names, rest[n_kept + 10:]))
        n = pl.program_id(0)

        @pl.when(n == 0)
        def _():
            for r in (dl_ref, dsink_ref, dlng_ref, dlnb_ref, dws_ref, dbs_ref, daog_ref, dgog_ref):
                r[...] = jnp.zeros_like(r)

        s["dmix"][...] = _dot(dy_ref[...], wout_ref[...])
        _mix_stage_kv(proj_ref, kvp_ref, s)
        _mask_heads(proj_ref, s["qm"])
        lng = lng_ref[...]
        _mix_stage_gmlp_pre(proj_ref, lng, lnb_ref[...], s, True)
        _mix_stage_gmlp_mix(ws2_ref, bfull_ref, s)

        aog, gog = aog_ref[...], gog_ref[...]
        for r0 in range(0, BLK, ROWS):
            rows = slice(r0, r0 + ROWS)
            attn, dma = s["attn"][rows, :], s["dmix"][rows, 0:ATTN_W]
            _, r_a = _rms(attn, aog)
            daog_ref[...] += _rowsum8(dma * attn * r_a)
            s["dattn"][rows, :] = _rms_bwd(dma, attn, r_a, aog)
            u, ms, dmg = s["u"][rows, :], s["ms"][rows, :], s["dmix"][rows, ATTN_W:ATTN_W + GMLP_W]
            gm = u * ms
            _, r_g = _rms(gm, gog)
            dgog_ref[...] += _rowsum8(dmg * gm * r_g)
            dgm = _rms_bwd(dmg, gm, r_g, gog)
            dproj_ref[rows, c_gu:c_gu + GMLP_W] = dgm * ms * s["du"][rows, :]
            dms = dgm * u
            dbs_ref[rows, :] += dms
            s["dmsb"][rows, :] = dms.astype(BF16)

        _mask_heads(s["dattn"], s["dom"])
        for g in range(2):
            s["lg"][g] = _dot_nt(_group_rows(s["dom"], g), s["vd"][g])
        lane = lax.broadcasted_iota(jnp.int32, (HALF, 128), 1)
        for hf in range(BLK // HALF):
            rows = slice(HALF * hf, HALF * (hf + 1))
            dsink = jnp.zeros((HALF, 128), F32)
            for h in range(N_HEADS):
                grows = slice(BLK * (h % 4) + HALF * hf, BLK * (h % 4) + HALF * (hf + 1))
                dp = s["lg"][h // 4, grows, :]
                p = s["p"][h, rows, :]
                s["pb"][h, rows, :] = p.astype(BF16)
                rs = jnp.sum(p * dp, axis=1, keepdims=True)
                dl = p * (dp - rs)
                dl_ref[h, rows, :] += dl
                dsink = dsink + jnp.where(lane == h, -s["psink"][rows, :] * rs, 0.0)
                s["dls"][h, rows, :] = (dl * (HEAD_DIM ** -0.5)).astype(BF16)
            dsink_ref[rows, :] += dsink
        for g in range(2):
            dq = _dot(_pair_rows(s["dls"], g), s["km"][g])
            dproj_ref[:, 256 * g:256 * g + 128] = dq[0:BLK]
            dproj_ref[:, 256 * g + 128:256 * g + 256] = dq[BLK:2 * BLK]
        lo_k = _lane_lo(2 * BLK)
        for col, lhs, rhs in ((0, "dls", "qm"), (KV_W, "pb", "dom")):
            raw = [_dot_tn(_group_rows(s[lhs], g), _group_rows(s[rhs], g)) for g in range(2)]
            both = [r + pltpu.roll(r, 64, 1) for r in raw]
            dkv = jnp.where(lo_k, both[0], both[1])
            dproj_ref[:, ATTN_W + col:ATTN_W + col + KV_W] = dkv[BLK:2 * BLK]
            dkvn_ref[:, col:col + KV_W] = dkv[0:BLK]

        for j in range(4):
            cols = slice(128 * j, 128 * (j + 1))
            dm2 = _stack_halves(s["dmsb"][:, cols])
            vnb = s["vnb"][:, cols]
            dws2 = _dot_nt(dm2, vnb)
            dws_ref[2 * j] += dws2[0:BLK]
            dws_ref[2 * j + 1] += dws2[BLK:2 * BLK]
            s["dvn"][:, cols] = _dot(wst2_ref[j], dm2)
        for r0 in range(0, BLK, ROWS):
            rows = slice(r0, r0 + ROWS)
            dvn, vhat = s["dvn"][rows, :], s["vhat"][rows, :]
            dlng_ref[...] += _rowsum8(dvn * vhat)
            dlnb_ref[...] += _rowsum8(dvn)
            dvh = dvn * lng
            dact = s["rstd"][rows, :] * (dvh - _seg_mean64(dvh) - vhat * _seg_mean64(dvh * vhat))
            dproj_ref[rows, c_gu + GMLP_W:IN_W] = dact * s["da"][rows, :]

    acc8 = lambda w: jax.ShapeDtypeStruct((8, w), F32)
    out_shape = [jax.ShapeDtypeStruct((t, IN_W), F32), jax.ShapeDtypeStruct((t, 2 * KV_W), F32),
                 jax.ShapeDtypeStruct((N_HEADS, BLK, 2 * BLK), F32), jax.ShapeDtypeStruct((BLK, 128), F32),
                 acc8(GMLP_W), acc8(GMLP_W), jax.ShapeDtypeStruct((N_GROUPS, BLK, BLK), F32),
                 jax.ShapeDtypeStruct((BLK, GMLP_W), F32), acc8(ATTN_W), acc8(GMLP_W)]
    out_specs = [pl.BlockSpec((BLK, IN_W), lambda n: (n, 0)),
                 pl.BlockSpec((BLK, 2 * KV_W), lambda n: ((n + nb - 1) % nb, 0)),
                 _full((N_HEADS, BLK, 2 * BLK)), _full((BLK, 128)), _full((8, GMLP_W)), _full((8, GMLP_W)),
                 _full((N_GROUPS, BLK, BLK)), _full((BLK, GMLP_W)), _full((8, ATTN_W)), _full((8, GMLP_W))]
    in_specs = _mix_specs(False) + [_full((N_GROUPS // 2, BLK, 2 * BLK)),
                               pl.BlockSpec((BLK, D_MODEL), lambda n: (n, 0)),
                               _full((D_MODEL, D_MODEL))] + _kept_specs()
    return _call(
        body, name="mix_bwd", grid=(nb,), out_shape=out_shape, in_specs=in_specs, out_specs=out_specs,
        scratch_shapes=shapes, sem=("arbitrary",), comm=comm,
        args=(proj, proj, lng, lnb, ws2, bfull, aog, gog, wst2, dy, w_out, *kept))


def _outproj(mixed, w_out, x, g1, ln1g, ln1b, sc2, sh2, tm, comm):
    t, d = x.shape

    def body(mx_ref, w_ref, x_ref, g1_ref, lg_ref, lb_ref, sc_ref, sh_ref, y_ref, x1_ref, h2_ref):
        y = _dot(mx_ref[...], w_ref[...])
        xhat, _ = _ln_stats(ALPHA * x_ref[...] + g1_ref[...] * y)
        x1 = xhat * lg_ref[...] + lb_ref[...]
        y_ref[...] = y
        x1_ref[...] = x1
        h2_ref[...] = (x1 * (1.0 + sc_ref[...]) + sh_ref[...]).astype(BF16)

    row = pl.BlockSpec((tm, d), lambda i: (i, 0))
    vec = _full((1, d))
    return _call(
        body, name="outproj", grid=(t // tm,),
        out_shape=[jax.ShapeDtypeStruct((t, d), F32), jax.ShapeDtypeStruct((t, d), F32),
                   jax.ShapeDtypeStruct((t, d), BF16)],
        in_specs=[row, _full((d, d)), row, vec, vec, vec, vec, vec], out_specs=[row, row, row],
        sem=("parallel",), comm=comm, args=(mixed, w_out, x, g1, ln1g, ln1b, sc2, sh2))


def _ffn_fwd(h2, w_gu_t, w_down, x1, target, g2, ln2g, ln2b, tm):
    t, d = x1.shape

    def body(h_ref, w_ref, wd_ref, x1_ref, tg_ref, g2_ref, lg_ref, lb_ref,
             dsu_ref, sg_ref, act_ref, dz_ref, dy_ref, loss_ref, dlg_ref, dlb_ref, dg2_ref):
        @pl.when(pl.program_id(0) == 0)
        def _():
            for r in (loss_ref, dlg_ref, dlb_ref, dg2_ref):
                r[...] = jnp.zeros_like(r)

        h = h_ref[...]
        g = _dot_nt(h, w_ref[0:D_FF])
        u = _dot_nt(h, w_ref[D_FF:2 * D_FF])
        s = _sigmoid(g)
        sg = g * s
        act = (sg * u).astype(BF16)
        dsu_ref[...] = (u * (s * (1.0 + g * (1.0 - s)))).astype(BF16)
        sg_ref[...] = sg.astype(BF16)
        act_ref[...] = act
        y2 = _dot(act, wd_ref[...])
        g2 = g2_ref[...]
        lg = lg_ref[...]
        xhat, rstd = _ln_stats(ALPHA * x1_ref[...] + g2 * y2)
        err = xhat * lg + lb_ref[...] - tg_ref[...]
        loss_ref[...] += _rowsum8(err * err)
        dx2 = err * (1.0 / d)
        dlg_ref[...] += _rowsum8(dx2 * xhat)
        dlb_ref[...] += _rowsum8(dx2)
        dz = _ln_bwd(dx2 * lg, xhat, rstd)
        dg2_ref[...] += _rowsum8(dz * y2)
        dz_ref[...] = dz
        dy_ref[...] = (g2 * dz).astype(BF16)

    row = pl.BlockSpec((tm, d), lambda i: (i, 0))
    wide = pl.BlockSpec((tm, D_FF), lambda i: (i, 0))
    vec = _full((1, d))
    acc = _full((8, d))
    acc_shape = jax.ShapeDtypeStruct((8, d), F32)
    wide_shape = jax.ShapeDtypeStruct((t, D_FF), BF16)
    return pl.pallas_call(
        body, name="ffn_fwd", grid=(t // tm,),
        out_shape=[wide_shape] * 3 + [jax.ShapeDtypeStruct((t, d), F32), jax.ShapeDtypeStruct((t, d), BF16)]
        + [acc_shape] * 4,
        in_specs=[row, _resident((2 * D_FF, d)), _resident((D_FF, d)), row, row, vec, vec, vec],
        out_specs=[wide] * 3 + [row, row, acc, acc, acc, acc], compiler_params=_params(("arbitrary",)),
    )(h2, w_gu_t, w_down, x1, target, g2, ln2g, ln2b)


def _resident(shape):
    nd = len(shape)
    return pl.BlockSpec(shape, lambda *_: (0,) * nd, pipeline_mode=pl.Buffered(1))


def _ffn_bwd(dy2, w_down, dsu, sg, w_gu_t, x1, x, y, dz2, sc2, g1, ln1g, tm, comm):
    t, d = x1.shape

    def body(dy2_ref, wd_ref, dsu_ref, sg_ref, w_ref, x1_ref, x_ref, y_ref, dz2_ref, sc_ref, g1_ref, lg_ref,
             dg_ref, du_ref, dz1_ref, dy_ref, dsc_ref, dsh_ref, dlg_ref, dlb_ref, dg1_ref):
        @pl.when(pl.program_id(0) == 0)
        def _():
            for r in (dsc_ref, dsh_ref, dlg_ref, dlb_ref, dg1_ref):
                r[...] = jnp.zeros_like(r)

        dact = _dot_nt(dy2_ref[...], wd_ref[...])
        dg = (dact * dsu_ref[...].astype(F32)).astype(BF16)
        du = (dact * sg_ref[...].astype(F32)).astype(BF16)
        dg_ref[...] = dg
        du_ref[...] = du
        dh2 = _dot(dg, w_ref[0:D_FF]) + _dot(du, w_ref[D_FF:2 * D_FF])
        x1 = x1_ref[...]
        y = y_ref[...]
        g1 = g1_ref[...]
        dsc_ref[...] += _rowsum8(dh2 * x1)
        dsh_ref[...] += _rowsum8(dh2)
        dx1 = dh2 * (1.0 + sc_ref[...]) + ALPHA * dz2_ref[...]
        xhat, rstd = _ln_stats(ALPHA * x_ref[...] + g1 * y)
        dlg_ref[...] += _rowsum8(dx1 * xhat)
        dlb_ref[...] += _rowsum8(dx1)
        dz1 = _ln_bwd(dx1 * lg_ref[...], xhat, rstd)
        dg1_ref[...] += _rowsum8(dz1 * y)
        dz1_ref[...] = dz1
        dy_ref[...] = (g1 * dz1).astype(BF16)

    row = pl.BlockSpec((tm, d), lambda i: (i, 0))
    wide = pl.BlockSpec((tm, D_FF), lambda i: (i, 0))
    vec = _full((1, d))
    acc = _full((8, d))
    acc_shape = jax.ShapeDtypeStruct((8, d), F32)
    wide_shape = jax.ShapeDtypeStruct((t, D_FF), BF16)
    return _call(
        body, name="ffn_bwd", grid=(t // tm,),
        out_shape=[wide_shape, wide_shape, jax.ShapeDtypeStruct((t, d), F32), jax.ShapeDtypeStruct((t, d), BF16)]
        + [acc_shape] * 5,
        in_specs=[row, _resident((D_FF, d)), wide, wide, _resident((2 * D_FF, d)), row, row, row, row, vec, vec, vec],
        out_specs=[wide, wide, row, row, acc, acc, acc, acc, acc], sem=("arbitrary",), comm=comm,
        args=(dy2, w_down, dsu, sg, w_gu_t, x1, x, y, dz2, sc2, g1, ln1g))


def _din(dproj, dkvn, w_in_t, x, dz1, sc1, tm, comm):
    t, d = x.shape

    def body(dp_ref, dkv_ref, w_ref, x_ref, dz1_ref, sc_ref, dx_ref, dpb_ref, dbin_ref, dsc_ref, dsh_ref):
        @pl.when(pl.program_id(0) == 0)
        def _():
            for r in (dbin_ref, dsc_ref, dsh_ref):
                r[...] = jnp.zeros_like(r)

        dp = jnp.concatenate([dp_ref[:, 0:ATTN_W], dp_ref[:, ATTN_W:ATTN_W + 2 * KV_W] + dkv_ref[...],
                              dp_ref[:, ATTN_W + 2 * KV_W:IN_W]], axis=1)
        dbin_ref[...] += _rowsum8(dp)
        dpb = dp.astype(BF16)
        dpb_ref[...] = dpb
        dh = _dot(dpb, w_ref[...])
        dsc_ref[...] += _rowsum8(dh * x_ref[...])
        dsh_ref[...] += _rowsum8(dh)
        dx_ref[...] = dh * (1.0 + sc_ref[...]) + ALPHA * dz1_ref[...]

    row = lambda w: pl.BlockSpec((tm, w), lambda i: (i, 0))
    return _call(
        body, name="din", grid=(t // tm,),
        out_shape=[jax.ShapeDtypeStruct((t, d), F32), jax.ShapeDtypeStruct((t, IN_W), BF16),
                   jax.ShapeDtypeStruct((8, IN_W), F32), jax.ShapeDtypeStruct((8, d), F32),
                   jax.ShapeDtypeStruct((8, d), F32)],
        in_specs=[row(IN_W), row(2 * KV_W), _full((IN_W, d)), row(d), row(d), _full((1, d))],
        out_specs=[row(d), row(IN_W), _full((8, IN_W)), _full((8, d)), _full((8, d))],
        sem=("arbitrary",), comm=comm, args=(dproj, dkvn, w_in_t, x, dz1, sc1))


def _wgrad(name, a, b, tmm, tk, comm=None, a2=None):
    t, m = a.shape
    n = b.shape[1]
    nk = t // tk
    nm = m // tmm

    def body(*refs):
        a_refs, (b_ref, o_ref, acc_ref) = refs[:-3], refs[-3:]
        i, k = pl.program_id(0), pl.program_id(1)

        @pl.when(k == 0)
        def _():
            acc_ref[...] = jnp.zeros_like(acc_ref)

        a_tile = a_refs[0][...] if a2 is None else jnp.where(i < nm, a_refs[0][...], a_refs[1][...])
        acc_ref[...] += _dot_tn(a_tile, b_ref[...])

        @pl.when(k == nk - 1)
        def _():
            o_ref[...] = acc_ref[...].astype(BF16)

    if a2 is None:
        a_specs, a_args, n_tiles = [pl.BlockSpec((tk, tmm), lambda i, k: (k, i))], (a,), nm
    else:
        a_specs = [pl.BlockSpec((tk, tmm), lambda i, k: (jnp.where(i < nm, k, 0), jnp.minimum(i, nm - 1))),
                   pl.BlockSpec((tk, tmm), lambda i, k: (jnp.where(i < nm, 0, k), jnp.maximum(i - nm, 0)))]
        a_args, n_tiles = (a, a2), 2 * nm
    (out,), got = _call(
        body, name=name, grid=(n_tiles, nk), out_shape=[jax.ShapeDtypeStruct((n_tiles * tmm, n), BF16)],
        in_specs=a_specs + [pl.BlockSpec((tk, n), lambda i, k: (k, 0))],
        out_specs=[pl.BlockSpec((tmm, n), lambda i, k: (i, 0))],
        scratch_shapes=[pltpu.VMEM((tmm, n), F32)], sem=("parallel", "arbitrary"), comm=comm, args=a_args + (b,))
    return out if comm is None else (out, got)


def _adamw(w, g, m, v):
    m = ADAM_B1 * m + (1.0 - ADAM_B1) * g
    v = ADAM_B2 * v + (1.0 - ADAM_B2) * (g * g)
    m_hat = m / (1.0 - ADAM_B1 ** ADAM_STEP)
    v_hat = v / (1.0 - ADAM_B2 ** ADAM_STEP)
    delta = -ADAM_LR * (m_hat / (jnp.sqrt(v_hat) + ADAM_EPS) + ADAM_WD * w)
    return delta, m, v


def _adam_reduce(name, parts, w, m, v, tr, comm=None):
    r, cdim = w.shape

    def body(p_ref, w_ref, m_ref, v_ref, g_ref, d_ref, mo_ref, vo_ref):
        g = p_ref[0].astype(F32)
        for s in range(1, N_DEV):
            g = g + p_ref[s].astype(F32)
        d_ref[...], mo_ref[...], vo_ref[...] = _adamw(w_ref[...], g, m_ref[...], v_ref[...])
        g_ref[...] = g

    tile = pl.BlockSpec((tr, cdim), lambda i: (i, 0))
    shp = jax.ShapeDtypeStruct((r, cdim), F32)
    res, got = _call(
        body, name=name, grid=(r // tr,), out_shape=[shp] * 4,
        in_specs=[pl.BlockSpec((N_DEV, tr, cdim), lambda i: (0, i, 0)), tile, tile, tile],
        out_specs=[tile] * 4, sem=("parallel",), comm=comm, args=(parts, w, m, v))
    return res if comm is None else (res, got)


def _adam_w_ada(c_all_t, dmod_cols, w, m, v):
    def body(ct_ref, dm_ref, w_ref, m_ref, v_ref, g_ref, d_ref, mo_ref, vo_ref):
        ct = ct_ref[...]
        s = (ct * _sigmoid(ct)).astype(BF16)
        g = _dot(s, dm_ref[...].astype(BF16))
        d_ref[...], mo_ref[...], vo_ref[...] = _adamw(w_ref[...], g, m_ref[...], v_ref[...])
        g_ref[...] = g

    shp = jax.ShapeDtypeStruct(w.shape, F32)
    return pl.pallas_call(
        body, name="adam_w_ada", grid=(1,), out_shape=[shp] * 4,
        in_specs=[_full(c_all_t.shape), _full(dmod_cols.shape)] + [_full(w.shape)] * 3,
        out_specs=[_full(w.shape)] * 4, compiler_params=_params(("arbitrary",)),
    )(c_all_t, dmod_cols, w, m, v)


SMALL_EARLY = ["rel_bias", "attn_sinks", "gmlp_ln_g", "gmlp_ln_b", "gmlp_w_s", "gmlp_b_s",
               "attn_out_g", "gmlp_out_g", "ln1_g", "ln1_b", "ln2_g", "ln2_b"]
SMALL_LATE = ["b_ada", "b_in", "loss"]
WEIGHTS = ["rel_bias", "w_ada", "b_ada", "w_in", "b_in", "attn_sinks", "gmlp_ln_g", "gmlp_ln_b", "gmlp_w_s",
           "gmlp_b_s", "attn_out_g", "gmlp_out_g", "w_out", "ln1_g", "ln1_b", "w_gate_up", "w_down", "ln2_g", "ln2_b"]


def _seg_rows(nelem):
    return -(-nelem // 1024) * 8


def _pack(named, names):
    parts = []
    for name in names:
        flat = named[name].reshape(-1).astype(F32)
        rows = _seg_rows(flat.shape[0])
        parts.append(jnp.pad(flat, (0, rows * 128 - flat.shape[0])).reshape(rows, 128))
    return jnp.concatenate(parts, axis=0)


def _adam_small(name, parts, names, wts, mom_m, mom_v):
    params = [n for n in names if n in wts]

    def view(n):
        nelem = math.prod(wts[n].shape)
        return (nelem // 128, 128) if nelem % 128 == 0 else (1, nelem)

    offsets, r0 = {}, 0
    for n in names:
        offsets[n] = r0
        r0 += _seg_rows(math.prod(wts[n].shape) if n in wts else 1)

    def body(*refs):
        p_ref, ins, outs = refs[0], refs[1:1 + 3 * len(params)], refs[1 + 3 * len(params):]

        def total(n, rows, lanes):
            o = offsets[n]
            g = p_ref[0, o:o + rows, 0:lanes]
            for s in range(1, N_DEV):
                g = g + p_ref[s, o:o + rows, 0:lanes]
            return g

        for i, n in enumerate(params):
            g = total(n, *view(n))
            w_ref, m_ref, v_ref = ins[3 * i:3 * i + 3]
            g_ref, d_ref, mo_ref, vo_ref = outs[4 * i:4 * i + 4]
            d_ref[...], mo_ref[...], vo_ref[...] = _adamw(w_ref[...], g, m_ref[...], v_ref[...])
            g_ref[...] = g
        for j, n in enumerate(n for n in names if n not in wts):
            outs[4 * len(params) + j][...] = total(n, 8, 128)

    args, in_specs, out_shape = [parts], [_full(parts.shape)], []
    for n in params:
        args += [t[n].reshape(view(n)) for t in (wts, mom_m, mom_v)]
        in_specs += [_full(view(n))] * 3
        out_shape += [jax.ShapeDtypeStruct(view(n), F32)] * 4
    out_shape += [jax.ShapeDtypeStruct((8, 128), F32) for n in names if n not in wts]
    res = pl.pallas_call(
        body, name=name, grid=(1,), out_shape=out_shape, in_specs=in_specs,
        out_specs=[_full(s.shape) for s in out_shape], compiler_params=_params(("arbitrary",)),
    )(*args)
    done = {n: tuple(r.reshape(wts[n].shape) for r in res[4 * i:4 * i + 4]) for i, n in enumerate(params)}
    sums = {n: res[4 * len(params) + j] for j, n in enumerate(n for n in names if n not in wts)}
    return done, sums


def _t5_bucket_map():
    qi = jnp.arange(BLK)[:, None]
    si = jnp.arange(2 * BLK)[None, :]
    n = jnp.maximum(qi + BLK - si, 0)
    max_exact = N_BUCKETS // 2
    nf = jnp.maximum(n, max_exact).astype(F32)
    large = max_exact + (jnp.log(nf / max_exact) / math.log(MAX_DISTANCE / max_exact)
                         * (N_BUCKETS - max_exact)).astype(jnp.int32)
    large = jnp.minimum(large, N_BUCKETS - 1)
    return jnp.where(n < max_exact, n, large).astype(jnp.int32)


def kernel(x, c, rel_bias, w_ada, b_ada, w_in, b_in, attn_sinks, gmlp_ln_g, gmlp_ln_b, gmlp_w_s, gmlp_b_s, attn_out_g, gmlp_out_g, w_out, ln1_g, ln1_b, w_gate_up, w_down, ln2_g, ln2_b, loss_target, m_rel_bias, m_w_ada, m_b_ada, m_w_in, m_b_in, m_attn_sinks, m_gmlp_ln_g, m_gmlp_ln_b, m_gmlp_w_s, m_gmlp_b_s, m_attn_out_g, m_gmlp_out_g, m_w_out, m_ln1_g, m_ln1_b, m_w_gate_up, m_w_down, m_ln2_g, m_ln2_b, v_rel_bias, v_w_ada, v_b_ada, v_w_in, v_b_in, v_attn_sinks, v_gmlp_ln_g, v_gmlp_ln_b, v_gmlp_w_s, v_gmlp_b_s, v_attn_out_g, v_gmlp_out_g, v_w_out, v_ln1_g, v_ln1_b, v_w_gate_up, v_w_down, v_ln2_g, v_ln2_b):
    wts = dict(rel_bias=rel_bias, w_ada=w_ada, b_ada=b_ada, w_in=w_in, b_in=b_in, attn_sinks=attn_sinks,
               gmlp_ln_g=gmlp_ln_g, gmlp_ln_b=gmlp_ln_b, gmlp_w_s=gmlp_w_s, gmlp_b_s=gmlp_b_s,
               attn_out_g=attn_out_g, gmlp_out_g=gmlp_out_g, w_out=w_out, ln1_g=ln1_g, ln1_b=ln1_b,
               w_gate_up=w_gate_up, w_down=w_down, ln2_g=ln2_g, ln2_b=ln2_b)
    mom_m = dict(rel_bias=m_rel_bias, w_ada=m_w_ada, b_ada=m_b_ada, w_in=m_w_in, b_in=m_b_in,
                 attn_sinks=m_attn_sinks, gmlp_ln_g=m_gmlp_ln_g, gmlp_ln_b=m_gmlp_ln_b, gmlp_w_s=m_gmlp_w_s,
                 gmlp_b_s=m_gmlp_b_s, attn_out_g=m_attn_out_g, gmlp_out_g=m_gmlp_out_g, w_out=m_w_out,
                 ln1_g=m_ln1_g, ln1_b=m_ln1_b, w_gate_up=m_w_gate_up, w_down=m_w_down, ln2_g=m_ln2_g,
                 ln2_b=m_ln2_b)
    mom_v = dict(rel_bias=v_rel_bias, w_ada=v_w_ada, b_ada=v_b_ada, w_in=v_w_in, b_in=v_b_in,
                 attn_sinks=v_attn_sinks, gmlp_ln_g=v_gmlp_ln_g, gmlp_ln_b=v_gmlp_ln_b, gmlp_w_s=v_gmlp_w_s,
                 gmlp_b_s=v_gmlp_b_s, attn_out_g=v_attn_out_g, gmlp_out_g=v_gmlp_out_g, w_out=v_w_out,
                 ln1_g=v_ln1_g, ln1_b=v_ln1_b, w_gate_up=v_w_gate_up, w_down=v_w_down, ln2_g=v_ln2_g,
                 ln2_b=v_ln2_b)

    t = x.shape[1]
    tm = min(512, t)
    tn_ff = D_FF // 2
    tk_long, tk_short = min(4096, t), min(2048, t)
    me = 4 * lax.axis_index("x") + 2 * lax.axis_index("y") + lax.axis_index("c")
    xs = x[0]
    target = loss_target[0]

    (c_g,) = _exchange("gather_c", [jnp.broadcast_to(c, (8, D_MODEL))], ("gather",))
    c_all = c_g[:, 0, :]

    ncol = w_ada.shape[2]
    b_cols = lax.dynamic_slice(b_ada, (0, me * ncol), (1, ncol))
    mod_part = _mod_partial(c_all, w_ada[0], b_cols)
    bucket = _t5_bucket_map()
    (bias,), (mod_g, w_in_g) = _bias_table(rel_bias, bucket,
                                           comm=([mod_part, w_in[0].T.astype(BF16)], ("gather", "gather2")))
    w_in_t = w_in_g.reshape(IN_W, D_MODEL)
    mod = lax.dynamic_slice(mod_g, (0, me, 0), (N_DEV, 1, ncol)).reshape(1, N_DEV * ncol)
    sh1, sc1, g1, sh2, sc2, g2 = [mod[:, i * D_MODEL:(i + 1) * D_MODEL] for i in range(6)]

    causal = jnp.tril(jnp.ones((BLK, BLK), dtype=bool))
    ws = jnp.where(causal[None], gmlp_w_s[0], 0.0).astype(BF16)
    pair = lambda w: jnp.concatenate([w[0::2], w[1::2]], axis=2)
    ws2, wst2 = pair(ws), pair(jnp.swapaxes(ws, 1, 2))
    bfull = jnp.repeat(gmlp_b_s[0].T, GMLP_W // N_GROUPS, axis=1)
    sinks = attn_sinks[0]

    (proj, h1), (w_down_g,) = _inproj(xs, sc1, sh1, w_in_t, b_in, tm, comm=([w_down[0].astype(BF16)], ("gather2",)))
    (mixed, *kept), (w_out_g, w_gu_g) = _mix_fwd(
        proj, bias, sinks, gmlp_ln_g, gmlp_ln_b, ws2, bfull, attn_out_g, gmlp_out_g,
        comm=([w_out[0].astype(BF16), w_gate_up[0].T.astype(BF16)], ("gather2", "gather2")))
    w_out_f = w_out_g.reshape(D_MODEL, D_MODEL)
    w_gu_t = w_gu_g.reshape(2 * D_FF, D_MODEL)
    (y1, x1, h2), _ = _outproj(mixed, w_out_f, xs, g1, ln1_g, ln1_b, sc2, sh2, tm, comm=None)
    w_down_f = w_down_g.reshape(D_FF, D_MODEL)
    dsu, sg, act, dz2, dy2, loss_p, d_ln2g, d_ln2b, d_g2 = _ffn_fwd(h2, w_gu_t, w_down_f, x1, target, g2, ln2_g, ln2_b,
                                                                    min(256, t))

    slots = lambda a: a.reshape(N_DEV, -1, D_MODEL)
    dw_down = _wgrad("wgrad_down", act, dy2, tn_ff, tk_short)
    (dgate, dup, dz1, dy1, d_sc2, d_sh2, d_ln1g, d_ln1b, d_g1), (r_down,) = _ffn_bwd(
        dy2, w_down_f, dsu, sg, w_gu_t, x1, xs, y1, dz2, sc2, g1, ln1_g, min(256, t),
        comm=([slots(dw_down)], ("scatter",)))
    dw_gu_t = _wgrad("wgrad_gate_up", dgate, h2, tn_ff, tk_short, a2=dup)
    dw_out = _wgrad("wgrad_out", mixed, dy1, D_MODEL, tk_long)
    ((dproj, dkvn, dl_acc, dsink_acc, d_lng, d_lnb, d_ws, d_bs, d_aog, d_gog), (r_gu, r_out)) = _mix_bwd(
        proj, gmlp_ln_g, gmlp_ln_b, ws2, wst2, bfull, attn_out_g, gmlp_out_g, dy1, w_out_f.T, kept,
        comm=([slots(dw_gu_t), slots(dw_out)], ("scatter", "scatter")))
    d_relb = _bias_grad(dl_acc, bucket)

    rsum = lambda a: jnp.sum(a, axis=0)
    early_g = dict(
        rel_bias=d_relb[:, 0, :N_BUCKETS].T, attn_sinks=rsum(dsink_acc)[:N_HEADS],
        gmlp_ln_g=rsum(d_lng), gmlp_ln_b=rsum(d_lnb), gmlp_w_s=jnp.where(causal[None], d_ws, 0.0),
        gmlp_b_s=jnp.sum(d_bs.reshape(BLK, N_GROUPS, GMLP_W // N_GROUPS), axis=2).T,
        attn_out_g=rsum(d_aog), gmlp_out_g=rsum(d_gog), ln1_g=rsum(d_ln1g), ln1_b=rsum(d_ln1b),
        ln2_g=rsum(d_ln2g), ln2_b=rsum(d_ln2b))
    (grad_x, dproj_b, d_bin, d_sc1, d_sh1), _ = _din(dproj, dkvn, w_in_t, xs, dz1, sc1, tm, comm=None)
    dw_in_t, (early_all,) = _wgrad("wgrad_in", dproj_b, h1, IN_W // 2, tk_long,
                                   comm=([_pack(early_g, SMALL_EARLY)], ("gather2",)))
    dmod = jnp.concatenate([rsum(d_sh1), rsum(d_sc1), rsum(d_g1), rsum(d_sh2), rsum(d_sc2), rsum(d_g2)])
    late_g = dict(b_ada=dmod, b_in=rsum(d_bin), loss=(0.5 / D_MODEL * jnp.sum(loss_p)).reshape(1))
    late_all, r_in = _exchange("scatter_in", [_pack(late_g, SMALL_LATE), slots(dw_in_t)], ("gather", "scatter"))

    small, _ = _adam_small("adam_small_early", early_all, SMALL_EARLY, wts, mom_m, mom_v)
    small_late, sums = _adam_small("adam_small_late", late_all, SMALL_LATE, wts, mom_m, mom_v)
    small.update(small_late)
    loss = sums["loss"][0, 0]

    dmod_all = late_all[:, :_seg_rows(6 * D_MODEL), :].reshape(N_DEV, 6 * D_MODEL)
    dmod_cols = lax.dynamic_slice(dmod_all, (0, me * ncol), (N_DEV, ncol))
    kpad = 128 - N_DEV
    ada = _adam_w_ada(jnp.pad(c_all.T, ((0, 0), (0, kpad))), jnp.pad(dmod_cols, ((0, kpad), (0, 0))),
                      w_ada[0], m_w_ada[0], v_w_ada[0])

    tr = lambda a: jnp.swapaxes(a, -1, -2)
    big = {}
    big["w_in"] = [tr(o)[None] for o in _adam_reduce("adam_w_in", r_in, w_in[0].T, m_w_in[0].T, v_w_in[0].T, 112)]
    big["w_out"] = [o[None] for o in _adam_reduce("adam_w_out", r_out, w_out[0], m_w_out[0], v_w_out[0], 128)]
    big["w_gate_up"] = [tr(o)[None] for o in _adam_reduce("adam_w_gu", r_gu, w_gate_up[0].T, m_w_gate_up[0].T,
                                                           v_w_gate_up[0].T, 352)]
    big["w_down"] = [o[None] for o in _adam_reduce("adam_w_down", r_down, w_down[0], m_w_down[0], v_w_down[0], 176)]
    big["w_ada"] = [o[None] for o in ada]

    outs = [[], [], [], []]
    for name in WEIGHTS:
        for i in range(4):
            outs[i].append(big[name][i] if name in big else small[name][i])
    return (loss, grad_x[None], *outs[0], *outs[1], *outs[2], *outs[3])
```

```python
import math

import jax
import jax.numpy as jnp
from jax import lax
from jax.experimental import pallas as pl
from jax.experimental.pallas import tpu as pltpu

F32 = jnp.float32
BF16 = jnp.bfloat16
MESH = pl.DeviceIdType.MESH

N_DEV = 8
D_MODEL = 1024
HEAD_DIM = 64
N_HEADS = 8
N_GROUPS = 8
ATTN_W = 512
KV_W = 128
GMLP_W = 512
IN_W = 1792
BLK = 128
N_BUCKETS = 32
MAX_DISTANCE = 128
D_FF = 2816
ALPHA = 2.0 ** 0.25
LN_EPS = 1e-5
NEG_INF = -1e30
ADAM_LR = 0.001
ADAM_B1 = 0.9
ADAM_B2 = 0.999
ADAM_EPS = 1e-08
ADAM_WD = 0.01
ADAM_STEP = 10
GELU_C0 = math.sqrt(2.0 / math.pi)
GELU_C1 = 0.044715

VMEM_LIMIT = 56 * 1024 * 1024


def _params(sem):
    return pltpu.CompilerParams(dimension_semantics=sem, vmem_limit_bytes=VMEM_LIMIT)


def _dot(a, b):
    return lax.dot_general(a, b, (((1,), (0,)), ((), ())), preferred_element_type=F32)


def _dot_nt(a, b):
    return lax.dot_general(a, b, (((1,), (1,)), ((), ())), preferred_element_type=F32)


def _dot_tn(a, b):
    return lax.dot_general(a, b, (((0,), (0,)), ((), ())), preferred_element_type=F32)


def _full(shape):
    nd = len(shape)
    return pl.BlockSpec(shape, lambda *_: (0,) * nd)


def _rowsum8(v):
    r, c = v.shape
    return jnp.sum(v.reshape(r // 8, 8, c), axis=0)


def _sigmoid(v):
    return 1.0 / (1.0 + jnp.exp(-v))


def _gelu_parts(v):
    v2 = v * v
    t = jnp.tanh(GELU_C0 * (v + GELU_C1 * v * v2))
    g = 0.5 * v * (1.0 + t)
    dg = 0.5 * (1.0 + t) + 0.5 * v * (1.0 - t * t) * (GELU_C0 * (1.0 + 3.0 * GELU_C1 * v2))
    return g, dg


def _ln_stats(z):
    mu = jnp.mean(z, axis=1, keepdims=True)
    zc = z - mu
    var = jnp.mean(zc * zc, axis=1, keepdims=True)
    rstd = lax.rsqrt(var + LN_EPS)
    return zc * rstd, rstd


def _ln_bwd(dxhat, xhat, rstd):
    m1 = jnp.mean(dxhat, axis=1, keepdims=True)
    m2 = jnp.mean(dxhat * xhat, axis=1, keepdims=True)
    return rstd * (dxhat - m1 - xhat * m2)


def _seg_mean64(v):
    r = v.shape[0]
    lo = lax.broadcasted_iota(jnp.int32, (r, 128), 1) < 64
    outs = []
    for j in range(v.shape[1] // 128):
        ch = v[:, 128 * j:128 * (j + 1)]
        s_lo = jnp.sum(jnp.where(lo, ch, 0.0), axis=1, keepdims=True)
        s_hi = jnp.sum(jnp.where(lo, 0.0, ch), axis=1, keepdims=True)
        outs.append(jnp.where(lo, s_lo, s_hi) * (1.0 / 64.0))
    return jnp.concatenate(outs, axis=1)


def _rms(a, g):
    r = lax.rsqrt(jnp.mean(a * a, axis=1, keepdims=True) + LN_EPS)
    return a * r * g, r


def _rms_bwd(dout, a, r, g):
    t = dout * g
    return r * t - a * (r * r * r) * jnp.mean(t * a, axis=1, keepdims=True)


PEER_ORDER = (1, 2, 4, 3, 5, 6, 7)


def _peer(j):
    x, y, c = lax.axis_index("x"), lax.axis_index("y"), lax.axis_index("c")
    px = 1 - x if j & 4 else x
    py = 1 - y if j & 2 else y
    pc = 1 - c if j & 1 else c
    return (px, py, pc), 4 * px + 2 * py + pc


SIBLING = 1
CHIP_FLIPS = (4, 2, 6)


def _exchange_phase(phase, ins, outs, modes, send_sems, recv_sems, loc_sems):
    me = 4 * lax.axis_index("x") + 2 * lax.axis_index("y") + lax.axis_index("c")
    for k, mode in enumerate(modes):
        def copy(i, src, slot, dev, k=k):
            return pltpu.make_async_remote_copy(src_ref=src, dst_ref=outs[k].at[slot], send_sem=send_sems.at[k, i],
                                                recv_sem=recv_sems.at[k, i], device_id=dev, device_id_type=MESH)

        src_me = ins[k].at[me] if mode == "scatter" else ins[k]
        local = pltpu.make_async_copy(src_me, outs[k].at[me], loc_sems.at[k])
        if mode == "gather2":
            sib_dev, sib_idx = _peer(SIBLING)
            chips = [_peer(j) for j in CHIP_FLIPS]
            far = [_peer(j | SIBLING)[1] for j in CHIP_FLIPS]
            if phase == "start":
                local.start()
                copy(0, ins[k], me, sib_dev).start()
                for i, (dev, _) in enumerate(chips):
                    copy(1 + i, ins[k], me, dev).start()
            elif phase == "mid":
                for i, (dev, idx) in enumerate(chips):
                    copy(1 + i, ins[k], idx, dev).wait_recv()
                    copy(4 + i, outs[k].at[idx], idx, sib_dev).start()
            else:
                copy(0, ins[k], sib_idx, sib_dev).wait_recv()
                for i, slot in enumerate(far):
                    copy(4 + i, ins[k], slot, sib_dev).wait_recv()
                copy(0, ins[k], me, sib_dev).wait_send()
                for i, (dev, idx) in enumerate(chips):
                    copy(1 + i, ins[k], me, dev).wait_send()
                    copy(4 + i, outs[k].at[idx], idx, sib_dev).wait_send()
                local.wait()
            continue
        peers = [_peer(j) for j in PEER_ORDER]
        if phase == "start":
            local.start()
            for i, (dev, idx) in enumerate(peers):
                copy(i, ins[k].at[idx] if mode == "scatter" else ins[k], me, dev).start()
        elif phase == "end":
            for i, (dev, idx) in enumerate(peers):
                copy(i, src_me, idx, dev).wait_recv()
            for i, (dev, idx) in enumerate(peers):
                copy(i, src_me, me, dev).wait_send()
            local.wait()


def _exchange_shapes(arrays, modes):
    return [jax.ShapeDtypeStruct((N_DEV,) + (a.shape[1:] if m == "scatter" else a.shape), a.dtype)
            for a, m in zip(arrays, modes)]


def _exchange_sems(n):
    return [pltpu.SemaphoreType.DMA((n, N_DEV - 1)), pltpu.SemaphoreType.DMA((n, N_DEV - 1)),
            pltpu.SemaphoreType.DMA((n,))]


def _exchange(name, arrays, modes):
    n = len(arrays)

    def body(*refs):
        for phase in ("start", "mid", "end"):
            _exchange_phase(phase, refs[:n], refs[n:2 * n], modes, *refs[2 * n:])

    any_spec = pl.BlockSpec(memory_space=pl.ANY)
    return pl.pallas_call(
        body, name=name, out_shape=_exchange_shapes(arrays, modes),
        in_specs=[any_spec] * n, out_specs=[any_spec] * n, scratch_shapes=_exchange_sems(n),
    )(*arrays)


N_CHIP = 4


def _scatter_two_level(name, pack, parts):
    r, ncols = parts.shape[1:]

    def body(pack_ref, parts_ref, late_ref, got_ref, sib_ref, h_ref, g_send, g_recv, g_loc, d_send, d_recv, i_send, i_recv):
        x, y, c = lax.axis_index("x"), lax.axis_index("y"), lax.axis_index("c")
        my_chip = 2 * x + y
        sib_dev, _ = _peer(SIBLING)
        gather = ([pack_ref], [late_ref], ("gather",), g_send, g_recv, g_loc)
        _exchange_phase("start", *gather)

        def to_sibling(q):
            return pltpu.make_async_remote_copy(src_ref=parts_ref.at[2 * q + 1 - c], dst_ref=sib_ref.at[q],
                                                send_sem=d_send.at[q], recv_sem=d_recv.at[q],
                                                device_id=sib_dev, device_id_type=MESH)

        for q in range(N_CHIP):
            to_sibling(q).start()
        for q in range(N_CHIP):
            to_sibling(q).wait_recv()
            h_ref[q] = (parts_ref[2 * q + c].astype(F32) + sib_ref[q].astype(F32)).astype(BF16)

        def to_chip(i, slot):
            dev, idx = _peer(CHIP_FLIPS[i])
            return pltpu.make_async_remote_copy(src_ref=h_ref.at[idx // 2], dst_ref=got_ref.at[slot],
                                                send_sem=i_send.at[i], recv_sem=i_recv.at[i],
                                                device_id=dev, device_id_type=MESH)

        for i in range(len(CHIP_FLIPS)):
            to_chip(i, my_chip).start()
        got_ref[my_chip] = h_ref[my_chip]
        for i in range(len(CHIP_FLIPS)):
            to_chip(i, _peer(CHIP_FLIPS[i])[1] // 2).wait_recv()
        for i in range(len(CHIP_FLIPS)):
            to_chip(i, my_chip).wait_send()
        for q in range(N_CHIP):
            to_sibling(q).wait_send()
        _exchange_phase("end", *gather)

    any_spec = pl.BlockSpec(memory_space=pl.ANY)
    vmem = pl.BlockSpec(memory_space=pltpu.VMEM)
    dma = pltpu.SemaphoreType.DMA
    return pl.pallas_call(
        body, name=name,
        out_shape=[jax.ShapeDtypeStruct((N_DEV,) + pack.shape, pack.dtype),
                   jax.ShapeDtypeStruct((N_CHIP, r, ncols), parts.dtype)],
        in_specs=[any_spec, vmem], out_specs=[any_spec, vmem],
        scratch_shapes=[pltpu.VMEM((N_CHIP, r, ncols), parts.dtype), pltpu.VMEM((N_CHIP, r, ncols), parts.dtype),
                        dma((1, N_DEV - 1)), dma((1, N_DEV - 1)), dma((1,)),
                        dma((N_CHIP,)), dma((N_CHIP,)), dma((len(CHIP_FLIPS),)), dma((len(CHIP_FLIPS),))],
        compiler_params=pltpu.CompilerParams(vmem_limit_bytes=VMEM_LIMIT),
    )(pack, parts)


def _call(body, *, name, grid, in_specs, out_specs, out_shape, args, sem, scratch_shapes=(), comm=None):
    if comm is None:
        outs = pl.pallas_call(body, name=name, grid=grid, in_specs=list(in_specs), out_specs=list(out_specs),
                              out_shape=list(out_shape), scratch_shapes=list(scratch_shapes),
                              compiler_params=_params(sem))(*args)
        return list(outs), []
    arrays, modes = comm
    n_in, n_out, nc, ns = len(in_specs), len(out_specs), len(arrays), len(scratch_shapes)
    n_steps = math.prod(grid)

    def hosted(*refs):
        ins, cins = refs[:n_in], refs[n_in:n_in + nc]
        outs, couts = refs[n_in + nc:n_in + nc + n_out], refs[n_in + nc + n_out:n_in + 2 * nc + n_out]
        scratch = refs[n_in + 2 * nc + n_out:]
        ex = (cins, couts, modes) + tuple(scratch[ns:])
        step = pl.program_id(0)
        for ax in range(1, len(grid)):
            step = step * grid[ax] + pl.program_id(ax)

        @pl.when(step == 0)
        def _():
            _exchange_phase("start", *ex)

        body(*ins, *outs, *scratch[:ns])

        if "gather2" in modes:
            @pl.when(step == (3 * n_steps) // 4)
            def _():
                _exchange_phase("mid", *ex)

        @pl.when(step == n_steps - 1)
        def _():
            _exchange_phase("end", *ex)

    any_spec = pl.BlockSpec(memory_space=pl.ANY)
    res = pl.pallas_call(
        hosted, name=name, grid=grid, in_specs=list(in_specs) + [any_spec] * nc,
        out_specs=list(out_specs) + [any_spec] * nc, out_shape=list(out_shape) + _exchange_shapes(arrays, modes),
        scratch_shapes=list(scratch_shapes) + _exchange_sems(nc),
        compiler_params=_params(tuple("arbitrary" for _ in grid)))(*args, *arrays)
    return list(res[:n_out]), list(res[n_out:])


def _mod_partial(c_all, w_ada, b_ada_cols):
    def body(c_ref, w_ref, b_ref, o_ref):
        cv = c_ref[...]
        s = (cv * _sigmoid(cv)).astype(BF16)
        o_ref[...] = _dot(s, w_ref[...].astype(BF16)) + b_ref[...]

    ncol = w_ada.shape[1]
    return pl.pallas_call(
        body, name="mod_partial", out_shape=jax.ShapeDtypeStruct((N_DEV, ncol), F32),
        in_specs=[_full(c_all.shape), _full(w_ada.shape), _full(b_ada_cols.shape)],
        out_specs=_full((N_DEV, ncol)), grid=(1,), compiler_params=_params(("arbitrary",)),
    )(c_all, w_ada, b_ada_cols)


def _bias_table(rel_bias, bucket, comm):
    def body(rb_ref, bk_ref, o_ref):
        h = pl.program_id(0)
        bk = bk_ref[...]
        acc = jnp.zeros((BLK, 2 * BLK), F32)
        for b in range(N_BUCKETS):
            acc = jnp.where(bk == b, rb_ref[b, h], acc)
        dist = (lax.broadcasted_iota(jnp.int32, (BLK, 2 * BLK), 0) + BLK
                - lax.broadcasted_iota(jnp.int32, (BLK, 2 * BLK), 1))
        o_ref[0] = jnp.where((dist >= 0) & (dist < BLK), acc, NEG_INF)

    return _call(
        body, name="bias_table", out_shape=[jax.ShapeDtypeStruct((N_HEADS, BLK, 2 * BLK), F32)],
        in_specs=[pl.BlockSpec(memory_space=pltpu.SMEM), _full((BLK, 2 * BLK))],
        out_specs=[pl.BlockSpec((1, BLK, 2 * BLK), lambda h: (h, 0, 0))], grid=(N_HEADS,),
        sem=("arbitrary",), comm=comm, args=(rel_bias, bucket))


def _bias_grad(dl_acc, bucket):
    def body(dl_ref, bk_ref, o_ref):
        bk = bk_ref[...]
        dl = dl_ref[0]
        lane = lax.broadcasted_iota(jnp.int32, (1, 128), 1)
        row = jnp.zeros((1, 128), F32)
        for b in range(N_BUCKETS):
            s = jnp.sum(jnp.sum(jnp.where(bk == b, dl, 0.0), axis=1, keepdims=True), axis=0, keepdims=True)
            row = jnp.where(lane == b, s, row)
        o_ref[0] = row

    return pl.pallas_call(
        body, name="bias_grad", out_shape=jax.ShapeDtypeStruct((N_HEADS, 1, 128), F32),
        in_specs=[pl.BlockSpec((1, BLK, 2 * BLK), lambda h: (h, 0, 0)), _full((BLK, 2 * BLK))],
        out_specs=pl.BlockSpec((1, 1, 128), lambda h: (h, 0, 0)), grid=(N_HEADS,),
        compiler_params=_params(("arbitrary",)),
    )(dl_acc, bucket)


def _inproj(x, sc1, sh1, w_in_t, b_in, tm, comm):
    t, d = x.shape
    n = w_in_t.shape[0]

    def body(x_ref, sc_ref, sh_ref, w_ref, b_ref, proj_ref, h_ref):
        h = (x_ref[...] * (1.0 + sc_ref[...]) + sh_ref[...]).astype(BF16)
        h_ref[...] = h
        proj_ref[...] = _dot_nt(h, w_ref[...]) + b_ref[...]

    row = lambda w: pl.BlockSpec((tm, w), lambda i: (i, 0))
    return _call(
        body, name="inproj", grid=(t // tm,),
        out_shape=[jax.ShapeDtypeStruct((t, n), F32), jax.ShapeDtypeStruct((t, d), BF16)],
        in_specs=[row(d), _full((1, d)), _full((1, d)), _full((n, d)), _full((1, n))],
        out_specs=[row(n), row(d)], sem=("parallel",), comm=comm, args=(x, sc1, sh1, w_in_t, b_in))


HALF = 64
ROWS = 32


def _lane_lo(rows):
    return lax.broadcasted_iota(jnp.int32, (rows, 128), 1) < 64


def _mix_stage_kv(proj_ref, kvp_ref, s):
    lo = _lane_lo(2 * BLK)
    for name, col in (("k", ATTN_W), ("v", ATTN_W + KV_W)):
        cur = jnp.concatenate([kvp_ref[:, col - ATTN_W:col - ATTN_W + KV_W], proj_ref[:, col:col + KV_W]], axis=0)
        plain, swapped = cur.astype(BF16), pltpu.roll(cur, 64, 1).astype(BF16)
        zero = jnp.zeros_like(plain)
        for g in range(2):
            dup = jnp.where(lo, plain, swapped) if g == 0 else jnp.where(lo, swapped, plain)
            s[name + "d"][g] = dup
            s[name + "m"][g] = jnp.concatenate([jnp.where(lo, dup, zero), jnp.where(lo, zero, dup)], axis=0)


def _group_rows(ref, g):
    return ref[4 * g:4 * g + 4].reshape(4 * BLK, ref.shape[2])


def _pair_rows(ref, g):
    return jnp.concatenate([jnp.concatenate([ref[4 * g + 2 * c], ref[4 * g + 2 * c + 1]], axis=1) for c in range(2)],
                           axis=0)


def _mask_heads(src_ref, dst_ref):
    lo = _lane_lo(BLK)
    for j in range(4):
        chunk = src_ref[:, 128 * j:128 * (j + 1)]
        dst_ref[2 * j] = jnp.where(lo, chunk, 0.0).astype(BF16)
        dst_ref[2 * j + 1] = jnp.where(lo, 0.0, chunk).astype(BF16)


def _mix_stage_attn(proj_ref, bias_ref, sinks_ref, n, s):
    _mask_heads(proj_ref, s["qm"])
    for g in range(2):
        s["lg"][g] = _dot_nt(_group_rows(s["qm"], g), s["kd"][g])
    n0mask = (n == 0) & (lax.broadcasted_iota(jnp.int32, (HALF, 2 * BLK), 1) < BLK)
    lane = lax.broadcasted_iota(jnp.int32, (HALF, 128), 1)
    for hf in range(BLK // HALF):
        rows = slice(HALF * hf, HALF * (hf + 1))
        psink = jnp.zeros((HALF, 128), F32)
        for h in range(N_HEADS):
            sk = sinks_ref[h]
            grows = slice(BLK * (h % 4) + HALF * hf, BLK * (h % 4) + HALF * (hf + 1))
            logit = s["lg"][h // 4, grows, :] * (HEAD_DIM ** -0.5) + bias_ref[h, rows, :]
            logit = jnp.where(n0mask, NEG_INF, logit)
            m = jnp.maximum(jnp.max(logit, axis=1, keepdims=True), sk)
            e = jnp.exp(logit - m)
            es = jnp.exp(sk - m)
            inv = 1.0 / (jnp.sum(e, axis=1, keepdims=True) + es)
            p = e * inv
            s["p"][h, rows, :] = p
            s["pb"][h, rows, :] = p.astype(BF16)
            psink = jnp.where(lane == h, es * inv, psink)
        s["psink"][rows, :] = psink
    for g in range(2):
        out = _dot(_pair_rows(s["pb"], g), s["vm"][g])
        s["attn"][:, 256 * g:256 * g + 128] = out[0:BLK]
        s["attn"][:, 256 * g + 128:256 * g + 256] = out[BLK:2 * BLK]


def _mix_stage_gmlp_pre(proj_ref, lng, lnb, s, keep):
    c0 = ATTN_W + 2 * KV_W
    for r0 in range(0, BLK, ROWS):
        rows = slice(r0, r0 + ROWS)
        u, du = _gelu_parts(proj_ref[rows, c0:c0 + GMLP_W])
        a, da = _gelu_parts(proj_ref[rows, c0 + GMLP_W:c0 + 2 * GMLP_W])
        ac = a - _seg_mean64(a)
        rstd = lax.rsqrt(_seg_mean64(ac * ac) + LN_EPS)
        vhat = ac * rstd
        s["u"][rows, :] = u
        s["vnb"][rows, :] = (vhat * lng + lnb).astype(BF16)
        if keep:
            s["du"][rows, :] = du
            s["da"][rows, :] = da
            s["vhat"][rows, :] = vhat
            s["rstd"][rows, :] = rstd


def _stack_halves(chunk):
    lo = _lane_lo(BLK)
    zero = jnp.zeros_like(chunk)
    return jnp.concatenate([jnp.where(lo, chunk, zero), jnp.where(lo, zero, chunk)], axis=0)


def _mix_stage_gmlp_mix(ws2_ref, bfull_ref, s):
    for j in range(4):
        cols = slice(128 * j, 128 * (j + 1))
        s["ms"][:, cols] = _dot(ws2_ref[j], _stack_halves(s["vnb"][:, cols])) + bfull_ref[:, cols]


def _mix_scratch(keep):
    f32 = lambda *shape: pltpu.VMEM(shape, F32)
    b16 = lambda *shape: pltpu.VMEM(shape, BF16)
    names = dict(kd=b16(2, 2 * BLK, 128), vd=b16(2, 2 * BLK, 128), km=b16(2, 4 * BLK, 128), vm=b16(2, 4 * BLK, 128),
                 qm=b16(N_HEADS, BLK, 128), lg=f32(2, 4 * BLK, 2 * BLK), pb=b16(N_HEADS, BLK, 2 * BLK),
                 u=f32(BLK, GMLP_W), vnb=b16(BLK, GMLP_W), ms=f32(BLK, GMLP_W))
    if keep:
        names.update(dom=b16(N_HEADS, BLK, 128), dls=b16(N_HEADS, BLK, 2 * BLK),
                     dattn=f32(BLK, ATTN_W), dmix=f32(BLK, D_MODEL), du=f32(BLK, GMLP_W), da=f32(BLK, GMLP_W),
                     vhat=f32(BLK, GMLP_W), rstd=f32(BLK, GMLP_W), dmsb=b16(BLK, GMLP_W), dvn=f32(BLK, GMLP_W))
    return list(names), list(names.values())


def _mix_specs(with_logit_inputs):
    logit_inputs = [_full((N_HEADS, BLK, 2 * BLK)), pl.BlockSpec(memory_space=pltpu.SMEM)] if with_logit_inputs else []
    return [pl.BlockSpec((BLK, IN_W), lambda n: (n, 0)),
            pl.BlockSpec((BLK, 2 * KV_W), lambda n: (jnp.maximum(n - 1, 0), ATTN_W // (2 * KV_W)))] + logit_inputs + [
            _full((1, GMLP_W)), _full((1, GMLP_W)),
            _full((N_GROUPS // 2, BLK, 2 * BLK)), _full((BLK, GMLP_W)),
            _full((1, ATTN_W)), _full((1, GMLP_W))]


KEPT = [("p", (N_HEADS, BLK, 2 * BLK), F32), ("psink", (BLK, 128), F32), ("attn", (BLK, ATTN_W), F32)]


def _kept_shapes(t):
    full = lambda blk: (blk[0], t, blk[2]) if len(blk) == 3 else (t, blk[1])
    return [jax.ShapeDtypeStruct(full(blk), dt) for _, blk, dt in KEPT]


def _kept_specs():
    return [pl.BlockSpec(blk, (lambda n: (0, n, 0)) if len(blk) == 3 else (lambda n: (n, 0))) for _, blk, _ in KEPT]


def _mix_fwd(proj, bias, sinks, lng, lnb, ws2, bfull, aog, gog, comm):
    t = proj.shape[0]
    names, shapes = _mix_scratch(False)

    def body(proj_ref, kvp_ref, bias_ref, sinks_ref, lng_ref, lnb_ref, ws2_ref, bfull_ref, aog_ref, gog_ref,
             out_ref, *rest):
        s = dict(zip([name for name, _, _ in KEPT] + names, rest))
        n = pl.program_id(0)
        _mix_stage_kv(proj_ref, kvp_ref, s)
        _mix_stage_attn(proj_ref, bias_ref, sinks_ref, n, s)
        _mix_stage_gmlp_pre(proj_ref, lng_ref[...], lnb_ref[...], s, False)
        _mix_stage_gmlp_mix(ws2_ref, bfull_ref, s)
        for r0 in range(0, BLK, ROWS):
            rows = slice(r0, r0 + ROWS)
            out_ref[rows, 0:ATTN_W] = _rms(s["attn"][rows, :], aog_ref[...])[0].astype(BF16)
            out_ref[rows, ATTN_W:ATTN_W + GMLP_W] = _rms(s["u"][rows, :] * s["ms"][rows, :], gog_ref[...])[0].astype(BF16)

    return _call(
        body, name="mix_fwd", grid=(t // BLK,),
        out_shape=[jax.ShapeDtypeStruct((t, D_MODEL), BF16)] + _kept_shapes(t),
        in_specs=_mix_specs(True), out_specs=[pl.BlockSpec((BLK, D_MODEL), lambda n: (n, 0))] + _kept_specs(),
        scratch_shapes=shapes,
        sem=("parallel",), comm=comm, args=(proj, proj, bias, sinks, lng, lnb, ws2, bfull, aog, gog))


def _mix_bwd(proj, lng, lnb, ws2, wst2, bfull, aog, gog, dy, w_out, kept, comm):
    t = proj.shape[0]
    nb = t // BLK
    names, shapes = _mix_scratch(True)
    c_gu = ATTN_W + 2 * KV_W

    def body(proj_ref, kvp_ref, lng_ref, lnb_ref, ws2_ref, bfull_ref, aog_ref, gog_ref,
             wst2_ref, dy_ref, wout_ref, *rest):
        n_kept = len(KEPT)
        s = dict(zip([name for name, _, _ in KEPT], rest[:n_kept]))
        (dproj_ref, dkvn_ref, dl_ref, dsink_ref, dlng_ref, dlnb_ref, dws_ref, dbs_ref, daog_ref,
         dgog_ref) = rest[n_kept:n_kept + 10]
        s.update(zip(names, rest[n_kept + 10:]))
        n = pl.program_id(0)

        @pl.when(n == 0)
        def _():
            for r in (dl_ref, dsink_ref, dlng_ref, dlnb_ref, dws_ref, dbs_ref, daog_ref, dgog_ref):
                r[...] = jnp.zeros_like(r)

        s["dmix"][...] = _dot(dy_ref[...], wout_ref[...])
        _mix_stage_kv(proj_ref, kvp_ref, s)
        _mask_heads(proj_ref, s["qm"])
        lng = lng_ref[...]
        _mix_stage_gmlp_pre(proj_ref, lng, lnb_ref[...], s, True)
        _mix_stage_gmlp_mix(ws2_ref, bfull_ref, s)

        aog, gog = aog_ref[...], gog_ref[...]
        for r0 in range(0, BLK, ROWS):
            rows = slice(r0, r0 + ROWS)
            attn, dma = s["attn"][rows, :], s["dmix"][rows, 0:ATTN_W]
            _, r_a = _rms(attn, aog)
            daog_ref[...] += _rowsum8(dma * attn * r_a)
            s["dattn"][rows, :] = _rms_bwd(dma, attn, r_a, aog)
            u, ms, dmg = s["u"][rows, :], s["ms"][rows, :], s["dmix"][rows, ATTN_W:ATTN_W + GMLP_W]
            gm = u * ms
            _, r_g = _rms(gm, gog)
            dgog_ref[...] += _rowsum8(dmg * gm * r_g)
            dgm = _rms_bwd(dmg, gm, r_g, gog)
            dproj_ref[rows, c_gu:c_gu + GMLP_W] = dgm * ms * s["du"][rows, :]
            dms = dgm * u
            dbs_ref[rows, :] += dms
            s["dmsb"][rows, :] = dms.astype(BF16)

        _mask_heads(s["dattn"], s["dom"])
        for g in range(2):
            s["lg"][g] = _dot_nt(_group_rows(s["dom"], g), s["vd"][g])
        lane = lax.broadcasted_iota(jnp.int32, (HALF, 128), 1)
        for hf in range(BLK // HALF):
            rows = slice(HALF * hf, HALF * (hf + 1))
            dsink = jnp.zeros((HALF, 128), F32)
            for h in range(N_HEADS):
                grows = slice(BLK * (h % 4) + HALF * hf, BLK * (h % 4) + HALF * (hf + 1))
                dp = s["lg"][h // 4, grows, :]
                p = s["p"][h, rows, :]
                s["pb"][h, rows, :] = p.astype(BF16)
                rs = jnp.sum(p * dp, axis=1, keepdims=True)
                dl = p * (dp - rs)
                dl_ref[h, rows, :] += dl
                dsink = dsink + jnp.where(lane == h, -s["psink"][rows, :] * rs, 0.0)
                s["dls"][h, rows, :] = (dl * (HEAD_DIM ** -0.5)).astype(BF16)
            dsink_ref[rows, :] += dsink
        for g in range(2):
            dq = _dot(_pair_rows(s["dls"], g), s["km"][g])
            dproj_ref[:, 256 * g:256 * g + 128] = dq[0:BLK]
            dproj_ref[:, 256 * g + 128:256 * g + 256] = dq[BLK:2 * BLK]
        lo_k = _lane_lo(2 * BLK)
        for col, lhs, rhs in ((0, "dls", "qm"), (KV_W, "pb", "dom")):
            raw = [_dot_tn(_group_rows(s[lhs], g), _group_rows(s[rhs], g)) for g in range(2)]
            both = [r + pltpu.roll(r, 64, 1) for r in raw]
            dkv = jnp.where(lo_k, both[0], both[1])
            dproj_ref[:, ATTN_W + col:ATTN_W + col + KV_W] = dkv[BLK:2 * BLK]
            dkvn_ref[:, col:col + KV_W] = dkv[0:BLK]

        for j in range(4):
            cols = slice(128 * j, 128 * (j + 1))
            dm2 = _stack_halves(s["dmsb"][:, cols])
            vnb = s["vnb"][:, cols]
            dws2 = _dot_nt(dm2, vnb)
            dws_ref[2 * j] += dws2[0:BLK]
            dws_ref[2 * j + 1] += dws2[BLK:2 * BLK]
            s["dvn"][:, cols] = _dot(wst2_ref[j], dm2)
        for r0 in range(0, BLK, ROWS):
            rows = slice(r0, r0 + ROWS)
            dvn, vhat = s["dvn"][rows, :], s["vhat"][rows, :]
            dlng_ref[...] += _rowsum8(dvn * vhat)
            dlnb_ref[...] += _rowsum8(dvn)
            dvh = dvn * lng
            dact = s["rstd"][rows, :] * (dvh - _seg_mean64(dvh) - vhat * _seg_mean64(dvh * vhat))
            dproj_ref[rows, c_gu + GMLP_W:IN_W] = dact * s["da"][rows, :]

    acc8 = lambda w: jax.ShapeDtypeStruct((8, w), F32)
    out_shape = [jax.ShapeDtypeStruct((t, IN_W), F32), jax.ShapeDtypeStruct((t, 2 * KV_W), F32),
                 jax.ShapeDtypeStruct((N_HEADS, BLK, 2 * BLK), F32), jax.ShapeDtypeStruct((BLK, 128), F32),
                 acc8(GMLP_W), acc8(GMLP_W), jax.ShapeDtypeStruct((N_GROUPS, BLK, BLK), F32),
                 jax.ShapeDtypeStruct((BLK, GMLP_W), F32), acc8(ATTN_W), acc8(GMLP_W)]
    out_specs = [pl.BlockSpec((BLK, IN_W), lambda n: (n, 0)),
                 pl.BlockSpec((BLK, 2 * KV_W), lambda n: ((n + nb - 1) % nb, 0)),
                 _full((N_HEADS, BLK, 2 * BLK)), _full((BLK, 128)), _full((8, GMLP_W)), _full((8, GMLP_W)),
                 _full((N_GROUPS, BLK, BLK)), _full((BLK, GMLP_W)), _full((8, ATTN_W)), _full((8, GMLP_W))]
    in_specs = _mix_specs(False) + [_full((N_GROUPS // 2, BLK, 2 * BLK)),
                               pl.BlockSpec((BLK, D_MODEL), lambda n: (n, 0)),
                               _full((D_MODEL, D_MODEL))] + _kept_specs()
    return _call(
        body, name="mix_bwd", grid=(nb,), out_shape=out_shape, in_specs=in_specs, out_specs=out_specs,
        scratch_shapes=shapes, sem=("arbitrary",), comm=comm,
        args=(proj, proj, lng, lnb, ws2, bfull, aog, gog, wst2, dy, w_out, *kept))


def _outproj(mixed, w_out, x, g1, ln1g, ln1b, sc2, sh2, tm, comm):
    t, d = x.shape

    def body(mx_ref, w_ref, x_ref, g1_ref, lg_ref, lb_ref, sc_ref, sh_ref, y_ref, x1_ref, h2_ref):
        y = _dot(mx_ref[...], w_ref[...])
        xhat, _ = _ln_stats(ALPHA * x_ref[...] + g1_ref[...] * y)
        x1 = xhat * lg_ref[...] + lb_ref[...]
        y_ref[...] = y
        x1_ref[...] = x1
        h2_ref[...] = (x1 * (1.0 + sc_ref[...]) + sh_ref[...]).astype(BF16)

    row = pl.BlockSpec((tm, d), lambda i: (i, 0))
    vec = _full((1, d))
    return _call(
        body, name="outproj", grid=(t // tm,),
        out_shape=[jax.ShapeDtypeStruct((t, d), F32), jax.ShapeDtypeStruct((t, d), F32),
                   jax.ShapeDtypeStruct((t, d), BF16)],
        in_specs=[row, _full((d, d)), row, vec, vec, vec, vec, vec], out_specs=[row, row, row],
        sem=("parallel",), comm=comm, args=(mixed, w_out, x, g1, ln1g, ln1b, sc2, sh2))


def _ffn_fwd(h2, w_gu_t, w_down, x1, target, g2, ln2g, ln2b, tm):
    t, d = x1.shape

    def body(h_ref, w_ref, wd_ref, x1_ref, tg_ref, g2_ref, lg_ref, lb_ref,
             dsu_ref, sg_ref, act_ref, dz_ref, dy_ref, loss_ref, dlg_ref, dlb_ref, dg2_ref):
        @pl.when(pl.program_id(0) == 0)
        def _():
            for r in (loss_ref, dlg_ref, dlb_ref, dg2_ref):
                r[...] = jnp.zeros_like(r)

        h = h_ref[...]
        g = _dot_nt(h, w_ref[0:D_FF])
        u = _dot_nt(h, w_ref[D_FF:2 * D_FF])
        s = _sigmoid(g)
        sg = g * s
        act = (sg * u).astype(BF16)
        dsu_ref[...] = (u * (s * (1.0 + g * (1.0 - s)))).astype(BF16)
        sg_ref[...] = sg.astype(BF16)
        act_ref[...] = act
        y2 = _dot(act, wd_ref[...])
        g2 = g2_ref[...]
        lg = lg_ref[...]
        xhat, rstd = _ln_stats(ALPHA * x1_ref[...] + g2 * y2)
        err = xhat * lg + lb_ref[...] - tg_ref[...]
        loss_ref[...] += _rowsum8(err * err)
        dx2 = err * (1.0 / d)
        dlg_ref[...] += _rowsum8(dx2 * xhat)
        dlb_ref[...] += _rowsum8(dx2)
        dz = _ln_bwd(dx2 * lg, xhat, rstd)
        dg2_ref[...] += _rowsum8(dz * y2)
        dz_ref[...] = dz
        dy_ref[...] = (g2 * dz).astype(BF16)

    row = pl.BlockSpec((tm, d), lambda i: (i, 0))
    wide = pl.BlockSpec((tm, D_FF), lambda i: (i, 0))
    vec = _full((1, d))
    acc = _full((8, d))
    acc_shape = jax.ShapeDtypeStruct((8, d), F32)
    wide_shape = jax.ShapeDtypeStruct((t, D_FF), BF16)
    return pl.pallas_call(
        body, name="ffn_fwd", grid=(t // tm,),
        out_shape=[wide_shape] * 3 + [jax.ShapeDtypeStruct((t, d), F32), jax.ShapeDtypeStruct((t, d), BF16)]
        + [acc_shape] * 4,
        in_specs=[row, _resident((2 * D_FF, d)), _resident((D_FF, d)), row, row, vec, vec, vec],
        out_specs=[wide] * 3 + [row, row, acc, acc, acc, acc], compiler_params=_params(("arbitrary",)),
    )(h2, w_gu_t, w_down, x1, target, g2, ln2g, ln2b)


def _resident(shape):
    nd = len(shape)
    return pl.BlockSpec(shape, lambda *_: (0,) * nd, pipeline_mode=pl.Buffered(1))


def _ffn_bwd(dy2, w_down, dsu, sg, w_gu_t, x1, x, y, dz2, sc2, g1, ln1g, tm, comm):
    t, d = x1.shape

    def body(dy2_ref, wd_ref, dsu_ref, sg_ref, w_ref, x1_ref, x_ref, y_ref, dz2_ref, sc_ref, g1_ref, lg_ref,
             dg_ref, du_ref, dz1_ref, dy_ref, dsc_ref, dsh_ref, dlg_ref, dlb_ref, dg1_ref):
        @pl.when(pl.program_id(0) == 0)
        def _():
            for r in (dsc_ref, dsh_ref, dlg_ref, dlb_ref, dg1_ref):
                r[...] = jnp.zeros_like(r)

        dact = _dot_nt(dy2_ref[...], wd_ref[...])
        dg = (dact * dsu_ref[...].astype(F32)).astype(BF16)
        du = (dact * sg_ref[...].astype(F32)).astype(BF16)
        dg_ref[...] = dg
        du_ref[...] = du
        dh2 = _dot(dg, w_ref[0:D_FF]) + _dot(du, w_ref[D_FF:2 * D_FF])
        x1 = x1_ref[...]
        y = y_ref[...]
        g1 = g1_ref[...]
        dsc_ref[...] += _rowsum8(dh2 * x1)
        dsh_ref[...] += _rowsum8(dh2)
        dx1 = dh2 * (1.0 + sc_ref[...]) + ALPHA * dz2_ref[...]
        xhat, rstd = _ln_stats(ALPHA * x_ref[...] + g1 * y)
        dlg_ref[...] += _rowsum8(dx1 * xhat)
        dlb_ref[...] += _rowsum8(dx1)
        dz1 = _ln_bwd(dx1 * lg_ref[...], xhat, rstd)
        dg1_ref[...] += _rowsum8(dz1 * y)
        dz1_ref[...] = dz1
        dy_ref[...] = (g1 * dz1).astype(BF16)

    row = pl.BlockSpec((tm, d), lambda i: (i, 0))
    wide = pl.BlockSpec((tm, D_FF), lambda i: (i, 0))
    vec = _full((1, d))
    acc = _full((8, d))
    acc_shape = jax.ShapeDtypeStruct((8, d), F32)
    wide_shape = jax.ShapeDtypeStruct((t, D_FF), BF16)
    return _call(
        body, name="ffn_bwd", grid=(t // tm,),
        out_shape=[wide_shape, wide_shape, jax.ShapeDtypeStruct((t, d), F32), jax.ShapeDtypeStruct((t, d), BF16)]
        + [acc_shape] * 5,
        in_specs=[row, _resident((D_FF, d)), wide, wide, _resident((2 * D_FF, d)), row, row, row, row, vec, vec, vec],
        out_specs=[wide, wide, row, row, acc, acc, acc, acc, acc], sem=("arbitrary",), comm=comm,
        args=(dy2, w_down, dsu, sg, w_gu_t, x1, x, y, dz2, sc2, g1, ln1g))


def _din(dproj, dkvn, w_in_t, x, dz1, sc1, tm, comm):
    t, d = x.shape

    def body(dp_ref, dkv_ref, w_ref, x_ref, dz1_ref, sc_ref, dx_ref, dpb_ref, dbin_ref, dsc_ref, dsh_ref):
        @pl.when(pl.program_id(0) == 0)
        def _():
            for r in (dbin_ref, dsc_ref, dsh_ref):
                r[...] = jnp.zeros_like(r)

        dp = jnp.concatenate([dp_ref[:, 0:ATTN_W], dp_ref[:, ATTN_W:ATTN_W + 2 * KV_W] + dkv_ref[...],
                              dp_ref[:, ATTN_W + 2 * KV_W:IN_W]], axis=1)
        dbin_ref[...] += _rowsum8(dp)
        dpb = dp.astype(BF16)
        dpb_ref[...] = dpb
        dh = _dot(dpb, w_ref[...])
        dsc_ref[...] += _rowsum8(dh * x_ref[...])
        dsh_ref[...] += _rowsum8(dh)
        dx_ref[...] = dh * (1.0 + sc_ref[...]) + ALPHA * dz1_ref[...]

    row = lambda w: pl.BlockSpec((tm, w), lambda i: (i, 0))
    return _call(
        body, name="din", grid=(t // tm,),
        out_shape=[jax.ShapeDtypeStruct((t, d), F32), jax.ShapeDtypeStruct((t, IN_W), BF16),
                   jax.ShapeDtypeStruct((8, IN_W), F32), jax.ShapeDtypeStruct((8, d), F32),
                   jax.ShapeDtypeStruct((8, d), F32)],
        in_specs=[row(IN_W), row(2 * KV_W), _full((IN_W, d)), row(d), row(d), _full((1, d))],
        out_specs=[row(d), row(IN_W), _full((8, IN_W)), _full((8, d)), _full((8, d))],
        sem=("arbitrary",), comm=comm, args=(dproj, dkvn, w_in_t, x, dz1, sc1))


def _wgrad(name, a, b, tmm, tk, comm=None, a2=None):
    t, m = a.shape
    n = b.shape[1]
    nk = t // tk
    nm = m // tmm

    def body(*refs):
        a_refs, (b_ref, o_ref, acc_ref) = refs[:-3], refs[-3:]
        i, k = pl.program_id(0), pl.program_id(1)

        @pl.when(k == 0)
        def _():
            acc_ref[...] = jnp.zeros_like(acc_ref)

        a_tile = a_refs[0][...] if a2 is None else jnp.where(i < nm, a_refs[0][...], a_refs[1][...])
        acc_ref[...] += _dot_tn(a_tile, b_ref[...])

        @pl.when(k == nk - 1)
        def _():
            o_ref[...] = acc_ref[...].astype(BF16)

    if a2 is None:
        a_specs, a_args, n_tiles = [pl.BlockSpec((tk, tmm), lambda i, k: (k, i))], (a,), nm
    else:
        a_specs = [pl.BlockSpec((tk, tmm), lambda i, k: (jnp.where(i < nm, k, 0), jnp.minimum(i, nm - 1))),
                   pl.BlockSpec((tk, tmm), lambda i, k: (jnp.where(i < nm, 0, k), jnp.maximum(i - nm, 0)))]
        a_args, n_tiles = (a, a2), 2 * nm
    (out,), got = _call(
        body, name=name, grid=(n_tiles, nk), out_shape=[jax.ShapeDtypeStruct((n_tiles * tmm, n), BF16)],
        in_specs=a_specs + [pl.BlockSpec((tk, n), lambda i, k: (k, 0))],
        out_specs=[pl.BlockSpec((tmm, n), lambda i, k: (i, 0))],
        scratch_shapes=[pltpu.VMEM((tmm, n), F32)], sem=("parallel", "arbitrary"), comm=comm, args=a_args + (b,))
    return out if comm is None else (out, got)


def _adamw(w, g, m, v):
    m = ADAM_B1 * m + (1.0 - ADAM_B1) * g
    v = ADAM_B2 * v + (1.0 - ADAM_B2) * (g * g)
    m_hat = m / (1.0 - ADAM_B1 ** ADAM_STEP)
    v_hat = v / (1.0 - ADAM_B2 ** ADAM_STEP)
    delta = -ADAM_LR * (m_hat / (jnp.sqrt(v_hat) + ADAM_EPS) + ADAM_WD * w)
    return delta, m, v


def _adam_reduce(name, parts, w, m, v, tr):
    r, cdim = w.shape
    n_slots = parts.shape[0]

    def body(p_ref, w_ref, m_ref, v_ref, g_ref, d_ref, mo_ref, vo_ref):
        g = p_ref[0].astype(F32)
        for s in range(1, n_slots):
            g = g + p_ref[s].astype(F32)
        d_ref[...], mo_ref[...], vo_ref[...] = _adamw(w_ref[...], g, m_ref[...], v_ref[...])
        g_ref[...] = g

    tile = pl.BlockSpec((tr, cdim), lambda i: (i, 0))
    shp = jax.ShapeDtypeStruct((r, cdim), F32)
    return pl.pallas_call(
        body, name=name, grid=(r // tr,), out_shape=[shp] * 4,
        in_specs=[pl.BlockSpec((n_slots, tr, cdim), lambda i: (0, i, 0)), tile, tile, tile],
        out_specs=[tile] * 4, compiler_params=_params(("parallel",)),
    )(parts, w, m, v)


def _adam_w_ada(c_all_t, dmod_cols, w, m, v):
    def body(ct_ref, dm_ref, w_ref, m_ref, v_ref, g_ref, d_ref, mo_ref, vo_ref):
        ct = ct_ref[...]
        s = (ct * _sigmoid(ct)).astype(BF16)
        g = _dot(s, dm_ref[...].astype(BF16))
        d_ref[...], mo_ref[...], vo_ref[...] = _adamw(w_ref[...], g, m_ref[...], v_ref[...])
        g_ref[...] = g

    shp = jax.ShapeDtypeStruct(w.shape, F32)
    return pl.pallas_call(
        body, name="adam_w_ada", grid=(1,), out_shape=[shp] * 4,
        in_specs=[_full(c_all_t.shape), _full(dmod_cols.shape)] + [_full(w.shape)] * 3,
        out_specs=[_full(w.shape)] * 4, compiler_params=_params(("arbitrary",)),
    )(c_all_t, dmod_cols, w, m, v)


SMALL_EARLY = ["rel_bias", "attn_sinks", "gmlp_ln_g", "gmlp_ln_b", "gmlp_w_s", "gmlp_b_s",
               "attn_out_g", "gmlp_out_g", "ln1_g", "ln1_b", "ln2_g", "ln2_b"]
SMALL_LATE = ["b_ada", "b_in", "loss"]
WEIGHTS = ["rel_bias", "w_ada", "b_ada", "w_in", "b_in", "attn_sinks", "gmlp_ln_g", "gmlp_ln_b", "gmlp_w_s",
           "gmlp_b_s", "attn_out_g", "gmlp_out_g", "w_out", "ln1_g", "ln1_b", "w_gate_up", "w_down", "ln2_g", "ln2_b"]


def _seg_rows(nelem):
    return -(-nelem // 1024) * 8


def _pack(named, names):
    parts = []
    for name in names:
        flat = named[name].reshape(-1).astype(F32)
        rows = _seg_rows(flat.shape[0])
        parts.append(jnp.pad(flat, (0, rows * 128 - flat.shape[0])).reshape(rows, 128))
    return jnp.concatenate(parts, axis=0)


def _adam_small(name, parts, names, wts, mom_m, mom_v):
    params = [n for n in names if n in wts]

    def view(n):
        nelem = math.prod(wts[n].shape)
        return (nelem // 128, 128) if nelem % 128 == 0 else (1, nelem)

    offsets, r0 = {}, 0
    for n in names:
        offsets[n] = r0
        r0 += _seg_rows(math.prod(wts[n].shape) if n in wts else 1)

    def body(*refs):
        p_ref, ins, outs = refs[0], refs[1:1 + 3 * len(params)], refs[1 + 3 * len(params):]

        def total(n, rows, lanes):
            o = offsets[n]
            g = p_ref[0, o:o + rows, 0:lanes]
            for s in range(1, N_DEV):
                g = g + p_ref[s, o:o + rows, 0:lanes]
            return g

        for i, n in enumerate(params):
            g = total(n, *view(n))
            w_ref, m_ref, v_ref = ins[3 * i:3 * i + 3]
            g_ref, d_ref, mo_ref, vo_ref = outs[4 * i:4 * i + 4]
            d_ref[...], mo_ref[...], vo_ref[...] = _adamw(w_ref[...], g, m_ref[...], v_ref[...])
            g_ref[...] = g
        for j, n in enumerate(n for n in names if n not in wts):
            outs[4 * len(params) + j][...] = total(n, 8, 128)

    args, in_specs, out_shape = [parts], [_full(parts.shape)], []
    for n in params:
        args += [t[n].reshape(view(n)) for t in (wts, mom_m, mom_v)]
        in_specs += [_full(view(n))] * 3
        out_shape += [jax.ShapeDtypeStruct(view(n), F32)] * 4
    out_shape += [jax.ShapeDtypeStruct((8, 128), F32) for n in names if n not in wts]
    res = pl.pallas_call(
        body, name=name, grid=(1,), out_shape=out_shape, in_specs=in_specs,
        out_specs=[_full(s.shape) for s in out_shape], compiler_params=_params(("arbitrary",)),
    )(*args)
    done = {n: tuple(r.reshape(wts[n].shape) for r in res[4 * i:4 * i + 4]) for i, n in enumerate(params)}
    sums = {n: res[4 * len(params) + j] for j, n in enumerate(n for n in names if n not in wts)}
    return done, sums


def _t5_bucket_map():
    qi = jnp.arange(BLK)[:, None]
    si = jnp.arange(2 * BLK)[None, :]
    n = jnp.maximum(qi + BLK - si, 0)
    max_exact = N_BUCKETS // 2
    nf = jnp.maximum(n, max_exact).astype(F32)
    large = max_exact + (jnp.log(nf / max_exact) / math.log(MAX_DISTANCE / max_exact)
                         * (N_BUCKETS - max_exact)).astype(jnp.int32)
    large = jnp.minimum(large, N_BUCKETS - 1)
    return jnp.where(n < max_exact, n, large).astype(jnp.int32)


def kernel(x, c, rel_bias, w_ada, b_ada, w_in, b_in, attn_sinks, gmlp_ln_g, gmlp_ln_b, gmlp_w_s, gmlp_b_s, attn_out_g, gmlp_out_g, w_out, ln1_g, ln1_b, w_gate_up, w_down, ln2_g, ln2_b, loss_target, m_rel_bias, m_w_ada, m_b_ada, m_w_in, m_b_in, m_attn_sinks, m_gmlp_ln_g, m_gmlp_ln_b, m_gmlp_w_s, m_gmlp_b_s, m_attn_out_g, m_gmlp_out_g, m_w_out, m_ln1_g, m_ln1_b, m_w_gate_up, m_w_down, m_ln2_g, m_ln2_b, v_rel_bias, v_w_ada, v_b_ada, v_w_in, v_b_in, v_attn_sinks, v_gmlp_ln_g, v_gmlp_ln_b, v_gmlp_w_s, v_gmlp_b_s, v_attn_out_g, v_gmlp_out_g, v_w_out, v_ln1_g, v_ln1_b, v_w_gate_up, v_w_down, v_ln2_g, v_ln2_b):
    wts = dict(rel_bias=rel_bias, w_ada=w_ada, b_ada=b_ada, w_in=w_in, b_in=b_in, attn_sinks=attn_sinks,
               gmlp_ln_g=gmlp_ln_g, gmlp_ln_b=gmlp_ln_b, gmlp_w_s=gmlp_w_s, gmlp_b_s=gmlp_b_s,
               attn_out_g=attn_out_g, gmlp_out_g=gmlp_out_g, w_out=w_out, ln1_g=ln1_g, ln1_b=ln1_b,
               w_gate_up=w_gate_up, w_down=w_down, ln2_g=ln2_g, ln2_b=ln2_b)
    mom_m = dict(rel_bias=m_rel_bias, w_ada=m_w_ada, b_ada=m_b_ada, w_in=m_w_in, b_in=m_b_in,
                 attn_sinks=m_attn_sinks, gmlp_ln_g=m_gmlp_ln_g, gmlp_ln_b=m_gmlp_ln_b, gmlp_w_s=m_gmlp_w_s,
                 gmlp_b_s=m_gmlp_b_s, attn_out_g=m_attn_out_g, gmlp_out_g=m_gmlp_out_g, w_out=m_w_out,
                 ln1_g=m_ln1_g, ln1_b=m_ln1_b, w_gate_up=m_w_gate_up, w_down=m_w_down, ln2_g=m_ln2_g,
                 ln2_b=m_ln2_b)
    mom_v = dict(rel_bias=v_rel_bias, w_ada=v_w_ada, b_ada=v_b_ada, w_in=v_w_in, b_in=v_b_in,
                 attn_sinks=v_attn_sinks, gmlp_ln_g=v_gmlp_ln_g, gmlp_ln_b=v_gmlp_ln_b, gmlp_w_s=v_gmlp_w_s,
                 gmlp_b_s=v_gmlp_b_s, attn_out_g=v_attn_out_g, gmlp_out_g=v_gmlp_out_g, w_out=v_w_out,
                 ln1_g=v_ln1_g, ln1_b=v_ln1_b, w_gate_up=v_w_gate_up, w_down=v_w_down, ln2_g=v_ln2_g,
                 ln2_b=v_ln2_b)

    t = x.shape[1]
    tm = min(512, t)
    tn_ff = D_FF // 2
    tk_long, tk_short = min(4096, t), min(2048, t)
    me = 4 * lax.axis_index("x") + 2 * lax.axis_index("y") + lax.axis_index("c")
    xs = x[0]
    target = loss_target[0]

    (c_g,) = _exchange("gather_c", [jnp.broadcast_to(c, (8, D_MODEL))], ("gather",))
    c_all = c_g[:, 0, :]

    ncol = w_ada.shape[2]
    b_cols = lax.dynamic_slice(b_ada, (0, me * ncol), (1, ncol))
    mod_part = _mod_partial(c_all, w_ada[0], b_cols)
    bucket = _t5_bucket_map()
    (bias,), (mod_g, w_in_g) = _bias_table(rel_bias, bucket,
                                           comm=([mod_part, w_in[0].T.astype(BF16)], ("gather", "gather2")))
    w_in_t = w_in_g.reshape(IN_W, D_MODEL)
    mod = lax.dynamic_slice(mod_g, (0, me, 0), (N_DEV, 1, ncol)).reshape(1, N_DEV * ncol)
    sh1, sc1, g1, sh2, sc2, g2 = [mod[:, i * D_MODEL:(i + 1) * D_MODEL] for i in range(6)]

    causal = jnp.tril(jnp.ones((BLK, BLK), dtype=bool))
    ws = jnp.where(causal[None], gmlp_w_s[0], 0.0).astype(BF16)
    pair = lambda w: jnp.concatenate([w[0::2], w[1::2]], axis=2)
    ws2, wst2 = pair(ws), pair(jnp.swapaxes(ws, 1, 2))
    bfull = jnp.repeat(gmlp_b_s[0].T, GMLP_W // N_GROUPS, axis=1)
    sinks = attn_sinks[0]

    (proj, h1), (w_down_g,) = _inproj(xs, sc1, sh1, w_in_t, b_in, tm, comm=([w_down[0].astype(BF16)], ("gather2",)))
    (mixed, *kept), (w_out_g, w_gu_g) = _mix_fwd(
        proj, bias, sinks, gmlp_ln_g, gmlp_ln_b, ws2, bfull, attn_out_g, gmlp_out_g,
        comm=([w_out[0].astype(BF16), w_gate_up[0].T.astype(BF16)], ("gather2", "gather2")))
    w_out_f = w_out_g.reshape(D_MODEL, D_MODEL)
    w_gu_t = w_gu_g.reshape(2 * D_FF, D_MODEL)
    (y1, x1, h2), _ = _outproj(mixed, w_out_f, xs, g1, ln1_g, ln1_b, sc2, sh2, tm, comm=None)
    w_down_f = w_down_g.reshape(D_FF, D_MODEL)
    dsu, sg, act, dz2, dy2, loss_p, d_ln2g, d_ln2b, d_g2 = _ffn_fwd(h2, w_gu_t, w_down_f, x1, target, g2, ln2_g, ln2_b,
                                                                    min(256, t))

    slots = lambda a: a.reshape(N_DEV, -1, D_MODEL)
    dw_down = _wgrad("wgrad_down", act, dy2, tn_ff, tk_short)
    (dgate, dup, dz1, dy1, d_sc2, d_sh2, d_ln1g, d_ln1b, d_g1), (r_down,) = _ffn_bwd(
        dy2, w_down_f, dsu, sg, w_gu_t, x1, xs, y1, dz2, sc2, g1, ln1_g, min(256, t),
        comm=([slots(dw_down)], ("scatter",)))
    dw_gu_t = _wgrad("wgrad_gate_up", dgate, h2, tn_ff, tk_short, a2=dup)
    dw_out = _wgrad("wgrad_out", mixed, dy1, D_MODEL, tk_long)
    ((dproj, dkvn, dl_acc, dsink_acc, d_lng, d_lnb, d_ws, d_bs, d_aog, d_gog), (r_gu, r_out)) = _mix_bwd(
        proj, gmlp_ln_g, gmlp_ln_b, ws2, wst2, bfull, attn_out_g, gmlp_out_g, dy1, w_out_f.T, kept,
        comm=([slots(dw_gu_t), slots(dw_out)], ("scatter", "scatter")))
    d_relb = _bias_grad(dl_acc, bucket)

    rsum = lambda a: jnp.sum(a, axis=0)
    early_g = dict(
        rel_bias=d_relb[:, 0, :N_BUCKETS].T, attn_sinks=rsum(dsink_acc)[:N_HEADS],
        gmlp_ln_g=rsum(d_lng), gmlp_ln_b=rsum(d_lnb), gmlp_w_s=jnp.where(causal[None], d_ws, 0.0),
        gmlp_b_s=jnp.sum(d_bs.reshape(BLK, N_GROUPS, GMLP_W // N_GROUPS), axis=2).T,
        attn_out_g=rsum(d_aog), gmlp_out_g=rsum(d_gog), ln1_g=rsum(d_ln1g), ln1_b=rsum(d_ln1b),
        ln2_g=rsum(d_ln2g), ln2_b=rsum(d_ln2b))
    (grad_x, dproj_b, d_bin, d_sc1, d_sh1), _ = _din(dproj, dkvn, w_in_t, xs, dz1, sc1, tm, comm=None)
    dw_in_t, (early_all,) = _wgrad("wgrad_in", dproj_b, h1, IN_W // 2, tk_long,
                                   comm=([_pack(early_g, SMALL_EARLY)], ("gather2",)))
    dmod = jnp.concatenate([rsum(d_sh1), rsum(d_sc1), rsum(d_g1), rsum(d_sh2), rsum(d_sc2), rsum(d_g2)])
    late_g = dict(b_ada=dmod, b_in=rsum(d_bin), loss=(0.5 / D_MODEL * jnp.sum(loss_p)).reshape(1))
    late_all, r_in = _scatter_two_level("scatter_in", _pack(late_g, SMALL_LATE), slots(dw_in_t))

    small, _ = _adam_small("adam_small_early", early_all, SMALL_EARLY, wts, mom_m, mom_v)
    small_late, sums = _adam_small("adam_small_late", late_all, SMALL_LATE, wts, mom_m, mom_v)
    small.update(small_late)
    loss = sums["loss"][0, 0]

    dmod_all = late_all[:, :_seg_rows(6 * D_MODEL), :].reshape(N_DEV, 6 * D_MODEL)
    dmod_cols = lax.dynamic_slice(dmod_all, (0, me * ncol), (N_DEV, ncol))
    kpad = 128 - N_DEV
    ada = _adam_w_ada(jnp.pad(c_all.T, ((0, 0), (0, kpad))), jnp.pad(dmod_cols, ((0, kpad), (0, 0))),
                      w_ada[0], m_w_ada[0], v_w_ada[0])

    tr = lambda a: jnp.swapaxes(a, -1, -2)
    big = {}
    big["w_in"] = [tr(o)[None] for o in _adam_reduce("adam_w_in", r_in, w_in[0].T, m_w_in[0].T, v_w_in[0].T, 112)]
    big["w_out"] = [o[None] for o in _adam_reduce("adam_w_out", r_out, w_out[0], m_w_out[0], v_w_out[0], 128)]
    big["w_gate_up"] = [tr(o)[None] for o in _adam_reduce("adam_w_gu", r_gu, w_gate_up[0].T, m_w_gate_up[0].T,
                                                           v_w_gate_up[0].T, 352)]
    big["w_down"] = [o[None] for o in _adam_reduce("adam_w_down", r_down, w_down[0], m_w_down[0], v_w_down[0], 176)]
    big["w_ada"] = [o[None] for o in ada]

    outs = [[], [], [], []]
    for name in WEIGHTS:
        for i in range(4):
            outs[i].append(big[name][i] if name in big else small[name][i])
    return (loss, grad_x[None], *outs[0], *outs[1], *outs[2], *outs[3])
```

```python
import math

import jax
import jax.numpy as jnp
from jax import lax
from jax.experimental import pallas as pl
from jax.experimental.pallas import tpu as pltpu

F32 = jnp.float32
BF16 = jnp.bfloat16
MESH = pl.DeviceIdType.MESH

N_DEV = 8
D_MODEL = 1024
HEAD_DIM = 64
N_HEADS = 8
N_GROUPS = 8
ATTN_W = 512
KV_W = 128
GMLP_W = 512
IN_W = 1792
BLK = 128
N_BUCKETS = 32
MAX_DISTANCE = 128
D_FF = 2816
ALPHA = 2.0 ** 0.25
LN_EPS = 1e-5
NEG_INF = -1e30
ADAM_LR = 0.001
ADAM_B1 = 0.9
ADAM_B2 = 0.999
ADAM_EPS = 1e-08
ADAM_WD = 0.01
ADAM_STEP = 10
GELU_C0 = math.sqrt(2.0 / math.pi)
GELU_C1 = 0.044715

VMEM_LIMIT = 56 * 1024 * 1024


def _params(sem):
    return pltpu.CompilerParams(dimension_semantics=sem, vmem_limit_bytes=VMEM_LIMIT)


def _dot(a, b):
    return lax.dot_general(a, b, (((1,), (0,)), ((), ())), preferred_element_type=F32)


def _dot_nt(a, b):
    return lax.dot_general(a, b, (((1,), (1,)), ((), ())), preferred_element_type=F32)


def _dot_tn(a, b):
    return lax.dot_general(a, b, (((0,), (0,)), ((), ())), preferred_element_type=F32)


def _full(shape):
    nd = len(shape)
    return pl.BlockSpec(shape, lambda *_: (0,) * nd)


def _rowsum8(v):
    r, c = v.shape
    return jnp.sum(v.reshape(r // 8, 8, c), axis=0)


def _sigmoid(v):
    return 1.0 / (1.0 + jnp.exp(-v))


def _gelu_parts(v):
    v2 = v * v
    t = jnp.tanh(GELU_C0 * (v + GELU_C1 * v * v2))
    g = 0.5 * v * (1.0 + t)
    dg = 0.5 * (1.0 + t) + 0.5 * v * (1.0 - t * t) * (GELU_C0 * (1.0 + 3.0 * GELU_C1 * v2))
    return g, dg


def _ln_stats(z):
    mu = jnp.mean(z, axis=1, keepdims=True)
    zc = z - mu
    var = jnp.mean(zc * zc, axis=1, keepdims=True)
    rstd = lax.rsqrt(var + LN_EPS)
    return zc * rstd, rstd


def _ln_bwd(dxhat, xhat, rstd):
    m1 = jnp.mean(dxhat, axis=1, keepdims=True)
    m2 = jnp.mean(dxhat * xhat, axis=1, keepdims=True)
    return rstd * (dxhat - m1 - xhat * m2)


def _seg_mean64(v):
    r = v.shape[0]
    lo = lax.broadcasted_iota(jnp.int32, (r, 128), 1) < 64
    outs = []
    for j in range(v.shape[1] // 128):
        ch = v[:, 128 * j:128 * (j + 1)]
        s_lo = jnp.sum(jnp.where(lo, ch, 0.0), axis=1, keepdims=True)
        s_hi = jnp.sum(jnp.where(lo, 0.0, ch), axis=1, keepdims=True)
        outs.append(jnp.where(lo, s_lo, s_hi) * (1.0 / 64.0))
    return jnp.concatenate(outs, axis=1)


def _rms(a, g):
    r = lax.rsqrt(jnp.mean(a * a, axis=1, keepdims=True) + LN_EPS)
    return a * r * g, r


def _rms_bwd(dout, a, r, g):
    t = dout * g
    return r * t - a * (r * r * r) * jnp.mean(t * a, axis=1, keepdims=True)


PEER_ORDER = (1, 2, 4, 3, 5, 6, 7)


def _peer(j):
    x, y, c = lax.axis_index("x"), lax.axis_index("y"), lax.axis_index("c")
    px = 1 - x if j & 4 else x
    py = 1 - y if j & 2 else y
    pc = 1 - c if j & 1 else c
    return (px, py, pc), 4 * px + 2 * py + pc


SIBLING = 1
CHIP_FLIPS = (4, 2, 6)


def _exchange_phase(phase, ins, outs, modes, send_sems, recv_sems, loc_sems):
    me = 4 * lax.axis_index("x") + 2 * lax.axis_index("y") + lax.axis_index("c")
    for k, mode in enumerate(modes):
        def copy(i, src, slot, dev, k=k):
            return pltpu.make_async_remote_copy(src_ref=src, dst_ref=outs[k].at[slot], send_sem=send_sems.at[k, i],
                                                recv_sem=recv_sems.at[k, i], device_id=dev, device_id_type=MESH)

        src_me = ins[k].at[me] if mode == "scatter" else ins[k]
        local = pltpu.make_async_copy(src_me, outs[k].at[me], loc_sems.at[k])
        if mode == "gather2":
            sib_dev, sib_idx = _peer(SIBLING)
            chips = [_peer(j) for j in CHIP_FLIPS]
            far = [_peer(j | SIBLING)[1] for j in CHIP_FLIPS]
            if phase == "start":
                local.start()
                copy(0, ins[k], me, sib_dev).start()
                for i, (dev, _) in enumerate(chips):
                    copy(1 + i, ins[k], me, dev).start()
            elif phase == "mid":
                for i, (dev, idx) in enumerate(chips):
                    copy(1 + i, ins[k], idx, dev).wait_recv()
                    copy(4 + i, outs[k].at[idx], idx, sib_dev).start()
            else:
                copy(0, ins[k], sib_idx, sib_dev).wait_recv()
                for i, slot in enumerate(far):
                    copy(4 + i, ins[k], slot, sib_dev).wait_recv()
                copy(0, ins[k], me, sib_dev).wait_send()
                for i, (dev, idx) in enumerate(chips):
                    copy(1 + i, ins[k], me, dev).wait_send()
                    copy(4 + i, outs[k].at[idx], idx, sib_dev).wait_send()
                local.wait()
            continue
        peers = [_peer(j) for j in PEER_ORDER]
        if phase == "start":
            local.start()
            for i, (dev, idx) in enumerate(peers):
                copy(i, ins[k].at[idx] if mode == "scatter" else ins[k], me, dev).start()
        elif phase == "end":
            for i, (dev, idx) in enumerate(peers):
                copy(i, src_me, idx, dev).wait_recv()
            for i, (dev, idx) in enumerate(peers):
                copy(i, src_me, me, dev).wait_send()
            local.wait()


def _exchange_shapes(arrays, modes):
    return [jax.ShapeDtypeStruct((N_DEV,) + (a.shape[1:] if m == "scatter" else a.shape), a.dtype)
            for a, m in zip(arrays, modes)]


def _exchange_sems(n):
    return [pltpu.SemaphoreType.DMA((n, N_DEV - 1)), pltpu.SemaphoreType.DMA((n, N_DEV - 1)),
            pltpu.SemaphoreType.DMA((n,))]


def _exchange(name, arrays, modes):
    n = len(arrays)

    def body(*refs):
        for phase in ("start", "mid", "end"):
            _exchange_phase(phase, refs[:n], refs[n:2 * n], modes, *refs[2 * n:])

    any_spec = pl.BlockSpec(memory_space=pl.ANY)
    return pl.pallas_call(
        body, name=name, out_shape=_exchange_shapes(arrays, modes),
        in_specs=[any_spec] * n, out_specs=[any_spec] * n, scratch_shapes=_exchange_sems(n),
    )(*arrays)


N_CHIP = 4


def _scatter_two_level(name, pack, parts):
    r, ncols = parts.shape[1:]

    def body(pack_ref, parts_ref, late_ref, got_ref, sib_ref, h_ref, g_send, g_recv, g_loc, d_send, d_recv, i_send, i_recv):
        x, y, c = lax.axis_index("x"), lax.axis_index("y"), lax.axis_index("c")
        my_chip = 2 * x + y
        sib_dev, _ = _peer(SIBLING)
        gather = ([pack_ref], [late_ref], ("gather",), g_send, g_recv, g_loc)
        _exchange_phase("start", *gather)

        def to_sibling(q):
            return pltpu.make_async_remote_copy(src_ref=parts_ref.at[2 * q + 1 - c], dst_ref=sib_ref.at[q],
                                                send_sem=d_send.at[q], recv_sem=d_recv.at[q],
                                                device_id=sib_dev, device_id_type=MESH)

        for q in range(N_CHIP):
            to_sibling(q).start()
        for q in range(N_CHIP):
            to_sibling(q).wait_recv()
            h_ref[q] = (parts_ref[2 * q + c].astype(F32) + sib_ref[q].astype(F32)).astype(BF16)

        def to_chip(i, slot):
            dev, idx = _peer(CHIP_FLIPS[i])
            return pltpu.make_async_remote_copy(src_ref=h_ref.at[idx // 2], dst_ref=got_ref.at[slot],
                                                send_sem=i_send.at[i], recv_sem=i_recv.at[i],
                                                device_id=dev, device_id_type=MESH)

        for i in range(len(CHIP_FLIPS)):
            to_chip(i, my_chip).start()
        got_ref[my_chip] = h_ref[my_chip]
        for i in range(len(CHIP_FLIPS)):
            to_chip(i, _peer(CHIP_FLIPS[i])[1] // 2).wait_recv()
        for i in range(len(CHIP_FLIPS)):
            to_chip(i, my_chip).wait_send()
        for q in range(N_CHIP):
            to_sibling(q).wait_send()
        _exchange_phase("end", *gather)

    any_spec = pl.BlockSpec(memory_space=pl.ANY)
    vmem = pl.BlockSpec(memory_space=pltpu.VMEM)
    dma = pltpu.SemaphoreType.DMA
    return pl.pallas_call(
        body, name=name,
        out_shape=[jax.ShapeDtypeStruct((N_DEV,) + pack.shape, pack.dtype),
                   jax.ShapeDtypeStruct((N_CHIP, r, ncols), parts.dtype)],
        in_specs=[any_spec, vmem], out_specs=[any_spec, vmem],
        scratch_shapes=[pltpu.VMEM((N_CHIP, r, ncols), parts.dtype), pltpu.VMEM((N_CHIP, r, ncols), parts.dtype),
                        dma((1, N_DEV - 1)), dma((1, N_DEV - 1)), dma((1,)),
                        dma((N_CHIP,)), dma((N_CHIP,)), dma((len(CHIP_FLIPS),)), dma((len(CHIP_FLIPS),))],
        compiler_params=pltpu.CompilerParams(vmem_limit_bytes=VMEM_LIMIT),
    )(pack, parts)


def _call(body, *, name, grid, in_specs, out_specs, out_shape, args, sem, scratch_shapes=(), comm=None):
    if comm is None:
        outs = pl.pallas_call(body, name=name, grid=grid, in_specs=list(in_specs), out_specs=list(out_specs),
                              out_shape=list(out_shape), scratch_shapes=list(scratch_shapes),
                              compiler_params=_params(sem))(*args)
        return list(outs), []
    arrays, modes = comm
    n_in, n_out, nc, ns = len(in_specs), len(out_specs), len(arrays), len(scratch_shapes)
    n_steps = math.prod(grid)

    def hosted(*refs):
        ins, cins = refs[:n_in], refs[n_in:n_in + nc]
        outs, couts = refs[n_in + nc:n_in + nc + n_out], refs[n_in + nc + n_out:n_in + 2 * nc + n_out]
        scratch = refs[n_in + 2 * nc + n_out:]
        ex = (cins, couts, modes) + tuple(scratch[ns:])
        step = pl.program_id(0)
        for ax in range(1, len(grid)):
            step = step * grid[ax] + pl.program_id(ax)

        @pl.when(step == 0)
        def _():
            _exchange_phase("start", *ex)

        body(*ins, *outs, *scratch[:ns])

        if "gather2" in modes:
            @pl.when(step == max((3 * n_steps) // 4 - 1, 0))
            def _():
                _exchange_phase("mid", *ex)

        @pl.when(step == n_steps - 1)
        def _():
            _exchange_phase("end", *ex)

    any_spec = pl.BlockSpec(memory_space=pl.ANY)
    res = pl.pallas_call(
        hosted, name=name, grid=grid, in_specs=list(in_specs) + [any_spec] * nc,
        out_specs=list(out_specs) + [any_spec] * nc, out_shape=list(out_shape) + _exchange_shapes(arrays, modes),
        scratch_shapes=list(scratch_shapes) + _exchange_sems(nc),
        compiler_params=_params(tuple("arbitrary" for _ in grid)))(*args, *arrays)
    return list(res[:n_out]), list(res[n_out:])


def _mod_partial(c_all, w_ada, b_ada_cols):
    def body(c_ref, w_ref, b_ref, o_ref):
        cv = c_ref[...]
        s = (cv * _sigmoid(cv)).astype(BF16)
        o_ref[...] = _dot(s, w_ref[...].astype(BF16)) + b_ref[...]

    ncol = w_ada.shape[1]
    return pl.pallas_call(
        body, name="mod_partial", out_shape=jax.ShapeDtypeStruct((N_DEV, ncol), F32),
        in_specs=[_full(c_all.shape), _full(w_ada.shape), _full(b_ada_cols.shape)],
        out_specs=_full((N_DEV, ncol)), grid=(1,), compiler_params=_params(("arbitrary",)),
    )(c_all, w_ada, b_ada_cols)


def _bias_table(rel_bias, bucket, comm):
    def body(rb_ref, bk_ref, o_ref):
        h = pl.program_id(0)
        bk = bk_ref[...]
        acc = jnp.zeros((BLK, 2 * BLK), F32)
        for b in range(N_BUCKETS):
            acc = jnp.where(bk == b, rb_ref[b, h], acc)
        dist = (lax.broadcasted_iota(jnp.int32, (BLK, 2 * BLK), 0) + BLK
                - lax.broadcasted_iota(jnp.int32, (BLK, 2 * BLK), 1))
        o_ref[0] = jnp.where((dist >= 0) & (dist < BLK), acc, NEG_INF)

    return _call(
        body, name="bias_table", out_shape=[jax.ShapeDtypeStruct((N_HEADS, BLK, 2 * BLK), F32)],
        in_specs=[pl.BlockSpec(memory_space=pltpu.SMEM), _full((BLK, 2 * BLK))],
        out_specs=[pl.BlockSpec((1, BLK, 2 * BLK), lambda h: (h, 0, 0))], grid=(N_HEADS,),
        sem=("arbitrary",), comm=comm, args=(rel_bias, bucket))


def _bias_grad(dl_acc, bucket):
    def body(dl_ref, bk_ref, o_ref):
        bk = bk_ref[...]
        dl = dl_ref[0]
        lane = lax.broadcasted_iota(jnp.int32, (1, 128), 1)
        row = jnp.zeros((1, 128), F32)
        for b in range(N_BUCKETS):
            s = jnp.sum(jnp.sum(jnp.where(bk == b, dl, 0.0), axis=1, keepdims=True), axis=0, keepdims=True)
            row = jnp.where(lane == b, s, row)
        o_ref[0] = row

    return pl.pallas_call(
        body, name="bias_grad", out_shape=jax.ShapeDtypeStruct((N_HEADS, 1, 128), F32),
        in_specs=[pl.BlockSpec((1, BLK, 2 * BLK), lambda h: (h, 0, 0)), _full((BLK, 2 * BLK))],
        out_specs=pl.BlockSpec((1, 1, 128), lambda h: (h, 0, 0)), grid=(N_HEADS,),
        compiler_params=_params(("arbitrary",)),
    )(dl_acc, bucket)


def _inproj(x, sc1, sh1, w_in_t, b_in, tm, comm):
    t, d = x.shape
    n = w_in_t.shape[0]

    def body(x_ref, sc_ref, sh_ref, w_ref, b_ref, proj_ref, h_ref):
        h = (x_ref[...] * (1.0 + sc_ref[...]) + sh_ref[...]).astype(BF16)
        h_ref[...] = h
        proj_ref[...] = _dot_nt(h, w_ref[...]) + b_ref[...]

    row = lambda w: pl.BlockSpec((tm, w), lambda i: (i, 0))
    return _call(
        body, name="inproj", grid=(t // tm,),
        out_shape=[jax.ShapeDtypeStruct((t, n), F32), jax.ShapeDtypeStruct((t, d), BF16)],
        in_specs=[row(d), _full((1, d)), _full((1, d)), _full((n, d)), _full((1, n))],
        out_specs=[row(n), row(d)], sem=("parallel",), comm=comm, args=(x, sc1, sh1, w_in_t, b_in))


HALF = 64
ROWS = 32


def _lane_lo(rows):
    return lax.broadcasted_iota(jnp.int32, (rows, 128), 1) < 64


def _mix_stage_kv(proj_ref, kvp_ref, s):
    lo = _lane_lo(2 * BLK)
    for name, col in (("k", ATTN_W), ("v", ATTN_W + KV_W)):
        cur = jnp.concatenate([kvp_ref[:, col - ATTN_W:col - ATTN_W + KV_W], proj_ref[:, col:col + KV_W]], axis=0)
        plain, swapped = cur.astype(BF16), pltpu.roll(cur, 64, 1).astype(BF16)
        zero = jnp.zeros_like(plain)
        for g in range(2):
            dup = jnp.where(lo, plain, swapped) if g == 0 else jnp.where(lo, swapped, plain)
            s[name + "d"][g] = dup
            s[name + "m"][g] = jnp.concatenate([jnp.where(lo, dup, zero), jnp.where(lo, zero, dup)], axis=0)


def _group_rows(ref, g):
    return ref[4 * g:4 * g + 4].reshape(4 * BLK, ref.shape[2])


def _pair_rows(ref, g):
    return jnp.concatenate([jnp.concatenate([ref[4 * g + 2 * c], ref[4 * g + 2 * c + 1]], axis=1) for c in range(2)],
                           axis=0)


def _mask_heads(src_ref, dst_ref):
    lo = _lane_lo(BLK)
    for j in range(4):
        chunk = src_ref[:, 128 * j:128 * (j + 1)]
        dst_ref[2 * j] = jnp.where(lo, chunk, 0.0).astype(BF16)
        dst_ref[2 * j + 1] = jnp.where(lo, 0.0, chunk).astype(BF16)


def _mix_stage_attn(proj_ref, bias_ref, sinks_ref, n, s):
    _mask_heads(proj_ref, s["qm"])
    for g in range(2):
        s["lg"][g] = _dot_nt(_group_rows(s["qm"], g), s["kd"][g])
    n0mask = (n == 0) & (lax.broadcasted_iota(jnp.int32, (HALF, 2 * BLK), 1) < BLK)
    lane = lax.broadcasted_iota(jnp.int32, (HALF, 128), 1)
    for hf in range(BLK // HALF):
        rows = slice(HALF * hf, HALF * (hf + 1))
        psink = jnp.zeros((HALF, 128), F32)
        for h in range(N_HEADS):
            sk = sinks_ref[h]
            grows = slice(BLK * (h % 4) + HALF * hf, BLK * (h % 4) + HALF * (hf + 1))
            logit = s["lg"][h // 4, grows, :] * (HEAD_DIM ** -0.5) + bias_ref[h, rows, :]
            logit = jnp.where(n0mask, NEG_INF, logit)
            m = jnp.maximum(jnp.max(logit, axis=1, keepdims=True), sk)
            e = jnp.exp(logit - m)
            es = jnp.exp(sk - m)
            inv = 1.0 / (jnp.sum(e, axis=1, keepdims=True) + es)
            p = e * inv
            s["p"][h, rows, :] = p
            s["pb"][h, rows, :] = p.astype(BF16)
            psink = jnp.where(lane == h, es * inv, psink)
        s["psink"][rows, :] = psink
    for g in range(2):
        out = _dot(_pair_rows(s["pb"], g), s["vm"][g])
        s["attn"][:, 256 * g:256 * g + 128] = out[0:BLK]
        s["attn"][:, 256 * g + 128:256 * g + 256] = out[BLK:2 * BLK]


def _mix_stage_gmlp_pre(proj_ref, lng, lnb, s, keep):
    c0 = ATTN_W + 2 * KV_W
    for r0 in range(0, BLK, ROWS):
        rows = slice(r0, r0 + ROWS)
        u, du = _gelu_parts(proj_ref[rows, c0:c0 + GMLP_W])
        a, da = _gelu_parts(proj_ref[rows, c0 + GMLP_W:c0 + 2 * GMLP_W])
        ac = a - _seg_mean64(a)
        rstd = lax.rsqrt(_seg_mean64(ac * ac) + LN_EPS)
        vhat = ac * rstd
        s["u"][rows, :] = u
        s["vnb"][rows, :] = (vhat * lng + lnb).astype(BF16)
        if keep:
            s["du"][rows, :] = du
            s["da"][rows, :] = da
            s["vhat"][rows, :] = vhat
            s["rstd"][rows, :] = rstd


def _stack_halves(chunk):
    lo = _lane_lo(BLK)
    zero = jnp.zeros_like(chunk)
    return jnp.concatenate([jnp.where(lo, chunk, zero), jnp.where(lo, zero, chunk)], axis=0)


def _mix_stage_gmlp_mix(ws2_ref, bfull_ref, s):
    for j in range(4):
        cols = slice(128 * j, 128 * (j + 1))
        s["ms"][:, cols] = _dot(ws2_ref[j], _stack_halves(s["vnb"][:, cols])) + bfull_ref[:, cols]


def _mix_scratch(keep):
    f32 = lambda *shape: pltpu.VMEM(shape, F32)
    b16 = lambda *shape: pltpu.VMEM(shape, BF16)
    names = dict(kd=b16(2, 2 * BLK, 128), vd=b16(2, 2 * BLK, 128), km=b16(2, 4 * BLK, 128), vm=b16(2, 4 * BLK, 128),
                 qm=b16(N_HEADS, BLK, 128), lg=f32(2, 4 * BLK, 2 * BLK), pb=b16(N_HEADS, BLK, 2 * BLK),
                 u=f32(BLK, GMLP_W), vnb=b16(BLK, GMLP_W), ms=f32(BLK, GMLP_W))
    if keep:
        names.update(dom=b16(N_HEADS, BLK, 128), dls=b16(N_HEADS, BLK, 2 * BLK),
                     dattn=f32(BLK, ATTN_W), dmix=f32(BLK, D_MODEL), du=f32(BLK, GMLP_W), da=f32(BLK, GMLP_W),
                     vhat=f32(BLK, GMLP_W), rstd=f32(BLK, GMLP_W), dmsb=b16(BLK, GMLP_W), dvn=f32(BLK, GMLP_W))
    return list(names), list(names.values())


def _mix_specs(with_logit_inputs):
    logit_inputs = [_full((N_HEADS, BLK, 2 * BLK)), pl.BlockSpec(memory_space=pltpu.SMEM)] if with_logit_inputs else []
    return [pl.BlockSpec((BLK, IN_W), lambda n: (n, 0)),
            pl.BlockSpec((BLK, 2 * KV_W), lambda n: (jnp.maximum(n - 1, 0), ATTN_W // (2 * KV_W)))] + logit_inputs + [
            _full((1, GMLP_W)), _full((1, GMLP_W)),
            _full((N_GROUPS // 2, BLK, 2 * BLK)), _full((BLK, GMLP_W)),
            _full((1, ATTN_W)), _full((1, GMLP_W))]


KEPT = [("p", (N_HEADS, BLK, 2 * BLK), F32), ("psink", (BLK, 128), F32), ("attn", (BLK, ATTN_W), F32)]


def _kept_shapes(t):
    full = lambda blk: (blk[0], t, blk[2]) if len(blk) == 3 else (t, blk[1])
    return [jax.ShapeDtypeStruct(full(blk), dt) for _, blk, dt in KEPT]


def _kept_specs():
    return [pl.BlockSpec(blk, (lambda n: (0, n, 0)) if len(blk) == 3 else (lambda n: (n, 0))) for _, blk, _ in KEPT]


def _mix_fwd(proj, bias, sinks, lng, lnb, ws2, bfull, aog, gog, comm):
    t = proj.shape[0]
    names, shapes = _mix_scratch(False)

    def body(proj_ref, kvp_ref, bias_ref, sinks_ref, lng_ref, lnb_ref, ws2_ref, bfull_ref, aog_ref, gog_ref,
             out_ref, *rest):
        s = dict(zip([name for name, _, _ in KEPT] + names, rest))
        n = pl.program_id(0)
        _mix_stage_kv(proj_ref, kvp_ref, s)
        _mix_stage_attn(proj_ref, bias_ref, sinks_ref, n, s)
        _mix_stage_gmlp_pre(proj_ref, lng_ref[...], lnb_ref[...], s, False)
        _mix_stage_gmlp_mix(ws2_ref, bfull_ref, s)
        for r0 in range(0, BLK, ROWS):
            rows = slice(r0, r0 + ROWS)
            out_ref[rows, 0:ATTN_W] = _rms(s["attn"][rows, :], aog_ref[...])[0].astype(BF16)
            out_ref[rows, ATTN_W:ATTN_W + GMLP_W] = _rms(s["u"][rows, :] * s["ms"][rows, :], gog_ref[...])[0].astype(BF16)

    return _call(
        body, name="mix_fwd", grid=(t // BLK,),
        out_shape=[jax.ShapeDtypeStruct((t, D_MODEL), BF16)] + _kept_shapes(t),
        in_specs=_mix_specs(True), out_specs=[pl.BlockSpec((BLK, D_MODEL), lambda n: (n, 0))] + _kept_specs(),
        scratch_shapes=shapes,
        sem=("parallel",), comm=comm, args=(proj, proj, bias, sinks, lng, lnb, ws2, bfull, aog, gog))


def _mix_bwd(proj, lng, lnb, ws2, wst2, bfull, aog, gog, dy, w_out, kept, comm):
    t = proj.shape[0]
    nb = t // BLK
    names, shapes = _mix_scratch(True)
    c_gu = ATTN_W + 2 * KV_W

    def body(proj_ref, kvp_ref, lng_ref, lnb_ref, ws2_ref, bfull_ref, aog_ref, gog_ref,
             wst2_ref, dy_ref, wout_ref, *rest):
        n_kept = len(KEPT)
        s = dict(zip([name for name, _, _ in KEPT], rest[:n_kept]))
        (dproj_ref, dkvn_ref, dl_ref, dsink_ref, dlng_ref, dlnb_ref, dws_ref, dbs_ref, daog_ref,
         dgog_ref) = rest[n_kept:n_kept + 10]
        s.update(zip(names, rest[n_kept + 10:]))
        n = pl.program_id(0)

        @pl.when(n == 0)
        def _():
            for r in (dl_ref, dsink_ref, dlng_ref, dlnb_ref, dws_ref, dbs_ref, daog_ref, dgog_ref):
                r[...] = jnp.zeros_like(r)

        s["dmix"][...] = _dot(dy_ref[...], wout_ref[...])
        _mix_stage_kv(proj_ref, kvp_ref, s)
        _mask_heads(proj_ref, s["qm"])
        lng = lng_ref[...]
        _mix_stage_gmlp_pre(proj_ref, lng, lnb_ref[...], s, True)
        _mix_stage_gmlp_mix(ws2_ref, bfull_ref, s)

        aog, gog = aog_ref[...], gog_ref[...]
        for r0 in range(0, BLK, ROWS):
            rows = slice(r0, r0 + ROWS)
            attn, dma = s["attn"][rows, :], s["dmix"][rows, 0:ATTN_W]
            _, r_a = _rms(attn, aog)
            daog_ref[...] += _rowsum8(dma * attn * r_a)
            s["dattn"][rows, :] = _rms_bwd(dma, attn, r_a, aog)
            u, ms, dmg = s["u"][rows, :], s["ms"][rows, :], s["dmix"][rows, ATTN_W:ATTN_W + GMLP_W]
            gm = u * ms
            _, r_g = _rms(gm, gog)
            dgog_ref[...] += _rowsum8(dmg * gm * r_g)
            dgm = _rms_bwd(dmg, gm, r_g, gog)
            dproj_ref[rows, c_gu:c_gu + GMLP_W] = dgm * ms * s["du"][rows, :]
            dms = dgm * u
            dbs_ref[rows, :] += dms
            s["dmsb"][rows, :] = dms.astype(BF16)

        _mask_heads(s["dattn"], s["dom"])
        for g in range(2):
            s["lg"][g] = _dot_nt(_group_rows(s["dom"], g), s["vd"][g])
        lane = lax.broadcasted_iota(jnp.int32, (HALF, 128), 1)
        for hf in range(BLK // HALF):
            rows = slice(HALF * hf, HALF * (hf + 1))
            dsink = jnp.zeros((HALF, 128), F32)
            for h in range(N_HEADS):
                grows = slice(BLK * (h % 4) + HALF * hf, BLK * (h % 4) + HALF * (hf + 1))
                dp = s["lg"][h // 4, grows, :]
                p = s["p"][h, rows, :]
                s["pb"][h, rows, :] = p.astype(BF16)
                rs = jnp.sum(p * dp, axis=1, keepdims=True)
                dl = p * (dp - rs)
                dl_ref[h, rows, :] += dl
                dsink = dsink + jnp.where(lane == h, -s["psink"][rows, :] * rs, 0.0)
                s["dls"][h, rows, :] = (dl * (HEAD_DIM ** -0.5)).astype(BF16)
            dsink_ref[rows, :] += dsink
        for g in range(2):
            dq = _dot(_pair_rows(s["dls"], g), s["km"][g])
            dproj_ref[:, 256 * g:256 * g + 128] = dq[0:BLK]
            dproj_ref[:, 256 * g + 128:256 * g + 256] = dq[BLK:2 * BLK]
        lo_k = _lane_lo(2 * BLK)
        for col, lhs, rhs in ((0, "dls", "qm"), (KV_W, "pb", "dom")):
            raw = [_dot_tn(_group_rows(s[lhs], g), _group_rows(s[rhs], g)) for g in range(2)]
            both = [r + pltpu.roll(r, 64, 1) for r in raw]
            dkv = jnp.where(lo_k, both[0], both[1])
            dproj_ref[:, ATTN_W + col:ATTN_W + col + KV_W] = dkv[BLK:2 * BLK]
            dkvn_ref[:, col:col + KV_W] = dkv[0:BLK]

        for j in range(4):
            cols = slice(128 * j, 128 * (j + 1))
            dm2 = _stack_halves(s["dmsb"][:, cols])
            vnb = s["vnb"][:, cols]
            dws2 = _dot_nt(dm2, vnb)
            dws_ref[2 * j] += dws2[0:BLK]
            dws_ref[2 * j + 1] += dws2[BLK:2 * BLK]
            s["dvn"][:, cols] = _dot(wst2_ref[j], dm2)
        for r0 in range(0, BLK, ROWS):
            rows = slice(r0, r0 + ROWS)
            dvn, vhat = s["dvn"][rows, :], s["vhat"][rows, :]
            dlng_ref[...] += _rowsum8(dvn * vhat)
            dlnb_ref[...] += _rowsum8(dvn)
            dvh = dvn * lng
            dact = s["rstd"][rows, :] * (dvh - _seg_mean64(dvh) - vhat * _seg_mean64(dvh * vhat))
            dproj_ref[rows, c_gu + GMLP_W:IN_W] = dact * s["da"][rows, :]

    acc8 = lambda w: jax.ShapeDtypeStruct((8, w), F32)
    out_shape = [jax.ShapeDtypeStruct((t, IN_W), F32), jax.ShapeDtypeStruct((t, 2 * KV_W), F32),
                 jax.ShapeDtypeStruct((N_HEADS, BLK, 2 * BLK), F32), jax.ShapeDtypeStruct((BLK, 128), F32),
                 acc8(GMLP_W), acc8(GMLP_W), jax.ShapeDtypeStruct((N_GROUPS, BLK, BLK), F32),
                 jax.ShapeDtypeStruct((BLK, GMLP_W), F32), acc8(ATTN_W), acc8(GMLP_W)]
    out_specs = [pl.BlockSpec((BLK, IN_W), lambda n: (n, 0)),
                 pl.BlockSpec((BLK, 2 * KV_W), lambda n: ((n + nb - 1) % nb, 0)),
                 _full((N_HEADS, BLK, 2 * BLK)), _full((BLK, 128)), _full((8, GMLP_W)), _full((8, GMLP_W)),
                 _full((N_GROUPS, BLK, BLK)), _full((BLK, GMLP_W)), _full((8, ATTN_W)), _full((8, GMLP_W))]
    in_specs = _mix_specs(False) + [_full((N_GROUPS // 2, BLK, 2 * BLK)),
                               pl.BlockSpec((BLK, D_MODEL), lambda n: (n, 0)),
                               _full((D_MODEL, D_MODEL))] + _kept_specs()
    return _call(
        body, name="mix_bwd", grid=(nb,), out_shape=out_shape, in_specs=in_specs, out_specs=out_specs,
        scratch_shapes=shapes, sem=("arbitrary",), comm=comm,
        args=(proj, proj, lng, lnb, ws2, bfull, aog, gog, wst2, dy, w_out, *kept))


def _outproj(mixed, w_out, x, g1, ln1g, ln1b, sc2, sh2, tm, comm):
    t, d = x.shape

    def body(mx_ref, w_ref, x_ref, g1_ref, lg_ref, lb_ref, sc_ref, sh_ref, y_ref, x1_ref, h2_ref):
        y = _dot(mx_ref[...], w_ref[...])
        xhat, _ = _ln_stats(ALPHA * x_ref[...] + g1_ref[...] * y)
        x1 = xhat * lg_ref[...] + lb_ref[...]
        y_ref[...] = y
        x1_ref[...] = x1
        h2_ref[...] = (x1 * (1.0 + sc_ref[...]) + sh_ref[...]).astype(BF16)

    row = pl.BlockSpec((tm, d), lambda i: (i, 0))
    vec = _full((1, d))
    return _call(
        body, name="outproj", grid=(t // tm,),
        out_shape=[jax.ShapeDtypeStruct((t, d), F32), jax.ShapeDtypeStruct((t, d), F32),
                   jax.ShapeDtypeStruct((t, d), BF16)],
        in_specs=[row, _full((d, d)), row, vec, vec, vec, vec, vec], out_specs=[row, row, row],
        sem=("parallel",), comm=comm, args=(mixed, w_out, x, g1, ln1g, ln1b, sc2, sh2))


def _ffn_fwd(h2, w_gu_t, w_down, x1, target, g2, ln2g, ln2b, tm):
    t, d = x1.shape

    def body(h_ref, w_ref, wd_ref, x1_ref, tg_ref, g2_ref, lg_ref, lb_ref,
             dsu_ref, sg_ref, act_ref, dz_ref, dy_ref, loss_ref, dlg_ref, dlb_ref, dg2_ref):
        @pl.when(pl.program_id(0) == 0)
        def _():
            for r in (loss_ref, dlg_ref, dlb_ref, dg2_ref):
                r[...] = jnp.zeros_like(r)

        h = h_ref[...]
        g = _dot_nt(h, w_ref[0:D_FF])
        u = _dot_nt(h, w_ref[D_FF:2 * D_FF])
        s = _sigmoid(g)
        sg = g * s
        act = (sg * u).astype(BF16)
        dsu_ref[...] = (u * (s * (1.0 + g * (1.0 - s)))).astype(BF16)
        sg_ref[...] = sg.astype(BF16)
        act_ref[...] = act
        y2 = _dot(act, wd_ref[...])
        g2 = g2_ref[...]
        lg = lg_ref[...]
        xhat, rstd = _ln_stats(ALPHA * x1_ref[...] + g2 * y2)
        err = xhat * lg + lb_ref[...] - tg_ref[...]
        loss_ref[...] += _rowsum8(err * err)
        dx2 = err * (1.0 / d)
        dlg_ref[...] += _rowsum8(dx2 * xhat)
        dlb_ref[...] += _rowsum8(dx2)
        dz = _ln_bwd(dx2 * lg, xhat, rstd)
        dg2_ref[...] += _rowsum8(dz * y2)
        dz_ref[...] = dz
        dy_ref[...] = (g2 * dz).astype(BF16)

    row = pl.BlockSpec((tm, d), lambda i: (i, 0))
    wide = pl.BlockSpec((tm, D_FF), lambda i: (i, 0))
    vec = _full((1, d))
    acc = _full((8, d))
    acc_shape = jax.ShapeDtypeStruct((8, d), F32)
    wide_shape = jax.ShapeDtypeStruct((t, D_FF), BF16)
    return pl.pallas_call(
        body, name="ffn_fwd", grid=(t // tm,),
        out_shape=[wide_shape] * 3 + [jax.ShapeDtypeStruct((t, d), F32), jax.ShapeDtypeStruct((t, d), BF16)]
        + [acc_shape] * 4,
        in_specs=[row, _resident((2 * D_FF, d)), _resident((D_FF, d)), row, row, vec, vec, vec],
        out_specs=[wide] * 3 + [row, row, acc, acc, acc, acc], compiler_params=_params(("arbitrary",)),
    )(h2, w_gu_t, w_down, x1, target, g2, ln2g, ln2b)


def _resident(shape):
    nd = len(shape)
    return pl.BlockSpec(shape, lambda *_: (0,) * nd, pipeline_mode=pl.Buffered(1))


def _ffn_bwd(dy2, w_down, dsu, sg, w_gu_t, x1, x, y, dz2, sc2, g1, ln1g, tm, comm):
    t, d = x1.shape

    def body(dy2_ref, wd_ref, dsu_ref, sg_ref, w_ref, x1_ref, x_ref, y_ref, dz2_ref, sc_ref, g1_ref, lg_ref,
             dg_ref, du_ref, dz1_ref, dy_ref, dsc_ref, dsh_ref, dlg_ref, dlb_ref, dg1_ref):
        @pl.when(pl.program_id(0) == 0)
        def _():
            for r in (dsc_ref, dsh_ref, dlg_ref, dlb_ref, dg1_ref):
                r[...] = jnp.zeros_like(r)

        dact = _dot_nt(dy2_ref[...], wd_ref[...])
        dg = (dact * dsu_ref[...].astype(F32)).astype(BF16)
        du = (dact * sg_ref[...].astype(F32)).astype(BF16)
        dg_ref[...] = dg
        du_ref[...] = du
        dh2 = _dot(dg, w_ref[0:D_FF]) + _dot(du, w_ref[D_FF:2 * D_FF])
        x1 = x1_ref[...]
        y = y_ref[...]
        g1 = g1_ref[...]
        dsc_ref[...] += _rowsum8(dh2 * x1)
        dsh_ref[...] += _rowsum8(dh2)
        dx1 = dh2 * (1.0 + sc_ref[...]) + ALPHA * dz2_ref[...]
        xhat, rstd = _ln_stats(ALPHA * x_ref[...] + g1 * y)
        dlg_ref[...] += _rowsum8(dx1 * xhat)
        dlb_ref[...] += _rowsum8(dx1)
        dz1 = _ln_bwd(dx1 * lg_ref[...], xhat, rstd)
        dg1_ref[...] += _rowsum8(dz1 * y)
        dz1_ref[...] = dz1
        dy_ref[...] = (g1 * dz1).astype(BF16)

    row = pl.BlockSpec((tm, d), lambda i: (i, 0))
    wide = pl.BlockSpec((tm, D_FF), lambda i: (i, 0))
    vec = _full((1, d))
    acc = _full((8, d))
    acc_shape = jax.ShapeDtypeStruct((8, d), F32)
    wide_shape = jax.ShapeDtypeStruct((t, D_FF), BF16)
    return _call(
        body, name="ffn_bwd", grid=(t // tm,),
        out_shape=[wide_shape, wide_shape, jax.ShapeDtypeStruct((t, d), F32), jax.ShapeDtypeStruct((t, d), BF16)]
        + [acc_shape] * 5,
        in_specs=[row, _resident((D_FF, d)), wide, wide, _resident((2 * D_FF, d)), row, row, row, row, vec, vec, vec],
        out_specs=[wide, wide, row, row, acc, acc, acc, acc, acc], sem=("arbitrary",), comm=comm,
        args=(dy2, w_down, dsu, sg, w_gu_t, x1, x, y, dz2, sc2, g1, ln1g))


def _din(dproj, dkvn, w_in_t, x, dz1, sc1, tm, comm):
    t, d = x.shape

    def body(dp_ref, dkv_ref, w_ref, x_ref, dz1_ref, sc_ref, dx_ref, dpb_ref, dbin_ref, dsc_ref, dsh_ref):
        @pl.when(pl.program_id(0) == 0)
        def _():
            for r in (dbin_ref, dsc_ref, dsh_ref):
                r[...] = jnp.zeros_like(r)

        dp = jnp.concatenate([dp_ref[:, 0:ATTN_W], dp_ref[:, ATTN_W:ATTN_W + 2 * KV_W] + dkv_ref[...],
                              dp_ref[:, ATTN_W + 2 * KV_W:IN_W]], axis=1)
        dbin_ref[...] += _rowsum8(dp)
        dpb = dp.astype(BF16)
        dpb_ref[...] = dpb
        dh = _dot(dpb, w_ref[...])
        dsc_ref[...] += _rowsum8(dh * x_ref[...])
        dsh_ref[...] += _rowsum8(dh)
        dx_ref[...] = dh * (1.0 + sc_ref[...]) + ALPHA * dz1_ref[...]

    row = lambda w: pl.BlockSpec((tm, w), lambda i: (i, 0))
    return _call(
        body, name="din", grid=(t // tm,),
        out_shape=[jax.ShapeDtypeStruct((t, d), F32), jax.ShapeDtypeStruct((t, IN_W), BF16),
                   jax.ShapeDtypeStruct((8, IN_W), F32), jax.ShapeDtypeStruct((8, d), F32),
                   jax.ShapeDtypeStruct((8, d), F32)],
        in_specs=[row(IN_W), row(2 * KV_W), _full((IN_W, d)), row(d), row(d), _full((1, d))],
        out_specs=[row(d), row(IN_W), _full((8, IN_W)), _full((8, d)), _full((8, d))],
        sem=("arbitrary",), comm=comm, args=(dproj, dkvn, w_in_t, x, dz1, sc1))


def _wgrad(name, a, b, tmm, tk, comm=None, a2=None):
    t, m = a.shape
    n = b.shape[1]
    nk = t // tk
    nm = m // tmm

    def body(*refs):
        a_refs, (b_ref, o_ref, acc_ref) = refs[:-3], refs[-3:]
        i, k = pl.program_id(0), pl.program_id(1)

        @pl.when(k == 0)
        def _():
            acc_ref[...] = jnp.zeros_like(acc_ref)

        a_tile = a_refs[0][...] if a2 is None else jnp.where(i < nm, a_refs[0][...], a_refs[1][...])
        acc_ref[...] += _dot_tn(a_tile, b_ref[...])

        @pl.when(k == nk - 1)
        def _():
            o_ref[...] = acc_ref[...].astype(BF16)

    if a2 is None:
        a_specs, a_args, n_tiles = [pl.BlockSpec((tk, tmm), lambda i, k: (k, i))], (a,), nm
    else:
        a_specs = [pl.BlockSpec((tk, tmm), lambda i, k: (jnp.where(i < nm, k, 0), jnp.minimum(i, nm - 1))),
                   pl.BlockSpec((tk, tmm), lambda i, k: (jnp.where(i < nm, 0, k), jnp.maximum(i - nm, 0)))]
        a_args, n_tiles = (a, a2), 2 * nm
    (out,), got = _call(
        body, name=name, grid=(n_tiles, nk), out_shape=[jax.ShapeDtypeStruct((n_tiles * tmm, n), BF16)],
        in_specs=a_specs + [pl.BlockSpec((tk, n), lambda i, k: (k, 0))],
        out_specs=[pl.BlockSpec((tmm, n), lambda i, k: (i, 0))],
        scratch_shapes=[pltpu.VMEM((tmm, n), F32)], sem=("parallel", "arbitrary"), comm=comm, args=a_args + (b,))
    return out if comm is None else (out, got)


def _adamw(w, g, m, v):
    m = ADAM_B1 * m + (1.0 - ADAM_B1) * g
    v = ADAM_B2 * v + (1.0 - ADAM_B2) * (g * g)
    m_hat = m / (1.0 - ADAM_B1 ** ADAM_STEP)
    v_hat = v / (1.0 - ADAM_B2 ** ADAM_STEP)
    delta = -ADAM_LR * (m_hat / (jnp.sqrt(v_hat) + ADAM_EPS) + ADAM_WD * w)
    return delta, m, v


def _adam_reduce(name, parts, w, m, v, tr):
    r, cdim = w.shape
    n_slots = parts.shape[0]

    def body(p_ref, w_ref, m_ref, v_ref, g_ref, d_ref, mo_ref, vo_ref):
        g = p_ref[0].astype(F32)
        for s in range(1, n_slots):
            g = g + p_ref[s].astype(F32)
        d_ref[...], mo_ref[...], vo_ref[...] = _adamw(w_ref[...], g, m_ref[...], v_ref[...])
        g_ref[...] = g

    tile = pl.BlockSpec((tr, cdim), lambda i: (i, 0))
    shp = jax.ShapeDtypeStruct((r, cdim), F32)
    return pl.pallas_call(
        body, name=name, grid=(r // tr,), out_shape=[shp] * 4,
        in_specs=[pl.BlockSpec((n_slots, tr, cdim), lambda i: (0, i, 0)), tile, tile, tile],
        out_specs=[tile] * 4, compiler_params=_params(("parallel",)),
    )(parts, w, m, v)


def _adam_w_ada(c_all_t, dmod_cols, w, m, v):
    def body(ct_ref, dm_ref, w_ref, m_ref, v_ref, g_ref, d_ref, mo_ref, vo_ref):
        ct = ct_ref[...]
        s = (ct * _sigmoid(ct)).astype(BF16)
        g = _dot(s, dm_ref[...].astype(BF16))
        d_ref[...], mo_ref[...], vo_ref[...] = _adamw(w_ref[...], g, m_ref[...], v_ref[...])
        g_ref[...] = g

    shp = jax.ShapeDtypeStruct(w.shape, F32)
    return pl.pallas_call(
        body, name="adam_w_ada", grid=(1,), out_shape=[shp] * 4,
        in_specs=[_full(c_all_t.shape), _full(dmod_cols.shape)] + [_full(w.shape)] * 3,
        out_specs=[_full(w.shape)] * 4, compiler_params=_params(("arbitrary",)),
    )(c_all_t, dmod_cols, w, m, v)


SMALL_EARLY = ["rel_bias", "attn_sinks", "gmlp_ln_g", "gmlp_ln_b", "gmlp_w_s", "gmlp_b_s",
               "attn_out_g", "gmlp_out_g", "ln1_g", "ln1_b", "ln2_g", "ln2_b"]
SMALL_LATE = ["b_ada", "b_in", "loss"]
WEIGHTS = ["rel_bias", "w_ada", "b_ada", "w_in", "b_in", "attn_sinks", "gmlp_ln_g", "gmlp_ln_b", "gmlp_w_s",
           "gmlp_b_s", "attn_out_g", "gmlp_out_g", "w_out", "ln1_g", "ln1_b", "w_gate_up", "w_down", "ln2_g", "ln2_b"]


def _seg_rows(nelem):
    return -(-nelem // 1024) * 8


def _pack(named, names):
    parts = []
    for name in names:
        flat = named[name].reshape(-1).astype(F32)
        rows = _seg_rows(flat.shape[0])
        parts.append(jnp.pad(flat, (0, rows * 128 - flat.shape[0])).reshape(rows, 128))
    return jnp.concatenate(parts, axis=0)


def _adam_small(name, parts, names, wts, mom_m, mom_v):
    params = [n for n in names if n in wts]

    def view(n):
        nelem = math.prod(wts[n].shape)
        return (nelem // 128, 128) if nelem % 128 == 0 else (1, nelem)

    offsets, r0 = {}, 0
    for n in names:
        offsets[n] = r0
        r0 += _seg_rows(math.prod(wts[n].shape) if n in wts else 1)

    def body(*refs):
        p_ref, ins, outs = refs[0], refs[1:1 + 3 * len(params)], refs[1 + 3 * len(params):]

        def total(n, rows, lanes):
            o = offsets[n]
            g = p_ref[0, o:o + rows, 0:lanes]
            for s in range(1, N_DEV):
                g = g + p_ref[s, o:o + rows, 0:lanes]
            return g

        for i, n in enumerate(params):
            g = total(n, *view(n))
            w_ref, m_ref, v_ref = ins[3 * i:3 * i + 3]
            g_ref, d_ref, mo_ref, vo_ref = outs[4 * i:4 * i + 4]
            d_ref[...], mo_ref[...], vo_ref[...] = _adamw(w_ref[...], g, m_ref[...], v_ref[...])
            g_ref[...] = g
        for j, n in enumerate(n for n in names if n not in wts):
            outs[4 * len(params) + j][...] = total(n, 8, 128)

    args, in_specs, out_shape = [parts], [_full(parts.shape)], []
    for n in params:
        args += [t[n].reshape(view(n)) for t in (wts, mom_m, mom_v)]
        in_specs += [_full(view(n))] * 3
        out_shape += [jax.ShapeDtypeStruct(view(n), F32)] * 4
    out_shape += [jax.ShapeDtypeStruct((8, 128), F32) for n in names if n not in wts]
    res = pl.pallas_call(
        body, name=name, grid=(1,), out_shape=out_shape, in_specs=in_specs,
        out_specs=[_full(s.shape) for s in out_shape], compiler_params=_params(("arbitrary",)),
    )(*args)
    done = {n: tuple(r.reshape(wts[n].shape) for r in res[4 * i:4 * i + 4]) for i, n in enumerate(params)}
    sums = {n: res[4 * len(params) + j] for j, n in enumerate(n for n in names if n not in wts)}
    return done, sums


def _t5_bucket_map():
    qi = jnp.arange(BLK)[:, None]
    si = jnp.arange(2 * BLK)[None, :]
    n = jnp.maximum(qi + BLK - si, 0)
    max_exact = N_BUCKETS // 2
    nf = jnp.maximum(n, max_exact).astype(F32)
    large = max_exact + (jnp.log(nf / max_exact) / math.log(MAX_DISTANCE / max_exact)
                         * (N_BUCKETS - max_exact)).astype(jnp.int32)
    large = jnp.minimum(large, N_BUCKETS - 1)
    return jnp.where(n < max_exact, n, large).astype(jnp.int32)


def kernel(x, c, rel_bias, w_ada, b_ada, w_in, b_in, attn_sinks, gmlp_ln_g, gmlp_ln_b, gmlp_w_s, gmlp_b_s, attn_out_g, gmlp_out_g, w_out, ln1_g, ln1_b, w_gate_up, w_down, ln2_g, ln2_b, loss_target, m_rel_bias, m_w_ada, m_b_ada, m_w_in, m_b_in, m_attn_sinks, m_gmlp_ln_g, m_gmlp_ln_b, m_gmlp_w_s, m_gmlp_b_s, m_attn_out_g, m_gmlp_out_g, m_w_out, m_ln1_g, m_ln1_b, m_w_gate_up, m_w_down, m_ln2_g, m_ln2_b, v_rel_bias, v_w_ada, v_b_ada, v_w_in, v_b_in, v_attn_sinks, v_gmlp_ln_g, v_gmlp_ln_b, v_gmlp_w_s, v_gmlp_b_s, v_attn_out_g, v_gmlp_out_g, v_w_out, v_ln1_g, v_ln1_b, v_w_gate_up, v_w_down, v_ln2_g, v_ln2_b):
    wts = dict(rel_bias=rel_bias, w_ada=w_ada, b_ada=b_ada, w_in=w_in, b_in=b_in, attn_sinks=attn_sinks,
               gmlp_ln_g=gmlp_ln_g, gmlp_ln_b=gmlp_ln_b, gmlp_w_s=gmlp_w_s, gmlp_b_s=gmlp_b_s,
               attn_out_g=attn_out_g, gmlp_out_g=gmlp_out_g, w_out=w_out, ln1_g=ln1_g, ln1_b=ln1_b,
               w_gate_up=w_gate_up, w_down=w_down, ln2_g=ln2_g, ln2_b=ln2_b)
    mom_m = dict(rel_bias=m_rel_bias, w_ada=m_w_ada, b_ada=m_b_ada, w_in=m_w_in, b_in=m_b_in,
                 attn_sinks=m_attn_sinks, gmlp_ln_g=m_gmlp_ln_g, gmlp_ln_b=m_gmlp_ln_b, gmlp_w_s=m_gmlp_w_s,
                 gmlp_b_s=m_gmlp_b_s, attn_out_g=m_attn_out_g, gmlp_out_g=m_gmlp_out_g, w_out=m_w_out,
                 ln1_g=m_ln1_g, ln1_b=m_ln1_b, w_gate_up=m_w_gate_up, w_down=m_w_down, ln2_g=m_ln2_g,
                 ln2_b=m_ln2_b)
    mom_v = dict(rel_bias=v_rel_bias, w_ada=v_w_ada, b_ada=v_b_ada, w_in=v_w_in, b_in=v_b_in,
                 attn_sinks=v_attn_sinks, gmlp_ln_g=v_gmlp_ln_g, gmlp_ln_b=v_gmlp_ln_b, gmlp_w_s=v_gmlp_w_s,
                 gmlp_b_s=v_gmlp_b_s, attn_out_g=v_attn_out_g, gmlp_out_g=v_gmlp_out_g, w_out=v_w_out,
                 ln1_g=v_ln1_g, ln1_b=v_ln1_b, w_gate_up=v_w_gate_up, w_down=v_w_down, ln2_g=v_ln2_g,
                 ln2_b=v_ln2_b)

    t = x.shape[1]
    tm = min(512, t)
    tn_ff = D_FF // 2
    tk_long, tk_short = min(4096, t), min(2048, t)
    me = 4 * lax.axis_index("x") + 2 * lax.axis_index("y") + lax.axis_index("c")
    xs = x[0]
    target = loss_target[0]

    (c_g,) = _exchange("gather_c", [jnp.broadcast_to(c, (8, D_MODEL))], ("gather",))
    c_all = c_g[:, 0, :]

    ncol = w_ada.shape[2]
    b_cols = lax.dynamic_slice(b_ada, (0, me * ncol), (1, ncol))
    mod_part = _mod_partial(c_all, w_ada[0], b_cols)
    bucket = _t5_bucket_map()
    (bias,), (mod_g, w_in_g) = _bias_table(rel_bias, bucket,
                                           comm=([mod_part, w_in[0].T.astype(BF16)], ("gather", "gather2")))
    w_in_t = w_in_g.reshape(IN_W, D_MODEL)
    mod = lax.dynamic_slice(mod_g, (0, me, 0), (N_DEV, 1, ncol)).reshape(1, N_DEV * ncol)
    sh1, sc1, g1, sh2, sc2, g2 = [mod[:, i * D_MODEL:(i + 1) * D_MODEL] for i in range(6)]

    causal = jnp.tril(jnp.ones((BLK, BLK), dtype=bool))
    ws = jnp.where(causal[None], gmlp_w_s[0], 0.0).astype(BF16)
    pair = lambda w: jnp.concatenate([w[0::2], w[1::2]], axis=2)
    ws2, wst2 = pair(ws), pair(jnp.swapaxes(ws, 1, 2))
    bfull = jnp.repeat(gmlp_b_s[0].T, GMLP_W // N_GROUPS, axis=1)
    sinks = attn_sinks[0]

    (proj, h1), (w_down_g,) = _inproj(xs, sc1, sh1, w_in_t, b_in, tm, comm=([w_down[0].astype(BF16)], ("gather2",)))
    (mixed, *kept), (w_out_g, w_gu_g) = _mix_fwd(
        proj, bias, sinks, gmlp_ln_g, gmlp_ln_b, ws2, bfull, attn_out_g, gmlp_out_g,
        comm=([w_out[0].astype(BF16), w_gate_up[0].T.astype(BF16)], ("gather2", "gather2")))
    w_out_f = w_out_g.reshape(D_MODEL, D_MODEL)
    w_gu_t = w_gu_g.reshape(2 * D_FF, D_MODEL)
    (y1, x1, h2), _ = _outproj(mixed, w_out_f, xs, g1, ln1_g, ln1_b, sc2, sh2, tm, comm=None)
    w_down_f = w_down_g.reshape(D_FF, D_MODEL)
    dsu, sg, act, dz2, dy2, loss_p, d_ln2g, d_ln2b, d_g2 = _ffn_fwd(h2, w_gu_t, w_down_f, x1, target, g2, ln2_g, ln2_b,
                                                                    min(256, t))

    slots = lambda a: a.reshape(N_DEV, -1, D_MODEL)
    dw_down = _wgrad("wgrad_down", act, dy2, tn_ff, tk_short)
    (dgate, dup, dz1, dy1, d_sc2, d_sh2, d_ln1g, d_ln1b, d_g1), (r_down,) = _ffn_bwd(
        dy2, w_down_f, dsu, sg, w_gu_t, x1, xs, y1, dz2, sc2, g1, ln1_g, min(256, t),
        comm=([slots(dw_down)], ("scatter",)))
    dw_gu_t = _wgrad("wgrad_gate_up", dgate, h2, tn_ff, tk_short, a2=dup)
    dw_out = _wgrad("wgrad_out", mixed, dy1, D_MODEL, tk_long)
    ((dproj, dkvn, dl_acc, dsink_acc, d_lng, d_lnb, d_ws, d_bs, d_aog, d_gog), (r_gu, r_out)) = _mix_bwd(
        proj, gmlp_ln_g, gmlp_ln_b, ws2, wst2, bfull, attn_out_g, gmlp_out_g, dy1, w_out_f.T, kept,
        comm=([slots(dw_gu_t), slots(dw_out)], ("scatter", "scatter")))
    d_relb = _bias_grad(dl_acc, bucket)

    rsum = lambda a: jnp.sum(a, axis=0)
    early_g = dict(
        rel_bias=d_relb[:, 0, :N_BUCKETS].T, attn_sinks=rsum(dsink_acc)[:N_HEADS],
        gmlp_ln_g=rsum(d_lng), gmlp_ln_b=rsum(d_lnb), gmlp_w_s=jnp.where(causal[None], d_ws, 0.0),
        gmlp_b_s=jnp.sum(d_bs.reshape(BLK, N_GROUPS, GMLP_W // N_GROUPS), axis=2).T,
        attn_out_g=rsum(d_aog), gmlp_out_g=rsum(d_gog), ln1_g=rsum(d_ln1g), ln1_b=rsum(d_ln1b),
        ln2_g=rsum(d_ln2g), ln2_b=rsum(d_ln2b))
    (grad_x, dproj_b, d_bin, d_sc1, d_sh1), _ = _din(dproj, dkvn, w_in_t, xs, dz1, sc1, tm, comm=None)
    dw_in_t, (early_all,) = _wgrad("wgrad_in", dproj_b, h1, IN_W // 2, tk_long,
                                   comm=([_pack(early_g, SMALL_EARLY)], ("gather2",)))
    dmod = jnp.concatenate([rsum(d_sh1), rsum(d_sc1), rsum(d_g1), rsum(d_sh2), rsum(d_sc2), rsum(d_g2)])
    late_g = dict(b_ada=dmod, b_in=rsum(d_bin), loss=(0.5 / D_MODEL * jnp.sum(loss_p)).reshape(1))
    late_all, r_in = _scatter_two_level("scatter_in", _pack(late_g, SMALL_LATE), slots(dw_in_t))

    small, _ = _adam_small("adam_small_early", early_all, SMALL_EARLY, wts, mom_m, mom_v)
    small_late, sums = _adam_small("adam_small_late", late_all, SMALL_LATE, wts, mom_m, mom_v)
    small.update(small_late)
    loss = sums["loss"][0, 0]

    dmod_all = late_all[:, :_seg_rows(6 * D_MODEL), :].reshape(N_DEV, 6 * D_MODEL)
    dmod_cols = lax.dynamic_slice(dmod_all, (0, me * ncol), (N_DEV, ncol))
    kpad = 128 - N_DEV
    ada = _adam_w_ada(jnp.pad(c_all.T, ((0, 0), (0, kpad))), jnp.pad(dmod_cols, ((0, kpad), (0, 0))),
                      w_ada[0], m_w_ada[0], v_w_ada[0])

    tr = lambda a: jnp.swapaxes(a, -1, -2)
    big = {}
    big["w_in"] = [tr(o)[None] for o in _adam_reduce("adam_w_in", r_in, w_in[0].T, m_w_in[0].T, v_w_in[0].T, 112)]
    big["w_out"] = [o[None] for o in _adam_reduce("adam_w_out", r_out, w_out[0], m_w_out[0], v_w_out[0], 128)]
    big["w_gate_up"] = [tr(o)[None] for o in _adam_reduce("adam_w_gu", r_gu, w_gate_up[0].T, m_w_gate_up[0].T,
                                                           v_w_gate_up[0].T, 352)]
    big["w_down"] = [o[None] for o in _adam_reduce("adam_w_down", r_down, w_down[0], m_w_down[0], v_w_down[0], 176)]
    big["w_ada"] = [o[None] for o in ada]

    outs = [[], [], [], []]
    for name in WEIGHTS:
        for i in range(4):
            outs[i].append(big[name][i] if name in big else small[name][i])
    return (loss, grad_x[None], *outs[0], *outs[1], *outs[2], *outs[3])
```

```python
import math

import jax
import jax.numpy as jnp
from jax import lax
from jax.experimental import pallas as pl
from jax.experimental.pallas import tpu as pltpu

F32 = jnp.float32
BF16 = jnp.bfloat16
MESH = pl.DeviceIdType.MESH

N_DEV = 8
D_MODEL = 1024
HEAD_DIM = 64
N_HEADS = 8
N_GROUPS = 8
ATTN_W = 512
KV_W = 128
GMLP_W = 512
IN_W = 1792
BLK = 128
N_BUCKETS = 32
MAX_DISTANCE = 128
D_FF = 2816
ALPHA = 2.0 ** 0.25
LN_EPS = 1e-5
NEG_INF = -1e30
ADAM_LR = 0.001
ADAM_B1 = 0.9
ADAM_B2 = 0.999
ADAM_EPS = 1e-08
ADAM_WD = 0.01
ADAM_STEP = 10
GELU_C0 = math.sqrt(2.0 / math.pi)
GELU_C1 = 0.044715

VMEM_LIMIT = 56 * 1024 * 1024


def _params(sem):
    return pltpu.CompilerParams(dimension_semantics=sem, vmem_limit_bytes=VMEM_LIMIT)


def _dot(a, b):
    return lax.dot_general(a, b, (((1,), (0,)), ((), ())), preferred_element_type=F32)


def _dot_nt(a, b):
    return lax.dot_general(a, b, (((1,), (1,)), ((), ())), preferred_element_type=F32)


def _dot_tn(a, b):
    return lax.dot_general(a, b, (((0,), (0,)), ((), ())), preferred_element_type=F32)


def _full(shape):
    nd = len(shape)
    return pl.BlockSpec(shape, lambda *_: (0,) * nd)


def _rowsum8(v):
    r, c = v.shape
    return jnp.sum(v.reshape(r // 8, 8, c), axis=0)


def _sigmoid(v):
    return 1.0 / (1.0 + jnp.exp(-v))


def _gelu_parts(v):
    v2 = v * v
    t = jnp.tanh(GELU_C0 * (v + GELU_C1 * v * v2))
    g = 0.5 * v * (1.0 + t)
    dg = 0.5 * (1.0 + t) + 0.5 * v * (1.0 - t * t) * (GELU_C0 * (1.0 + 3.0 * GELU_C1 * v2))
    return g, dg


def _ln_stats(z):
    mu = jnp.mean(z, axis=1, keepdims=True)
    zc = z - mu
    var = jnp.mean(zc * zc, axis=1, keepdims=True)
    rstd = lax.rsqrt(var + LN_EPS)
    return zc * rstd, rstd


def _ln_bwd(dxhat, xhat, rstd):
    m1 = jnp.mean(dxhat, axis=1, keepdims=True)
    m2 = jnp.mean(dxhat * xhat, axis=1, keepdims=True)
    return rstd * (dxhat - m1 - xhat * m2)


def _seg_mean64(v):
    r = v.shape[0]
    lo = lax.broadcasted_iota(jnp.int32, (r, 128), 1) < 64
    outs = []
    for j in range(v.shape[1] // 128):
        ch = v[:, 128 * j:128 * (j + 1)]
        s_lo = jnp.sum(jnp.where(lo, ch, 0.0), axis=1, keepdims=True)
        s_hi = jnp.sum(jnp.where(lo, 0.0, ch), axis=1, keepdims=True)
        outs.append(jnp.where(lo, s_lo, s_hi) * (1.0 / 64.0))
    return jnp.concatenate(outs, axis=1)


def _rms(a, g):
    r = lax.rsqrt(jnp.mean(a * a, axis=1, keepdims=True) + LN_EPS)
    return a * r * g, r


def _rms_bwd(dout, a, r, g):
    t = dout * g
    return r * t - a * (r * r * r) * jnp.mean(t * a, axis=1, keepdims=True)


PEER_ORDER = (1, 2, 4, 3, 5, 6, 7)


def _peer(j):
    x, y, c = lax.axis_index("x"), lax.axis_index("y"), lax.axis_index("c")
    px = 1 - x if j & 4 else x
    py = 1 - y if j & 2 else y
    pc = 1 - c if j & 1 else c
    return (px, py, pc), 4 * px + 2 * py + pc


SIBLING = 1
CHIP_FLIPS = (4, 2, 6)


def _exchange_phase(phase, ins, outs, modes, send_sems, recv_sems, loc_sems):
    me = 4 * lax.axis_index("x") + 2 * lax.axis_index("y") + lax.axis_index("c")
    for k, mode in enumerate(modes):
        def copy(i, src, slot, dev, k=k):
            return pltpu.make_async_remote_copy(src_ref=src, dst_ref=outs[k].at[slot], send_sem=send_sems.at[k, i],
                                                recv_sem=recv_sems.at[k, i], device_id=dev, device_id_type=MESH)

        src_me = ins[k].at[me] if mode == "scatter" else ins[k]
        local = pltpu.make_async_copy(src_me, outs[k].at[me], loc_sems.at[k])
        if mode == "gather2":
            sib_dev, sib_idx = _peer(SIBLING)
            chips = [_peer(j) for j in CHIP_FLIPS]
            far = [_peer(j | SIBLING)[1] for j in CHIP_FLIPS]
            if phase == "start":
                local.start()
                copy(0, ins[k], me, sib_dev).start()
                for i, (dev, _) in enumerate(chips):
                    copy(1 + i, ins[k], me, dev).start()
            elif phase == "mid":
                for i, (dev, idx) in enumerate(chips):
                    copy(1 + i, ins[k], idx, dev).wait_recv()
                    copy(4 + i, outs[k].at[idx], idx, sib_dev).start()
            else:
                copy(0, ins[k], sib_idx, sib_dev).wait_recv()
                for i, slot in enumerate(far):
                    copy(4 + i, ins[k], slot, sib_dev).wait_recv()
                copy(0, ins[k], me, sib_dev).wait_send()
                for i, (dev, idx) in enumerate(chips):
                    copy(1 + i, ins[k], me, dev).wait_send()
                    copy(4 + i, outs[k].at[idx], idx, sib_dev).wait_send()
                local.wait()
            continue
        peers = [_peer(j) for j in PEER_ORDER]
        if phase == "start":
            local.start()
            for i, (dev, idx) in enumerate(peers):
                copy(i, ins[k].at[idx] if mode == "scatter" else ins[k], me, dev).start()
        elif phase == "end":
            for i, (dev, idx) in enumerate(peers):
                copy(i, src_me, idx, dev).wait_recv()
            for i, (dev, idx) in enumerate(peers):
                copy(i, src_me, me, dev).wait_send()
            local.wait()


def _exchange_shapes(arrays, modes):
    return [jax.ShapeDtypeStruct((N_DEV,) + (a.shape[1:] if m == "scatter" else a.shape), a.dtype)
            for a, m in zip(arrays, modes)]


def _exchange_sems(n):
    return [pltpu.SemaphoreType.DMA((n, N_DEV - 1)), pltpu.SemaphoreType.DMA((n, N_DEV - 1)),
            pltpu.SemaphoreType.DMA((n,))]


def _exchange(name, arrays, modes):
    n = len(arrays)

    def body(*refs):
        for phase in ("start", "mid", "end"):
            _exchange_phase(phase, refs[:n], refs[n:2 * n], modes, *refs[2 * n:])

    any_spec = pl.BlockSpec(memory_space=pl.ANY)
    return pl.pallas_call(
        body, name=name, out_shape=_exchange_shapes(arrays, modes),
        in_specs=[any_spec] * n, out_specs=[any_spec] * n, scratch_shapes=_exchange_sems(n),
    )(*arrays)


N_CHIP = 4


def _scatter_two_level(name, pack, parts):
    r, ncols = parts.shape[1:]

    def body(pack_ref, parts_ref, late_ref, got_ref, sib_ref, h_ref, g_send, g_recv, g_loc, d_send, d_recv, i_send, i_recv):
        x, y, c = lax.axis_index("x"), lax.axis_index("y"), lax.axis_index("c")
        my_chip = 2 * x + y
        sib_dev, _ = _peer(SIBLING)
        gather = ([pack_ref], [late_ref], ("gather",), g_send, g_recv, g_loc)
        _exchange_phase("start", *gather)

        def to_sibling(q):
            return pltpu.make_async_remote_copy(src_ref=parts_ref.at[2 * q + 1 - c], dst_ref=sib_ref.at[q],
                                                send_sem=d_send.at[q], recv_sem=d_recv.at[q],
                                                device_id=sib_dev, device_id_type=MESH)

        for q in range(N_CHIP):
            to_sibling(q).start()
        for q in range(N_CHIP):
            to_sibling(q).wait_recv()
            h_ref[q] = (parts_ref[2 * q + c].astype(F32) + sib_ref[q].astype(F32)).astype(BF16)

        def to_chip(i, slot):
            dev, idx = _peer(CHIP_FLIPS[i])
            return pltpu.make_async_remote_copy(src_ref=h_ref.at[idx // 2], dst_ref=got_ref.at[slot],
                                                send_sem=i_send.at[i], recv_sem=i_recv.at[i],
                                                device_id=dev, device_id_type=MESH)

        for i in range(len(CHIP_FLIPS)):
            to_chip(i, my_chip).start()
        got_ref[my_chip] = h_ref[my_chip]
        for i in range(len(CHIP_FLIPS)):
            to_chip(i, _peer(CHIP_FLIPS[i])[1] // 2).wait_recv()
        for i in range(len(CHIP_FLIPS)):
            to_chip(i, my_chip).wait_send()
        for q in range(N_CHIP):
            to_sibling(q).wait_send()
        _exchange_phase("end", *gather)

    any_spec = pl.BlockSpec(memory_space=pl.ANY)
    vmem = pl.BlockSpec(memory_space=pltpu.VMEM)
    dma = pltpu.SemaphoreType.DMA
    return pl.pallas_call(
        body, name=name,
        out_shape=[jax.ShapeDtypeStruct((N_DEV,) + pack.shape, pack.dtype),
                   jax.ShapeDtypeStruct((N_CHIP, r, ncols), parts.dtype)],
        in_specs=[any_spec, vmem], out_specs=[any_spec, vmem],
        scratch_shapes=[pltpu.VMEM((N_CHIP, r, ncols), parts.dtype), pltpu.VMEM((N_CHIP, r, ncols), parts.dtype),
                        dma((1, N_DEV - 1)), dma((1, N_DEV - 1)), dma((1,)),
                        dma((N_CHIP,)), dma((N_CHIP,)), dma((len(CHIP_FLIPS),)), dma((len(CHIP_FLIPS),))],
        compiler_params=pltpu.CompilerParams(vmem_limit_bytes=VMEM_LIMIT),
    )(pack, parts)


def _call(body, *, name, grid, in_specs, out_specs, out_shape, args, sem, scratch_shapes=(), comm=None):
    if comm is None:
        outs = pl.pallas_call(body, name=name, grid=grid, in_specs=list(in_specs), out_specs=list(out_specs),
                              out_shape=list(out_shape), scratch_shapes=list(scratch_shapes),
                              compiler_params=_params(sem))(*args)
        return list(outs), []
    arrays, modes = comm
    n_in, n_out, nc, ns = len(in_specs), len(out_specs), len(arrays), len(scratch_shapes)
    n_steps = math.prod(grid)

    def hosted(*refs):
        ins, cins = refs[:n_in], refs[n_in:n_in + nc]
        outs, couts = refs[n_in + nc:n_in + nc + n_out], refs[n_in + nc + n_out:n_in + 2 * nc + n_out]
        scratch = refs[n_in + 2 * nc + n_out:]
        ex = (cins, couts, modes) + tuple(scratch[ns:])
        step = pl.program_id(0)
        for ax in range(1, len(grid)):
            step = step * grid[ax] + pl.program_id(ax)

        @pl.when(step == 0)
        def _():
            _exchange_phase("start", *ex)

        body(*ins, *outs, *scratch[:ns])

        if "gather2" in modes:
            @pl.when(step == (3 * n_steps) // 4)
            def _():
                _exchange_phase("mid", *ex)

        @pl.when(step == n_steps - 1)
        def _():
            _exchange_phase("end", *ex)

    any_spec = pl.BlockSpec(memory_space=pl.ANY)
    res = pl.pallas_call(
        hosted, name=name, grid=grid, in_specs=list(in_specs) + [any_spec] * nc,
        out_specs=list(out_specs) + [any_spec] * nc, out_shape=list(out_shape) + _exchange_shapes(arrays, modes),
        scratch_shapes=list(scratch_shapes) + _exchange_sems(nc),
        compiler_params=_params(tuple("arbitrary" for _ in grid)))(*args, *arrays)
    return list(res[:n_out]), list(res[n_out:])


def _mod_partial(c_all, w_ada, b_ada_cols):
    def body(c_ref, w_ref, b_ref, o_ref):
        cv = c_ref[...]
        s = (cv * _sigmoid(cv)).astype(BF16)
        o_ref[...] = _dot(s, w_ref[...].astype(BF16)) + b_ref[...]

    ncol = w_ada.shape[1]
    return pl.pallas_call(
        body, name="mod_partial", out_shape=jax.ShapeDtypeStruct((N_DEV, ncol), F32),
        in_specs=[_full(c_all.shape), _full(w_ada.shape), _full(b_ada_cols.shape)],
        out_specs=_full((N_DEV, ncol)), grid=(1,), compiler_params=_params(("arbitrary",)),
    )(c_all, w_ada, b_ada_cols)


def _bias_table(rel_bias, bucket, comm):
    def body(rb_ref, bk_ref, o_ref):
        h = pl.program_id(0)
        bk = bk_ref[...]
        acc = jnp.zeros((BLK, 2 * BLK), F32)
        for b in range(N_BUCKETS):
            acc = jnp.where(bk == b, rb_ref[b, h], acc)
        dist = (lax.broadcasted_iota(jnp.int32, (BLK, 2 * BLK), 0) + BLK
                - lax.broadcasted_iota(jnp.int32, (BLK, 2 * BLK), 1))
        o_ref[0] = jnp.where((dist >= 0) & (dist < BLK), acc, NEG_INF)

    return _call(
        body, name="bias_table", out_shape=[jax.ShapeDtypeStruct((N_HEADS, BLK, 2 * BLK), F32)],
        in_specs=[pl.BlockSpec(memory_space=pltpu.SMEM), _full((BLK, 2 * BLK))],
        out_specs=[pl.BlockSpec((1, BLK, 2 * BLK), lambda h: (h, 0, 0))], grid=(N_HEADS,),
        sem=("arbitrary",), comm=comm, args=(rel_bias, bucket))


def _bias_grad(dl_acc, bucket):
    def body(dl_ref, bk_ref, o_ref):
        bk = bk_ref[...]
        dl = dl_ref[0]
        lane = lax.broadcasted_iota(jnp.int32, (1, 128), 1)
        row = jnp.zeros((1, 128), F32)
        for b in range(N_BUCKETS):
            s = jnp.sum(jnp.sum(jnp.where(bk == b, dl, 0.0), axis=1, keepdims=True), axis=0, keepdims=True)
            row = jnp.where(lane == b, s, row)
        o_ref[0] = row

    return pl.pallas_call(
        body, name="bias_grad", out_shape=jax.ShapeDtypeStruct((N_HEADS, 1, 128), F32),
        in_specs=[pl.BlockSpec((1, BLK, 2 * BLK), lambda h: (h, 0, 0)), _full((BLK, 2 * BLK))],
        out_specs=pl.BlockSpec((1, 1, 128), lambda h: (h, 0, 0)), grid=(N_HEADS,),
        compiler_params=_params(("arbitrary",)),
    )(dl_acc, bucket)


def _inproj(x, sc1, sh1, w_in_t, b_in, tm, comm):
    t, d = x.shape
    n = w_in_t.shape[0]

    def body(x_ref, sc_ref, sh_ref, w_ref, b_ref, proj_ref, h_ref):
        h = (x_ref[...] * (1.0 + sc_ref[...]) + sh_ref[...]).astype(BF16)
        h_ref[...] = h
        proj_ref[...] = _dot_nt(h, w_ref[...]) + b_ref[...]

    row = lambda w: pl.BlockSpec((tm, w), lambda i: (i, 0))
    return _call(
        body, name="inproj", grid=(t // tm,),
        out_shape=[jax.ShapeDtypeStruct((t, n), F32), jax.ShapeDtypeStruct((t, d), BF16)],
        in_specs=[row(d), _full((1, d)), _full((1, d)), _full((n, d)), _full((1, n))],
        out_specs=[row(n), row(d)], sem=("parallel",), comm=comm, args=(x, sc1, sh1, w_in_t, b_in))


HALF = 64
ROWS = 32


def _lane_lo(rows):
    return lax.broadcasted_iota(jnp.int32, (rows, 128), 1) < 64


def _mix_stage_kv(proj_ref, kvp_ref, s):
    lo = _lane_lo(2 * BLK)
    for name, col in (("k", ATTN_W), ("v", ATTN_W + KV_W)):
        cur = jnp.concatenate([kvp_ref[:, col - ATTN_W:col - ATTN_W + KV_W], proj_ref[:, col:col + KV_W]], axis=0)
        plain, swapped = cur.astype(BF16), pltpu.roll(cur, 64, 1).astype(BF16)
        zero = jnp.zeros_like(plain)
        for g in range(2):
            dup = jnp.where(lo, plain, swapped) if g == 0 else jnp.where(lo, swapped, plain)
            s[name + "d"][g] = dup
            s[name + "m"][g] = jnp.concatenate([jnp.where(lo, dup, zero), jnp.where(lo, zero, dup)], axis=0)


def _group_rows(ref, g):
    return ref[4 * g:4 * g + 4].reshape(4 * BLK, ref.shape[2])


def _pair_rows(ref, g):
    return jnp.concatenate([jnp.concatenate([ref[4 * g + 2 * c], ref[4 * g + 2 * c + 1]], axis=1) for c in range(2)],
                           axis=0)


def _mask_heads(src_ref, dst_ref):
    lo = _lane_lo(BLK)
    for j in range(4):
        chunk = src_ref[:, 128 * j:128 * (j + 1)]
        dst_ref[2 * j] = jnp.where(lo, chunk, 0.0).astype(BF16)
        dst_ref[2 * j + 1] = jnp.where(lo, 0.0, chunk).astype(BF16)


def _mix_stage_attn(proj_ref, bias_ref, sinks_ref, n, s):
    _mask_heads(proj_ref, s["qm"])
    for g in range(2):
        s["lg"][g] = _dot_nt(_group_rows(s["qm"], g), s["kd"][g])
    n0mask = (n == 0) & (lax.broadcasted_iota(jnp.int32, (HALF, 2 * BLK), 1) < BLK)
    lane = lax.broadcasted_iota(jnp.int32, (HALF, 128), 1)
    for hf in range(BLK // HALF):
        rows = slice(HALF * hf, HALF * (hf + 1))
        psink = jnp.zeros((HALF, 128), F32)
        for h in range(N_HEADS):
            sk = sinks_ref[h]
            grows = slice(BLK * (h % 4) + HALF * hf, BLK * (h % 4) + HALF * (hf + 1))
            logit = s["lg"][h // 4, grows, :] * (HEAD_DIM ** -0.5) + bias_ref[h, rows, :]
            logit = jnp.where(n0mask, NEG_INF, logit)
            m = jnp.maximum(jnp.max(logit, axis=1, keepdims=True), sk)
            e = jnp.exp(logit - m)
            es = jnp.exp(sk - m)
            inv = 1.0 / (jnp.sum(e, axis=1, keepdims=True) + es)
            p = e * inv
            s["p"][h, rows, :] = p
            s["pb"][h, rows, :] = p.astype(BF16)
            psink = jnp.where(lane == h, es * inv, psink)
        s["psink"][rows, :] = psink
    for g in range(2):
        out = _dot(_pair_rows(s["pb"], g), s["vm"][g])
        s["attn"][:, 256 * g:256 * g + 128] = out[0:BLK]
        s["attn"][:, 256 * g + 128:256 * g + 256] = out[BLK:2 * BLK]


def _mix_stage_gmlp_pre(proj_ref, lng, lnb, s, keep):
    c0 = ATTN_W + 2 * KV_W
    for r0 in range(0, BLK, ROWS):
        rows = slice(r0, r0 + ROWS)
        u, du = _gelu_parts(proj_ref[rows, c0:c0 + GMLP_W])
        a, da = _gelu_parts(proj_ref[rows, c0 + GMLP_W:c0 + 2 * GMLP_W])
        ac = a - _seg_mean64(a)
        rstd = lax.rsqrt(_seg_mean64(ac * ac) + LN_EPS)
        vhat = ac * rstd
        s["u"][rows, :] = u
        s["vnb"][rows, :] = (vhat * lng + lnb).astype(BF16)
        if keep:
            s["du"][rows, :] = du
            s["da"][rows, :] = da
            s["vhat"][rows, :] = vhat
            s["rstd"][rows, :] = rstd


def _stack_halves(chunk):
    lo = _lane_lo(BLK)
    zero = jnp.zeros_like(chunk)
    return jnp.concatenate([jnp.where(lo, chunk, zero), jnp.where(lo, zero, chunk)], axis=0)


def _mix_stage_gmlp_mix(ws2_ref, bfull_ref, s):
    for j in range(4):
        cols = slice(128 * j, 128 * (j + 1))
        s["ms"][:, cols] = _dot(ws2_ref[j], _stack_halves(s["vnb"][:, cols])) + bfull_ref[:, cols]


def _mix_scratch(keep):
    f32 = lambda *shape: pltpu.VMEM(shape, F32)
    b16 = lambda *shape: pltpu.VMEM(shape, BF16)
    names = dict(kd=b16(2, 2 * BLK, 128), vd=b16(2, 2 * BLK, 128), km=b16(2, 4 * BLK, 128), vm=b16(2, 4 * BLK, 128),
                 qm=b16(N_HEADS, BLK, 128), lg=f32(2, 4 * BLK, 2 * BLK), pb=b16(N_HEADS, BLK, 2 * BLK),
                 u=f32(BLK, GMLP_W), vnb=b16(BLK, GMLP_W), ms=f32(BLK, GMLP_W))
    if keep:
        names.update(dom=b16(N_HEADS, BLK, 128), dls=b16(N_HEADS, BLK, 2 * BLK),
                     dattn=f32(BLK, ATTN_W), dmix=f32(BLK, D_MODEL), du=f32(BLK, GMLP_W), da=f32(BLK, GMLP_W),
                     vhat=f32(BLK, GMLP_W), rstd=f32(BLK, GMLP_W), dmsb=b16(BLK, GMLP_W), dvn=f32(BLK, GMLP_W))
    return list(names), list(names.values())


def _mix_specs(with_logit_inputs):
    logit_inputs = [_full((N_HEADS, BLK, 2 * BLK)), pl.BlockSpec(memory_space=pltpu.SMEM)] if with_logit_inputs else []
    return [pl.BlockSpec((BLK, IN_W), lambda n: (n, 0)),
            pl.BlockSpec((BLK, 2 * KV_W), lambda n: (jnp.maximum(n - 1, 0), ATTN_W // (2 * KV_W)))] + logit_inputs + [
            _full((1, GMLP_W)), _full((1, GMLP_W)),
            _full((N_GROUPS // 2, BLK, 2 * BLK)), _full((BLK, GMLP_W)),
            _full((1, ATTN_W)), _full((1, GMLP_W))]


KEPT = [("p", (N_HEADS, BLK, 2 * BLK), F32), ("psink", (BLK, 128), F32), ("attn", (BLK, ATTN_W), F32)]


def _kept_shapes(t):
    full = lambda blk: (blk[0], t, blk[2]) if len(blk) == 3 else (t, blk[1])
    return [jax.ShapeDtypeStruct(full(blk), dt) for _, blk, dt in KEPT]


def _kept_specs():
    return [pl.BlockSpec(blk, (lambda n: (0, n, 0)) if len(blk) == 3 else (lambda n: (n, 0))) for _, blk, _ in KEPT]


def _mix_fwd(proj, bias, sinks, lng, lnb, ws2, bfull, aog, gog, comm):
    t = proj.shape[0]
    names, shapes = _mix_scratch(False)

    def body(proj_ref, kvp_ref, bias_ref, sinks_ref, lng_ref, lnb_ref, ws2_ref, bfull_ref, aog_ref, gog_ref,
             out_ref, *rest):
        s = dict(zip([name for name, _, _ in KEPT] + names, rest))
        n = pl.program_id(0)
        _mix_stage_kv(proj_ref, kvp_ref, s)
        _mix_stage_attn(proj_ref, bias_ref, sinks_ref, n, s)
        _mix_stage_gmlp_pre(proj_ref, lng_ref[...], lnb_ref[...], s, False)
        _mix_stage_gmlp_mix(ws2_ref, bfull_ref, s)
        for r0 in range(0, BLK, ROWS):
            rows = slice(r0, r0 + ROWS)
            out_ref[rows, 0:ATTN_W] = _rms(s["attn"][rows, :], aog_ref[...])[0].astype(BF16)
            out_ref[rows, ATTN_W:ATTN_W + GMLP_W] = _rms(s["u"][rows, :] * s["ms"][rows, :], gog_ref[...])[0].astype(BF16)

    return _call(
        body, name="mix_fwd", grid=(t // BLK,),
        out_shape=[jax.ShapeDtypeStruct((t, D_MODEL), BF16)] + _kept_shapes(t),
        in_specs=_mix_specs(True), out_specs=[pl.BlockSpec((BLK, D_MODEL), lambda n: (n, 0))] + _kept_specs(),
        scratch_shapes=shapes,
        sem=("parallel",), comm=comm, args=(proj, proj, bias, sinks, lng, lnb, ws2, bfull, aog, gog))


def _mix_bwd(proj, lng, lnb, ws2, wst2, bfull, aog, gog, dy, w_out, kept, comm):
    t = proj.shape[0]
    nb = t // BLK
    names, shapes = _mix_scratch(True)
    c_gu = ATTN_W + 2 * KV_W

    def body(proj_ref, kvp_ref, lng_ref, lnb_ref, ws2_ref, bfull_ref, aog_ref, gog_ref,
             wst2_ref, dy_ref, wout_ref, *rest):
        n_kept = len(KEPT)
        s = dict(zip([name for name, _, _ in KEPT], rest[:n_kept]))
        (dproj_ref, dkvn_ref, dl_ref, dsink_ref, dlng_ref, dlnb_ref, dws_ref, dbs_ref, daog_ref,
         dgog_ref) = rest[n_kept:n_kept + 10]
        s.update(zip(names, rest[n_kept + 10:]))
        n = pl.program_id(0)

        @pl.when(n == 0)
        def _():
            for r in (dl_ref, dsink_ref, dlng_ref, dlnb_ref, dws_ref, dbs_ref, daog_ref, dgog_ref):
                r[...] = jnp.zeros_like(r)

        s["dmix"][...] = _dot(dy_ref[...], wout_ref[...])
        _mix_stage_kv(proj_ref, kvp_ref, s)
        _mask_heads(proj_ref, s["qm"])
        lng = lng_ref[...]
        _mix_stage_gmlp_pre(proj_ref, lng, lnb_ref[...], s, True)
        _mix_stage_gmlp_mix(ws2_ref, bfull_ref, s)

        aog, gog = aog_ref[...], gog_ref[...]
        for r0 in range(0, BLK, ROWS):
            rows = slice(r0, r0 + ROWS)
            attn, dma = s["attn"][rows, :], s["dmix"][rows, 0:ATTN_W]
            _, r_a = _rms(attn, aog)
            daog_ref[...] += _rowsum8(dma * attn * r_a)
            s["dattn"][rows, :] = _rms_bwd(dma, attn, r_a, aog)
            u, ms, dmg = s["u"][rows, :], s["ms"][rows, :], s["dmix"][rows, ATTN_W:ATTN_W + GMLP_W]
            gm = u * ms
            _, r_g = _rms(gm, gog)
            dgog_ref[...] += _rowsum8(dmg * gm * r_g)
            dgm = _rms_bwd(dmg, gm, r_g, gog)
            dproj_ref[rows, c_gu:c_gu + GMLP_W] = dgm * ms * s["du"][rows, :]
            dms = dgm * u
            dbs_ref[rows, :] += dms
            s["dmsb"][rows, :] = dms.astype(BF16)

        _mask_heads(s["dattn"], s["dom"])
        for g in range(2):
            s["lg"][g] = _dot_nt(_group_rows(s["dom"], g), s["vd"][g])
        lane = lax.broadcasted_iota(jnp.int32, (HALF, 128), 1)
        for hf in range(BLK // HALF):
            rows = slice(HALF * hf, HALF * (hf + 1))
            dsink = jnp.zeros((HALF, 128), F32)
            for h in range(N_HEADS):
                grows = slice(BLK * (h % 4) + HALF * hf, BLK * (h % 4) + HALF * (hf + 1))
                dp = s["lg"][h // 4, grows, :]
                p = s["p"][h, rows, :]
                s["pb"][h, rows, :] = p.astype(BF16)
                rs = jnp.sum(p * dp, axis=1, keepdims=True)
                dl = p * (dp - rs)
                dl_ref[h, rows, :] += dl
                dsink = dsink + jnp.where(lane == h, -s["psink"][rows, :] * rs, 0.0)
                s["dls"][h, rows, :] = (dl * (HEAD_DIM ** -0.5)).astype(BF16)
            dsink_ref[rows, :] += dsink
        for g in range(2):
            dq = _dot(_pair_rows(s["dls"], g), s["km"][g])
            dproj_ref[:, 256 * g:256 * g + 128] = dq[0:BLK]
            dproj_ref[:, 256 * g + 128:256 * g + 256] = dq[BLK:2 * BLK]
        lo_k = _lane_lo(2 * BLK)
        for col, lhs, rhs in ((0, "dls", "qm"), (KV_W, "pb", "dom")):
            raw = [_dot_tn(_group_rows(s[lhs], g), _group_rows(s[rhs], g)) for g in range(2)]
            both = [r + pltpu.roll(r, 64, 1) for r in raw]
            dkv = jnp.where(lo_k, both[0], both[1])
            dproj_ref[:, ATTN_W + col:ATTN_W + col + KV_W] = dkv[BLK:2 * BLK]
            dkvn_ref[:, col:col + KV_W] = dkv[0:BLK]

        for j in range(4):
            cols = slice(128 * j, 128 * (j + 1))
            dm2 = _stack_halves(s["dmsb"][:, cols])
            vnb = s["vnb"][:, cols]
            dws2 = _dot_nt(dm2, vnb)
            dws_ref[2 * j] += dws2[0:BLK]
            dws_ref[2 * j + 1] += dws2[BLK:2 * BLK]
            s["dvn"][:, cols] = _dot(wst2_ref[j], dm2)
        for r0 in range(0, BLK, ROWS):
            rows = slice(r0, r0 + ROWS)
            dvn, vhat = s["dvn"][rows, :], s["vhat"][rows, :]
            dlng_ref[...] += _rowsum8(dvn * vhat)
            dlnb_ref[...] += _rowsum8(dvn)
            dvh = dvn * lng
            dact = s["rstd"][rows, :] * (dvh - _seg_mean64(dvh) - vhat * _seg_mean64(dvh * vhat))
            dproj_ref[rows, c_gu + GMLP_W:IN_W] = dact * s["da"][rows, :]

    acc8 = lambda w: jax.ShapeDtypeStruct((8, w), F32)
    out_shape = [jax.ShapeDtypeStruct((t, IN_W), F32), jax.ShapeDtypeStruct((t, 2 * KV_W), F32),
                 jax.ShapeDtypeStruct((N_HEADS, BLK, 2 * BLK), F32), jax.ShapeDtypeStruct((BLK, 128), F32),
                 acc8(GMLP_W), acc8(GMLP_W), jax.ShapeDtypeStruct((N_GROUPS, BLK, BLK), F32),
                 jax.ShapeDtypeStruct((BLK, GMLP_W), F32), acc8(ATTN_W), acc8(GMLP_W)]
    out_specs = [pl.BlockSpec((BLK, IN_W), lambda n: (n, 0)),
                 pl.BlockSpec((BLK, 2 * KV_W), lambda n: ((n + nb - 1) % nb, 0)),
                 _full((N_HEADS, BLK, 2 * BLK)), _full((BLK, 128)), _full((8, GMLP_W)), _full((8, GMLP_W)),
                 _full((N_GROUPS, BLK, BLK)), _full((BLK, GMLP_W)), _full((8, ATTN_W)), _full((8, GMLP_W))]
    in_specs = _mix_specs(False) + [_full((N_GROUPS // 2, BLK, 2 * BLK)),
                               pl.BlockSpec((BLK, D_MODEL), lambda n: (n, 0)),
                               _full((D_MODEL, D_MODEL))] + _kept_specs()
    return _call(
        body, name="mix_bwd", grid=(nb,), out_shape=out_shape, in_specs=in_specs, out_specs=out_specs,
        scratch_shapes=shapes, sem=("arbitrary",), comm=comm,
        args=(proj, proj, lng, lnb, ws2, bfull, aog, gog, wst2, dy, w_out, *kept))


def _outproj(mixed, w_out, x, g1, ln1g, ln1b, sc2, sh2, tm, comm):
    t, d = x.shape

    def body(mx_ref, w_ref, x_ref, g1_ref, lg_ref, lb_ref, sc_ref, sh_ref, y_ref, x1_ref, h2_ref):
        y = _dot(mx_ref[...], w_ref[...])
        xhat, _ = _ln_stats(ALPHA * x_ref[...] + g1_ref[...] * y)
        x1 = xhat * lg_ref[...] + lb_ref[...]
        y_ref[...] = y
        x1_ref[...] = x1
        h2_ref[...] = (x1 * (1.0 + sc_ref[...]) + sh_ref[...]).astype(BF16)

    row = pl.BlockSpec((tm, d), lambda i: (i, 0))
    vec = _full((1, d))
    return _call(
        body, name="outproj", grid=(t // tm,),
        out_shape=[jax.ShapeDtypeStruct((t, d), F32), jax.ShapeDtypeStruct((t, d), F32),
                   jax.ShapeDtypeStruct((t, d), BF16)],
        in_specs=[row, _full((d, d)), row, vec, vec, vec, vec, vec], out_specs=[row, row, row],
        sem=("parallel",), comm=comm, args=(mixed, w_out, x, g1, ln1g, ln1b, sc2, sh2))


def _ffn_fwd(h2, w_gu_t, w_down, x1, target, g2, ln2g, ln2b, tm):
    t, d = x1.shape

    def body(h_ref, w_ref, wd_ref, x1_ref, tg_ref, g2_ref, lg_ref, lb_ref,
             dsu_ref, sg_ref, act_ref, dz_ref, dy_ref, loss_ref, dlg_ref, dlb_ref, dg2_ref):
        @pl.when(pl.program_id(0) == 0)
        def _():
            for r in (loss_ref, dlg_ref, dlb_ref, dg2_ref):
                r[...] = jnp.zeros_like(r)

        h = h_ref[...]
        g = _dot_nt(h, w_ref[0:D_FF])
        u = _dot_nt(h, w_ref[D_FF:2 * D_FF])
        s = _sigmoid(g)
        sg = g * s
        act = (sg * u).astype(BF16)
        dsu_ref[...] = (u * (s * (1.0 + g * (1.0 - s)))).astype(BF16)
        sg_ref[...] = sg.astype(BF16)
        act_ref[...] = act
        y2 = _dot(act, wd_ref[...])
        g2 = g2_ref[...]
        lg = lg_ref[...]
        xhat, rstd = _ln_stats(ALPHA * x1_ref[...] + g2 * y2)
        err = xhat * lg + lb_ref[...] - tg_ref[...]
        loss_ref[...] += _rowsum8(err * err)
        dx2 = err * (1.0 / d)
        dlg_ref[...] += _rowsum8(dx2 * xhat)
        dlb_ref[...] += _rowsum8(dx2)
        dz = _ln_bwd(dx2 * lg, xhat, rstd)
        dg2_ref[...] += _rowsum8(dz * y2)
        dz_ref[...] = dz
        dy_ref[...] = (g2 * dz).astype(BF16)

    row = pl.BlockSpec((tm, d), lambda i: (i, 0))
    wide = pl.BlockSpec((tm, D_FF), lambda i: (i, 0))
    vec = _full((1, d))
    acc = _full((8, d))
    acc_shape = jax.ShapeDtypeStruct((8, d), F32)
    wide_shape = jax.ShapeDtypeStruct((t, D_FF), BF16)
    return pl.pallas_call(
        body, name="ffn_fwd", grid=(t // tm,),
        out_shape=[wide_shape] * 3 + [jax.ShapeDtypeStruct((t, d), F32), jax.ShapeDtypeStruct((t, d), BF16)]
        + [acc_shape] * 4,
        in_specs=[row, _resident((2 * D_FF, d)), _resident((D_FF, d)), row, row, vec, vec, vec],
        out_specs=[wide] * 3 + [row, row, acc, acc, acc, acc], compiler_params=_params(("arbitrary",)),
    )(h2, w_gu_t, w_down, x1, target, g2, ln2g, ln2b)


def _resident(shape):
    nd = len(shape)
    return pl.BlockSpec(shape, lambda *_: (0,) * nd, pipeline_mode=pl.Buffered(1))


def _ffn_bwd(dy2, w_down, dsu, sg, w_gu_t, x1, x, y, dz2, sc2, g1, ln1g, tm, comm):
    t, d = x1.shape

    def body(dy2_ref, wd_ref, dsu_ref, sg_ref, w_ref, x1_ref, x_ref, y_ref, dz2_ref, sc_ref, g1_ref, lg_ref,
             dg_ref, du_ref, dz1_ref, dy_ref, dsc_ref, dsh_ref, dlg_ref, dlb_ref, dg1_ref):
        @pl.when(pl.program_id(0) == 0)
        def _():
            for r in (dsc_ref, dsh_ref, dlg_ref, dlb_ref, dg1_ref):
                r[...] = jnp.zeros_like(r)

        dact = _dot_nt(dy2_ref[...], wd_ref[...])
        dg = (dact * dsu_ref[...].astype(F32)).astype(BF16)
        du = (dact * sg_ref[...].astype(F32)).astype(BF16)
        dg_ref[...] = dg
        du_ref[...] = du
        dh2 = _dot(dg, w_ref[0:D_FF]) + _dot(du, w_ref[D_FF:2 * D_FF])
        x1 = x1_ref[...]
        y = y_ref[...]
        g1 = g1_ref[...]
        dsc_ref[...] += _rowsum8(dh2 * x1)
        dsh_ref[...] += _rowsum8(dh2)
        dx1 = dh2 * (1.0 + sc_ref[...]) + ALPHA * dz2_ref[...]
        xhat, rstd = _ln_stats(ALPHA * x_ref[...] + g1 * y)
        dlg_ref[...] += _rowsum8(dx1 * xhat)
        dlb_ref[...] += _rowsum8(dx1)
        dz1 = _ln_bwd(dx1 * lg_ref[...], xhat, rstd)
        dg1_ref[...] += _rowsum8(dz1 * y)
        dz1_ref[...] = dz1
        dy_ref[...] = (g1 * dz1).astype(BF16)

    row = pl.BlockSpec((tm, d), lambda i: (i, 0))
    wide = pl.BlockSpec((tm, D_FF), lambda i: (i, 0))
    vec = _full((1, d))
    acc = _full((8, d))
    acc_shape = jax.ShapeDtypeStruct((8, d), F32)
    wide_shape = jax.ShapeDtypeStruct((t, D_FF), BF16)
    return _call(
        body, name="ffn_bwd", grid=(t // tm,),
        out_shape=[wide_shape, wide_shape, jax.ShapeDtypeStruct((t, d), F32), jax.ShapeDtypeStruct((t, d), BF16)]
        + [acc_shape] * 5,
        in_specs=[row, _resident((D_FF, d)), wide, wide, _resident((2 * D_FF, d)), row, row, row, row, vec, vec, vec],
        out_specs=[wide, wide, row, row, acc, acc, acc, acc, acc], sem=("arbitrary",), comm=comm,
        args=(dy2, w_down, dsu, sg, w_gu_t, x1, x, y, dz2, sc2, g1, ln1g))


def _din(dproj, dkvn, w_in_t, x, dz1, sc1, tm, comm):
    t, d = x.shape

    def body(dp_ref, dkv_ref, w_ref, x_ref, dz1_ref, sc_ref, dx_ref, dpb_ref, dbin_ref, dsc_ref, dsh_ref):
        @pl.when(pl.program_id(0) == 0)
        def _():
            for r in (dbin_ref, dsc_ref, dsh_ref):
                r[...] = jnp.zeros_like(r)

        dp = jnp.concatenate([dp_ref[:, 0:ATTN_W], dp_ref[:, ATTN_W:ATTN_W + 2 * KV_W] + dkv_ref[...],
                              dp_ref[:, ATTN_W + 2 * KV_W:IN_W]], axis=1)
        dbin_ref[...] += _rowsum8(dp)
        dpb = dp.astype(BF16)
        dpb_ref[...] = dpb
        dh = _dot(dpb, w_ref[...])
        dsc_ref[...] += _rowsum8(dh * x_ref[...])
        dsh_ref[...] += _rowsum8(dh)
        dx_ref[...] = dh * (1.0 + sc_ref[...]) + ALPHA * dz1_ref[...]

    row = lambda w: pl.BlockSpec((tm, w), lambda i: (i, 0))
    return _call(
        body, name="din", grid=(t // tm,),
        out_shape=[jax.ShapeDtypeStruct((t, d), F32), jax.ShapeDtypeStruct((t, IN_W), BF16),
                   jax.ShapeDtypeStruct((8, IN_W), F32), jax.ShapeDtypeStruct((8, d), F32),
                   jax.ShapeDtypeStruct((8, d), F32)],
        in_specs=[row(IN_W), row(2 * KV_W), _full((IN_W, d)), row(d), row(d), _full((1, d))],
        out_specs=[row(d), row(IN_W), _full((8, IN_W)), _full((8, d)), _full((8, d))],
        sem=("arbitrary",), comm=comm, args=(dproj, dkvn, w_in_t, x, dz1, sc1))


def _wgrad(name, a, b, tmm, tk, comm=None, a2=None):
    t, m = a.shape
    n = b.shape[1]
    nk = t // tk
    nm = m // tmm

    def body(*refs):
        a_refs, (b_ref, o_ref, acc_ref) = refs[:-3], refs[-3:]
        i, k = pl.program_id(0), pl.program_id(1)

        @pl.when(k == 0)
        def _():
            acc_ref[...] = jnp.zeros_like(acc_ref)

        a_tile = a_refs[0][...] if a2 is None else jnp.where(i < nm, a_refs[0][...], a_refs[1][...])
        acc_ref[...] += _dot_tn(a_tile, b_ref[...])

        @pl.when(k == nk - 1)
        def _():
            o_ref[...] = acc_ref[...].astype(BF16)

    if a2 is None:
        a_specs, a_args, n_tiles = [pl.BlockSpec((tk, tmm), lambda i, k: (k, i))], (a,), nm
    else:
        a_specs = [pl.BlockSpec((tk, tmm), lambda i, k: (jnp.where(i < nm, k, 0), jnp.minimum(i, nm - 1))),
                   pl.BlockSpec((tk, tmm), lambda i, k: (jnp.where(i < nm, 0, k), jnp.maximum(i - nm, 0)))]
        a_args, n_tiles = (a, a2), 2 * nm
    (out,), got = _call(
        body, name=name, grid=(n_tiles, nk), out_shape=[jax.ShapeDtypeStruct((n_tiles * tmm, n), BF16)],
        in_specs=a_specs + [pl.BlockSpec((tk, n), lambda i, k: (k, 0))],
        out_specs=[pl.BlockSpec((tmm, n), lambda i, k: (i, 0))],
        scratch_shapes=[pltpu.VMEM((tmm, n), F32)], sem=("parallel", "arbitrary"), comm=comm, args=a_args + (b,))
    return out if comm is None else (out, got)


def _adamw(w, g, m, v):
    m = ADAM_B1 * m + (1.0 - ADAM_B1) * g
    v = ADAM_B2 * v + (1.0 - ADAM_B2) * (g * g)
    m_hat = m / (1.0 - ADAM_B1 ** ADAM_STEP)
    v_hat = v / (1.0 - ADAM_B2 ** ADAM_STEP)
    delta = -ADAM_LR * (m_hat / (jnp.sqrt(v_hat) + ADAM_EPS) + ADAM_WD * w)
    return delta, m, v


def _adam_reduce(name, parts, w, m, v, tr):
    r, cdim = w.shape
    n_slots = parts.shape[0]

    def body(p_ref, w_ref, m_ref, v_ref, g_ref, d_ref, mo_ref, vo_ref):
        g = p_ref[0].astype(F32)
        for s in range(1, n_slots):
            g = g + p_ref[s].astype(F32)
        d_ref[...], mo_ref[...], vo_ref[...] = _adamw(w_ref[...], g, m_ref[...], v_ref[...])
        g_ref[...] = g

    tile = pl.BlockSpec((tr, cdim), lambda i: (i, 0))
    shp = jax.ShapeDtypeStruct((r, cdim), F32)
    return pl.pallas_call(
        body, name=name, grid=(r // tr,), out_shape=[shp] * 4,
        in_specs=[pl.BlockSpec((n_slots, tr, cdim), lambda i: (0, i, 0)), tile, tile, tile],
        out_specs=[tile] * 4, compiler_params=_params(("parallel",)),
    )(parts, w, m, v)


def _adam_w_ada(c_all_t, dmod_cols, w, m, v):
    def body(ct_ref, dm_ref, w_ref, m_ref, v_ref, g_ref, d_ref, mo_ref, vo_ref):
        ct = ct_ref[...]
        s = (ct * _sigmoid(ct)).astype(BF16)
        g = _dot(s, dm_ref[...].astype(BF16))
        d_ref[...], mo_ref[...], vo_ref[...] = _adamw(w_ref[...], g, m_ref[...], v_ref[...])
        g_ref[...] = g

    shp = jax.ShapeDtypeStruct(w.shape, F32)
    return pl.pallas_call(
        body, name="adam_w_ada", grid=(1,), out_shape=[shp] * 4,
        in_specs=[_full(c_all_t.shape), _full(dmod_cols.shape)] + [_full(w.shape)] * 3,
        out_specs=[_full(w.shape)] * 4, compiler_params=_params(("arbitrary",)),
    )(c_all_t, dmod_cols, w, m, v)


SMALL_EARLY = ["rel_bias", "attn_sinks", "gmlp_ln_g", "gmlp_ln_b", "gmlp_b_s",
               "attn_out_g", "gmlp_out_g", "ln1_g", "ln1_b", "ln2_g", "ln2_b"]
SMALL_LATE = ["b_ada", "b_in", "loss"]
WEIGHTS = ["rel_bias", "w_ada", "b_ada", "w_in", "b_in", "attn_sinks", "gmlp_ln_g", "gmlp_ln_b", "gmlp_w_s",
           "gmlp_b_s", "attn_out_g", "gmlp_out_g", "w_out", "ln1_g", "ln1_b", "w_gate_up", "w_down", "ln2_g", "ln2_b"]


def _seg_rows(nelem):
    return -(-nelem // 1024) * 8


def _pack(named, names):
    parts = []
    for name in names:
        flat = named[name].reshape(-1).astype(F32)
        rows = _seg_rows(flat.shape[0])
        parts.append(jnp.pad(flat, (0, rows * 128 - flat.shape[0])).reshape(rows, 128))
    return jnp.concatenate(parts, axis=0)


def _adam_small(name, parts, names, wts, mom_m, mom_v):
    params = [n for n in names if n in wts]

    def view(n):
        nelem = math.prod(wts[n].shape)
        return (nelem // 128, 128) if nelem % 128 == 0 else (1, nelem)

    offsets, r0 = {}, 0
    for n in names:
        offsets[n] = r0
        r0 += _seg_rows(math.prod(wts[n].shape) if n in wts else 1)

    def body(*refs):
        p_ref, ins, outs = refs[0], refs[1:1 + 3 * len(params)], refs[1 + 3 * len(params):]

        def total(n, rows, lanes):
            o = offsets[n]
            g = p_ref[0, o:o + rows, 0:lanes]
            for s in range(1, N_DEV):
                g = g + p_ref[s, o:o + rows, 0:lanes]
            return g

        for i, n in enumerate(params):
            g = total(n, *view(n))
            w_ref, m_ref, v_ref = ins[3 * i:3 * i + 3]
            g_ref, d_ref, mo_ref, vo_ref = outs[4 * i:4 * i + 4]
            d_ref[...], mo_ref[...], vo_ref[...] = _adamw(w_ref[...], g, m_ref[...], v_ref[...])
            g_ref[...] = g
        for j, n in enumerate(n for n in names if n not in wts):
            outs[4 * len(params) + j][...] = total(n, 8, 128)

    args, in_specs, out_shape = [parts], [_full(parts.shape)], []
    for n in params:
        args += [t[n].reshape(view(n)) for t in (wts, mom_m, mom_v)]
        in_specs += [_full(view(n))] * 3
        out_shape += [jax.ShapeDtypeStruct(view(n), F32)] * 4
    out_shape += [jax.ShapeDtypeStruct((8, 128), F32) for n in names if n not in wts]
    res = pl.pallas_call(
        body, name=name, grid=(1,), out_shape=out_shape, in_specs=in_specs,
        out_specs=[_full(s.shape) for s in out_shape], compiler_params=_params(("arbitrary",)),
    )(*args)
    done = {n: tuple(r.reshape(wts[n].shape) for r in res[4 * i:4 * i + 4]) for i, n in enumerate(params)}
    sums = {n: res[4 * len(params) + j] for j, n in enumerate(n for n in names if n not in wts)}
    return done, sums


def _t5_bucket_map():
    qi = jnp.arange(BLK)[:, None]
    si = jnp.arange(2 * BLK)[None, :]
    n = jnp.maximum(qi + BLK - si, 0)
    max_exact = N_BUCKETS // 2
    nf = jnp.maximum(n, max_exact).astype(F32)
    large = max_exact + (jnp.log(nf / max_exact) / math.log(MAX_DISTANCE / max_exact)
                         * (N_BUCKETS - max_exact)).astype(jnp.int32)
    large = jnp.minimum(large, N_BUCKETS - 1)
    return jnp.where(n < max_exact, n, large).astype(jnp.int32)


def kernel(x, c, rel_bias, w_ada, b_ada, w_in, b_in, attn_sinks, gmlp_ln_g, gmlp_ln_b, gmlp_w_s, gmlp_b_s, attn_out_g, gmlp_out_g, w_out, ln1_g, ln1_b, w_gate_up, w_down, ln2_g, ln2_b, loss_target, m_rel_bias, m_w_ada, m_b_ada, m_w_in, m_b_in, m_attn_sinks, m_gmlp_ln_g, m_gmlp_ln_b, m_gmlp_w_s, m_gmlp_b_s, m_attn_out_g, m_gmlp_out_g, m_w_out, m_ln1_g, m_ln1_b, m_w_gate_up, m_w_down, m_ln2_g, m_ln2_b, v_rel_bias, v_w_ada, v_b_ada, v_w_in, v_b_in, v_attn_sinks, v_gmlp_ln_g, v_gmlp_ln_b, v_gmlp_w_s, v_gmlp_b_s, v_attn_out_g, v_gmlp_out_g, v_w_out, v_ln1_g, v_ln1_b, v_w_gate_up, v_w_down, v_ln2_g, v_ln2_b):
    wts = dict(rel_bias=rel_bias, w_ada=w_ada, b_ada=b_ada, w_in=w_in, b_in=b_in, attn_sinks=attn_sinks,
               gmlp_ln_g=gmlp_ln_g, gmlp_ln_b=gmlp_ln_b, gmlp_w_s=gmlp_w_s, gmlp_b_s=gmlp_b_s,
               attn_out_g=attn_out_g, gmlp_out_g=gmlp_out_g, w_out=w_out, ln1_g=ln1_g, ln1_b=ln1_b,
               w_gate_up=w_gate_up, w_down=w_down, ln2_g=ln2_g, ln2_b=ln2_b)
    mom_m = dict(rel_bias=m_rel_bias, w_ada=m_w_ada, b_ada=m_b_ada, w_in=m_w_in, b_in=m_b_in,
                 attn_sinks=m_attn_sinks, gmlp_ln_g=m_gmlp_ln_g, gmlp_ln_b=m_gmlp_ln_b, gmlp_w_s=m_gmlp_w_s,
                 gmlp_b_s=m_gmlp_b_s, attn_out_g=m_attn_out_g, gmlp_out_g=m_gmlp_out_g, w_out=m_w_out,
                 ln1_g=m_ln1_g, ln1_b=m_ln1_b, w_gate_up=m_w_gate_up, w_down=m_w_down, ln2_g=m_ln2_g,
                 ln2_b=m_ln2_b)
    mom_v = dict(rel_bias=v_rel_bias, w_ada=v_w_ada, b_ada=v_b_ada, w_in=v_w_in, b_in=v_b_in,
                 attn_sinks=v_attn_sinks, gmlp_ln_g=v_gmlp_ln_g, gmlp_ln_b=v_gmlp_ln_b, gmlp_w_s=v_gmlp_w_s,
                 gmlp_b_s=v_gmlp_b_s, attn_out_g=v_attn_out_g, gmlp_out_g=v_gmlp_out_g, w_out=v_w_out,
                 ln1_g=v_ln1_g, ln1_b=v_ln1_b, w_gate_up=v_w_gate_up, w_down=v_w_down, ln2_g=v_ln2_g,
                 ln2_b=v_ln2_b)

    t = x.shape[1]
    tm = min(512, t)
    tn_ff = D_FF // 2
    tk_long, tk_short = min(4096, t), min(2048, t)
    me = 4 * lax.axis_index("x") + 2 * lax.axis_index("y") + lax.axis_index("c")
    xs = x[0]
    target = loss_target[0]

    (c_g,) = _exchange("gather_c", [jnp.broadcast_to(c, (8, D_MODEL))], ("gather",))
    c_all = c_g[:, 0, :]

    ncol = w_ada.shape[2]
    b_cols = lax.dynamic_slice(b_ada, (0, me * ncol), (1, ncol))
    mod_part = _mod_partial(c_all, w_ada[0], b_cols)
    bucket = _t5_bucket_map()
    (bias,), (mod_g, w_in_g) = _bias_table(rel_bias, bucket,
                                           comm=([mod_part, w_in[0].T.astype(BF16)], ("gather", "gather2")))
    w_in_t = w_in_g.reshape(IN_W, D_MODEL)
    mod = lax.dynamic_slice(mod_g, (0, me, 0), (N_DEV, 1, ncol)).reshape(1, N_DEV * ncol)
    sh1, sc1, g1, sh2, sc2, g2 = [mod[:, i * D_MODEL:(i + 1) * D_MODEL] for i in range(6)]

    causal = jnp.tril(jnp.ones((BLK, BLK), dtype=bool))
    ws = jnp.where(causal[None], gmlp_w_s[0], 0.0).astype(BF16)
    pair = lambda w: jnp.concatenate([w[0::2], w[1::2]], axis=2)
    ws2, wst2 = pair(ws), pair(jnp.swapaxes(ws, 1, 2))
    bfull = jnp.repeat(gmlp_b_s[0].T, GMLP_W // N_GROUPS, axis=1)
    sinks = attn_sinks[0]

    (proj, h1), (w_down_g,) = _inproj(xs, sc1, sh1, w_in_t, b_in, tm, comm=([w_down[0].astype(BF16)], ("gather2",)))
    (mixed, *kept), (w_out_g, w_gu_g) = _mix_fwd(
        proj, bias, sinks, gmlp_ln_g, gmlp_ln_b, ws2, bfull, attn_out_g, gmlp_out_g,
        comm=([w_out[0].astype(BF16), w_gate_up[0].T.astype(BF16)], ("gather2", "gather2")))
    w_out_f = w_out_g.reshape(D_MODEL, D_MODEL)
    w_gu_t = w_gu_g.reshape(2 * D_FF, D_MODEL)
    (y1, x1, h2), _ = _outproj(mixed, w_out_f, xs, g1, ln1_g, ln1_b, sc2, sh2, tm, comm=None)
    w_down_f = w_down_g.reshape(D_FF, D_MODEL)
    dsu, sg, act, dz2, dy2, loss_p, d_ln2g, d_ln2b, d_g2 = _ffn_fwd(h2, w_gu_t, w_down_f, x1, target, g2, ln2_g, ln2_b,
                                                                    min(256, t))

    slots = lambda a: a.reshape(N_DEV, -1, D_MODEL)
    dw_down = _wgrad("wgrad_down", act, dy2, tn_ff, tk_short)
    (dgate, dup, dz1, dy1, d_sc2, d_sh2, d_ln1g, d_ln1b, d_g1), (r_down,) = _ffn_bwd(
        dy2, w_down_f, dsu, sg, w_gu_t, x1, xs, y1, dz2, sc2, g1, ln1_g, min(256, t),
        comm=([slots(dw_down)], ("scatter",)))
    dw_gu_t = _wgrad("wgrad_gate_up", dgate, h2, tn_ff, tk_short, a2=dup)
    dw_out = _wgrad("wgrad_out", mixed, dy1, D_MODEL, tk_long)
    ((dproj, dkvn, dl_acc, dsink_acc, d_lng, d_lnb, d_ws, d_bs, d_aog, d_gog), (r_gu, r_out)) = _mix_bwd(
        proj, gmlp_ln_g, gmlp_ln_b, ws2, wst2, bfull, attn_out_g, gmlp_out_g, dy1, w_out_f.T, kept,
        comm=([slots(dw_gu_t), slots(dw_out)], ("scatter", "scatter")))
    d_relb = _bias_grad(dl_acc, bucket)

    rsum = lambda a: jnp.sum(a, axis=0)
    early_g = dict(
        rel_bias=d_relb[:, 0, :N_BUCKETS].T, attn_sinks=rsum(dsink_acc)[:N_HEADS],
        gmlp_ln_g=rsum(d_lng), gmlp_ln_b=rsum(d_lnb),
        gmlp_b_s=jnp.sum(d_bs.reshape(BLK, N_GROUPS, GMLP_W // N_GROUPS), axis=2).T,
        attn_out_g=rsum(d_aog), gmlp_out_g=rsum(d_gog), ln1_g=rsum(d_ln1g), ln1_b=rsum(d_ln1b),
        ln2_g=rsum(d_ln2g), ln2_b=rsum(d_ln2b))
    (grad_x, dproj_b, d_bin, d_sc1, d_sh1), _ = _din(dproj, dkvn, w_in_t, xs, dz1, sc1, tm, comm=None)
    ws_rows = lambda a: a.reshape(N_GROUPS * BLK, BLK)
    d_ws_b = ws_rows(jnp.where(causal[None], d_ws, 0.0)).astype(BF16)
    dw_in_t, (early_all, ws_all) = _wgrad("wgrad_in", dproj_b, h1, IN_W // 2, tk_long,
                                          comm=([_pack(early_g, SMALL_EARLY), d_ws_b], ("gather2", "gather2")))
    dmod = jnp.concatenate([rsum(d_sh1), rsum(d_sc1), rsum(d_g1), rsum(d_sh2), rsum(d_sc2), rsum(d_g2)])
    late_g = dict(b_ada=dmod, b_in=rsum(d_bin), loss=(0.5 / D_MODEL * jnp.sum(loss_p)).reshape(1))
    late_all, r_in = _scatter_two_level("scatter_in", _pack(late_g, SMALL_LATE), slots(dw_in_t))

    small, _ = _adam_small("adam_small_early", early_all, SMALL_EARLY, wts, mom_m, mom_v)
    small_late, sums = _adam_small("adam_small_late", late_all, SMALL_LATE, wts, mom_m, mom_v)
    small.update(small_late)
    small["gmlp_w_s"] = [o.reshape(gmlp_w_s.shape) for o in _adam_reduce(
        "adam_w_s", ws_all, ws_rows(gmlp_w_s), ws_rows(m_gmlp_w_s), ws_rows(v_gmlp_w_s), N_GROUPS * BLK // 2)]
    loss = sums["loss"][0, 0]

    dmod_all = late_all[:, :_seg_rows(6 * D_MODEL), :].reshape(N_DEV, 6 * D_MODEL)
    dmod_cols = lax.dynamic_slice(dmod_all, (0, me * ncol), (N_DEV, ncol))
    kpad = 128 - N_DEV
    ada = _adam_w_ada(jnp.pad(c_all.T, ((0, 0), (0, kpad))), jnp.pad(dmod_cols, ((0, kpad), (0, 0))),
                      w_ada[0], m_w_ada[0], v_w_ada[0])

    tr = lambda a: jnp.swapaxes(a, -1, -2)
    big = {}
    big["w_in"] = [tr(o)[None] for o in _adam_reduce("adam_w_in", r_in, w_in[0].T, m_w_in[0].T, v_w_in[0].T, 112)]
    big["w_out"] = [o[None] for o in _adam_reduce("adam_w_out", r_out, w_out[0], m_w_out[0], v_w_out[0], 128)]
    big["w_gate_up"] = [tr(o)[None] for o in _adam_reduce("adam_w_gu", r_gu, w_gate_up[0].T, m_w_gate_up[0].T,
                                                           v_w_gate_up[0].T, 352)]
    big["w_down"] = [o[None] for o in _adam_reduce("adam_w_down", r_down, w_down[0], m_w_down[0], v_w_down[0], 176)]
    big["w_ada"] = [o[None] for o in ada]

    outs = [[], [], [], []]
    for name in WEIGHTS:
        for i in range(4):
            outs[i].append(big[name][i] if name in big else small[name][i])
    return (loss, grad_x[None], *outs[0], *outs[1], *outs[2], *outs[3])
```

```python
import math

import jax
import jax.numpy as jnp
from jax import lax
from jax.experimental import pallas as pl
from jax.experimental.pallas import tpu as pltpu

F32 = jnp.float32
BF16 = jnp.bfloat16
MESH = pl.DeviceIdType.MESH

N_DEV = 8
D_MODEL = 1024
HEAD_DIM = 64
N_HEADS = 8
N_GROUPS = 8
ATTN_W = 512
KV_W = 128
GMLP_W = 512
IN_W = 1792
BLK = 128
N_BUCKETS = 32
MAX_DISTANCE = 128
D_FF = 2816
ALPHA = 2.0 ** 0.25
LN_EPS = 1e-5
NEG_INF = -1e30
ADAM_LR = 0.001
ADAM_B1 = 0.9
ADAM_B2 = 0.999
ADAM_EPS = 1e-08
ADAM_WD = 0.01
ADAM_STEP = 10
GELU_C0 = math.sqrt(2.0 / math.pi)
GELU_C1 = 0.044715

VMEM_LIMIT = 56 * 1024 * 1024


def _params(sem):
    return pltpu.CompilerParams(dimension_semantics=sem, vmem_limit_bytes=VMEM_LIMIT)


def _dot(a, b):
    return lax.dot_general(a, b, (((1,), (0,)), ((), ())), preferred_element_type=F32)


def _dot_nt(a, b):
    return lax.dot_general(a, b, (((1,), (1,)), ((), ())), preferred_element_type=F32)


def _dot_tn(a, b):
    return lax.dot_general(a, b, (((0,), (0,)), ((), ())), preferred_element_type=F32)


def _full(shape):
    nd = len(shape)
    return pl.BlockSpec(shape, lambda *_: (0,) * nd)


def _rowsum8(v):
    r, c = v.shape
    return jnp.sum(v.reshape(r // 8, 8, c), axis=0)


def _sigmoid(v):
    return 1.0 / (1.0 + jnp.exp(-v))


def _gelu_parts(v):
    v2 = v * v
    t = jnp.tanh(GELU_C0 * (v + GELU_C1 * v * v2))
    g = 0.5 * v * (1.0 + t)
    dg = 0.5 * (1.0 + t) + 0.5 * v * (1.0 - t * t) * (GELU_C0 * (1.0 + 3.0 * GELU_C1 * v2))
    return g, dg


def _ln_stats(z):
    mu = jnp.mean(z, axis=1, keepdims=True)
    zc = z - mu
    var = jnp.mean(zc * zc, axis=1, keepdims=True)
    rstd = lax.rsqrt(var + LN_EPS)
    return zc * rstd, rstd


def _ln_bwd(dxhat, xhat, rstd):
    m1 = jnp.mean(dxhat, axis=1, keepdims=True)
    m2 = jnp.mean(dxhat * xhat, axis=1, keepdims=True)
    return rstd * (dxhat - m1 - xhat * m2)


def _seg_mean64(v):
    r = v.shape[0]
    lo = lax.broadcasted_iota(jnp.int32, (r, 128), 1) < 64
    outs = []
    for j in range(v.shape[1] // 128):
        ch = v[:, 128 * j:128 * (j + 1)]
        s_lo = jnp.sum(jnp.where(lo, ch, 0.0), axis=1, keepdims=True)
        s_hi = jnp.sum(jnp.where(lo, 0.0, ch), axis=1, keepdims=True)
        outs.append(jnp.where(lo, s_lo, s_hi) * (1.0 / 64.0))
    return jnp.concatenate(outs, axis=1)


def _rms(a, g):
    r = lax.rsqrt(jnp.mean(a * a, axis=1, keepdims=True) + LN_EPS)
    return a * r * g, r


def _rms_bwd(dout, a, r, g):
    t = dout * g
    return r * t - a * (r * r * r) * jnp.mean(t * a, axis=1, keepdims=True)


PEER_ORDER = (1, 2, 4, 3, 5, 6, 7)


def _peer(j):
    x, y, c = lax.axis_index("x"), lax.axis_index("y"), lax.axis_index("c")
    px = 1 - x if j & 4 else x
    py = 1 - y if j & 2 else y
    pc = 1 - c if j & 1 else c
    return (px, py, pc), 4 * px + 2 * py + pc


SIBLING = 1
CHIP_FLIPS = (4, 2, 6)


def _exchange_phase(phase, ins, outs, modes, send_sems, recv_sems, loc_sems):
    me = 4 * lax.axis_index("x") + 2 * lax.axis_index("y") + lax.axis_index("c")
    for k, mode in enumerate(modes):
        def copy(i, src, slot, dev, k=k):
            return pltpu.make_async_remote_copy(src_ref=src, dst_ref=outs[k].at[slot], send_sem=send_sems.at[k, i],
                                                recv_sem=recv_sems.at[k, i], device_id=dev, device_id_type=MESH)

        src_me = ins[k].at[me] if mode == "scatter" else ins[k]
        local = pltpu.make_async_copy(src_me, outs[k].at[me], loc_sems.at[k])
        if mode == "gather2":
            sib_dev, sib_idx = _peer(SIBLING)
            chips = [_peer(j) for j in CHIP_FLIPS]
            far = [_peer(j | SIBLING)[1] for j in CHIP_FLIPS]
            if phase == "start":
                local.start()
                copy(0, ins[k], me, sib_dev).start()
                for i, (dev, _) in enumerate(chips):
                    copy(1 + i, ins[k], me, dev).start()
            elif phase == "mid":
                for i, (dev, idx) in enumerate(chips):
                    copy(1 + i, ins[k], idx, dev).wait_recv()
                    copy(4 + i, outs[k].at[idx], idx, sib_dev).start()
            else:
                copy(0, ins[k], sib_idx, sib_dev).wait_recv()
                for i, slot in enumerate(far):
                    copy(4 + i, ins[k], slot, sib_dev).wait_recv()
                copy(0, ins[k], me, sib_dev).wait_send()
                for i, (dev, idx) in enumerate(chips):
                    copy(1 + i, ins[k], me, dev).wait_send()
                    copy(4 + i, outs[k].at[idx], idx, sib_dev).wait_send()
                local.wait()
            continue
        peers = [_peer(j) for j in PEER_ORDER]
        if phase == "start":
            local.start()
            for i, (dev, idx) in enumerate(peers):
                copy(i, ins[k].at[idx] if mode == "scatter" else ins[k], me, dev).start()
        elif phase == "end":
            for i, (dev, idx) in enumerate(peers):
                copy(i, src_me, idx, dev).wait_recv()
            for i, (dev, idx) in enumerate(peers):
                copy(i, src_me, me, dev).wait_send()
            local.wait()


def _exchange_shapes(arrays, modes):
    return [jax.ShapeDtypeStruct((N_DEV,) + (a.shape[1:] if m == "scatter" else a.shape), a.dtype)
            for a, m in zip(arrays, modes)]


def _exchange_sems(n):
    return [pltpu.SemaphoreType.DMA((n, N_DEV - 1)), pltpu.SemaphoreType.DMA((n, N_DEV - 1)),
            pltpu.SemaphoreType.DMA((n,))]


def _exchange(name, arrays, modes):
    n = len(arrays)

    def body(*refs):
        for phase in ("start", "mid", "end"):
            _exchange_phase(phase, refs[:n], refs[n:2 * n], modes, *refs[2 * n:])

    any_spec = pl.BlockSpec(memory_space=pl.ANY)
    return pl.pallas_call(
        body, name=name, out_shape=_exchange_shapes(arrays, modes),
        in_specs=[any_spec] * n, out_specs=[any_spec] * n, scratch_shapes=_exchange_sems(n),
    )(*arrays)


N_CHIP = 4


def _scatter_two_level(name, pack, parts):
    r, ncols = parts.shape[1:]

    def body(pack_ref, parts_ref, late_ref, got_ref, sib_ref, h_ref, g_send, g_recv, g_loc, d_send, d_recv, i_send, i_recv):
        x, y, c = lax.axis_index("x"), lax.axis_index("y"), lax.axis_index("c")
        my_chip = 2 * x + y
        sib_dev, _ = _peer(SIBLING)
        gather = ([pack_ref], [late_ref], ("gather",), g_send, g_recv, g_loc)
        _exchange_phase("start", *gather)

        def to_sibling(q):
            return pltpu.make_async_remote_copy(src_ref=parts_ref.at[2 * q + 1 - c], dst_ref=sib_ref.at[q],
                                                send_sem=d_send.at[q], recv_sem=d_recv.at[q],
                                                device_id=sib_dev, device_id_type=MESH)

        for q in range(N_CHIP):
            to_sibling(q).start()
        for q in range(N_CHIP):
            to_sibling(q).wait_recv()
            h_ref[q] = (parts_ref[2 * q + c].astype(F32) + sib_ref[q].astype(F32)).astype(BF16)

        def to_chip(i, slot):
            dev, idx = _peer(CHIP_FLIPS[i])
            return pltpu.make_async_remote_copy(src_ref=h_ref.at[idx // 2], dst_ref=got_ref.at[slot],
                                                send_sem=i_send.at[i], recv_sem=i_recv.at[i],
                                                device_id=dev, device_id_type=MESH)

        for i in range(len(CHIP_FLIPS)):
            to_chip(i, my_chip).start()
        got_ref[my_chip] = h_ref[my_chip]
        for i in range(len(CHIP_FLIPS)):
            to_chip(i, _peer(CHIP_FLIPS[i])[1] // 2).wait_recv()
        for i in range(len(CHIP_FLIPS)):
            to_chip(i, my_chip).wait_send()
        for q in range(N_CHIP):
            to_sibling(q).wait_send()
        _exchange_phase("end", *gather)

    any_spec = pl.BlockSpec(memory_space=pl.ANY)
    vmem = pl.BlockSpec(memory_space=pltpu.VMEM)
    dma = pltpu.SemaphoreType.DMA
    return pl.pallas_call(
        body, name=name,
        out_shape=[jax.ShapeDtypeStruct((N_DEV,) + pack.shape, pack.dtype),
                   jax.ShapeDtypeStruct((N_CHIP, r, ncols), parts.dtype)],
        in_specs=[any_spec, vmem], out_specs=[any_spec, vmem],
        scratch_shapes=[pltpu.VMEM((N_CHIP, r, ncols), parts.dtype), pltpu.VMEM((N_CHIP, r, ncols), parts.dtype),
                        dma((1, N_DEV - 1)), dma((1, N_DEV - 1)), dma((1,)),
                        dma((N_CHIP,)), dma((N_CHIP,)), dma((len(CHIP_FLIPS),)), dma((len(CHIP_FLIPS),))],
        compiler_params=pltpu.CompilerParams(vmem_limit_bytes=VMEM_LIMIT),
    )(pack, parts)


def _call(body, *, name, grid, in_specs, out_specs, out_shape, args, sem, scratch_shapes=(), comm=None):
    if comm is None:
        outs = pl.pallas_call(body, name=name, grid=grid, in_specs=list(in_specs), out_specs=list(out_specs),
                              out_shape=list(out_shape), scratch_shapes=list(scratch_shapes),
                              compiler_params=_params(sem))(*args)
        return list(outs), []
    arrays, modes = comm
    n_in, n_out, nc, ns = len(in_specs), len(out_specs), len(arrays), len(scratch_shapes)
    n_steps = math.prod(grid)

    def hosted(*refs):
        ins, cins = refs[:n_in], refs[n_in:n_in + nc]
        outs, couts = refs[n_in + nc:n_in + nc + n_out], refs[n_in + nc + n_out:n_in + 2 * nc + n_out]
        scratch = refs[n_in + 2 * nc + n_out:]
        ex = (cins, couts, modes) + tuple(scratch[ns:])
        step = pl.program_id(0)
        for ax in range(1, len(grid)):
            step = step * grid[ax] + pl.program_id(ax)

        @pl.when(step == 0)
        def _():
            _exchange_phase("start", *ex)

        body(*ins, *outs, *scratch[:ns])

        if "gather2" in modes:
            @pl.when(step == (3 * n_steps) // 4)
            def _():
                _exchange_phase("mid", *ex)

        @pl.when(step == n_steps - 1)
        def _():
            _exchange_phase("end", *ex)

    any_spec = pl.BlockSpec(memory_space=pl.ANY)
    res = pl.pallas_call(
        hosted, name=name, grid=grid, in_specs=list(in_specs) + [any_spec] * nc,
        out_specs=list(out_specs) + [any_spec] * nc, out_shape=list(out_shape) + _exchange_shapes(arrays, modes),
        scratch_shapes=list(scratch_shapes) + _exchange_sems(nc),
        compiler_params=_params(tuple("arbitrary" for _ in grid)))(*args, *arrays)
    return list(res[:n_out]), list(res[n_out:])


def _head(c8, w_ada, b_ada_cols, w_in_shard):
    ncol = w_ada.shape[1]

    def body(c_ref, w_ref, b_ref, win_ref, cg_ref, mg_ref, wg_ref, part_ref, *sems):
        ex_c = ([c_ref], [cg_ref], ("gather",)) + sems[0:3]
        ex_m = ([part_ref], [mg_ref], ("gather",)) + sems[3:6]
        ex_w = ([win_ref], [wg_ref], ("gather2",)) + sems[6:9]
        _exchange_phase("start", *ex_c)
        _exchange_phase("start", *ex_w)
        _exchange_phase("end", *ex_c)
        cv = cg_ref[:, 0, :]
        s = (cv * _sigmoid(cv)).astype(BF16)
        part_ref[...] = _dot(s, w_ref[...].astype(BF16)) + b_ref[...]
        _exchange_phase("start", *ex_m)
        _exchange_phase("mid", *ex_w)
        _exchange_phase("end", *ex_m)
        _exchange_phase("end", *ex_w)

    any_spec = pl.BlockSpec(memory_space=pl.ANY)
    vmem = pl.BlockSpec(memory_space=pltpu.VMEM)
    return pl.pallas_call(
        body, name="head",
        out_shape=[jax.ShapeDtypeStruct((N_DEV,) + c8.shape, F32), jax.ShapeDtypeStruct((N_DEV, N_DEV, ncol), F32),
                   jax.ShapeDtypeStruct((N_DEV,) + w_in_shard.shape, w_in_shard.dtype)],
        in_specs=[vmem, vmem, vmem, any_spec], out_specs=[vmem, vmem, any_spec],
        scratch_shapes=[pltpu.VMEM((N_DEV, ncol), F32)] + _exchange_sems(1) * 3,
        compiler_params=pltpu.CompilerParams(vmem_limit_bytes=VMEM_LIMIT),
    )(c8, w_ada, b_ada_cols, w_in_shard)


def _bias_table(rel_bias, bucket, comm):
    def body(rb_ref, bk_ref, o_ref):
        h = pl.program_id(0)
        bk = bk_ref[...]
        acc = jnp.zeros((BLK, 2 * BLK), F32)
        for b in range(N_BUCKETS):
            acc = jnp.where(bk == b, rb_ref[b, h], acc)
        dist = (lax.broadcasted_iota(jnp.int32, (BLK, 2 * BLK), 0) + BLK
                - lax.broadcasted_iota(jnp.int32, (BLK, 2 * BLK), 1))
        o_ref[0] = jnp.where((dist >= 0) & (dist < BLK), acc, NEG_INF)

    return _call(
        body, name="bias_table", out_shape=[jax.ShapeDtypeStruct((N_HEADS, BLK, 2 * BLK), F32)],
        in_specs=[pl.BlockSpec(memory_space=pltpu.SMEM), _full((BLK, 2 * BLK))],
        out_specs=[pl.BlockSpec((1, BLK, 2 * BLK), lambda h: (h, 0, 0))], grid=(N_HEADS,),
        sem=("arbitrary",), comm=comm, args=(rel_bias, bucket))


def _bias_grad(dl_acc, bucket):
    def body(dl_ref, bk_ref, o_ref):
        bk = bk_ref[...]
        dl = dl_ref[0]
        lane = lax.broadcasted_iota(jnp.int32, (1, 128), 1)
        row = jnp.zeros((1, 128), F32)
        for b in range(N_BUCKETS):
            s = jnp.sum(jnp.sum(jnp.where(bk == b, dl, 0.0), axis=1, keepdims=True), axis=0, keepdims=True)
            row = jnp.where(lane == b, s, row)
        o_ref[0] = row

    return pl.pallas_call(
        body, name="bias_grad", out_shape=jax.ShapeDtypeStruct((N_HEADS, 1, 128), F32),
        in_specs=[pl.BlockSpec((1, BLK, 2 * BLK), lambda h: (h, 0, 0)), _full((BLK, 2 * BLK))],
        out_specs=pl.BlockSpec((1, 1, 128), lambda h: (h, 0, 0)), grid=(N_HEADS,),
        compiler_params=_params(("arbitrary",)),
    )(dl_acc, bucket)


def _inproj(x, sc1, sh1, w_in_t, b_in, tm, comm):
    t, d = x.shape
    n = w_in_t.shape[0]

    def body(x_ref, sc_ref, sh_ref, w_ref, b_ref, proj_ref, h_ref):
        h = (x_ref[...] * (1.0 + sc_ref[...]) + sh_ref[...]).astype(BF16)
        h_ref[...] = h
        proj_ref[...] = _dot_nt(h, w_ref[...]) + b_ref[...]

    row = lambda w: pl.BlockSpec((tm, w), lambda i: (i, 0))
    return _call(
        body, name="inproj", grid=(t // tm,),
        out_shape=[jax.ShapeDtypeStruct((t, n), F32), jax.ShapeDtypeStruct((t, d), BF16)],
        in_specs=[row(d), _full((1, d)), _full((1, d)), _full((n, d)), _full((1, n))],
        out_specs=[row(n), row(d)], sem=("parallel",), comm=comm, args=(x, sc1, sh1, w_in_t, b_in))


HALF = 64
ROWS = 32


def _lane_lo(rows):
    return lax.broadcasted_iota(jnp.int32, (rows, 128), 1) < 64


def _mix_stage_kv(proj_ref, kvp_ref, s):
    lo = _lane_lo(2 * BLK)
    for name, col in (("k", ATTN_W), ("v", ATTN_W + KV_W)):
        cur = jnp.concatenate([kvp_ref[:, col - ATTN_W:col - ATTN_W + KV_W], proj_ref[:, col:col + KV_W]], axis=0)
        plain, swapped = cur.astype(BF16), pltpu.roll(cur, 64, 1).astype(BF16)
        zero = jnp.zeros_like(plain)
        for g in range(2):
            dup = jnp.where(lo, plain, swapped) if g == 0 else jnp.where(lo, swapped, plain)
            s[name + "d"][g] = dup
            s[name + "m"][g] = jnp.concatenate([jnp.where(lo, dup, zero), jnp.where(lo, zero, dup)], axis=0)


def _group_rows(ref, g):
    return ref[4 * g:4 * g + 4].reshape(4 * BLK, ref.shape[2])


def _pair_rows(ref, g):
    return jnp.concatenate([jnp.concatenate([ref[4 * g + 2 * c], ref[4 * g + 2 * c + 1]], axis=1) for c in range(2)],
                           axis=0)


def _mask_heads(src_ref, dst_ref):
    lo = _lane_lo(BLK)
    for j in range(4):
        chunk = src_ref[:, 128 * j:128 * (j + 1)]
        dst_ref[2 * j] = jnp.where(lo, chunk, 0.0).astype(BF16)
        dst_ref[2 * j + 1] = jnp.where(lo, 0.0, chunk).astype(BF16)


def _mix_stage_attn(proj_ref, bias_ref, sinks_ref, n, s):
    _mask_heads(proj_ref, s["qm"])
    for g in range(2):
        s["lg"][g] = _dot_nt(_group_rows(s["qm"], g), s["kd"][g])
    n0mask = (n == 0) & (lax.broadcasted_iota(jnp.int32, (HALF, 2 * BLK), 1) < BLK)
    lane = lax.broadcasted_iota(jnp.int32, (HALF, 128), 1)
    for hf in range(BLK // HALF):
        rows = slice(HALF * hf, HALF * (hf + 1))
        psink = jnp.zeros((HALF, 128), F32)
        for h in range(N_HEADS):
            sk = sinks_ref[h]
            grows = slice(BLK * (h % 4) + HALF * hf, BLK * (h % 4) + HALF * (hf + 1))
            logit = s["lg"][h // 4, grows, :] * (HEAD_DIM ** -0.5) + bias_ref[h, rows, :]
            logit = jnp.where(n0mask, NEG_INF, logit)
            m = jnp.maximum(jnp.max(logit, axis=1, keepdims=True), sk)
            e = jnp.exp(logit - m)
            es = jnp.exp(sk - m)
            inv = 1.0 / (jnp.sum(e, axis=1, keepdims=True) + es)
            p = e * inv
            s["p"][h, rows, :] = p
            s["pb"][h, rows, :] = p.astype(BF16)
            psink = jnp.where(lane == h, es * inv, psink)
        s["psink"][rows, :] = psink
    for g in range(2):
        out = _dot(_pair_rows(s["pb"], g), s["vm"][g])
        s["attn"][:, 256 * g:256 * g + 128] = out[0:BLK]
        s["attn"][:, 256 * g + 128:256 * g + 256] = out[BLK:2 * BLK]


def _mix_stage_gmlp_pre(proj_ref, lng, lnb, s, keep):
    c0 = ATTN_W + 2 * KV_W
    for r0 in range(0, BLK, ROWS):
        rows = slice(r0, r0 + ROWS)
        u, du = _gelu_parts(proj_ref[rows, c0:c0 + GMLP_W])
        a, da = _gelu_parts(proj_ref[rows, c0 + GMLP_W:c0 + 2 * GMLP_W])
        ac = a - _seg_mean64(a)
        rstd = lax.rsqrt(_seg_mean64(ac * ac) + LN_EPS)
        vhat = ac * rstd
        s["u"][rows, :] = u
        s["vnb"][rows, :] = (vhat * lng + lnb).astype(BF16)
        if keep:
            s["du"][rows, :] = du
            s["da"][rows, :] = da
            s["vhat"][rows, :] = vhat
            s["rstd"][rows, :] = rstd


def _stack_halves(chunk):
    lo = _lane_lo(BLK)
    zero = jnp.zeros_like(chunk)
    return jnp.concatenate([jnp.where(lo, chunk, zero), jnp.where(lo, zero, chunk)], axis=0)


def _mix_stage_gmlp_mix(ws2_ref, bfull_ref, s):
    for j in range(4):
        cols = slice(128 * j, 128 * (j + 1))
        s["ms"][:, cols] = _dot(ws2_ref[j], _stack_halves(s["vnb"][:, cols])) + bfull_ref[:, cols]


def _mix_scratch(keep):
    f32 = lambda *shape: pltpu.VMEM(shape, F32)
    b16 = lambda *shape: pltpu.VMEM(shape, BF16)
    names = dict(kd=b16(2, 2 * BLK, 128), vd=b16(2, 2 * BLK, 128), km=b16(2, 4 * BLK, 128), vm=b16(2, 4 * BLK, 128),
                 qm=b16(N_HEADS, BLK, 128), lg=f32(2, 4 * BLK, 2 * BLK), pb=b16(N_HEADS, BLK, 2 * BLK),
                 u=f32(BLK, GMLP_W), vnb=b16(BLK, GMLP_W), ms=f32(BLK, GMLP_W))
    if keep:
        names.update(dom=b16(N_HEADS, BLK, 128), dls=b16(N_HEADS, BLK, 2 * BLK),
                     dattn=f32(BLK, ATTN_W), dmix=f32(BLK, D_MODEL), du=f32(BLK, GMLP_W), da=f32(BLK, GMLP_W),
                     vhat=f32(BLK, GMLP_W), rstd=f32(BLK, GMLP_W), dmsb=b16(BLK, GMLP_W), dvn=f32(BLK, GMLP_W))
    return list(names), list(names.values())


def _mix_specs(with_logit_inputs):
    logit_inputs = [_full((N_HEADS, BLK, 2 * BLK)), pl.BlockSpec(memory_space=pltpu.SMEM)] if with_logit_inputs else []
    return [pl.BlockSpec((BLK, IN_W), lambda n: (n, 0)),
            pl.BlockSpec((BLK, 2 * KV_W), lambda n: (jnp.maximum(n - 1, 0), ATTN_W // (2 * KV_W)))] + logit_inputs + [
            _full((1, GMLP_W)), _full((1, GMLP_W)),
            _full((N_GROUPS // 2, BLK, 2 * BLK)), _full((BLK, GMLP_W)),
            _full((1, ATTN_W)), _full((1, GMLP_W))]


KEPT = [("p", (N_HEADS, BLK, 2 * BLK), F32), ("psink", (BLK, 128), F32), ("attn", (BLK, ATTN_W), F32)]


def _kept_shapes(t):
    full = lambda blk: (blk[0], t, blk[2]) if len(blk) == 3 else (t, blk[1])
    return [jax.ShapeDtypeStruct(full(blk), dt) for _, blk, dt in KEPT]


def _kept_specs():
    return [pl.BlockSpec(blk, (lambda n: (0, n, 0)) if len(blk) == 3 else (lambda n: (n, 0))) for _, blk, _ in KEPT]


def _mix_fwd(proj, bias, sinks, lng, lnb, ws2, bfull, aog, gog, comm):
    t = proj.shape[0]
    names, shapes = _mix_scratch(False)

    def body(proj_ref, kvp_ref, bias_ref, sinks_ref, lng_ref, lnb_ref, ws2_ref, bfull_ref, aog_ref, gog_ref,
             out_ref, *rest):
        s = dict(zip([name for name, _, _ in KEPT] + names, rest))
        n = pl.program_id(0)
        _mix_stage_kv(proj_ref, kvp_ref, s)
        _mix_stage_attn(proj_ref, bias_ref, sinks_ref, n, s)
        _mix_stage_gmlp_pre(proj_ref, lng_ref[...], lnb_ref[...], s, False)
        _mix_stage_gmlp_mix(ws2_ref, bfull_ref, s)
        for r0 in range(0, BLK, ROWS):
            rows = slice(r0, r0 + ROWS)
            out_ref[rows, 0:ATTN_W] = _rms(s["attn"][rows, :], aog_ref[...])[0].astype(BF16)
            out_ref[rows, ATTN_W:ATTN_W + GMLP_W] = _rms(s["u"][rows, :] * s["ms"][rows, :], gog_ref[...])[0].astype(BF16)

    return _call(
        body, name="mix_fwd", grid=(t // BLK,),
        out_shape=[jax.ShapeDtypeStruct((t, D_MODEL), BF16)] + _kept_shapes(t),
        in_specs=_mix_specs(True), out_specs=[pl.BlockSpec((BLK, D_MODEL), lambda n: (n, 0))] + _kept_specs(),
        scratch_shapes=shapes,
        sem=("parallel",), comm=comm, args=(proj, proj, bias, sinks, lng, lnb, ws2, bfull, aog, gog))


def _mix_bwd(proj, lng, lnb, ws2, wst2, bfull, aog, gog, dy, w_out, kept, comm):
    t = proj.shape[0]
    nb = t // BLK
    names, shapes = _mix_scratch(True)
    c_gu = ATTN_W + 2 * KV_W

    def body(proj_ref, kvp_ref, lng_ref, lnb_ref, ws2_ref, bfull_ref, aog_ref, gog_ref,
             wst2_ref, dy_ref, wout_ref, *rest):
        n_kept = len(KEPT)
        s = dict(zip([name for name, _, _ in KEPT], rest[:n_kept]))
        (dproj_ref, dkvn_ref, dl_ref, dsink_ref, dlng_ref, dlnb_ref, dws_ref, dbs_ref, daog_ref,
         dgog_ref) = rest[n_kept:n_kept + 10]
        s.update(zip(names, rest[n_kept + 10:]))
        n = pl.program_id(0)

        @pl.when(n == 0)
        def _():
            for r in (dl_ref, dsink_ref, dlng_ref, dlnb_ref, dws_ref, dbs_ref, daog_ref, dgog_ref):
                r[...] = jnp.zeros_like(r)

        s["dmix"][...] = _dot(dy_ref[...], wout_ref[...])
        _mix_stage_kv(proj_ref, kvp_ref, s)
        _mask_heads(proj_ref, s["qm"])
        lng = lng_ref[...]
        _mix_stage_gmlp_pre(proj_ref, lng, lnb_ref[...], s, True)
        _mix_stage_gmlp_mix(ws2_ref, bfull_ref, s)

        aog, gog = aog_ref[...], gog_ref[...]
        for r0 in range(0, BLK, ROWS):
            rows = slice(r0, r0 + ROWS)
            attn, dma = s["attn"][rows, :], s["dmix"][rows, 0:ATTN_W]
            _, r_a = _rms(attn, aog)
            daog_ref[...] += _rowsum8(dma * attn * r_a)
            s["dattn"][rows, :] = _rms_bwd(dma, attn, r_a, aog)
            u, ms, dmg = s["u"][rows, :], s["ms"][rows, :], s["dmix"][rows, ATTN_W:ATTN_W + GMLP_W]
            gm = u * ms
            _, r_g = _rms(gm, gog)
            dgog_ref[...] += _rowsum8(dmg * gm * r_g)
            dgm = _rms_bwd(dmg, gm, r_g, gog)
            dproj_ref[rows, c_gu:c_gu + GMLP_W] = dgm * ms * s["du"][rows, :]
            dms = dgm * u
            dbs_ref[rows, :] += dms
            s["dmsb"][rows, :] = dms.astype(BF16)

        _mask_heads(s["dattn"], s["dom"])
        for g in range(2):
            s["lg"][g] = _dot_nt(_group_rows(s["dom"], g), s["vd"][g])
        lane = lax.broadcasted_iota(jnp.int32, (HALF, 128), 1)
        for hf in range(BLK // HALF):
            rows = slice(HALF * hf, HALF * (hf + 1))
            dsink = jnp.zeros((HALF, 128), F32)
            for h in range(N_HEADS):
                grows = slice(BLK * (h % 4) + HALF * hf, BLK * (h % 4) + HALF * (hf + 1))
                dp = s["lg"][h // 4, grows, :]
                p = s["p"][h, rows, :]
                s["pb"][h, rows, :] = p.astype(BF16)
                rs = jnp.sum(p * dp, axis=1, keepdims=True)
                dl = p * (dp - rs)
                dl_ref[h, rows, :] += dl
                dsink = dsink + jnp.where(lane == h, -s["psink"][rows, :] * rs, 0.0)
                s["dls"][h, rows, :] = (dl * (HEAD_DIM ** -0.5)).astype(BF16)
            dsink_ref[rows, :] += dsink
        for g in range(2):
            dq = _dot(_pair_rows(s["dls"], g), s["km"][g])
            dproj_ref[:, 256 * g:256 * g + 128] = dq[0:BLK]
            dproj_ref[:, 256 * g + 128:256 * g + 256] = dq[BLK:2 * BLK]
        lo_k = _lane_lo(2 * BLK)
        for col, lhs, rhs in ((0, "dls", "qm"), (KV_W, "pb", "dom")):
            raw = [_dot_tn(_group_rows(s[lhs], g), _group_rows(s[rhs], g)) for g in range(2)]
            both = [r + pltpu.roll(r, 64, 1) for r in raw]
            dkv = jnp.where(lo_k, both[0], both[1])
            dproj_ref[:, ATTN_W + col:ATTN_W + col + KV_W] = dkv[BLK:2 * BLK]
            dkvn_ref[:, col:col + KV_W] = dkv[0:BLK]

        for j in range(4):
            cols = slice(128 * j, 128 * (j + 1))
            dm2 = _stack_halves(s["dmsb"][:, cols])
            vnb = s["vnb"][:, cols]
            dws2 = _dot_nt(dm2, vnb)
            dws_ref[2 * j] += dws2[0:BLK]
            dws_ref[2 * j + 1] += dws2[BLK:2 * BLK]
            s["dvn"][:, cols] = _dot(wst2_ref[j], dm2)
        for r0 in range(0, BLK, ROWS):
            rows = slice(r0, r0 + ROWS)
            dvn, vhat = s["dvn"][rows, :], s["vhat"][rows, :]
            dlng_ref[...] += _rowsum8(dvn * vhat)
            dlnb_ref[...] += _rowsum8(dvn)
            dvh = dvn * lng
            dact = s["rstd"][rows, :] * (dvh - _seg_mean64(dvh) - vhat * _seg_mean64(dvh * vhat))
            dproj_ref[rows, c_gu + GMLP_W:IN_W] = dact * s["da"][rows, :]

    acc8 = lambda w: jax.ShapeDtypeStruct((8, w), F32)
    out_shape = [jax.ShapeDtypeStruct((t, IN_W), F32), jax.ShapeDtypeStruct((t, 2 * KV_W), F32),
                 jax.ShapeDtypeStruct((N_HEADS, BLK, 2 * BLK), F32), jax.ShapeDtypeStruct((BLK, 128), F32),
                 acc8(GMLP_W), acc8(GMLP_W), jax.ShapeDtypeStruct((N_GROUPS, BLK, BLK), F32),
                 jax.ShapeDtypeStruct((BLK, GMLP_W), F32), acc8(ATTN_W), acc8(GMLP_W)]
    out_specs = [pl.BlockSpec((BLK, IN_W), lambda n: (n, 0)),
                 pl.BlockSpec((BLK, 2 * KV_W), lambda n: ((n + nb - 1) % nb, 0)),
                 _full((N_HEADS, BLK, 2 * BLK)), _full((BLK, 128)), _full((8, GMLP_W)), _full((8, GMLP_W)),
                 _full((N_GROUPS, BLK, BLK)), _full((BLK, GMLP_W)), _full((8, ATTN_W)), _full((8, GMLP_W))]
    in_specs = _mix_specs(False) + [_full((N_GROUPS // 2, BLK, 2 * BLK)),
                               pl.BlockSpec((BLK, D_MODEL), lambda n: (n, 0)),
                               _full((D_MODEL, D_MODEL))] + _kept_specs()
    return _call(
        body, name="mix_bwd", grid=(nb,), out_shape=out_shape, in_specs=in_specs, out_specs=out_specs,
        scratch_shapes=shapes, sem=("arbitrary",), comm=comm,
        args=(proj, proj, lng, lnb, ws2, bfull, aog, gog, wst2, dy, w_out, *kept))


def _outproj(mixed, w_out, x, g1, ln1g, ln1b, sc2, sh2, tm, comm):
    t, d = x.shape

    def body(mx_ref, w_ref, x_ref, g1_ref, lg_ref, lb_ref, sc_ref, sh_ref, y_ref, x1_ref, h2_ref):
        y = _dot(mx_ref[...], w_ref[...])
        xhat, _ = _ln_stats(ALPHA * x_ref[...] + g1_ref[...] * y)
        x1 = xhat * lg_ref[...] + lb_ref[...]
        y_ref[...] = y
        x1_ref[...] = x1
        h2_ref[...] = (x1 * (1.0 + sc_ref[...]) + sh_ref[...]).astype(BF16)

    row = pl.BlockSpec((tm, d), lambda i: (i, 0))
    vec = _full((1, d))
    return _call(
        body, name="outproj", grid=(t // tm,),
        out_shape=[jax.ShapeDtypeStruct((t, d), F32), jax.ShapeDtypeStruct((t, d), F32),
                   jax.ShapeDtypeStruct((t, d), BF16)],
        in_specs=[row, _full((d, d)), row, vec, vec, vec, vec, vec], out_specs=[row, row, row],
        sem=("parallel",), comm=comm, args=(mixed, w_out, x, g1, ln1g, ln1b, sc2, sh2))


def _ffn_fwd(h2, w_gu_t, w_down, x1, target, g2, ln2g, ln2b, tm):
    t, d = x1.shape

    def body(h_ref, w_ref, wd_ref, x1_ref, tg_ref, g2_ref, lg_ref, lb_ref,
             dsu_ref, sg_ref, act_ref, dz_ref, dy_ref, loss_ref, dlg_ref, dlb_ref, dg2_ref):
        @pl.when(pl.program_id(0) == 0)
        def _():
            for r in (loss_ref, dlg_ref, dlb_ref, dg2_ref):
                r[...] = jnp.zeros_like(r)

        h = h_ref[...]
        g = _dot_nt(h, w_ref[0:D_FF])
        u = _dot_nt(h, w_ref[D_FF:2 * D_FF])
        s = _sigmoid(g)
        sg = g * s
        act = (sg * u).astype(BF16)
        dsu_ref[...] = (u * (s * (1.0 + g * (1.0 - s)))).astype(BF16)
        sg_ref[...] = sg.astype(BF16)
        act_ref[...] = act
        y2 = _dot(act, wd_ref[...])
        g2 = g2_ref[...]
        lg = lg_ref[...]
        xhat, rstd = _ln_stats(ALPHA * x1_ref[...] + g2 * y2)
        err = xhat * lg + lb_ref[...] - tg_ref[...]
        loss_ref[...] += _rowsum8(err * err)
        dx2 = err * (1.0 / d)
        dlg_ref[...] += _rowsum8(dx2 * xhat)
        dlb_ref[...] += _rowsum8(dx2)
        dz = _ln_bwd(dx2 * lg, xhat, rstd)
        dg2_ref[...] += _rowsum8(dz * y2)
        dz_ref[...] = dz
        dy_ref[...] = (g2 * dz).astype(BF16)

    row = pl.BlockSpec((tm, d), lambda i: (i, 0))
    wide = pl.BlockSpec((tm, D_FF), lambda i: (i, 0))
    vec = _full((1, d))
    acc = _full((8, d))
    acc_shape = jax.ShapeDtypeStruct((8, d), F32)
    wide_shape = jax.ShapeDtypeStruct((t, D_FF), BF16)
    return pl.pallas_call(
        body, name="ffn_fwd", grid=(t // tm,),
        out_shape=[wide_shape] * 3 + [jax.ShapeDtypeStruct((t, d), F32), jax.ShapeDtypeStruct((t, d), BF16)]
        + [acc_shape] * 4,
        in_specs=[row, _resident((2 * D_FF, d)), _resident((D_FF, d)), row, row, vec, vec, vec],
        out_specs=[wide] * 3 + [row, row, acc, acc, acc, acc], compiler_params=_params(("arbitrary",)),
    )(h2, w_gu_t, w_down, x1, target, g2, ln2g, ln2b)


def _resident(shape):
    nd = len(shape)
    return pl.BlockSpec(shape, lambda *_: (0,) * nd, pipeline_mode=pl.Buffered(1))


def _ffn_bwd(dy2, w_down, dsu, sg, w_gu_t, x1, x, y, dz2, sc2, g1, ln1g, tm, comm):
    t, d = x1.shape

    def body(dy2_ref, wd_ref, dsu_ref, sg_ref, w_ref, x1_ref, x_ref, y_ref, dz2_ref, sc_ref, g1_ref, lg_ref,
             dg_ref, du_ref, dz1_ref, dy_ref, dsc_ref, dsh_ref, dlg_ref, dlb_ref, dg1_ref):
        @pl.when(pl.program_id(0) == 0)
        def _():
            for r in (dsc_ref, dsh_ref, dlg_ref, dlb_ref, dg1_ref):
                r[...] = jnp.zeros_like(r)

        dact = _dot_nt(dy2_ref[...], wd_ref[...])
        dg = (dact * dsu_ref[...].astype(F32)).astype(BF16)
        du = (dact * sg_ref[...].astype(F32)).astype(BF16)
        dg_ref[...] = dg
        du_ref[...] = du
        dh2 = _dot(dg, w_ref[0:D_FF]) + _dot(du, w_ref[D_FF:2 * D_FF])
        x1 = x1_ref[...]
        y = y_ref[...]
        g1 = g1_ref[...]
        dsc_ref[...] += _rowsum8(dh2 * x1)
        dsh_ref[...] += _rowsum8(dh2)
        dx1 = dh2 * (1.0 + sc_ref[...]) + ALPHA * dz2_ref[...]
        xhat, rstd = _ln_stats(ALPHA * x_ref[...] + g1 * y)
        dlg_ref[...] += _rowsum8(dx1 * xhat)
        dlb_ref[...] += _rowsum8(dx1)
        dz1 = _ln_bwd(dx1 * lg_ref[...], xhat, rstd)
        dg1_ref[...] += _rowsum8(dz1 * y)
        dz1_ref[...] = dz1
        dy_ref[...] = (g1 * dz1).astype(BF16)

    row = pl.BlockSpec((tm, d), lambda i: (i, 0))
    wide = pl.BlockSpec((tm, D_FF), lambda i: (i, 0))
    vec = _full((1, d))
    acc = _full((8, d))
    acc_shape = jax.ShapeDtypeStruct((8, d), F32)
    wide_shape = jax.ShapeDtypeStruct((t, D_FF), BF16)
    return _call(
        body, name="ffn_bwd", grid=(t // tm,),
        out_shape=[wide_shape, wide_shape, jax.ShapeDtypeStruct((t, d), F32), jax.ShapeDtypeStruct((t, d), BF16)]
        + [acc_shape] * 5,
        in_specs=[row, _resident((D_FF, d)), wide, wide, _resident((2 * D_FF, d)), row, row, row, row, vec, vec, vec],
        out_specs=[wide, wide, row, row, acc, acc, acc, acc, acc], sem=("arbitrary",), comm=comm,
        args=(dy2, w_down, dsu, sg, w_gu_t, x1, x, y, dz2, sc2, g1, ln1g))


def _din(dproj, dkvn, w_in_t, x, dz1, sc1, tm, comm):
    t, d = x.shape

    def body(dp_ref, dkv_ref, w_ref, x_ref, dz1_ref, sc_ref, dx_ref, dpb_ref, dbin_ref, dsc_ref, dsh_ref):
        @pl.when(pl.program_id(0) == 0)
        def _():
            for r in (dbin_ref, dsc_ref, dsh_ref):
                r[...] = jnp.zeros_like(r)

        dp = jnp.concatenate([dp_ref[:, 0:ATTN_W], dp_ref[:, ATTN_W:ATTN_W + 2 * KV_W] + dkv_ref[...],
                              dp_ref[:, ATTN_W + 2 * KV_W:IN_W]], axis=1)
        dbin_ref[...] += _rowsum8(dp)
        dpb = dp.astype(BF16)
        dpb_ref[...] = dpb
        dh = _dot(dpb, w_ref[...])
        dsc_ref[...] += _rowsum8(dh * x_ref[...])
        dsh_ref[...] += _rowsum8(dh)
        dx_ref[...] = dh * (1.0 + sc_ref[...]) + ALPHA * dz1_ref[...]

    row = lambda w: pl.BlockSpec((tm, w), lambda i: (i, 0))
    return _call(
        body, name="din", grid=(t // tm,),
        out_shape=[jax.ShapeDtypeStruct((t, d), F32), jax.ShapeDtypeStruct((t, IN_W), BF16),
                   jax.ShapeDtypeStruct((8, IN_W), F32), jax.ShapeDtypeStruct((8, d), F32),
                   jax.ShapeDtypeStruct((8, d), F32)],
        in_specs=[row(IN_W), row(2 * KV_W), _full((IN_W, d)), row(d), row(d), _full((1, d))],
        out_specs=[row(d), row(IN_W), _full((8, IN_W)), _full((8, d)), _full((8, d))],
        sem=("arbitrary",), comm=comm, args=(dproj, dkvn, w_in_t, x, dz1, sc1))


def _wgrad(name, a, b, tmm, tk, comm=None, a2=None):
    t, m = a.shape
    n = b.shape[1]
    nk = t // tk
    nm = m // tmm

    def body(*refs):
        a_refs, (b_ref, o_ref, acc_ref) = refs[:-3], refs[-3:]
        i, k = pl.program_id(0), pl.program_id(1)

        @pl.when(k == 0)
        def _():
            acc_ref[...] = jnp.zeros_like(acc_ref)

        a_tile = a_refs[0][...] if a2 is None else jnp.where(i < nm, a_refs[0][...], a_refs[1][...])
        acc_ref[...] += _dot_tn(a_tile, b_ref[...])

        @pl.when(k == nk - 1)
        def _():
            o_ref[...] = acc_ref[...].astype(BF16)

    if a2 is None:
        a_specs, a_args, n_tiles = [pl.BlockSpec((tk, tmm), lambda i, k: (k, i))], (a,), nm
    else:
        a_specs = [pl.BlockSpec((tk, tmm), lambda i, k: (jnp.where(i < nm, k, 0), jnp.minimum(i, nm - 1))),
                   pl.BlockSpec((tk, tmm), lambda i, k: (jnp.where(i < nm, 0, k), jnp.maximum(i - nm, 0)))]
        a_args, n_tiles = (a, a2), 2 * nm
    (out,), got = _call(
        body, name=name, grid=(n_tiles, nk), out_shape=[jax.ShapeDtypeStruct((n_tiles * tmm, n), BF16)],
        in_specs=a_specs + [pl.BlockSpec((tk, n), lambda i, k: (k, 0))],
        out_specs=[pl.BlockSpec((tmm, n), lambda i, k: (i, 0))],
        scratch_shapes=[pltpu.VMEM((tmm, n), F32)], sem=("parallel", "arbitrary"), comm=comm, args=a_args + (b,))
    return out if comm is None else (out, got)


def _adamw(w, g, m, v):
    m = ADAM_B1 * m + (1.0 - ADAM_B1) * g
    v = ADAM_B2 * v + (1.0 - ADAM_B2) * (g * g)
    m_hat = m / (1.0 - ADAM_B1 ** ADAM_STEP)
    v_hat = v / (1.0 - ADAM_B2 ** ADAM_STEP)
    delta = -ADAM_LR * (m_hat / (jnp.sqrt(v_hat) + ADAM_EPS) + ADAM_WD * w)
    return delta, m, v


def _adam_reduce(name, parts, w, m, v, tr):
    r, cdim = w.shape
    n_slots = parts.shape[0]

    def body(p_ref, w_ref, m_ref, v_ref, g_ref, d_ref, mo_ref, vo_ref):
        g = p_ref[0].astype(F32)
        for s in range(1, n_slots):
            g = g + p_ref[s].astype(F32)
        d_ref[...], mo_ref[...], vo_ref[...] = _adamw(w_ref[...], g, m_ref[...], v_ref[...])
        g_ref[...] = g

    tile = pl.BlockSpec((tr, cdim), lambda i: (i, 0))
    shp = jax.ShapeDtypeStruct((r, cdim), F32)
    return pl.pallas_call(
        body, name=name, grid=(r // tr,), out_shape=[shp] * 4,
        in_specs=[pl.BlockSpec((n_slots, tr, cdim), lambda i: (0, i, 0)), tile, tile, tile],
        out_specs=[tile] * 4, compiler_params=_params(("parallel",)),
    )(parts, w, m, v)


def _adam_w_ada(c_all_t, dmod_cols, w, m, v):
    def body(ct_ref, dm_ref, w_ref, m_ref, v_ref, g_ref, d_ref, mo_ref, vo_ref):
        ct = ct_ref[...]
        s = (ct * _sigmoid(ct)).astype(BF16)
        g = _dot(s, dm_ref[...].astype(BF16))
        d_ref[...], mo_ref[...], vo_ref[...] = _adamw(w_ref[...], g, m_ref[...], v_ref[...])
        g_ref[...] = g

    shp = jax.ShapeDtypeStruct(w.shape, F32)
    return pl.pallas_call(
        body, name="adam_w_ada", grid=(1,), out_shape=[shp] * 4,
        in_specs=[_full(c_all_t.shape), _full(dmod_cols.shape)] + [_full(w.shape)] * 3,
        out_specs=[_full(w.shape)] * 4, compiler_params=_params(("arbitrary",)),
    )(c_all_t, dmod_cols, w, m, v)


SMALL_EARLY = ["rel_bias", "attn_sinks", "gmlp_ln_g", "gmlp_ln_b", "gmlp_b_s",
               "attn_out_g", "gmlp_out_g", "ln1_g", "ln1_b", "ln2_g", "ln2_b"]
SMALL_LATE = ["b_ada", "b_in", "loss"]
WEIGHTS = ["rel_bias", "w_ada", "b_ada", "w_in", "b_in", "attn_sinks", "gmlp_ln_g", "gmlp_ln_b", "gmlp_w_s",
           "gmlp_b_s", "attn_out_g", "gmlp_out_g", "w_out", "ln1_g", "ln1_b", "w_gate_up", "w_down", "ln2_g", "ln2_b"]


def _seg_rows(nelem):
    return -(-nelem // 1024) * 8


def _pack(named, names):
    parts = []
    for name in names:
        flat = named[name].reshape(-1).astype(F32)
        rows = _seg_rows(flat.shape[0])
        parts.append(jnp.pad(flat, (0, rows * 128 - flat.shape[0])).reshape(rows, 128))
    return jnp.concatenate(parts, axis=0)


def _adam_small(name, parts, names, wts, mom_m, mom_v):
    params = [n for n in names if n in wts]

    def view(n):
        nelem = math.prod(wts[n].shape)
        return (nelem // 128, 128) if nelem % 128 == 0 else (1, nelem)

    offsets, r0 = {}, 0
    for n in names:
        offsets[n] = r0
        r0 += _seg_rows(math.prod(wts[n].shape) if n in wts else 1)

    def body(*refs):
        p_ref, ins, outs = refs[0], refs[1:1 + 3 * len(params)], refs[1 + 3 * len(params):]

        def total(n, rows, lanes):
            o = offsets[n]
            g = p_ref[0, o:o + rows, 0:lanes]
            for s in range(1, N_DEV):
                g = g + p_ref[s, o:o + rows, 0:lanes]
            return g

        for i, n in enumerate(params):
            g = total(n, *view(n))
            w_ref, m_ref, v_ref = ins[3 * i:3 * i + 3]
            g_ref, d_ref, mo_ref, vo_ref = outs[4 * i:4 * i + 4]
            d_ref[...], mo_ref[...], vo_ref[...] = _adamw(w_ref[...], g, m_ref[...], v_ref[...])
            g_ref[...] = g
        for j, n in enumerate(n for n in names if n not in wts):
            outs[4 * len(params) + j][...] = total(n, 8, 128)

    args, in_specs, out_shape = [parts], [_full(parts.shape)], []
    for n in params:
        args += [t[n].reshape(view(n)) for t in (wts, mom_m, mom_v)]
        in_specs += [_full(view(n))] * 3
        out_shape += [jax.ShapeDtypeStruct(view(n), F32)] * 4
    out_shape += [jax.ShapeDtypeStruct((8, 128), F32) for n in names if n not in wts]
    res = pl.pallas_call(
        body, name=name, grid=(1,), out_shape=out_shape, in_specs=in_specs,
        out_specs=[_full(s.shape) for s in out_shape], compiler_params=_params(("arbitrary",)),
    )(*args)
    done = {n: tuple(r.reshape(wts[n].shape) for r in res[4 * i:4 * i + 4]) for i, n in enumerate(params)}
    sums = {n: res[4 * len(params) + j] for j, n in enumerate(n for n in names if n not in wts)}
    return done, sums


def _t5_bucket_map():
    qi = jnp.arange(BLK)[:, None]
    si = jnp.arange(2 * BLK)[None, :]
    n = jnp.maximum(qi + BLK - si, 0)
    max_exact = N_BUCKETS // 2
    nf = jnp.maximum(n, max_exact).astype(F32)
    large = max_exact + (jnp.log(nf / max_exact) / math.log(MAX_DISTANCE / max_exact)
                         * (N_BUCKETS - max_exact)).astype(jnp.int32)
    large = jnp.minimum(large, N_BUCKETS - 1)
    return jnp.where(n < max_exact, n, large).astype(jnp.int32)


def kernel(x, c, rel_bias, w_ada, b_ada, w_in, b_in, attn_sinks, gmlp_ln_g, gmlp_ln_b, gmlp_w_s, gmlp_b_s, attn_out_g, gmlp_out_g, w_out, ln1_g, ln1_b, w_gate_up, w_down, ln2_g, ln2_b, loss_target, m_rel_bias, m_w_ada, m_b_ada, m_w_in, m_b_in, m_attn_sinks, m_gmlp_ln_g, m_gmlp_ln_b, m_gmlp_w_s, m_gmlp_b_s, m_attn_out_g, m_gmlp_out_g, m_w_out, m_ln1_g, m_ln1_b, m_w_gate_up, m_w_down, m_ln2_g, m_ln2_b, v_rel_bias, v_w_ada, v_b_ada, v_w_in, v_b_in, v_attn_sinks, v_gmlp_ln_g, v_gmlp_ln_b, v_gmlp_w_s, v_gmlp_b_s, v_attn_out_g, v_gmlp_out_g, v_w_out, v_ln1_g, v_ln1_b, v_w_gate_up, v_w_down, v_ln2_g, v_ln2_b):
    wts = dict(rel_bias=rel_bias, w_ada=w_ada, b_ada=b_ada, w_in=w_in, b_in=b_in, attn_sinks=attn_sinks,
               gmlp_ln_g=gmlp_ln_g, gmlp_ln_b=gmlp_ln_b, gmlp_w_s=gmlp_w_s, gmlp_b_s=gmlp_b_s,
               attn_out_g=attn_out_g, gmlp_out_g=gmlp_out_g, w_out=w_out, ln1_g=ln1_g, ln1_b=ln1_b,
               w_gate_up=w_gate_up, w_down=w_down, ln2_g=ln2_g, ln2_b=ln2_b)
    mom_m = dict(rel_bias=m_rel_bias, w_ada=m_w_ada, b_ada=m_b_ada, w_in=m_w_in, b_in=m_b_in,
                 attn_sinks=m_attn_sinks, gmlp_ln_g=m_gmlp_ln_g, gmlp_ln_b=m_gmlp_ln_b, gmlp_w_s=m_gmlp_w_s,
                 gmlp_b_s=m_gmlp_b_s, attn_out_g=m_attn_out_g, gmlp_out_g=m_gmlp_out_g, w_out=m_w_out,
                 ln1_g=m_ln1_g, ln1_b=m_ln1_b, w_gate_up=m_w_gate_up, w_down=m_w_down, ln2_g=m_ln2_g,
                 ln2_b=m_ln2_b)
    mom_v = dict(rel_bias=v_rel_bias, w_ada=v_w_ada, b_ada=v_b_ada, w_in=v_w_in, b_in=v_b_in,
                 attn_sinks=v_attn_sinks, gmlp_ln_g=v_gmlp_ln_g, gmlp_ln_b=v_gmlp_ln_b, gmlp_w_s=v_gmlp_w_s,
                 gmlp_b_s=v_gmlp_b_s, attn_out_g=v_attn_out_g, gmlp_out_g=v_gmlp_out_g, w_out=v_w_out,
                 ln1_g=v_ln1_g, ln1_b=v_ln1_b, w_gate_up=v_w_gate_up, w_down=v_w_down, ln2_g=v_ln2_g,
                 ln2_b=v_ln2_b)

    t = x.shape[1]
    tm = min(512, t)
    tn_ff = D_FF // 2
    tk_long, tk_short = min(4096, t), min(2048, t)
    me = 4 * lax.axis_index("x") + 2 * lax.axis_index("y") + lax.axis_index("c")
    xs = x[0]
    target = loss_target[0]

    ncol = w_ada.shape[2]
    b_cols = lax.dynamic_slice(b_ada, (0, me * ncol), (1, ncol))
    c_g, mod_g, w_in_g = _head(jnp.broadcast_to(c, (8, D_MODEL)), w_ada[0], b_cols, w_in[0].T.astype(BF16))
    c_all = c_g[:, 0, :]
    bucket = _t5_bucket_map()
    (bias,), _ = _bias_table(rel_bias, bucket, comm=None)
    w_in_t = w_in_g.reshape(IN_W, D_MODEL)
    mod = lax.dynamic_slice(mod_g, (0, me, 0), (N_DEV, 1, ncol)).reshape(1, N_DEV * ncol)
    sh1, sc1, g1, sh2, sc2, g2 = [mod[:, i * D_MODEL:(i + 1) * D_MODEL] for i in range(6)]

    causal = jnp.tril(jnp.ones((BLK, BLK), dtype=bool))
    ws = jnp.where(causal[None], gmlp_w_s[0], 0.0).astype(BF16)
    pair = lambda w: jnp.concatenate([w[0::2], w[1::2]], axis=2)
    ws2, wst2 = pair(ws), pair(jnp.swapaxes(ws, 1, 2))
    bfull = jnp.repeat(gmlp_b_s[0].T, GMLP_W // N_GROUPS, axis=1)
    sinks = attn_sinks[0]

    (proj, h1), (w_down_g,) = _inproj(xs, sc1, sh1, w_in_t, b_in, tm, comm=([w_down[0].astype(BF16)], ("gather2",)))
    (mixed, *kept), (w_out_g, w_gu_g) = _mix_fwd(
        proj, bias, sinks, gmlp_ln_g, gmlp_ln_b, ws2, bfull, attn_out_g, gmlp_out_g,
        comm=([w_out[0].astype(BF16), w_gate_up[0].T.astype(BF16)], ("gather2", "gather2")))
    w_out_f = w_out_g.reshape(D_MODEL, D_MODEL)
    w_gu_t = w_gu_g.reshape(2 * D_FF, D_MODEL)
    (y1, x1, h2), _ = _outproj(mixed, w_out_f, xs, g1, ln1_g, ln1_b, sc2, sh2, tm, comm=None)
    w_down_f = w_down_g.reshape(D_FF, D_MODEL)
    dsu, sg, act, dz2, dy2, loss_p, d_ln2g, d_ln2b, d_g2 = _ffn_fwd(h2, w_gu_t, w_down_f, x1, target, g2, ln2_g, ln2_b,
                                                                    min(256, t))

    slots = lambda a: a.reshape(N_DEV, -1, D_MODEL)
    dw_down = _wgrad("wgrad_down", act, dy2, tn_ff, tk_short)
    (dgate, dup, dz1, dy1, d_sc2, d_sh2, d_ln1g, d_ln1b, d_g1), (r_down,) = _ffn_bwd(
        dy2, w_down_f, dsu, sg, w_gu_t, x1, xs, y1, dz2, sc2, g1, ln1_g, min(256, t),
        comm=([slots(dw_down)], ("scatter",)))
    dw_gu_t = _wgrad("wgrad_gate_up", dgate, h2, tn_ff, tk_short, a2=dup)
    dw_out = _wgrad("wgrad_out", mixed, dy1, D_MODEL, tk_long)
    ((dproj, dkvn, dl_acc, dsink_acc, d_lng, d_lnb, d_ws, d_bs, d_aog, d_gog), (r_gu, r_out)) = _mix_bwd(
        proj, gmlp_ln_g, gmlp_ln_b, ws2, wst2, bfull, attn_out_g, gmlp_out_g, dy1, w_out_f.T, kept,
        comm=([slots(dw_gu_t), slots(dw_out)], ("scatter", "scatter")))
    d_relb = _bias_grad(dl_acc, bucket)

    rsum = lambda a: jnp.sum(a, axis=0)
    early_g = dict(
        rel_bias=d_relb[:, 0, :N_BUCKETS].T, attn_sinks=rsum(dsink_acc)[:N_HEADS],
        gmlp_ln_g=rsum(d_lng), gmlp_ln_b=rsum(d_lnb),
        gmlp_b_s=jnp.sum(d_bs.reshape(BLK, N_GROUPS, GMLP_W // N_GROUPS), axis=2).T,
        attn_out_g=rsum(d_aog), gmlp_out_g=rsum(d_gog), ln1_g=rsum(d_ln1g), ln1_b=rsum(d_ln1b),
        ln2_g=rsum(d_ln2g), ln2_b=rsum(d_ln2b))
    (grad_x, dproj_b, d_bin, d_sc1, d_sh1), _ = _din(dproj, dkvn, w_in_t, xs, dz1, sc1, tm, comm=None)
    ws_rows = lambda a: a.reshape(N_GROUPS * BLK, BLK)
    d_ws_b = ws_rows(jnp.where(causal[None], d_ws, 0.0)).astype(BF16)
    dw_in_t, (early_all, ws_all) = _wgrad("wgrad_in", dproj_b, h1, IN_W // 2, tk_long,
                                          comm=([_pack(early_g, SMALL_EARLY), d_ws_b], ("gather2", "gather2")))
    dmod = jnp.concatenate([rsum(d_sh1), rsum(d_sc1), rsum(d_g1), rsum(d_sh2), rsum(d_sc2), rsum(d_g2)])
    late_g = dict(b_ada=dmod, b_in=rsum(d_bin), loss=(0.5 / D_MODEL * jnp.sum(loss_p)).reshape(1))
    late_all, r_in = _scatter_two_level("scatter_in", _pack(late_g, SMALL_LATE), slots(dw_in_t))

    small, _ = _adam_small("adam_small_early", early_all, SMALL_EARLY, wts, mom_m, mom_v)
    small_late, sums = _adam_small("adam_small_late", late_all, SMALL_LATE, wts, mom_m, mom_v)
    small.update(small_late)
    small["gmlp_w_s"] = [o.reshape(gmlp_w_s.shape) for o in _adam_reduce(
        "adam_w_s", ws_all, ws_rows(gmlp_w_s), ws_rows(m_gmlp_w_s), ws_rows(v_gmlp_w_s), N_GROUPS * BLK // 2)]
    loss = sums["loss"][0, 0]

    dmod_all = late_all[:, :_seg_rows(6 * D_MODEL), :].reshape(N_DEV, 6 * D_MODEL)
    dmod_cols = lax.dynamic_slice(dmod_all, (0, me * ncol), (N_DEV, ncol))
    kpad = 128 - N_DEV
    ada = _adam_w_ada(jnp.pad(c_all.T, ((0, 0), (0, kpad))), jnp.pad(dmod_cols, ((0, kpad), (0, 0))),
                      w_ada[0], m_w_ada[0], v_w_ada[0])

    tr = lambda a: jnp.swapaxes(a, -1, -2)
    big = {}
    big["w_in"] = [tr(o)[None] for o in _adam_reduce("adam_w_in", r_in, w_in[0].T, m_w_in[0].T, v_w_in[0].T, 112)]
    big["w_out"] = [o[None] for o in _adam_reduce("adam_w_out", r_out, w_out[0], m_w_out[0], v_w_out[0], 128)]
    big["w_gate_up"] = [tr(o)[None] for o in _adam_reduce("adam_w_gu", r_gu, w_gate_up[0].T, m_w_gate_up[0].T,
                                                           v_w_gate_up[0].T, 352)]
    big["w_down"] = [o[None] for o in _adam_reduce("adam_w_down", r_down, w_down[0], m_w_down[0], v_w_down[0], 176)]
    big["w_ada"] = [o[None] for o in ada]

    outs = [[], [], [], []]
    for name in WEIGHTS:
        for i in range(4):
            outs[i].append(big[name][i] if name in big else small[name][i])
    return (loss, grad_x[None], *outs[0], *outs[1], *outs[2], *outs[3])
```

```python
import math

import jax
import jax.numpy as jnp
from jax import lax
from jax.experimental import pallas as pl
from jax.experimental.pallas import tpu as pltpu

F32 = jnp.float32
BF16 = jnp.bfloat16
MESH = pl.DeviceIdType.MESH

N_DEV = 8
D_MODEL = 1024
HEAD_DIM = 64
N_HEADS = 8
N_GROUPS = 8
ATTN_W = 512
KV_W = 128
GMLP_W = 512
IN_W = 1792
BLK = 128
N_BUCKETS = 32
MAX_DISTANCE = 128
D_FF = 2816
ALPHA = 2.0 ** 0.25
LN_EPS = 1e-5
NEG_INF = -1e30
ADAM_LR = 0.001
ADAM_B1 = 0.9
ADAM_B2 = 0.999
ADAM_EPS = 1e-08
ADAM_WD = 0.01
ADAM_STEP = 10
GELU_C0 = math.sqrt(2.0 / math.pi)
GELU_C1 = 0.044715

VMEM_LIMIT = 56 * 1024 * 1024


def _params(sem):
    return pltpu.CompilerParams(dimension_semantics=sem, vmem_limit_bytes=VMEM_LIMIT)


def _dot(a, b):
    return lax.dot_general(a, b, (((1,), (0,)), ((), ())), preferred_element_type=F32)


def _dot_nt(a, b):
    return lax.dot_general(a, b, (((1,), (1,)), ((), ())), preferred_element_type=F32)


def _dot_tn(a, b):
    return lax.dot_general(a, b, (((0,), (0,)), ((), ())), preferred_element_type=F32)


def _full(shape):
    nd = len(shape)
    return pl.BlockSpec(shape, lambda *_: (0,) * nd)


def _colsum(v):
    r, c = v.shape
    return jnp.sum(jnp.sum(v.reshape(r // 8, 8, c), axis=0), axis=0, keepdims=True)


def _sigmoid(v):
    return 1.0 / (1.0 + jnp.exp(-v))


def _gelu_parts(v):
    v2 = v * v
    t = jnp.tanh(GELU_C0 * (v + GELU_C1 * v * v2))
    g = 0.5 * v * (1.0 + t)
    dg = 0.5 * (1.0 + t) + 0.5 * v * (1.0 - t * t) * (GELU_C0 * (1.0 + 3.0 * GELU_C1 * v2))
    return g, dg


def _ln_stats(z):
    mu = jnp.mean(z, axis=1, keepdims=True)
    zc = z - mu
    var = jnp.mean(zc * zc, axis=1, keepdims=True)
    rstd = lax.rsqrt(var + LN_EPS)
    return zc * rstd, rstd


def _ln_bwd(dxhat, xhat, rstd):
    m1 = jnp.mean(dxhat, axis=1, keepdims=True)
    m2 = jnp.mean(dxhat * xhat, axis=1, keepdims=True)
    return rstd * (dxhat - m1 - xhat * m2)


def _seg_mean64(v):
    r = v.shape[0]
    lo = lax.broadcasted_iota(jnp.int32, (r, 128), 1) < 64
    outs = []
    for j in range(v.shape[1] // 128):
        ch = v[:, 128 * j:128 * (j + 1)]
        s_lo = jnp.sum(jnp.where(lo, ch, 0.0), axis=1, keepdims=True)
        s_hi = jnp.sum(jnp.where(lo, 0.0, ch), axis=1, keepdims=True)
        outs.append(jnp.where(lo, s_lo, s_hi) * (1.0 / 64.0))
    return jnp.concatenate(outs, axis=1)


def _rms(a, g):
    r = lax.rsqrt(jnp.mean(a * a, axis=1, keepdims=True) + LN_EPS)
    return a * r * g, r


def _rms_bwd(dout, a, r, g):
    t = dout * g
    return r * t - a * (r * r * r) * jnp.mean(t * a, axis=1, keepdims=True)


PEER_ORDER = (1, 2, 4, 3, 5, 6, 7)


def _peer(j):
    x, y, c = lax.axis_index("x"), lax.axis_index("y"), lax.axis_index("c")
    px = 1 - x if j & 4 else x
    py = 1 - y if j & 2 else y
    pc = 1 - c if j & 1 else c
    return (px, py, pc), 4 * px + 2 * py + pc


SIBLING = 1
CHIP_FLIPS = (4, 2, 6)


def _exchange_phase(phase, ins, outs, modes, send_sems, recv_sems, loc_sems):
    me = 4 * lax.axis_index("x") + 2 * lax.axis_index("y") + lax.axis_index("c")
    for k, mode in enumerate(modes):
        def copy(i, src, slot, dev, k=k):
            return pltpu.make_async_remote_copy(src_ref=src, dst_ref=outs[k].at[slot], send_sem=send_sems.at[k, i],
                                                recv_sem=recv_sems.at[k, i], device_id=dev, device_id_type=MESH)

        src_me = ins[k].at[me] if mode == "scatter" else ins[k]
        local = pltpu.make_async_copy(src_me, outs[k].at[me], loc_sems.at[k])
        if mode == "gather2":
            sib_dev, sib_idx = _peer(SIBLING)
            chips = [_peer(j) for j in CHIP_FLIPS]
            far = [_peer(j | SIBLING)[1] for j in CHIP_FLIPS]
            if phase == "start":
                local.start()
                copy(0, ins[k], me, sib_dev).start()
                for i, (dev, _) in enumerate(chips):
                    copy(1 + i, ins[k], me, dev).start()
            elif phase == "mid":
                for i, (dev, idx) in enumerate(chips):
                    copy(1 + i, ins[k], idx, dev).wait_recv()
                    copy(4 + i, outs[k].at[idx], idx, sib_dev).start()
            else:
                copy(0, ins[k], sib_idx, sib_dev).wait_recv()
                for i, slot in enumerate(far):
                    copy(4 + i, ins[k], slot, sib_dev).wait_recv()
                copy(0, ins[k], me, sib_dev).wait_send()
                for i, (dev, idx) in enumerate(chips):
                    copy(1 + i, ins[k], me, dev).wait_send()
                    copy(4 + i, outs[k].at[idx], idx, sib_dev).wait_send()
                local.wait()
            continue
        peers = [_peer(j) for j in PEER_ORDER]
        if phase == "start":
            local.start()
            for i, (dev, idx) in enumerate(peers):
                copy(i, ins[k].at[idx] if mode == "scatter" else ins[k], me, dev).start()
        elif phase == "end":
            for i, (dev, idx) in enumerate(peers):
                copy(i, src_me, idx, dev).wait_recv()
            for i, (dev, idx) in enumerate(peers):
                copy(i, src_me, me, dev).wait_send()
            local.wait()


def _exchange_shapes(arrays, modes):
    return [jax.ShapeDtypeStruct((N_DEV,) + (a.shape[1:] if m == "scatter" else a.shape), a.dtype)
            for a, m in zip(arrays, modes)]


def _exchange_sems(n):
    return [pltpu.SemaphoreType.DMA((n, N_DEV - 1)), pltpu.SemaphoreType.DMA((n, N_DEV - 1)),
            pltpu.SemaphoreType.DMA((n,))]


def _exchange(name, arrays, modes):
    n = len(arrays)

    def body(*refs):
        for phase in ("start", "mid", "end"):
            _exchange_phase(phase, refs[:n], refs[n:2 * n], modes, *refs[2 * n:])

    any_spec = pl.BlockSpec(memory_space=pl.ANY)
    return pl.pallas_call(
        body, name=name, out_shape=_exchange_shapes(arrays, modes),
        in_specs=[any_spec] * n, out_specs=[any_spec] * n, scratch_shapes=_exchange_sems(n),
    )(*arrays)


N_CHIP = 4


def _scatter_two_level(name, pack, parts):
    r, ncols = parts.shape[1:]

    def body(pack_ref, parts_ref, late_ref, got_ref, sib_ref, h_ref, g_send, g_recv, g_loc, d_send, d_recv, i_send, i_recv):
        x, y, c = lax.axis_index("x"), lax.axis_index("y"), lax.axis_index("c")
        my_chip = 2 * x + y
        sib_dev, _ = _peer(SIBLING)
        gather = ([pack_ref], [late_ref], ("gather",), g_send, g_recv, g_loc)
        _exchange_phase("start", *gather)

        def to_sibling(q):
            return pltpu.make_async_remote_copy(src_ref=parts_ref.at[2 * q + 1 - c], dst_ref=sib_ref.at[q],
                                                send_sem=d_send.at[q], recv_sem=d_recv.at[q],
                                                device_id=sib_dev, device_id_type=MESH)

        for q in range(N_CHIP):
            to_sibling(q).start()
        for q in range(N_CHIP):
            to_sibling(q).wait_recv()
            h_ref[q] = (parts_ref[2 * q + c].astype(F32) + sib_ref[q].astype(F32)).astype(BF16)

        def to_chip(i, slot):
            dev, idx = _peer(CHIP_FLIPS[i])
            return pltpu.make_async_remote_copy(src_ref=h_ref.at[idx // 2], dst_ref=got_ref.at[slot],
                                                send_sem=i_send.at[i], recv_sem=i_recv.at[i],
                                                device_id=dev, device_id_type=MESH)

        for i in range(len(CHIP_FLIPS)):
            to_chip(i, my_chip).start()
        got_ref[my_chip] = h_ref[my_chip]
        for i in range(len(CHIP_FLIPS)):
            to_chip(i, _peer(CHIP_FLIPS[i])[1] // 2).wait_recv()
        for i in range(len(CHIP_FLIPS)):
            to_chip(i, my_chip).wait_send()
        for q in range(N_CHIP):
            to_sibling(q).wait_send()
        _exchange_phase("end", *gather)

    any_spec = pl.BlockSpec(memory_space=pl.ANY)
    vmem = pl.BlockSpec(memory_space=pltpu.VMEM)
    dma = pltpu.SemaphoreType.DMA
    return pl.pallas_call(
        body, name=name,
        out_shape=[jax.ShapeDtypeStruct((N_DEV,) + pack.shape, pack.dtype),
                   jax.ShapeDtypeStruct((N_CHIP, r, ncols), parts.dtype)],
        in_specs=[any_spec, vmem], out_specs=[any_spec, vmem],
        scratch_shapes=[pltpu.VMEM((N_CHIP, r, ncols), parts.dtype), pltpu.VMEM((N_CHIP, r, ncols), parts.dtype),
                        dma((1, N_DEV - 1)), dma((1, N_DEV - 1)), dma((1,)),
                        dma((N_CHIP,)), dma((N_CHIP,)), dma((len(CHIP_FLIPS),)), dma((len(CHIP_FLIPS),))],
        compiler_params=pltpu.CompilerParams(vmem_limit_bytes=VMEM_LIMIT),
    )(pack, parts)


def _call(body, *, name, grid, in_specs, out_specs, out_shape, args, sem, scratch_shapes=(), comm=None):
    if comm is None:
        outs = pl.pallas_call(body, name=name, grid=grid, in_specs=list(in_specs), out_specs=list(out_specs),
                              out_shape=list(out_shape), scratch_shapes=list(scratch_shapes),
                              compiler_params=_params(sem))(*args)
        return list(outs), []
    arrays, modes = comm
    n_in, n_out, nc, ns = len(in_specs), len(out_specs), len(arrays), len(scratch_shapes)
    n_steps = math.prod(grid)

    def hosted(*refs):
        ins, cins = refs[:n_in], refs[n_in:n_in + nc]
        outs, couts = refs[n_in + nc:n_in + nc + n_out], refs[n_in + nc + n_out:n_in + 2 * nc + n_out]
        scratch = refs[n_in + 2 * nc + n_out:]
        ex = (cins, couts, modes) + tuple(scratch[ns:])
        step = pl.program_id(0)
        for ax in range(1, len(grid)):
            step = step * grid[ax] + pl.program_id(ax)

        @pl.when(step == 0)
        def _():
            _exchange_phase("start", *ex)

        body(*ins, *outs, *scratch[:ns])

        if "gather2" in modes:
            @pl.when(step == (3 * n_steps) // 4)
            def _():
                _exchange_phase("mid", *ex)

        @pl.when(step == n_steps - 1)
        def _():
            _exchange_phase("end", *ex)

    any_spec = pl.BlockSpec(memory_space=pl.ANY)
    res = pl.pallas_call(
        hosted, name=name, grid=grid, in_specs=list(in_specs) + [any_spec] * nc,
        out_specs=list(out_specs) + [any_spec] * nc, out_shape=list(out_shape) + _exchange_shapes(arrays, modes),
        scratch_shapes=list(scratch_shapes) + _exchange_sems(nc),
        compiler_params=_params(tuple("arbitrary" for _ in grid)))(*args, *arrays)
    return list(res[:n_out]), list(res[n_out:])


def _mod_partial(c_all, w_ada, b_ada_cols):
    def body(c_ref, w_ref, b_ref, o_ref):
        cv = c_ref[...]
        s = (cv * _sigmoid(cv)).astype(BF16)
        o_ref[...] = _dot(s, w_ref[...].astype(BF16)) + b_ref[...]

    ncol = w_ada.shape[1]
    return pl.pallas_call(
        body, name="mod_partial", out_shape=jax.ShapeDtypeStruct((N_DEV, ncol), F32),
        in_specs=[_full(c_all.shape), _full(w_ada.shape), _full(b_ada_cols.shape)],
        out_specs=_full((N_DEV, ncol)), grid=(1,), compiler_params=_params(("arbitrary",)),
    )(c_all, w_ada, b_ada_cols)


def _bias_table(rel_bias, bucket, comm):
    def body(rb_ref, bk_ref, o_ref):
        h = pl.program_id(0)
        bk = bk_ref[...]
        acc = jnp.zeros((BLK, 2 * BLK), F32)
        for b in range(N_BUCKETS):
            acc = jnp.where(bk == b, rb_ref[b, h], acc)
        dist = (lax.broadcasted_iota(jnp.int32, (BLK, 2 * BLK), 0) + BLK
                - lax.broadcasted_iota(jnp.int32, (BLK, 2 * BLK), 1))
        o_ref[0] = jnp.where((dist >= 0) & (dist < BLK), acc, NEG_INF)

    return _call(
        body, name="bias_table", out_shape=[jax.ShapeDtypeStruct((N_HEADS, BLK, 2 * BLK), F32)],
        in_specs=[pl.BlockSpec(memory_space=pltpu.SMEM), _full((BLK, 2 * BLK))],
        out_specs=[pl.BlockSpec((1, BLK, 2 * BLK), lambda h: (h, 0, 0))], grid=(N_HEADS,),
        sem=("arbitrary",), comm=comm, args=(rel_bias, bucket))


def _bias_grad(dl_acc, bucket):
    def body(dl_ref, bk_ref, o_ref):
        bk = bk_ref[...]
        dl = dl_ref[0]
        lane = lax.broadcasted_iota(jnp.int32, (1, 128), 1)
        row = jnp.zeros((1, 128), F32)
        for b in range(N_BUCKETS):
            s = jnp.sum(jnp.sum(jnp.where(bk == b, dl, 0.0), axis=1, keepdims=True), axis=0, keepdims=True)
            row = jnp.where(lane == b, s, row)
        o_ref[0] = row

    return pl.pallas_call(
        body, name="bias_grad", out_shape=jax.ShapeDtypeStruct((N_HEADS, 1, 128), F32),
        in_specs=[pl.BlockSpec((1, BLK, 2 * BLK), lambda h: (h, 0, 0)), _full((BLK, 2 * BLK))],
        out_specs=pl.BlockSpec((1, 1, 128), lambda h: (h, 0, 0)), grid=(N_HEADS,),
        compiler_params=_params(("arbitrary",)),
    )(dl_acc, bucket)


def _inproj(x, sc1, sh1, w_in_t, b_in, tm, comm):
    t, d = x.shape
    n = w_in_t.shape[0]

    def body(x_ref, sc_ref, sh_ref, w_ref, b_ref, proj_ref, h_ref):
        h = (x_ref[...] * (1.0 + sc_ref[...]) + sh_ref[...]).astype(BF16)
        h_ref[...] = h
        proj_ref[...] = _dot_nt(h, w_ref[...]) + b_ref[...]

    row = lambda w: pl.BlockSpec((tm, w), lambda i: (i, 0))
    return _call(
        body, name="inproj", grid=(t // tm,),
        out_shape=[jax.ShapeDtypeStruct((t, n), F32), jax.ShapeDtypeStruct((t, d), BF16)],
        in_specs=[row(d), _full((1, d)), _full((1, d)), _full((n, d)), _full((1, n))],
        out_specs=[row(n), row(d)], sem=("parallel",), comm=comm, args=(x, sc1, sh1, w_in_t, b_in))


HALF = 64
ROWS = 32


def _lane_lo(rows):
    return lax.broadcasted_iota(jnp.int32, (rows, 128), 1) < 64


def _mix_stage_kv(proj_ref, kvp_ref, s):
    lo = _lane_lo(2 * BLK)
    for name, col in (("k", ATTN_W), ("v", ATTN_W + KV_W)):
        cur = jnp.concatenate([kvp_ref[:, col - ATTN_W:col - ATTN_W + KV_W], proj_ref[:, col:col + KV_W]], axis=0)
        plain, swapped = cur.astype(BF16), pltpu.roll(cur, 64, 1).astype(BF16)
        zero = jnp.zeros_like(plain)
        for g in range(2):
            dup = jnp.where(lo, plain, swapped) if g == 0 else jnp.where(lo, swapped, plain)
            s[name + "d"][g] = dup
            s[name + "m"][g] = jnp.concatenate([jnp.where(lo, dup, zero), jnp.where(lo, zero, dup)], axis=0)


def _group_rows(ref, g):
    return ref[4 * g:4 * g + 4].reshape(4 * BLK, ref.shape[2])


def _pair_rows(ref, g):
    return jnp.concatenate([jnp.concatenate([ref[4 * g + 2 * c], ref[4 * g + 2 * c + 1]], axis=1) for c in range(2)],
                           axis=0)


def _mask_heads(src_ref, dst_ref):
    lo = _lane_lo(BLK)
    for j in range(4):
        chunk = src_ref[:, 128 * j:128 * (j + 1)]
        dst_ref[2 * j] = jnp.where(lo, chunk, 0.0).astype(BF16)
        dst_ref[2 * j + 1] = jnp.where(lo, 0.0, chunk).astype(BF16)


def _mix_stage_attn(proj_ref, bias_ref, sinks_ref, n, s):
    _mask_heads(proj_ref, s["qm"])
    for g in range(2):
        s["lg"][g] = _dot_nt(_group_rows(s["qm"], g), s["kd"][g])
    n0mask = (n == 0) & (lax.broadcasted_iota(jnp.int32, (HALF, 2 * BLK), 1) < BLK)
    lane = lax.broadcasted_iota(jnp.int32, (HALF, 128), 1)
    for hf in range(BLK // HALF):
        rows = slice(HALF * hf, HALF * (hf + 1))
        psink = jnp.zeros((HALF, 128), F32)
        for h in range(N_HEADS):
            sk = sinks_ref[h]
            grows = slice(BLK * (h % 4) + HALF * hf, BLK * (h % 4) + HALF * (hf + 1))
            logit = s["lg"][h // 4, grows, :] * (HEAD_DIM ** -0.5) + bias_ref[h, rows, :]
            logit = jnp.where(n0mask, NEG_INF, logit)
            m = jnp.maximum(jnp.max(logit, axis=1, keepdims=True), sk)
            e = jnp.exp(logit - m)
            es = jnp.exp(sk - m)
            inv = 1.0 / (jnp.sum(e, axis=1, keepdims=True) + es)
            p = e * inv
            s["p"][h, rows, :] = p
            s["pb"][h, rows, :] = p.astype(BF16)
            psink = jnp.where(lane == h, es * inv, psink)
        s["psink"][rows, :] = psink
    for g in range(2):
        out = _dot(_pair_rows(s["pb"], g), s["vm"][g])
        s["attn"][:, 256 * g:256 * g + 128] = out[0:BLK]
        s["attn"][:, 256 * g + 128:256 * g + 256] = out[BLK:2 * BLK]


def _mix_stage_gmlp_pre(proj_ref, lng, lnb, s, keep):
    c0 = ATTN_W + 2 * KV_W
    for r0 in range(0, BLK, ROWS):
        rows = slice(r0, r0 + ROWS)
        u, du = _gelu_parts(proj_ref[rows, c0:c0 + GMLP_W])
        a, da = _gelu_parts(proj_ref[rows, c0 + GMLP_W:c0 + 2 * GMLP_W])
        ac = a - _seg_mean64(a)
        rstd = lax.rsqrt(_seg_mean64(ac * ac) + LN_EPS)
        vhat = ac * rstd
        s["u"][rows, :] = u
        s["vnb"][rows, :] = (vhat * lng + lnb).astype(BF16)
        if keep:
            s["du"][rows, :] = du
            s["da"][rows, :] = da
            s["vhat"][rows, :] = vhat
            s["rstd"][rows, :] = rstd


def _stack_halves(chunk):
    lo = _lane_lo(BLK)
    zero = jnp.zeros_like(chunk)
    return jnp.concatenate([jnp.where(lo, chunk, zero), jnp.where(lo, zero, chunk)], axis=0)


def _mix_stage_gmlp_mix(ws2_ref, bfull_ref, s):
    for j in range(4):
        cols = slice(128 * j, 128 * (j + 1))
        s["ms"][:, cols] = _dot(ws2_ref[j], _stack_halves(s["vnb"][:, cols])) + bfull_ref[:, cols]


def _mix_scratch(keep):
    f32 = lambda *shape: pltpu.VMEM(shape, F32)
    b16 = lambda *shape: pltpu.VMEM(shape, BF16)
    names = dict(kd=b16(2, 2 * BLK, 128), vd=b16(2, 2 * BLK, 128), km=b16(2, 4 * BLK, 128), vm=b16(2, 4 * BLK, 128),
                 qm=b16(N_HEADS, BLK, 128), lg=f32(2, 4 * BLK, 2 * BLK), pb=b16(N_HEADS, BLK, 2 * BLK),
                 u=f32(BLK, GMLP_W), vnb=b16(BLK, GMLP_W), ms=f32(BLK, GMLP_W))
    if keep:
        names.update(dom=b16(N_HEADS, BLK, 128), dls=b16(N_HEADS, BLK, 2 * BLK),
                     dattn=f32(BLK, ATTN_W), dmix=f32(BLK, D_MODEL), du=f32(BLK, GMLP_W), da=f32(BLK, GMLP_W),
                     vhat=f32(BLK, GMLP_W), rstd=f32(BLK, GMLP_W), dmsb=b16(BLK, GMLP_W), dvn=f32(BLK, GMLP_W))
    return list(names), list(names.values())


def _mix_specs(with_logit_inputs):
    logit_inputs = [_full((N_HEADS, BLK, 2 * BLK)), pl.BlockSpec(memory_space=pltpu.SMEM)] if with_logit_inputs else []
    return [pl.BlockSpec((BLK, IN_W), lambda n: (n, 0)),
            pl.BlockSpec((BLK, 2 * KV_W), lambda n: (jnp.maximum(n - 1, 0), ATTN_W // (2 * KV_W)))] + logit_inputs + [
            _full((1, GMLP_W)), _full((1, GMLP_W)),
            _full((N_GROUPS // 2, BLK, 2 * BLK)), _full((BLK, GMLP_W)),
            _full((1, ATTN_W)), _full((1, GMLP_W))]


KEPT = [("p", (N_HEADS, BLK, 2 * BLK), F32), ("psink", (BLK, 128), F32), ("attn", (BLK, ATTN_W), F32)]


def _kept_shapes(t):
    full = lambda blk: (blk[0], t, blk[2]) if len(blk) == 3 else (t, blk[1])
    return [jax.ShapeDtypeStruct(full(blk), dt) for _, blk, dt in KEPT]


def _kept_specs():
    return [pl.BlockSpec(blk, (lambda n: (0, n, 0)) if len(blk) == 3 else (lambda n: (n, 0))) for _, blk, _ in KEPT]


def _mix_fwd(proj, bias, sinks, lng, lnb, ws2, bfull, aog, gog, comm):
    t = proj.shape[0]
    names, shapes = _mix_scratch(False)

    def body(proj_ref, kvp_ref, bias_ref, sinks_ref, lng_ref, lnb_ref, ws2_ref, bfull_ref, aog_ref, gog_ref,
             out_ref, *rest):
        s = dict(zip([name for name, _, _ in KEPT] + names, rest))
        n = pl.program_id(0)
        _mix_stage_kv(proj_ref, kvp_ref, s)
        _mix_stage_attn(proj_ref, bias_ref, sinks_ref, n, s)
        _mix_stage_gmlp_pre(proj_ref, lng_ref[...], lnb_ref[...], s, False)
        _mix_stage_gmlp_mix(ws2_ref, bfull_ref, s)
        for r0 in range(0, BLK, ROWS):
            rows = slice(r0, r0 + ROWS)
            out_ref[rows, 0:ATTN_W] = _rms(s["attn"][rows, :], aog_ref[...])[0].astype(BF16)
            out_ref[rows, ATTN_W:ATTN_W + GMLP_W] = _rms(s["u"][rows, :] * s["ms"][rows, :], gog_ref[...])[0].astype(BF16)

    return _call(
        body, name="mix_fwd", grid=(t // BLK,),
        out_shape=[jax.ShapeDtypeStruct((t, D_MODEL), BF16)] + _kept_shapes(t),
        in_specs=_mix_specs(True), out_specs=[pl.BlockSpec((BLK, D_MODEL), lambda n: (n, 0))] + _kept_specs(),
        scratch_shapes=shapes,
        sem=("parallel",), comm=comm, args=(proj, proj, bias, sinks, lng, lnb, ws2, bfull, aog, gog))


def _mix_bwd(proj, lng, lnb, ws2, wst2, bfull, aog, gog, dy, w_out, kept, comm):
    t = proj.shape[0]
    nb = t // BLK
    names, shapes = _mix_scratch(True)
    c_gu = ATTN_W + 2 * KV_W

    def body(proj_ref, kvp_ref, lng_ref, lnb_ref, ws2_ref, bfull_ref, aog_ref, gog_ref,
             wst2_ref, dy_ref, wout_ref, *rest):
        n_kept = len(KEPT)
        s = dict(zip([name for name, _, _ in KEPT], rest[:n_kept]))
        (dproj_ref, dkvn_ref, dl_ref, dsink_ref, dlng_ref, dlnb_ref, dws_ref, dbs_ref, daog_ref,
         dgog_ref) = rest[n_kept:n_kept + 10]
        s.update(zip(names, rest[n_kept + 10:]))
        n = pl.program_id(0)

        @pl.when(n == 0)
        def _():
            for r in (dl_ref, dsink_ref, dlng_ref, dlnb_ref, dws_ref, dbs_ref, daog_ref, dgog_ref):
                r[...] = jnp.zeros_like(r)

        s["dmix"][...] = _dot(dy_ref[...], wout_ref[...])
        _mix_stage_kv(proj_ref, kvp_ref, s)
        _mask_heads(proj_ref, s["qm"])
        lng = lng_ref[...]
        _mix_stage_gmlp_pre(proj_ref, lng, lnb_ref[...], s, True)
        _mix_stage_gmlp_mix(ws2_ref, bfull_ref, s)

        aog, gog = aog_ref[...], gog_ref[...]
        for r0 in range(0, BLK, ROWS):
            rows = slice(r0, r0 + ROWS)
            attn, dma = s["attn"][rows, :], s["dmix"][rows, 0:ATTN_W]
            _, r_a = _rms(attn, aog)
            daog_ref[...] += _colsum(dma * attn * r_a)
            s["dattn"][rows, :] = _rms_bwd(dma, attn, r_a, aog)
            u, ms, dmg = s["u"][rows, :], s["ms"][rows, :], s["dmix"][rows, ATTN_W:ATTN_W + GMLP_W]
            gm = u * ms
            _, r_g = _rms(gm, gog)
            dgog_ref[...] += _colsum(dmg * gm * r_g)
            dgm = _rms_bwd(dmg, gm, r_g, gog)
            dproj_ref[rows, c_gu:c_gu + GMLP_W] = dgm * ms * s["du"][rows, :]
            dms = dgm * u
            dbs_ref[rows, :] += dms
            s["dmsb"][rows, :] = dms.astype(BF16)

        _mask_heads(s["dattn"], s["dom"])
        for g in range(2):
            s["lg"][g] = _dot_nt(_group_rows(s["dom"], g), s["vd"][g])
        lane = lax.broadcasted_iota(jnp.int32, (HALF, 128), 1)
        for hf in range(BLK // HALF):
            rows = slice(HALF * hf, HALF * (hf + 1))
            dsink = jnp.zeros((HALF, 128), F32)
            for h in range(N_HEADS):
                grows = slice(BLK * (h % 4) + HALF * hf, BLK * (h % 4) + HALF * (hf + 1))
                dp = s["lg"][h // 4, grows, :]
                p = s["p"][h, rows, :]
                s["pb"][h, rows, :] = p.astype(BF16)
                rs = jnp.sum(p * dp, axis=1, keepdims=True)
                dl = p * (dp - rs)
                dl_ref[h, rows, :] += dl
                dsink = dsink + jnp.where(lane == h, -s["psink"][rows, :] * rs, 0.0)
                s["dls"][h, rows, :] = (dl * (HEAD_DIM ** -0.5)).astype(BF16)
            dsink_ref[rows, :] += dsink
        for g in range(2):
            dq = _dot(_pair_rows(s["dls"], g), s["km"][g])
            dproj_ref[:, 256 * g:256 * g + 128] = dq[0:BLK]
            dproj_ref[:, 256 * g + 128:256 * g + 256] = dq[BLK:2 * BLK]
        lo_k = _lane_lo(2 * BLK)
        for col, lhs, rhs in ((0, "dls", "qm"), (KV_W, "pb", "dom")):
            raw = [_dot_tn(_group_rows(s[lhs], g), _group_rows(s[rhs], g)) for g in range(2)]
            both = [r + pltpu.roll(r, 64, 1) for r in raw]
            dkv = jnp.where(lo_k, both[0], both[1])
            dproj_ref[:, ATTN_W + col:ATTN_W + col + KV_W] = dkv[BLK:2 * BLK]
            dkvn_ref[:, col:col + KV_W] = dkv[0:BLK]

        for j in range(4):
            cols = slice(128 * j, 128 * (j + 1))
            dm2 = _stack_halves(s["dmsb"][:, cols])
            vnb = s["vnb"][:, cols]
            dws2 = _dot_nt(dm2, vnb)
            dws_ref[2 * j] += dws2[0:BLK]
            dws_ref[2 * j + 1] += dws2[BLK:2 * BLK]
            s["dvn"][:, cols] = _dot(wst2_ref[j], dm2)
        for r0 in range(0, BLK, ROWS):
            rows = slice(r0, r0 + ROWS)
            dvn, vhat = s["dvn"][rows, :], s["vhat"][rows, :]
            dlng_ref[...] += _colsum(dvn * vhat)
            dlnb_ref[...] += _colsum(dvn)
            dvh = dvn * lng
            dact = s["rstd"][rows, :] * (dvh - _seg_mean64(dvh) - vhat * _seg_mean64(dvh * vhat))
            dproj_ref[rows, c_gu + GMLP_W:IN_W] = dact * s["da"][rows, :]

    acc_row = lambda w: jax.ShapeDtypeStruct((1, w), F32)
    out_shape = [jax.ShapeDtypeStruct((t, IN_W), F32), jax.ShapeDtypeStruct((t, 2 * KV_W), F32),
                 jax.ShapeDtypeStruct((N_HEADS, BLK, 2 * BLK), F32), jax.ShapeDtypeStruct((BLK, 128), F32),
                 acc_row(GMLP_W), acc_row(GMLP_W), jax.ShapeDtypeStruct((N_GROUPS, BLK, BLK), F32),
                 jax.ShapeDtypeStruct((BLK, GMLP_W), F32), acc_row(ATTN_W), acc_row(GMLP_W)]
    out_specs = [pl.BlockSpec((BLK, IN_W), lambda n: (n, 0)),
                 pl.BlockSpec((BLK, 2 * KV_W), lambda n: ((n + nb - 1) % nb, 0)),
                 _full((N_HEADS, BLK, 2 * BLK)), _full((BLK, 128)), _full((1, GMLP_W)), _full((1, GMLP_W)),
                 _full((N_GROUPS, BLK, BLK)), _full((BLK, GMLP_W)), _full((1, ATTN_W)), _full((1, GMLP_W))]
    in_specs = _mix_specs(False) + [_full((N_GROUPS // 2, BLK, 2 * BLK)),
                               pl.BlockSpec((BLK, D_MODEL), lambda n: (n, 0)),
                               _full((D_MODEL, D_MODEL))] + _kept_specs()
    return _call(
        body, name="mix_bwd", grid=(nb,), out_shape=out_shape, in_specs=in_specs, out_specs=out_specs,
        scratch_shapes=shapes, sem=("arbitrary",), comm=comm,
        args=(proj, proj, lng, lnb, ws2, bfull, aog, gog, wst2, dy, w_out, *kept))


def _outproj(mixed, w_out, x, g1, ln1g, ln1b, sc2, sh2, tm, comm):
    t, d = x.shape

    def body(mx_ref, w_ref, x_ref, g1_ref, lg_ref, lb_ref, sc_ref, sh_ref, y_ref, x1_ref, h2_ref):
        y = _dot(mx_ref[...], w_ref[...])
        xhat, _ = _ln_stats(ALPHA * x_ref[...] + g1_ref[...] * y)
        x1 = xhat * lg_ref[...] + lb_ref[...]
        y_ref[...] = y
        x1_ref[...] = x1
        h2_ref[...] = (x1 * (1.0 + sc_ref[...]) + sh_ref[...]).astype(BF16)

    row = pl.BlockSpec((tm, d), lambda i: (i, 0))
    vec = _full((1, d))
    return _call(
        body, name="outproj", grid=(t // tm,),
        out_shape=[jax.ShapeDtypeStruct((t, d), F32), jax.ShapeDtypeStruct((t, d), F32),
                   jax.ShapeDtypeStruct((t, d), BF16)],
        in_specs=[row, _full((d, d)), row, vec, vec, vec, vec, vec], out_specs=[row, row, row],
        sem=("parallel",), comm=comm, args=(mixed, w_out, x, g1, ln1g, ln1b, sc2, sh2))


def _ffn_fwd(h2, w_gu_t, w_down, x1, target, g2, ln2g, ln2b, tm):
    t, d = x1.shape

    def body(h_ref, w_ref, wd_ref, x1_ref, tg_ref, g2_ref, lg_ref, lb_ref,
             dsu_ref, sg_ref, act_ref, dz_ref, dy_ref, loss_ref, dlg_ref, dlb_ref, dg2_ref):
        @pl.when(pl.program_id(0) == 0)
        def _():
            for r in (loss_ref, dlg_ref, dlb_ref, dg2_ref):
                r[...] = jnp.zeros_like(r)

        h = h_ref[...]
        g = _dot_nt(h, w_ref[0:D_FF])
        u = _dot_nt(h, w_ref[D_FF:2 * D_FF])
        s = _sigmoid(g)
        sg = g * s
        act = (sg * u).astype(BF16)
        dsu_ref[...] = (u * (s * (1.0 + g * (1.0 - s)))).astype(BF16)
        sg_ref[...] = sg.astype(BF16)
        act_ref[...] = act
        y2 = _dot(act, wd_ref[...])
        g2 = g2_ref[...]
        lg = lg_ref[...]
        xhat, rstd = _ln_stats(ALPHA * x1_ref[...] + g2 * y2)
        err = xhat * lg + lb_ref[...] - tg_ref[...]
        loss_ref[...] += _colsum(err * err)
        dx2 = err * (1.0 / d)
        dlg_ref[...] += _colsum(dx2 * xhat)
        dlb_ref[...] += _colsum(dx2)
        dz = _ln_bwd(dx2 * lg, xhat, rstd)
        dg2_ref[...] += _colsum(dz * y2)
        dz_ref[...] = dz
        dy_ref[...] = (g2 * dz).astype(BF16)

    row = pl.BlockSpec((tm, d), lambda i: (i, 0))
    wide = pl.BlockSpec((tm, D_FF), lambda i: (i, 0))
    vec = _full((1, d))
    acc = _full((1, d))
    acc_shape = jax.ShapeDtypeStruct((1, d), F32)
    wide_shape = jax.ShapeDtypeStruct((t, D_FF), BF16)
    return pl.pallas_call(
        body, name="ffn_fwd", grid=(t // tm,),
        out_shape=[wide_shape] * 3 + [jax.ShapeDtypeStruct((t, d), F32), jax.ShapeDtypeStruct((t, d), BF16)]
        + [acc_shape] * 4,
        in_specs=[row, _resident((2 * D_FF, d)), _resident((D_FF, d)), row, row, vec, vec, vec],
        out_specs=[wide] * 3 + [row, row, acc, acc, acc, acc], compiler_params=_params(("arbitrary",)),
    )(h2, w_gu_t, w_down, x1, target, g2, ln2g, ln2b)


def _resident(shape):
    nd = len(shape)
    return pl.BlockSpec(shape, lambda *_: (0,) * nd, pipeline_mode=pl.Buffered(1))


def _ffn_bwd(dy2, w_down, dsu, sg, w_gu_t, x1, x, y, dz2, sc2, g1, ln1g, tm, comm):
    t, d = x1.shape

    def body(dy2_ref, wd_ref, dsu_ref, sg_ref, w_ref, x1_ref, x_ref, y_ref, dz2_ref, sc_ref, g1_ref, lg_ref,
             dg_ref, du_ref, dz1_ref, dy_ref, dsc_ref, dsh_ref, dlg_ref, dlb_ref, dg1_ref):
        @pl.when(pl.program_id(0) == 0)
        def _():
            for r in (dsc_ref, dsh_ref, dlg_ref, dlb_ref, dg1_ref):
                r[...] = jnp.zeros_like(r)

        dact = _dot_nt(dy2_ref[...], wd_ref[...])
        dg = (dact * dsu_ref[...].astype(F32)).astype(BF16)
        du = (dact * sg_ref[...].astype(F32)).astype(BF16)
        dg_ref[...] = dg
        du_ref[...] = du
        dh2 = _dot(dg, w_ref[0:D_FF]) + _dot(du, w_ref[D_FF:2 * D_FF])
        x1 = x1_ref[...]
        y = y_ref[...]
        g1 = g1_ref[...]
        dsc_ref[...] += _colsum(dh2 * x1)
        dsh_ref[...] += _colsum(dh2)
        dx1 = dh2 * (1.0 + sc_ref[...]) + ALPHA * dz2_ref[...]
        xhat, rstd = _ln_stats(ALPHA * x_ref[...] + g1 * y)
        dlg_ref[...] += _colsum(dx1 * xhat)
        dlb_ref[...] += _colsum(dx1)
        dz1 = _ln_bwd(dx1 * lg_ref[...], xhat, rstd)
        dg1_ref[...] += _colsum(dz1 * y)
        dz1_ref[...] = dz1
        dy_ref[...] = (g1 * dz1).astype(BF16)

    row = pl.BlockSpec((tm, d), lambda i: (i, 0))
    wide = pl.BlockSpec((tm, D_FF), lambda i: (i, 0))
    vec = _full((1, d))
    acc = _full((1, d))
    acc_shape = jax.ShapeDtypeStruct((1, d), F32)
    wide_shape = jax.ShapeDtypeStruct((t, D_FF), BF16)
    return _call(
        body, name="ffn_bwd", grid=(t // tm,),
        out_shape=[wide_shape, wide_shape, jax.ShapeDtypeStruct((t, d), F32), jax.ShapeDtypeStruct((t, d), BF16)]
        + [acc_shape] * 5,
        in_specs=[row, _resident((D_FF, d)), wide, wide, _resident((2 * D_FF, d)), row, row, row, row, vec, vec, vec],
        out_specs=[wide, wide, row, row, acc, acc, acc, acc, acc], sem=("arbitrary",), comm=comm,
        args=(dy2, w_down, dsu, sg, w_gu_t, x1, x, y, dz2, sc2, g1, ln1g))


def _din(dproj, dkvn, w_in_t, x, dz1, sc1, tm, comm):
    t, d = x.shape

    def body(dp_ref, dkv_ref, w_ref, x_ref, dz1_ref, sc_ref, dx_ref, dpb_ref, dbin_ref, dsc_ref, dsh_ref):
        @pl.when(pl.program_id(0) == 0)
        def _():
            for r in (dbin_ref, dsc_ref, dsh_ref):
                r[...] = jnp.zeros_like(r)

        dp = jnp.concatenate([dp_ref[:, 0:ATTN_W], dp_ref[:, ATTN_W:ATTN_W + 2 * KV_W] + dkv_ref[...],
                              dp_ref[:, ATTN_W + 2 * KV_W:IN_W]], axis=1)
        dbin_ref[...] += _colsum(dp)
        dpb = dp.astype(BF16)
        dpb_ref[...] = dpb
        dh = _dot(dpb, w_ref[...])
        dsc_ref[...] += _colsum(dh * x_ref[...])
        dsh_ref[...] += _colsum(dh)
        dx_ref[...] = dh * (1.0 + sc_ref[...]) + ALPHA * dz1_ref[...]

    row = lambda w: pl.BlockSpec((tm, w), lambda i: (i, 0))
    return _call(
        body, name="din", grid=(t // tm,),
        out_shape=[jax.ShapeDtypeStruct((t, d), F32), jax.ShapeDtypeStruct((t, IN_W), BF16),
                   jax.ShapeDtypeStruct((1, IN_W), F32), jax.ShapeDtypeStruct((1, d), F32),
                   jax.ShapeDtypeStruct((1, d), F32)],
        in_specs=[row(IN_W), row(2 * KV_W), _full((IN_W, d)), row(d), row(d), _full((1, d))],
        out_specs=[row(d), row(IN_W), _full((1, IN_W)), _full((1, d)), _full((1, d))],
        sem=("arbitrary",), comm=comm, args=(dproj, dkvn, w_in_t, x, dz1, sc1))


def _wgrad(name, a, b, tmm, tk, comm=None, a2=None):
    t, m = a.shape
    n = b.shape[1]
    nk = t // tk
    nm = m // tmm

    def body(*refs):
        a_refs, (b_ref, o_ref, acc_ref) = refs[:-3], refs[-3:]
        i, k = pl.program_id(0), pl.program_id(1)

        @pl.when(k == 0)
        def _():
            acc_ref[...] = jnp.zeros_like(acc_ref)

        a_tile = a_refs[0][...] if a2 is None else jnp.where(i < nm, a_refs[0][...], a_refs[1][...])
        acc_ref[...] += _dot_tn(a_tile, b_ref[...])

        @pl.when(k == nk - 1)
        def _():
            o_ref[...] = acc_ref[...].astype(BF16)

    if a2 is None:
        a_specs, a_args, n_tiles = [pl.BlockSpec((tk, tmm), lambda i, k: (k, i))], (a,), nm
    else:
        a_specs = [pl.BlockSpec((tk, tmm), lambda i, k: (jnp.where(i < nm, k, 0), jnp.minimum(i, nm - 1))),
                   pl.BlockSpec((tk, tmm), lambda i, k: (jnp.where(i < nm, 0, k), jnp.maximum(i - nm, 0)))]
        a_args, n_tiles = (a, a2), 2 * nm
    (out,), got = _call(
        body, name=name, grid=(n_tiles, nk), out_shape=[jax.ShapeDtypeStruct((n_tiles * tmm, n), BF16)],
        in_specs=a_specs + [pl.BlockSpec((tk, n), lambda i, k: (k, 0))],
        out_specs=[pl.BlockSpec((tmm, n), lambda i, k: (i, 0))],
        scratch_shapes=[pltpu.VMEM((tmm, n), F32)], sem=("parallel", "arbitrary"), comm=comm, args=a_args + (b,))
    return out if comm is None else (out, got)


def _adamw(w, g, m, v):
    m = ADAM_B1 * m + (1.0 - ADAM_B1) * g
    v = ADAM_B2 * v + (1.0 - ADAM_B2) * (g * g)
    m_hat = m / (1.0 - ADAM_B1 ** ADAM_STEP)
    v_hat = v / (1.0 - ADAM_B2 ** ADAM_STEP)
    delta = -ADAM_LR * (m_hat / (jnp.sqrt(v_hat) + ADAM_EPS) + ADAM_WD * w)
    return delta, m, v


def _adam_reduce(name, parts, w, m, v, tr):
    r, cdim = w.shape
    n_slots = parts.shape[0]

    def body(p_ref, w_ref, m_ref, v_ref, g_ref, d_ref, mo_ref, vo_ref):
        g = p_ref[0].astype(F32)
        for s in range(1, n_slots):
            g = g + p_ref[s].astype(F32)
        d_ref[...], mo_ref[...], vo_ref[...] = _adamw(w_ref[...], g, m_ref[...], v_ref[...])
        g_ref[...] = g

    tile = pl.BlockSpec((tr, cdim), lambda i: (i, 0))
    shp = jax.ShapeDtypeStruct((r, cdim), F32)
    return pl.pallas_call(
        body, name=name, grid=(r // tr,), out_shape=[shp] * 4,
        in_specs=[pl.BlockSpec((n_slots, tr, cdim), lambda i: (0, i, 0)), tile, tile, tile],
        out_specs=[tile] * 4, compiler_params=_params(("parallel",)),
    )(parts, w, m, v)


def _adam_w_ada(c_all_t, dmod_cols, w, m, v):
    def body(ct_ref, dm_ref, w_ref, m_ref, v_ref, g_ref, d_ref, mo_ref, vo_ref):
        ct = ct_ref[...]
        s = (ct * _sigmoid(ct)).astype(BF16)
        g = _dot(s, dm_ref[...].astype(BF16))
        d_ref[...], mo_ref[...], vo_ref[...] = _adamw(w_ref[...], g, m_ref[...], v_ref[...])
        g_ref[...] = g

    shp = jax.ShapeDtypeStruct(w.shape, F32)
    return pl.pallas_call(
        body, name="adam_w_ada", grid=(1,), out_shape=[shp] * 4,
        in_specs=[_full(c_all_t.shape), _full(dmod_cols.shape)] + [_full(w.shape)] * 3,
        out_specs=[_full(w.shape)] * 4, compiler_params=_params(("arbitrary",)),
    )(c_all_t, dmod_cols, w, m, v)


SMALL_EARLY = ["rel_bias", "attn_sinks", "gmlp_ln_g", "gmlp_ln_b", "gmlp_b_s",
               "attn_out_g", "gmlp_out_g", "ln1_g", "ln1_b", "ln2_g", "ln2_b"]
SMALL_LATE = ["b_ada", "b_in", "loss"]
WEIGHTS = ["rel_bias", "w_ada", "b_ada", "w_in", "b_in", "attn_sinks", "gmlp_ln_g", "gmlp_ln_b", "gmlp_w_s",
           "gmlp_b_s", "attn_out_g", "gmlp_out_g", "w_out", "ln1_g", "ln1_b", "w_gate_up", "w_down", "ln2_g", "ln2_b"]


def _seg_rows(nelem):
    return -(-nelem // 1024) * 8


def _pack(named, names):
    parts = []
    for name in names:
        flat = named[name].reshape(-1).astype(F32)
        rows = _seg_rows(flat.shape[0])
        parts.append(jnp.pad(flat, (0, rows * 128 - flat.shape[0])).reshape(rows, 128))
    return jnp.concatenate(parts, axis=0)


def _adam_small(name, parts, names, wts, mom_m, mom_v):
    params = [n for n in names if n in wts]

    def view(n):
        nelem = math.prod(wts[n].shape)
        return (nelem // 128, 128) if nelem % 128 == 0 else (1, nelem)

    offsets, r0 = {}, 0
    for n in names:
        offsets[n] = r0
        r0 += _seg_rows(math.prod(wts[n].shape) if n in wts else 1)

    def body(*refs):
        p_ref, ins, outs = refs[0], refs[1:1 + 3 * len(params)], refs[1 + 3 * len(params):]

        def total(n, rows, lanes):
            o = offsets[n]
            g = p_ref[0, o:o + rows, 0:lanes]
            for s in range(1, N_DEV):
                g = g + p_ref[s, o:o + rows, 0:lanes]
            return g

        for i, n in enumerate(params):
            g = total(n, *view(n))
            w_ref, m_ref, v_ref = ins[3 * i:3 * i + 3]
            g_ref, d_ref, mo_ref, vo_ref = outs[4 * i:4 * i + 4]
            d_ref[...], mo_ref[...], vo_ref[...] = _adamw(w_ref[...], g, m_ref[...], v_ref[...])
            g_ref[...] = g
        for j, n in enumerate(n for n in names if n not in wts):
            outs[4 * len(params) + j][...] = total(n, 8, 128)

    args, in_specs, out_shape = [parts], [_full(parts.shape)], []
    for n in params:
        args += [t[n].reshape(view(n)) for t in (wts, mom_m, mom_v)]
        in_specs += [_full(view(n))] * 3
        out_shape += [jax.ShapeDtypeStruct(view(n), F32)] * 4
    out_shape += [jax.ShapeDtypeStruct((8, 128), F32) for n in names if n not in wts]
    res = pl.pallas_call(
        body, name=name, grid=(1,), out_shape=out_shape, in_specs=in_specs,
        out_specs=[_full(s.shape) for s in out_shape], compiler_params=_params(("arbitrary",)),
    )(*args)
    done = {n: tuple(r.reshape(wts[n].shape) for r in res[4 * i:4 * i + 4]) for i, n in enumerate(params)}
    sums = {n: res[4 * len(params) + j] for j, n in enumerate(n for n in names if n not in wts)}
    return done, sums


def _t5_bucket_map():
    qi = jnp.arange(BLK)[:, None]
    si = jnp.arange(2 * BLK)[None, :]
    n = jnp.maximum(qi + BLK - si, 0)
    max_exact = N_BUCKETS // 2
    nf = jnp.maximum(n, max_exact).astype(F32)
    large = max_exact + (jnp.log(nf / max_exact) / math.log(MAX_DISTANCE / max_exact)
                         * (N_BUCKETS - max_exact)).astype(jnp.int32)
    large = jnp.minimum(large, N_BUCKETS - 1)
    return jnp.where(n < max_exact, n, large).astype(jnp.int32)


def kernel(x, c, rel_bias, w_ada, b_ada, w_in, b_in, attn_sinks, gmlp_ln_g, gmlp_ln_b, gmlp_w_s, gmlp_b_s, attn_out_g, gmlp_out_g, w_out, ln1_g, ln1_b, w_gate_up, w_down, ln2_g, ln2_b, loss_target, m_rel_bias, m_w_ada, m_b_ada, m_w_in, m_b_in, m_attn_sinks, m_gmlp_ln_g, m_gmlp_ln_b, m_gmlp_w_s, m_gmlp_b_s, m_attn_out_g, m_gmlp_out_g, m_w_out, m_ln1_g, m_ln1_b, m_w_gate_up, m_w_down, m_ln2_g, m_ln2_b, v_rel_bias, v_w_ada, v_b_ada, v_w_in, v_b_in, v_attn_sinks, v_gmlp_ln_g, v_gmlp_ln_b, v_gmlp_w_s, v_gmlp_b_s, v_attn_out_g, v_gmlp_out_g, v_w_out, v_ln1_g, v_ln1_b, v_w_gate_up, v_w_down, v_ln2_g, v_ln2_b):
    wts = dict(rel_bias=rel_bias, w_ada=w_ada, b_ada=b_ada, w_in=w_in, b_in=b_in, attn_sinks=attn_sinks,
               gmlp_ln_g=gmlp_ln_g, gmlp_ln_b=gmlp_ln_b, gmlp_w_s=gmlp_w_s, gmlp_b_s=gmlp_b_s,
               attn_out_g=attn_out_g, gmlp_out_g=gmlp_out_g, w_out=w_out, ln1_g=ln1_g, ln1_b=ln1_b,
               w_gate_up=w_gate_up, w_down=w_down, ln2_g=ln2_g, ln2_b=ln2_b)
    mom_m = dict(rel_bias=m_rel_bias, w_ada=m_w_ada, b_ada=m_b_ada, w_in=m_w_in, b_in=m_b_in,
                 attn_sinks=m_attn_sinks, gmlp_ln_g=m_gmlp_ln_g, gmlp_ln_b=m_gmlp_ln_b, gmlp_w_s=m_gmlp_w_s,
                 gmlp_b_s=m_gmlp_b_s, attn_out_g=m_attn_out_g, gmlp_out_g=m_gmlp_out_g, w_out=m_w_out,
                 ln1_g=m_ln1_g, ln1_b=m_ln1_b, w_gate_up=m_w_gate_up, w_down=m_w_down, ln2_g=m_ln2_g,
                 ln2_b=m_ln2_b)
    mom_v = dict(rel_bias=v_rel_bias, w_ada=v_w_ada, b_ada=v_b_ada, w_in=v_w_in, b_in=v_b_in,
                 attn_sinks=v_attn_sinks, gmlp_ln_g=v_gmlp_ln_g, gmlp_ln_b=v_gmlp_ln_b, gmlp_w_s=v_gmlp_w_s,
                 gmlp_b_s=v_gmlp_b_s, attn_out_g=v_attn_out_g, gmlp_out_g=v_gmlp_out_g, w_out=v_w_out,
                 ln1_g=v_ln1_g, ln1_b=v_ln1_b, w_gate_up=v_w_gate_up, w_down=v_w_down, ln2_g=v_ln2_g,
                 ln2_b=v_ln2_b)

    t = x.shape[1]
    tm = min(512, t)
    tn_ff = D_FF // 2
    tk_long, tk_short = min(4096, t), min(2048, t)
    me = 4 * lax.axis_index("x") + 2 * lax.axis_index("y") + lax.axis_index("c")
    xs = x[0]
    target = loss_target[0]

    (c_g,) = _exchange("gather_c", [jnp.broadcast_to(c, (8, D_MODEL))], ("gather",))
    c_all = c_g[:, 0, :]

    ncol = w_ada.shape[2]
    b_cols = lax.dynamic_slice(b_ada, (0, me * ncol), (1, ncol))
    mod_part = _mod_partial(c_all, w_ada[0], b_cols)
    bucket = _t5_bucket_map()
    (bias,), (mod_g, w_in_g) = _bias_table(rel_bias, bucket,
                                           comm=([mod_part, w_in[0].T.astype(BF16)], ("gather", "gather2")))
    w_in_t = w_in_g.reshape(IN_W, D_MODEL)
    mod = lax.dynamic_slice(mod_g, (0, me, 0), (N_DEV, 1, ncol)).reshape(1, N_DEV * ncol)
    sh1, sc1, g1, sh2, sc2, g2 = [mod[:, i * D_MODEL:(i + 1) * D_MODEL] for i in range(6)]

    causal = jnp.tril(jnp.ones((BLK, BLK), dtype=bool))
    ws = jnp.where(causal[None], gmlp_w_s[0], 0.0).astype(BF16)
    pair = lambda w: jnp.concatenate([w[0::2], w[1::2]], axis=2)
    ws2, wst2 = pair(ws), pair(jnp.swapaxes(ws, 1, 2))
    bfull = jnp.repeat(gmlp_b_s[0].T, GMLP_W // N_GROUPS, axis=1)
    sinks = attn_sinks[0]

    (proj, h1), (w_down_g,) = _inproj(xs, sc1, sh1, w_in_t, b_in, tm, comm=([w_down[0].astype(BF16)], ("gather2",)))
    (mixed, *kept), (w_out_g, w_gu_g) = _mix_fwd(
        proj, bias, sinks, gmlp_ln_g, gmlp_ln_b, ws2, bfull, attn_out_g, gmlp_out_g,
        comm=([w_out[0].astype(BF16), w_gate_up[0].T.astype(BF16)], ("gather2", "gather2")))
    w_out_f = w_out_g.reshape(D_MODEL, D_MODEL)
    w_gu_t = w_gu_g.reshape(2 * D_FF, D_MODEL)
    (y1, x1, h2), _ = _outproj(mixed, w_out_f, xs, g1, ln1_g, ln1_b, sc2, sh2, tm, comm=None)
    w_down_f = w_down_g.reshape(D_FF, D_MODEL)
    dsu, sg, act, dz2, dy2, loss_p, d_ln2g, d_ln2b, d_g2 = _ffn_fwd(h2, w_gu_t, w_down_f, x1, target, g2, ln2_g, ln2_b,
                                                                    min(256, t))

    slots = lambda a: a.reshape(N_DEV, -1, D_MODEL)
    dw_down = _wgrad("wgrad_down", act, dy2, tn_ff, tk_short)
    (dgate, dup, dz1, dy1, d_sc2, d_sh2, d_ln1g, d_ln1b, d_g1), (r_down,) = _ffn_bwd(
        dy2, w_down_f, dsu, sg, w_gu_t, x1, xs, y1, dz2, sc2, g1, ln1_g, min(256, t),
        comm=([slots(dw_down)], ("scatter",)))
    dw_gu_t = _wgrad("wgrad_gate_up", dgate, h2, tn_ff, tk_short, a2=dup)
    dw_out = _wgrad("wgrad_out", mixed, dy1, D_MODEL, tk_long)
    ((dproj, dkvn, dl_acc, dsink_acc, d_lng, d_lnb, d_ws, d_bs, d_aog, d_gog), (r_gu, r_out)) = _mix_bwd(
        proj, gmlp_ln_g, gmlp_ln_b, ws2, wst2, bfull, attn_out_g, gmlp_out_g, dy1, w_out_f.T, kept,
        comm=([slots(dw_gu_t), slots(dw_out)], ("scatter", "scatter")))
    d_relb = _bias_grad(dl_acc, bucket)

    rsum = lambda a: jnp.sum(a, axis=0)
    early_g = dict(
        rel_bias=d_relb[:, 0, :N_BUCKETS].T, attn_sinks=rsum(dsink_acc)[:N_HEADS],
        gmlp_ln_g=rsum(d_lng), gmlp_ln_b=rsum(d_lnb),
        gmlp_b_s=jnp.sum(d_bs.reshape(BLK, N_GROUPS, GMLP_W // N_GROUPS), axis=2).T,
        attn_out_g=rsum(d_aog), gmlp_out_g=rsum(d_gog), ln1_g=rsum(d_ln1g), ln1_b=rsum(d_ln1b),
        ln2_g=rsum(d_ln2g), ln2_b=rsum(d_ln2b))
    (grad_x, dproj_b, d_bin, d_sc1, d_sh1), _ = _din(dproj, dkvn, w_in_t, xs, dz1, sc1, tm, comm=None)
    ws_rows = lambda a: a.reshape(N_GROUPS * BLK, BLK)
    d_ws_b = ws_rows(jnp.where(causal[None], d_ws, 0.0)).astype(BF16)
    dw_in_t, (early_all, ws_all) = _wgrad("wgrad_in", dproj_b, h1, IN_W // 2, tk_long,
                                          comm=([_pack(early_g, SMALL_EARLY), d_ws_b], ("gather2", "gather2")))
    dmod = jnp.concatenate([rsum(d_sh1), rsum(d_sc1), rsum(d_g1), rsum(d_sh2), rsum(d_sc2), rsum(d_g2)])
    late_g = dict(b_ada=dmod, b_in=rsum(d_bin), loss=(0.5 / D_MODEL * jnp.sum(loss_p)).reshape(1))
    late_all, r_in = _scatter_two_level("scatter_in", _pack(late_g, SMALL_LATE), slots(dw_in_t))

    small, _ = _adam_small("adam_small_early", early_all, SMALL_EARLY, wts, mom_m, mom_v)
    small_late, sums = _adam_small("adam_small_late", late_all, SMALL_LATE, wts, mom_m, mom_v)
    small.update(small_late)
    small["gmlp_w_s"] = [o.reshape(gmlp_w_s.shape) for o in _adam_reduce(
        "adam_w_s", ws_all, ws_rows(gmlp_w_s), ws_rows(m_gmlp_w_s), ws_rows(v_gmlp_w_s), N_GROUPS * BLK // 2)]
    loss = sums["loss"][0, 0]

    dmod_all = late_all[:, :_seg_rows(6 * D_MODEL), :].reshape(N_DEV, 6 * D_MODEL)
    dmod_cols = lax.dynamic_slice(dmod_all, (0, me * ncol), (N_DEV, ncol))
    kpad = 128 - N_DEV
    ada = _adam_w_ada(jnp.pad(c_all.T, ((0, 0), (0, kpad))), jnp.pad(dmod_cols, ((0, kpad), (0, 0))),
                      w_ada[0], m_w_ada[0], v_w_ada[0])

    tr = lambda a: jnp.swapaxes(a, -1, -2)
    big = {}
    big["w_in"] = [tr(o)[None] for o in _adam_reduce("adam_w_in", r_in, w_in[0].T, m_w_in[0].T, v_w_in[0].T, 112)]
    big["w_out"] = [o[None] for o in _adam_reduce("adam_w_out", r_out, w_out[0], m_w_out[0], v_w_out[0], 128)]
    big["w_gate_up"] = [tr(o)[None] for o in _adam_reduce("adam_w_gu", r_gu, w_gate_up[0].T, m_w_gate_up[0].T,
                                                           v_w_gate_up[0].T, 352)]
    big["w_down"] = [o[None] for o in _adam_reduce("adam_w_down", r_down, w_down[0], m_w_down[0], v_w_down[0], 176)]
    big["w_ada"] = [o[None] for o in ada]

    outs = [[], [], [], []]
    for name in WEIGHTS:
        for i in range(4):
            outs[i].append(big[name][i] if name in big else small[name][i])
    return (loss, grad_x[None], *outs[0], *outs[1], *outs[2], *outs[3])
```

```python
import math

import jax
import jax.numpy as jnp
from jax import lax
from jax.experimental import pallas as pl
from jax.experimental.pallas import tpu as pltpu

F32 = jnp.float32
BF16 = jnp.bfloat16
MESH = pl.DeviceIdType.MESH

N_DEV = 8
D_MODEL = 1024
HEAD_DIM = 64
N_HEADS = 8
N_GROUPS = 8
ATTN_W = 512
KV_W = 128
GMLP_W = 512
IN_W = 1792
BLK = 128
N_BUCKETS = 32
MAX_DISTANCE = 128
D_FF = 2816
ALPHA = 2.0 ** 0.25
LN_EPS = 1e-5
NEG_INF = -1e30
ADAM_LR = 0.001
ADAM_B1 = 0.9
ADAM_B2 = 0.999
ADAM_EPS = 1e-08
ADAM_WD = 0.01
ADAM_STEP = 10
GELU_C0 = math.sqrt(2.0 / math.pi)
GELU_C1 = 0.044715

VMEM_LIMIT = 56 * 1024 * 1024


def _params(sem):
    return pltpu.CompilerParams(dimension_semantics=sem, vmem_limit_bytes=VMEM_LIMIT)


def _dot(a, b):
    return lax.dot_general(a, b, (((1,), (0,)), ((), ())), preferred_element_type=F32)


def _dot_nt(a, b):
    return lax.dot_general(a, b, (((1,), (1,)), ((), ())), preferred_element_type=F32)


def _dot_tn(a, b):
    return lax.dot_general(a, b, (((0,), (0,)), ((), ())), preferred_element_type=F32)


def _full(shape):
    nd = len(shape)
    return pl.BlockSpec(shape, lambda *_: (0,) * nd)


def _rowsum8(v):
    r, c = v.shape
    return jnp.sum(v.reshape(r // 8, 8, c), axis=0)


def _sigmoid(v):
    return 1.0 / (1.0 + jnp.exp(-v))


def _gelu_parts(v):
    v2 = v * v
    t = jnp.tanh(GELU_C0 * (v + GELU_C1 * v * v2))
    g = 0.5 * v * (1.0 + t)
    dg = 0.5 * (1.0 + t) + 0.5 * v * (1.0 - t * t) * (GELU_C0 * (1.0 + 3.0 * GELU_C1 * v2))
    return g, dg


def _ln_stats(z):
    mu = jnp.mean(z, axis=1, keepdims=True)
    zc = z - mu
    var = jnp.mean(zc * zc, axis=1, keepdims=True)
    rstd = lax.rsqrt(var + LN_EPS)
    return zc * rstd, rstd


def _ln_bwd(dxhat, xhat, rstd):
    m1 = jnp.mean(dxhat, axis=1, keepdims=True)
    m2 = jnp.mean(dxhat * xhat, axis=1, keepdims=True)
    return rstd * (dxhat - m1 - xhat * m2)


def _seg_mean64(v):
    r = v.shape[0]
    lo = lax.broadcasted_iota(jnp.int32, (r, 128), 1) < 64
    outs = []
    for j in range(v.shape[1] // 128):
        ch = v[:, 128 * j:128 * (j + 1)]
        s_lo = jnp.sum(jnp.where(lo, ch, 0.0), axis=1, keepdims=True)
        s_hi = jnp.sum(jnp.where(lo, 0.0, ch), axis=1, keepdims=True)
        outs.append(jnp.where(lo, s_lo, s_hi) * (1.0 / 64.0))
    return jnp.concatenate(outs, axis=1)


def _rms(a, g):
    r = lax.rsqrt(jnp.mean(a * a, axis=1, keepdims=True) + LN_EPS)
    return a * r * g, r


def _rms_bwd(dout, a, r, g):
    t = dout * g
    return r * t - a * (r * r * r) * jnp.mean(t * a, axis=1, keepdims=True)


PEER_ORDER = (1, 2, 4, 3, 5, 6, 7)


def _peer(j):
    x, y, c = lax.axis_index("x"), lax.axis_index("y"), lax.axis_index("c")
    px = 1 - x if j & 4 else x
    py = 1 - y if j & 2 else y
    pc = 1 - c if j & 1 else c
    return (px, py, pc), 4 * px + 2 * py + pc


SIBLING = 1
CHIP_FLIPS = (4, 2, 6)


def _exchange_phase(phase, ins, outs, modes, send_sems, recv_sems, loc_sems):
    me = 4 * lax.axis_index("x") + 2 * lax.axis_index("y") + lax.axis_index("c")
    for k, mode in enumerate(modes):
        def copy(i, src, slot, dev, k=k):
            return pltpu.make_async_remote_copy(src_ref=src, dst_ref=outs[k].at[slot], send_sem=send_sems.at[k, i],
                                                recv_sem=recv_sems.at[k, i], device_id=dev, device_id_type=MESH)

        src_me = ins[k].at[me] if mode == "scatter" else ins[k]
        local = pltpu.make_async_copy(src_me, outs[k].at[me], loc_sems.at[k])
        if mode == "gather2":
            sib_dev, sib_idx = _peer(SIBLING)
            chips = [_peer(j) for j in CHIP_FLIPS]
            far = [_peer(j | SIBLING)[1] for j in CHIP_FLIPS]
            if phase == "start":
                local.start()
                copy(0, ins[k], me, sib_dev).start()
                for i, (dev, _) in enumerate(chips):
                    copy(1 + i, ins[k], me, dev).start()
            elif phase == "mid":
                for i, (dev, idx) in enumerate(chips):
                    copy(1 + i, ins[k], idx, dev).wait_recv()
                    copy(4 + i, outs[k].at[idx], idx, sib_dev).start()
            else:
                copy(0, ins[k], sib_idx, sib_dev).wait_recv()
                for i, slot in enumerate(far):
                    copy(4 + i, ins[k], slot, sib_dev).wait_recv()
                copy(0, ins[k], me, sib_dev).wait_send()
                for i, (dev, idx) in enumerate(chips):
                    copy(1 + i, ins[k], me, dev).wait_send()
                    copy(4 + i, outs[k].at[idx], idx, sib_dev).wait_send()
                local.wait()
            continue
        peers = [_peer(j) for j in PEER_ORDER]
        if phase == "start":
            local.start()
            for i, (dev, idx) in enumerate(peers):
                copy(i, ins[k].at[idx] if mode == "scatter" else ins[k], me, dev).start()
        elif phase == "end":
            for i, (dev, idx) in enumerate(peers):
                copy(i, src_me, idx, dev).wait_recv()
            for i, (dev, idx) in enumerate(peers):
                copy(i, src_me, me, dev).wait_send()
            local.wait()


def _exchange_shapes(arrays, modes):
    return [jax.ShapeDtypeStruct((N_DEV,) + (a.shape[1:] if m == "scatter" else a.shape), a.dtype)
            for a, m in zip(arrays, modes)]


def _exchange_sems(n):
    return [pltpu.SemaphoreType.DMA((n, N_DEV - 1)), pltpu.SemaphoreType.DMA((n, N_DEV - 1)),
            pltpu.SemaphoreType.DMA((n,))]


def _exchange(name, arrays, modes):
    n = len(arrays)

    def body(*refs):
        for phase in ("start", "mid", "end"):
            _exchange_phase(phase, refs[:n], refs[n:2 * n], modes, *refs[2 * n:])

    any_spec = pl.BlockSpec(memory_space=pl.ANY)
    return pl.pallas_call(
        body, name=name, out_shape=_exchange_shapes(arrays, modes),
        in_specs=[any_spec] * n, out_specs=[any_spec] * n, scratch_shapes=_exchange_sems(n),
    )(*arrays)


N_CHIP = 4


def _scatter_two_level(name, pack, parts):
    r, ncols = parts.shape[1:]

    def body(pack_ref, parts_ref, late_ref, got_ref, sib_ref, h_ref, g_send, g_recv, g_loc, d_send, d_recv, i_send, i_recv):
        x, y, c = lax.axis_index("x"), lax.axis_index("y"), lax.axis_index("c")
        my_chip = 2 * x + y
        sib_dev, _ = _peer(SIBLING)
        gather = ([pack_ref], [late_ref], ("gather",), g_send, g_recv, g_loc)
        _exchange_phase("start", *gather)

        def to_sibling(q):
            return pltpu.make_async_remote_copy(src_ref=parts_ref.at[2 * q + 1 - c], dst_ref=sib_ref.at[q],
                                                send_sem=d_send.at[q], recv_sem=d_recv.at[q],
                                                device_id=sib_dev, device_id_type=MESH)

        for q in range(N_CHIP):
            to_sibling(q).start()
        for q in range(N_CHIP):
            to_sibling(q).wait_recv()
            h_ref[q] = (parts_ref[2 * q + c].astype(F32) + sib_ref[q].astype(F32)).astype(BF16)

        def to_chip(i, slot):
            dev, idx = _peer(CHIP_FLIPS[i])
            return pltpu.make_async_remote_copy(src_ref=h_ref.at[idx // 2], dst_ref=got_ref.at[slot],
                                                send_sem=i_send.at[i], recv_sem=i_recv.at[i],
                                                device_id=dev, device_id_type=MESH)

        for i in range(len(CHIP_FLIPS)):
            to_chip(i, my_chip).start()
        got_ref[my_chip] = h_ref[my_chip]
        for i in range(len(CHIP_FLIPS)):
            to_chip(i, _peer(CHIP_FLIPS[i])[1] // 2).wait_recv()
        for i in range(len(CHIP_FLIPS)):
            to_chip(i, my_chip).wait_send()
        for q in range(N_CHIP):
            to_sibling(q).wait_send()
        _exchange_phase("end", *gather)

    any_spec = pl.BlockSpec(memory_space=pl.ANY)
    vmem = pl.BlockSpec(memory_space=pltpu.VMEM)
    dma = pltpu.SemaphoreType.DMA
    return pl.pallas_call(
        body, name=name,
        out_shape=[jax.ShapeDtypeStruct((N_DEV,) + pack.shape, pack.dtype),
                   jax.ShapeDtypeStruct((N_CHIP, r, ncols), parts.dtype)],
        in_specs=[any_spec, vmem], out_specs=[any_spec, vmem],
        scratch_shapes=[pltpu.VMEM((N_CHIP, r, ncols), parts.dtype), pltpu.VMEM((N_CHIP, r, ncols), parts.dtype),
                        dma((1, N_DEV - 1)), dma((1, N_DEV - 1)), dma((1,)),
                        dma((N_CHIP,)), dma((N_CHIP,)), dma((len(CHIP_FLIPS),)), dma((len(CHIP_FLIPS),))],
        compiler_params=pltpu.CompilerParams(vmem_limit_bytes=VMEM_LIMIT),
    )(pack, parts)


def _call(body, *, name, grid, in_specs, out_specs, out_shape, args, sem, scratch_shapes=(), comm=None):
    if comm is None:
        outs = pl.pallas_call(body, name=name, grid=grid, in_specs=list(in_specs), out_specs=list(out_specs),
                              out_shape=list(out_shape), scratch_shapes=list(scratch_shapes),
                              compiler_params=_params(sem))(*args)
        return list(outs), []
    arrays, modes = comm
    n_in, n_out, nc, ns = len(in_specs), len(out_specs), len(arrays), len(scratch_shapes)
    n_steps = math.prod(grid)

    def hosted(*refs):
        ins, cins = refs[:n_in], refs[n_in:n_in + nc]
        outs, couts = refs[n_in + nc:n_in + nc + n_out], refs[n_in + nc + n_out:n_in + 2 * nc + n_out]
        scratch = refs[n_in + 2 * nc + n_out:]
        ex = (cins, couts, modes) + tuple(scratch[ns:])
        step = pl.program_id(0)
        for ax in range(1, len(grid)):
            step = step * grid[ax] + pl.program_id(ax)

        @pl.when(step == 0)
        def _():
            _exchange_phase("start", *ex)

        body(*ins, *outs, *scratch[:ns])

        if "gather2" in modes:
            @pl.when(step == (3 * n_steps) // 4)
            def _():
                _exchange_phase("mid", *ex)

        @pl.when(step == n_steps - 1)
        def _():
            _exchange_phase("end", *ex)

    any_spec = pl.BlockSpec(memory_space=pl.ANY)
    res = pl.pallas_call(
        hosted, name=name, grid=grid, in_specs=list(in_specs) + [any_spec] * nc,
        out_specs=list(out_specs) + [any_spec] * nc, out_shape=list(out_shape) + _exchange_shapes(arrays, modes),
        scratch_shapes=list(scratch_shapes) + _exchange_sems(nc),
        compiler_params=_params(tuple("arbitrary" for _ in grid)))(*args, *arrays)
    return list(res[:n_out]), list(res[n_out:])


def _mod_partial(c_all, w_ada, b_ada_cols):
    def body(c_ref, w_ref, b_ref, o_ref):
        cv = c_ref[...]
        s = (cv * _sigmoid(cv)).astype(BF16)
        o_ref[...] = _dot(s, w_ref[...].astype(BF16)) + b_ref[...]

    ncol = w_ada.shape[1]
    return pl.pallas_call(
        body, name="mod_partial", out_shape=jax.ShapeDtypeStruct((N_DEV, ncol), F32),
        in_specs=[_full(c_all.shape), _full(w_ada.shape), _full(b_ada_cols.shape)],
        out_specs=_full((N_DEV, ncol)), grid=(1,), compiler_params=_params(("arbitrary",)),
    )(c_all, w_ada, b_ada_cols)


def _bias_table(rel_bias, bucket, comm):
    def body(rb_ref, bk_ref, o_ref):
        h = pl.program_id(0)
        bk = bk_ref[...]
        acc = jnp.zeros((BLK, 2 * BLK), F32)
        for b in range(N_BUCKETS):
            acc = jnp.where(bk == b, rb_ref[b, h], acc)
        dist = (lax.broadcasted_iota(jnp.int32, (BLK, 2 * BLK), 0) + BLK
                - lax.broadcasted_iota(jnp.int32, (BLK, 2 * BLK), 1))
        o_ref[0] = jnp.where((dist >= 0) & (dist < BLK), acc, NEG_INF)

    return _call(
        body, name="bias_table", out_shape=[jax.ShapeDtypeStruct((N_HEADS, BLK, 2 * BLK), F32)],
        in_specs=[pl.BlockSpec(memory_space=pltpu.SMEM), _full((BLK, 2 * BLK))],
        out_specs=[pl.BlockSpec((1, BLK, 2 * BLK), lambda h: (h, 0, 0))], grid=(N_HEADS,),
        sem=("arbitrary",), comm=comm, args=(rel_bias, bucket))


def _bias_grad(dl_acc, bucket):
    def body(dl_ref, bk_ref, o_ref):
        bk = bk_ref[...]
        dl = dl_ref[0]
        lane = lax.broadcasted_iota(jnp.int32, (1, 128), 1)
        row = jnp.zeros((1, 128), F32)
        for b in range(N_BUCKETS):
            s = jnp.sum(jnp.sum(jnp.where(bk == b, dl, 0.0), axis=1, keepdims=True), axis=0, keepdims=True)
            row = jnp.where(lane == b, s, row)
        o_ref[0] = row

    return pl.pallas_call(
        body, name="bias_grad", out_shape=jax.ShapeDtypeStruct((N_HEADS, 1, 128), F32),
        in_specs=[pl.BlockSpec((1, BLK, 2 * BLK), lambda h: (h, 0, 0)), _full((BLK, 2 * BLK))],
        out_specs=pl.BlockSpec((1, 1, 128), lambda h: (h, 0, 0)), grid=(N_HEADS,),
        compiler_params=_params(("arbitrary",)),
    )(dl_acc, bucket)


def _inproj(x, sc1, sh1, w_in_t, b_in, tm, comm):
    t, d = x.shape
    n = w_in_t.shape[0]

    def body(x_ref, sc_ref, sh_ref, w_ref, b_ref, proj_ref, h_ref):
        h = (x_ref[...] * (1.0 + sc_ref[...]) + sh_ref[...]).astype(BF16)
        h_ref[...] = h
        proj_ref[...] = _dot_nt(h, w_ref[...]) + b_ref[...]

    row = lambda w: pl.BlockSpec((tm, w), lambda i: (i, 0))
    return _call(
        body, name="inproj", grid=(t // tm,),
        out_shape=[jax.ShapeDtypeStruct((t, n), F32), jax.ShapeDtypeStruct((t, d), BF16)],
        in_specs=[row(d), _full((1, d)), _full((1, d)), _full((n, d)), _full((1, n))],
        out_specs=[row(n), row(d)], sem=("parallel",), comm=comm, args=(x, sc1, sh1, w_in_t, b_in))


HALF = 64
ROWS = 32


def _lane_lo(rows):
    return lax.broadcasted_iota(jnp.int32, (rows, 128), 1) < 64


def _mix_stage_kv(proj_ref, kvp_ref, s):
    lo = _lane_lo(2 * BLK)
    for name, col in (("k", ATTN_W), ("v", ATTN_W + KV_W)):
        cur = jnp.concatenate([kvp_ref[:, col - ATTN_W:col - ATTN_W + KV_W], proj_ref[:, col:col + KV_W]], axis=0)
        plain, swapped = cur.astype(BF16), pltpu.roll(cur, 64, 1).astype(BF16)
        zero = jnp.zeros_like(plain)
        for g in range(2):
            dup = jnp.where(lo, plain, swapped) if g == 0 else jnp.where(lo, swapped, plain)
            s[name + "d"][g] = dup
            s[name + "m"][g] = jnp.concatenate([jnp.where(lo, dup, zero), jnp.where(lo, zero, dup)], axis=0)


def _group_rows(ref, g):
    return ref[4 * g:4 * g + 4].reshape(4 * BLK, ref.shape[2])


def _pair_rows(ref, g):
    return jnp.concatenate([jnp.concatenate([ref[4 * g + 2 * c], ref[4 * g + 2 * c + 1]], axis=1) for c in range(2)],
                           axis=0)


def _mask_heads(src_ref, dst_ref):
    lo = _lane_lo(BLK)
    for j in range(4):
        chunk = src_ref[:, 128 * j:128 * (j + 1)]
        dst_ref[2 * j] = jnp.where(lo, chunk, 0.0).astype(BF16)
        dst_ref[2 * j + 1] = jnp.where(lo, 0.0, chunk).astype(BF16)


def _mix_stage_attn(proj_ref, bias_ref, sinks_ref, n, s):
    _mask_heads(proj_ref, s["qm"])
    for g in range(2):
        s["lg"][g] = _dot_nt(_group_rows(s["qm"], g), s["kd"][g])
    n0mask = (n == 0) & (lax.broadcasted_iota(jnp.int32, (HALF, 2 * BLK), 1) < BLK)
    lane = lax.broadcasted_iota(jnp.int32, (HALF, 128), 1)
    for hf in range(BLK // HALF):
        rows = slice(HALF * hf, HALF * (hf + 1))
        psink = jnp.zeros((HALF, 128), F32)
        for h in range(N_HEADS):
            sk = sinks_ref[h]
            grows = slice(BLK * (h % 4) + HALF * hf, BLK * (h % 4) + HALF * (hf + 1))
            logit = s["lg"][h // 4, grows, :] * (HEAD_DIM ** -0.5) + bias_ref[h, rows, :]
            logit = jnp.where(n0mask, NEG_INF, logit)
            m = jnp.maximum(jnp.max(logit, axis=1, keepdims=True), sk)
            e = jnp.exp(logit - m)
            es = jnp.exp(sk - m)
            inv = 1.0 / (jnp.sum(e, axis=1, keepdims=True) + es)
            p = e * inv
            s["p"][h, rows, :] = p
            s["pb"][h, rows, :] = p.astype(BF16)
            psink = jnp.where(lane == h, es * inv, psink)
        s["psink"][rows, :] = psink
    for g in range(2):
        out = _dot(_pair_rows(s["pb"], g), s["vm"][g])
        s["attn"][:, 256 * g:256 * g + 128] = out[0:BLK]
        s["attn"][:, 256 * g + 128:256 * g + 256] = out[BLK:2 * BLK]


def _mix_stage_gmlp_pre(proj_ref, lng, lnb, s, keep):
    c0 = ATTN_W + 2 * KV_W
    for r0 in range(0, BLK, ROWS):
        rows = slice(r0, r0 + ROWS)
        u, du = _gelu_parts(proj_ref[rows, c0:c0 + GMLP_W])
        a, da = _gelu_parts(proj_ref[rows, c0 + GMLP_W:c0 + 2 * GMLP_W])
        ac = a - _seg_mean64(a)
        rstd = lax.rsqrt(_seg_mean64(ac * ac) + LN_EPS)
        vhat = ac * rstd
        s["u"][rows, :] = u
        s["vnb"][rows, :] = (vhat * lng + lnb).astype(BF16)
        if keep:
            s["du"][rows, :] = du
            s["da"][rows, :] = da
            s["vhat"][rows, :] = vhat
            s["rstd"][rows, :] = rstd


def _stack_halves(chunk):
    lo = _lane_lo(BLK)
    zero = jnp.zeros_like(chunk)
    return jnp.concatenate([jnp.where(lo, chunk, zero), jnp.where(lo, zero, chunk)], axis=0)


def _mix_stage_gmlp_mix(ws2_ref, bfull_ref, s):
    for j in range(4):
        cols = slice(128 * j, 128 * (j + 1))
        s["ms"][:, cols] = _dot(ws2_ref[j], _stack_halves(s["vnb"][:, cols])) + bfull_ref[:, cols]


def _mix_scratch(keep):
    f32 = lambda *shape: pltpu.VMEM(shape, F32)
    b16 = lambda *shape: pltpu.VMEM(shape, BF16)
    names = dict(kd=b16(2, 2 * BLK, 128), vd=b16(2, 2 * BLK, 128), km=b16(2, 4 * BLK, 128), vm=b16(2, 4 * BLK, 128),
                 qm=b16(N_HEADS, BLK, 128), lg=f32(2, 4 * BLK, 2 * BLK), pb=b16(N_HEADS, BLK, 2 * BLK),
                 u=f32(BLK, GMLP_W), vnb=b16(BLK, GMLP_W), ms=f32(BLK, GMLP_W))
    if keep:
        names.update(dom=b16(N_HEADS, BLK, 128), dls=b16(N_HEADS, BLK, 2 * BLK),
                     dattn=f32(BLK, ATTN_W), dmix=f32(BLK, D_MODEL), du=f32(BLK, GMLP_W), da=f32(BLK, GMLP_W),
                     vhat=f32(BLK, GMLP_W), rstd=f32(BLK, GMLP_W), dmsb=b16(BLK, GMLP_W), dvn=f32(BLK, GMLP_W))
    return list(names), list(names.values())


def _mix_specs(with_logit_inputs):
    logit_inputs = [_full((N_HEADS, BLK, 2 * BLK)), pl.BlockSpec(memory_space=pltpu.SMEM)] if with_logit_inputs else []
    return [pl.BlockSpec((BLK, IN_W), lambda n: (n, 0)),
            pl.BlockSpec((BLK, 2 * KV_W), lambda n: (jnp.maximum(n - 1, 0), ATTN_W // (2 * KV_W)))] + logit_inputs + [
            _full((1, GMLP_W)), _full((1, GMLP_W)),
            _full((N_GROUPS // 2, BLK, 2 * BLK)), _full((BLK, GMLP_W)),
            _full((1, ATTN_W)), _full((1, GMLP_W))]


KEPT = [("p", (N_HEADS, BLK, 2 * BLK), F32), ("psink", (BLK, 128), F32), ("attn", (BLK, ATTN_W), F32)]


def _kept_shapes(t):
    full = lambda blk: (blk[0], t, blk[2]) if len(blk) == 3 else (t, blk[1])
    return [jax.ShapeDtypeStruct(full(blk), dt) for _, blk, dt in KEPT]


def _kept_specs():
    return [pl.BlockSpec(blk, (lambda n: (0, n, 0)) if len(blk) == 3 else (lambda n: (n, 0))) for _, blk, _ in KEPT]


def _mix_fwd(proj, bias, sinks, lng, lnb, ws2, bfull, aog, gog, comm):
    t = proj.shape[0]
    names, shapes = _mix_scratch(False)

    def body(proj_ref, kvp_ref, bias_ref, sinks_ref, lng_ref, lnb_ref, ws2_ref, bfull_ref, aog_ref, gog_ref,
             out_ref, *rest):
        s = dict(zip([name for name, _, _ in KEPT] + names, rest))
        n = pl.program_id(0)
        _mix_stage_kv(proj_ref, kvp_ref, s)
        _mix_stage_attn(proj_ref, bias_ref, sinks_ref, n, s)
        _mix_stage_gmlp_pre(proj_ref, lng_ref[...], lnb_ref[...], s, False)
        _mix_stage_gmlp_mix(ws2_ref, bfull_ref, s)
        for r0 in range(0, BLK, ROWS):
            rows = slice(r0, r0 + ROWS)
            out_ref[rows, 0:ATTN_W] = _rms(s["attn"][rows, :], aog_ref[...])[0].astype(BF16)
            out_ref[rows, ATTN_W:ATTN_W + GMLP_W] = _rms(s["u"][rows, :] * s["ms"][rows, :], gog_ref[...])[0].astype(BF16)

    return _call(
        body, name="mix_fwd", grid=(t // BLK,),
        out_shape=[jax.ShapeDtypeStruct((t, D_MODEL), BF16)] + _kept_shapes(t),
        in_specs=_mix_specs(True), out_specs=[pl.BlockSpec((BLK, D_MODEL), lambda n: (n, 0))] + _kept_specs(),
        scratch_shapes=shapes,
        sem=("parallel",), comm=comm, args=(proj, proj, bias, sinks, lng, lnb, ws2, bfull, aog, gog))


def _mix_bwd(proj, lng, lnb, ws2, wst2, bfull, aog, gog, dy, w_out, kept, comm):
    t = proj.shape[0]
    nb = t // BLK
    names, shapes = _mix_scratch(True)
    c_gu = ATTN_W + 2 * KV_W

    def body(proj_ref, kvp_ref, lng_ref, lnb_ref, ws2_ref, bfull_ref, aog_ref, gog_ref,
             wst2_ref, dy_ref, wout_ref, *rest):
        n_kept = len(KEPT)
        s = dict(zip([name for name, _, _ in KEPT], rest[:n_kept]))
        (dproj_ref, dkvn_ref, dl_ref, dsink_ref, dlng_ref, dlnb_ref, dws_ref, dbs_ref, daog_ref,
         dgog_ref) = rest[n_kept:n_kept + 10]
        s.update(zip(names, rest[n_kept + 10:]))
        n = pl.program_id(0)

        @pl.when(n == 0)
        def _():
            for r in (dl_ref, dsink_ref, dlng_ref, dlnb_ref, dws_ref, dbs_ref, daog_ref, dgog_ref):
                r[...] = jnp.zeros_like(r)

        s["dmix"][...] = _dot(dy_ref[...], wout_ref[...])
        _mix_stage_kv(proj_ref, kvp_ref, s)
        _mask_heads(proj_ref, s["qm"])
        lng = lng_ref[...]
        _mix_stage_gmlp_pre(proj_ref, lng, lnb_ref[...], s, True)
        _mix_stage_gmlp_mix(ws2_ref, bfull_ref, s)

        aog, gog = aog_ref[...], gog_ref[...]
        for r0 in range(0, BLK, ROWS):
            rows = slice(r0, r0 + ROWS)
            attn, dma = s["attn"][rows, :], s["dmix"][rows, 0:ATTN_W]
            _, r_a = _rms(attn, aog)
            daog_ref[...] += _rowsum8(dma * attn * r_a)
            s["dattn"][rows, :] = _rms_bwd(dma, attn, r_a, aog)
            u, ms, dmg = s["u"][rows, :], s["ms"][rows, :], s["dmix"][rows, ATTN_W:ATTN_W + GMLP_W]
            gm = u * ms
            _, r_g = _rms(gm, gog)
            dgog_ref[...] += _rowsum8(dmg * gm * r_g)
            dgm = _rms_bwd(dmg, gm, r_g, gog)
            dproj_ref[rows, c_gu:c_gu + GMLP_W] = dgm * ms * s["du"][rows, :]
            dms = dgm * u
            dbs_ref[rows, :] += dms
            s["dmsb"][rows, :] = dms.astype(BF16)

        _mask_heads(s["dattn"], s["dom"])
        for g in range(2):
            s["lg"][g] = _dot_nt(_group_rows(s["dom"], g), s["vd"][g])
        lane = lax.broadcasted_iota(jnp.int32, (HALF, 128), 1)
        for hf in range(BLK // HALF):
            rows = slice(HALF * hf, HALF * (hf + 1))
            dsink = jnp.zeros((HALF, 128), F32)
            for h in range(N_HEADS):
                grows = slice(BLK * (h % 4) + HALF * hf, BLK * (h % 4) + HALF * (hf + 1))
                dp = s["lg"][h // 4, grows, :]
                p = s["p"][h, rows, :]
                s["pb"][h, rows, :] = p.astype(BF16)
                rs = jnp.sum(p * dp, axis=1, keepdims=True)
                dl = p * (dp - rs)
                dl_ref[h, rows, :] += dl
                dsink = dsink + jnp.where(lane == h, -s["psink"][rows, :] * rs, 0.0)
                s["dls"][h, rows, :] = (dl * (HEAD_DIM ** -0.5)).astype(BF16)
            dsink_ref[rows, :] += dsink
        for g in range(2):
            dq = _dot(_pair_rows(s["dls"], g), s["km"][g])
            dproj_ref[:, 256 * g:256 * g + 128] = dq[0:BLK]
            dproj_ref[:, 256 * g + 128:256 * g + 256] = dq[BLK:2 * BLK]
        lo_k = _lane_lo(2 * BLK)
        for col, lhs, rhs in ((0, "dls", "qm"), (KV_W, "pb", "dom")):
            raw = [_dot_tn(_group_rows(s[lhs], g), _group_rows(s[rhs], g)) for g in range(2)]
            both = [r + pltpu.roll(r, 64, 1) for r in raw]
            dkv = jnp.where(lo_k, both[0], both[1])
            dproj_ref[:, ATTN_W + col:ATTN_W + col + KV_W] = dkv[BLK:2 * BLK]
            dkvn_ref[:, col:col + KV_W] = dkv[0:BLK]

        for j in range(4):
            cols = slice(128 * j, 128 * (j + 1))
            dm2 = _stack_halves(s["dmsb"][:, cols])
            vnb = s["vnb"][:, cols]
            dws2 = _dot_nt(dm2, vnb)
            dws_ref[2 * j] += dws2[0:BLK]
            dws_ref[2 * j + 1] += dws2[BLK:2 * BLK]
            s["dvn"][:, cols] = _dot(wst2_ref[j], dm2)
        for r0 in range(0, BLK, ROWS):
            rows = slice(r0, r0 + ROWS)
            dvn, vhat = s["dvn"][rows, :], s["vhat"][rows, :]
            dlng_ref[...] += _rowsum8(dvn * vhat)
            dlnb_ref[...] += _rowsum8(dvn)
            dvh = dvn * lng
            dact = s["rstd"][rows, :] * (dvh - _seg_mean64(dvh) - vhat * _seg_mean64(dvh * vhat))
            dproj_ref[rows, c_gu + GMLP_W:IN_W] = dact * s["da"][rows, :]

    acc8 = lambda w: jax.ShapeDtypeStruct((8, w), F32)
    out_shape = [jax.ShapeDtypeStruct((t, IN_W), F32), jax.ShapeDtypeStruct((t, 2 * KV_W), F32),
                 jax.ShapeDtypeStruct((N_HEADS, BLK, 2 * BLK), F32), jax.ShapeDtypeStruct((BLK, 128), F32),
                 acc8(GMLP_W), acc8(GMLP_W), jax.ShapeDtypeStruct((N_GROUPS, BLK, BLK), F32),
                 jax.ShapeDtypeStruct((BLK, GMLP_W), F32), acc8(ATTN_W), acc8(GMLP_W)]
    out_specs = [pl.BlockSpec((BLK, IN_W), lambda n: (n, 0)),
                 pl.BlockSpec((BLK, 2 * KV_W), lambda n: ((n + nb - 1) % nb, 0)),
                 _full((N_HEADS, BLK, 2 * BLK)), _full((BLK, 128)), _full((8, GMLP_W)), _full((8, GMLP_W)),
                 _full((N_GROUPS, BLK, BLK)), _full((BLK, GMLP_W)), _full((8, ATTN_W)), _full((8, GMLP_W))]
    in_specs = _mix_specs(False) + [_full((N_GROUPS // 2, BLK, 2 * BLK)),
                               pl.BlockSpec((BLK, D_MODEL), lambda n: (n, 0)),
                               _full((D_MODEL, D_MODEL))] + _kept_specs()
    return _call(
        body, name="mix_bwd", grid=(nb,), out_shape=out_shape, in_specs=in_specs, out_specs=out_specs,
        scratch_shapes=shapes, sem=("arbitrary",), comm=comm,
        args=(proj, proj, lng, lnb, ws2, bfull, aog, gog, wst2, dy, w_out, *kept))


def _outproj(mixed, w_out, x, g1, ln1g, ln1b, tm, comm):
    t, d = x.shape

    def body(mx_ref, w_ref, x_ref, g1_ref, lg_ref, lb_ref, y_ref, x1_ref):
        y = _dot(mx_ref[...], w_ref[...])
        xhat, _ = _ln_stats(ALPHA * x_ref[...] + g1_ref[...] * y)
        y_ref[...] = y
        x1_ref[...] = xhat * lg_ref[...] + lb_ref[...]

    row = pl.BlockSpec((tm, d), lambda i: (i, 0))
    vec = _full((1, d))
    return _call(
        body, name="outproj", grid=(t // tm,),
        out_shape=[jax.ShapeDtypeStruct((t, d), F32), jax.ShapeDtypeStruct((t, d), F32)],
        in_specs=[row, _full((d, d)), row, vec, vec, vec], out_specs=[row, row],
        sem=("parallel",), comm=comm, args=(mixed, w_out, x, g1, ln1g, ln1b))


def _ffn_fwd(w_gu_t, w_down, x1, target, sc2, sh2, g2, ln2g, ln2b, tm):
    t, d = x1.shape

    def body(w_ref, wd_ref, x1_ref, tg_ref, sc_ref, sh_ref, g2_ref, lg_ref, lb_ref,
             h_ref, dsu_ref, sg_ref, act_ref, dz_ref, dy_ref, loss_ref, dlg_ref, dlb_ref, dg2_ref):
        @pl.when(pl.program_id(0) == 0)
        def _():
            for r in (loss_ref, dlg_ref, dlb_ref, dg2_ref):
                r[...] = jnp.zeros_like(r)

        h = (x1_ref[...] * (1.0 + sc_ref[...]) + sh_ref[...]).astype(BF16)
        h_ref[...] = h
        g = _dot_nt(h, w_ref[0:D_FF])
        u = _dot_nt(h, w_ref[D_FF:2 * D_FF])
        s = _sigmoid(g)
        sg = g * s
        act = (sg * u).astype(BF16)
        dsu_ref[...] = (u * (s * (1.0 + g * (1.0 - s)))).astype(BF16)
        sg_ref[...] = sg.astype(BF16)
        act_ref[...] = act
        y2 = _dot(act, wd_ref[...])
        g2 = g2_ref[...]
        lg = lg_ref[...]
        xhat, rstd = _ln_stats(ALPHA * x1_ref[...] + g2 * y2)
        err = xhat * lg + lb_ref[...] - tg_ref[...]
        loss_ref[...] += _rowsum8(err * err)
        dx2 = err * (1.0 / d)
        dlg_ref[...] += _rowsum8(dx2 * xhat)
        dlb_ref[...] += _rowsum8(dx2)
        dz = _ln_bwd(dx2 * lg, xhat, rstd)
        dg2_ref[...] += _rowsum8(dz * y2)
        dz_ref[...] = dz
        dy_ref[...] = (g2 * dz).astype(BF16)

    row = pl.BlockSpec((tm, d), lambda i: (i, 0))
    wide = pl.BlockSpec((tm, D_FF), lambda i: (i, 0))
    vec = _full((1, d))
    acc = _full((8, d))
    acc_shape = jax.ShapeDtypeStruct((8, d), F32)
    wide_shape = jax.ShapeDtypeStruct((t, D_FF), BF16)
    return pl.pallas_call(
        body, name="ffn_fwd", grid=(t // tm,),
        out_shape=[jax.ShapeDtypeStruct((t, d), BF16)] + [wide_shape] * 3
        + [jax.ShapeDtypeStruct((t, d), F32), jax.ShapeDtypeStruct((t, d), BF16)] + [acc_shape] * 4,
        in_specs=[_resident((2 * D_FF, d)), _resident((D_FF, d)), row, row, vec, vec, vec, vec, vec],
        out_specs=[row] + [wide] * 3 + [row, row, acc, acc, acc, acc], compiler_params=_params(("arbitrary",)),
    )(w_gu_t, w_down, x1, target, sc2, sh2, g2, ln2g, ln2b)


def _resident(shape):
    nd = len(shape)
    return pl.BlockSpec(shape, lambda *_: (0,) * nd, pipeline_mode=pl.Buffered(1))


def _ffn_bwd(dy2, w_down, dsu, sg, w_gu_t, x1, x, y, dz2, sc2, g1, ln1g, tm, comm):
    t, d = x1.shape

    def body(dy2_ref, wd_ref, dsu_ref, sg_ref, w_ref, x1_ref, x_ref, y_ref, dz2_ref, sc_ref, g1_ref, lg_ref,
             dg_ref, du_ref, dz1_ref, dy_ref, dsc_ref, dsh_ref, dlg_ref, dlb_ref, dg1_ref):
        @pl.when(pl.program_id(0) == 0)
        def _():
            for r in (dsc_ref, dsh_ref, dlg_ref, dlb_ref, dg1_ref):
                r[...] = jnp.zeros_like(r)

        dact = _dot_nt(dy2_ref[...], wd_ref[...])
        dg = (dact * dsu_ref[...].astype(F32)).astype(BF16)
        du = (dact * sg_ref[...].astype(F32)).astype(BF16)
        dg_ref[...] = dg
        du_ref[...] = du
        dh2 = _dot(dg, w_ref[0:D_FF]) + _dot(du, w_ref[D_FF:2 * D_FF])
        x1 = x1_ref[...]
        y = y_ref[...]
        g1 = g1_ref[...]
        dsc_ref[...] += _rowsum8(dh2 * x1)
        dsh_ref[...] += _rowsum8(dh2)
        dx1 = dh2 * (1.0 + sc_ref[...]) + ALPHA * dz2_ref[...]
        xhat, rstd = _ln_stats(ALPHA * x_ref[...] + g1 * y)
        dlg_ref[...] += _rowsum8(dx1 * xhat)
        dlb_ref[...] += _rowsum8(dx1)
        dz1 = _ln_bwd(dx1 * lg_ref[...], xhat, rstd)
        dg1_ref[...] += _rowsum8(dz1 * y)
        dz1_ref[...] = dz1
        dy_ref[...] = (g1 * dz1).astype(BF16)

    row = pl.BlockSpec((tm, d), lambda i: (i, 0))
    wide = pl.BlockSpec((tm, D_FF), lambda i: (i, 0))
    vec = _full((1, d))
    acc = _full((8, d))
    acc_shape = jax.ShapeDtypeStruct((8, d), F32)
    wide_shape = jax.ShapeDtypeStruct((t, D_FF), BF16)
    return _call(
        body, name="ffn_bwd", grid=(t // tm,),
        out_shape=[wide_shape, wide_shape, jax.ShapeDtypeStruct((t, d), F32), jax.ShapeDtypeStruct((t, d), BF16)]
        + [acc_shape] * 5,
        in_specs=[row, _resident((D_FF, d)), wide, wide, _resident((2 * D_FF, d)), row, row, row, row, vec, vec, vec],
        out_specs=[wide, wide, row, row, acc, acc, acc, acc, acc], sem=("arbitrary",), comm=comm,
        args=(dy2, w_down, dsu, sg, w_gu_t, x1, x, y, dz2, sc2, g1, ln1g))


def _din(dproj, dkvn, w_in_t, x, dz1, sc1, tm, comm):
    t, d = x.shape

    def body(dp_ref, dkv_ref, w_ref, x_ref, dz1_ref, sc_ref, dx_ref, dpb_ref, dbin_ref, dsc_ref, dsh_ref):
        @pl.when(pl.program_id(0) == 0)
        def _():
            for r in (dbin_ref, dsc_ref, dsh_ref):
                r[...] = jnp.zeros_like(r)

        dp = jnp.concatenate([dp_ref[:, 0:ATTN_W], dp_ref[:, ATTN_W:ATTN_W + 2 * KV_W] + dkv_ref[...],
                              dp_ref[:, ATTN_W + 2 * KV_W:IN_W]], axis=1)
        dbin_ref[...] += _rowsum8(dp)
        dpb = dp.astype(BF16)
        dpb_ref[...] = dpb
        dh = _dot(dpb, w_ref[...])
        dsc_ref[...] += _rowsum8(dh * x_ref[...])
        dsh_ref[...] += _rowsum8(dh)
        dx_ref[...] = dh * (1.0 + sc_ref[...]) + ALPHA * dz1_ref[...]

    row = lambda w: pl.BlockSpec((tm, w), lambda i: (i, 0))
    return _call(
        body, name="din", grid=(t // tm,),
        out_shape=[jax.ShapeDtypeStruct((t, d), F32), jax.ShapeDtypeStruct((t, IN_W), BF16),
                   jax.ShapeDtypeStruct((8, IN_W), F32), jax.ShapeDtypeStruct((8, d), F32),
                   jax.ShapeDtypeStruct((8, d), F32)],
        in_specs=[row(IN_W), row(2 * KV_W), _full((IN_W, d)), row(d), row(d), _full((1, d))],
        out_specs=[row(d), row(IN_W), _full((8, IN_W)), _full((8, d)), _full((8, d))],
        sem=("arbitrary",), comm=comm, args=(dproj, dkvn, w_in_t, x, dz1, sc1))


def _wgrad(name, a, b, tmm, tk, comm=None, a2=None):
    t, m = a.shape
    n = b.shape[1]
    nk = t // tk
    nm = m // tmm

    def body(*refs):
        a_refs, (b_ref, o_ref, acc_ref) = refs[:-3], refs[-3:]
        i, k = pl.program_id(0), pl.program_id(1)

        @pl.when(k == 0)
        def _():
            acc_ref[...] = jnp.zeros_like(acc_ref)

        a_tile = a_refs[0][...] if a2 is None else jnp.where(i < nm, a_refs[0][...], a_refs[1][...])
        acc_ref[...] += _dot_tn(a_tile, b_ref[...])

        @pl.when(k == nk - 1)
        def _():
            o_ref[...] = acc_ref[...].astype(BF16)

    if a2 is None:
        a_specs, a_args, n_tiles = [pl.BlockSpec((tk, tmm), lambda i, k: (k, i))], (a,), nm
    else:
        a_specs = [pl.BlockSpec((tk, tmm), lambda i, k: (jnp.where(i < nm, k, 0), jnp.minimum(i, nm - 1))),
                   pl.BlockSpec((tk, tmm), lambda i, k: (jnp.where(i < nm, 0, k), jnp.maximum(i - nm, 0)))]
        a_args, n_tiles = (a, a2), 2 * nm
    (out,), got = _call(
        body, name=name, grid=(n_tiles, nk), out_shape=[jax.ShapeDtypeStruct((n_tiles * tmm, n), BF16)],
        in_specs=a_specs + [pl.BlockSpec((tk, n), lambda i, k: (k, 0))],
        out_specs=[pl.BlockSpec((tmm, n), lambda i, k: (i, 0))],
        scratch_shapes=[pltpu.VMEM((tmm, n), F32)], sem=("parallel", "arbitrary"), comm=comm, args=a_args + (b,))
    return out if comm is None else (out, got)


def _adamw(w, g, m, v):
    m = ADAM_B1 * m + (1.0 - ADAM_B1) * g
    v = ADAM_B2 * v + (1.0 - ADAM_B2) * (g * g)
    m_hat = m / (1.0 - ADAM_B1 ** ADAM_STEP)
    v_hat = v / (1.0 - ADAM_B2 ** ADAM_STEP)
    delta = -ADAM_LR * (m_hat / (jnp.sqrt(v_hat) + ADAM_EPS) + ADAM_WD * w)
    return delta, m, v


def _adam_reduce(name, parts, w, m, v, tr):
    r, cdim = w.shape
    n_slots = parts.shape[0]

    def body(p_ref, w_ref, m_ref, v_ref, g_ref, d_ref, mo_ref, vo_ref):
        g = p_ref[0].astype(F32)
        for s in range(1, n_slots):
            g = g + p_ref[s].astype(F32)
        d_ref[...], mo_ref[...], vo_ref[...] = _adamw(w_ref[...], g, m_ref[...], v_ref[...])
        g_ref[...] = g

    tile = pl.BlockSpec((tr, cdim), lambda i: (i, 0))
    shp = jax.ShapeDtypeStruct((r, cdim), F32)
    return pl.pallas_call(
        body, name=name, grid=(r // tr,), out_shape=[shp] * 4,
        in_specs=[pl.BlockSpec((n_slots, tr, cdim), lambda i: (0, i, 0)), tile, tile, tile],
        out_specs=[tile] * 4, compiler_params=_params(("parallel",)),
    )(parts, w, m, v)


def _adam_w_ada(c_all_t, dmod_cols, w, m, v):
    def body(ct_ref, dm_ref, w_ref, m_ref, v_ref, g_ref, d_ref, mo_ref, vo_ref):
        ct = ct_ref[...]
        s = (ct * _sigmoid(ct)).astype(BF16)
        g = _dot(s, dm_ref[...].astype(BF16))
        d_ref[...], mo_ref[...], vo_ref[...] = _adamw(w_ref[...], g, m_ref[...], v_ref[...])
        g_ref[...] = g

    shp = jax.ShapeDtypeStruct(w.shape, F32)
    return pl.pallas_call(
        body, name="adam_w_ada", grid=(1,), out_shape=[shp] * 4,
        in_specs=[_full(c_all_t.shape), _full(dmod_cols.shape)] + [_full(w.shape)] * 3,
        out_specs=[_full(w.shape)] * 4, compiler_params=_params(("arbitrary",)),
    )(c_all_t, dmod_cols, w, m, v)


SMALL_EARLY = ["rel_bias", "attn_sinks", "gmlp_ln_g", "gmlp_ln_b", "gmlp_b_s",
               "attn_out_g", "gmlp_out_g", "ln1_g", "ln1_b", "ln2_g", "ln2_b"]
SMALL_LATE = ["b_ada", "b_in", "loss"]
WEIGHTS = ["rel_bias", "w_ada", "b_ada", "w_in", "b_in", "attn_sinks", "gmlp_ln_g", "gmlp_ln_b", "gmlp_w_s",
           "gmlp_b_s", "attn_out_g", "gmlp_out_g", "w_out", "ln1_g", "ln1_b", "w_gate_up", "w_down", "ln2_g", "ln2_b"]


def _seg_rows(nelem):
    return -(-nelem // 1024) * 8


def _pack(named, names):
    parts = []
    for name in names:
        flat = named[name].reshape(-1).astype(F32)
        rows = _seg_rows(flat.shape[0])
        parts.append(jnp.pad(flat, (0, rows * 128 - flat.shape[0])).reshape(rows, 128))
    return jnp.concatenate(parts, axis=0)


def _adam_small(name, parts, names, wts, mom_m, mom_v):
    params = [n for n in names if n in wts]

    def view(n):
        nelem = math.prod(wts[n].shape)
        return (nelem // 128, 128) if nelem % 128 == 0 else (1, nelem)

    offsets, r0 = {}, 0
    for n in names:
        offsets[n] = r0
        r0 += _seg_rows(math.prod(wts[n].shape) if n in wts else 1)

    def body(*refs):
        p_ref, ins, outs = refs[0], refs[1:1 + 3 * len(params)], refs[1 + 3 * len(params):]

        def total(n, rows, lanes):
            o = offsets[n]
            g = p_ref[0, o:o + rows, 0:lanes]
            for s in range(1, N_DEV):
                g = g + p_ref[s, o:o + rows, 0:lanes]
            return g

        for i, n in enumerate(params):
            g = total(n, *view(n))
            w_ref, m_ref, v_ref = ins[3 * i:3 * i + 3]
            g_ref, d_ref, mo_ref, vo_ref = outs[4 * i:4 * i + 4]
            d_ref[...], mo_ref[...], vo_ref[...] = _adamw(w_ref[...], g, m_ref[...], v_ref[...])
            g_ref[...] = g
        for j, n in enumerate(n for n in names if n not in wts):
            outs[4 * len(params) + j][...] = total(n, 8, 128)

    args, in_specs, out_shape = [parts], [_full(parts.shape)], []
    for n in params:
        args += [t[n].reshape(view(n)) for t in (wts, mom_m, mom_v)]
        in_specs += [_full(view(n))] * 3
        out_shape += [jax.ShapeDtypeStruct(view(n), F32)] * 4
    out_shape += [jax.ShapeDtypeStruct((8, 128), F32) for n in names if n not in wts]
    res = pl.pallas_call(
        body, name=name, grid=(1,), out_shape=out_shape, in_specs=in_specs,
        out_specs=[_full(s.shape) for s in out_shape], compiler_params=_params(("arbitrary",)),
    )(*args)
    done = {n: tuple(r.reshape(wts[n].shape) for r in res[4 * i:4 * i + 4]) for i, n in enumerate(params)}
    sums = {n: res[4 * len(params) + j] for j, n in enumerate(n for n in names if n not in wts)}
    return done, sums


def _t5_bucket_map():
    qi = jnp.arange(BLK)[:, None]
    si = jnp.arange(2 * BLK)[None, :]
    n = jnp.maximum(qi + BLK - si, 0)
    max_exact = N_BUCKETS // 2
    nf = jnp.maximum(n, max_exact).astype(F32)
    large = max_exact + (jnp.log(nf / max_exact) / math.log(MAX_DISTANCE / max_exact)
                         * (N_BUCKETS - max_exact)).astype(jnp.int32)
    large = jnp.minimum(large, N_BUCKETS - 1)
    return jnp.where(n < max_exact, n, large).astype(jnp.int32)


def kernel(x, c, rel_bias, w_ada, b_ada, w_in, b_in, attn_sinks, gmlp_ln_g, gmlp_ln_b, gmlp_w_s, gmlp_b_s, attn_out_g, gmlp_out_g, w_out, ln1_g, ln1_b, w_gate_up, w_down, ln2_g, ln2_b, loss_target, m_rel_bias, m_w_ada, m_b_ada, m_w_in, m_b_in, m_attn_sinks, m_gmlp_ln_g, m_gmlp_ln_b, m_gmlp_w_s, m_gmlp_b_s, m_attn_out_g, m_gmlp_out_g, m_w_out, m_ln1_g, m_ln1_b, m_w_gate_up, m_w_down, m_ln2_g, m_ln2_b, v_rel_bias, v_w_ada, v_b_ada, v_w_in, v_b_in, v_attn_sinks, v_gmlp_ln_g, v_gmlp_ln_b, v_gmlp_w_s, v_gmlp_b_s, v_attn_out_g, v_gmlp_out_g, v_w_out, v_ln1_g, v_ln1_b, v_w_gate_up, v_w_down, v_ln2_g, v_ln2_b):
    wts = dict(rel_bias=rel_bias, w_ada=w_ada, b_ada=b_ada, w_in=w_in, b_in=b_in, attn_sinks=attn_sinks,
               gmlp_ln_g=gmlp_ln_g, gmlp_ln_b=gmlp_ln_b, gmlp_w_s=gmlp_w_s, gmlp_b_s=gmlp_b_s,
               attn_out_g=attn_out_g, gmlp_out_g=gmlp_out_g, w_out=w_out, ln1_g=ln1_g, ln1_b=ln1_b,
               w_gate_up=w_gate_up, w_down=w_down, ln2_g=ln2_g, ln2_b=ln2_b)
    mom_m = dict(rel_bias=m_rel_bias, w_ada=m_w_ada, b_ada=m_b_ada, w_in=m_w_in, b_in=m_b_in,
                 attn_sinks=m_attn_sinks, gmlp_ln_g=m_gmlp_ln_g, gmlp_ln_b=m_gmlp_ln_b, gmlp_w_s=m_gmlp_w_s,
                 gmlp_b_s=m_gmlp_b_s, attn_out_g=m_attn_out_g, gmlp_out_g=m_gmlp_out_g, w_out=m_w_out,
                 ln1_g=m_ln1_g, ln1_b=m_ln1_b, w_gate_up=m_w_gate_up, w_down=m_w_down, ln2_g=m_ln2_g,
                 ln2_b=m_ln2_b)
    mom_v = dict(rel_bias=v_rel_bias, w_ada=v_w_ada, b_ada=v_b_ada, w_in=v_w_in, b_in=v_b_in,
                 attn_sinks=v_attn_sinks, gmlp_ln_g=v_gmlp_ln_g, gmlp_ln_b=v_gmlp_ln_b, gmlp_w_s=v_gmlp_w_s,
                 gmlp_b_s=v_gmlp_b_s, attn_out_g=v_attn_out_g, gmlp_out_g=v_gmlp_out_g, w_out=v_w_out,
                 ln1_g=v_ln1_g, ln1_b=v_ln1_b, w_gate_up=v_w_gate_up, w_down=v_w_down, ln2_g=v_ln2_g,
                 ln2_b=v_ln2_b)

    t = x.shape[1]
    tm = min(512, t)
    tn_ff = D_FF // 2
    tk_long, tk_short = min(4096, t), min(2048, t)
    me = 4 * lax.axis_index("x") + 2 * lax.axis_index("y") + lax.axis_index("c")
    xs = x[0]
    target = loss_target[0]

    (c_g,) = _exchange("gather_c", [jnp.broadcast_to(c, (8, D_MODEL))], ("gather",))
    c_all = c_g[:, 0, :]

    ncol = w_ada.shape[2]
    b_cols = lax.dynamic_slice(b_ada, (0, me * ncol), (1, ncol))
    mod_part = _mod_partial(c_all, w_ada[0], b_cols)
    bucket = _t5_bucket_map()
    (bias,), (mod_g, w_in_g) = _bias_table(rel_bias, bucket,
                                           comm=([mod_part, w_in[0].T.astype(BF16)], ("gather", "gather2")))
    w_in_t = w_in_g.reshape(IN_W, D_MODEL)
    mod = lax.dynamic_slice(mod_g, (0, me, 0), (N_DEV, 1, ncol)).reshape(1, N_DEV * ncol)
    sh1, sc1, g1, sh2, sc2, g2 = [mod[:, i * D_MODEL:(i + 1) * D_MODEL] for i in range(6)]

    causal = jnp.tril(jnp.ones((BLK, BLK), dtype=bool))
    ws = jnp.where(causal[None], gmlp_w_s[0], 0.0).astype(BF16)
    pair = lambda w: jnp.concatenate([w[0::2], w[1::2]], axis=2)
    ws2, wst2 = pair(ws), pair(jnp.swapaxes(ws, 1, 2))
    bfull = jnp.repeat(gmlp_b_s[0].T, GMLP_W // N_GROUPS, axis=1)
    sinks = attn_sinks[0]

    (proj, h1), (w_down_g,) = _inproj(xs, sc1, sh1, w_in_t, b_in, tm, comm=([w_down[0].astype(BF16)], ("gather2",)))
    (mixed, *kept), (w_out_g, w_gu_g) = _mix_fwd(
        proj, bias, sinks, gmlp_ln_g, gmlp_ln_b, ws2, bfull, attn_out_g, gmlp_out_g,
        comm=([w_out[0].astype(BF16), w_gate_up[0].T.astype(BF16)], ("gather2", "gather2")))
    w_out_f = w_out_g.reshape(D_MODEL, D_MODEL)
    w_gu_t = w_gu_g.reshape(2 * D_FF, D_MODEL)
    (y1, x1), _ = _outproj(mixed, w_out_f, xs, g1, ln1_g, ln1_b, tm, comm=None)
    w_down_f = w_down_g.reshape(D_FF, D_MODEL)
    h2, dsu, sg, act, dz2, dy2, loss_p, d_ln2g, d_ln2b, d_g2 = _ffn_fwd(w_gu_t, w_down_f, x1, target, sc2, sh2, g2,
                                                                        ln2_g, ln2_b, min(256, t))

    slots = lambda a: a.reshape(N_DEV, -1, D_MODEL)
    dw_down = _wgrad("wgrad_down", act, dy2, tn_ff, tk_short)
    (dgate, dup, dz1, dy1, d_sc2, d_sh2, d_ln1g, d_ln1b, d_g1), (r_down,) = _ffn_bwd(
        dy2, w_down_f, dsu, sg, w_gu_t, x1, xs, y1, dz2, sc2, g1, ln1_g, min(256, t),
        comm=([slots(dw_down)], ("scatter",)))
    dw_gu_t = _wgrad("wgrad_gate_up", dgate, h2, tn_ff, tk_short, a2=dup)
    dw_out = _wgrad("wgrad_out", mixed, dy1, D_MODEL, tk_long)
    ((dproj, dkvn, dl_acc, dsink_acc, d_lng, d_lnb, d_ws, d_bs, d_aog, d_gog), (r_gu, r_out)) = _mix_bwd(
        proj, gmlp_ln_g, gmlp_ln_b, ws2, wst2, bfull, attn_out_g, gmlp_out_g, dy1, w_out_f.T, kept,
        comm=([slots(dw_gu_t), slots(dw_out)], ("scatter", "scatter")))
    d_relb = _bias_grad(dl_acc, bucket)

    rsum = lambda a: jnp.sum(a, axis=0)
    early_g = dict(
        rel_bias=d_relb[:, 0, :N_BUCKETS].T, attn_sinks=rsum(dsink_acc)[:N_HEADS],
        gmlp_ln_g=rsum(d_lng), gmlp_ln_b=rsum(d_lnb),
        gmlp_b_s=jnp.sum(d_bs.reshape(BLK, N_GROUPS, GMLP_W // N_GROUPS), axis=2).T,
        attn_out_g=rsum(d_aog), gmlp_out_g=rsum(d_gog), ln1_g=rsum(d_ln1g), ln1_b=rsum(d_ln1b),
        ln2_g=rsum(d_ln2g), ln2_b=rsum(d_ln2b))
    (grad_x, dproj_b, d_bin, d_sc1, d_sh1), _ = _din(dproj, dkvn, w_in_t, xs, dz1, sc1, tm, comm=None)
    ws_rows = lambda a: a.reshape(N_GROUPS * BLK, BLK)
    d_ws_b = ws_rows(jnp.where(causal[None], d_ws, 0.0)).astype(BF16)
    dw_in_t, (early_all, ws_all) = _wgrad("wgrad_in", dproj_b, h1, IN_W // 2, tk_long,
                                          comm=([_pack(early_g, SMALL_EARLY), d_ws_b], ("gather2", "gather2")))
    dmod = jnp.concatenate([rsum(d_sh1), rsum(d_sc1), rsum(d_g1), rsum(d_sh2), rsum(d_sc2), rsum(d_g2)])
    late_g = dict(b_ada=dmod, b_in=rsum(d_bin), loss=(0.5 / D_MODEL * jnp.sum(loss_p)).reshape(1))
    late_all, r_in = _scatter_two_level("scatter_in", _pack(late_g, SMALL_LATE), slots(dw_in_t))

    small, _ = _adam_small("adam_small_early", early_all, SMALL_EARLY, wts, mom_m, mom_v)
    small_late, sums = _adam_small("adam_small_late", late_all, SMALL_LATE, wts, mom_m, mom_v)
    small.update(small_late)
    small["gmlp_w_s"] = [o.reshape(gmlp_w_s.shape) for o in _adam_reduce(
        "adam_w_s", ws_all, ws_rows(gmlp_w_s), ws_rows(m_gmlp_w_s), ws_rows(v_gmlp_w_s), N_GROUPS * BLK // 2)]
    loss = sums["loss"][0, 0]

    dmod_all = late_all[:, :_seg_rows(6 * D_MODEL), :].reshape(N_DEV, 6 * D_MODEL)
    dmod_cols = lax.dynamic_slice(dmod_all, (0, me * ncol), (N_DEV, ncol))
    kpad = 128 - N_DEV
    ada = _adam_w_ada(jnp.pad(c_all.T, ((0, 0), (0, kpad))), jnp.pad(dmod_cols, ((0, kpad), (0, 0))),
                      w_ada[0], m_w_ada[0], v_w_ada[0])

    tr = lambda a: jnp.swapaxes(a, -1, -2)
    big = {}
    big["w_in"] = [tr(o)[None] for o in _adam_reduce("adam_w_in", r_in, w_in[0].T, m_w_in[0].T, v_w_in[0].T, 112)]
    big["w_out"] = [o[None] for o in _adam_reduce("adam_w_out", r_out, w_out[0], m_w_out[0], v_w_out[0], 128)]
    big["w_gate_up"] = [tr(o)[None] for o in _adam_reduce("adam_w_gu", r_gu, w_gate_up[0].T, m_w_gate_up[0].T,
                                                           v_w_gate_up[0].T, 352)]
    big["w_down"] = [o[None] for o in _adam_reduce("adam_w_down", r_down, w_down[0], m_w_down[0], v_w_down[0], 176)]
    big["w_ada"] = [o[None] for o in ada]

    outs = [[], [], [], []]
    for name in WEIGHTS:
        for i in range(4):
            outs[i].append(big[name][i] if name in big else small[name][i])
    return (loss, grad_x[None], *outs[0], *outs[1], *outs[2], *outs[3])
```

```python
import math

import jax
import jax.numpy as jnp
from jax import lax
from jax.experimental import pallas as pl
from jax.experimental.pallas import tpu as pltpu

F32 = jnp.float32
BF16 = jnp.bfloat16
MESH = pl.DeviceIdType.MESH

N_DEV = 8
D_MODEL = 1024
HEAD_DIM = 64
N_HEADS = 8
N_GROUPS = 8
ATTN_W = 512
KV_W = 128
GMLP_W = 512
IN_W = 1792
BLK = 128
N_BUCKETS = 32
MAX_DISTANCE = 128
D_FF = 2816
ALPHA = 2.0 ** 0.25
LN_EPS = 1e-5
NEG_INF = -1e30
ADAM_LR = 0.001
ADAM_B1 = 0.9
ADAM_B2 = 0.999
ADAM_EPS = 1e-08
ADAM_WD = 0.01
ADAM_STEP = 10
GELU_C0 = math.sqrt(2.0 / math.pi)
GELU_C1 = 0.044715

VMEM_LIMIT = 56 * 1024 * 1024


def _params(sem):
    return pltpu.CompilerParams(dimension_semantics=sem, vmem_limit_bytes=VMEM_LIMIT)


def _dot(a, b):
    return lax.dot_general(a, b, (((1,), (0,)), ((), ())), preferred_element_type=F32)


def _dot_nt(a, b):
    return lax.dot_general(a, b, (((1,), (1,)), ((), ())), preferred_element_type=F32)


def _dot_tn(a, b):
    return lax.dot_general(a, b, (((0,), (0,)), ((), ())), preferred_element_type=F32)


def _full(shape):
    nd = len(shape)
    return pl.BlockSpec(shape, lambda *_: (0,) * nd)


def _stream(rows, cols):
    return pl.BlockSpec((rows, cols), lambda i: (i, 0))


def _rowsum8(v):
    r, c = v.shape
    return jnp.sum(v.reshape(r // 8, 8, c), axis=0)


def _sigmoid(v):
    return 1.0 / (1.0 + jnp.exp(-v))


def _gelu_parts(v):
    v2 = v * v
    t = jnp.tanh(GELU_C0 * (v + GELU_C1 * v * v2))
    g = 0.5 * v * (1.0 + t)
    dg = 0.5 * (1.0 + t) + 0.5 * v * (1.0 - t * t) * (GELU_C0 * (1.0 + 3.0 * GELU_C1 * v2))
    return g, dg


def _ln_stats(z):
    mu = jnp.mean(z, axis=1, keepdims=True)
    zc = z - mu
    var = jnp.mean(zc * zc, axis=1, keepdims=True)
    rstd = lax.rsqrt(var + LN_EPS)
    return zc * rstd, rstd


def _ln_bwd(dxhat, xhat, rstd):
    m1 = jnp.mean(dxhat, axis=1, keepdims=True)
    m2 = jnp.mean(dxhat * xhat, axis=1, keepdims=True)
    return rstd * (dxhat - m1 - xhat * m2)


def _seg_mean64(v):
    r = v.shape[0]
    lo = lax.broadcasted_iota(jnp.int32, (r, 128), 1) < 64
    outs = []
    for j in range(v.shape[1] // 128):
        ch = v[:, 128 * j:128 * (j + 1)]
        s_lo = jnp.sum(jnp.where(lo, ch, 0.0), axis=1, keepdims=True)
        s_hi = jnp.sum(jnp.where(lo, 0.0, ch), axis=1, keepdims=True)
        outs.append(jnp.where(lo, s_lo, s_hi) * (1.0 / 64.0))
    return jnp.concatenate(outs, axis=1)


def _rms(a, g):
    r = lax.rsqrt(jnp.mean(a * a, axis=1, keepdims=True) + LN_EPS)
    return a * r * g, r


def _rms_bwd(dout, a, r, g):
    t = dout * g
    return r * t - a * (r * r * r) * jnp.mean(t * a, axis=1, keepdims=True)


PEER_ORDER = (1, 2, 4, 3, 5, 6, 7)


def _peer(j):
    x, y, c = lax.axis_index("x"), lax.axis_index("y"), lax.axis_index("c")
    px = 1 - x if j & 4 else x
    py = 1 - y if j & 2 else y
    pc = 1 - c if j & 1 else c
    return (px, py, pc), 4 * px + 2 * py + pc


SIBLING = 1
CHIP_FLIPS = (4, 2, 6)


def _exchange_phase(phase, ins, outs, modes, send_sems, recv_sems, loc_sems):
    me = 4 * lax.axis_index("x") + 2 * lax.axis_index("y") + lax.axis_index("c")
    for k, mode in enumerate(modes):
        def copy(i, src, slot, dev, k=k):
            return pltpu.make_async_remote_copy(src_ref=src, dst_ref=outs[k].at[slot], send_sem=send_sems.at[k, i],
                                                recv_sem=recv_sems.at[k, i], device_id=dev, device_id_type=MESH)

        src_me = ins[k].at[me] if mode == "scatter" else ins[k]
        local = pltpu.make_async_copy(src_me, outs[k].at[me], loc_sems.at[k])
        if mode == "gather2":
            sib_dev, sib_idx = _peer(SIBLING)
            chips = [_peer(j) for j in CHIP_FLIPS]
            far = [_peer(j | SIBLING)[1] for j in CHIP_FLIPS]
            if phase == "start":
                local.start()
                copy(0, ins[k], me, sib_dev).start()
                for i, (dev, _) in enumerate(chips):
                    copy(1 + i, ins[k], me, dev).start()
            elif phase == "mid":
                for i, (dev, idx) in enumerate(chips):
                    copy(1 + i, ins[k], idx, dev).wait_recv()
                    copy(4 + i, outs[k].at[idx], idx, sib_dev).start()
            else:
                copy(0, ins[k], sib_idx, sib_dev).wait_recv()
                for i, slot in enumerate(far):
                    copy(4 + i, ins[k], slot, sib_dev).wait_recv()
                copy(0, ins[k], me, sib_dev).wait_send()
                for i, (dev, idx) in enumerate(chips):
                    copy(1 + i, ins[k], me, dev).wait_send()
                    copy(4 + i, outs[k].at[idx], idx, sib_dev).wait_send()
                local.wait()
            continue
        peers = [_peer(j) for j in PEER_ORDER]
        if phase == "start":
            local.start()
            for i, (dev, idx) in enumerate(peers):
                copy(i, ins[k].at[idx] if mode == "scatter" else ins[k], me, dev).start()
        elif phase == "end":
            for i, (dev, idx) in enumerate(peers):
                copy(i, src_me, idx, dev).wait_recv()
            for i, (dev, idx) in enumerate(peers):
                copy(i, src_me, me, dev).wait_send()
            local.wait()


def _exchange_shapes(arrays, modes):
    return [jax.ShapeDtypeStruct((N_DEV,) + (a.shape[1:] if m == "scatter" else a.shape), a.dtype)
            for a, m in zip(arrays, modes)]


def _exchange_sems(n):
    return [pltpu.SemaphoreType.DMA((n, N_DEV - 1)), pltpu.SemaphoreType.DMA((n, N_DEV - 1)),
            pltpu.SemaphoreType.DMA((n,))]


def _exchange(name, arrays, modes):
    n = len(arrays)

    def body(*refs):
        for phase in ("start", "mid", "end"):
            _exchange_phase(phase, refs[:n], refs[n:2 * n], modes, *refs[2 * n:])

    any_spec = pl.BlockSpec(memory_space=pl.ANY)
    return pl.pallas_call(
        body, name=name, out_shape=_exchange_shapes(arrays, modes),
        in_specs=[any_spec] * n, out_specs=[any_spec] * n, scratch_shapes=_exchange_sems(n),
    )(*arrays)


N_CHIP = 4


def _scatter_two_level(name, pack, parts):
    r, ncols = parts.shape[1:]

    def body(pack_ref, parts_ref, late_ref, got_ref, sib_ref, h_ref, g_send, g_recv, g_loc, d_send, d_recv, i_send, i_recv):
        x, y, c = lax.axis_index("x"), lax.axis_index("y"), lax.axis_index("c")
        my_chip = 2 * x + y
        sib_dev, _ = _peer(SIBLING)
        gather = ([pack_ref], [late_ref], ("gather",), g_send, g_recv, g_loc)
        _exchange_phase("start", *gather)

        def to_sibling(q):
            return pltpu.make_async_remote_copy(src_ref=parts_ref.at[2 * q + 1 - c], dst_ref=sib_ref.at[q],
                                                send_sem=d_send.at[q], recv_sem=d_recv.at[q],
                                                device_id=sib_dev, device_id_type=MESH)

        for q in range(N_CHIP):
            to_sibling(q).start()
        for q in range(N_CHIP):
            to_sibling(q).wait_recv()
            h_ref[q] = (parts_ref[2 * q + c].astype(F32) + sib_ref[q].astype(F32)).astype(BF16)

        def to_chip(i, slot):
            dev, idx = _peer(CHIP_FLIPS[i])
            return pltpu.make_async_remote_copy(src_ref=h_ref.at[idx // 2], dst_ref=got_ref.at[slot],
                                                send_sem=i_send.at[i], recv_sem=i_recv.at[i],
                                                device_id=dev, device_id_type=MESH)

        for i in range(len(CHIP_FLIPS)):
            to_chip(i, my_chip).start()
        got_ref[my_chip] = h_ref[my_chip]
        for i in range(len(CHIP_FLIPS)):
            to_chip(i, _peer(CHIP_FLIPS[i])[1] // 2).wait_recv()
        for i in range(len(CHIP_FLIPS)):
            to_chip(i, my_chip).wait_send()
        for q in range(N_CHIP):
            to_sibling(q).wait_send()
        _exchange_phase("end", *gather)

    any_spec = pl.BlockSpec(memory_space=pl.ANY)
    vmem = pl.BlockSpec(memory_space=pltpu.VMEM)
    dma = pltpu.SemaphoreType.DMA
    return pl.pallas_call(
        body, name=name,
        out_shape=[jax.ShapeDtypeStruct((N_DEV,) + pack.shape, pack.dtype),
                   jax.ShapeDtypeStruct((N_CHIP, r, ncols), parts.dtype)],
        in_specs=[any_spec, vmem], out_specs=[any_spec, vmem],
        scratch_shapes=[pltpu.VMEM((N_CHIP, r, ncols), parts.dtype), pltpu.VMEM((N_CHIP, r, ncols), parts.dtype),
                        dma((1, N_DEV - 1)), dma((1, N_DEV - 1)), dma((1,)),
                        dma((N_CHIP,)), dma((N_CHIP,)), dma((len(CHIP_FLIPS),)), dma((len(CHIP_FLIPS),))],
        compiler_params=pltpu.CompilerParams(vmem_limit_bytes=VMEM_LIMIT),
    )(pack, parts)


def _call(body, *, name, grid, in_specs, out_specs, out_shape, args, sem, scratch_shapes=(), comm=None):
    if comm is None:
        outs = pl.pallas_call(body, name=name, grid=grid, in_specs=list(in_specs), out_specs=list(out_specs),
                              out_shape=list(out_shape), scratch_shapes=list(scratch_shapes),
                              compiler_params=_params(sem))(*args)
        return list(outs), []
    arrays, modes = comm
    n_in, n_out, nc, ns = len(in_specs), len(out_specs), len(arrays), len(scratch_shapes)
    n_steps = math.prod(grid)

    def hosted(*refs):
        ins, cins = refs[:n_in], refs[n_in:n_in + nc]
        outs, couts = refs[n_in + nc:n_in + nc + n_out], refs[n_in + nc + n_out:n_in + 2 * nc + n_out]
        scratch = refs[n_in + 2 * nc + n_out:]
        ex = (cins, couts, modes) + tuple(scratch[ns:])
        step = pl.program_id(0)
        for ax in range(1, len(grid)):
            step = step * grid[ax] + pl.program_id(ax)

        @pl.when(step == 0)
        def _():
            _exchange_phase("start", *ex)

        body(*ins, *outs, *scratch[:ns])

        if "gather2" in modes:
            @pl.when(step == (3 * n_steps) // 4)
            def _():
                _exchange_phase("mid", *ex)

        @pl.when(step == n_steps - 1)
        def _():
            _exchange_phase("end", *ex)

    any_spec = pl.BlockSpec(memory_space=pl.ANY)
    res = pl.pallas_call(
        hosted, name=name, grid=grid, in_specs=list(in_specs) + [any_spec] * nc,
        out_specs=list(out_specs) + [any_spec] * nc, out_shape=list(out_shape) + _exchange_shapes(arrays, modes),
        scratch_shapes=list(scratch_shapes) + _exchange_sems(nc),
        compiler_params=_params(tuple("arbitrary" for _ in grid)))(*args, *arrays)
    return list(res[:n_out]), list(res[n_out:])


def _mod_partial(c_all, w_ada, b_ada_cols):
    def body(c_ref, w_ref, b_ref, o_ref):
        cv = c_ref[...]
        s = (cv * _sigmoid(cv)).astype(BF16)
        o_ref[...] = _dot(s, w_ref[...].astype(BF16)) + b_ref[...]

    ncol = w_ada.shape[1]
    return pl.pallas_call(
        body, name="mod_partial", out_shape=jax.ShapeDtypeStruct((N_DEV, ncol), F32),
        in_specs=[_full(c_all.shape), _full(w_ada.shape), _full(b_ada_cols.shape)],
        out_specs=_full((N_DEV, ncol)), grid=(1,), compiler_params=_params(("arbitrary",)),
    )(c_all, w_ada, b_ada_cols)


def _bias_table(rel_bias, bucket, comm):
    def body(rb_ref, bk_ref, o_ref):
        h = pl.program_id(0)
        bk = bk_ref[...]
        acc = jnp.zeros((BLK, 2 * BLK), F32)
        for b in range(N_BUCKETS):
            acc = jnp.where(bk == b, rb_ref[b, h], acc)
        dist = (lax.broadcasted_iota(jnp.int32, (BLK, 2 * BLK), 0) + BLK
                - lax.broadcasted_iota(jnp.int32, (BLK, 2 * BLK), 1))
        o_ref[0] = jnp.where((dist >= 0) & (dist < BLK), acc, NEG_INF)

    return _call(
        body, name="bias_table", out_shape=[jax.ShapeDtypeStruct((N_HEADS, BLK, 2 * BLK), F32)],
        in_specs=[pl.BlockSpec(memory_space=pltpu.SMEM), _full((BLK, 2 * BLK))],
        out_specs=[pl.BlockSpec((1, BLK, 2 * BLK), lambda h: (h, 0, 0))], grid=(N_HEADS,),
        sem=("arbitrary",), comm=comm, args=(rel_bias, bucket))


def _bias_grad(dl_acc, bucket):
    def body(dl_ref, bk_ref, o_ref):
        bk = bk_ref[...]
        dl = dl_ref[0]
        lane = lax.broadcasted_iota(jnp.int32, (1, 128), 1)
        row = jnp.zeros((1, 128), F32)
        for b in range(N_BUCKETS):
            s = jnp.sum(jnp.sum(jnp.where(bk == b, dl, 0.0), axis=1, keepdims=True), axis=0, keepdims=True)
            row = jnp.where(lane == b, s, row)
        o_ref[0] = row

    return pl.pallas_call(
        body, name="bias_grad", out_shape=jax.ShapeDtypeStruct((N_HEADS, 1, 128), F32),
        in_specs=[pl.BlockSpec((1, BLK, 2 * BLK), lambda h: (h, 0, 0)), _full((BLK, 2 * BLK))],
        out_specs=pl.BlockSpec((1, 1, 128), lambda h: (h, 0, 0)), grid=(N_HEADS,),
        compiler_params=_params(("arbitrary",)),
    )(dl_acc, bucket)


def _inproj(x, sc1, sh1, w_in_t, b_in, tm, comm):
    t, d = x.shape
    n = w_in_t.shape[0]

    def body(x_ref, sc_ref, sh_ref, w_ref, b_ref, proj_ref, h_ref):
        h = (x_ref[...] * (1.0 + sc_ref[...]) + sh_ref[...]).astype(BF16)
        h_ref[...] = h
        proj_ref[...] = _dot_nt(h, w_ref[...]) + b_ref[...]

    row = lambda w: pl.BlockSpec((tm, w), lambda i: (i, 0))
    return _call(
        body, name="inproj", grid=(t // tm,),
        out_shape=[jax.ShapeDtypeStruct((t, n), F32), jax.ShapeDtypeStruct((t, d), BF16)],
        in_specs=[_stream(tm, d), _full((1, d)), _full((1, d)), _full((n, d)), _full((1, n))],
        out_specs=[row(n), row(d)], sem=("parallel",), comm=comm, args=(x, sc1, sh1, w_in_t, b_in))


HALF = 64
ROWS = 32


def _lane_lo(rows):
    return lax.broadcasted_iota(jnp.int32, (rows, 128), 1) < 64


def _mix_stage_kv(proj_ref, kvp_ref, s):
    lo = _lane_lo(2 * BLK)
    for name, col in (("k", ATTN_W), ("v", ATTN_W + KV_W)):
        cur = jnp.concatenate([kvp_ref[:, col - ATTN_W:col - ATTN_W + KV_W], proj_ref[:, col:col + KV_W]], axis=0)
        plain, swapped = cur.astype(BF16), pltpu.roll(cur, 64, 1).astype(BF16)
        zero = jnp.zeros_like(plain)
        for g in range(2):
            dup = jnp.where(lo, plain, swapped) if g == 0 else jnp.where(lo, swapped, plain)
            s[name + "d"][g] = dup
            s[name + "m"][g] = jnp.concatenate([jnp.where(lo, dup, zero), jnp.where(lo, zero, dup)], axis=0)


def _group_rows(ref, g):
    return ref[4 * g:4 * g + 4].reshape(4 * BLK, ref.shape[2])


def _pair_rows(ref, g):
    return jnp.concatenate([jnp.concatenate([ref[4 * g + 2 * c], ref[4 * g + 2 * c + 1]], axis=1) for c in range(2)],
                           axis=0)


def _mask_heads(src_ref, dst_ref):
    lo = _lane_lo(BLK)
    for j in range(4):
        chunk = src_ref[:, 128 * j:128 * (j + 1)]
        dst_ref[2 * j] = jnp.where(lo, chunk, 0.0).astype(BF16)
        dst_ref[2 * j + 1] = jnp.where(lo, 0.0, chunk).astype(BF16)


def _mix_stage_attn(proj_ref, bias_ref, sinks_ref, n, s):
    _mask_heads(proj_ref, s["qm"])
    for g in range(2):
        s["lg"][g] = _dot_nt(_group_rows(s["qm"], g), s["kd"][g])
    n0mask = (n == 0) & (lax.broadcasted_iota(jnp.int32, (HALF, 2 * BLK), 1) < BLK)
    lane = lax.broadcasted_iota(jnp.int32, (HALF, 128), 1)
    for hf in range(BLK // HALF):
        rows = slice(HALF * hf, HALF * (hf + 1))
        psink = jnp.zeros((HALF, 128), F32)
        for h in range(N_HEADS):
            sk = sinks_ref[h]
            grows = slice(BLK * (h % 4) + HALF * hf, BLK * (h % 4) + HALF * (hf + 1))
            logit = s["lg"][h // 4, grows, :] * (HEAD_DIM ** -0.5) + bias_ref[h, rows, :]
            logit = jnp.where(n0mask, NEG_INF, logit)
            m = jnp.maximum(jnp.max(logit, axis=1, keepdims=True), sk)
            e = jnp.exp(logit - m)
            es = jnp.exp(sk - m)
            inv = 1.0 / (jnp.sum(e, axis=1, keepdims=True) + es)
            p = e * inv
            s["p"][h, rows, :] = p
            s["pb"][h, rows, :] = p.astype(BF16)
            psink = jnp.where(lane == h, es * inv, psink)
        s["psink"][rows, :] = psink
    for g in range(2):
        out = _dot(_pair_rows(s["pb"], g), s["vm"][g])
        s["attn"][:, 256 * g:256 * g + 128] = out[0:BLK]
        s["attn"][:, 256 * g + 128:256 * g + 256] = out[BLK:2 * BLK]


def _mix_stage_gmlp_pre(proj_ref, lng, lnb, s, keep):
    c0 = ATTN_W + 2 * KV_W
    for r0 in range(0, BLK, ROWS):
        rows = slice(r0, r0 + ROWS)
        u, du = _gelu_parts(proj_ref[rows, c0:c0 + GMLP_W])
        a, da = _gelu_parts(proj_ref[rows, c0 + GMLP_W:c0 + 2 * GMLP_W])
        ac = a - _seg_mean64(a)
        rstd = lax.rsqrt(_seg_mean64(ac * ac) + LN_EPS)
        vhat = ac * rstd
        s["u"][rows, :] = u
        s["vnb"][rows, :] = (vhat * lng + lnb).astype(BF16)
        if keep:
            s["du"][rows, :] = du
            s["da"][rows, :] = da
            s["vhat"][rows, :] = vhat
            s["rstd"][rows, :] = rstd


def _stack_halves(chunk):
    lo = _lane_lo(BLK)
    zero = jnp.zeros_like(chunk)
    return jnp.concatenate([jnp.where(lo, chunk, zero), jnp.where(lo, zero, chunk)], axis=0)


def _mix_stage_gmlp_mix(ws2_ref, bfull_ref, s):
    for j in range(4):
        cols = slice(128 * j, 128 * (j + 1))
        s["ms"][:, cols] = _dot(ws2_ref[j], _stack_halves(s["vnb"][:, cols])) + bfull_ref[:, cols]


def _mix_scratch(keep):
    f32 = lambda *shape: pltpu.VMEM(shape, F32)
    b16 = lambda *shape: pltpu.VMEM(shape, BF16)
    names = dict(kd=b16(2, 2 * BLK, 128), vd=b16(2, 2 * BLK, 128), km=b16(2, 4 * BLK, 128), vm=b16(2, 4 * BLK, 128),
                 qm=b16(N_HEADS, BLK, 128), lg=f32(2, 4 * BLK, 2 * BLK), pb=b16(N_HEADS, BLK, 2 * BLK),
                 u=f32(BLK, GMLP_W), vnb=b16(BLK, GMLP_W), ms=f32(BLK, GMLP_W))
    if keep:
        names.update(dom=b16(N_HEADS, BLK, 128), dls=b16(N_HEADS, BLK, 2 * BLK),
                     dattn=f32(BLK, ATTN_W), dmix=f32(BLK, D_MODEL), du=f32(BLK, GMLP_W), da=f32(BLK, GMLP_W),
                     vhat=f32(BLK, GMLP_W), rstd=f32(BLK, GMLP_W), dmsb=b16(BLK, GMLP_W), dvn=f32(BLK, GMLP_W))
    return list(names), list(names.values())


def _mix_specs(with_logit_inputs):
    logit_inputs = [_full((N_HEADS, BLK, 2 * BLK)), pl.BlockSpec(memory_space=pltpu.SMEM)] if with_logit_inputs else []
    return [pl.BlockSpec((BLK, IN_W), lambda n: (n, 0)),
            pl.BlockSpec((BLK, 2 * KV_W), lambda n: (jnp.maximum(n - 1, 0), ATTN_W // (2 * KV_W)))] + logit_inputs + [
            _full((1, GMLP_W)), _full((1, GMLP_W)),
            _full((N_GROUPS // 2, BLK, 2 * BLK)), _full((BLK, GMLP_W)),
            _full((1, ATTN_W)), _full((1, GMLP_W))]


KEPT = [("p", (N_HEADS, BLK, 2 * BLK), F32), ("psink", (BLK, 128), F32), ("attn", (BLK, ATTN_W), F32)]


def _kept_shapes(t):
    full = lambda blk: (blk[0], t, blk[2]) if len(blk) == 3 else (t, blk[1])
    return [jax.ShapeDtypeStruct(full(blk), dt) for _, blk, dt in KEPT]


def _kept_specs():
    return [pl.BlockSpec(blk, (lambda n: (0, n, 0)) if len(blk) == 3 else (lambda n: (n, 0))) for _, blk, _ in KEPT]


def _mix_fwd(proj, bias, sinks, lng, lnb, ws2, bfull, aog, gog, comm):
    t = proj.shape[0]
    names, shapes = _mix_scratch(False)

    def body(proj_ref, kvp_ref, bias_ref, sinks_ref, lng_ref, lnb_ref, ws2_ref, bfull_ref, aog_ref, gog_ref,
             out_ref, *rest):
        s = dict(zip([name for name, _, _ in KEPT] + names, rest))
        n = pl.program_id(0)
        _mix_stage_kv(proj_ref, kvp_ref, s)
        _mix_stage_attn(proj_ref, bias_ref, sinks_ref, n, s)
        _mix_stage_gmlp_pre(proj_ref, lng_ref[...], lnb_ref[...], s, False)
        _mix_stage_gmlp_mix(ws2_ref, bfull_ref, s)
        for r0 in range(0, BLK, ROWS):
            rows = slice(r0, r0 + ROWS)
            out_ref[rows, 0:ATTN_W] = _rms(s["attn"][rows, :], aog_ref[...])[0].astype(BF16)
            out_ref[rows, ATTN_W:ATTN_W + GMLP_W] = _rms(s["u"][rows, :] * s["ms"][rows, :], gog_ref[...])[0].astype(BF16)

    return _call(
        body, name="mix_fwd", grid=(t // BLK,),
        out_shape=[jax.ShapeDtypeStruct((t, D_MODEL), BF16)] + _kept_shapes(t),
        in_specs=_mix_specs(True), out_specs=[pl.BlockSpec((BLK, D_MODEL), lambda n: (n, 0))] + _kept_specs(),
        scratch_shapes=shapes,
        sem=("parallel",), comm=comm, args=(proj, proj, bias, sinks, lng, lnb, ws2, bfull, aog, gog))


def _mix_bwd(proj, lng, lnb, ws2, wst2, bfull, aog, gog, dy, w_out, kept, comm):
    t = proj.shape[0]
    nb = t // BLK
    names, shapes = _mix_scratch(True)
    c_gu = ATTN_W + 2 * KV_W

    def body(proj_ref, kvp_ref, lng_ref, lnb_ref, ws2_ref, bfull_ref, aog_ref, gog_ref,
             wst2_ref, dy_ref, wout_ref, *rest):
        n_kept = len(KEPT)
        s = dict(zip([name for name, _, _ in KEPT], rest[:n_kept]))
        (dproj_ref, dkvn_ref, dl_ref, dsink_ref, dlng_ref, dlnb_ref, dws_ref, dbs_ref, daog_ref,
         dgog_ref) = rest[n_kept:n_kept + 10]
        s.update(zip(names, rest[n_kept + 10:]))
        n = pl.program_id(0)

        @pl.when(n == 0)
        def _():
            for r in (dl_ref, dsink_ref, dlng_ref, dlnb_ref, dws_ref, dbs_ref, daog_ref, dgog_ref):
                r[...] = jnp.zeros_like(r)

        s["dmix"][...] = _dot(dy_ref[...], wout_ref[...])
        _mix_stage_kv(proj_ref, kvp_ref, s)
        _mask_heads(proj_ref, s["qm"])
        lng = lng_ref[...]
        _mix_stage_gmlp_pre(proj_ref, lng, lnb_ref[...], s, True)
        _mix_stage_gmlp_mix(ws2_ref, bfull_ref, s)

        aog, gog = aog_ref[...], gog_ref[...]
        for r0 in range(0, BLK, ROWS):
            rows = slice(r0, r0 + ROWS)
            attn, dma = s["attn"][rows, :], s["dmix"][rows, 0:ATTN_W]
            _, r_a = _rms(attn, aog)
            daog_ref[...] += _rowsum8(dma * attn * r_a)
            s["dattn"][rows, :] = _rms_bwd(dma, attn, r_a, aog)
            u, ms, dmg = s["u"][rows, :], s["ms"][rows, :], s["dmix"][rows, ATTN_W:ATTN_W + GMLP_W]
            gm = u * ms
            _, r_g = _rms(gm, gog)
            dgog_ref[...] += _rowsum8(dmg * gm * r_g)
            dgm = _rms_bwd(dmg, gm, r_g, gog)
            dproj_ref[rows, c_gu:c_gu + GMLP_W] = dgm * ms * s["du"][rows, :]
            dms = dgm * u
            dbs_ref[rows, :] += dms
            s["dmsb"][rows, :] = dms.astype(BF16)

        _mask_heads(s["dattn"], s["dom"])
        for g in range(2):
            s["lg"][g] = _dot_nt(_group_rows(s["dom"], g), s["vd"][g])
        lane = lax.broadcasted_iota(jnp.int32, (HALF, 128), 1)
        for hf in range(BLK // HALF):
            rows = slice(HALF * hf, HALF * (hf + 1))
            dsink = jnp.zeros((HALF, 128), F32)
            for h in range(N_HEADS):
                grows = slice(BLK * (h % 4) + HALF * hf, BLK * (h % 4) + HALF * (hf + 1))
                dp = s["lg"][h // 4, grows, :]
                p = s["p"][h, rows, :]
                s["pb"][h, rows, :] = p.astype(BF16)
                rs = jnp.sum(p * dp, axis=1, keepdims=True)
                dl = p * (dp - rs)
                dl_ref[h, rows, :] += dl
                dsink = dsink + jnp.where(lane == h, -s["psink"][rows, :] * rs, 0.0)
                s["dls"][h, rows, :] = (dl * (HEAD_DIM ** -0.5)).astype(BF16)
            dsink_ref[rows, :] += dsink
        for g in range(2):
            dq = _dot(_pair_rows(s["dls"], g), s["km"][g])
            dproj_ref[:, 256 * g:256 * g + 128] = dq[0:BLK]
            dproj_ref[:, 256 * g + 128:256 * g + 256] = dq[BLK:2 * BLK]
        lo_k = _lane_lo(2 * BLK)
        for col, lhs, rhs in ((0, "dls", "qm"), (KV_W, "pb", "dom")):
            raw = [_dot_tn(_group_rows(s[lhs], g), _group_rows(s[rhs], g)) for g in range(2)]
            both = [r + pltpu.roll(r, 64, 1) for r in raw]
            dkv = jnp.where(lo_k, both[0], both[1])
            dproj_ref[:, ATTN_W + col:ATTN_W + col + KV_W] = dkv[BLK:2 * BLK]
            dkvn_ref[:, col:col + KV_W] = dkv[0:BLK]

        for j in range(4):
            cols = slice(128 * j, 128 * (j + 1))
            dm2 = _stack_halves(s["dmsb"][:, cols])
            vnb = s["vnb"][:, cols]
            dws2 = _dot_nt(dm2, vnb)
            dws_ref[2 * j] += dws2[0:BLK]
            dws_ref[2 * j + 1] += dws2[BLK:2 * BLK]
            s["dvn"][:, cols] = _dot(wst2_ref[j], dm2)
        for r0 in range(0, BLK, ROWS):
            rows = slice(r0, r0 + ROWS)
            dvn, vhat = s["dvn"][rows, :], s["vhat"][rows, :]
            dlng_ref[...] += _rowsum8(dvn * vhat)
            dlnb_ref[...] += _rowsum8(dvn)
            dvh = dvn * lng
            dact = s["rstd"][rows, :] * (dvh - _seg_mean64(dvh) - vhat * _seg_mean64(dvh * vhat))
            dproj_ref[rows, c_gu + GMLP_W:IN_W] = dact * s["da"][rows, :]

    acc8 = lambda w: jax.ShapeDtypeStruct((8, w), F32)
    out_shape = [jax.ShapeDtypeStruct((t, IN_W), F32), jax.ShapeDtypeStruct((t, 2 * KV_W), F32),
                 jax.ShapeDtypeStruct((N_HEADS, BLK, 2 * BLK), F32), jax.ShapeDtypeStruct((BLK, 128), F32),
                 acc8(GMLP_W), acc8(GMLP_W), jax.ShapeDtypeStruct((N_GROUPS, BLK, BLK), F32),
                 jax.ShapeDtypeStruct((BLK, GMLP_W), F32), acc8(ATTN_W), acc8(GMLP_W)]
    out_specs = [pl.BlockSpec((BLK, IN_W), lambda n: (n, 0)),
                 pl.BlockSpec((BLK, 2 * KV_W), lambda n: ((n + nb - 1) % nb, 0)),
                 _full((N_HEADS, BLK, 2 * BLK)), _full((BLK, 128)), _full((8, GMLP_W)), _full((8, GMLP_W)),
                 _full((N_GROUPS, BLK, BLK)), _full((BLK, GMLP_W)), _full((8, ATTN_W)), _full((8, GMLP_W))]
    in_specs = _mix_specs(False) + [_full((N_GROUPS // 2, BLK, 2 * BLK)),
                               pl.BlockSpec((BLK, D_MODEL), lambda n: (n, 0)),
                               _full((D_MODEL, D_MODEL))] + _kept_specs()
    return _call(
        body, name="mix_bwd", grid=(nb,), out_shape=out_shape, in_specs=in_specs, out_specs=out_specs,
        scratch_shapes=shapes, sem=("arbitrary",), comm=comm,
        args=(proj, proj, lng, lnb, ws2, bfull, aog, gog, wst2, dy, w_out, *kept))


def _outproj(mixed, w_out, x, g1, ln1g, ln1b, sc2, sh2, tm, comm):
    t, d = x.shape

    def body(mx_ref, w_ref, x_ref, g1_ref, lg_ref, lb_ref, sc_ref, sh_ref, y_ref, x1_ref, h2_ref):
        y = _dot(mx_ref[...], w_ref[...])
        xhat, _ = _ln_stats(ALPHA * x_ref[...] + g1_ref[...] * y)
        x1 = xhat * lg_ref[...] + lb_ref[...]
        y_ref[...] = y
        x1_ref[...] = x1
        h2_ref[...] = (x1 * (1.0 + sc_ref[...]) + sh_ref[...]).astype(BF16)

    row = pl.BlockSpec((tm, d), lambda i: (i, 0))
    vec = _full((1, d))
    return _call(
        body, name="outproj", grid=(t // tm,),
        out_shape=[jax.ShapeDtypeStruct((t, d), F32), jax.ShapeDtypeStruct((t, d), F32),
                   jax.ShapeDtypeStruct((t, d), BF16)],
        in_specs=[_stream(tm, d), _full((d, d)), _stream(tm, d), vec, vec, vec, vec, vec], out_specs=[row, row, row],
        sem=("parallel",), comm=comm, args=(mixed, w_out, x, g1, ln1g, ln1b, sc2, sh2))


def _ffn_fwd(h2, w_gu_t, w_down, x1, target, g2, ln2g, ln2b, tm):
    t, d = x1.shape

    def body(h_ref, w_ref, wd_ref, x1_ref, tg_ref, g2_ref, lg_ref, lb_ref,
             dsu_ref, sg_ref, act_ref, dz_ref, dy_ref, loss_ref, dlg_ref, dlb_ref, dg2_ref):
        @pl.when(pl.program_id(0) == 0)
        def _():
            for r in (loss_ref, dlg_ref, dlb_ref, dg2_ref):
                r[...] = jnp.zeros_like(r)

        h = h_ref[...]
        g = _dot_nt(h, w_ref[0:D_FF])
        u = _dot_nt(h, w_ref[D_FF:2 * D_FF])
        s = _sigmoid(g)
        sg = g * s
        act = (sg * u).astype(BF16)
        dsu_ref[...] = (u * (s * (1.0 + g * (1.0 - s)))).astype(BF16)
        sg_ref[...] = sg.astype(BF16)
        act_ref[...] = act
        y2 = _dot(act, wd_ref[...])
        g2 = g2_ref[...]
        lg = lg_ref[...]
        xhat, rstd = _ln_stats(ALPHA * x1_ref[...] + g2 * y2)
        err = xhat * lg + lb_ref[...] - tg_ref[...]
        loss_ref[...] += _rowsum8(err * err)
        dx2 = err * (1.0 / d)
        dlg_ref[...] += _rowsum8(dx2 * xhat)
        dlb_ref[...] += _rowsum8(dx2)
        dz = _ln_bwd(dx2 * lg, xhat, rstd)
        dg2_ref[...] += _rowsum8(dz * y2)
        dz_ref[...] = dz
        dy_ref[...] = (g2 * dz).astype(BF16)

    row = pl.BlockSpec((tm, d), lambda i: (i, 0))
    wide = pl.BlockSpec((tm, D_FF), lambda i: (i, 0))
    vec = _full((1, d))
    acc = _full((8, d))
    acc_shape = jax.ShapeDtypeStruct((8, d), F32)
    wide_shape = jax.ShapeDtypeStruct((t, D_FF), BF16)
    return pl.pallas_call(
        body, name="ffn_fwd", grid=(t // tm,),
        out_shape=[wide_shape] * 3 + [jax.ShapeDtypeStruct((t, d), F32), jax.ShapeDtypeStruct((t, d), BF16)]
        + [acc_shape] * 4,
        in_specs=[_stream(tm, d), _resident((2 * D_FF, d)), _resident((D_FF, d)), _stream(tm, d), _stream(tm, d),
                  vec, vec, vec],
        out_specs=[wide] * 3 + [row, row, acc, acc, acc, acc], compiler_params=_params(("arbitrary",)),
    )(h2, w_gu_t, w_down, x1, target, g2, ln2g, ln2b)


def _resident(shape):
    nd = len(shape)
    return pl.BlockSpec(shape, lambda *_: (0,) * nd, pipeline_mode=pl.Buffered(1))


def _ffn_bwd(dy2, w_down, dsu, sg, w_gu_t, x1, x, y, dz2, sc2, g1, ln1g, tm, comm):
    t, d = x1.shape

    def body(dy2_ref, wd_ref, dsu_ref, sg_ref, w_ref, x1_ref, x_ref, y_ref, dz2_ref, sc_ref, g1_ref, lg_ref,
             dg_ref, du_ref, dz1_ref, dy_ref, dsc_ref, dsh_ref, dlg_ref, dlb_ref, dg1_ref):
        @pl.when(pl.program_id(0) == 0)
        def _():
            for r in (dsc_ref, dsh_ref, dlg_ref, dlb_ref, dg1_ref):
                r[...] = jnp.zeros_like(r)

        dact = _dot_nt(dy2_ref[...], wd_ref[...])
        dg = (dact * dsu_ref[...].astype(F32)).astype(BF16)
        du = (dact * sg_ref[...].astype(F32)).astype(BF16)
        dg_ref[...] = dg
        du_ref[...] = du
        dh2 = _dot(dg, w_ref[0:D_FF]) + _dot(du, w_ref[D_FF:2 * D_FF])
        x1 = x1_ref[...]
        y = y_ref[...]
        g1 = g1_ref[...]
        dsc_ref[...] += _rowsum8(dh2 * x1)
        dsh_ref[...] += _rowsum8(dh2)
        dx1 = dh2 * (1.0 + sc_ref[...]) + ALPHA * dz2_ref[...]
        xhat, rstd = _ln_stats(ALPHA * x_ref[...] + g1 * y)
        dlg_ref[...] += _rowsum8(dx1 * xhat)
        dlb_ref[...] += _rowsum8(dx1)
        dz1 = _ln_bwd(dx1 * lg_ref[...], xhat, rstd)
        dg1_ref[...] += _rowsum8(dz1 * y)
        dz1_ref[...] = dz1
        dy_ref[...] = (g1 * dz1).astype(BF16)

    row = pl.BlockSpec((tm, d), lambda i: (i, 0))
    wide = pl.BlockSpec((tm, D_FF), lambda i: (i, 0))
    vec = _full((1, d))
    acc = _full((8, d))
    acc_shape = jax.ShapeDtypeStruct((8, d), F32)
    wide_shape = jax.ShapeDtypeStruct((t, D_FF), BF16)
    return _call(
        body, name="ffn_bwd", grid=(t // tm,),
        out_shape=[wide_shape, wide_shape, jax.ShapeDtypeStruct((t, d), F32), jax.ShapeDtypeStruct((t, d), BF16)]
        + [acc_shape] * 5,
        in_specs=[_stream(tm, d), _resident((D_FF, d)), _stream(tm, D_FF), _stream(tm, D_FF), _resident((2 * D_FF, d)),
                  _stream(tm, d), _stream(tm, d), _stream(tm, d), _stream(tm, d), vec, vec, vec],
        out_specs=[wide, wide, row, row, acc, acc, acc, acc, acc], sem=("arbitrary",), comm=comm,
        args=(dy2, w_down, dsu, sg, w_gu_t, x1, x, y, dz2, sc2, g1, ln1g))


def _din(dproj, dkvn, w_in_t, x, dz1, sc1, tm, comm):
    t, d = x.shape
    nt = t // tm
    depth = 3

    def body(dp_hbm, dkv_hbm, w_ref, x_hbm, dz1_hbm, sc_ref, dx_ref, dpb_ref, dbin_ref, dsc_ref, dsh_ref,
             dp_buf, dkv_buf, x_buf, dz1_buf, sems):
        i = pl.program_id(0)
        streams = ((dp_hbm, dp_buf), (dkv_hbm, dkv_buf), (x_hbm, x_buf), (dz1_hbm, dz1_buf))

        def fetch(k, tile, slot):
            src, buf = streams[k]
            return pltpu.make_async_copy(src.at[pl.ds(pl.multiple_of(tile * tm, tm), tm), :], buf.at[slot],
                                         sems.at[k, slot])

        @pl.when(i == 0)
        def _():
            for r in (dbin_ref, dsc_ref, dsh_ref):
                r[...] = jnp.zeros_like(r)
            for ahead in range(min(depth - 1, nt)):
                for k in range(len(streams)):
                    fetch(k, ahead, ahead).start()

        @pl.when(i + depth - 1 < nt)
        def _():
            for k in range(len(streams)):
                fetch(k, i + depth - 1, (i + depth - 1) % depth).start()

        slot = i % depth
        for k in range(len(streams)):
            fetch(k, i, slot).wait()
        dp = jnp.concatenate([dp_buf[slot, :, 0:ATTN_W], dp_buf[slot, :, ATTN_W:ATTN_W + 2 * KV_W] + dkv_buf[slot],
                              dp_buf[slot, :, ATTN_W + 2 * KV_W:IN_W]], axis=1)
        dbin_ref[...] += _rowsum8(dp)
        dpb = dp.astype(BF16)
        dpb_ref[...] = dpb
        dh = _dot(dpb, w_ref[...])
        dsc_ref[...] += _rowsum8(dh * x_buf[slot])
        dsh_ref[...] += _rowsum8(dh)
        dx_ref[...] = dh * (1.0 + sc_ref[...]) + ALPHA * dz1_buf[slot]

    row = lambda w: pl.BlockSpec((tm, w), lambda i: (i, 0))
    any_spec = pl.BlockSpec(memory_space=pl.ANY)
    return _call(
        body, name="din", grid=(nt,),
        out_shape=[jax.ShapeDtypeStruct((t, d), F32), jax.ShapeDtypeStruct((t, IN_W), BF16),
                   jax.ShapeDtypeStruct((8, IN_W), F32), jax.ShapeDtypeStruct((8, d), F32),
                   jax.ShapeDtypeStruct((8, d), F32)],
        in_specs=[any_spec, any_spec, _full((IN_W, d)), any_spec, any_spec, _full((1, d))],
        out_specs=[row(d), row(IN_W), _full((8, IN_W)), _full((8, d)), _full((8, d))],
        scratch_shapes=[pltpu.VMEM((depth, tm, IN_W), F32), pltpu.VMEM((depth, tm, 2 * KV_W), F32),
                        pltpu.VMEM((depth, tm, d), F32), pltpu.VMEM((depth, tm, d), F32),
                        pltpu.SemaphoreType.DMA((4, depth))],
        sem=("arbitrary",), comm=comm, args=(dproj, dkvn, w_in_t, x, dz1, sc1))


def _wgrad(name, a, b, tmm, tk, comm=None, a2=None):
    t, m = a.shape
    n = b.shape[1]
    nk = t // tk
    nm = m // tmm

    def body(*refs):
        a_refs, (b_ref, o_ref, acc_ref) = refs[:-3], refs[-3:]
        i, k = pl.program_id(0), pl.program_id(1)

        @pl.when(k == 0)
        def _():
            acc_ref[...] = jnp.zeros_like(acc_ref)

        a_tile = a_refs[0][...] if a2 is None else jnp.where(i < nm, a_refs[0][...], a_refs[1][...])
        acc_ref[...] += _dot_tn(a_tile, b_ref[...])

        @pl.when(k == nk - 1)
        def _():
            o_ref[...] = acc_ref[...].astype(BF16)

    if a2 is None:
        a_specs, a_args, n_tiles = [pl.BlockSpec((tk, tmm), lambda i, k: (k, i))], (a,), nm
    else:
        a_specs = [pl.BlockSpec((tk, tmm), lambda i, k: (jnp.where(i < nm, k, 0), jnp.minimum(i, nm - 1))),
                   pl.BlockSpec((tk, tmm), lambda i, k: (jnp.where(i < nm, 0, k), jnp.maximum(i - nm, 0)))]
        a_args, n_tiles = (a, a2), 2 * nm
    (out,), got = _call(
        body, name=name, grid=(n_tiles, nk), out_shape=[jax.ShapeDtypeStruct((n_tiles * tmm, n), BF16)],
        in_specs=a_specs + [pl.BlockSpec((tk, n), lambda i, k: (k, 0))],
        out_specs=[pl.BlockSpec((tmm, n), lambda i, k: (i, 0))],
        scratch_shapes=[pltpu.VMEM((tmm, n), F32)], sem=("parallel", "arbitrary"), comm=comm, args=a_args + (b,))
    return out if comm is None else (out, got)


def _adamw(w, g, m, v):
    m = ADAM_B1 * m + (1.0 - ADAM_B1) * g
    v = ADAM_B2 * v + (1.0 - ADAM_B2) * (g * g)
    m_hat = m / (1.0 - ADAM_B1 ** ADAM_STEP)
    v_hat = v / (1.0 - ADAM_B2 ** ADAM_STEP)
    delta = -ADAM_LR * (m_hat / (jnp.sqrt(v_hat) + ADAM_EPS) + ADAM_WD * w)
    return delta, m, v


def _adam_reduce(name, parts, w, m, v, tr):
    r, cdim = w.shape
    n_slots = parts.shape[0]

    def body(p_ref, w_ref, m_ref, v_ref, g_ref, d_ref, mo_ref, vo_ref):
        g = p_ref[0].astype(F32)
        for s in range(1, n_slots):
            g = g + p_ref[s].astype(F32)
        d_ref[...], mo_ref[...], vo_ref[...] = _adamw(w_ref[...], g, m_ref[...], v_ref[...])
        g_ref[...] = g

    tile = pl.BlockSpec((tr, cdim), lambda i: (i, 0))
    shp = jax.ShapeDtypeStruct((r, cdim), F32)
    return pl.pallas_call(
        body, name=name, grid=(r // tr,), out_shape=[shp] * 4,
        in_specs=[pl.BlockSpec((n_slots, tr, cdim), lambda i: (0, i, 0)), tile, tile, tile],
        out_specs=[tile] * 4, compiler_params=_params(("parallel",)),
    )(parts, w, m, v)


def _adam_w_ada(c_all_t, dmod_cols, w, m, v):
    def body(ct_ref, dm_ref, w_ref, m_ref, v_ref, g_ref, d_ref, mo_ref, vo_ref):
        ct = ct_ref[...]
        s = (ct * _sigmoid(ct)).astype(BF16)
        g = _dot(s, dm_ref[...].astype(BF16))
        d_ref[...], mo_ref[...], vo_ref[...] = _adamw(w_ref[...], g, m_ref[...], v_ref[...])
        g_ref[...] = g

    shp = jax.ShapeDtypeStruct(w.shape, F32)
    return pl.pallas_call(
        body, name="adam_w_ada", grid=(1,), out_shape=[shp] * 4,
        in_specs=[_full(c_all_t.shape), _full(dmod_cols.shape)] + [_full(w.shape)] * 3,
        out_specs=[_full(w.shape)] * 4, compiler_params=_params(("arbitrary",)),
    )(c_all_t, dmod_cols, w, m, v)


SMALL_EARLY = ["rel_bias", "attn_sinks", "gmlp_ln_g", "gmlp_ln_b", "gmlp_b_s",
               "attn_out_g", "gmlp_out_g", "ln1_g", "ln1_b", "ln2_g", "ln2_b"]
SMALL_LATE = ["b_ada", "b_in", "loss"]
WEIGHTS = ["rel_bias", "w_ada", "b_ada", "w_in", "b_in", "attn_sinks", "gmlp_ln_g", "gmlp_ln_b", "gmlp_w_s",
           "gmlp_b_s", "attn_out_g", "gmlp_out_g", "w_out", "ln1_g", "ln1_b", "w_gate_up", "w_down", "ln2_g", "ln2_b"]


def _seg_rows(nelem):
    return -(-nelem // 1024) * 8


def _pack(named, names):
    parts = []
    for name in names:
        flat = named[name].reshape(-1).astype(F32)
        rows = _seg_rows(flat.shape[0])
        parts.append(jnp.pad(flat, (0, rows * 128 - flat.shape[0])).reshape(rows, 128))
    return jnp.concatenate(parts, axis=0)


def _adam_small(name, parts, names, wts, mom_m, mom_v):
    params = [n for n in names if n in wts]

    def view(n):
        nelem = math.prod(wts[n].shape)
        return (nelem // 128, 128) if nelem % 128 == 0 else (1, nelem)

    offsets, r0 = {}, 0
    for n in names:
        offsets[n] = r0
        r0 += _seg_rows(math.prod(wts[n].shape) if n in wts else 1)

    def body(*refs):
        p_ref, ins, outs = refs[0], refs[1:1 + 3 * len(params)], refs[1 + 3 * len(params):]

        def total(n, rows, lanes):
            o = offsets[n]
            g = p_ref[0, o:o + rows, 0:lanes]
            for s in range(1, N_DEV):
                g = g + p_ref[s, o:o + rows, 0:lanes]
            return g

        for i, n in enumerate(params):
            g = total(n, *view(n))
            w_ref, m_ref, v_ref = ins[3 * i:3 * i + 3]
            g_ref, d_ref, mo_ref, vo_ref = outs[4 * i:4 * i + 4]
            d_ref[...], mo_ref[...], vo_ref[...] = _adamw(w_ref[...], g, m_ref[...], v_ref[...])
            g_ref[...] = g
        for j, n in enumerate(n for n in names if n not in wts):
            outs[4 * len(params) + j][...] = total(n, 8, 128)

    args, in_specs, out_shape = [parts], [_full(parts.shape)], []
    for n in params:
        args += [t[n].reshape(view(n)) for t in (wts, mom_m, mom_v)]
        in_specs += [_full(view(n))] * 3
        out_shape += [jax.ShapeDtypeStruct(view(n), F32)] * 4
    out_shape += [jax.ShapeDtypeStruct((8, 128), F32) for n in names if n not in wts]
    res = pl.pallas_call(
        body, name=name, grid=(1,), out_shape=out_shape, in_specs=in_specs,
        out_specs=[_full(s.shape) for s in out_shape], compiler_params=_params(("arbitrary",)),
    )(*args)
    done = {n: tuple(r.reshape(wts[n].shape) for r in res[4 * i:4 * i + 4]) for i, n in enumerate(params)}
    sums = {n: res[4 * len(params) + j] for j, n in enumerate(n for n in names if n not in wts)}
    return done, sums


def _t5_bucket_map():
    qi = jnp.arange(BLK)[:, None]
    si = jnp.arange(2 * BLK)[None, :]
    n = jnp.maximum(qi + BLK - si, 0)
    max_exact = N_BUCKETS // 2
    nf = jnp.maximum(n, max_exact).astype(F32)
    large = max_exact + (jnp.log(nf / max_exact) / math.log(MAX_DISTANCE / max_exact)
                         * (N_BUCKETS - max_exact)).astype(jnp.int32)
    large = jnp.minimum(large, N_BUCKETS - 1)
    return jnp.where(n < max_exact, n, large).astype(jnp.int32)


def kernel(x, c, rel_bias, w_ada, b_ada, w_in, b_in, attn_sinks, gmlp_ln_g, gmlp_ln_b, gmlp_w_s, gmlp_b_s, attn_out_g, gmlp_out_g, w_out, ln1_g, ln1_b, w_gate_up, w_down, ln2_g, ln2_b, loss_target, m_rel_bias, m_w_ada, m_b_ada, m_w_in, m_b_in, m_attn_sinks, m_gmlp_ln_g, m_gmlp_ln_b, m_gmlp_w_s, m_gmlp_b_s, m_attn_out_g, m_gmlp_out_g, m_w_out, m_ln1_g, m_ln1_b, m_w_gate_up, m_w_down, m_ln2_g, m_ln2_b, v_rel_bias, v_w_ada, v_b_ada, v_w_in, v_b_in, v_attn_sinks, v_gmlp_ln_g, v_gmlp_ln_b, v_gmlp_w_s, v_gmlp_b_s, v_attn_out_g, v_gmlp_out_g, v_w_out, v_ln1_g, v_ln1_b, v_w_gate_up, v_w_down, v_ln2_g, v_ln2_b):
    wts = dict(rel_bias=rel_bias, w_ada=w_ada, b_ada=b_ada, w_in=w_in, b_in=b_in, attn_sinks=attn_sinks,
               gmlp_ln_g=gmlp_ln_g, gmlp_ln_b=gmlp_ln_b, gmlp_w_s=gmlp_w_s, gmlp_b_s=gmlp_b_s,
               attn_out_g=attn_out_g, gmlp_out_g=gmlp_out_g, w_out=w_out, ln1_g=ln1_g, ln1_b=ln1_b,
               w_gate_up=w_gate_up, w_down=w_down, ln2_g=ln2_g, ln2_b=ln2_b)
    mom_m = dict(rel_bias=m_rel_bias, w_ada=m_w_ada, b_ada=m_b_ada, w_in=m_w_in, b_in=m_b_in,
                 attn_sinks=m_attn_sinks, gmlp_ln_g=m_gmlp_ln_g, gmlp_ln_b=m_gmlp_ln_b, gmlp_w_s=m_gmlp_w_s,
                 gmlp_b_s=m_gmlp_b_s, attn_out_g=m_attn_out_g, gmlp_out_g=m_gmlp_out_g, w_out=m_w_out,
                 ln1_g=m_ln1_g, ln1_b=m_ln1_b, w_gate_up=m_w_gate_up, w_down=m_w_down, ln2_g=m_ln2_g,
                 ln2_b=m_ln2_b)
    mom_v = dict(rel_bias=v_rel_bias, w_ada=v_w_ada, b_ada=v_b_ada, w_in=v_w_in, b_in=v_b_in,
                 attn_sinks=v_attn_sinks, gmlp_ln_g=v_gmlp_ln_g, gmlp_ln_b=v_gmlp_ln_b, gmlp_w_s=v_gmlp_w_s,
                 gmlp_b_s=v_gmlp_b_s, attn_out_g=v_attn_out_g, gmlp_out_g=v_gmlp_out_g, w_out=v_w_out,
                 ln1_g=v_ln1_g, ln1_b=v_ln1_b, w_gate_up=v_w_gate_up, w_down=v_w_down, ln2_g=v_ln2_g,
                 ln2_b=v_ln2_b)

    t = x.shape[1]
    tm = min(512, t)
    tn_ff = D_FF // 2
    tk_long, tk_short = min(4096, t), min(2048, t)
    me = 4 * lax.axis_index("x") + 2 * lax.axis_index("y") + lax.axis_index("c")
    xs = x[0]
    target = loss_target[0]

    (c_g,) = _exchange("gather_c", [jnp.broadcast_to(c, (8, D_MODEL))], ("gather",))
    c_all = c_g[:, 0, :]

    ncol = w_ada.shape[2]
    b_cols = lax.dynamic_slice(b_ada, (0, me * ncol), (1, ncol))
    mod_part = _mod_partial(c_all, w_ada[0], b_cols)
    bucket = _t5_bucket_map()
    (bias,), (mod_g, w_in_g) = _bias_table(rel_bias, bucket,
                                           comm=([mod_part, w_in[0].T.astype(BF16)], ("gather", "gather2")))
    w_in_t = w_in_g.reshape(IN_W, D_MODEL)
    mod = lax.dynamic_slice(mod_g, (0, me, 0), (N_DEV, 1, ncol)).reshape(1, N_DEV * ncol)
    sh1, sc1, g1, sh2, sc2, g2 = [mod[:, i * D_MODEL:(i + 1) * D_MODEL] for i in range(6)]

    causal = jnp.tril(jnp.ones((BLK, BLK), dtype=bool))
    ws = jnp.where(causal[None], gmlp_w_s[0], 0.0).astype(BF16)
    pair = lambda w: jnp.concatenate([w[0::2], w[1::2]], axis=2)
    ws2, wst2 = pair(ws), pair(jnp.swapaxes(ws, 1, 2))
    bfull = jnp.repeat(gmlp_b_s[0].T, GMLP_W // N_GROUPS, axis=1)
    sinks = attn_sinks[0]

    (proj, h1), (w_down_g,) = _inproj(xs, sc1, sh1, w_in_t, b_in, tm, comm=([w_down[0].astype(BF16)], ("gather2",)))
    (mixed, *kept), (w_out_g, w_gu_g) = _mix_fwd(
        proj, bias, sinks, gmlp_ln_g, gmlp_ln_b, ws2, bfull, attn_out_g, gmlp_out_g,
        comm=([w_out[0].astype(BF16), w_gate_up[0].T.astype(BF16)], ("gather2", "gather2")))
    w_out_f = w_out_g.reshape(D_MODEL, D_MODEL)
    w_gu_t = w_gu_g.reshape(2 * D_FF, D_MODEL)
    (y1, x1, h2), _ = _outproj(mixed, w_out_f, xs, g1, ln1_g, ln1_b, sc2, sh2, tm, comm=None)
    w_down_f = w_down_g.reshape(D_FF, D_MODEL)
    dsu, sg, act, dz2, dy2, loss_p, d_ln2g, d_ln2b, d_g2 = _ffn_fwd(h2, w_gu_t, w_down_f, x1, target, g2, ln2_g, ln2_b,
                                                                    min(256, t))

    slots = lambda a: a.reshape(N_DEV, -1, D_MODEL)
    dw_down = _wgrad("wgrad_down", act, dy2, tn_ff, tk_short)
    (dgate, dup, dz1, dy1, d_sc2, d_sh2, d_ln1g, d_ln1b, d_g1), (r_down,) = _ffn_bwd(
        dy2, w_down_f, dsu, sg, w_gu_t, x1, xs, y1, dz2, sc2, g1, ln1_g, min(256, t),
        comm=([slots(dw_down)], ("scatter",)))
    dw_gu_t = _wgrad("wgrad_gate_up", dgate, h2, tn_ff, tk_short, a2=dup)
    dw_out = _wgrad("wgrad_out", mixed, dy1, D_MODEL, tk_long)
    ((dproj, dkvn, dl_acc, dsink_acc, d_lng, d_lnb, d_ws, d_bs, d_aog, d_gog), (r_gu, r_out)) = _mix_bwd(
        proj, gmlp_ln_g, gmlp_ln_b, ws2, wst2, bfull, attn_out_g, gmlp_out_g, dy1, w_out_f.T, kept,
        comm=([slots(dw_gu_t), slots(dw_out)], ("scatter", "scatter")))
    d_relb = _bias_grad(dl_acc, bucket)

    rsum = lambda a: jnp.sum(a, axis=0)
    early_g = dict(
        rel_bias=d_relb[:, 0, :N_BUCKETS].T, attn_sinks=rsum(dsink_acc)[:N_HEADS],
        gmlp_ln_g=rsum(d_lng), gmlp_ln_b=rsum(d_lnb),
        gmlp_b_s=jnp.sum(d_bs.reshape(BLK, N_GROUPS, GMLP_W // N_GROUPS), axis=2).T,
        attn_out_g=rsum(d_aog), gmlp_out_g=rsum(d_gog), ln1_g=rsum(d_ln1g), ln1_b=rsum(d_ln1b),
        ln2_g=rsum(d_ln2g), ln2_b=rsum(d_ln2b))
    (grad_x, dproj_b, d_bin, d_sc1, d_sh1), _ = _din(dproj, dkvn, w_in_t, xs, dz1, sc1, tm, comm=None)
    ws_rows = lambda a: a.reshape(N_GROUPS * BLK, BLK)
    d_ws_b = ws_rows(jnp.where(causal[None], d_ws, 0.0)).astype(BF16)
    dw_in_t, (early_all, ws_all) = _wgrad("wgrad_in", dproj_b, h1, IN_W // 2, tk_long,
                                          comm=([_pack(early_g, SMALL_EARLY), d_ws_b], ("gather2", "gather2")))
    dmod = jnp.concatenate([rsum(d_sh1), rsum(d_sc1), rsum(d_g1), rsum(d_sh2), rsum(d_sc2), rsum(d_g2)])
    late_g = dict(b_ada=dmod, b_in=rsum(d_bin), loss=(0.5 / D_MODEL * jnp.sum(loss_p)).reshape(1))
    late_all, r_in = _scatter_two_level("scatter_in", _pack(late_g, SMALL_LATE), slots(dw_in_t))

    small, _ = _adam_small("adam_small_early", early_all, SMALL_EARLY, wts, mom_m, mom_v)
    small_late, sums = _adam_small("adam_small_late", late_all, SMALL_LATE, wts, mom_m, mom_v)
    small.update(small_late)
    small["gmlp_w_s"] = [o.reshape(gmlp_w_s.shape) for o in _adam_reduce(
        "adam_w_s", ws_all, ws_rows(gmlp_w_s), ws_rows(m_gmlp_w_s), ws_rows(v_gmlp_w_s), N_GROUPS * BLK // 2)]
    loss = sums["loss"][0, 0]

    dmod_all = late_all[:, :_seg_rows(6 * D_MODEL), :].reshape(N_DEV, 6 * D_MODEL)
    dmod_cols = lax.dynamic_slice(dmod_all, (0, me * ncol), (N_DEV, ncol))
    kpad = 128 - N_DEV
    ada = _adam_w_ada(jnp.pad(c_all.T, ((0, 0), (0, kpad))), jnp.pad(dmod_cols, ((0, kpad), (0, 0))),
                      w_ada[0], m_w_ada[0], v_w_ada[0])

    tr = lambda a: jnp.swapaxes(a, -1, -2)
    big = {}
    big["w_in"] = [tr(o)[None] for o in _adam_reduce("adam_w_in", r_in, w_in[0].T, m_w_in[0].T, v_w_in[0].T, 112)]
    big["w_out"] = [o[None] for o in _adam_reduce("adam_w_out", r_out, w_out[0], m_w_out[0], v_w_out[0], 128)]
    big["w_gate_up"] = [tr(o)[None] for o in _adam_reduce("adam_w_gu", r_gu, w_gate_up[0].T, m_w_gate_up[0].T,
                                                           v_w_gate_up[0].T, 352)]
    big["w_down"] = [o[None] for o in _adam_reduce("adam_w_down", r_down, w_down[0], m_w_down[0], v_w_down[0], 176)]
    big["w_ada"] = [o[None] for o in ada]

    outs = [[], [], [], []]
    for name in WEIGHTS:
        for i in range(4):
            outs[i].append(big[name][i] if name in big else small[name][i])
    return (loss, grad_x[None], *outs[0], *outs[1], *outs[2], *outs[3])
```

```python
import math

import jax
import jax.numpy as jnp
from jax import lax
from jax.experimental import pallas as pl
from jax.experimental.pallas import tpu as pltpu

F32 = jnp.float32
BF16 = jnp.bfloat16
MESH = pl.DeviceIdType.MESH

N_DEV = 8
D_MODEL = 1024
HEAD_DIM = 64
N_HEADS = 8
N_GROUPS = 8
ATTN_W = 512
KV_W = 128
GMLP_W = 512
IN_W = 1792
BLK = 128
N_BUCKETS = 32
MAX_DISTANCE = 128
D_FF = 2816
ALPHA = 2.0 ** 0.25
LN_EPS = 1e-5
NEG_INF = -1e30
ADAM_LR = 0.001
ADAM_B1 = 0.9
ADAM_B2 = 0.999
ADAM_EPS = 1e-08
ADAM_WD = 0.01
ADAM_STEP = 10
GELU_C0 = math.sqrt(2.0 / math.pi)
GELU_C1 = 0.044715

VMEM_LIMIT = 56 * 1024 * 1024


def _params(sem):
    return pltpu.CompilerParams(dimension_semantics=sem, vmem_limit_bytes=VMEM_LIMIT)


def _dot(a, b):
    return lax.dot_general(a, b, (((1,), (0,)), ((), ())), preferred_element_type=F32)


def _dot_nt(a, b):
    return lax.dot_general(a, b, (((1,), (1,)), ((), ())), preferred_element_type=F32)


def _dot_tn(a, b):
    return lax.dot_general(a, b, (((0,), (0,)), ((), ())), preferred_element_type=F32)


def _full(shape):
    nd = len(shape)
    return pl.BlockSpec(shape, lambda *_: (0,) * nd)


def _stream(rows, cols):
    return pl.BlockSpec((rows, cols), lambda i: (i, 0))


def _rowsum8(v):
    r, c = v.shape
    return jnp.sum(v.reshape(r // 8, 8, c), axis=0)


def _sigmoid(v):
    return 1.0 / (1.0 + jnp.exp(-v))


def _gelu_parts(v):
    v2 = v * v
    t = jnp.tanh(GELU_C0 * (v + GELU_C1 * v * v2))
    g = 0.5 * v * (1.0 + t)
    dg = 0.5 * (1.0 + t) + 0.5 * v * (1.0 - t * t) * (GELU_C0 * (1.0 + 3.0 * GELU_C1 * v2))
    return g, dg


def _ln_stats(z):
    mu = jnp.mean(z, axis=1, keepdims=True)
    zc = z - mu
    var = jnp.mean(zc * zc, axis=1, keepdims=True)
    rstd = lax.rsqrt(var + LN_EPS)
    return zc * rstd, rstd


def _ln_bwd(dxhat, xhat, rstd):
    m1 = jnp.mean(dxhat, axis=1, keepdims=True)
    m2 = jnp.mean(dxhat * xhat, axis=1, keepdims=True)
    return rstd * (dxhat - m1 - xhat * m2)


def _seg_mean64(v):
    r = v.shape[0]
    lo = lax.broadcasted_iota(jnp.int32, (r, 128), 1) < 64
    outs = []
    for j in range(v.shape[1] // 128):
        ch = v[:, 128 * j:128 * (j + 1)]
        s_lo = jnp.sum(jnp.where(lo, ch, 0.0), axis=1, keepdims=True)
        s_hi = jnp.sum(jnp.where(lo, 0.0, ch), axis=1, keepdims=True)
        outs.append(jnp.where(lo, s_lo, s_hi) * (1.0 / 64.0))
    return jnp.concatenate(outs, axis=1)


def _rms(a, g):
    r = lax.rsqrt(jnp.mean(a * a, axis=1, keepdims=True) + LN_EPS)
    return a * r * g, r


def _rms_bwd(dout, a, r, g):
    t = dout * g
    return r * t - a * (r * r * r) * jnp.mean(t * a, axis=1, keepdims=True)


PEER_ORDER = (1, 2, 4, 3, 5, 6, 7)


def _peer(j):
    x, y, c = lax.axis_index("x"), lax.axis_index("y"), lax.axis_index("c")
    px = 1 - x if j & 4 else x
    py = 1 - y if j & 2 else y
    pc = 1 - c if j & 1 else c
    return (px, py, pc), 4 * px + 2 * py + pc


SIBLING = 1
CHIP_FLIPS = (4, 2, 6)


def _exchange_phase(phase, ins, outs, modes, send_sems, recv_sems, loc_sems):
    me = 4 * lax.axis_index("x") + 2 * lax.axis_index("y") + lax.axis_index("c")
    for k, mode in enumerate(modes):
        def copy(i, src, slot, dev, k=k):
            return pltpu.make_async_remote_copy(src_ref=src, dst_ref=outs[k].at[slot], send_sem=send_sems.at[k, i],
                                                recv_sem=recv_sems.at[k, i], device_id=dev, device_id_type=MESH)

        src_me = ins[k].at[me] if mode == "scatter" else ins[k]
        local = pltpu.make_async_copy(src_me, outs[k].at[me], loc_sems.at[k])
        if mode == "gather2":
            sib_dev, sib_idx = _peer(SIBLING)
            chips = [_peer(j) for j in CHIP_FLIPS]
            far = [_peer(j | SIBLING)[1] for j in CHIP_FLIPS]
            if phase == "start":
                local.start()
                copy(0, ins[k], me, sib_dev).start()
                for i, (dev, _) in enumerate(chips):
                    copy(1 + i, ins[k], me, dev).start()
            elif phase == "mid":
                for i, (dev, idx) in enumerate(chips):
                    copy(1 + i, ins[k], idx, dev).wait_recv()
                    copy(4 + i, outs[k].at[idx], idx, sib_dev).start()
            else:
                copy(0, ins[k], sib_idx, sib_dev).wait_recv()
                for i, slot in enumerate(far):
                    copy(4 + i, ins[k], slot, sib_dev).wait_recv()
                copy(0, ins[k], me, sib_dev).wait_send()
                for i, (dev, idx) in enumerate(chips):
                    copy(1 + i, ins[k], me, dev).wait_send()
                    copy(4 + i, outs[k].at[idx], idx, sib_dev).wait_send()
                local.wait()
            continue
        peers = [_peer(j) for j in PEER_ORDER]
        if phase == "start":
            local.start()
            for i, (dev, idx) in enumerate(peers):
                copy(i, ins[k].at[idx] if mode == "scatter" else ins[k], me, dev).start()
        elif phase == "end":
            for i, (dev, idx) in enumerate(peers):
                copy(i, src_me, idx, dev).wait_recv()
            for i, (dev, idx) in enumerate(peers):
                copy(i, src_me, me, dev).wait_send()
            local.wait()


def _exchange_shapes(arrays, modes):
    return [jax.ShapeDtypeStruct((N_DEV,) + (a.shape[1:] if m == "scatter" else a.shape), a.dtype)
            for a, m in zip(arrays, modes)]


def _exchange_sems(n):
    return [pltpu.SemaphoreType.DMA((n, N_DEV - 1)), pltpu.SemaphoreType.DMA((n, N_DEV - 1)),
            pltpu.SemaphoreType.DMA((n,))]


def _exchange(name, arrays, modes):
    n = len(arrays)

    def body(*refs):
        for phase in ("start", "mid", "end"):
            _exchange_phase(phase, refs[:n], refs[n:2 * n], modes, *refs[2 * n:])

    any_spec = pl.BlockSpec(memory_space=pl.ANY)
    return pl.pallas_call(
        body, name=name, out_shape=_exchange_shapes(arrays, modes),
        in_specs=[any_spec] * n, out_specs=[any_spec] * n, scratch_shapes=_exchange_sems(n),
    )(*arrays)


N_CHIP = 4


def _scatter_two_level(name, pack, parts):
    r, ncols = parts.shape[1:]

    def body(pack_ref, parts_ref, late_ref, got_ref, sib_ref, h_ref, g_send, g_recv, g_loc, d_send, d_recv, i_send, i_recv):
        x, y, c = lax.axis_index("x"), lax.axis_index("y"), lax.axis_index("c")
        my_chip = 2 * x + y
        sib_dev, _ = _peer(SIBLING)
        gather = ([pack_ref], [late_ref], ("gather",), g_send, g_recv, g_loc)
        _exchange_phase("start", *gather)

        def to_sibling(q):
            return pltpu.make_async_remote_copy(src_ref=parts_ref.at[2 * q + 1 - c], dst_ref=sib_ref.at[q],
                                                send_sem=d_send.at[q], recv_sem=d_recv.at[q],
                                                device_id=sib_dev, device_id_type=MESH)

        for q in range(N_CHIP):
            to_sibling(q).start()
        for q in range(N_CHIP):
            to_sibling(q).wait_recv()
            h_ref[q] = (parts_ref[2 * q + c].astype(F32) + sib_ref[q].astype(F32)).astype(BF16)

        def to_chip(i, slot):
            dev, idx = _peer(CHIP_FLIPS[i])
            return pltpu.make_async_remote_copy(src_ref=h_ref.at[idx // 2], dst_ref=got_ref.at[slot],
                                                send_sem=i_send.at[i], recv_sem=i_recv.at[i],
                                                device_id=dev, device_id_type=MESH)

        for i in range(len(CHIP_FLIPS)):
            to_chip(i, my_chip).start()
        got_ref[my_chip] = h_ref[my_chip]
        for i in range(len(CHIP_FLIPS)):
            to_chip(i, _peer(CHIP_FLIPS[i])[1] // 2).wait_recv()
        for i in range(len(CHIP_FLIPS)):
            to_chip(i, my_chip).wait_send()
        for q in range(N_CHIP):
            to_sibling(q).wait_send()
        _exchange_phase("end", *gather)

    any_spec = pl.BlockSpec(memory_space=pl.ANY)
    vmem = pl.BlockSpec(memory_space=pltpu.VMEM)
    dma = pltpu.SemaphoreType.DMA
    return pl.pallas_call(
        body, name=name,
        out_shape=[jax.ShapeDtypeStruct((N_DEV,) + pack.shape, pack.dtype),
                   jax.ShapeDtypeStruct((N_CHIP, r, ncols), parts.dtype)],
        in_specs=[any_spec, vmem], out_specs=[any_spec, vmem],
        scratch_shapes=[pltpu.VMEM((N_CHIP, r, ncols), parts.dtype), pltpu.VMEM((N_CHIP, r, ncols), parts.dtype),
                        dma((1, N_DEV - 1)), dma((1, N_DEV - 1)), dma((1,)),
                        dma((N_CHIP,)), dma((N_CHIP,)), dma((len(CHIP_FLIPS),)), dma((len(CHIP_FLIPS),))],
        compiler_params=pltpu.CompilerParams(vmem_limit_bytes=VMEM_LIMIT),
    )(pack, parts)


def _call(body, *, name, grid, in_specs, out_specs, out_shape, args, sem, scratch_shapes=(), comm=None):
    if comm is None:
        outs = pl.pallas_call(body, name=name, grid=grid, in_specs=list(in_specs), out_specs=list(out_specs),
                              out_shape=list(out_shape), scratch_shapes=list(scratch_shapes),
                              compiler_params=_params(sem))(*args)
        return list(outs), []
    arrays, modes = comm
    n_in, n_out, nc, ns = len(in_specs), len(out_specs), len(arrays), len(scratch_shapes)
    n_steps = math.prod(grid)

    def hosted(*refs):
        ins, cins = refs[:n_in], refs[n_in:n_in + nc]
        outs, couts = refs[n_in + nc:n_in + nc + n_out], refs[n_in + nc + n_out:n_in + 2 * nc + n_out]
        scratch = refs[n_in + 2 * nc + n_out:]
        ex = (cins, couts, modes) + tuple(scratch[ns:])
        step = pl.program_id(0)
        for ax in range(1, len(grid)):
            step = step * grid[ax] + pl.program_id(ax)

        @pl.when(step == 0)
        def _():
            _exchange_phase("start", *ex)

        body(*ins, *outs, *scratch[:ns])

        if "gather2" in modes:
            @pl.when(step == (3 * n_steps) // 4)
            def _():
                _exchange_phase("mid", *ex)

        @pl.when(step == n_steps - 1)
        def _():
            _exchange_phase("end", *ex)

    any_spec = pl.BlockSpec(memory_space=pl.ANY)
    res = pl.pallas_call(
        hosted, name=name, grid=grid, in_specs=list(in_specs) + [any_spec] * nc,
        out_specs=list(out_specs) + [any_spec] * nc, out_shape=list(out_shape) + _exchange_shapes(arrays, modes),
        scratch_shapes=list(scratch_shapes) + _exchange_sems(nc),
        compiler_params=_params(tuple("arbitrary" for _ in grid)))(*args, *arrays)
    return list(res[:n_out]), list(res[n_out:])


def _mod_partial(c_all, w_ada, b_ada_cols):
    def body(c_ref, w_ref, b_ref, o_ref):
        cv = c_ref[...]
        s = (cv * _sigmoid(cv)).astype(BF16)
        o_ref[...] = _dot(s, w_ref[...].astype(BF16)) + b_ref[...]

    ncol = w_ada.shape[1]
    return pl.pallas_call(
        body, name="mod_partial", out_shape=jax.ShapeDtypeStruct((N_DEV, ncol), F32),
        in_specs=[_full(c_all.shape), _full(w_ada.shape), _full(b_ada_cols.shape)],
        out_specs=_full((N_DEV, ncol)), grid=(1,), compiler_params=_params(("arbitrary",)),
    )(c_all, w_ada, b_ada_cols)


def _bias_table(rel_bias, bucket, comm):
    def body(rb_ref, bk_ref, o_ref):
        h = pl.program_id(0)
        bk = bk_ref[...]
        acc = jnp.zeros((BLK, 2 * BLK), F32)
        for b in range(N_BUCKETS):
            acc = jnp.where(bk == b, rb_ref[b, h], acc)
        dist = (lax.broadcasted_iota(jnp.int32, (BLK, 2 * BLK), 0) + BLK
                - lax.broadcasted_iota(jnp.int32, (BLK, 2 * BLK), 1))
        o_ref[0] = jnp.where((dist >= 0) & (dist < BLK), acc, NEG_INF)

    return _call(
        body, name="bias_table", out_shape=[jax.ShapeDtypeStruct((N_HEADS, BLK, 2 * BLK), F32)],
        in_specs=[pl.BlockSpec(memory_space=pltpu.SMEM), _full((BLK, 2 * BLK))],
        out_specs=[pl.BlockSpec((1, BLK, 2 * BLK), lambda h: (h, 0, 0))], grid=(N_HEADS,),
        sem=("arbitrary",), comm=comm, args=(rel_bias, bucket))


def _bias_grad(dl_acc, bucket):
    def body(dl_ref, bk_ref, o_ref):
        bk = bk_ref[...]
        dl = dl_ref[0]
        lane = lax.broadcasted_iota(jnp.int32, (1, 128), 1)
        row = jnp.zeros((1, 128), F32)
        for b in range(N_BUCKETS):
            s = jnp.sum(jnp.sum(jnp.where(bk == b, dl, 0.0), axis=1, keepdims=True), axis=0, keepdims=True)
            row = jnp.where(lane == b, s, row)
        o_ref[0] = row

    return pl.pallas_call(
        body, name="bias_grad", out_shape=jax.ShapeDtypeStruct((N_HEADS, 1, 128), F32),
        in_specs=[pl.BlockSpec((1, BLK, 2 * BLK), lambda h: (h, 0, 0)), _full((BLK, 2 * BLK))],
        out_specs=pl.BlockSpec((1, 1, 128), lambda h: (h, 0, 0)), grid=(N_HEADS,),
        compiler_params=_params(("arbitrary",)),
    )(dl_acc, bucket)


def _inproj(x, sc1, sh1, w_in_t, b_in, tm, comm):
    t, d = x.shape
    n = w_in_t.shape[0]

    def body(x_ref, sc_ref, sh_ref, w_ref, b_ref, proj_ref, h_ref):
        h = (x_ref[...] * (1.0 + sc_ref[...]) + sh_ref[...]).astype(BF16)
        h_ref[...] = h
        proj_ref[...] = _dot_nt(h, w_ref[...]) + b_ref[...]

    row = lambda w: pl.BlockSpec((tm, w), lambda i: (i, 0))
    return _call(
        body, name="inproj", grid=(t // tm,),
        out_shape=[jax.ShapeDtypeStruct((t, n), F32), jax.ShapeDtypeStruct((t, d), BF16)],
        in_specs=[_stream(tm, d), _full((1, d)), _full((1, d)), _full((n, d)), _full((1, n))],
        out_specs=[row(n), row(d)], sem=("parallel",), comm=comm, args=(x, sc1, sh1, w_in_t, b_in))


HALF = 64
ROWS = 32


def _lane_lo(rows):
    return lax.broadcasted_iota(jnp.int32, (rows, 128), 1) < 64


def _mix_stage_kv(proj_ref, kvp_ref, s):
    lo = _lane_lo(2 * BLK)
    for name, col in (("k", ATTN_W), ("v", ATTN_W + KV_W)):
        cur = jnp.concatenate([kvp_ref[:, col - ATTN_W:col - ATTN_W + KV_W], proj_ref[:, col:col + KV_W]], axis=0)
        plain, swapped = cur.astype(BF16), pltpu.roll(cur, 64, 1).astype(BF16)
        zero = jnp.zeros_like(plain)
        for g in range(2):
            dup = jnp.where(lo, plain, swapped) if g == 0 else jnp.where(lo, swapped, plain)
            s[name + "d"][g] = dup
            s[name + "m"][g] = jnp.concatenate([jnp.where(lo, dup, zero), jnp.where(lo, zero, dup)], axis=0)


def _group_rows(ref, g):
    return ref[4 * g:4 * g + 4].reshape(4 * BLK, ref.shape[2])


def _pair_rows(ref, g):
    return jnp.concatenate([jnp.concatenate([ref[4 * g + 2 * c], ref[4 * g + 2 * c + 1]], axis=1) for c in range(2)],
                           axis=0)


def _mask_heads(src_ref, dst_ref):
    lo = _lane_lo(BLK)
    for j in range(4):
        chunk = src_ref[:, 128 * j:128 * (j + 1)]
        dst_ref[2 * j] = jnp.where(lo, chunk, 0.0).astype(BF16)
        dst_ref[2 * j + 1] = jnp.where(lo, 0.0, chunk).astype(BF16)


def _mix_stage_attn(proj_ref, bias_ref, sinks_ref, n, s):
    _mask_heads(proj_ref, s["qm"])
    for g in range(2):
        s["lg"][g] = _dot_nt(_group_rows(s["qm"], g), s["kd"][g])
    n0mask = (n == 0) & (lax.broadcasted_iota(jnp.int32, (HALF, 2 * BLK), 1) < BLK)
    lane = lax.broadcasted_iota(jnp.int32, (HALF, 128), 1)
    for hf in range(BLK // HALF):
        rows = slice(HALF * hf, HALF * (hf + 1))
        psink = jnp.zeros((HALF, 128), F32)
        for h in range(N_HEADS):
            sk = sinks_ref[h]
            grows = slice(BLK * (h % 4) + HALF * hf, BLK * (h % 4) + HALF * (hf + 1))
            logit = s["lg"][h // 4, grows, :] * (HEAD_DIM ** -0.5) + bias_ref[h, rows, :]
            logit = jnp.where(n0mask, NEG_INF, logit)
            m = jnp.maximum(jnp.max(logit, axis=1, keepdims=True), sk)
            e = jnp.exp(logit - m)
            es = jnp.exp(sk - m)
            inv = 1.0 / (jnp.sum(e, axis=1, keepdims=True) + es)
            p = e * inv
            s["p"][h, rows, :] = p
            s["pb"][h, rows, :] = p.astype(BF16)
            psink = jnp.where(lane == h, es * inv, psink)
        s["psink"][rows, :] = psink
    for g in range(2):
        out = _dot(_pair_rows(s["pb"], g), s["vm"][g])
        s["attn"][:, 256 * g:256 * g + 128] = out[0:BLK]
        s["attn"][:, 256 * g + 128:256 * g + 256] = out[BLK:2 * BLK]


def _mix_stage_gmlp_pre(proj_ref, lng, lnb, s, keep):
    c0 = ATTN_W + 2 * KV_W
    for r0 in range(0, BLK, ROWS):
        rows = slice(r0, r0 + ROWS)
        u, du = _gelu_parts(proj_ref[rows, c0:c0 + GMLP_W])
        a, da = _gelu_parts(proj_ref[rows, c0 + GMLP_W:c0 + 2 * GMLP_W])
        ac = a - _seg_mean64(a)
        rstd = lax.rsqrt(_seg_mean64(ac * ac) + LN_EPS)
        vhat = ac * rstd
        s["u"][rows, :] = u
        s["vnb"][rows, :] = (vhat * lng + lnb).astype(BF16)
        if keep:
            s["du"][rows, :] = du
            s["da"][rows, :] = da
            s["vhat"][rows, :] = vhat
            s["rstd"][rows, :] = rstd


def _stack_halves(chunk):
    lo = _lane_lo(BLK)
    zero = jnp.zeros_like(chunk)
    return jnp.concatenate([jnp.where(lo, chunk, zero), jnp.where(lo, zero, chunk)], axis=0)


def _mix_stage_gmlp_mix(ws2_ref, bfull_ref, s):
    for j in range(4):
        cols = slice(128 * j, 128 * (j + 1))
        s["ms"][:, cols] = _dot(ws2_ref[j], _stack_halves(s["vnb"][:, cols])) + bfull_ref[:, cols]


def _mix_scratch(keep):
    f32 = lambda *shape: pltpu.VMEM(shape, F32)
    b16 = lambda *shape: pltpu.VMEM(shape, BF16)
    names = dict(kd=b16(2, 2 * BLK, 128), vd=b16(2, 2 * BLK, 128), km=b16(2, 4 * BLK, 128), vm=b16(2, 4 * BLK, 128),
                 qm=b16(N_HEADS, BLK, 128), lg=f32(2, 4 * BLK, 2 * BLK), pb=b16(N_HEADS, BLK, 2 * BLK),
                 u=f32(BLK, GMLP_W), vnb=b16(BLK, GMLP_W), ms=f32(BLK, GMLP_W))
    if keep:
        names.update(dom=b16(N_HEADS, BLK, 128), dls=b16(N_HEADS, BLK, 2 * BLK),
                     dattn=f32(BLK, ATTN_W), dmix=f32(BLK, D_MODEL), du=f32(BLK, GMLP_W), da=f32(BLK, GMLP_W),
                     vhat=f32(BLK, GMLP_W), rstd=f32(BLK, GMLP_W), dmsb=b16(BLK, GMLP_W), dvn=f32(BLK, GMLP_W))
    return list(names), list(names.values())


def _mix_specs(with_logit_inputs):
    logit_inputs = [_full((N_HEADS, BLK, 2 * BLK)), pl.BlockSpec(memory_space=pltpu.SMEM)] if with_logit_inputs else []
    return [pl.BlockSpec((BLK, IN_W), lambda n: (n, 0)),
            pl.BlockSpec((BLK, 2 * KV_W), lambda n: (jnp.maximum(n - 1, 0), ATTN_W // (2 * KV_W)))] + logit_inputs + [
            _full((1, GMLP_W)), _full((1, GMLP_W)),
            _full((N_GROUPS // 2, BLK, 2 * BLK)), _full((BLK, GMLP_W)),
            _full((1, ATTN_W)), _full((1, GMLP_W))]


KEPT = [("p", (N_HEADS, BLK, 2 * BLK), F32), ("psink", (BLK, 128), F32), ("attn", (BLK, ATTN_W), F32)]


def _kept_shapes(t):
    full = lambda blk: (blk[0], t, blk[2]) if len(blk) == 3 else (t, blk[1])
    return [jax.ShapeDtypeStruct(full(blk), dt) for _, blk, dt in KEPT]


def _kept_specs():
    return [pl.BlockSpec(blk, (lambda n: (0, n, 0)) if len(blk) == 3 else (lambda n: (n, 0))) for _, blk, _ in KEPT]


def _mix_fwd(proj, bias, sinks, lng, lnb, ws2, bfull, aog, gog, comm):
    t = proj.shape[0]
    names, shapes = _mix_scratch(False)

    def body(proj_ref, kvp_ref, bias_ref, sinks_ref, lng_ref, lnb_ref, ws2_ref, bfull_ref, aog_ref, gog_ref,
             out_ref, *rest):
        s = dict(zip([name for name, _, _ in KEPT] + names, rest))
        n = pl.program_id(0)
        _mix_stage_kv(proj_ref, kvp_ref, s)
        _mix_stage_attn(proj_ref, bias_ref, sinks_ref, n, s)
        _mix_stage_gmlp_pre(proj_ref, lng_ref[...], lnb_ref[...], s, False)
        _mix_stage_gmlp_mix(ws2_ref, bfull_ref, s)
        for r0 in range(0, BLK, ROWS):
            rows = slice(r0, r0 + ROWS)
            out_ref[rows, 0:ATTN_W] = _rms(s["attn"][rows, :], aog_ref[...])[0].astype(BF16)
            out_ref[rows, ATTN_W:ATTN_W + GMLP_W] = _rms(s["u"][rows, :] * s["ms"][rows, :], gog_ref[...])[0].astype(BF16)

    return _call(
        body, name="mix_fwd", grid=(t // BLK,),
        out_shape=[jax.ShapeDtypeStruct((t, D_MODEL), BF16)] + _kept_shapes(t),
        in_specs=_mix_specs(True), out_specs=[pl.BlockSpec((BLK, D_MODEL), lambda n: (n, 0))] + _kept_specs(),
        scratch_shapes=shapes,
        sem=("parallel",), comm=comm, args=(proj, proj, bias, sinks, lng, lnb, ws2, bfull, aog, gog))


def _mix_bwd(proj, lng, lnb, ws2, wst2, bfull, aog, gog, dy, w_out, kept, comm):
    t = proj.shape[0]
    nb = t // BLK
    names, shapes = _mix_scratch(True)
    c_gu = ATTN_W + 2 * KV_W

    def body(proj_ref, kvp_ref, lng_ref, lnb_ref, ws2_ref, bfull_ref, aog_ref, gog_ref,
             wst2_ref, dy_ref, wout_ref, *rest):
        n_kept = len(KEPT)
        s = dict(zip([name for name, _, _ in KEPT], rest[:n_kept]))
        (dproj_ref, dkvn_ref, dl_ref, dsink_ref, dlng_ref, dlnb_ref, dws_ref, dbs_ref, daog_ref,
         dgog_ref) = rest[n_kept:n_kept + 10]
        s.update(zip(names, rest[n_kept + 10:]))
        n = pl.program_id(0)

        @pl.when(n == 0)
        def _():
            for r in (dl_ref, dsink_ref, dlng_ref, dlnb_ref, dws_ref, dbs_ref, daog_ref, dgog_ref):
                r[...] = jnp.zeros_like(r)

        s["dmix"][...] = _dot(dy_ref[...], wout_ref[...])
        _mix_stage_kv(proj_ref, kvp_ref, s)
        _mask_heads(proj_ref, s["qm"])
        lng = lng_ref[...]
        _mix_stage_gmlp_pre(proj_ref, lng, lnb_ref[...], s, True)
        _mix_stage_gmlp_mix(ws2_ref, bfull_ref, s)

        aog, gog = aog_ref[...], gog_ref[...]
        for r0 in range(0, BLK, ROWS):
            rows = slice(r0, r0 + ROWS)
            attn, dma = s["attn"][rows, :], s["dmix"][rows, 0:ATTN_W]
            _, r_a = _rms(attn, aog)
            daog_ref[...] += _rowsum8(dma * attn * r_a)
            s["dattn"][rows, :] = _rms_bwd(dma, attn, r_a, aog)
            u, ms, dmg = s["u"][rows, :], s["ms"][rows, :], s["dmix"][rows, ATTN_W:ATTN_W + GMLP_W]
            gm = u * ms
            _, r_g = _rms(gm, gog)
            dgog_ref[...] += _rowsum8(dmg * gm * r_g)
            dgm = _rms_bwd(dmg, gm, r_g, gog)
            dproj_ref[rows, c_gu:c_gu + GMLP_W] = (dgm * ms * s["du"][rows, :]).astype(BF16)
            dms = dgm * u
            dbs_ref[rows, :] += dms
            s["dmsb"][rows, :] = dms.astype(BF16)

        _mask_heads(s["dattn"], s["dom"])
        for g in range(2):
            s["lg"][g] = _dot_nt(_group_rows(s["dom"], g), s["vd"][g])
        lane = lax.broadcasted_iota(jnp.int32, (HALF, 128), 1)
        for hf in range(BLK // HALF):
            rows = slice(HALF * hf, HALF * (hf + 1))
            dsink = jnp.zeros((HALF, 128), F32)
            for h in range(N_HEADS):
                grows = slice(BLK * (h % 4) + HALF * hf, BLK * (h % 4) + HALF * (hf + 1))
                dp = s["lg"][h // 4, grows, :]
                p = s["p"][h, rows, :]
                s["pb"][h, rows, :] = p.astype(BF16)
                rs = jnp.sum(p * dp, axis=1, keepdims=True)
                dl = p * (dp - rs)
                dl_ref[h, rows, :] += dl
                dsink = dsink + jnp.where(lane == h, -s["psink"][rows, :] * rs, 0.0)
                s["dls"][h, rows, :] = (dl * (HEAD_DIM ** -0.5)).astype(BF16)
            dsink_ref[rows, :] += dsink
        for g in range(2):
            dq = _dot(_pair_rows(s["dls"], g), s["km"][g])
            dproj_ref[:, 256 * g:256 * g + 128] = dq[0:BLK].astype(BF16)
            dproj_ref[:, 256 * g + 128:256 * g + 256] = dq[BLK:2 * BLK].astype(BF16)
        lo_k = _lane_lo(2 * BLK)
        for col, lhs, rhs in ((0, "dls", "qm"), (KV_W, "pb", "dom")):
            raw = [_dot_tn(_group_rows(s[lhs], g), _group_rows(s[rhs], g)) for g in range(2)]
            both = [r + pltpu.roll(r, 64, 1) for r in raw]
            dkv = jnp.where(lo_k, both[0], both[1])
            dproj_ref[:, ATTN_W + col:ATTN_W + col + KV_W] = dkv[BLK:2 * BLK].astype(BF16)
            dkvn_ref[:, col:col + KV_W] = dkv[0:BLK]

        for j in range(4):
            cols = slice(128 * j, 128 * (j + 1))
            dm2 = _stack_halves(s["dmsb"][:, cols])
            vnb = s["vnb"][:, cols]
            dws2 = _dot_nt(dm2, vnb)
            dws_ref[2 * j] += dws2[0:BLK]
            dws_ref[2 * j + 1] += dws2[BLK:2 * BLK]
            s["dvn"][:, cols] = _dot(wst2_ref[j], dm2)
        for r0 in range(0, BLK, ROWS):
            rows = slice(r0, r0 + ROWS)
            dvn, vhat = s["dvn"][rows, :], s["vhat"][rows, :]
            dlng_ref[...] += _rowsum8(dvn * vhat)
            dlnb_ref[...] += _rowsum8(dvn)
            dvh = dvn * lng
            dact = s["rstd"][rows, :] * (dvh - _seg_mean64(dvh) - vhat * _seg_mean64(dvh * vhat))
            dproj_ref[rows, c_gu + GMLP_W:IN_W] = (dact * s["da"][rows, :]).astype(BF16)

    acc8 = lambda w: jax.ShapeDtypeStruct((8, w), F32)
    out_shape = [jax.ShapeDtypeStruct((t, IN_W), BF16), jax.ShapeDtypeStruct((t, 2 * KV_W), F32),
                 jax.ShapeDtypeStruct((N_HEADS, BLK, 2 * BLK), F32), jax.ShapeDtypeStruct((BLK, 128), F32),
                 acc8(GMLP_W), acc8(GMLP_W), jax.ShapeDtypeStruct((N_GROUPS, BLK, BLK), F32),
                 jax.ShapeDtypeStruct((BLK, GMLP_W), F32), acc8(ATTN_W), acc8(GMLP_W)]
    out_specs = [pl.BlockSpec((BLK, IN_W), lambda n: (n, 0)),
                 pl.BlockSpec((BLK, 2 * KV_W), lambda n: ((n + nb - 1) % nb, 0)),
                 _full((N_HEADS, BLK, 2 * BLK)), _full((BLK, 128)), _full((8, GMLP_W)), _full((8, GMLP_W)),
                 _full((N_GROUPS, BLK, BLK)), _full((BLK, GMLP_W)), _full((8, ATTN_W)), _full((8, GMLP_W))]
    in_specs = _mix_specs(False) + [_full((N_GROUPS // 2, BLK, 2 * BLK)),
                               pl.BlockSpec((BLK, D_MODEL), lambda n: (n, 0)),
                               _full((D_MODEL, D_MODEL))] + _kept_specs()
    return _call(
        body, name="mix_bwd", grid=(nb,), out_shape=out_shape, in_specs=in_specs, out_specs=out_specs,
        scratch_shapes=shapes, sem=("arbitrary",), comm=comm,
        args=(proj, proj, lng, lnb, ws2, bfull, aog, gog, wst2, dy, w_out, *kept))


def _outproj(mixed, w_out, x, g1, ln1g, ln1b, sc2, sh2, tm, comm):
    t, d = x.shape

    def body(mx_ref, w_ref, x_ref, g1_ref, lg_ref, lb_ref, sc_ref, sh_ref, y_ref, x1_ref, h2_ref):
        y = _dot(mx_ref[...], w_ref[...])
        xhat, _ = _ln_stats(ALPHA * x_ref[...] + g1_ref[...] * y)
        x1 = xhat * lg_ref[...] + lb_ref[...]
        y_ref[...] = y
        x1_ref[...] = x1
        h2_ref[...] = (x1 * (1.0 + sc_ref[...]) + sh_ref[...]).astype(BF16)

    row = pl.BlockSpec((tm, d), lambda i: (i, 0))
    vec = _full((1, d))
    return _call(
        body, name="outproj", grid=(t // tm,),
        out_shape=[jax.ShapeDtypeStruct((t, d), F32), jax.ShapeDtypeStruct((t, d), F32),
                   jax.ShapeDtypeStruct((t, d), BF16)],
        in_specs=[_stream(tm, d), _full((d, d)), _stream(tm, d), vec, vec, vec, vec, vec], out_specs=[row, row, row],
        sem=("parallel",), comm=comm, args=(mixed, w_out, x, g1, ln1g, ln1b, sc2, sh2))


def _ffn_fwd(h2, w_gu_t, w_down, x1, target, g2, ln2g, ln2b, tm):
    t, d = x1.shape

    def body(h_ref, w_ref, wd_ref, x1_ref, tg_ref, g2_ref, lg_ref, lb_ref,
             dsu_ref, sg_ref, act_ref, dz_ref, dy_ref, loss_ref, dlg_ref, dlb_ref, dg2_ref):
        @pl.when(pl.program_id(0) == 0)
        def _():
            for r in (loss_ref, dlg_ref, dlb_ref, dg2_ref):
                r[...] = jnp.zeros_like(r)

        h = h_ref[...]
        g = _dot_nt(h, w_ref[0:D_FF])
        u = _dot_nt(h, w_ref[D_FF:2 * D_FF])
        s = _sigmoid(g)
        sg = g * s
        act = (sg * u).astype(BF16)
        dsu_ref[...] = (u * (s * (1.0 + g * (1.0 - s)))).astype(BF16)
        sg_ref[...] = sg.astype(BF16)
        act_ref[...] = act
        y2 = _dot(act, wd_ref[...])
        g2 = g2_ref[...]
        lg = lg_ref[...]
        xhat, rstd = _ln_stats(ALPHA * x1_ref[...] + g2 * y2)
        err = xhat * lg + lb_ref[...] - tg_ref[...]
        loss_ref[...] += _rowsum8(err * err)
        dx2 = err * (1.0 / d)
        dlg_ref[...] += _rowsum8(dx2 * xhat)
        dlb_ref[...] += _rowsum8(dx2)
        dz = _ln_bwd(dx2 * lg, xhat, rstd)
        dg2_ref[...] += _rowsum8(dz * y2)
        dz_ref[...] = dz
        dy_ref[...] = (g2 * dz).astype(BF16)

    row = pl.BlockSpec((tm, d), lambda i: (i, 0))
    wide = pl.BlockSpec((tm, D_FF), lambda i: (i, 0))
    vec = _full((1, d))
    acc = _full((8, d))
    acc_shape = jax.ShapeDtypeStruct((8, d), F32)
    wide_shape = jax.ShapeDtypeStruct((t, D_FF), BF16)
    return pl.pallas_call(
        body, name="ffn_fwd", grid=(t // tm,),
        out_shape=[wide_shape] * 3 + [jax.ShapeDtypeStruct((t, d), F32), jax.ShapeDtypeStruct((t, d), BF16)]
        + [acc_shape] * 4,
        in_specs=[_stream(tm, d), _resident((2 * D_FF, d)), _resident((D_FF, d)), _stream(tm, d), _stream(tm, d),
                  vec, vec, vec],
        out_specs=[wide] * 3 + [row, row, acc, acc, acc, acc], compiler_params=_params(("arbitrary",)),
    )(h2, w_gu_t, w_down, x1, target, g2, ln2g, ln2b)


def _resident(shape):
    nd = len(shape)
    return pl.BlockSpec(shape, lambda *_: (0,) * nd, pipeline_mode=pl.Buffered(1))


def _ffn_bwd(dy2, w_down, dsu, sg, w_gu_t, x1, x, y, dz2, sc2, g1, ln1g, tm, comm):
    t, d = x1.shape

    def body(dy2_ref, wd_ref, dsu_ref, sg_ref, w_ref, x1_ref, x_ref, y_ref, dz2_ref, sc_ref, g1_ref, lg_ref,
             dg_ref, du_ref, dz1_ref, dy_ref, dsc_ref, dsh_ref, dlg_ref, dlb_ref, dg1_ref):
        @pl.when(pl.program_id(0) == 0)
        def _():
            for r in (dsc_ref, dsh_ref, dlg_ref, dlb_ref, dg1_ref):
                r[...] = jnp.zeros_like(r)

        dact = _dot_nt(dy2_ref[...], wd_ref[...])
        dg = (dact * dsu_ref[...].astype(F32)).astype(BF16)
        du = (dact * sg_ref[...].astype(F32)).astype(BF16)
        dg_ref[...] = dg
        du_ref[...] = du
        dh2 = _dot(dg, w_ref[0:D_FF]) + _dot(du, w_ref[D_FF:2 * D_FF])
        x1 = x1_ref[...]
        y = y_ref[...]
        g1 = g1_ref[...]
        dsc_ref[...] += _rowsum8(dh2 * x1)
        dsh_ref[...] += _rowsum8(dh2)
        dx1 = dh2 * (1.0 + sc_ref[...]) + ALPHA * dz2_ref[...]
        xhat, rstd = _ln_stats(ALPHA * x_ref[...] + g1 * y)
        dlg_ref[...] += _rowsum8(dx1 * xhat)
        dlb_ref[...] += _rowsum8(dx1)
        dz1 = _ln_bwd(dx1 * lg_ref[...], xhat, rstd)
        dg1_ref[...] += _rowsum8(dz1 * y)
        dz1_ref[...] = dz1
        dy_ref[...] = (g1 * dz1).astype(BF16)

    row = pl.BlockSpec((tm, d), lambda i: (i, 0))
    wide = pl.BlockSpec((tm, D_FF), lambda i: (i, 0))
    vec = _full((1, d))
    acc = _full((8, d))
    acc_shape = jax.ShapeDtypeStruct((8, d), F32)
    wide_shape = jax.ShapeDtypeStruct((t, D_FF), BF16)
    return _call(
        body, name="ffn_bwd", grid=(t // tm,),
        out_shape=[wide_shape, wide_shape, jax.ShapeDtypeStruct((t, d), F32), jax.ShapeDtypeStruct((t, d), BF16)]
        + [acc_shape] * 5,
        in_specs=[_stream(tm, d), _resident((D_FF, d)), _stream(tm, D_FF), _stream(tm, D_FF), _resident((2 * D_FF, d)),
                  _stream(tm, d), _stream(tm, d), _stream(tm, d), _stream(tm, d), vec, vec, vec],
        out_specs=[wide, wide, row, row, acc, acc, acc, acc, acc], sem=("arbitrary",), comm=comm,
        args=(dy2, w_down, dsu, sg, w_gu_t, x1, x, y, dz2, sc2, g1, ln1g))


def _din(dproj, dkvn, w_in_t, x, dz1, sc1, tm, comm):
    t, d = x.shape

    def body(dp_ref, dkv_ref, w_ref, x_ref, dz1_ref, sc_ref, dx_ref, dpb_ref, dbin_ref, dsc_ref, dsh_ref):
        @pl.when(pl.program_id(0) == 0)
        def _():
            for r in (dbin_ref, dsc_ref, dsh_ref):
                r[...] = jnp.zeros_like(r)

        dp = dp_ref[...].astype(F32)
        dp = jnp.concatenate([dp[:, 0:ATTN_W], dp[:, ATTN_W:ATTN_W + 2 * KV_W] + dkv_ref[...],
                              dp[:, ATTN_W + 2 * KV_W:IN_W]], axis=1)
        dbin_ref[...] += _rowsum8(dp)
        dpb = dp.astype(BF16)
        dpb_ref[...] = dpb
        dh = _dot(dpb, w_ref[...])
        dsc_ref[...] += _rowsum8(dh * x_ref[...])
        dsh_ref[...] += _rowsum8(dh)
        dx_ref[...] = dh * (1.0 + sc_ref[...]) + ALPHA * dz1_ref[...]

    row = lambda w: pl.BlockSpec((tm, w), lambda i: (i, 0))
    return _call(
        body, name="din", grid=(t // tm,),
        out_shape=[jax.ShapeDtypeStruct((t, d), F32), jax.ShapeDtypeStruct((t, IN_W), BF16),
                   jax.ShapeDtypeStruct((8, IN_W), F32), jax.ShapeDtypeStruct((8, d), F32),
                   jax.ShapeDtypeStruct((8, d), F32)],
        in_specs=[row(IN_W), row(2 * KV_W), _full((IN_W, d)), row(d), row(d), _full((1, d))],
        out_specs=[row(d), row(IN_W), _full((8, IN_W)), _full((8, d)), _full((8, d))],
        sem=("arbitrary",), comm=comm, args=(dproj, dkvn, w_in_t, x, dz1, sc1))


def _wgrad(name, a, b, tmm, tk, comm=None, a2=None):
    t, m = a.shape
    n = b.shape[1]
    nk = t // tk
    nm = m // tmm

    def body(*refs):
        a_refs, (b_ref, o_ref, acc_ref) = refs[:-3], refs[-3:]
        i, k = pl.program_id(0), pl.program_id(1)

        @pl.when(k == 0)
        def _():
            acc_ref[...] = jnp.zeros_like(acc_ref)

        a_tile = a_refs[0][...] if a2 is None else jnp.where(i < nm, a_refs[0][...], a_refs[1][...])
        acc_ref[...] += _dot_tn(a_tile, b_ref[...])

        @pl.when(k == nk - 1)
        def _():
            o_ref[...] = acc_ref[...].astype(BF16)

    if a2 is None:
        a_specs, a_args, n_tiles = [pl.BlockSpec((tk, tmm), lambda i, k: (k, i))], (a,), nm
    else:
        a_specs = [pl.BlockSpec((tk, tmm), lambda i, k: (jnp.where(i < nm, k, 0), jnp.minimum(i, nm - 1))),
                   pl.BlockSpec((tk, tmm), lambda i, k: (jnp.where(i < nm, 0, k), jnp.maximum(i - nm, 0)))]
        a_args, n_tiles = (a, a2), 2 * nm
    (out,), got = _call(
        body, name=name, grid=(n_tiles, nk), out_shape=[jax.ShapeDtypeStruct((n_tiles * tmm, n), BF16)],
        in_specs=a_specs + [pl.BlockSpec((tk, n), lambda i, k: (k, 0))],
        out_specs=[pl.BlockSpec((tmm, n), lambda i, k: (i, 0))],
        scratch_shapes=[pltpu.VMEM((tmm, n), F32)], sem=("parallel", "arbitrary"), comm=comm, args=a_args + (b,))
    return out if comm is None else (out, got)


def _adamw(w, g, m, v):
    m = ADAM_B1 * m + (1.0 - ADAM_B1) * g
    v = ADAM_B2 * v + (1.0 - ADAM_B2) * (g * g)
    m_hat = m / (1.0 - ADAM_B1 ** ADAM_STEP)
    v_hat = v / (1.0 - ADAM_B2 ** ADAM_STEP)
    delta = -ADAM_LR * (m_hat / (jnp.sqrt(v_hat) + ADAM_EPS) + ADAM_WD * w)
    return delta, m, v


def _adam_reduce(name, parts, w, m, v, tr):
    r, cdim = w.shape
    n_slots = parts.shape[0]

    def body(p_ref, w_ref, m_ref, v_ref, g_ref, d_ref, mo_ref, vo_ref):
        g = p_ref[0].astype(F32)
        for s in range(1, n_slots):
            g = g + p_ref[s].astype(F32)
        d_ref[...], mo_ref[...], vo_ref[...] = _adamw(w_ref[...], g, m_ref[...], v_ref[...])
        g_ref[...] = g

    tile = pl.BlockSpec((tr, cdim), lambda i: (i, 0))
    shp = jax.ShapeDtypeStruct((r, cdim), F32)
    return pl.pallas_call(
        body, name=name, grid=(r // tr,), out_shape=[shp] * 4,
        in_specs=[pl.BlockSpec((n_slots, tr, cdim), lambda i: (0, i, 0)), tile, tile, tile],
        out_specs=[tile] * 4, compiler_params=_params(("parallel",)),
    )(parts, w, m, v)


def _adam_w_ada(c_all_t, dmod_cols, w, m, v):
    def body(ct_ref, dm_ref, w_ref, m_ref, v_ref, g_ref, d_ref, mo_ref, vo_ref):
        ct = ct_ref[...]
        s = (ct * _sigmoid(ct)).astype(BF16)
        g = _dot(s, dm_ref[...].astype(BF16))
        d_ref[...], mo_ref[...], vo_ref[...] = _adamw(w_ref[...], g, m_ref[...], v_ref[...])
        g_ref[...] = g

    shp = jax.ShapeDtypeStruct(w.shape, F32)
    return pl.pallas_call(
        body, name="adam_w_ada", grid=(1,), out_shape=[shp] * 4,
        in_specs=[_full(c_all_t.shape), _full(dmod_cols.shape)] + [_full(w.shape)] * 3,
        out_specs=[_full(w.shape)] * 4, compiler_params=_params(("arbitrary",)),
    )(c_all_t, dmod_cols, w, m, v)


SMALL_EARLY = ["rel_bias", "attn_sinks", "gmlp_ln_g", "gmlp_ln_b", "gmlp_b_s",
               "attn_out_g", "gmlp_out_g", "ln1_g", "ln1_b", "ln2_g", "ln2_b"]
SMALL_LATE = ["b_ada", "b_in", "loss"]
WEIGHTS = ["rel_bias", "w_ada", "b_ada", "w_in", "b_in", "attn_sinks", "gmlp_ln_g", "gmlp_ln_b", "gmlp_w_s",
           "gmlp_b_s", "attn_out_g", "gmlp_out_g", "w_out", "ln1_g", "ln1_b", "w_gate_up", "w_down", "ln2_g", "ln2_b"]


def _seg_rows(nelem):
    return -(-nelem // 1024) * 8


def _pack(named, names):
    parts = []
    for name in names:
        flat = named[name].reshape(-1).astype(F32)
        rows = _seg_rows(flat.shape[0])
        parts.append(jnp.pad(flat, (0, rows * 128 - flat.shape[0])).reshape(rows, 128))
    return jnp.concatenate(parts, axis=0)


def _adam_small(name, parts, names, wts, mom_m, mom_v):
    params = [n for n in names if n in wts]

    def view(n):
        nelem = math.prod(wts[n].shape)
        return (nelem // 128, 128) if nelem % 128 == 0 else (1, nelem)

    offsets, r0 = {}, 0
    for n in names:
        offsets[n] = r0
        r0 += _seg_rows(math.prod(wts[n].shape) if n in wts else 1)

    def body(*refs):
        p_ref, ins, outs = refs[0], refs[1:1 + 3 * len(params)], refs[1 + 3 * len(params):]

        def total(n, rows, lanes):
            o = offsets[n]
            g = p_ref[0, o:o + rows, 0:lanes]
            for s in range(1, N_DEV):
                g = g + p_ref[s, o:o + rows, 0:lanes]
            return g

        for i, n in enumerate(params):
            g = total(n, *view(n))
            w_ref, m_ref, v_ref = ins[3 * i:3 * i + 3]
            g_ref, d_ref, mo_ref, vo_ref = outs[4 * i:4 * i + 4]
            d_ref[...], mo_ref[...], vo_ref[...] = _adamw(w_ref[...], g, m_ref[...], v_ref[...])
            g_ref[...] = g
        for j, n in enumerate(n for n in names if n not in wts):
            outs[4 * len(params) + j][...] = total(n, 8, 128)

    args, in_specs, out_shape = [parts], [_full(parts.shape)], []
    for n in params:
        args += [t[n].reshape(view(n)) for t in (wts, mom_m, mom_v)]
        in_specs += [_full(view(n))] * 3
        out_shape += [jax.ShapeDtypeStruct(view(n), F32)] * 4
    out_shape += [jax.ShapeDtypeStruct((8, 128), F32) for n in names if n not in wts]
    res = pl.pallas_call(
        body, name=name, grid=(1,), out_shape=out_shape, in_specs=in_specs,
        out_specs=[_full(s.shape) for s in out_shape], compiler_params=_params(("arbitrary",)),
    )(*args)
    done = {n: tuple(r.reshape(wts[n].shape) for r in res[4 * i:4 * i + 4]) for i, n in enumerate(params)}
    sums = {n: res[4 * len(params) + j] for j, n in enumerate(n for n in names if n not in wts)}
    return done, sums


def _t5_bucket_map():
    qi = jnp.arange(BLK)[:, None]
    si = jnp.arange(2 * BLK)[None, :]
    n = jnp.maximum(qi + BLK - si, 0)
    max_exact = N_BUCKETS // 2
    nf = jnp.maximum(n, max_exact).astype(F32)
    large = max_exact + (jnp.log(nf / max_exact) / math.log(MAX_DISTANCE / max_exact)
                         * (N_BUCKETS - max_exact)).astype(jnp.int32)
    large = jnp.minimum(large, N_BUCKETS - 1)
    return jnp.where(n < max_exact, n, large).astype(jnp.int32)


def kernel(x, c, rel_bias, w_ada, b_ada, w_in, b_in, attn_sinks, gmlp_ln_g, gmlp_ln_b, gmlp_w_s, gmlp_b_s, attn_out_g, gmlp_out_g, w_out, ln1_g, ln1_b, w_gate_up, w_down, ln2_g, ln2_b, loss_target, m_rel_bias, m_w_ada, m_b_ada, m_w_in, m_b_in, m_attn_sinks, m_gmlp_ln_g, m_gmlp_ln_b, m_gmlp_w_s, m_gmlp_b_s, m_attn_out_g, m_gmlp_out_g, m_w_out, m_ln1_g, m_ln1_b, m_w_gate_up, m_w_down, m_ln2_g, m_ln2_b, v_rel_bias, v_w_ada, v_b_ada, v_w_in, v_b_in, v_attn_sinks, v_gmlp_ln_g, v_gmlp_ln_b, v_gmlp_w_s, v_gmlp_b_s, v_attn_out_g, v_gmlp_out_g, v_w_out, v_ln1_g, v_ln1_b, v_w_gate_up, v_w_down, v_ln2_g, v_ln2_b):
    wts = dict(rel_bias=rel_bias, w_ada=w_ada, b_ada=b_ada, w_in=w_in, b_in=b_in, attn_sinks=attn_sinks,
               gmlp_ln_g=gmlp_ln_g, gmlp_ln_b=gmlp_ln_b, gmlp_w_s=gmlp_w_s, gmlp_b_s=gmlp_b_s,
               attn_out_g=attn_out_g, gmlp_out_g=gmlp_out_g, w_out=w_out, ln1_g=ln1_g, ln1_b=ln1_b,
               w_gate_up=w_gate_up, w_down=w_down, ln2_g=ln2_g, ln2_b=ln2_b)
    mom_m = dict(rel_bias=m_rel_bias, w_ada=m_w_ada, b_ada=m_b_ada, w_in=m_w_in, b_in=m_b_in,
                 attn_sinks=m_attn_sinks, gmlp_ln_g=m_gmlp_ln_g, gmlp_ln_b=m_gmlp_ln_b, gmlp_w_s=m_gmlp_w_s,
                 gmlp_b_s=m_gmlp_b_s, attn_out_g=m_attn_out_g, gmlp_out_g=m_gmlp_out_g, w_out=m_w_out,
                 ln1_g=m_ln1_g, ln1_b=m_ln1_b, w_gate_up=m_w_gate_up, w_down=m_w_down, ln2_g=m_ln2_g,
                 ln2_b=m_ln2_b)
    mom_v = dict(rel_bias=v_rel_bias, w_ada=v_w_ada, b_ada=v_b_ada, w_in=v_w_in, b_in=v_b_in,
                 attn_sinks=v_attn_sinks, gmlp_ln_g=v_gmlp_ln_g, gmlp_ln_b=v_gmlp_ln_b, gmlp_w_s=v_gmlp_w_s,
                 gmlp_b_s=v_gmlp_b_s, attn_out_g=v_attn_out_g, gmlp_out_g=v_gmlp_out_g, w_out=v_w_out,
                 ln1_g=v_ln1_g, ln1_b=v_ln1_b, w_gate_up=v_w_gate_up, w_down=v_w_down, ln2_g=v_ln2_g,
                 ln2_b=v_ln2_b)

    t = x.shape[1]
    tm = min(512, t)
    tn_ff = D_FF // 2
    tk_long, tk_short = min(4096, t), min(2048, t)
    me = 4 * lax.axis_index("x") + 2 * lax.axis_index("y") + lax.axis_index("c")
    xs = x[0]
    target = loss_target[0]

    (c_g,) = _exchange("gather_c", [jnp.broadcast_to(c, (8, D_MODEL))], ("gather",))
    c_all = c_g[:, 0, :]

    ncol = w_ada.shape[2]
    b_cols = lax.dynamic_slice(b_ada, (0, me * ncol), (1, ncol))
    mod_part = _mod_partial(c_all, w_ada[0], b_cols)
    bucket = _t5_bucket_map()
    (bias,), (mod_g, w_in_g) = _bias_table(rel_bias, bucket,
                                           comm=([mod_part, w_in[0].T.astype(BF16)], ("gather", "gather2")))
    w_in_t = w_in_g.reshape(IN_W, D_MODEL)
    mod = lax.dynamic_slice(mod_g, (0, me, 0), (N_DEV, 1, ncol)).reshape(1, N_DEV * ncol)
    sh1, sc1, g1, sh2, sc2, g2 = [mod[:, i * D_MODEL:(i + 1) * D_MODEL] for i in range(6)]

    causal = jnp.tril(jnp.ones((BLK, BLK), dtype=bool))
    ws = jnp.where(causal[None], gmlp_w_s[0], 0.0).astype(BF16)
    pair = lambda w: jnp.concatenate([w[0::2], w[1::2]], axis=2)
    ws2, wst2 = pair(ws), pair(jnp.swapaxes(ws, 1, 2))
    bfull = jnp.repeat(gmlp_b_s[0].T, GMLP_W // N_GROUPS, axis=1)
    sinks = attn_sinks[0]

    (proj, h1), (w_down_g,) = _inproj(xs, sc1, sh1, w_in_t, b_in, tm, comm=([w_down[0].astype(BF16)], ("gather2",)))
    (mixed, *kept), (w_out_g, w_gu_g) = _mix_fwd(
        proj, bias, sinks, gmlp_ln_g, gmlp_ln_b, ws2, bfull, attn_out_g, gmlp_out_g,
        comm=([w_out[0].astype(BF16), w_gate_up[0].T.astype(BF16)], ("gather2", "gather2")))
    w_out_f = w_out_g.reshape(D_MODEL, D_MODEL)
    w_gu_t = w_gu_g.reshape(2 * D_FF, D_MODEL)
    (y1, x1, h2), _ = _outproj(mixed, w_out_f, xs, g1, ln1_g, ln1_b, sc2, sh2, tm, comm=None)
    w_down_f = w_down_g.reshape(D_FF, D_MODEL)
    dsu, sg, act, dz2, dy2, loss_p, d_ln2g, d_ln2b, d_g2 = _ffn_fwd(h2, w_gu_t, w_down_f, x1, target, g2, ln2_g, ln2_b,
                                                                    min(256, t))

    slots = lambda a: a.reshape(N_DEV, -1, D_MODEL)
    dw_down = _wgrad("wgrad_down", act, dy2, tn_ff, tk_short)
    (dgate, dup, dz1, dy1, d_sc2, d_sh2, d_ln1g, d_ln1b, d_g1), (r_down,) = _ffn_bwd(
        dy2, w_down_f, dsu, sg, w_gu_t, x1, xs, y1, dz2, sc2, g1, ln1_g, min(256, t),
        comm=([slots(dw_down)], ("scatter",)))
    dw_gu_t = _wgrad("wgrad_gate_up", dgate, h2, tn_ff, tk_short, a2=dup)
    dw_out = _wgrad("wgrad_out", mixed, dy1, D_MODEL, tk_long)
    ((dproj, dkvn, dl_acc, dsink_acc, d_lng, d_lnb, d_ws, d_bs, d_aog, d_gog), (r_gu, r_out)) = _mix_bwd(
        proj, gmlp_ln_g, gmlp_ln_b, ws2, wst2, bfull, attn_out_g, gmlp_out_g, dy1, w_out_f.T, kept,
        comm=([slots(dw_gu_t), slots(dw_out)], ("scatter", "scatter")))
    d_relb = _bias_grad(dl_acc, bucket)

    rsum = lambda a: jnp.sum(a, axis=0)
    early_g = dict(
        rel_bias=d_relb[:, 0, :N_BUCKETS].T, attn_sinks=rsum(dsink_acc)[:N_HEADS],
        gmlp_ln_g=rsum(d_lng), gmlp_ln_b=rsum(d_lnb),
        gmlp_b_s=jnp.sum(d_bs.reshape(BLK, N_GROUPS, GMLP_W // N_GROUPS), axis=2).T,
        attn_out_g=rsum(d_aog), gmlp_out_g=rsum(d_gog), ln1_g=rsum(d_ln1g), ln1_b=rsum(d_ln1b),
        ln2_g=rsum(d_ln2g), ln2_b=rsum(d_ln2b))
    (grad_x, dproj_b, d_bin, d_sc1, d_sh1), _ = _din(dproj, dkvn, w_in_t, xs, dz1, sc1, tm, comm=None)
    ws_rows = lambda a: a.reshape(N_GROUPS * BLK, BLK)
    d_ws_b = ws_rows(jnp.where(causal[None], d_ws, 0.0)).astype(BF16)
    dw_in_t, (early_all, ws_all) = _wgrad("wgrad_in", dproj_b, h1, IN_W // 2, tk_long,
                                          comm=([_pack(early_g, SMALL_EARLY), d_ws_b], ("gather2", "gather2")))
    dmod = jnp.concatenate([rsum(d_sh1), rsum(d_sc1), rsum(d_g1), rsum(d_sh2), rsum(d_sc2), rsum(d_g2)])
    late_g = dict(b_ada=dmod, b_in=rsum(d_bin), loss=(0.5 / D_MODEL * jnp.sum(loss_p)).reshape(1))
    late_all, r_in = _scatter_two_level("scatter_in", _pack(late_g, SMALL_LATE), slots(dw_in_t))

    small, _ = _adam_small("adam_small_early", early_all, SMALL_EARLY, wts, mom_m, mom_v)
    small_late, sums = _adam_small("adam_small_late", late_all, SMALL_LATE, wts, mom_m, mom_v)
    small.update(small_late)
    small["gmlp_w_s"] = [o.reshape(gmlp_w_s.shape) for o in _adam_reduce(
        "adam_w_s", ws_all, ws_rows(gmlp_w_s), ws_rows(m_gmlp_w_s), ws_rows(v_gmlp_w_s), N_GROUPS * BLK // 2)]
    loss = sums["loss"][0, 0]

    dmod_all = late_all[:, :_seg_rows(6 * D_MODEL), :].reshape(N_DEV, 6 * D_MODEL)
    dmod_cols = lax.dynamic_slice(dmod_all, (0, me * ncol), (N_DEV, ncol))
    kpad = 128 - N_DEV
    ada = _adam_w_ada(jnp.pad(c_all.T, ((0, 0), (0, kpad))), jnp.pad(dmod_cols, ((0, kpad), (0, 0))),
                      w_ada[0], m_w_ada[0], v_w_ada[0])

    tr = lambda a: jnp.swapaxes(a, -1, -2)
    big = {}
    big["w_in"] = [tr(o)[None] for o in _adam_reduce("adam_w_in", r_in, w_in[0].T, m_w_in[0].T, v_w_in[0].T, 112)]
    big["w_out"] = [o[None] for o in _adam_reduce("adam_w_out", r_out, w_out[0], m_w_out[0], v_w_out[0], 128)]
    big["w_gate_up"] = [tr(o)[None] for o in _adam_reduce("adam_w_gu", r_gu, w_gate_up[0].T, m_w_gate_up[0].T,
                                                           v_w_gate_up[0].T, 352)]
    big["w_down"] = [o[None] for o in _adam_reduce("adam_w_down", r_down, w_down[0], m_w_down[0], v_w_down[0], 176)]
    big["w_ada"] = [o[None] for o in ada]

    outs = [[], [], [], []]
    for name in WEIGHTS:
        for i in range(4):
            outs[i].append(big[name][i] if name in big else small[name][i])
    return (loss, grad_x[None], *outs[0], *outs[1], *outs[2], *outs[3])
```
